```python
import jax, jax.numpy as jnp
from jax import lax
import numpy as np

D_MODEL = 1024
BATCH = 8
SEQ = 8192
DEPTH = 1

CONV_WIDTH = 512
CONV_GROUPS = 8
CONV_KERNEL = 3
N_Q_HEADS = 8
N_KV_HEADS = 2
HEAD_DIM = 64
ATTN_WIDTH = N_Q_HEADS * HEAD_DIM
KV_WIDTH = N_KV_HEADS * HEAD_DIM
WINDOW = 128
BLOCK = 128
ROPE_THETA = 500000.0
ROT_DIM = HEAD_DIM // 4
MIX_WIDTH = CONV_WIDTH + ATTN_WIDTH
IN_PROJ_WIDTH = 3 * CONV_WIDTH + ATTN_WIDTH + 2 * KV_WIDTH
D_FF = 2816
FFN_RES_SCALE = 0.5
RMS_EPS = 1e-5
MASK_VALUE = -1e30

kernel_name = "hybrid_shortconv_swa_sink_macaron"


def rms_norm(x, gain):
    xf = x.astype(jnp.float32)
    inv = lax.rsqrt(jnp.mean(xf * xf, axis=-1, keepdims=True) + RMS_EPS)
    return (xf * inv).astype(x.dtype) * gain


def swiglu(h, w_gate, w_up, w_down):
    return (jax.nn.silu(h @ w_gate) * (h @ w_up)) @ w_down


def partial_rotary(t, seq_len):
    half = ROT_DIM // 2
    inv_freq = ROPE_THETA ** (-jnp.arange(0, ROT_DIM, 2, dtype=jnp.float32) / ROT_DIM)
    ang = jnp.arange(seq_len, dtype=jnp.float32)[:, None] * inv_freq[None, :]
    cos = jnp.cos(ang)[None, :, None, :].astype(t.dtype)
    sin = jnp.sin(ang)[None, :, None, :].astype(t.dtype)
    t1, t2, t_pass = t[..., :half], t[..., half:ROT_DIM], t[..., ROT_DIM:]
    return jnp.concatenate([t1 * cos - t2 * sin, t2 * cos + t1 * sin, t_pass], axis=-1)


def short_conv_mixer(b_gate, c_gate, u, conv_w):
    v = c_gate * u
    y = lax.conv_general_dilated(
        v, conv_w[:, None, :], window_strides=(1,), padding=[(CONV_KERNEL - 1, 0)],
        dimension_numbers=('NWC', 'WIO', 'NWC'), feature_group_count=CONV_WIDTH)
    return b_gate * y


def sliding_window_sink_attention(q, k, v, sinks):
    b, s = q.shape[0], q.shape[1]
    nb = s // BLOCK
    g = N_Q_HEADS // N_KV_HEADS
    qb = q.reshape(b, nb, BLOCK, N_KV_HEADS, g, HEAD_DIM)

    def band(t):
        tp = jnp.pad(t, ((0, 0), (BLOCK, 0), (0, 0), (0, 0))).reshape(b, nb + 1, BLOCK, N_KV_HEADS, HEAD_DIM)
        return jnp.concatenate([tp[:, :-1], tp[:, 1:]], axis=2)

    kb, vb = band(k), band(v)
    scores = jnp.einsum('bnqhgd,bnkhd->bnhgqk', qb, kb).astype(jnp.float32) * (HEAD_DIM ** -0.5)

    qi = jnp.arange(BLOCK)[:, None]
    kj = jnp.arange(2 * BLOCK)[None, :]
    rel = kj - BLOCK - qi
    in_window = (rel <= 0) & (rel > -WINDOW)
    blk = jnp.arange(nb)[:, None, None]
    k_exists = (blk * BLOCK + kj[None] - BLOCK) >= 0
    mask = (in_window[None] & k_exists)[None, :, None, None]
    scores = jnp.where(mask, scores, MASK_VALUE)

    sink = sinks.astype(jnp.float32).reshape(N_KV_HEADS, g)[None, None, :, :, None, None]
    m = jnp.maximum(jnp.max(scores, axis=-1, keepdims=True), sink)
    p = jnp.exp(scores - m)
    probs = p / (jnp.sum(p, axis=-1, keepdims=True) + jnp.exp(sink - m))
    out = jnp.einsum('bnhgqk,bnkhd->bnqhgd', probs.astype(vb.dtype), vb)
    return out.reshape(b, s, ATTN_WIDTH)


def _fwd_setup_inputs(seed: int = 0) -> dict:
    key = jax.random.key(seed)
    ks = jax.random.split(key, 16)
    f32 = jnp.float32

    def w(k, shape, fan_in):
        return jax.random.normal(k, shape, f32) * (fan_in ** -0.5)

    def gain(k):
        return 1.0 + 0.02 * jax.random.normal(k, (DEPTH, D_MODEL), f32)

    return {
        "x": jax.random.normal(ks[0], (BATCH, SEQ, D_MODEL), f32),
        "ffn1_norm": gain(ks[1]),
        "ffn1_w_gate": w(ks[2], (DEPTH, D_MODEL, D_FF), D_MODEL),
        "ffn1_w_up": w(ks[3], (DEPTH, D_MODEL, D_FF), D_MODEL),
        "ffn1_w_down": w(ks[4], (DEPTH, D_FF, D_MODEL), D_FF),
        "mix_norm": gain(ks[5]),
        "w_in": w(ks[6], (DEPTH, D_MODEL, IN_PROJ_WIDTH), D_MODEL),
        "conv_w": w(ks[7], (DEPTH, CONV_KERNEL, CONV_WIDTH), CONV_KERNEL),
        "attn_sinks": 0.5 * jax.random.normal(ks[8], (DEPTH, N_Q_HEADS), f32),
        "w_out": w(ks[9], (DEPTH, MIX_WIDTH, D_MODEL), MIX_WIDTH),
        "ffn2_norm": gain(ks[10]),
        "ffn2_w_gate": w(ks[11], (DEPTH, D_MODEL, D_FF), D_MODEL),
        "ffn2_w_up": w(ks[12], (DEPTH, D_MODEL, D_FF), D_MODEL),
        "ffn2_w_down": w(ks[13], (DEPTH, D_FF, D_MODEL), D_FF),
        "final_norm": 1.0 + 0.02 * jax.random.normal(ks[14], (D_MODEL,), f32),
    }


def _fwd_reference(x, ffn1_norm, ffn1_w_gate, ffn1_w_up, ffn1_w_down, mix_norm, w_in, conv_w,
              attn_sinks, w_out, ffn2_norm, ffn2_w_gate, ffn2_w_up, ffn2_w_down, final_norm):
    b, s, _ = x.shape
    splits = np.cumsum([CONV_WIDTH, CONV_WIDTH, CONV_WIDTH, ATTN_WIDTH, KV_WIDTH]).tolist()
    for l in range(DEPTH):
        x = x + FFN_RES_SCALE * swiglu(rms_norm(x, ffn1_norm[l]), ffn1_w_gate[l], ffn1_w_up[l], ffn1_w_down[l])

        h = rms_norm(x, mix_norm[l])
        z = h @ w_in[l]
        b_gate, c_gate, u, q, k, v = jnp.split(z, splits, axis=-1)

        y_conv = short_conv_mixer(b_gate, c_gate, u, conv_w[l])

        q = partial_rotary(q.reshape(b, s, N_Q_HEADS, HEAD_DIM), s)
        k = partial_rotary(k.reshape(b, s, N_KV_HEADS, HEAD_DIM), s)
        v = v.reshape(b, s, N_KV_HEADS, HEAD_DIM)
        y_attn = sliding_window_sink_attention(q, k, v, attn_sinks[l])

        x = x + jnp.concatenate([y_conv, y_attn], axis=-1) @ w_out[l]

        x = x + FFN_RES_SCALE * swiglu(rms_norm(x, ffn2_norm[l]), ffn2_w_gate[l], ffn2_w_up[l], ffn2_w_down[l])
    return rms_norm(x, final_norm)


import jax as _jax
import jax.numpy as _jnp

TWIN_FORMAT = 'train_step'
FWD_PARAMS = ['x', 'ffn1_norm', 'ffn1_w_gate', 'ffn1_w_up', 'ffn1_w_down', 'mix_norm', 'w_in', 'conv_w', 'attn_sinks', 'w_out', 'ffn2_norm', 'ffn2_w_gate', 'ffn2_w_up', 'ffn2_w_down', 'final_norm']
TWIN_WEIGHTS = ['ffn1_norm', 'ffn1_w_gate', 'ffn1_w_up', 'ffn1_w_down', 'mix_norm', 'w_in', 'conv_w', 'attn_sinks', 'w_out', 'ffn2_norm', 'ffn2_w_gate', 'ffn2_w_up', 'ffn2_w_down', 'final_norm']
TWIN_DIFF_INPUT = 'x'
TWIN_INPUTS = ['x', 'ffn1_norm', 'ffn1_w_gate', 'ffn1_w_up', 'ffn1_w_down', 'mix_norm', 'w_in', 'conv_w', 'attn_sinks', 'w_out', 'ffn2_norm', 'ffn2_w_gate', 'ffn2_w_up', 'ffn2_w_down', 'final_norm', 'loss_target', 'm_ffn1_norm', 'm_ffn1_w_gate', 'm_ffn1_w_up', 'm_ffn1_w_down', 'm_mix_norm', 'm_w_in', 'm_conv_w', 'm_attn_sinks', 'm_w_out', 'm_ffn2_norm', 'm_ffn2_w_gate', 'm_ffn2_w_up', 'm_ffn2_w_down', 'm_final_norm', 'v_ffn1_norm', 'v_ffn1_w_gate', 'v_ffn1_w_up', 'v_ffn1_w_down', 'v_mix_norm', 'v_w_in', 'v_conv_w', 'v_attn_sinks', 'v_w_out', 'v_ffn2_norm', 'v_ffn2_w_gate', 'v_ffn2_w_up', 'v_ffn2_w_down', 'v_final_norm']
TWIN_OUTPUTS = ['loss', 'grad_x', 'grad_ffn1_norm', 'grad_ffn1_w_gate', 'grad_ffn1_w_up', 'grad_ffn1_w_down', 'grad_mix_norm', 'grad_w_in', 'grad_conv_w', 'grad_attn_sinks', 'grad_w_out', 'grad_ffn2_norm', 'grad_ffn2_w_gate', 'grad_ffn2_w_up', 'grad_ffn2_w_down', 'grad_final_norm', 'delta_ffn1_norm', 'delta_ffn1_w_gate', 'delta_ffn1_w_up', 'delta_ffn1_w_down', 'delta_mix_norm', 'delta_w_in', 'delta_conv_w', 'delta_attn_sinks', 'delta_w_out', 'delta_ffn2_norm', 'delta_ffn2_w_gate', 'delta_ffn2_w_up', 'delta_ffn2_w_down', 'delta_final_norm', 'new_m_ffn1_norm', 'new_m_ffn1_w_gate', 'new_m_ffn1_w_up', 'new_m_ffn1_w_down', 'new_m_mix_norm', 'new_m_w_in', 'new_m_conv_w', 'new_m_attn_sinks', 'new_m_w_out', 'new_m_ffn2_norm', 'new_m_ffn2_w_gate', 'new_m_ffn2_w_up', 'new_m_ffn2_w_down', 'new_m_final_norm', 'new_v_ffn1_norm', 'new_v_ffn1_w_gate', 'new_v_ffn1_w_up', 'new_v_ffn1_w_down', 'new_v_mix_norm', 'new_v_w_in', 'new_v_conv_w', 'new_v_attn_sinks', 'new_v_w_out', 'new_v_ffn2_norm', 'new_v_ffn2_w_gate', 'new_v_ffn2_w_up', 'new_v_ffn2_w_down', 'new_v_final_norm']
TWIN_LEAF_KINDS = {'loss': 'loss', 'grad_x': 'grad_x', 'grad_ffn1_norm': 'grad_w', 'grad_ffn1_w_gate': 'grad_w', 'grad_ffn1_w_up': 'grad_w', 'grad_ffn1_w_down': 'grad_w', 'grad_mix_norm': 'grad_w', 'grad_w_in': 'grad_w', 'grad_conv_w': 'grad_w', 'grad_attn_sinks': 'grad_w', 'grad_w_out': 'grad_w', 'grad_ffn2_norm': 'grad_w', 'grad_ffn2_w_gate': 'grad_w', 'grad_ffn2_w_up': 'grad_w', 'grad_ffn2_w_down': 'grad_w', 'grad_final_norm': 'grad_w', 'delta_ffn1_norm': 'delta_w', 'delta_ffn1_w_gate': 'delta_w', 'delta_ffn1_w_up': 'delta_w', 'delta_ffn1_w_down': 'delta_w', 'delta_mix_norm': 'delta_w', 'delta_w_in': 'delta_w', 'delta_conv_w': 'delta_w', 'delta_attn_sinks': 'delta_w', 'delta_w_out': 'delta_w', 'delta_ffn2_norm': 'delta_w', 'delta_ffn2_w_gate': 'delta_w', 'delta_ffn2_w_up': 'delta_w', 'delta_ffn2_w_down': 'delta_w', 'delta_final_norm': 'delta_w', 'new_m_ffn1_norm': 'new_m', 'new_m_ffn1_w_gate': 'new_m', 'new_m_ffn1_w_up': 'new_m', 'new_m_ffn1_w_down': 'new_m', 'new_m_mix_norm': 'new_m', 'new_m_w_in': 'new_m', 'new_m_conv_w': 'new_m', 'new_m_attn_sinks': 'new_m', 'new_m_w_out': 'new_m', 'new_m_ffn2_norm': 'new_m', 'new_m_ffn2_w_gate': 'new_m', 'new_m_ffn2_w_up': 'new_m', 'new_m_ffn2_w_down': 'new_m', 'new_m_final_norm': 'new_m', 'new_v_ffn1_norm': 'new_v', 'new_v_ffn1_w_gate': 'new_v', 'new_v_ffn1_w_up': 'new_v', 'new_v_ffn1_w_down': 'new_v', 'new_v_mix_norm': 'new_v', 'new_v_w_in': 'new_v', 'new_v_conv_w': 'new_v', 'new_v_attn_sinks': 'new_v', 'new_v_w_out': 'new_v', 'new_v_ffn2_norm': 'new_v', 'new_v_ffn2_w_gate': 'new_v', 'new_v_ffn2_w_up': 'new_v', 'new_v_ffn2_w_down': 'new_v', 'new_v_final_norm': 'new_v'}


def _forward(args):
    return _fwd_reference(*[args[k] for k in FWD_PARAMS])


def _output_shape():
    def fwd():
        inp = _fwd_setup_inputs(0)
        return _fwd_reference(*[inp[k] for k in FWD_PARAMS])
    out = _jax.eval_shape(fwd)
    return out.shape, out.dtype

N_MICROBATCH = 1
ADAM_LR = 0.001
ADAM_B1 = 0.9
ADAM_B2 = 0.999
ADAM_EPS = 1e-08
ADAM_WD = 0.01
ADAM_STEP = 10
PER_EXAMPLE_BATCH_AXIS = {'x': 0, 'loss_target': 0}
SHARED_INPUTS = []
_WEIGHT_DTYPES = {'ffn1_norm': _jnp.float32, 'ffn1_w_gate': _jnp.float32, 'ffn1_w_up': _jnp.float32, 'ffn1_w_down': _jnp.float32, 'mix_norm': _jnp.float32, 'w_in': _jnp.float32, 'conv_w': _jnp.float32, 'attn_sinks': _jnp.float32, 'w_out': _jnp.float32, 'ffn2_norm': _jnp.float32, 'ffn2_w_gate': _jnp.float32, 'ffn2_w_up': _jnp.float32, 'ffn2_w_down': _jnp.float32, 'final_norm': _jnp.float32}
MOMENT_SCALE = {'ffn1_norm': 1.466606e-01, 'ffn1_w_gate': 5.848293e-02, 'ffn1_w_up': 5.672196e-02, 'ffn1_w_down': 9.412078e-02, 'mix_norm': 2.543267e-01, 'w_in': 1.672002e-01, 'conv_w': 2.211481e-01, 'attn_sinks': 3.060626e-02, 'w_out': 1.448912e-01, 'ffn2_norm': 8.625498e-02, 'ffn2_w_gate': 3.589849e-02, 'ffn2_w_up': 3.503389e-02, 'ffn2_w_down': 5.805919e-02, 'final_norm': 6.399645e+01}


def _to_microbatches(a, axis):
    t = _jnp.moveaxis(a, axis, 0)
    t = t.reshape((N_MICROBATCH, t.shape[0] // N_MICROBATCH) + t.shape[1:])
    return _jnp.moveaxis(t, 1, axis + 1)


def setup_inputs(seed: int = 0) -> dict:
    inp = _fwd_setup_inputs(seed)
    key = _jax.random.fold_in(_jax.random.key(seed), 7919)
    shape, _ = _output_shape()
    out = dict(inp)
    out["loss_target"] = _jax.random.normal(_jax.random.fold_in(key, 0), shape, _jnp.float32)
    for i, name in enumerate(TWIN_WEIGHTS):
        w = inp[name].astype(_jnp.float32)
        if MOMENT_SCALE is None:
            s = _jnp.sqrt(_jnp.mean(_jnp.square(w)) + 1e-30)
        else:
            s = MOMENT_SCALE[name]
        km, kv = _jax.random.split(_jax.random.fold_in(key, i + 1))
        out[name] = w
        out["m_" + name] = s * _jax.random.normal(km, w.shape, _jnp.float32)
        out["v_" + name] = (s * s) * _jax.random.uniform(kv, w.shape, _jnp.float32, 0.5, 1.5)
    if N_MICROBATCH > 1:
        for name, axis in PER_EXAMPLE_BATCH_AXIS.items():
            out[name] = _to_microbatches(out[name], axis)
    return {'x': out['x'], 'ffn1_norm': out['ffn1_norm'], 'ffn1_w_gate': out['ffn1_w_gate'], 'ffn1_w_up': out['ffn1_w_up'], 'ffn1_w_down': out['ffn1_w_down'], 'mix_norm': out['mix_norm'], 'w_in': out['w_in'], 'conv_w': out['conv_w'], 'attn_sinks': out['attn_sinks'], 'w_out': out['w_out'], 'ffn2_norm': out['ffn2_norm'], 'ffn2_w_gate': out['ffn2_w_gate'], 'ffn2_w_up': out['ffn2_w_up'], 'ffn2_w_down': out['ffn2_w_down'], 'final_norm': out['final_norm'], 'loss_target': out['loss_target'], 'm_ffn1_norm': out['m_ffn1_norm'], 'm_ffn1_w_gate': out['m_ffn1_w_gate'], 'm_ffn1_w_up': out['m_ffn1_w_up'], 'm_ffn1_w_down': out['m_ffn1_w_down'], 'm_mix_norm': out['m_mix_norm'], 'm_w_in': out['m_w_in'], 'm_conv_w': out['m_conv_w'], 'm_attn_sinks': out['m_attn_sinks'], 'm_w_out': out['m_w_out'], 'm_ffn2_norm': out['m_ffn2_norm'], 'm_ffn2_w_gate': out['m_ffn2_w_gate'], 'm_ffn2_w_up': out['m_ffn2_w_up'], 'm_ffn2_w_down': out['m_ffn2_w_down'], 'm_final_norm': out['m_final_norm'], 'v_ffn1_norm': out['v_ffn1_norm'], 'v_ffn1_w_gate': out['v_ffn1_w_gate'], 'v_ffn1_w_up': out['v_ffn1_w_up'], 'v_ffn1_w_down': out['v_ffn1_w_down'], 'v_mix_norm': out['v_mix_norm'], 'v_w_in': out['v_w_in'], 'v_conv_w': out['v_conv_w'], 'v_attn_sinks': out['v_attn_sinks'], 'v_w_out': out['v_w_out'], 'v_ffn2_norm': out['v_ffn2_norm'], 'v_ffn2_w_gate': out['v_ffn2_w_gate'], 'v_ffn2_w_up': out['v_ffn2_w_up'], 'v_ffn2_w_down': out['v_ffn2_w_down'], 'v_final_norm': out['v_final_norm']}


def _loss(weights, diff, rest, loss_target):
    with _jax.named_scope("forward"):
        args = {**rest, TWIN_DIFF_INPUT: diff, **{k: w.astype(_WEIGHT_DTYPES[k]) for k, w in weights.items()}}
        y = _forward(args)
    with _jax.named_scope("loss_head"):
        err = _jnp.square(y.astype(_jnp.float32) - loss_target)
        return 0.5 * _jnp.sum(_jnp.mean(err, axis=-1)) if err.ndim else 0.5 * err


def _adamw(w, g, m, v):
    m = ADAM_B1 * m + (1.0 - ADAM_B1) * g
    v = ADAM_B2 * v + (1.0 - ADAM_B2) * _jnp.square(g)
    m_hat = m / (1.0 - ADAM_B1 ** ADAM_STEP)
    v_hat = v / (1.0 - ADAM_B2 ** ADAM_STEP)
    delta = -ADAM_LR * (m_hat / (_jnp.sqrt(v_hat) + ADAM_EPS) + ADAM_WD * w)
    return delta, m, v


def reference(x, ffn1_norm, ffn1_w_gate, ffn1_w_up, ffn1_w_down, mix_norm, w_in, conv_w, attn_sinks, w_out, ffn2_norm, ffn2_w_gate, ffn2_w_up, ffn2_w_down, final_norm, loss_target, m_ffn1_norm, m_ffn1_w_gate, m_ffn1_w_up, m_ffn1_w_down, m_mix_norm, m_w_in, m_conv_w, m_attn_sinks, m_w_out, m_ffn2_norm, m_ffn2_w_gate, m_ffn2_w_up, m_ffn2_w_down, m_final_norm, v_ffn1_norm, v_ffn1_w_gate, v_ffn1_w_up, v_ffn1_w_down, v_mix_norm, v_w_in, v_conv_w, v_attn_sinks, v_w_out, v_ffn2_norm, v_ffn2_w_gate, v_ffn2_w_up, v_ffn2_w_down, v_final_norm):
    given = dict(x=x, ffn1_norm=ffn1_norm, ffn1_w_gate=ffn1_w_gate, ffn1_w_up=ffn1_w_up, ffn1_w_down=ffn1_w_down, mix_norm=mix_norm, w_in=w_in, conv_w=conv_w, attn_sinks=attn_sinks, w_out=w_out, ffn2_norm=ffn2_norm, ffn2_w_gate=ffn2_w_gate, ffn2_w_up=ffn2_w_up, ffn2_w_down=ffn2_w_down, final_norm=final_norm, loss_target=loss_target, m_ffn1_norm=m_ffn1_norm, m_ffn1_w_gate=m_ffn1_w_gate, m_ffn1_w_up=m_ffn1_w_up, m_ffn1_w_down=m_ffn1_w_down, m_mix_norm=m_mix_norm, m_w_in=m_w_in, m_conv_w=m_conv_w, m_attn_sinks=m_attn_sinks, m_w_out=m_w_out, m_ffn2_norm=m_ffn2_norm, m_ffn2_w_gate=m_ffn2_w_gate, m_ffn2_w_up=m_ffn2_w_up, m_ffn2_w_down=m_ffn2_w_down, m_final_norm=m_final_norm, v_ffn1_norm=v_ffn1_norm, v_ffn1_w_gate=v_ffn1_w_gate, v_ffn1_w_up=v_ffn1_w_up, v_ffn1_w_down=v_ffn1_w_down, v_mix_norm=v_mix_norm, v_w_in=v_w_in, v_conv_w=v_conv_w, v_attn_sinks=v_attn_sinks, v_w_out=v_w_out, v_ffn2_norm=v_ffn2_norm, v_ffn2_w_gate=v_ffn2_w_gate, v_ffn2_w_up=v_ffn2_w_up, v_ffn2_w_down=v_ffn2_w_down, v_final_norm=v_final_norm)
    weights = {n: given[n] for n in TWIN_WEIGHTS}
    shared = {n: given[n] for n in SHARED_INPUTS}
    per_example = {n: given[n] for n in ['x']}
    grad_fn = _jax.value_and_grad(_loss, argnums=(0, 1))

    def one_microbatch(ex, loss_target):
        ex = dict(ex)
        diff = ex.pop(TWIN_DIFF_INPUT)
        return grad_fn(weights, diff, {**shared, **ex}, loss_target)

    if N_MICROBATCH == 1:
        loss, (grad_w, grad_x) = one_microbatch(per_example, given["loss_target"])
    else:
        def body(carry, xs):
            loss_sum, grad_sum = carry
            l_k, (gw_k, gx_k) = one_microbatch(xs[0], xs[1])
            with _jax.named_scope("update"):
                return (loss_sum + l_k, _jax.tree.map(_jnp.add, grad_sum, gw_k)), gx_k

        init = (_jnp.zeros((), _jnp.float32), _jax.tree.map(_jnp.zeros_like, weights))
        (loss, grad_w), grad_x = _jax.lax.scan(body, init, (per_example, given["loss_target"]))
    with _jax.named_scope("update"):
        delta_w, new_m, new_v = {}, {}, {}
        for n in TWIN_WEIGHTS:
            delta_w[n], new_m[n], new_v[n] = _adamw(weights[n], grad_w[n], given["m_" + n], given["v_" + n])
    return (loss, grad_x, *[grad_w[n] for n in TWIN_WEIGHTS], *[delta_w[n] for n in TWIN_WEIGHTS],
            *[new_m[n] for n in TWIN_WEIGHTS], *[new_v[n] for n in TWIN_WEIGHTS])
```

```python
import functools

import jax
import jax.numpy as jnp
from jax import lax
from jax.experimental import pallas as pl
from jax.experimental.pallas import tpu as pltpu

F32 = jnp.float32
BF16 = jnp.bfloat16
MESH = pl.DeviceIdType.MESH

CONV_WIDTH = 512
N_Q_HEADS = 8
HEAD_DIM = 64
BLOCK = 128
ROPE_THETA = 500000.0
ROT_DIM = 16
RMS_EPS = 1e-5
MASK_VALUE = -1e30
ATTN_SCALE = HEAD_DIM ** -0.5
FFN_RES_SCALE = 0.5
ADAM_LR = 0.001
ADAM_B1 = 0.9
ADAM_B2 = 0.999
ADAM_EPS = 1e-08
ADAM_WD = 0.01
ADAM_STEP = 10

N_CHIPS = 4
N_DEV = 8
LANES = 128
VMEM_LIMIT = 56 * 1024 * 1024

_pcall = pl.pallas_call


def _params(n_axes, vmem=VMEM_LIMIT):
    return pltpu.CompilerParams(dimension_semantics=("arbitrary",) * n_axes, vmem_limit_bytes=vmem)


def _dot(a, b):
    return jnp.dot(a, b, preferred_element_type=F32)


def _dot_nt(a, b):
    return lax.dot_general(a, b, (((1,), (1,)), ((), ())), preferred_element_type=F32)


def _dot_tn(a, b):
    return lax.dot_general(a, b, (((0,), (0,)), ((), ())), preferred_element_type=F32)


def _rms_inv(x):
    return lax.rsqrt(jnp.mean(x * x, axis=-1, keepdims=True) + RMS_EPS)


def _norm_bwd(dh, x, g):
    inv = _rms_inv(x)
    xhat = x * inv
    dg = jnp.sum(dh * xhat, axis=0, keepdims=True)
    dxhat = dh * g
    dx = inv * (dxhat - xhat * jnp.mean(dxhat * xhat, axis=-1, keepdims=True))
    return dx, dg


def _ffn_fwd(x, g, wg, wu, wd, name):
    T, D = x.shape
    nq, _, fq = wg.shape
    tm = min(T, 512)

    def body(x_ref, g_ref, wg_ref, wu_ref, wd_ref, xo_ref, h_ref, gate_ref, up_ref, acc_ref):
        q = pl.program_id(1)

        @pl.when(q == 0)
        def _():
            xv = x_ref[...]
            h_ref[...] = ((xv * _rms_inv(xv)) * g_ref[...]).astype(BF16)
            acc_ref[...] = jnp.zeros_like(acc_ref)

        h = h_ref[...]
        gate = _dot(h, wg_ref[0])
        up = _dot(h, wu_ref[0])
        gate_ref[0] = gate.astype(BF16)
        up_ref[0] = up.astype(BF16)
        act = (gate * jax.nn.sigmoid(gate) * up).astype(BF16)
        acc_ref[...] += _dot(act, wd_ref[0])

        @pl.when(q == nq - 1)
        def _():
            xo_ref[...] = x_ref[...] + FFN_RES_SCALE * acc_ref[...]

    return _pcall(
        body, name=name, grid=(T // tm, nq),
        in_specs=[
            pl.BlockSpec((tm, D), lambda i, q: (i, 0)),
            pl.BlockSpec((1, D), lambda i, q: (0, 0)),
            pl.BlockSpec((1, D, fq), lambda i, q: (q, 0, 0)),
            pl.BlockSpec((1, D, fq), lambda i, q: (q, 0, 0)),
            pl.BlockSpec((1, fq, D), lambda i, q: (q, 0, 0)),
        ],
        out_specs=[
            pl.BlockSpec((tm, D), lambda i, q: (i, 0)),
            pl.BlockSpec((tm, D), lambda i, q: (i, 0)),
            pl.BlockSpec((1, tm, fq), lambda i, q: (q, i, 0)),
            pl.BlockSpec((1, tm, fq), lambda i, q: (q, i, 0)),
        ],
        out_shape=[
            jax.ShapeDtypeStruct((T, D), F32),
            jax.ShapeDtypeStruct((T, D), BF16),
            jax.ShapeDtypeStruct((nq, T, fq), BF16),
            jax.ShapeDtypeStruct((nq, T, fq), BF16),
        ],
        scratch_shapes=[pltpu.VMEM((tm, D), F32)],
        compiler_params=_params(2),
    )(x, g, wg, wu, wd)


def _ffn_bwd(dy, x, g, gate, up, wg, wu, wd, name):
    T, D = x.shape
    nq, _, fq = wg.shape
    tm = min(T, 512)

    def body(dy_ref, x_ref, g_ref, gate_ref, up_ref, wg_ref, wu_ref, wd_ref,
             dx_ref, dyb_ref, dgate_ref, dup_ref, act_ref, dg_ref, dh_ref):
        i = pl.program_id(0)
        q = pl.program_id(1)

        @pl.when(jnp.logical_and(i == 0, q == 0))
        def _():
            dg_ref[...] = jnp.zeros_like(dg_ref)

        @pl.when(q == 0)
        def _():
            dyb_ref[...] = (FFN_RES_SCALE * dy_ref[...]).astype(BF16)
            dh_ref[...] = jnp.zeros_like(dh_ref)

        dact = _dot_nt(dyb_ref[...], wd_ref[0])
        gt = gate_ref[0].astype(F32)
        u = up_ref[0].astype(F32)
        sig = jax.nn.sigmoid(gt)
        silu = gt * sig
        act_ref[0] = (silu * u).astype(BF16)
        dup = (dact * silu).astype(BF16)
        dgate = (dact * u * (sig * (1.0 + gt * (1.0 - sig)))).astype(BF16)
        dgate_ref[0] = dgate
        dup_ref[0] = dup
        dh_ref[...] += _dot_nt(dgate, wg_ref[0]) + _dot_nt(dup, wu_ref[0])

        @pl.when(q == nq - 1)
        def _():
            dxn, dg = _norm_bwd(dh_ref[...], x_ref[...], g_ref[...])
            dx_ref[...] = dy_ref[...] + dxn
            dg_ref[...] += dg

    return _pcall(
        body, name=name, grid=(T // tm, nq),
        in_specs=[
            pl.BlockSpec((tm, D), lambda i, q: (i, 0)),
            pl.BlockSpec((tm, D), lambda i, q: (i, 0)),
            pl.BlockSpec((1, D), lambda i, q: (0, 0)),
            pl.BlockSpec((1, tm, fq), lambda i, q: (q, i, 0)),
            pl.BlockSpec((1, tm, fq), lambda i, q: (q, i, 0)),
            pl.BlockSpec((1, D, fq), lambda i, q: (q, 0, 0)),
            pl.BlockSpec((1, D, fq), lambda i, q: (q, 0, 0)),
            pl.BlockSpec((1, fq, D), lambda i, q: (q, 0, 0)),
        ],
        out_specs=[
            pl.BlockSpec((tm, D), lambda i, q: (i, 0)),
            pl.BlockSpec((tm, D), lambda i, q: (i, 0)),
            pl.BlockSpec((1, tm, fq), lambda i, q: (q, i, 0)),
            pl.BlockSpec((1, tm, fq), lambda i, q: (q, i, 0)),
            pl.BlockSpec((1, tm, fq), lambda i, q: (q, i, 0)),
            pl.BlockSpec((1, D), lambda i, q: (0, 0)),
        ],
        out_shape=[
            jax.ShapeDtypeStruct((T, D), F32),
            jax.ShapeDtypeStruct((T, D), BF16),
            jax.ShapeDtypeStruct((nq, T, fq), BF16),
            jax.ShapeDtypeStruct((nq, T, fq), BF16),
            jax.ShapeDtypeStruct((nq, T, fq), BF16),
            jax.ShapeDtypeStruct((1, D), F32),
        ],
        scratch_shapes=[pltpu.VMEM((tm, D), F32)],
        compiler_params=_params(2),
    )(dy, x, g, gate, up, wg, wu, wd)


def _matmul_tn(a, b, name):
    na, T, n1 = a.shape
    nb, _, n2 = b.shape
    nq = max(na, nb)
    tk = min(T, 512)
    nk = T // tk

    def body(a_ref, b_ref, o_ref, ob_ref):
        k = pl.program_id(1)

        @pl.when(k == 0)
        def _():
            o_ref[...] = jnp.zeros_like(o_ref)

        o_ref[0] += _dot_tn(a_ref[0], b_ref[0])

        @pl.when(k == nk - 1)
        def _():
            ob_ref[...] = o_ref[...].astype(BF16)

    a_map = (lambda q, k: (q, k, 0)) if na > 1 else (lambda q, k: (0, k, 0))
    b_map = (lambda q, k: (q, k, 0)) if nb > 1 else (lambda q, k: (0, k, 0))
    return _pcall(
        body, name=name, grid=(nq, nk),
        in_specs=[pl.BlockSpec((1, tk, n1), a_map), pl.BlockSpec((1, tk, n2), b_map)],
        out_specs=[pl.BlockSpec((1, n1, n2), lambda q, k: (q, 0, 0)),
                   pl.BlockSpec((1, n1, n2), lambda q, k: (q, 0, 0))],
        out_shape=[jax.ShapeDtypeStruct((nq, n1, n2), F32), jax.ShapeDtypeStruct((nq, n1, n2), BF16)],
        compiler_params=_params(2),
    )(a, b)


def _norm_matmul(x, g, w, name):
    T, D = x.shape
    n = w.shape[1]
    tm = min(T, 512)

    def body(x_ref, g_ref, w_ref, z_ref, h_ref):
        xv = x_ref[...]
        h = ((xv * _rms_inv(xv)) * g_ref[...]).astype(BF16)
        h_ref[...] = h
        z_ref[...] = _dot(h, w_ref[...])

    return _pcall(
        body, name=name, grid=(T // tm,),
        in_specs=[pl.BlockSpec((tm, D), lambda i: (i, 0)), pl.BlockSpec((1, D), lambda i: (0, 0)),
                  pl.BlockSpec((D, n), lambda i: (0, 0))],
        out_specs=[pl.BlockSpec((tm, n), lambda i: (i, 0)), pl.BlockSpec((tm, D), lambda i: (i, 0))],
        out_shape=[jax.ShapeDtypeStruct((T, n), F32), jax.ShapeDtypeStruct((T, D), BF16)],
        compiler_params=_params(1),
    )(x, g, w)


def _matmul_residual(y, w, x, name):
    T, D = x.shape
    kdim = y.shape[1]
    tm = min(T, 512)

    def body(y_ref, w_ref, x_ref, o_ref):
        o_ref[...] = x_ref[...] + _dot(y_ref[...], w_ref[...])

    return _pcall(
        body, name=name, grid=(T // tm,),
        in_specs=[pl.BlockSpec((tm, kdim), lambda i: (i, 0)), pl.BlockSpec((kdim, D), lambda i: (0, 0)),
                  pl.BlockSpec((tm, D), lambda i: (i, 0))],
        out_specs=pl.BlockSpec((tm, D), lambda i: (i, 0)),
        out_shape=jax.ShapeDtypeStruct((T, D), F32),
        compiler_params=_params(1),
    )(y, w, x)


def _matmul_nt(dx, w, name):
    T, D = dx.shape
    kdim = w.shape[0]
    tm = min(T, 512)

    def body(dx_ref, w_ref, dy_ref, dxb_ref):
        dxb = dx_ref[...].astype(BF16)
        dxb_ref[...] = dxb
        dy_ref[...] = _dot_nt(dxb, w_ref[...])

    return _pcall(
        body, name=name, grid=(T // tm,),
        in_specs=[pl.BlockSpec((tm, D), lambda i: (i, 0)), pl.BlockSpec((kdim, D), lambda i: (0, 0))],
        out_specs=[pl.BlockSpec((tm, kdim), lambda i: (i, 0)), pl.BlockSpec((tm, D), lambda i: (i, 0))],
        out_shape=[jax.ShapeDtypeStruct((T, kdim), F32), jax.ShapeDtypeStruct((T, D), BF16)],
        compiler_params=_params(1),
    )(dx, w)


def _matmul_nt_norm_bwd(dz, w, x, g, dres, name):
    T, D = x.shape
    n = dz.shape[1]
    tm = min(T, 512)

    def body(dz_ref, w_ref, x_ref, g_ref, dres_ref, dx_ref, dg_ref):
        @pl.when(pl.program_id(0) == 0)
        def _():
            dg_ref[...] = jnp.zeros_like(dg_ref)

        dh = _dot_nt(dz_ref[...], w_ref[...])
        dxn, dg = _norm_bwd(dh, x_ref[...], g_ref[...])
        dx_ref[...] = dres_ref[...] + dxn
        dg_ref[...] += dg

    return _pcall(
        body, name=name, grid=(T // tm,),
        in_specs=[pl.BlockSpec((tm, n), lambda i: (i, 0)), pl.BlockSpec((D, n), lambda i: (0, 0)),
                  pl.BlockSpec((tm, D), lambda i: (i, 0)), pl.BlockSpec((1, D), lambda i: (0, 0)),
                  pl.BlockSpec((tm, D), lambda i: (i, 0))],
        out_specs=[pl.BlockSpec((tm, D), lambda i: (i, 0)), pl.BlockSpec((1, D), lambda i: (0, 0))],
        out_shape=[jax.ShapeDtypeStruct((T, D), F32), jax.ShapeDtypeStruct((1, D), F32)],
        compiler_params=_params(1),
    )(dz, w, x, g, dres)


def _loss_head(x, g, target, name):
    T, D = x.shape
    tm = min(T, 512)

    def body(x_ref, g_ref, t_ref, dx_ref, dg_ref, loss_ref):
        @pl.when(pl.program_id(0) == 0)
        def _():
            dg_ref[...] = jnp.zeros_like(dg_ref)
            loss_ref[...] = jnp.zeros_like(loss_ref)

        xv = x_ref[...]
        gv = g_ref[...]
        out = (xv * _rms_inv(xv)) * gv
        diff = out - t_ref[...]
        loss_ref[...] += 0.5 * jnp.sum(jnp.mean(diff * diff, axis=-1, keepdims=True))
        dxn, dg = _norm_bwd(diff * (1.0 / D), xv, gv)
        dx_ref[...] = dxn
        dg_ref[...] += dg

    return _pcall(
        body, name=name, grid=(T // tm,),
        in_specs=[pl.BlockSpec((tm, D), lambda i: (i, 0)), pl.BlockSpec((1, D), lambda i: (0, 0)),
                  pl.BlockSpec((tm, D), lambda i: (i, 0))],
        out_specs=[pl.BlockSpec((tm, D), lambda i: (i, 0)), pl.BlockSpec((1, D), lambda i: (0, 0)),
                   pl.BlockSpec((1, LANES), lambda i: (0, 0))],
        out_shape=[jax.ShapeDtypeStruct((T, D), F32), jax.ShapeDtypeStruct((1, D), F32),
                   jax.ShapeDtypeStruct((1, LANES), F32)],
        compiler_params=_params(1),
    )(x, g, target)


Z_Q = 3 * CONV_WIDTH
Z_K = Z_Q + N_Q_HEADS * HEAD_DIM
Z_V = Z_K + LANES
Z_END = Z_V + LANES


def _rope_tables(T):
    half = ROT_DIM // 2
    inv_freq = ROPE_THETA ** (-jnp.arange(0, ROT_DIM, 2, dtype=F32) / ROT_DIM)
    ang = jnp.arange(T, dtype=F32)[:, None] * inv_freq[None, :]
    cos, sin = jnp.cos(ang), jnp.sin(ang)
    zero = jnp.zeros((T, half), F32)
    rest0 = jnp.zeros((T, HEAD_DIM - ROT_DIM), F32)
    c = jnp.concatenate([cos, cos, rest0 + 1.0], axis=1)
    s1 = jnp.concatenate([-sin, zero, rest0], axis=1)
    s2 = jnp.concatenate([zero, sin, rest0], axis=1)
    return jnp.concatenate([c, c, s1, s1, s2, s2], axis=1)


def _tab3(tab):
    return tab[:, 0:LANES], tab[:, LANES:2 * LANES], tab[:, 2 * LANES:3 * LANES]


def _rot(x, tab):
    c, s1, s2 = _tab3(tab)
    return x * c + pltpu.roll(x, LANES - ROT_DIM // 2, 1) * s1 + pltpu.roll(x, ROT_DIM // 2, 1) * s2


def _rot_t(d, tab):
    c, s1, s2 = _tab3(tab)
    return d * c + pltpu.roll(d * s1, ROT_DIM // 2, 1) + pltpu.roll(d * s2, LANES - ROT_DIM // 2, 1)


def _head_pads(a):
    lo = lax.broadcasted_iota(jnp.int32, a.shape, 1) < HEAD_DIM
    nat0 = jnp.where(lo, a, 0.0)
    nat1 = jnp.where(lo, 0.0, a)
    return {
        (0, 0): nat0.astype(BF16), (0, 1): pltpu.roll(nat0, HEAD_DIM, 1).astype(BF16),
        (1, 0): pltpu.roll(nat1, HEAD_DIM, 1).astype(BF16), (1, 1): nat1.astype(BF16),
    }


def _from_pads(even, odd, kv):
    lo = lax.broadcasted_iota(jnp.int32, even.shape, 1) < HEAD_DIM
    if kv == 0:
        return jnp.where(lo, even + pltpu.roll(odd, HEAD_DIM, 1), 0.0)
    return jnp.where(lo, 0.0, pltpu.roll(even, HEAD_DIM, 1) + odd)


def _window_mask(has_prev):
    ii = lax.broadcasted_iota(jnp.int32, (BLOCK, 2 * BLOCK), 0)
    jj = lax.broadcasted_iota(jnp.int32, (BLOCK, 2 * BLOCK), 1)
    rel = jj - BLOCK - ii
    return (rel <= 0) & (rel > -BLOCK) & ((jj >= BLOCK) | has_prev)


def _softmax_sink(q2, kpad, sink, mask):
    s = _dot_nt(q2, kpad) * ATTN_SCALE
    s = jnp.where(mask, s, MASK_VALUE)
    m = jnp.maximum(jnp.max(s, axis=-1, keepdims=True), sink)
    p = jnp.exp(s - m)
    esink = jnp.exp(sink - m)
    rden = 1.0 / (jnp.sum(p, axis=-1, keepdims=True) + esink)
    return p * rden, esink * rden


def _conv_taps(cg, u, cg_prev, u_prev, has_prev):
    vv = cg * u
    halo = jnp.where(has_prev, cg_prev * u_prev, 0.0)
    ext = jnp.concatenate([halo, vv], axis=0)
    rows = ext.shape[0]
    vv1 = pltpu.roll(ext, 1, 0)[8:rows]
    vv2 = pltpu.roll(ext, 2, 0)[8:rows]
    return vv, vv1, vv2


def _mix_specs(nb):
    cur = lambda n: jnp.minimum(n, nb - 1)
    prev = lambda n: jnp.maximum(jnp.minimum(n, nb - 1) - 1, 0)
    rows8_prev = lambda n: jnp.maximum(16 * jnp.minimum(n, nb - 1) - 1, 0)
    return cur, prev, [
        pl.BlockSpec((BLOCK, Z_END), lambda n: (cur(n), 0)),
        pl.BlockSpec((BLOCK, 2 * LANES), lambda n: (prev(n), Z_K // (2 * LANES))),
        pl.BlockSpec((8, CONV_WIDTH), lambda n: (rows8_prev(n), 1)),
        pl.BlockSpec((8, CONV_WIDTH), lambda n: (rows8_prev(n), 2)),
        pl.BlockSpec((BLOCK, 3 * LANES), lambda n: (cur(n), 0)),
        pl.BlockSpec((BLOCK, 3 * LANES), lambda n: (prev(n), 0)),
        pl.BlockSpec((3, CONV_WIDTH), lambda n: (0, 0)),
        pl.BlockSpec(memory_space=pltpu.SMEM),
    ]


def _mix_core_fwd(z, tab, conv_w, sinks, name):
    T = z.shape[0]
    nb = T // BLOCK
    _, _, specs = _mix_specs(nb)

    def body(z_ref, zkvp_ref, cgp_ref, up_ref, tab_ref, tabp_ref, cw_ref, sink_ref, y_ref):
        has_prev = pl.program_id(0) > 0
        bg = z_ref[:, 0:CONV_WIDTH]
        vv, vv1, vv2 = _conv_taps(z_ref[:, CONV_WIDTH:2 * CONV_WIDTH], z_ref[:, 2 * CONV_WIDTH:Z_Q],
                                  cgp_ref[...], up_ref[...], has_prev)
        conv = cw_ref[0:1, :] * vv2 + cw_ref[1:2, :] * vv1 + cw_ref[2:3, :] * vv
        y_ref[:, 0:CONV_WIDTH] = (bg * conv).astype(BF16)

        tab_c = tab_ref[...]
        tab_p = tabp_ref[...]
        k_all = jnp.concatenate([_rot(zkvp_ref[:, 0:LANES], tab_p), _rot(z_ref[:, Z_K:Z_V], tab_c)], axis=0)
        v_all = jnp.concatenate([zkvp_ref[:, LANES:2 * LANES], z_ref[:, Z_V:Z_END]], axis=0)
        kp = _head_pads(k_all)
        vp = _head_pads(v_all)
        mask = _window_mask(has_prev)
        for c in range(N_Q_HEADS // 2):
            kv = c // 2
            q2 = _rot(z_ref[:, Z_Q + LANES * c:Z_Q + LANES * (c + 1)], tab_c).astype(BF16)
            o = jnp.zeros((BLOCK, LANES), F32)
            for par in range(2):
                probs, _ = _softmax_sink(q2, kp[(kv, par)], sink_ref[0, 2 * c + par], mask)
                o = o + _dot(probs.astype(BF16), vp[(kv, par)])
            y_ref[:, CONV_WIDTH + LANES * c:CONV_WIDTH + LANES * (c + 1)] = o.astype(BF16)

    return _pcall(
        body, name=name, grid=(nb,), in_specs=specs,
        out_specs=pl.BlockSpec((BLOCK, 2 * CONV_WIDTH), lambda n: (n, 0)),
        out_shape=jax.ShapeDtypeStruct((T, 2 * CONV_WIDTH), BF16),
        compiler_params=_params(1),
    )(z, z, z, z, tab, tab, conv_w, sinks)


def _mix_core_bwd(z, dy, tab, conv_w, sinks, name):
    T = z.shape[0]
    nb = T // BLOCK
    cur, _, specs = _mix_specs(nb)
    rows8_next = lambda n: jnp.minimum(16 * (cur(n) + 1), 16 * nb - 1)
    specs = specs[:4] + [
        pl.BlockSpec((8, CONV_WIDTH), lambda n: (rows8_next(n), 0)),
        pl.BlockSpec((BLOCK, 2 * CONV_WIDTH), lambda n: (cur(n), 0)),
        pl.BlockSpec((8, CONV_WIDTH), lambda n: (rows8_next(n), 0)),
    ] + specs[4:]

    def body(z_ref, zkvp_ref, cgp_ref, up_ref, bgn_ref, dy_ref, dyn_ref, tab_ref, tabp_ref, cw_ref, sink_ref,
             dz_ref, dcw_ref, dsk_ref, main_ref, kv_ref):
        n = pl.program_id(0)

        @pl.when(n == 0)
        def _():
            main_ref[...] = jnp.zeros_like(main_ref)
            kv_ref[...] = jnp.zeros_like(kv_ref)
            dcw_ref[...] = jnp.zeros_like(dcw_ref)
            dsk_ref[...] = jnp.zeros_like(dsk_ref)

        @pl.when(n < nb)
        def _():
            has_prev = n > 0
            has_next = n < nb - 1
            bg = z_ref[:, 0:CONV_WIDTH]
            cg = z_ref[:, CONV_WIDTH:2 * CONV_WIDTH]
            u = z_ref[:, 2 * CONV_WIDTH:Z_Q]
            vv, vv1, vv2 = _conv_taps(cg, u, cgp_ref[...], up_ref[...], has_prev)
            w0, w1, w2 = cw_ref[0:1, :], cw_ref[1:2, :], cw_ref[2:3, :]
            dyc = dy_ref[:, 0:CONV_WIDTH]
            dbg = dyc * (w0 * vv2 + w1 * vv1 + w2 * vv)
            dconv = dyc * bg
            dconv_next = jnp.where(has_next, dyn_ref[...] * bgn_ref[...], 0.0)
            ext = jnp.concatenate([dconv, dconv_next], axis=0)
            rows = ext.shape[0]
            dvv = w2 * dconv + w1 * pltpu.roll(ext, rows - 1, 0)[0:BLOCK] + w0 * pltpu.roll(ext, rows - 2, 0)[0:BLOCK]
            dcw_ref[0:1, :] += jnp.sum(dconv * vv2, axis=0, keepdims=True)
            dcw_ref[1:2, :] += jnp.sum(dconv * vv1, axis=0, keepdims=True)
            dcw_ref[2:3, :] += jnp.sum(dconv * vv, axis=0, keepdims=True)

            tab_c = tab_ref[...]
            tab_p = tabp_ref[...]
            k_all = jnp.concatenate([_rot(zkvp_ref[:, 0:LANES], tab_p), _rot(z_ref[:, Z_K:Z_V], tab_c)], axis=0)
            v_all = jnp.concatenate([zkvp_ref[:, LANES:2 * LANES], z_ref[:, Z_V:Z_END]], axis=0)
            kp = _head_pads(k_all)
            vp = _head_pads(v_all)
            mask = _window_mask(has_prev)
            dq_chunks = []
            dk_nat = jnp.zeros((2 * BLOCK, LANES), F32)
            dv_nat = jnp.zeros((2 * BLOCK, LANES), F32)
            for kv in range(2):
                q2s, dos, pbs, dss = [], [], [[], []], [[], []]
                for c in (2 * kv, 2 * kv + 1):
                    q2 = _rot(z_ref[:, Z_Q + LANES * c:Z_Q + LANES * (c + 1)], tab_c).astype(BF16)
                    do2 = dy_ref[:, CONV_WIDTH + LANES * c:CONV_WIDTH + LANES * (c + 1)].astype(BF16)
                    dq = jnp.zeros((BLOCK, LANES), F32)
                    for par in range(2):
                        h = 2 * c + par
                        probs, psink = _softmax_sink(q2, kp[(kv, par)], sink_ref[0, h], mask)
                        dp = _dot_nt(do2, vp[(kv, par)])
                        delta = jnp.sum(dp * probs, axis=-1, keepdims=True)
                        ds = (probs * (dp - delta) * ATTN_SCALE).astype(BF16)
                        dsk_ref[h:h + 1, :] += jnp.sum(-psink * delta)
                        dq = dq + _dot(ds, kp[(kv, par)])
                        pbs[par].append(probs.astype(BF16))
                        dss[par].append(ds)
                    q2s.append(q2)
                    dos.append(do2)
                    dq_chunks.append(_rot_t(dq, tab_c))
                q_st = jnp.concatenate(q2s, axis=0)
                do_st = jnp.concatenate(dos, axis=0)
                dk_par = [_dot_tn(jnp.concatenate(dss[par], axis=0), q_st) for par in range(2)]
                dv_par = [_dot_tn(jnp.concatenate(pbs[par], axis=0), do_st) for par in range(2)]
                dk_nat = dk_nat + _from_pads(dk_par[0], dk_par[1], kv)
                dv_nat = dv_nat + _from_pads(dv_par[0], dv_par[1], kv)

            dk_prev = _rot_t(kv_ref[:, 0:LANES] + dk_nat[0:BLOCK], tab_p)
            dv_prev = kv_ref[:, LANES:2 * LANES] + dv_nat[0:BLOCK]
            dz_ref[:, 0:Z_K] = main_ref[...]
            dz_ref[:, Z_K:Z_V] = dk_prev.astype(BF16)
            dz_ref[:, Z_V:Z_END] = dv_prev.astype(BF16)
            main_ref[:, 0:CONV_WIDTH] = dbg.astype(BF16)
            main_ref[:, CONV_WIDTH:2 * CONV_WIDTH] = (dvv * u).astype(BF16)
            main_ref[:, 2 * CONV_WIDTH:Z_Q] = (dvv * cg).astype(BF16)
            for c in range(N_Q_HEADS // 2):
                main_ref[:, Z_Q + LANES * c:Z_Q + LANES * (c + 1)] = dq_chunks[c].astype(BF16)
            kv_ref[:, 0:LANES] = dk_nat[BLOCK:2 * BLOCK]
            kv_ref[:, LANES:2 * LANES] = dv_nat[BLOCK:2 * BLOCK]

        @pl.when(n == nb)
        def _():
            dz_ref[:, 0:Z_K] = main_ref[...]
            dz_ref[:, Z_K:Z_V] = _rot_t(kv_ref[:, 0:LANES], tab_ref[...]).astype(BF16)
            dz_ref[:, Z_V:Z_END] = kv_ref[:, LANES:2 * LANES].astype(BF16)

    return _pcall(
        body, name=name, grid=(nb + 1,), in_specs=specs,
        out_specs=[pl.BlockSpec((BLOCK, Z_END), lambda n: (jnp.maximum(n - 1, 0), 0)),
                   pl.BlockSpec((8, CONV_WIDTH), lambda n: (0, 0)),
                   pl.BlockSpec((8, LANES), lambda n: (0, 0))],
        out_shape=[jax.ShapeDtypeStruct((T, Z_END), BF16), jax.ShapeDtypeStruct((8, CONV_WIDTH), F32),
                   jax.ShapeDtypeStruct((8, LANES), F32)],
        scratch_shapes=[pltpu.VMEM((BLOCK, Z_K), BF16), pltpu.VMEM((BLOCK, 2 * LANES), F32)],
        compiler_params=_params(1),
    )(z, z, z, z, z, dy, dy, tab, tab, conv_w, sinks)


HBM_SPEC = pl.BlockSpec(memory_space=pltpu.HBM)


def _place():
    x, y, c = lax.axis_index("x"), lax.axis_index("y"), lax.axis_index("c")
    chips = [(1 - x, y), (x, 1 - y), (1 - x, 1 - y)]
    return x, y, c, chips


def _all_gather_chips(shards, name):
    n = len(shards)

    def body(*refs):
        ins, outs = refs[:n], refs[n:2 * n]
        send_sems, recv_sems, local_sems = refs[2 * n:]
        x, y, c, chips = _place()
        me = 2 * x + y
        local = [pltpu.make_async_copy(ins[t], outs[t].at[me], local_sems.at[t]) for t in range(n)]
        for cp in local:
            cp.start()

        def copy(t, j, block):
            return pltpu.make_async_remote_copy(
                src_ref=ins[t], dst_ref=outs[t].at[block], send_sem=send_sems.at[3 * t + j],
                recv_sem=recv_sems.at[3 * t + j], device_id=(*chips[j], c), device_id_type=MESH)

        sends = [copy(t, j, me) for t in range(n) for j in range(3)]
        for cp in sends:
            cp.start()
        for t in range(n):
            for j, (px, py) in enumerate(chips):
                copy(t, j, 2 * px + py).wait_recv()
        for cp in sends:
            cp.wait_send()
        for cp in local:
            cp.wait()

    return _pcall(
        body, name=name, in_specs=[HBM_SPEC] * n, out_specs=[HBM_SPEC] * n,
        out_shape=[jax.ShapeDtypeStruct((N_CHIPS, *s.shape), s.dtype) for s in shards],
        scratch_shapes=[pltpu.SemaphoreType.DMA((3 * n,)), pltpu.SemaphoreType.DMA((3 * n,)),
                        pltpu.SemaphoreType.DMA((n,))],
    )(*shards)


def _sibling_halves(grads_b, name):
    n = len(grads_b)

    def body(*refs):
        ins, outs = refs[:n], refs[n:2 * n]
        send_sems, recv_sems = refs[2 * n:]
        x, y, c, _ = _place()

        def copy(t):
            half = ins[t].shape[1] // 2
            return pltpu.make_async_remote_copy(
                src_ref=ins[t].at[:, pl.ds((1 - c) * half, half), :], dst_ref=outs[t],
                send_sem=send_sems.at[t], recv_sem=recv_sems.at[t], device_id=(x, y, 1 - c), device_id_type=MESH)

        cps = [copy(t) for t in range(n)]
        for cp in cps:
            cp.start()
        for cp in cps:
            cp.wait()

    return _pcall(
        body, name=name, in_specs=[HBM_SPEC] * n, out_specs=[HBM_SPEC] * n,
        out_shape=[jax.ShapeDtypeStruct((g.shape[0], g.shape[1] // 2, g.shape[2]), g.dtype) for g in grads_b],
        scratch_shapes=[pltpu.SemaphoreType.DMA((n,)), pltpu.SemaphoreType.DMA((n,))],
    )(*grads_b)


def _scatter_chips(parts_b, name):
    n = len(parts_b)

    def body(*refs):
        ins, outs = refs[:n], refs[n:2 * n]
        send_sems, recv_sems = refs[2 * n:]
        x, y, c, chips = _place()

        def copy(t, j):
            px, py = chips[j]
            return pltpu.make_async_remote_copy(
                src_ref=ins[t].at[2 * px + py], dst_ref=outs[t].at[j], send_sem=send_sems.at[3 * t + j],
                recv_sem=recv_sems.at[3 * t + j], device_id=(px, py, c), device_id_type=MESH)

        cps = [copy(t, j) for t in range(n) for j in range(3)]
        for cp in cps:
            cp.start()
        for cp in cps:
            cp.wait()

    return _pcall(
        body, name=name, in_specs=[HBM_SPEC] * n, out_specs=[HBM_SPEC] * n,
        out_shape=[jax.ShapeDtypeStruct((3, *p.shape[1:]), p.dtype) for p in parts_b],
        scratch_shapes=[pltpu.SemaphoreType.DMA((3 * n,)), pltpu.SemaphoreType.DMA((3 * n,))],
    )(*parts_b)


def _join_halves(halves, name):
    n = len(halves)

    def body(*refs):
        ins, outs = refs[:n], refs[n:2 * n]
        send_sems, recv_sems, local_sems = refs[2 * n:]
        x, y, c, _ = _place()

        def rows(t, core):
            half = ins[t].shape[0]
            return outs[t].at[pl.ds(core * half, half), :]

        local = [pltpu.make_async_copy(ins[t], rows(t, c), local_sems.at[t]) for t in range(n)]
        for cp in local:
            cp.start()

        def copy(t, core):
            return pltpu.make_async_remote_copy(
                src_ref=ins[t], dst_ref=rows(t, core), send_sem=send_sems.at[t], recv_sem=recv_sems.at[t],
                device_id=(x, y, 1 - c), device_id_type=MESH)

        sends = [copy(t, c) for t in range(n)]
        for cp in sends:
            cp.start()
        for t in range(n):
            copy(t, 1 - c).wait_recv()
        for cp in sends:
            cp.wait_send()
        for cp in local:
            cp.wait()

    return _pcall(
        body, name=name, in_specs=[HBM_SPEC] * n, out_specs=[HBM_SPEC] * n,
        out_shape=[jax.ShapeDtypeStruct((2 * h.shape[0], h.shape[1]), h.dtype) for h in halves],
        scratch_shapes=[pltpu.SemaphoreType.DMA((n,)), pltpu.SemaphoreType.DMA((n,)),
                        pltpu.SemaphoreType.DMA((n,))],
    )(*halves)


def _all_reduce_small(vec, name):
    s = vec.shape[1]

    def body(v_ref, o_ref, all_ref, send_sems, recv_sems, local_sem):
        x, y, c, _ = _place()
        me = 4 * x + 2 * y + c

        def rows(dev):
            return all_ref.at[pl.ds(pl.multiple_of(8 * dev, 8), 8), :]

        mine = pltpu.make_async_copy(v_ref, rows(me), local_sem)
        mine.start()
        rel = [((k >> 2) & 1, (k >> 1) & 1, k & 1) for k in range(1, N_DEV)]

        def peer(k):
            fx, fy, fc = rel[k]
            return (x ^ fx, y ^ fy, c ^ fc)

        def copy(k, dev):
            return pltpu.make_async_remote_copy(
                src_ref=v_ref, dst_ref=rows(dev), send_sem=send_sems.at[k], recv_sem=recv_sems.at[k],
                device_id=peer(k), device_id_type=MESH)

        sends = [copy(k, me) for k in range(N_DEV - 1)]
        for cp in sends:
            cp.start()
        for k in range(N_DEV - 1):
            px, py, pc = peer(k)
            copy(k, 4 * px + 2 * py + pc).wait_recv()
        for cp in sends:
            cp.wait_send()
        mine.wait()
        total = all_ref[0:8, :]
        for dev in range(1, N_DEV):
            total = total + all_ref[8 * dev:8 * dev + 8, :]
        o_ref[...] = total

    return _pcall(
        body, name=name,
        in_specs=[pl.BlockSpec(memory_space=pltpu.VMEM)], out_specs=pl.BlockSpec(memory_space=pltpu.VMEM),
        out_shape=jax.ShapeDtypeStruct((8, s), F32),
        scratch_shapes=[pltpu.VMEM((8 * N_DEV, s), F32), pltpu.SemaphoreType.DMA((N_DEV - 1,)),
                        pltpu.SemaphoreType.DMA((N_DEV - 1,)), pltpu.SemaphoreType.DMA],
    )(vec)


ROW_SPLIT = 2


def _pair_sum(grads, recvd, core, name):
    n = len(grads)

    def body(core_ref, *refs):
        g, r = refs[:n], refs[n:2 * n]
        s, sb = refs[2 * n:3 * n], refs[3 * n:]
        for t in range(n):
            tot = g[t][...] + r[t][...].astype(F32)
            s[t][...] = tot
            sb[t][...] = tot.astype(BF16)

    def blk(a):
        return (1, a.shape[1] // ROW_SPLIT, a.shape[2])

    in_specs = [pl.BlockSpec(blk(r), lambda q, i, core_ref: (q, core_ref[0] * ROW_SPLIT + i, 0)) for r in recvd]
    in_specs += [pl.BlockSpec(blk(r), lambda q, i, core_ref: (q, i, 0)) for r in recvd]
    out_specs = [pl.BlockSpec(blk(r), lambda q, i, core_ref: (q, i, 0)) for r in recvd] * 2
    return _pcall(
        body, name=name,
        grid_spec=pltpu.PrefetchScalarGridSpec(num_scalar_prefetch=1, grid=(N_CHIPS, ROW_SPLIT),
                                               in_specs=in_specs, out_specs=out_specs),
        out_shape=[jax.ShapeDtypeStruct(r.shape, F32) for r in recvd] + [jax.ShapeDtypeStruct(r.shape, BF16) for r in recvd],
        compiler_params=_params(2),
    )(core, *grads, *recvd)


def _chip_sum(parts, recvd, chip, name):
    n = len(parts)

    def body(chip_ref, *refs):
        p, r, o = refs[:n], refs[n:2 * n], refs[2 * n:]
        for t in range(n):
            tot = p[t][0]
            for j in range(3):
                tot = tot + r[t][j].astype(F32)
            o[t][...] = tot

    in_specs = [pl.BlockSpec((1, p.shape[1] // ROW_SPLIT, p.shape[2]), lambda i, chip_ref: (chip_ref[0], i, 0))
                for p in parts]
    in_specs += [pl.BlockSpec((3, r.shape[1] // ROW_SPLIT, r.shape[2]), lambda i, chip_ref: (0, i, 0)) for r in recvd]
    out_specs = [pl.BlockSpec((p.shape[1] // ROW_SPLIT, p.shape[2]), lambda i, chip_ref: (i, 0)) for p in parts]
    return _pcall(
        body, name=name,
        grid_spec=pltpu.PrefetchScalarGridSpec(num_scalar_prefetch=1, grid=(ROW_SPLIT,),
                                               in_specs=in_specs, out_specs=out_specs),
        out_shape=[jax.ShapeDtypeStruct(p.shape[1:], F32) for p in parts],
        compiler_params=_params(1),
    )(chip, *parts, *recvd)


def _adamw_math(w, g, m, v):
    m = ADAM_B1 * m + (1.0 - ADAM_B1) * g
    v = ADAM_B2 * v + (1.0 - ADAM_B2) * (g * g)
    m_hat = m / (1.0 - ADAM_B1 ** ADAM_STEP)
    v_hat = v / (1.0 - ADAM_B2 ** ADAM_STEP)
    delta = -ADAM_LR * (m_hat / (jnp.sqrt(v_hat) + ADAM_EPS) + ADAM_WD * w)
    return delta, m, v


def _adamw(ws, gs, ms, vs, row_blocks, name):
    n = len(ws)

    def body(*refs):
        w, g, m, v = refs[:n], refs[n:2 * n], refs[2 * n:3 * n], refs[3 * n:4 * n]
        d, mo, vo = refs[4 * n:5 * n], refs[5 * n:6 * n], refs[6 * n:]
        for t in range(n):
            delta, m_new, v_new = _adamw_math(w[t][...], g[t][...], m[t][...], v[t][...])
            d[t][...] = delta
            mo[t][...] = m_new
            vo[t][...] = v_new

    specs = [pl.BlockSpec((a.shape[0] // row_blocks, a.shape[1]), lambda i: (i, 0)) for a in ws]
    shapes = [jax.ShapeDtypeStruct(a.shape, F32) for a in ws]
    return _pcall(
        body, name=name, grid=(row_blocks,), in_specs=specs * 4, out_specs=specs * 3, out_shape=shapes * 3,
        compiler_params=_params(1),
    )(*ws, *gs, *ms, *vs)


def kernel(x, ffn1_norm, ffn1_w_gate, ffn1_w_up, ffn1_w_down, mix_norm, w_in, conv_w, attn_sinks, w_out, ffn2_norm, ffn2_w_gate, ffn2_w_up, ffn2_w_down, final_norm, loss_target, m_ffn1_norm, m_ffn1_w_gate, m_ffn1_w_up, m_ffn1_w_down, m_mix_norm, m_w_in, m_conv_w, m_attn_sinks, m_w_out, m_ffn2_norm, m_ffn2_w_gate, m_ffn2_w_up, m_ffn2_w_down, m_final_norm, v_ffn1_norm, v_ffn1_w_gate, v_ffn1_w_up, v_ffn1_w_down, v_mix_norm, v_w_in, v_conv_w, v_attn_sinks, v_w_out, v_ffn2_norm, v_ffn2_w_gate, v_ffn2_w_up, v_ffn2_w_down, v_final_norm):
    T, D = x.shape[1], x.shape[2]
    chip = (2 * lax.axis_index("x") + lax.axis_index("y")).astype(jnp.int32)
    core = lax.axis_index("c").astype(jnp.int32)
    x0 = x[0]
    target = loss_target[0]
    gf = final_norm.reshape(1, D)

    big = [ffn1_w_gate[0], ffn1_w_up[0], ffn1_w_down[0], w_in[0], w_out[0], ffn2_w_gate[0], ffn2_w_up[0], ffn2_w_down[0]]
    gathered = _all_gather_chips([w.astype(BF16) for w in big] + [conv_w[0]], "gather_weights")
    wg1, wu1, wd1, win4, wout4, wg2, wu2, wd2, convw4 = gathered
    win = jnp.transpose(win4, (1, 0, 2)).reshape(D, -1)
    wout = wout4.reshape(-1, D)
    convw = jnp.transpose(convw4, (1, 0, 2)).reshape(3, -1)
    tab = _rope_tables(T)

    x1, h1, gate1, up1 = _ffn_fwd(x0, ffn1_norm, wg1, wu1, wd1, "ffn1_fwd")
    z, hm = _norm_matmul(x1, mix_norm, win, "mix_in_fwd")
    ymix = _mix_core_fwd(z, tab, convw, attn_sinks, "mix_core_fwd")
    x2 = _matmul_residual(ymix, wout, x1, "mix_out_fwd")
    x3, h2, gate2, up2 = _ffn_fwd(x2, ffn2_norm, wg2, wu2, wd2, "ffn2_fwd")
    dx3, dgf, loss_part = _loss_head(x3, gf, target, "loss_head")

    dx2, dyb2, dgate2, dup2, act2, dg2 = _ffn_bwd(dx3, x2, ffn2_norm, gate2, up2, wg2, wu2, wd2, "ffn2_bwd")
    gwg2, gwg2b = _matmul_tn(h2[None], dgate2, "ffn2_dwg")
    gwu2, gwu2b = _matmul_tn(h2[None], dup2, "ffn2_dwu")
    gwd2, gwd2b = _matmul_tn(act2, dyb2[None], "ffn2_dwd")
    dymix, dx2b = _matmul_nt(dx2, wout, "mix_out_bwd")
    ymix4 = jnp.transpose(ymix.reshape(T, N_CHIPS, -1), (1, 0, 2))
    gwout, gwoutb = _matmul_tn(ymix4, dx2b[None], "mix_dwout")
    dz, dcw, dsk = _mix_core_bwd(z, dymix, tab, convw, attn_sinks, "mix_core_bwd")
    gwin_full, gwin_fullb = _matmul_tn(hm[None], dz[None], "mix_dwin")
    dx1, dgm = _matmul_nt_norm_bwd(dz, win, x1, mix_norm, dx2, "mix_in_bwd")
    dx0, dyb1, dgate1, dup1, act1, dg1 = _ffn_bwd(dx1, x0, ffn1_norm, gate1, up1, wg1, wu1, wd1, "ffn1_bwd")
    gwg1, gwg1b = _matmul_tn(h1[None], dgate1, "ffn1_dwg")
    gwu1, gwu1b = _matmul_tn(h1[None], dup1, "ffn1_dwu")
    gwd1, gwd1b = _matmul_tn(act1, dyb1[None], "ffn1_dwd")

    def quarters(a):
        return jnp.transpose(a[0].reshape(D, N_CHIPS, -1), (1, 0, 2))

    gwin, gwinb = quarters(gwin_full), quarters(gwin_fullb)

    grads = [gwg1, gwu1, gwd1, gwin, gwout, gwg2, gwu2, gwd2]
    grads_b = [gwg1b, gwu1b, gwd1b, gwinb, gwoutb, gwg2b, gwu2b, gwd2b]
    from_sibling = _sibling_halves(grads_b, "grads_to_sibling")
    pair = _pair_sum(grads, from_sibling, core.reshape(1), "grads_pair_sum")
    pair_f, pair_b = pair[:len(grads)], pair[len(grads):]
    from_chips = _scatter_chips(pair_b, "grads_to_chips")
    mine_half = _chip_sum(pair_f, from_chips, chip.reshape(1), "grads_chip_sum")
    g_big = _join_halves(mine_half, "grads_join")

    pad = lambda a: jnp.pad(a, ((0, 0), (0, LANES - a.shape[1])))
    vec = jnp.concatenate([dg1, dgm, dg2, dgf, dcw[0:3].reshape(1, -1), pad(dsk[:, 0].reshape(1, -1)),
                           pad(loss_part[:, 0:1])], axis=1)
    total = _all_reduce_small(jnp.pad(vec, ((0, 7), (0, 0))), "small_all_reduce")[0:1]
    g_n1, g_nm, g_n2, g_nf = (total[:, k * D:(k + 1) * D] for k in range(4))
    cw_full = total[:, 4 * D:4 * D + 3 * CONV_WIDTH].reshape(3, CONV_WIDTH)
    cq = CONV_WIDTH // N_CHIPS
    g_cw = lax.dynamic_slice(cw_full, (0, chip * cq), (3, cq))
    off = 4 * D + 3 * CONV_WIDTH
    g_sk = total[:, off:off + N_Q_HEADS]
    loss = total[0, off + LANES]

    ws = big
    ms = [m_ffn1_w_gate[0], m_ffn1_w_up[0], m_ffn1_w_down[0], m_w_in[0], m_w_out[0], m_ffn2_w_gate[0], m_ffn2_w_up[0], m_ffn2_w_down[0]]
    vs = [v_ffn1_w_gate[0], v_ffn1_w_up[0], v_ffn1_w_down[0], v_w_in[0], v_w_out[0], v_ffn2_w_gate[0], v_ffn2_w_up[0], v_ffn2_w_down[0]]
    upd = {}
    for name_, idx in (("adamw_a", [0, 1, 2, 4]), ("adamw_b", [3, 5, 6, 7])):
        res = _adamw([ws[i] for i in idx], [g_big[i] for i in idx], [ms[i] for i in idx], [vs[i] for i in idx], 8, name_)
        k = len(idx)
        for j, i in enumerate(idx):
            upd[i] = (res[j], res[k + j], res[2 * k + j])
    sw = [ffn1_norm, mix_norm, conv_w[0], attn_sinks, ffn2_norm, gf]
    sg = [g_n1, g_nm, g_cw, g_sk, g_n2, g_nf]
    sm = [m_ffn1_norm, m_mix_norm, m_conv_w[0], m_attn_sinks, m_ffn2_norm, m_final_norm.reshape(1, D)]
    sv = [v_ffn1_norm, v_mix_norm, v_conv_w[0], v_attn_sinks, v_ffn2_norm, v_final_norm.reshape(1, D)]
    sres = _adamw(sw, sg, sm, sv, 1, "adamw_small")
    supd = [(sres[j], sres[6 + j], sres[12 + j]) for j in range(6)]

    order = [("s", 0), ("b", 0), ("b", 1), ("b", 2), ("s", 1), ("b", 3), ("s", 2), ("s", 3), ("b", 4),
             ("s", 4), ("b", 5), ("b", 6), ("b", 7), ("s", 5)]

    def leaf(kind, i, which):
        if kind == "b":
            a = g_big[i] if which == 0 else upd[i][which - 1]
            return a[None]
        a = sg[i] if which == 0 else supd[i][which - 1]
        if i == 2:
            return a[None]
        if i == 5:
            return a.reshape(D)
        return a

    outs = [loss, dx0[None]]
    for which in range(4):
        outs += [leaf(kind, i, which) for kind, i in order]
    return tuple(outs)
```

```python
import functools

import jax
import jax.numpy as jnp
from jax import lax
from jax.experimental import pallas as pl
from jax.experimental.pallas import tpu as pltpu

F32 = jnp.float32
BF16 = jnp.bfloat16
MESH = pl.DeviceIdType.MESH

CONV_WIDTH = 512
N_Q_HEADS = 8
HEAD_DIM = 64
BLOCK = 128
ROPE_THETA = 500000.0
ROT_DIM = 16
RMS_EPS = 1e-5
MASK_VALUE = -1e30
ATTN_SCALE = HEAD_DIM ** -0.5
FFN_RES_SCALE = 0.5
ADAM_LR = 0.001
ADAM_B1 = 0.9
ADAM_B2 = 0.999
ADAM_EPS = 1e-08
ADAM_WD = 0.01
ADAM_STEP = 10

N_CHIPS = 4
N_DEV = 8
LANES = 128
VMEM_LIMIT = 56 * 1024 * 1024

_pcall = pl.pallas_call


def _params(n_axes, vmem=VMEM_LIMIT):
    return pltpu.CompilerParams(dimension_semantics=("arbitrary",) * n_axes, vmem_limit_bytes=vmem)


def _dot(a, b):
    return jnp.dot(a, b, preferred_element_type=F32)


def _dot_nt(a, b):
    return lax.dot_general(a, b, (((1,), (1,)), ((), ())), preferred_element_type=F32)


def _dot_tn(a, b):
    return lax.dot_general(a, b, (((0,), (0,)), ((), ())), preferred_element_type=F32)


def _rms_inv(x):
    return lax.rsqrt(jnp.mean(x * x, axis=-1, keepdims=True) + RMS_EPS)


def _norm_bwd(dh, x, g):
    inv = _rms_inv(x)
    xhat = x * inv
    dg = jnp.sum(dh * xhat, axis=0, keepdims=True)
    dxhat = dh * g
    dx = inv * (dxhat - xhat * jnp.mean(dxhat * xhat, axis=-1, keepdims=True))
    return dx, dg


TOKEN_TILE = 512
F_SPLIT = 2
MXU_COLS = 256


def _chunks(n):
    out, c0 = [], 0
    while c0 < n:
        size = min(MXU_COLS, n - c0)
        out.append((c0, size))
        c0 += size
    return out


def _load_weights(hbm_refs, vmem_refs, sems):
    copies = [pltpu.make_async_copy(h, v, sems.at[k]) for k, (h, v) in enumerate(zip(hbm_refs, vmem_refs))]
    for cp in copies:
        cp.start()
    for cp in copies:
        cp.wait()


def _ffn_fwd(x, g, wgt, wut, wd, name):
    T, D = x.shape
    F = wgt.shape[0]
    tm = min(T, TOKEN_TILE)
    tf = F // F_SPLIT

    def body(x_ref, g_ref, wg_hbm, wu_hbm, wd_hbm, xo_ref, h_ref, gate_ref, up_ref, act_ref,
             wg_ref, wu_ref, wd_ref, acc_ref, sems):
        i = pl.program_id(0)
        f = pl.program_id(1)

        @pl.when(jnp.logical_and(i == 0, f == 0))
        def _():
            _load_weights((wg_hbm, wu_hbm, wd_hbm), (wg_ref, wu_ref, wd_ref), sems)

        @pl.when(f == 0)
        def _():
            xv = x_ref[...]
            h_ref[...] = ((xv * _rms_inv(xv)) * g_ref[...]).astype(BF16)
            acc_ref[...] = jnp.zeros_like(acc_ref)

        h = h_ref[...]
        for c0, size in _chunks(tf):
            rows = pl.ds(pl.multiple_of(f * tf + c0, LANES), size)
            gate = _dot_nt(h, wg_ref[rows, :])
            up = _dot_nt(h, wu_ref[rows, :])
            gate_ref[:, c0:c0 + size] = gate.astype(BF16)
            up_ref[:, c0:c0 + size] = up.astype(BF16)
            act_ref[:, c0:c0 + size] = (gate * jax.nn.sigmoid(gate) * up).astype(BF16)
        acc_ref[...] += _dot(act_ref[...], wd_ref[pl.ds(pl.multiple_of(f * tf, LANES), tf), :])

        @pl.when(f == F_SPLIT - 1)
        def _():
            xo_ref[...] = x_ref[...] + FFN_RES_SCALE * acc_ref[...]

    any_spec = pl.BlockSpec(memory_space=pl.ANY)
    return _pcall(
        body, name=name, grid=(T // tm, F_SPLIT),
        in_specs=[pl.BlockSpec((tm, D), lambda i, f: (i, 0)), pl.BlockSpec((1, D), lambda i, f: (0, 0)),
                  any_spec, any_spec, any_spec],
        out_specs=[pl.BlockSpec((tm, D), lambda i, f: (i, 0)), pl.BlockSpec((tm, D), lambda i, f: (i, 0)),
                   pl.BlockSpec((tm, tf), lambda i, f: (i, f)), pl.BlockSpec((tm, tf), lambda i, f: (i, f)),
                   pl.BlockSpec((tm, tf), lambda i, f: (i, f))],
        out_shape=[jax.ShapeDtypeStruct((T, D), F32), jax.ShapeDtypeStruct((T, D), BF16),
                   jax.ShapeDtypeStruct((T, F), BF16), jax.ShapeDtypeStruct((T, F), BF16),
                   jax.ShapeDtypeStruct((T, F), BF16)],
        scratch_shapes=[pltpu.VMEM((F, D), BF16), pltpu.VMEM((F, D), BF16), pltpu.VMEM((F, D), BF16),
                        pltpu.VMEM((tm, D), F32), pltpu.SemaphoreType.DMA((3,))],
        compiler_params=_params(2),
    )(x, g, wgt, wut, wd)


def _ffn_bwd(dy, x, g, gate, up, wgt, wut, wd, name):
    T, D = x.shape
    F = wgt.shape[0]
    tm = min(T, TOKEN_TILE)
    tf = F // F_SPLIT

    def body(dy_ref, x_ref, g_ref, gate_ref, up_ref, wg_hbm, wu_hbm, wd_hbm,
             dx_ref, dyb_ref, dgate_ref, dup_ref, dg_ref, wg_ref, wu_ref, wd_ref, dh_ref, sems):
        i = pl.program_id(0)
        f = pl.program_id(1)

        @pl.when(jnp.logical_and(i == 0, f == 0))
        def _():
            _load_weights((wg_hbm, wu_hbm, wd_hbm), (wg_ref, wu_ref, wd_ref), sems)
            dg_ref[...] = jnp.zeros_like(dg_ref)

        @pl.when(f == 0)
        def _():
            dyb_ref[...] = (FFN_RES_SCALE * dy_ref[...]).astype(BF16)
            dh_ref[...] = jnp.zeros_like(dh_ref)

        dyb = dyb_ref[...]
        for c0, size in _chunks(tf):
            rows = pl.ds(pl.multiple_of(f * tf + c0, LANES), size)
            dact = _dot_nt(dyb, wd_ref[rows, :])
            gt = gate_ref[:, c0:c0 + size].astype(F32)
            u = up_ref[:, c0:c0 + size].astype(F32)
            sig = jax.nn.sigmoid(gt)
            dup_ref[:, c0:c0 + size] = (dact * (gt * sig)).astype(BF16)
            dgate_ref[:, c0:c0 + size] = (dact * u * (sig * (1.0 + gt * (1.0 - sig)))).astype(BF16)
        rows = pl.ds(pl.multiple_of(f * tf, LANES), tf)
        dh_ref[...] += _dot(dgate_ref[...], wg_ref[rows, :]) + _dot(dup_ref[...], wu_ref[rows, :])

        @pl.when(f == F_SPLIT - 1)
        def _():
            dxn, dg = _norm_bwd(dh_ref[...], x_ref[...], g_ref[...])
            dx_ref[...] = dy_ref[...] + dxn
            dg_ref[...] += dg

    any_spec = pl.BlockSpec(memory_space=pl.ANY)
    return _pcall(
        body, name=name, grid=(T // tm, F_SPLIT),
        in_specs=[pl.BlockSpec((tm, D), lambda i, f: (i, 0)), pl.BlockSpec((tm, D), lambda i, f: (i, 0)),
                  pl.BlockSpec((1, D), lambda i, f: (0, 0)),
                  pl.BlockSpec((tm, tf), lambda i, f: (i, f)), pl.BlockSpec((tm, tf), lambda i, f: (i, f)),
                  any_spec, any_spec, any_spec],
        out_specs=[pl.BlockSpec((tm, D), lambda i, f: (i, 0)), pl.BlockSpec((tm, D), lambda i, f: (i, 0)),
                   pl.BlockSpec((tm, tf), lambda i, f: (i, f)), pl.BlockSpec((tm, tf), lambda i, f: (i, f)),
                   pl.BlockSpec((1, D), lambda i, f: (0, 0))],
        out_shape=[jax.ShapeDtypeStruct((T, D), F32), jax.ShapeDtypeStruct((T, D), BF16),
                   jax.ShapeDtypeStruct((T, F), BF16), jax.ShapeDtypeStruct((T, F), BF16),
                   jax.ShapeDtypeStruct((1, D), F32)],
        scratch_shapes=[pltpu.VMEM((F, D), BF16), pltpu.VMEM((F, D), BF16), pltpu.VMEM((F, D), BF16),
                        pltpu.VMEM((tm, D), F32), pltpu.SemaphoreType.DMA((3,))],
        compiler_params=_params(2),
    )(dy, x, g, gate, up, wgt, wut, wd)


def _matmul_tn(a, b, row_split, name):
    T, n1 = a.shape
    n2 = b.shape[1]
    tn = n1 // row_split
    tk = min(T, 1024)
    nk = T // tk

    def body(a_ref, b_ref, o_ref, ob_ref):
        k = pl.program_id(1)

        @pl.when(k == 0)
        def _():
            o_ref[...] = jnp.zeros_like(o_ref)

        o_ref[...] += _dot_tn(a_ref[...], b_ref[...])

        @pl.when(k == nk - 1)
        def _():
            ob_ref[...] = o_ref[...].astype(BF16)

    return _pcall(
        body, name=name, grid=(row_split, nk),
        in_specs=[pl.BlockSpec((tk, tn), lambda j, k: (k, j)), pl.BlockSpec((tk, n2), lambda j, k: (k, 0))],
        out_specs=[pl.BlockSpec((tn, n2), lambda j, k: (j, 0)), pl.BlockSpec((tn, n2), lambda j, k: (j, 0))],
        out_shape=[jax.ShapeDtypeStruct((n1, n2), F32), jax.ShapeDtypeStruct((n1, n2), BF16)],
        compiler_params=_params(2),
    )(a, b)


def _norm_matmul(x, g, wt, name):
    T, D = x.shape
    n = wt.shape[0]
    tm = min(T, TOKEN_TILE)

    def body(x_ref, g_ref, w_ref, z_ref, h_ref):
        xv = x_ref[...]
        h = ((xv * _rms_inv(xv)) * g_ref[...]).astype(BF16)
        h_ref[...] = h
        z_ref[...] = _dot_nt(h, w_ref[...])

    return _pcall(
        body, name=name, grid=(T // tm,),
        in_specs=[pl.BlockSpec((tm, D), lambda i: (i, 0)), pl.BlockSpec((1, D), lambda i: (0, 0)),
                  pl.BlockSpec((n, D), lambda i: (0, 0))],
        out_specs=[pl.BlockSpec((tm, n), lambda i: (i, 0)), pl.BlockSpec((tm, D), lambda i: (i, 0))],
        out_shape=[jax.ShapeDtypeStruct((T, n), F32), jax.ShapeDtypeStruct((T, D), BF16)],
        compiler_params=_params(1),
    )(x, g, wt)


def _matmul_residual(y, w, x, name):
    T, D = x.shape
    kdim = y.shape[1]
    tm = min(T, TOKEN_TILE)

    def body(y_ref, w_ref, x_ref, o_ref):
        o_ref[...] = x_ref[...] + _dot(y_ref[...], w_ref[...])

    return _pcall(
        body, name=name, grid=(T // tm,),
        in_specs=[pl.BlockSpec((tm, kdim), lambda i: (i, 0)), pl.BlockSpec((kdim, D), lambda i: (0, 0)),
                  pl.BlockSpec((tm, D), lambda i: (i, 0))],
        out_specs=pl.BlockSpec((tm, D), lambda i: (i, 0)),
        out_shape=jax.ShapeDtypeStruct((T, D), F32),
        compiler_params=_params(1),
    )(y, w, x)


def _matmul_nt(dx, w, name):
    T, D = dx.shape
    kdim = w.shape[0]
    tm = min(T, TOKEN_TILE)

    def body(dx_ref, w_ref, dy_ref, dxb_ref):
        dxb = dx_ref[...].astype(BF16)
        dxb_ref[...] = dxb
        dy_ref[...] = _dot_nt(dxb, w_ref[...])

    return _pcall(
        body, name=name, grid=(T // tm,),
        in_specs=[pl.BlockSpec((tm, D), lambda i: (i, 0)), pl.BlockSpec((kdim, D), lambda i: (0, 0))],
        out_specs=[pl.BlockSpec((tm, kdim), lambda i: (i, 0)), pl.BlockSpec((tm, D), lambda i: (i, 0))],
        out_shape=[jax.ShapeDtypeStruct((T, kdim), F32), jax.ShapeDtypeStruct((T, D), BF16)],
        compiler_params=_params(1),
    )(dx, w)


def _matmul_norm_bwd(dz, wt, x, g, dres, name):
    T, D = x.shape
    n = dz.shape[1]
    tm = min(T, TOKEN_TILE)

    def body(dz_ref, w_ref, x_ref, g_ref, dres_ref, dx_ref, dg_ref):
        @pl.when(pl.program_id(0) == 0)
        def _():
            dg_ref[...] = jnp.zeros_like(dg_ref)

        dh = _dot(dz_ref[...], w_ref[...])
        dxn, dg = _norm_bwd(dh, x_ref[...], g_ref[...])
        dx_ref[...] = dres_ref[...] + dxn
        dg_ref[...] += dg

    return _pcall(
        body, name=name, grid=(T // tm,),
        in_specs=[pl.BlockSpec((tm, n), lambda i: (i, 0)), pl.BlockSpec((n, D), lambda i: (0, 0)),
                  pl.BlockSpec((tm, D), lambda i: (i, 0)), pl.BlockSpec((1, D), lambda i: (0, 0)),
                  pl.BlockSpec((tm, D), lambda i: (i, 0))],
        out_specs=[pl.BlockSpec((tm, D), lambda i: (i, 0)), pl.BlockSpec((1, D), lambda i: (0, 0))],
        out_shape=[jax.ShapeDtypeStruct((T, D), F32), jax.ShapeDtypeStruct((1, D), F32)],
        compiler_params=_params(1),
    )(dz, wt, x, g, dres)


def _loss_head(x, g, target, name):
    T, D = x.shape
    tm = min(T, 512)

    def body(x_ref, g_ref, t_ref, dx_ref, dg_ref, loss_ref):
        @pl.when(pl.program_id(0) == 0)
        def _():
            dg_ref[...] = jnp.zeros_like(dg_ref)
            loss_ref[...] = jnp.zeros_like(loss_ref)

        xv = x_ref[...]
        gv = g_ref[...]
        out = (xv * _rms_inv(xv)) * gv
        diff = out - t_ref[...]
        loss_ref[...] += 0.5 * jnp.sum(jnp.mean(diff * diff, axis=-1, keepdims=True))
        dxn, dg = _norm_bwd(diff * (1.0 / D), xv, gv)
        dx_ref[...] = dxn
        dg_ref[...] += dg

    return _pcall(
        body, name=name, grid=(T // tm,),
        in_specs=[pl.BlockSpec((tm, D), lambda i: (i, 0)), pl.BlockSpec((1, D), lambda i: (0, 0)),
                  pl.BlockSpec((tm, D), lambda i: (i, 0))],
        out_specs=[pl.BlockSpec((tm, D), lambda i: (i, 0)), pl.BlockSpec((1, D), lambda i: (0, 0)),
                   pl.BlockSpec((1, LANES), lambda i: (0, 0))],
        out_shape=[jax.ShapeDtypeStruct((T, D), F32), jax.ShapeDtypeStruct((1, D), F32),
                   jax.ShapeDtypeStruct((1, LANES), F32)],
        compiler_params=_params(1),
    )(x, g, target)


Z_Q = 3 * CONV_WIDTH
Z_K = Z_Q + N_Q_HEADS * HEAD_DIM
Z_V = Z_K + LANES
Z_END = Z_V + LANES


def _rope_tables(T):
    half = ROT_DIM // 2
    inv_freq = ROPE_THETA ** (-jnp.arange(0, ROT_DIM, 2, dtype=F32) / ROT_DIM)
    ang = jnp.arange(T, dtype=F32)[:, None] * inv_freq[None, :]
    cos, sin = lax.optimization_barrier((jnp.cos(ang), jnp.sin(ang)))
    zero = jnp.zeros((T, half), F32)
    rest0 = jnp.zeros((T, HEAD_DIM - ROT_DIM), F32)
    c = jnp.concatenate([cos, cos, rest0 + 1.0], axis=1)
    s1 = jnp.concatenate([-sin, zero, rest0], axis=1)
    s2 = jnp.concatenate([zero, sin, rest0], axis=1)
    return jnp.concatenate([c, c, s1, s1, s2, s2], axis=1)


def _tab3(tab):
    return tab[:, 0:LANES], tab[:, LANES:2 * LANES], tab[:, 2 * LANES:3 * LANES]


def _rot(x, tab):
    c, s1, s2 = _tab3(tab)
    return x * c + pltpu.roll(x, LANES - ROT_DIM // 2, 1) * s1 + pltpu.roll(x, ROT_DIM // 2, 1) * s2


def _rot_t(d, tab):
    c, s1, s2 = _tab3(tab)
    return d * c + pltpu.roll(d * s1, ROT_DIM // 2, 1) + pltpu.roll(d * s2, LANES - ROT_DIM // 2, 1)


def _head_pads(a):
    lo = lax.broadcasted_iota(jnp.int32, a.shape, 1) < HEAD_DIM
    nat0 = jnp.where(lo, a, 0.0)
    nat1 = jnp.where(lo, 0.0, a)
    return {
        (0, 0): nat0.astype(BF16), (0, 1): pltpu.roll(nat0, HEAD_DIM, 1).astype(BF16),
        (1, 0): pltpu.roll(nat1, HEAD_DIM, 1).astype(BF16), (1, 1): nat1.astype(BF16),
    }


def _from_pads(even, odd, kv):
    lo = lax.broadcasted_iota(jnp.int32, even.shape, 1) < HEAD_DIM
    if kv == 0:
        return jnp.where(lo, even + pltpu.roll(odd, HEAD_DIM, 1), 0.0)
    return jnp.where(lo, 0.0, pltpu.roll(even, HEAD_DIM, 1) + odd)


def _window_mask(has_prev):
    ii = lax.broadcasted_iota(jnp.int32, (BLOCK, 2 * BLOCK), 0)
    jj = lax.broadcasted_iota(jnp.int32, (BLOCK, 2 * BLOCK), 1)
    rel = jj - BLOCK - ii
    return (rel <= 0) & (rel > -BLOCK) & ((jj >= BLOCK) | has_prev)


def _softmax_sink(q2, kpad, sink, mask):
    s = _dot_nt(q2, kpad) * ATTN_SCALE
    s = jnp.where(mask, s, MASK_VALUE)
    m = jnp.maximum(jnp.max(s, axis=-1, keepdims=True), sink)
    p = jnp.exp(s - m)
    esink = jnp.exp(sink - m)
    rden = 1.0 / (jnp.sum(p, axis=-1, keepdims=True) + esink)
    return p * rden, esink * rden


def _conv_taps(cg, u, cg_prev, u_prev, has_prev):
    vv = cg * u
    halo = jnp.where(has_prev, cg_prev * u_prev, 0.0)
    ext = jnp.concatenate([halo, vv], axis=0)
    rows = ext.shape[0]
    vv1 = pltpu.roll(ext, 1, 0)[8:rows]
    vv2 = pltpu.roll(ext, 2, 0)[8:rows]
    return vv, vv1, vv2


def _mix_specs(nb):
    cur = lambda n: jnp.minimum(n, nb - 1)
    prev = lambda n: jnp.maximum(jnp.minimum(n, nb - 1) - 1, 0)
    rows8_prev = lambda n: jnp.maximum(16 * jnp.minimum(n, nb - 1) - 1, 0)
    return cur, prev, [
        pl.BlockSpec((BLOCK, Z_END), lambda n: (cur(n), 0)),
        pl.BlockSpec((BLOCK, 2 * LANES), lambda n: (prev(n), Z_K // (2 * LANES))),
        pl.BlockSpec((8, CONV_WIDTH), lambda n: (rows8_prev(n), 1)),
        pl.BlockSpec((8, CONV_WIDTH), lambda n: (rows8_prev(n), 2)),
        pl.BlockSpec((BLOCK, 3 * LANES), lambda n: (cur(n), 0)),
        pl.BlockSpec((BLOCK, 3 * LANES), lambda n: (prev(n), 0)),
        pl.BlockSpec((3, CONV_WIDTH), lambda n: (0, 0)),
        pl.BlockSpec(memory_space=pltpu.SMEM),
    ]


def _mix_core_fwd(z, tab, conv_w, sinks, name):
    T = z.shape[0]
    nb = T // BLOCK
    _, _, specs = _mix_specs(nb)

    def body(z_ref, zkvp_ref, cgp_ref, up_ref, tab_ref, tabp_ref, cw_ref, sink_ref, y_ref):
        has_prev = pl.program_id(0) > 0
        bg = z_ref[:, 0:CONV_WIDTH]
        vv, vv1, vv2 = _conv_taps(z_ref[:, CONV_WIDTH:2 * CONV_WIDTH], z_ref[:, 2 * CONV_WIDTH:Z_Q],
                                  cgp_ref[...], up_ref[...], has_prev)
        conv = cw_ref[0:1, :] * vv2 + cw_ref[1:2, :] * vv1 + cw_ref[2:3, :] * vv
        y_ref[:, 0:CONV_WIDTH] = (bg * conv).astype(BF16)

        tab_c = tab_ref[...]
        tab_p = tabp_ref[...]
        k_all = jnp.concatenate([_rot(zkvp_ref[:, 0:LANES], tab_p), _rot(z_ref[:, Z_K:Z_V], tab_c)], axis=0)
        v_all = jnp.concatenate([zkvp_ref[:, LANES:2 * LANES], z_ref[:, Z_V:Z_END]], axis=0)
        kp = _head_pads(k_all)
        vp = _head_pads(v_all)
        mask = _window_mask(has_prev)
        for c in range(N_Q_HEADS // 2):
            kv = c // 2
            q2 = _rot(z_ref[:, Z_Q + LANES * c:Z_Q + LANES * (c + 1)], tab_c).astype(BF16)
            o = jnp.zeros((BLOCK, LANES), F32)
            for par in range(2):
                probs, _ = _softmax_sink(q2, kp[(kv, par)], sink_ref[0, 2 * c + par], mask)
                o = o + _dot(probs.astype(BF16), vp[(kv, par)])
            y_ref[:, CONV_WIDTH + LANES * c:CONV_WIDTH + LANES * (c + 1)] = o.astype(BF16)

    return _pcall(
        body, name=name, grid=(nb,), in_specs=specs,
        out_specs=pl.BlockSpec((BLOCK, 2 * CONV_WIDTH), lambda n: (n, 0)),
        out_shape=jax.ShapeDtypeStruct((T, 2 * CONV_WIDTH), BF16),
        compiler_params=_params(1),
    )(z, z, z, z, tab, tab, conv_w, sinks)


def _mix_core_bwd(z, dy, tab, conv_w, sinks, name):
    T = z.shape[0]
    nb = T // BLOCK
    cur, _, specs = _mix_specs(nb)
    rows8_next = lambda n: jnp.minimum(16 * (cur(n) + 1), 16 * nb - 1)
    specs = specs[:4] + [
        pl.BlockSpec((8, CONV_WIDTH), lambda n: (rows8_next(n), 0)),
        pl.BlockSpec((BLOCK, 2 * CONV_WIDTH), lambda n: (cur(n), 0)),
        pl.BlockSpec((8, CONV_WIDTH), lambda n: (rows8_next(n), 0)),
    ] + specs[4:]

    def body(z_ref, zkvp_ref, cgp_ref, up_ref, bgn_ref, dy_ref, dyn_ref, tab_ref, tabp_ref, cw_ref, sink_ref,
             dz_ref, dcw_ref, dsk_ref, main_ref, kv_ref):
        n = pl.program_id(0)

        @pl.when(n == 0)
        def _():
            main_ref[...] = jnp.zeros_like(main_ref)
            kv_ref[...] = jnp.zeros_like(kv_ref)
            dcw_ref[...] = jnp.zeros_like(dcw_ref)
            dsk_ref[...] = jnp.zeros_like(dsk_ref)

        @pl.when(n < nb)
        def _():
            has_prev = n > 0
            has_next = n < nb - 1
            bg = z_ref[:, 0:CONV_WIDTH]
            cg = z_ref[:, CONV_WIDTH:2 * CONV_WIDTH]
            u = z_ref[:, 2 * CONV_WIDTH:Z_Q]
            vv, vv1, vv2 = _conv_taps(cg, u, cgp_ref[...], up_ref[...], has_prev)
            w0, w1, w2 = cw_ref[0:1, :], cw_ref[1:2, :], cw_ref[2:3, :]
            dyc = dy_ref[:, 0:CONV_WIDTH]
            dbg = dyc * (w0 * vv2 + w1 * vv1 + w2 * vv)
            dconv = dyc * bg
            dconv_next = jnp.where(has_next, dyn_ref[...] * bgn_ref[...], 0.0)
            ext = jnp.concatenate([dconv, dconv_next], axis=0)
            rows = ext.shape[0]
            dvv = w2 * dconv + w1 * pltpu.roll(ext, rows - 1, 0)[0:BLOCK] + w0 * pltpu.roll(ext, rows - 2, 0)[0:BLOCK]
            dcw_ref[0:1, :] += jnp.sum(dconv * vv2, axis=0, keepdims=True)
            dcw_ref[1:2, :] += jnp.sum(dconv * vv1, axis=0, keepdims=True)
            dcw_ref[2:3, :] += jnp.sum(dconv * vv, axis=0, keepdims=True)

            tab_c = tab_ref[...]
            tab_p = tabp_ref[...]
            k_all = jnp.concatenate([_rot(zkvp_ref[:, 0:LANES], tab_p), _rot(z_ref[:, Z_K:Z_V], tab_c)], axis=0)
            v_all = jnp.concatenate([zkvp_ref[:, LANES:2 * LANES], z_ref[:, Z_V:Z_END]], axis=0)
            kp = _head_pads(k_all)
            vp = _head_pads(v_all)
            mask = _window_mask(has_prev)
            dq_chunks = []
            dk_nat = jnp.zeros((2 * BLOCK, LANES), F32)
            dv_nat = jnp.zeros((2 * BLOCK, LANES), F32)
            for kv in range(2):
                q2s, dos, pbs, dss = [], [], [[], []], [[], []]
                for c in (2 * kv, 2 * kv + 1):
                    q2 = _rot(z_ref[:, Z_Q + LANES * c:Z_Q + LANES * (c + 1)], tab_c).astype(BF16)
                    do2 = dy_ref[:, CONV_WIDTH + LANES * c:CONV_WIDTH + LANES * (c + 1)].astype(BF16)
                    dq = jnp.zeros((BLOCK, LANES), F32)
                    for par in range(2):
                        h = 2 * c + par
                        probs, psink = _softmax_sink(q2, kp[(kv, par)], sink_ref[0, h], mask)
                        dp = _dot_nt(do2, vp[(kv, par)])
                        delta = jnp.sum(dp * probs, axis=-1, keepdims=True)
                        ds = (probs * (dp - delta) * ATTN_SCALE).astype(BF16)
                        dsk_ref[h:h + 1, :] += jnp.sum(-psink * delta)
                        dq = dq + _dot(ds, kp[(kv, par)])
                        pbs[par].append(probs.astype(BF16))
                        dss[par].append(ds)
                    q2s.append(q2)
                    dos.append(do2)
                    dq_chunks.append(_rot_t(dq, tab_c))
                q_st = jnp.concatenate(q2s, axis=0)
                do_st = jnp.concatenate(dos, axis=0)
                dk_par = [_dot_tn(jnp.concatenate(dss[par], axis=0), q_st) for par in range(2)]
                dv_par = [_dot_tn(jnp.concatenate(pbs[par], axis=0), do_st) for par in range(2)]
                dk_nat = dk_nat + _from_pads(dk_par[0], dk_par[1], kv)
                dv_nat = dv_nat + _from_pads(dv_par[0], dv_par[1], kv)

            dk_prev = _rot_t(kv_ref[:, 0:LANES] + dk_nat[0:BLOCK], tab_p)
            dv_prev = kv_ref[:, LANES:2 * LANES] + dv_nat[0:BLOCK]
            dz_ref[:, 0:Z_K] = main_ref[...]
            dz_ref[:, Z_K:Z_V] = dk_prev.astype(BF16)
            dz_ref[:, Z_V:Z_END] = dv_prev.astype(BF16)
            main_ref[:, 0:CONV_WIDTH] = dbg.astype(BF16)
            main_ref[:, CONV_WIDTH:2 * CONV_WIDTH] = (dvv * u).astype(BF16)
            main_ref[:, 2 * CONV_WIDTH:Z_Q] = (dvv * cg).astype(BF16)
            for c in range(N_Q_HEADS // 2):
                main_ref[:, Z_Q + LANES * c:Z_Q + LANES * (c + 1)] = dq_chunks[c].astype(BF16)
            kv_ref[:, 0:LANES] = dk_nat[BLOCK:2 * BLOCK]
            kv_ref[:, LANES:2 * LANES] = dv_nat[BLOCK:2 * BLOCK]

        @pl.when(n == nb)
        def _():
            dz_ref[:, 0:Z_K] = main_ref[...]
            dz_ref[:, Z_K:Z_V] = _rot_t(kv_ref[:, 0:LANES], tab_ref[...]).astype(BF16)
            dz_ref[:, Z_V:Z_END] = kv_ref[:, LANES:2 * LANES].astype(BF16)

    return _pcall(
        body, name=name, grid=(nb + 1,), in_specs=specs,
        out_specs=[pl.BlockSpec((BLOCK, Z_END), lambda n: (jnp.maximum(n - 1, 0), 0)),
                   pl.BlockSpec((8, CONV_WIDTH), lambda n: (0, 0)),
                   pl.BlockSpec((8, LANES), lambda n: (0, 0))],
        out_shape=[jax.ShapeDtypeStruct((T, Z_END), BF16), jax.ShapeDtypeStruct((8, CONV_WIDTH), F32),
                   jax.ShapeDtypeStruct((8, LANES), F32)],
        scratch_shapes=[pltpu.VMEM((BLOCK, Z_K), BF16), pltpu.VMEM((BLOCK, 2 * LANES), F32)],
        compiler_params=_params(1),
    )(z, z, z, z, z, dy, dy, tab, tab, conv_w, sinks)


HBM_SPEC = pl.BlockSpec(memory_space=pltpu.HBM)


def _place():
    x, y, c = lax.axis_index("x"), lax.axis_index("y"), lax.axis_index("c")
    chips = [(1 - x, y), (x, 1 - y), (1 - x, 1 - y)]
    return x, y, c, chips


def _gather_weights(shards, conv_shard, name):
    n = len(shards)

    def body(*refs):
        ins, conv_in = refs[:n], refs[n]
        outs, conv_out = refs[n + 1:2 * n + 1], refs[2 * n + 1]
        send_a, recv_a, send_b, recv_b, send_c, recv_c = refs[2 * n + 2:]
        x, y, c, chips = _place()
        me = 2 * x + y

        def rows(t, core):
            half = ins[t].shape[0] // 2
            return pl.ds(pl.multiple_of(core * half, 16), half)

        def first(t, j, block, core):
            return pltpu.make_async_remote_copy(
                src_ref=ins[t].at[rows(t, core), :], dst_ref=outs[t].at[block, rows(t, core), :],
                send_sem=send_a.at[3 * t + j], recv_sem=recv_a.at[3 * t + j],
                device_id=(*chips[j], c), device_id_type=MESH)

        def passed(t, j, block, core):
            ref = outs[t].at[block, rows(t, core), :]
            return pltpu.make_async_remote_copy(
                src_ref=ref, dst_ref=ref, send_sem=send_b.at[3 * t + j], recv_sem=recv_b.at[3 * t + j],
                device_id=(x, y, 1 - c), device_id_type=MESH)

        def conv(j, block):
            return pltpu.make_async_remote_copy(
                src_ref=conv_in, dst_ref=conv_out.at[block], send_sem=send_c.at[j], recv_sem=recv_c.at[j],
                device_id=(*chips[j], c), device_id_type=MESH)

        sends = [first(t, j, me, c) for t in range(n) for j in range(3)] + [conv(j, me) for j in range(3)]
        for cp in sends:
            cp.start()
        for t in range(n):
            for j, (px, py) in enumerate(chips):
                first(t, j, 2 * px + py, c).wait_recv()
                fwd = passed(t, j, 2 * px + py, c)
                fwd.start()
                sends.append(fwd)
        for t in range(n):
            for j, (px, py) in enumerate(chips):
                passed(t, j, 2 * px + py, 1 - c).wait_recv()
        for j, (px, py) in enumerate(chips):
            conv(j, 2 * px + py).wait_recv()
        for cp in sends:
            cp.wait_send()

    return _pcall(
        body, name=name, in_specs=[HBM_SPEC] * (n + 1), out_specs=[HBM_SPEC] * (n + 1),
        out_shape=[jax.ShapeDtypeStruct((N_CHIPS, *s.shape), s.dtype) for s in (*shards, conv_shard)],
        scratch_shapes=[pltpu.SemaphoreType.DMA((3 * n,)), pltpu.SemaphoreType.DMA((3 * n,)),
                        pltpu.SemaphoreType.DMA((3 * n,)), pltpu.SemaphoreType.DMA((3 * n,)),
                        pltpu.SemaphoreType.DMA((3,)), pltpu.SemaphoreType.DMA((3,))],
    )(*shards, conv_shard)


def _sibling_halves(grads_b, name):
    n = len(grads_b)

    def body(*refs):
        ins, outs = refs[:n], refs[n:2 * n]
        send_sems, recv_sems = refs[2 * n:]
        x, y, c, _ = _place()

        def copy(t):
            half = ins[t].shape[1] // 2
            return pltpu.make_async_remote_copy(
                src_ref=ins[t].at[:, pl.ds((1 - c) * half, half), :], dst_ref=outs[t],
                send_sem=send_sems.at[t], recv_sem=recv_sems.at[t], device_id=(x, y, 1 - c), device_id_type=MESH)

        cps = [copy(t) for t in range(n)]
        for cp in cps:
            cp.start()
        for cp in cps:
            cp.wait()

    return _pcall(
        body, name=name, in_specs=[HBM_SPEC] * n, out_specs=[HBM_SPEC] * n,
        out_shape=[jax.ShapeDtypeStruct((g.shape[0], g.shape[1] // 2, g.shape[2]), g.dtype) for g in grads_b],
        scratch_shapes=[pltpu.SemaphoreType.DMA((n,)), pltpu.SemaphoreType.DMA((n,))],
    )(*grads_b)


def _scatter_chips(parts_b, name):
    n = len(parts_b)

    def body(*refs):
        ins, outs = refs[:n], refs[n:2 * n]
        send_sems, recv_sems = refs[2 * n:]
        x, y, c, chips = _place()

        def copy(t, j):
            px, py = chips[j]
            return pltpu.make_async_remote_copy(
                src_ref=ins[t].at[2 * px + py], dst_ref=outs[t].at[j], send_sem=send_sems.at[3 * t + j],
                recv_sem=recv_sems.at[3 * t + j], device_id=(px, py, c), device_id_type=MESH)

        cps = [copy(t, j) for t in range(n) for j in range(3)]
        for cp in cps:
            cp.start()
        for cp in cps:
            cp.wait()

    return _pcall(
        body, name=name, in_specs=[HBM_SPEC] * n, out_specs=[HBM_SPEC] * n,
        out_shape=[jax.ShapeDtypeStruct((3, *p.shape[1:]), p.dtype) for p in parts_b],
        scratch_shapes=[pltpu.SemaphoreType.DMA((3 * n,)), pltpu.SemaphoreType.DMA((3 * n,))],
    )(*parts_b)


def _join_halves(shards, name):
    n = len(shards)

    def body(*refs):
        ins, outs = refs[:n], refs[n:2 * n]
        send_sems, recv_sems = refs[2 * n:]
        x, y, c, _ = _place()

        def copy(t, core):
            half = ins[t].shape[0] // 2
            rows = pl.ds(pl.multiple_of(core * half, 8), half)
            return pltpu.make_async_remote_copy(
                src_ref=ins[t].at[rows, :], dst_ref=outs[t].at[rows, :], send_sem=send_sems.at[t],
                recv_sem=recv_sems.at[t], device_id=(x, y, 1 - c), device_id_type=MESH)

        sends = [copy(t, c) for t in range(n)]
        for cp in sends:
            cp.start()
        for t in range(n):
            copy(t, 1 - c).wait_recv()
        for cp in sends:
            cp.wait_send()

    return _pcall(
        body, name=name, in_specs=[HBM_SPEC] * n, out_specs=[HBM_SPEC] * n,
        out_shape=[jax.ShapeDtypeStruct(s.shape, s.dtype) for s in shards],
        input_output_aliases={t: t for t in range(n)},
        scratch_shapes=[pltpu.SemaphoreType.DMA((n,)), pltpu.SemaphoreType.DMA((n,))],
    )(*shards)


def _all_reduce_small(vec, name):
    s = vec.shape[1]

    def body(v_ref, o_ref, all_ref, send_sems, recv_sems, local_sem):
        x, y, c, _ = _place()
        me = 4 * x + 2 * y + c

        def rows(dev):
            return all_ref.at[pl.ds(pl.multiple_of(8 * dev, 8), 8), :]

        mine = pltpu.make_async_copy(v_ref, rows(me), local_sem)
        mine.start()
        rel = [((k >> 2) & 1, (k >> 1) & 1, k & 1) for k in range(1, N_DEV)]

        def peer(k):
            fx, fy, fc = rel[k]
            return (x ^ fx, y ^ fy, c ^ fc)

        def copy(k, dev):
            return pltpu.make_async_remote_copy(
                src_ref=v_ref, dst_ref=rows(dev), send_sem=send_sems.at[k], recv_sem=recv_sems.at[k],
                device_id=peer(k), device_id_type=MESH)

        sends = [copy(k, me) for k in range(N_DEV - 1)]
        for cp in sends:
            cp.start()
        for k in range(N_DEV - 1):
            px, py, pc = peer(k)
            copy(k, 4 * px + 2 * py + pc).wait_recv()
        for cp in sends:
            cp.wait_send()
        mine.wait()
        total = all_ref[0:8, :]
        for dev in range(1, N_DEV):
            total = total + all_ref[8 * dev:8 * dev + 8, :]
        o_ref[...] = total

    return _pcall(
        body, name=name,
        in_specs=[pl.BlockSpec(memory_space=pltpu.VMEM)], out_specs=pl.BlockSpec(memory_space=pltpu.VMEM),
        out_shape=jax.ShapeDtypeStruct((8, s), F32),
        scratch_shapes=[pltpu.VMEM((8 * N_DEV, s), F32), pltpu.SemaphoreType.DMA((N_DEV - 1,)),
                        pltpu.SemaphoreType.DMA((N_DEV - 1,)), pltpu.SemaphoreType.DMA],
    )(vec)


ROW_SPLIT = 2


def _pair_sum(grads, recvd, core, name):
    n = len(grads)

    def body(core_ref, *refs):
        g, r = refs[:n], refs[n:2 * n]
        s, sb = refs[2 * n:3 * n], refs[3 * n:]
        for t in range(n):
            tot = g[t][...] + r[t][...].astype(F32)
            s[t][...] = tot
            sb[t][...] = tot.astype(BF16)

    def blk(a):
        return (1, a.shape[1] // ROW_SPLIT, a.shape[2])

    in_specs = [pl.BlockSpec(blk(r), lambda q, i, core_ref: (q, core_ref[0] * ROW_SPLIT + i, 0)) for r in recvd]
    in_specs += [pl.BlockSpec(blk(r), lambda q, i, core_ref: (q, i, 0)) for r in recvd]
    out_specs = [pl.BlockSpec(blk(r), lambda q, i, core_ref: (q, i, 0)) for r in recvd] * 2
    return _pcall(
        body, name=name,
        grid_spec=pltpu.PrefetchScalarGridSpec(num_scalar_prefetch=1, grid=(N_CHIPS, ROW_SPLIT),
                                               in_specs=in_specs, out_specs=out_specs),
        out_shape=[jax.ShapeDtypeStruct(r.shape, F32) for r in recvd] + [jax.ShapeDtypeStruct(r.shape, BF16) for r in recvd],
        compiler_params=_params(2),
    )(core, *grads, *recvd)


def _chip_sum(parts, recvd, place, name):
    n = len(parts)

    def body(place_ref, *refs):
        p, r, o = refs[:n], refs[n:2 * n], refs[2 * n:]
        for t in range(n):
            tot = p[t][0]
            for j in range(3):
                tot = tot + r[t][j].astype(F32)
            o[t][...] = tot

    in_specs = [pl.BlockSpec((1, p.shape[1] // ROW_SPLIT, p.shape[2]), lambda i, place_ref: (place_ref[0], i, 0))
                for p in parts]
    in_specs += [pl.BlockSpec((3, r.shape[1] // ROW_SPLIT, r.shape[2]), lambda i, place_ref: (0, i, 0)) for r in recvd]
    out_specs = [pl.BlockSpec((p.shape[1] // ROW_SPLIT, p.shape[2]),
                              lambda i, place_ref: (place_ref[1] * ROW_SPLIT + i, 0)) for p in parts]
    return _pcall(
        body, name=name,
        grid_spec=pltpu.PrefetchScalarGridSpec(num_scalar_prefetch=1, grid=(ROW_SPLIT,),
                                               in_specs=in_specs, out_specs=out_specs),
        out_shape=[jax.ShapeDtypeStruct((2 * p.shape[1], p.shape[2]), F32) for p in parts],
        compiler_params=_params(1),
    )(place, *parts, *recvd)


def _adamw_math(w, g, m, v):
    m = ADAM_B1 * m + (1.0 - ADAM_B1) * g
    v = ADAM_B2 * v + (1.0 - ADAM_B2) * (g * g)
    m_hat = m / (1.0 - ADAM_B1 ** ADAM_STEP)
    v_hat = v / (1.0 - ADAM_B2 ** ADAM_STEP)
    delta = -ADAM_LR * (m_hat / (jnp.sqrt(v_hat) + ADAM_EPS) + ADAM_WD * w)
    return delta, m, v


def _adamw(ws, gs, ms, vs, row_blocks, name):
    n = len(ws)

    def body(*refs):
        w, g, m, v = refs[:n], refs[n:2 * n], refs[2 * n:3 * n], refs[3 * n:4 * n]
        d, mo, vo = refs[4 * n:5 * n], refs[5 * n:6 * n], refs[6 * n:]
        for t in range(n):
            delta, m_new, v_new = _adamw_math(w[t][...], g[t][...], m[t][...], v[t][...])
            d[t][...] = delta
            mo[t][...] = m_new
            vo[t][...] = v_new

    specs = [pl.BlockSpec((a.shape[0] // row_blocks, a.shape[1]), lambda i: (i, 0)) for a in ws]
    shapes = [jax.ShapeDtypeStruct(a.shape, F32) for a in ws]
    return _pcall(
        body, name=name, grid=(row_blocks,), in_specs=specs * 4, out_specs=specs * 3, out_shape=shapes * 3,
        compiler_params=_params(1),
    )(*ws, *gs, *ms, *vs)


def kernel(x, ffn1_norm, ffn1_w_gate, ffn1_w_up, ffn1_w_down, mix_norm, w_in, conv_w, attn_sinks, w_out, ffn2_norm, ffn2_w_gate, ffn2_w_up, ffn2_w_down, final_norm, loss_target, m_ffn1_norm, m_ffn1_w_gate, m_ffn1_w_up, m_ffn1_w_down, m_mix_norm, m_w_in, m_conv_w, m_attn_sinks, m_w_out, m_ffn2_norm, m_ffn2_w_gate, m_ffn2_w_up, m_ffn2_w_down, m_final_norm, v_ffn1_norm, v_ffn1_w_gate, v_ffn1_w_up, v_ffn1_w_down, v_mix_norm, v_w_in, v_conv_w, v_attn_sinks, v_w_out, v_ffn2_norm, v_ffn2_w_gate, v_ffn2_w_up, v_ffn2_w_down, v_final_norm):
    T, D = x.shape[1], x.shape[2]
    chip = (2 * lax.axis_index("x") + lax.axis_index("y")).astype(jnp.int32)
    core = lax.axis_index("c").astype(jnp.int32)
    x0 = x[0]
    target = loss_target[0]
    gf = final_norm.reshape(1, D)

    tr = lambda w: jnp.swapaxes(w[0], 0, 1)
    big = [tr(ffn1_w_gate), tr(ffn1_w_up), ffn1_w_down[0], tr(w_in), w_out[0], tr(ffn2_w_gate), tr(ffn2_w_up), ffn2_w_down[0]]
    transposed = [True, True, False, True, False, True, True, False]
    own_b = [w.astype(BF16) for w in big]
    gathered = _gather_weights(own_b, conv_w[0], "gather_weights")
    full = [lax.dynamic_update_slice(g, o[None], (chip, 0, 0)).reshape(-1, D) for g, o in zip(gathered[:-1], own_b)]
    wg1, wu1, wd1, win, wout, wg2, wu2, wd2 = full
    convw4 = lax.dynamic_update_slice(gathered[-1], conv_w, (chip, 0, 0))
    convw = jnp.transpose(convw4, (1, 0, 2)).reshape(3, -1)
    tab = _rope_tables(T)

    x1, h1, gate1, up1, act1 = _ffn_fwd(x0, ffn1_norm, wg1, wu1, wd1, "ffn1_fwd")
    z, hm = _norm_matmul(x1, mix_norm, win, "mix_in_fwd")
    ymix = _mix_core_fwd(z, tab, convw, attn_sinks, "mix_core_fwd")
    x2 = _matmul_residual(ymix, wout, x1, "mix_out_fwd")
    x3, h2, gate2, up2, act2 = _ffn_fwd(x2, ffn2_norm, wg2, wu2, wd2, "ffn2_fwd")
    dx3, dgf, loss_part = _loss_head(x3, gf, target, "loss_head")

    dx2, dyb2, dgate2, dup2, dg2 = _ffn_bwd(dx3, x2, ffn2_norm, gate2, up2, wg2, wu2, wd2, "ffn2_bwd")
    gwg2 = _matmul_tn(dgate2, h2, F_SPLIT, "ffn2_dwg")
    gwu2 = _matmul_tn(dup2, h2, F_SPLIT, "ffn2_dwu")
    gwd2 = _matmul_tn(act2, dyb2, F_SPLIT, "ffn2_dwd")
    dymix, dx2b = _matmul_nt(dx2, wout, "mix_out_bwd")
    gwout = _matmul_tn(ymix, dx2b, F_SPLIT, "mix_dwout")
    dz, dcw, dsk = _mix_core_bwd(z, dymix, tab, convw, attn_sinks, "mix_core_bwd")
    gwin = _matmul_tn(dz, hm, F_SPLIT, "mix_dwin")
    dx1, dgm = _matmul_norm_bwd(dz, win, x1, mix_norm, dx2, "mix_in_bwd")
    dx0, dyb1, dgate1, dup1, dg1 = _ffn_bwd(dx1, x0, ffn1_norm, gate1, up1, wg1, wu1, wd1, "ffn1_bwd")
    gwg1 = _matmul_tn(dgate1, h1, F_SPLIT, "ffn1_dwg")
    gwu1 = _matmul_tn(dup1, h1, F_SPLIT, "ffn1_dwu")
    gwd1 = _matmul_tn(act1, dyb1, F_SPLIT, "ffn1_dwd")

    both = [gwg1, gwu1, gwd1, gwin, gwout, gwg2, gwu2, gwd2]
    grads = [g[0].reshape(N_CHIPS, -1, D) for g in both]
    grads_b = [g[1].reshape(N_CHIPS, -1, D) for g in both]
    from_sibling = _sibling_halves(grads_b, "grads_to_sibling")
    pair = _pair_sum(grads, from_sibling, core.reshape(1), "grads_pair_sum")
    pair_f, pair_b = pair[:len(grads)], pair[len(grads):]
    from_chips = _scatter_chips(pair_b, "grads_to_chips")
    mine_half = _chip_sum(pair_f, from_chips, jnp.stack([chip, core]), "grads_chip_sum")
    g_big = _join_halves(mine_half, "grads_join")

    pad = lambda a: jnp.pad(a, ((0, 0), (0, LANES - a.shape[1])))
    vec = jnp.concatenate([dg1, dgm, dg2, dgf, dcw[0:3].reshape(1, -1), pad(dsk[:, 0].reshape(1, -1)),
                           pad(loss_part[:, 0:1])], axis=1)
    total = _all_reduce_small(jnp.pad(vec, ((0, 7), (0, 0))), "small_all_reduce")[0:1]
    g_n1, g_nm, g_n2, g_nf = (total[:, k * D:(k + 1) * D] for k in range(4))
    cw_full = total[:, 4 * D:4 * D + 3 * CONV_WIDTH].reshape(3, CONV_WIDTH)
    cq = CONV_WIDTH // N_CHIPS
    g_cw = lax.dynamic_slice(cw_full, (0, chip * cq), (3, cq))
    off = 4 * D + 3 * CONV_WIDTH
    g_sk = total[:, off:off + N_Q_HEADS]
    loss = total[0, off + LANES]

    ws = big
    ms = [tr(m_ffn1_w_gate), tr(m_ffn1_w_up), m_ffn1_w_down[0], tr(m_w_in), m_w_out[0], tr(m_ffn2_w_gate), tr(m_ffn2_w_up), m_ffn2_w_down[0]]
    vs = [tr(v_ffn1_w_gate), tr(v_ffn1_w_up), v_ffn1_w_down[0], tr(v_w_in), v_w_out[0], tr(v_ffn2_w_gate), tr(v_ffn2_w_up), v_ffn2_w_down[0]]
    upd = {}
    for name_, idx in (("adamw_a", [0, 1, 2, 4]), ("adamw_b", [3, 5, 6, 7])):
        res = _adamw([ws[i] for i in idx], [g_big[i] for i in idx], [ms[i] for i in idx], [vs[i] for i in idx], 8, name_)
        k = len(idx)
        for j, i in enumerate(idx):
            upd[i] = (res[j], res[k + j], res[2 * k + j])
    sw = [ffn1_norm, mix_norm, conv_w[0], attn_sinks, ffn2_norm, gf]
    sg = [g_n1, g_nm, g_cw, g_sk, g_n2, g_nf]
    sm = [m_ffn1_norm, m_mix_norm, m_conv_w[0], m_attn_sinks, m_ffn2_norm, m_final_norm.reshape(1, D)]
    sv = [v_ffn1_norm, v_mix_norm, v_conv_w[0], v_attn_sinks, v_ffn2_norm, v_final_norm.reshape(1, D)]
    sres = _adamw(sw, sg, sm, sv, 1, "adamw_small")
    supd = [(sres[j], sres[6 + j], sres[12 + j]) for j in range(6)]

    order = [("s", 0), ("b", 0), ("b", 1), ("b", 2), ("s", 1), ("b", 3), ("s", 2), ("s", 3), ("b", 4),
             ("s", 4), ("b", 5), ("b", 6), ("b", 7), ("s", 5)]

    def leaf(kind, i, which):
        if kind == "b":
            a = g_big[i] if which == 0 else upd[i][which - 1]
            return (jnp.swapaxes(a, 0, 1) if transposed[i] else a)[None]
        a = sg[i] if which == 0 else supd[i][which - 1]
        if i == 2:
            return a[None]
        if i == 5:
            return a.reshape(D)
        return a

    outs = [loss, dx0[None]]
    for which in range(4):
        outs += [leaf(kind, i, which) for kind, i in order]
    return tuple(outs)
```

```python
import functools

import jax
import jax.numpy as jnp
from jax import lax
from jax.experimental import pallas as pl
from jax.experimental.pallas import tpu as pltpu

F32 = jnp.float32
BF16 = jnp.bfloat16
MESH = pl.DeviceIdType.MESH

CONV_WIDTH = 512
N_Q_HEADS = 8
HEAD_DIM = 64
BLOCK = 128
ROPE_THETA = 500000.0
ROT_DIM = 16
RMS_EPS = 1e-5
MASK_VALUE = -1e30
ATTN_SCALE = HEAD_DIM ** -0.5
FFN_RES_SCALE = 0.5
ADAM_LR = 0.001
ADAM_B1 = 0.9
ADAM_B2 = 0.999
ADAM_EPS = 1e-08
ADAM_WD = 0.01
ADAM_STEP = 10

N_CHIPS = 4
N_DEV = 8
LANES = 128
VMEM_LIMIT = 56 * 1024 * 1024

_pcall = pl.pallas_call
HBM_SPEC = pl.BlockSpec(memory_space=pltpu.HBM)
ANY_SPEC = pl.BlockSpec(memory_space=pl.ANY)


def _params(n_axes, vmem=VMEM_LIMIT):
    return pltpu.CompilerParams(dimension_semantics=("arbitrary",) * n_axes, vmem_limit_bytes=vmem)


def _dot(a, b):
    return jnp.dot(a, b, preferred_element_type=F32)


def _dot_nt(a, b):
    return lax.dot_general(a, b, (((1,), (1,)), ((), ())), preferred_element_type=F32)


def _dot_tn(a, b):
    return lax.dot_general(a, b, (((0,), (0,)), ((), ())), preferred_element_type=F32)


def _rms_inv(x):
    return lax.rsqrt(jnp.mean(x * x, axis=-1, keepdims=True) + RMS_EPS)


def _norm_bwd(dh, x, g):
    inv = _rms_inv(x)
    xhat = x * inv
    dg = jnp.sum(dh * xhat, axis=0, keepdims=True)
    dxhat = dh * g
    dx = inv * (dxhat - xhat * jnp.mean(dxhat * xhat, axis=-1, keepdims=True))
    return dx, dg


def _place():
    x, y, c = lax.axis_index("x"), lax.axis_index("y"), lax.axis_index("c")
    chips = [(1 - x, y), (x, 1 - y), (1 - x, 1 - y)]
    return x, y, c, chips


class _Plan:
    def __init__(self, arrays, out_shapes, n_sems, start, finish, middle=None):
        self.arrays, self.out_shapes, self.n_sems = list(arrays), list(out_shapes), n_sems
        self.start, self.finish, self.middle = start, finish, middle

    def specs(self):
        k = len(self.arrays)
        sems = [pltpu.SemaphoreType.DMA((self.n_sems,)), pltpu.SemaphoreType.DMA((self.n_sems,))]
        return [HBM_SPEC] * k, [HBM_SPEC] * len(self.out_shapes), self.out_shapes, sems


def _sibling_plan(grads_b):
    n = len(grads_b)

    def copies(ins, outs, send_sems, recv_sems):
        x, y, c, _ = _place()

        def copy(t):
            half = ins[t].shape[1] // 2
            return pltpu.make_async_remote_copy(
                src_ref=ins[t].at[:, pl.ds(pl.multiple_of((1 - c) * half, 16), half), :], dst_ref=outs[t],
                send_sem=send_sems.at[t], recv_sem=recv_sems.at[t], device_id=(x, y, 1 - c), device_id_type=MESH)

        return [copy(t) for t in range(n)]

    def start(*refs):
        for cp in copies(*refs):
            cp.start()

    def finish(*refs):
        for cp in copies(*refs):
            cp.wait()

    shapes = [jax.ShapeDtypeStruct((g.shape[0], g.shape[1] // 2, g.shape[2]), g.dtype) for g in grads_b]
    return _Plan(grads_b, shapes, n, start, finish)


def _scatter_plan(parts_b):
    n = len(parts_b)

    def copies(ins, outs, send_sems, recv_sems):
        x, y, c, chips = _place()

        def copy(t, j):
            px, py = chips[j]
            return pltpu.make_async_remote_copy(
                src_ref=ins[t].at[2 * px + py], dst_ref=outs[t].at[j], send_sem=send_sems.at[3 * t + j],
                recv_sem=recv_sems.at[3 * t + j], device_id=(px, py, c), device_id_type=MESH)

        return [copy(t, j) for t in range(n) for j in range(3)]

    def start(*refs):
        for cp in copies(*refs):
            cp.start()

    def finish(*refs):
        for cp in copies(*refs):
            cp.wait()

    shapes = [jax.ShapeDtypeStruct((3, *p.shape[1:]), p.dtype) for p in parts_b]
    return _Plan(parts_b, shapes, 3 * n, start, finish)


def _gather_plan(shards, small=()):
    n, ns = len(shards), len(small)

    def parts(ins, outs, send_sems, recv_sems):
        x, y, c, chips = _place()
        me = 2 * x + y

        def rows(t, core):
            half = ins[t].shape[0] // 2
            return pl.ds(pl.multiple_of(core * half, 16), half)

        def first(t, j, block, core):
            return pltpu.make_async_remote_copy(
                src_ref=ins[t].at[rows(t, core), :], dst_ref=outs[t].at[block, rows(t, core), :],
                send_sem=send_sems.at[6 * t + j], recv_sem=recv_sems.at[6 * t + j],
                device_id=(*chips[j], c), device_id_type=MESH)

        def passed(t, j, block, core):
            ref = outs[t].at[block, rows(t, core), :]
            return pltpu.make_async_remote_copy(
                src_ref=ref, dst_ref=ref, send_sem=send_sems.at[6 * t + 3 + j], recv_sem=recv_sems.at[6 * t + 3 + j],
                device_id=(x, y, 1 - c), device_id_type=MESH)

        def whole(s, j, block):
            k = 6 * n + 3 * s + j
            return pltpu.make_async_remote_copy(
                src_ref=ins[n + s], dst_ref=outs[n + s].at[block], send_sem=send_sems.at[k], recv_sem=recv_sems.at[k],
                device_id=(*chips[j], c), device_id_type=MESH)

        blocks = [2 * px + py for px, py in chips]
        return c, me, blocks, first, passed, whole

    def start(*refs):
        c, me, _, first, _, whole = parts(*refs)
        for t in range(n):
            for j in range(3):
                first(t, j, me, c).start()
        for s in range(ns):
            for j in range(3):
                whole(s, j, me).start()

    def middle(*refs):
        c, _, blocks, first, passed, _ = parts(*refs)
        for t in range(n):
            for j in range(3):
                first(t, j, blocks[j], c).wait_recv()
                passed(t, j, blocks[j], c).start()

    def finish(*refs):
        c, me, blocks, first, passed, whole = parts(*refs)
        for t in range(n):
            for j in range(3):
                passed(t, j, blocks[j], 1 - c).wait_recv()
        for s in range(ns):
            for j in range(3):
                whole(s, j, blocks[j]).wait_recv()
        for t in range(n):
            for j in range(3):
                first(t, j, me, c).wait_send()
                passed(t, j, blocks[j], c).wait_send()
        for s in range(ns):
            for j in range(3):
                whole(s, j, me).wait_send()

    arrays = [*shards, *small]
    shapes = [jax.ShapeDtypeStruct((N_CHIPS, *a.shape), a.dtype) for a in arrays]
    return _Plan(arrays, shapes, 6 * n + 3 * ns, start, finish, middle)


def _run_comm(plan, name):
    k = len(plan.arrays)
    in_specs, out_specs, out_shape, sems = plan.specs()

    def body(*refs):
        cr = (refs[:k], refs[k:k + len(out_shape)], refs[-2], refs[-1])
        plan.start(*cr)
        if plan.middle is not None:
            plan.middle(*cr)
        plan.finish(*cr)

    return _pcall(body, name=name, in_specs=in_specs, out_specs=out_specs, out_shape=out_shape,
                  scratch_shapes=sems)(*plan.arrays)


def _carried(plan, in_specs, out_specs, out_shape, scratch):
    if plan is None:
        return in_specs, out_specs, out_shape, scratch
    p_in, p_out, p_shape, p_sems = plan.specs()
    return in_specs + p_in, out_specs + p_out, out_shape + p_shape, scratch + p_sems


def _unpack(refs, n_in, n_out, plan):
    k_in = len(plan.arrays) if plan else 0
    k_out = len(plan.out_shapes) if plan else 0
    ins = refs[:n_in]
    outs = refs[n_in + k_in:n_in + k_in + n_out]
    rest = refs[n_in + k_in + n_out + k_out:]
    if plan is None:
        return ins, outs, rest, None
    cr = (refs[n_in:n_in + k_in], refs[n_in + k_in + n_out:n_in + k_in + n_out + k_out], rest[-2], rest[-1])
    return ins, outs, rest[:-2], cr


def _hook(plan, cr, which, cond):
    fn = getattr(plan, which) if plan is not None else None
    if fn is not None:
        pl.when(cond)(lambda: fn(*cr))


def _join_halves(shards, name):
    n = len(shards)

    def body(*refs):
        ins, outs = refs[:n], refs[n:2 * n]
        send_sems, recv_sems = refs[2 * n:]
        x, y, c, _ = _place()

        def copy(t, core):
            half = ins[t].shape[0] // 2
            rows = pl.ds(pl.multiple_of(core * half, 8), half)
            return pltpu.make_async_remote_copy(
                src_ref=ins[t].at[rows, :], dst_ref=outs[t].at[rows, :], send_sem=send_sems.at[t],
                recv_sem=recv_sems.at[t], device_id=(x, y, 1 - c), device_id_type=MESH)

        sends = [copy(t, c) for t in range(n)]
        for cp in sends:
            cp.start()
        for t in range(n):
            copy(t, 1 - c).wait_recv()
        for cp in sends:
            cp.wait_send()

    return _pcall(
        body, name=name, in_specs=[HBM_SPEC] * n, out_specs=[HBM_SPEC] * n,
        out_shape=[jax.ShapeDtypeStruct(s.shape, s.dtype) for s in shards],
        input_output_aliases={t: t for t in range(n)},
        scratch_shapes=[pltpu.SemaphoreType.DMA((n,)), pltpu.SemaphoreType.DMA((n,))],
    )(*shards)


def _all_reduce_small(vec, name):
    s = vec.shape[1]

    def body(v_ref, o_ref, all_ref, send_sems, recv_sems, local_sem):
        x, y, c, _ = _place()
        me = 4 * x + 2 * y + c

        def rows(dev):
            return all_ref.at[pl.ds(pl.multiple_of(8 * dev, 8), 8), :]

        mine = pltpu.make_async_copy(v_ref, rows(me), local_sem)
        mine.start()
        rel = [((k >> 2) & 1, (k >> 1) & 1, k & 1) for k in range(1, N_DEV)]

        def peer(k):
            fx, fy, fc = rel[k]
            return (x ^ fx, y ^ fy, c ^ fc)

        def copy(k, dev):
            return pltpu.make_async_remote_copy(
                src_ref=v_ref, dst_ref=rows(dev), send_sem=send_sems.at[k], recv_sem=recv_sems.at[k],
                device_id=peer(k), device_id_type=MESH)

        sends = [copy(k, me) for k in range(N_DEV - 1)]
        for cp in sends:
            cp.start()
        for k in range(N_DEV - 1):
            px, py, pc = peer(k)
            copy(k, 4 * px + 2 * py + pc).wait_recv()
        for cp in sends:
            cp.wait_send()
        mine.wait()
        total = all_ref[0:8, :]
        for dev in range(1, N_DEV):
            total = total + all_ref[8 * dev:8 * dev + 8, :]
        o_ref[...] = total

    return _pcall(
        body, name=name,
        in_specs=[pl.BlockSpec(memory_space=pltpu.VMEM)], out_specs=pl.BlockSpec(memory_space=pltpu.VMEM),
        out_shape=jax.ShapeDtypeStruct((8, s), F32),
        scratch_shapes=[pltpu.VMEM((8 * N_DEV, s), F32), pltpu.SemaphoreType.DMA((N_DEV - 1,)),
                        pltpu.SemaphoreType.DMA((N_DEV - 1,)), pltpu.SemaphoreType.DMA],
    )(vec)


TOKEN_TILE = 512
F_SPLIT = 2
MXU_COLS = 256


def _chunks(n):
    out, c0 = [], 0
    while c0 < n:
        size = min(MXU_COLS, n - c0)
        out.append((c0, size))
        c0 += size
    return out


def _load_weights(hbm_refs, vmem_refs, sems):
    copies = [pltpu.make_async_copy(h, v, sems.at[k]) for k, (h, v) in enumerate(zip(hbm_refs, vmem_refs))]
    for cp in copies:
        cp.start()
    for cp in copies:
        cp.wait()


def _ffn_fwd(x, g, wgt, wut, wd, name, plan=None):
    T, D = x.shape
    F = wgt.shape[0]
    tm = min(T, TOKEN_TILE)
    tf = F // F_SPLIT
    ni = T // tm

    def body(*refs):
        (x_ref, g_ref, wg_hbm, wu_hbm, wd_hbm), (xo_ref, h_ref, gate_ref, up_ref, act_ref), scratch, cr = _unpack(refs, 5, 5, plan)
        wg_ref, wu_ref, wd_ref, acc_ref, sems = scratch
        i = pl.program_id(0)
        f = pl.program_id(1)
        _hook(plan, cr, "start", jnp.logical_and(i == 0, f == 0))

        @pl.when(jnp.logical_and(i == 0, f == 0))
        def _():
            _load_weights((wg_hbm, wu_hbm, wd_hbm), (wg_ref, wu_ref, wd_ref), sems)

        @pl.when(f == 0)
        def _():
            xv = x_ref[...]
            h_ref[...] = ((xv * _rms_inv(xv)) * g_ref[...]).astype(BF16)
            acc_ref[...] = jnp.zeros_like(acc_ref)

        h = h_ref[...]
        for c0, size in _chunks(tf):
            rows = pl.ds(pl.multiple_of(f * tf + c0, LANES), size)
            gate = _dot_nt(h, wg_ref[rows, :])
            up = _dot_nt(h, wu_ref[rows, :])
            gate_ref[:, c0:c0 + size] = gate.astype(BF16)
            up_ref[:, c0:c0 + size] = up.astype(BF16)
            act_ref[:, c0:c0 + size] = (gate * jax.nn.sigmoid(gate) * up).astype(BF16)
        acc_ref[...] += _dot(act_ref[...], wd_ref[pl.ds(pl.multiple_of(f * tf, LANES), tf), :])

        @pl.when(f == F_SPLIT - 1)
        def _():
            xo_ref[...] = x_ref[...] + FFN_RES_SCALE * acc_ref[...]

        _hook(plan, cr, "middle", jnp.logical_and(i == (3 * ni) // 4, f == 0))
        _hook(plan, cr, "finish", jnp.logical_and(i == ni - 1, f == F_SPLIT - 1))

    in_specs, out_specs, out_shape, scratch = _carried(
        plan,
        [pl.BlockSpec((tm, D), lambda i, f: (i, 0)), pl.BlockSpec((1, D), lambda i, f: (0, 0)),
         ANY_SPEC, ANY_SPEC, ANY_SPEC],
        [pl.BlockSpec((tm, D), lambda i, f: (i, 0)), pl.BlockSpec((tm, D), lambda i, f: (i, 0)),
         pl.BlockSpec((tm, tf), lambda i, f: (i, f)), pl.BlockSpec((tm, tf), lambda i, f: (i, f)),
         pl.BlockSpec((tm, tf), lambda i, f: (i, f))],
        [jax.ShapeDtypeStruct((T, D), F32), jax.ShapeDtypeStruct((T, D), BF16),
         jax.ShapeDtypeStruct((T, F), BF16), jax.ShapeDtypeStruct((T, F), BF16), jax.ShapeDtypeStruct((T, F), BF16)],
        [pltpu.VMEM((F, D), BF16), pltpu.VMEM((F, D), BF16), pltpu.VMEM((F, D), BF16),
         pltpu.VMEM((tm, D), F32), pltpu.SemaphoreType.DMA((3,))])
    return _pcall(
        body, name=name, grid=(ni, F_SPLIT), in_specs=in_specs, out_specs=out_specs, out_shape=out_shape,
        scratch_shapes=scratch, compiler_params=_params(2),
    )(x, g, wgt, wut, wd, *(plan.arrays if plan else ()))


def _ffn_bwd(dy, x, g, gate, up, wgt, wut, wd, name, plan=None):
    T, D = x.shape
    F = wgt.shape[0]
    tm = min(T, TOKEN_TILE)
    tf = F // F_SPLIT
    ni = T // tm

    def body(*refs):
        ins, outs, scratch, cr = _unpack(refs, 8, 5, plan)
        dy_ref, x_ref, g_ref, gate_ref, up_ref, wg_hbm, wu_hbm, wd_hbm = ins
        dx_ref, dyb_ref, dgate_ref, dup_ref, dg_ref = outs
        wg_ref, wu_ref, wd_ref, dh_ref, sems = scratch
        i = pl.program_id(0)
        f = pl.program_id(1)
        _hook(plan, cr, "start", jnp.logical_and(i == 0, f == 0))

        @pl.when(jnp.logical_and(i == 0, f == 0))
        def _():
            _load_weights((wg_hbm, wu_hbm, wd_hbm), (wg_ref, wu_ref, wd_ref), sems)
            dg_ref[...] = jnp.zeros_like(dg_ref)

        @pl.when(f == 0)
        def _():
            dyb_ref[...] = (FFN_RES_SCALE * dy_ref[...]).astype(BF16)
            dh_ref[...] = jnp.zeros_like(dh_ref)

        dyb = dyb_ref[...]
        for c0, size in _chunks(tf):
            rows = pl.ds(pl.multiple_of(f * tf + c0, LANES), size)
            dact = _dot_nt(dyb, wd_ref[rows, :])
            gt = gate_ref[:, c0:c0 + size].astype(F32)
            u = up_ref[:, c0:c0 + size].astype(F32)
            sig = jax.nn.sigmoid(gt)
            dup_ref[:, c0:c0 + size] = (dact * (gt * sig)).astype(BF16)
            dgate_ref[:, c0:c0 + size] = (dact * u * (sig * (1.0 + gt * (1.0 - sig)))).astype(BF16)
        rows = pl.ds(pl.multiple_of(f * tf, LANES), tf)
        dh_ref[...] += _dot(dgate_ref[...], wg_ref[rows, :]) + _dot(dup_ref[...], wu_ref[rows, :])

        @pl.when(f == F_SPLIT - 1)
        def _():
            dxn, dg = _norm_bwd(dh_ref[...], x_ref[...], g_ref[...])
            dx_ref[...] = dy_ref[...] + dxn
            dg_ref[...] += dg

        _hook(plan, cr, "finish", jnp.logical_and(i == ni - 1, f == F_SPLIT - 1))

    in_specs, out_specs, out_shape, scratch = _carried(
        plan,
        [pl.BlockSpec((tm, D), lambda i, f: (i, 0)), pl.BlockSpec((tm, D), lambda i, f: (i, 0)),
         pl.BlockSpec((1, D), lambda i, f: (0, 0)),
         pl.BlockSpec((tm, tf), lambda i, f: (i, f)), pl.BlockSpec((tm, tf), lambda i, f: (i, f)),
         ANY_SPEC, ANY_SPEC, ANY_SPEC],
        [pl.BlockSpec((tm, D), lambda i, f: (i, 0)), pl.BlockSpec((tm, D), lambda i, f: (i, 0)),
         pl.BlockSpec((tm, tf), lambda i, f: (i, f)), pl.BlockSpec((tm, tf), lambda i, f: (i, f)),
         pl.BlockSpec((1, D), lambda i, f: (0, 0))],
        [jax.ShapeDtypeStruct((T, D), F32), jax.ShapeDtypeStruct((T, D), BF16),
         jax.ShapeDtypeStruct((T, F), BF16), jax.ShapeDtypeStruct((T, F), BF16), jax.ShapeDtypeStruct((1, D), F32)],
        [pltpu.VMEM((F, D), BF16), pltpu.VMEM((F, D), BF16), pltpu.VMEM((F, D), BF16),
         pltpu.VMEM((tm, D), F32), pltpu.SemaphoreType.DMA((3,))])
    return _pcall(
        body, name=name, grid=(ni, F_SPLIT), in_specs=in_specs, out_specs=out_specs, out_shape=out_shape,
        scratch_shapes=scratch, compiler_params=_params(2),
    )(dy, x, g, gate, up, wgt, wut, wd, *(plan.arrays if plan else ()))


def _matmul_tn(a, b, row_split, name):
    T, n1 = a.shape
    n2 = b.shape[1]
    tn = n1 // row_split
    tk = min(T, 1024)
    nk = T // tk

    def body(a_ref, b_ref, o_ref, ob_ref):
        k = pl.program_id(1)

        @pl.when(k == 0)
        def _():
            o_ref[...] = jnp.zeros_like(o_ref)

        o_ref[...] += _dot_tn(a_ref[...], b_ref[...])

        @pl.when(k == nk - 1)
        def _():
            ob_ref[...] = o_ref[...].astype(BF16)

    return _pcall(
        body, name=name, grid=(row_split, nk),
        in_specs=[pl.BlockSpec((tk, tn), lambda j, k: (k, j)), pl.BlockSpec((tk, n2), lambda j, k: (k, 0))],
        out_specs=[pl.BlockSpec((tn, n2), lambda j, k: (j, 0)), pl.BlockSpec((tn, n2), lambda j, k: (j, 0))],
        out_shape=[jax.ShapeDtypeStruct((n1, n2), F32), jax.ShapeDtypeStruct((n1, n2), BF16)],
        compiler_params=_params(2),
    )(a, b)


def _norm_matmul(x, g, wt, name):
    T, D = x.shape
    n = wt.shape[0]
    tm = min(T, TOKEN_TILE)

    def body(x_ref, g_ref, w_ref, z_ref, h_ref):
        xv = x_ref[...]
        h = ((xv * _rms_inv(xv)) * g_ref[...]).astype(BF16)
        h_ref[...] = h
        z_ref[...] = _dot_nt(h, w_ref[...])

    return _pcall(
        body, name=name, grid=(T // tm,),
        in_specs=[pl.BlockSpec((tm, D), lambda i: (i, 0)), pl.BlockSpec((1, D), lambda i: (0, 0)),
                  pl.BlockSpec((n, D), lambda i: (0, 0))],
        out_specs=[pl.BlockSpec((tm, n), lambda i: (i, 0)), pl.BlockSpec((tm, D), lambda i: (i, 0))],
        out_shape=[jax.ShapeDtypeStruct((T, n), F32), jax.ShapeDtypeStruct((T, D), BF16)],
        compiler_params=_params(1),
    )(x, g, wt)


def _matmul_residual(y, w, x, name):
    T, D = x.shape
    kdim = y.shape[1]
    tm = min(T, TOKEN_TILE)

    def body(y_ref, w_ref, x_ref, o_ref):
        o_ref[...] = x_ref[...] + _dot(y_ref[...], w_ref[...])

    return _pcall(
        body, name=name, grid=(T // tm,),
        in_specs=[pl.BlockSpec((tm, kdim), lambda i: (i, 0)), pl.BlockSpec((kdim, D), lambda i: (0, 0)),
                  pl.BlockSpec((tm, D), lambda i: (i, 0))],
        out_specs=pl.BlockSpec((tm, D), lambda i: (i, 0)),
        out_shape=jax.ShapeDtypeStruct((T, D), F32),
        compiler_params=_params(1),
    )(y, w, x)


def _matmul_nt(dx, w, name, plan=None):
    T, D = dx.shape
    kdim = w.shape[0]
    tm = min(T, TOKEN_TILE)
    ni = T // tm

    def body(*refs):
        (dx_ref, w_ref), (dy_ref, dxb_ref), _, cr = _unpack(refs, 2, 2, plan)
        i = pl.program_id(0)
        _hook(plan, cr, "start", i == 0)
        dxb = dx_ref[...].astype(BF16)
        dxb_ref[...] = dxb
        dy_ref[...] = _dot_nt(dxb, w_ref[...])
        _hook(plan, cr, "finish", i == ni - 1)

    in_specs, out_specs, out_shape, scratch = _carried(
        plan,
        [pl.BlockSpec((tm, D), lambda i: (i, 0)), pl.BlockSpec((kdim, D), lambda i: (0, 0))],
        [pl.BlockSpec((tm, kdim), lambda i: (i, 0)), pl.BlockSpec((tm, D), lambda i: (i, 0))],
        [jax.ShapeDtypeStruct((T, kdim), F32), jax.ShapeDtypeStruct((T, D), BF16)], [])
    return _pcall(
        body, name=name, grid=(ni,), in_specs=in_specs, out_specs=out_specs, out_shape=out_shape,
        scratch_shapes=scratch, compiler_params=_params(1),
    )(dx, w, *(plan.arrays if plan else ()))


def _matmul_norm_bwd(dz, wt, x, g, dres, name, plan=None):
    T, D = x.shape
    n = dz.shape[1]
    tm = min(T, TOKEN_TILE)
    ni = T // tm

    def body(*refs):
        (dz_ref, w_ref, x_ref, g_ref, dres_ref), (dx_ref, dg_ref), _, cr = _unpack(refs, 5, 2, plan)
        i = pl.program_id(0)
        _hook(plan, cr, "start", i == 0)

        @pl.when(i == 0)
        def _():
            dg_ref[...] = jnp.zeros_like(dg_ref)

        dh = _dot(dz_ref[...], w_ref[...])
        dxn, dg = _norm_bwd(dh, x_ref[...], g_ref[...])
        dx_ref[...] = dres_ref[...] + dxn
        dg_ref[...] += dg
        _hook(plan, cr, "finish", i == ni - 1)

    in_specs, out_specs, out_shape, scratch = _carried(
        plan,
        [pl.BlockSpec((tm, n), lambda i: (i, 0)), pl.BlockSpec((n, D), lambda i: (0, 0)),
         pl.BlockSpec((tm, D), lambda i: (i, 0)), pl.BlockSpec((1, D), lambda i: (0, 0)),
         pl.BlockSpec((tm, D), lambda i: (i, 0))],
        [pl.BlockSpec((tm, D), lambda i: (i, 0)), pl.BlockSpec((1, D), lambda i: (0, 0))],
        [jax.ShapeDtypeStruct((T, D), F32), jax.ShapeDtypeStruct((1, D), F32)], [])
    return _pcall(
        body, name=name, grid=(ni,), in_specs=in_specs, out_specs=out_specs, out_shape=out_shape,
        scratch_shapes=scratch, compiler_params=_params(1),
    )(dz, wt, x, g, dres, *(plan.arrays if plan else ()))


def _loss_head(x, g, target, name):
    T, D = x.shape
    tm = min(T, TOKEN_TILE)

    def body(x_ref, g_ref, t_ref, dx_ref, dg_ref, loss_ref):
        @pl.when(pl.program_id(0) == 0)
        def _():
            dg_ref[...] = jnp.zeros_like(dg_ref)
            loss_ref[...] = jnp.zeros_like(loss_ref)

        xv = x_ref[...]
        gv = g_ref[...]
        out = (xv * _rms_inv(xv)) * gv
        diff = out - t_ref[...]
        loss_ref[...] += 0.5 * jnp.sum(jnp.mean(diff * diff, axis=-1, keepdims=True))
        dxn, dg = _norm_bwd(diff * (1.0 / D), xv, gv)
        dx_ref[...] = dxn
        dg_ref[...] += dg

    return _pcall(
        body, name=name, grid=(T // tm,),
        in_specs=[pl.BlockSpec((tm, D), lambda i: (i, 0)), pl.BlockSpec((1, D), lambda i: (0, 0)),
                  pl.BlockSpec((tm, D), lambda i: (i, 0))],
        out_specs=[pl.BlockSpec((tm, D), lambda i: (i, 0)), pl.BlockSpec((1, D), lambda i: (0, 0)),
                   pl.BlockSpec((1, LANES), lambda i: (0, 0))],
        out_shape=[jax.ShapeDtypeStruct((T, D), F32), jax.ShapeDtypeStruct((1, D), F32),
                   jax.ShapeDtypeStruct((1, LANES), F32)],
        compiler_params=_params(1),
    )(x, g, target)


Z_Q = 3 * CONV_WIDTH
Z_K = Z_Q + N_Q_HEADS * HEAD_DIM
Z_V = Z_K + LANES
Z_END = Z_V + LANES


def _rope_tables(T):
    half = ROT_DIM // 2
    inv_freq = ROPE_THETA ** (-jnp.arange(0, ROT_DIM, 2, dtype=F32) / ROT_DIM)
    ang = inv_freq[:, None] * jnp.arange(T, dtype=F32)[None, :]
    cos, sin = lax.optimization_barrier((jnp.cos(ang), jnp.sin(ang)))
    cos, sin = cos.T, sin.T
    zero = jnp.zeros((T, half), F32)
    rest0 = jnp.zeros((T, HEAD_DIM - ROT_DIM), F32)
    c = jnp.concatenate([cos, cos, rest0 + 1.0], axis=1)
    s1 = jnp.concatenate([-sin, zero, rest0], axis=1)
    s2 = jnp.concatenate([zero, sin, rest0], axis=1)
    return jnp.concatenate([c, c, s1, s1, s2, s2], axis=1)


def _tab3(tab):
    return tab[:, 0:LANES], tab[:, LANES:2 * LANES], tab[:, 2 * LANES:3 * LANES]


def _rot(x, tab):
    c, s1, s2 = _tab3(tab)
    return x * c + pltpu.roll(x, LANES - ROT_DIM // 2, 1) * s1 + pltpu.roll(x, ROT_DIM // 2, 1) * s2


def _rot_t(d, tab):
    c, s1, s2 = _tab3(tab)
    return d * c + pltpu.roll(d * s1, ROT_DIM // 2, 1) + pltpu.roll(d * s2, LANES - ROT_DIM // 2, 1)


def _head_pads(a):
    lo = lax.broadcasted_iota(jnp.int32, a.shape, 1) < HEAD_DIM
    nat0 = jnp.where(lo, a, 0.0)
    nat1 = jnp.where(lo, 0.0, a)
    return {
        (0, 0): nat0.astype(BF16), (0, 1): pltpu.roll(nat0, HEAD_DIM, 1).astype(BF16),
        (1, 0): pltpu.roll(nat1, HEAD_DIM, 1).astype(BF16), (1, 1): nat1.astype(BF16),
    }


def _from_pads(even, odd, kv):
    lo = lax.broadcasted_iota(jnp.int32, even.shape, 1) < HEAD_DIM
    if kv == 0:
        return jnp.where(lo, even + pltpu.roll(odd, HEAD_DIM, 1), 0.0)
    return jnp.where(lo, 0.0, pltpu.roll(even, HEAD_DIM, 1) + odd)


def _window_mask(has_prev):
    ii = lax.broadcasted_iota(jnp.int32, (BLOCK, 2 * BLOCK), 0)
    jj = lax.broadcasted_iota(jnp.int32, (BLOCK, 2 * BLOCK), 1)
    rel = jj - BLOCK - ii
    return (rel <= 0) & (rel > -BLOCK) & ((jj >= BLOCK) | has_prev)


def _softmax_sink(q2, kpad, sink, mask):
    s = _dot_nt(q2, kpad) * ATTN_SCALE
    s = jnp.where(mask, s, MASK_VALUE)
    m = jnp.maximum(jnp.max(s, axis=-1, keepdims=True), sink)
    p = jnp.exp(s - m)
    esink = jnp.exp(sink - m)
    rden = 1.0 / (jnp.sum(p, axis=-1, keepdims=True) + esink)
    return p * rden, esink * rden


def _conv_taps(cg, u, cg_prev, u_prev, has_prev):
    vv = cg * u
    halo = jnp.where(has_prev, cg_prev * u_prev, 0.0)
    ext = jnp.concatenate([halo, vv], axis=0)
    rows = ext.shape[0]
    vv1 = pltpu.roll(ext, 1, 0)[8:rows]
    vv2 = pltpu.roll(ext, 2, 0)[8:rows]
    return vv, vv1, vv2


def _mix_specs(nb):
    cur = lambda n: jnp.minimum(n, nb - 1)
    prev = lambda n: jnp.maximum(jnp.minimum(n, nb - 1) - 1, 0)
    rows8_prev = lambda n: jnp.maximum(16 * jnp.minimum(n, nb - 1) - 1, 0)
    return cur, prev, [
        pl.BlockSpec((BLOCK, Z_END), lambda n: (cur(n), 0)),
        pl.BlockSpec((BLOCK, 2 * LANES), lambda n: (prev(n), Z_K // (2 * LANES))),
        pl.BlockSpec((8, CONV_WIDTH), lambda n: (rows8_prev(n), 1)),
        pl.BlockSpec((8, CONV_WIDTH), lambda n: (rows8_prev(n), 2)),
        pl.BlockSpec((BLOCK, 3 * LANES), lambda n: (cur(n), 0)),
        pl.BlockSpec((BLOCK, 3 * LANES), lambda n: (prev(n), 0)),
        pl.BlockSpec((3, CONV_WIDTH), lambda n: (0, 0)),
        pl.BlockSpec(memory_space=pltpu.SMEM),
    ]


def _mix_core_fwd(z, tab, conv_w, sinks, name):
    T = z.shape[0]
    nb = T // BLOCK
    _, _, specs = _mix_specs(nb)

    def body(z_ref, zkvp_ref, cgp_ref, up_ref, tab_ref, tabp_ref, cw_ref, sink_ref, y_ref):
        has_prev = pl.program_id(0) > 0
        bg = z_ref[:, 0:CONV_WIDTH]
        vv, vv1, vv2 = _conv_taps(z_ref[:, CONV_WIDTH:2 * CONV_WIDTH], z_ref[:, 2 * CONV_WIDTH:Z_Q],
                                  cgp_ref[...], up_ref[...], has_prev)
        conv = cw_ref[0:1, :] * vv2 + cw_ref[1:2, :] * vv1 + cw_ref[2:3, :] * vv
        y_ref[:, 0:CONV_WIDTH] = (bg * conv).astype(BF16)

        tab_c = tab_ref[...]
        tab_p = tabp_ref[...]
        k_all = jnp.concatenate([_rot(zkvp_ref[:, 0:LANES], tab_p), _rot(z_ref[:, Z_K:Z_V], tab_c)], axis=0)
        v_all = jnp.concatenate([zkvp_ref[:, LANES:2 * LANES], z_ref[:, Z_V:Z_END]], axis=0)
        kp = _head_pads(k_all)
        vp = _head_pads(v_all)
        mask = _window_mask(has_prev)
        for c in range(N_Q_HEADS // 2):
            kv = c // 2
            q2 = _rot(z_ref[:, Z_Q + LANES * c:Z_Q + LANES * (c + 1)], tab_c).astype(BF16)
            o = jnp.zeros((BLOCK, LANES), F32)
            for par in range(2):
                probs, _ = _softmax_sink(q2, kp[(kv, par)], sink_ref[0, 2 * c + par], mask)
                o = o + _dot(probs.astype(BF16), vp[(kv, par)])
            y_ref[:, CONV_WIDTH + LANES * c:CONV_WIDTH + LANES * (c + 1)] = o.astype(BF16)

    return _pcall(
        body, name=name, grid=(nb,), in_specs=specs,
        out_specs=pl.BlockSpec((BLOCK, 2 * CONV_WIDTH), lambda n: (n, 0)),
        out_shape=jax.ShapeDtypeStruct((T, 2 * CONV_WIDTH), BF16),
        compiler_params=_params(1),
    )(z, z, z, z, tab, tab, conv_w, sinks)


def _mix_core_bwd(z, dy, tab, conv_w, sinks, name, plan=None):
    T = z.shape[0]
    nb = T // BLOCK
    cur, _, specs = _mix_specs(nb)
    rows8_next = lambda n: jnp.minimum(16 * (cur(n) + 1), 16 * nb - 1)
    specs = specs[:4] + [
        pl.BlockSpec((8, CONV_WIDTH), lambda n: (rows8_next(n), 0)),
        pl.BlockSpec((BLOCK, 2 * CONV_WIDTH), lambda n: (cur(n), 0)),
        pl.BlockSpec((8, CONV_WIDTH), lambda n: (rows8_next(n), 0)),
    ] + specs[4:]

    def body(*refs):
        ins, outs, scratch, cr = _unpack(refs, 11, 3, plan)
        z_ref, zkvp_ref, cgp_ref, up_ref, bgn_ref, dy_ref, dyn_ref, tab_ref, tabp_ref, cw_ref, sink_ref = ins
        dz_ref, dcw_ref, dsk_ref = outs
        main_ref, kv_ref = scratch
        n = pl.program_id(0)
        _hook(plan, cr, "start", n == 0)

        @pl.when(n == 0)
        def _():
            main_ref[...] = jnp.zeros_like(main_ref)
            kv_ref[...] = jnp.zeros_like(kv_ref)
            dcw_ref[...] = jnp.zeros_like(dcw_ref)
            dsk_ref[...] = jnp.zeros_like(dsk_ref)

        @pl.when(n < nb)
        def _():
            has_prev = n > 0
            has_next = n < nb - 1
            bg = z_ref[:, 0:CONV_WIDTH]
            cg = z_ref[:, CONV_WIDTH:2 * CONV_WIDTH]
            u = z_ref[:, 2 * CONV_WIDTH:Z_Q]
            vv, vv1, vv2 = _conv_taps(cg, u, cgp_ref[...], up_ref[...], has_prev)
            w0, w1, w2 = cw_ref[0:1, :], cw_ref[1:2, :], cw_ref[2:3, :]
            dyc = dy_ref[:, 0:CONV_WIDTH]
            dbg = dyc * (w0 * vv2 + w1 * vv1 + w2 * vv)
            dconv = dyc * bg
            dconv_next = jnp.where(has_next, dyn_ref[...] * bgn_ref[...], 0.0)
            ext = jnp.concatenate([dconv, dconv_next], axis=0)
            rows = ext.shape[0]
            dvv = w2 * dconv + w1 * pltpu.roll(ext, rows - 1, 0)[0:BLOCK] + w0 * pltpu.roll(ext, rows - 2, 0)[0:BLOCK]
            dcw_ref[0:1, :] += jnp.sum(dconv * vv2, axis=0, keepdims=True)
            dcw_ref[1:2, :] += jnp.sum(dconv * vv1, axis=0, keepdims=True)
            dcw_ref[2:3, :] += jnp.sum(dconv * vv, axis=0, keepdims=True)

            tab_c = tab_ref[...]
            tab_p = tabp_ref[...]
            k_all = jnp.concatenate([_rot(zkvp_ref[:, 0:LANES], tab_p), _rot(z_ref[:, Z_K:Z_V], tab_c)], axis=0)
            v_all = jnp.concatenate([zkvp_ref[:, LANES:2 * LANES], z_ref[:, Z_V:Z_END]], axis=0)
            kp = _head_pads(k_all)
            vp = _head_pads(v_all)
            mask = _window_mask(has_prev)
            dq_chunks = []
            dk_nat = jnp.zeros((2 * BLOCK, LANES), F32)
            dv_nat = jnp.zeros((2 * BLOCK, LANES), F32)
            for kv in range(2):
                q2s, dos, pbs, dss = [], [], [[], []], [[], []]
                for c in (2 * kv, 2 * kv + 1):
                    q2 = _rot(z_ref[:, Z_Q + LANES * c:Z_Q + LANES * (c + 1)], tab_c).astype(BF16)
                    do2 = dy_ref[:, CONV_WIDTH + LANES * c:CONV_WIDTH + LANES * (c + 1)].astype(BF16)
                    dq = jnp.zeros((BLOCK, LANES), F32)
                    for par in range(2):
                        h = 2 * c + par
                        probs, psink = _softmax_sink(q2, kp[(kv, par)], sink_ref[0, h], mask)
                        dp = _dot_nt(do2, vp[(kv, par)])
                        delta = jnp.sum(dp * probs, axis=-1, keepdims=True)
                        ds = (probs * (dp - delta) * ATTN_SCALE).astype(BF16)
                        dsk_ref[h:h + 1, :] += jnp.sum(-psink * delta)
                        dq = dq + _dot(ds, kp[(kv, par)])
                        pbs[par].append(probs.astype(BF16))
                        dss[par].append(ds)
                    q2s.append(q2)
                    dos.append(do2)
                    dq_chunks.append(_rot_t(dq, tab_c))
                q_st = jnp.concatenate(q2s, axis=0)
                do_st = jnp.concatenate(dos, axis=0)
                dk_par = [_dot_tn(jnp.concatenate(dss[par], axis=0), q_st) for par in range(2)]
                dv_par = [_dot_tn(jnp.concatenate(pbs[par], axis=0), do_st) for par in range(2)]
                dk_nat = dk_nat + _from_pads(dk_par[0], dk_par[1], kv)
                dv_nat = dv_nat + _from_pads(dv_par[0], dv_par[1], kv)

            dk_prev = _rot_t(kv_ref[:, 0:LANES] + dk_nat[0:BLOCK], tab_p)
            dv_prev = kv_ref[:, LANES:2 * LANES] + dv_nat[0:BLOCK]
            dz_ref[:, 0:Z_K] = main_ref[...]
            dz_ref[:, Z_K:Z_V] = dk_prev.astype(BF16)
            dz_ref[:, Z_V:Z_END] = dv_prev.astype(BF16)
            main_ref[:, 0:CONV_WIDTH] = dbg.astype(BF16)
            main_ref[:, CONV_WIDTH:2 * CONV_WIDTH] = (dvv * u).astype(BF16)
            main_ref[:, 2 * CONV_WIDTH:Z_Q] = (dvv * cg).astype(BF16)
            for c in range(N_Q_HEADS // 2):
                main_ref[:, Z_Q + LANES * c:Z_Q + LANES * (c + 1)] = dq_chunks[c].astype(BF16)
            kv_ref[:, 0:LANES] = dk_nat[BLOCK:2 * BLOCK]
            kv_ref[:, LANES:2 * LANES] = dv_nat[BLOCK:2 * BLOCK]

        @pl.when(n == nb)
        def _():
            dz_ref[:, 0:Z_K] = main_ref[...]
            dz_ref[:, Z_K:Z_V] = _rot_t(kv_ref[:, 0:LANES], tab_ref[...]).astype(BF16)
            dz_ref[:, Z_V:Z_END] = kv_ref[:, LANES:2 * LANES].astype(BF16)

        _hook(plan, cr, "finish", n == nb)

    in_specs, out_specs, out_shape, scratch = _carried(
        plan, specs,
        [pl.BlockSpec((BLOCK, Z_END), lambda n: (jnp.maximum(n - 1, 0), 0)),
         pl.BlockSpec((8, CONV_WIDTH), lambda n: (0, 0)), pl.BlockSpec((8, LANES), lambda n: (0, 0))],
        [jax.ShapeDtypeStruct((T, Z_END), BF16), jax.ShapeDtypeStruct((8, CONV_WIDTH), F32),
         jax.ShapeDtypeStruct((8, LANES), F32)],
        [pltpu.VMEM((BLOCK, Z_K), BF16), pltpu.VMEM((BLOCK, 2 * LANES), F32)])
    return _pcall(
        body, name=name, grid=(nb + 1,), in_specs=in_specs, out_specs=out_specs, out_shape=out_shape,
        scratch_shapes=scratch, compiler_params=_params(1),
    )(z, z, z, z, z, dy, dy, tab, tab, conv_w, sinks, *(plan.arrays if plan else ()))


ROW_SPLIT = 2


def _pair_sum(grads, recvd, core, name):
    n = len(grads)

    def body(core_ref, *refs):
        g, r = refs[:n], refs[n:2 * n]
        s, sb = refs[2 * n:3 * n], refs[3 * n:]
        for t in range(n):
            tot = g[t][...] + r[t][...].astype(F32)
            s[t][...] = tot
            sb[t][...] = tot.astype(BF16)

    def blk(a):
        return (1, a.shape[1] // ROW_SPLIT, a.shape[2])

    in_specs = [pl.BlockSpec(blk(r), lambda q, i, core_ref: (q, core_ref[0] * ROW_SPLIT + i, 0)) for r in recvd]
    in_specs += [pl.BlockSpec(blk(r), lambda q, i, core_ref: (q, i, 0)) for r in recvd]
    out_specs = [pl.BlockSpec(blk(r), lambda q, i, core_ref: (q, i, 0)) for r in recvd] * 2
    return _pcall(
        body, name=name,
        grid_spec=pltpu.PrefetchScalarGridSpec(num_scalar_prefetch=1, grid=(N_CHIPS, ROW_SPLIT),
                                               in_specs=in_specs, out_specs=out_specs),
        out_shape=[jax.ShapeDtypeStruct(r.shape, F32) for r in recvd] + [jax.ShapeDtypeStruct(r.shape, BF16) for r in recvd],
        compiler_params=_params(2),
    )(core, *grads, *recvd)


def _chip_sum(parts, recvd, place, name):
    n = len(parts)

    def body(place_ref, *refs):
        p, r, o = refs[:n], refs[n:2 * n], refs[2 * n:]
        for t in range(n):
            tot = p[t][0]
            for j in range(3):
                tot = tot + r[t][j].astype(F32)
            o[t][...] = tot

    in_specs = [pl.BlockSpec((1, p.shape[1] // ROW_SPLIT, p.shape[2]), lambda i, place_ref: (place_ref[0], i, 0))
                for p in parts]
    in_specs += [pl.BlockSpec((3, r.shape[1] // ROW_SPLIT, r.shape[2]), lambda i, place_ref: (0, i, 0)) for r in recvd]
    out_specs = [pl.BlockSpec((p.shape[1] // ROW_SPLIT, p.shape[2]),
                              lambda i, place_ref: (place_ref[1] * ROW_SPLIT + i, 0)) for p in parts]
    return _pcall(
        body, name=name,
        grid_spec=pltpu.PrefetchScalarGridSpec(num_scalar_prefetch=1, grid=(ROW_SPLIT,),
                                               in_specs=in_specs, out_specs=out_specs),
        out_shape=[jax.ShapeDtypeStruct((2 * p.shape[1], p.shape[2]), F32) for p in parts],
        compiler_params=_params(1),
    )(place, *parts, *recvd)


def _adamw_math(w, g, m, v):
    m = ADAM_B1 * m + (1.0 - ADAM_B1) * g
    v = ADAM_B2 * v + (1.0 - ADAM_B2) * (g * g)
    m_hat = m / (1.0 - ADAM_B1 ** ADAM_STEP)
    v_hat = v / (1.0 - ADAM_B2 ** ADAM_STEP)
    delta = -ADAM_LR * (m_hat / (jnp.sqrt(v_hat) + ADAM_EPS) + ADAM_WD * w)
    return delta, m, v


def _adamw(ws, gs, ms, vs, row_blocks, name):
    n = len(ws)

    def body(*refs):
        w, g, m, v = refs[:n], refs[n:2 * n], refs[2 * n:3 * n], refs[3 * n:4 * n]
        d, mo, vo = refs[4 * n:5 * n], refs[5 * n:6 * n], refs[6 * n:]
        for t in range(n):
            delta, m_new, v_new = _adamw_math(w[t][...], g[t][...], m[t][...], v[t][...])
            d[t][...] = delta
            mo[t][...] = m_new
            vo[t][...] = v_new

    specs = [pl.BlockSpec((a.shape[0] // row_blocks, a.shape[1]), lambda i: (i, 0)) for a in ws]
    shapes = [jax.ShapeDtypeStruct(a.shape, F32) for a in ws]
    return _pcall(
        body, name=name, grid=(row_blocks,), in_specs=specs * 4, out_specs=specs * 3, out_shape=shapes * 3,
        compiler_params=_params(1),
    )(*ws, *gs, *ms, *vs)


def kernel(x, ffn1_norm, ffn1_w_gate, ffn1_w_up, ffn1_w_down, mix_norm, w_in, conv_w, attn_sinks, w_out, ffn2_norm, ffn2_w_gate, ffn2_w_up, ffn2_w_down, final_norm, loss_target, m_ffn1_norm, m_ffn1_w_gate, m_ffn1_w_up, m_ffn1_w_down, m_mix_norm, m_w_in, m_conv_w, m_attn_sinks, m_w_out, m_ffn2_norm, m_ffn2_w_gate, m_ffn2_w_up, m_ffn2_w_down, m_final_norm, v_ffn1_norm, v_ffn1_w_gate, v_ffn1_w_up, v_ffn1_w_down, v_mix_norm, v_w_in, v_conv_w, v_attn_sinks, v_w_out, v_ffn2_norm, v_ffn2_w_gate, v_ffn2_w_up, v_ffn2_w_down, v_final_norm):
    T, D = x.shape[1], x.shape[2]
    chip = (2 * lax.axis_index("x") + lax.axis_index("y")).astype(jnp.int32)
    core = lax.axis_index("c").astype(jnp.int32)
    place = jnp.stack([chip, core])
    x0 = x[0]
    target = loss_target[0]
    gf = final_norm.reshape(1, D)

    tr = lambda w: jnp.swapaxes(w[0], 0, 1)
    big = [tr(ffn1_w_gate), tr(ffn1_w_up), ffn1_w_down[0], tr(w_in), w_out[0], tr(ffn2_w_gate), tr(ffn2_w_up), ffn2_w_down[0]]
    transposed = [True, True, False, True, False, True, True, False]
    own_b = [w.astype(BF16) for w in big]

    def whole(gathered, own):
        return lax.dynamic_update_slice(gathered, own[None], (chip, 0, 0)).reshape(-1, D)

    got1 = _run_comm(_gather_plan(own_b[0:3]), "gather_ffn1")
    wg1, wu1, wd1 = (whole(g, o) for g, o in zip(got1, own_b[0:3]))
    tab = _rope_tables(T)

    res = _ffn_fwd(x0, ffn1_norm, wg1, wu1, wd1, "ffn1_fwd", _gather_plan(own_b[3:8], [conv_w[0]]))
    x1, h1, gate1, up1, act1 = res[:5]
    win, wout, wg2, wu2, wd2 = (whole(g, o) for g, o in zip(res[5:10], own_b[3:8]))
    convw4 = lax.dynamic_update_slice(res[10], conv_w, (chip, 0, 0))
    convw = jnp.transpose(convw4, (1, 0, 2)).reshape(3, -1)
    z, hm = _norm_matmul(x1, mix_norm, win, "mix_in_fwd")
    ymix = _mix_core_fwd(z, tab, convw, attn_sinks, "mix_core_fwd")
    x2 = _matmul_residual(ymix, wout, x1, "mix_out_fwd")
    x3, h2, gate2, up2, act2 = _ffn_fwd(x2, ffn2_norm, wg2, wu2, wd2, "ffn2_fwd")
    dx3, dgf, loss_part = _loss_head(x3, gf, target, "loss_head")

    def quarters(pairs):
        return [p[0].reshape(N_CHIPS, -1, D) for p in pairs], [p[1].reshape(N_CHIPS, -1, D) for p in pairs]

    def split(res, n):
        return res[:n], res[n:]

    dx2, dyb2, dgate2, dup2, dg2 = _ffn_bwd(dx3, x2, ffn2_norm, gate2, up2, wg2, wu2, wd2, "ffn2_bwd")
    g2, g2b = quarters([_matmul_tn(dgate2, h2, F_SPLIT, "ffn2_dwg"), _matmul_tn(dup2, h2, F_SPLIT, "ffn2_dwu"),
                        _matmul_tn(act2, dyb2, F_SPLIT, "ffn2_dwd")])
    (dymix, dx2b), sib2 = split(_matmul_nt(dx2, wout, "mix_out_bwd", _sibling_plan(g2b)), 2)
    gwout = _matmul_tn(ymix, dx2b, F_SPLIT, "mix_dwout")
    pair2 = _pair_sum(g2, sib2, core.reshape(1), "pair_sum_ffn2")
    (dz, dcw, dsk), chips2 = split(_mix_core_bwd(z, dymix, tab, convw, attn_sinks, "mix_core_bwd", _scatter_plan(pair2[3:])), 3)
    half2 = _chip_sum(pair2[:3], chips2, place, "chip_sum_ffn2")
    gwin = _matmul_tn(dz, hm, F_SPLIT, "mix_dwin")
    gm, gmb = quarters([gwin, gwout])
    (dx1, dgm), sibm = split(_matmul_norm_bwd(dz, win, x1, mix_norm, dx2, "mix_in_bwd", _sibling_plan(gmb)), 2)
    pairm = _pair_sum(gm, sibm, core.reshape(1), "pair_sum_mix")
    (dx0, dyb1, dgate1, dup1, dg1), chipsm = split(
        _ffn_bwd(dx1, x0, ffn1_norm, gate1, up1, wg1, wu1, wd1, "ffn1_bwd", _scatter_plan(pairm[2:])), 5)
    halfm = _chip_sum(pairm[:2], chipsm, place, "chip_sum_mix")
    g1, g1b = quarters([_matmul_tn(dgate1, h1, F_SPLIT, "ffn1_dwg"), _matmul_tn(dup1, h1, F_SPLIT, "ffn1_dwu"),
                        _matmul_tn(act1, dyb1, F_SPLIT, "ffn1_dwd")])
    sib1 = _run_comm(_sibling_plan(g1b), "grads1_to_sibling")
    pair1 = _pair_sum(g1, sib1, core.reshape(1), "pair_sum_ffn1")
    chips1 = _run_comm(_scatter_plan(pair1[3:]), "grads1_to_chips")
    half1 = _chip_sum(pair1[:3], chips1, place, "chip_sum_ffn1")
    g_big = _join_halves([*half1, *halfm, *half2], "grads_join")

    pad = lambda a: jnp.pad(a, ((0, 0), (0, LANES - a.shape[1])))
    vec = jnp.concatenate([dg1, dgm, dg2, dgf, dcw[0:3].reshape(1, -1), pad(dsk[:, 0].reshape(1, -1)),
                           pad(loss_part[:, 0:1])], axis=1)
    total = _all_reduce_small(jnp.pad(vec, ((0, 7), (0, 0))), "small_all_reduce")[0:1]
    g_n1, g_nm, g_n2, g_nf = (total[:, k * D:(k + 1) * D] for k in range(4))
    cw_full = total[:, 4 * D:4 * D + 3 * CONV_WIDTH].reshape(3, CONV_WIDTH)
    cq = CONV_WIDTH // N_CHIPS
    g_cw = lax.dynamic_slice(cw_full, (0, chip * cq), (3, cq))
    off = 4 * D + 3 * CONV_WIDTH
    g_sk = total[:, off:off + N_Q_HEADS]
    loss = total[0, off + LANES]

    ws = big
    ms = [tr(m_ffn1_w_gate), tr(m_ffn1_w_up), m_ffn1_w_down[0], tr(m_w_in), m_w_out[0], tr(m_ffn2_w_gate), tr(m_ffn2_w_up), m_ffn2_w_down[0]]
    vs = [tr(v_ffn1_w_gate), tr(v_ffn1_w_up), v_ffn1_w_down[0], tr(v_w_in), v_w_out[0], tr(v_ffn2_w_gate), tr(v_ffn2_w_up), v_ffn2_w_down[0]]
    upd = {}
    for name_, idx in (("adamw_a", [0, 1, 2, 4]), ("adamw_b", [3, 5, 6, 7])):
        res = _adamw([ws[i] for i in idx], [g_big[i] for i in idx], [ms[i] for i in idx], [vs[i] for i in idx], 8, name_)
        k = len(idx)
        for j, i in enumerate(idx):
            upd[i] = (res[j], res[k + j], res[2 * k + j])
    sw = [ffn1_norm, mix_norm, conv_w[0], attn_sinks, ffn2_norm, gf]
    sg = [g_n1, g_nm, g_cw, g_sk, g_n2, g_nf]
    sm = [m_ffn1_norm, m_mix_norm, m_conv_w[0], m_attn_sinks, m_ffn2_norm, m_final_norm.reshape(1, D)]
    sv = [v_ffn1_norm, v_mix_norm, v_conv_w[0], v_attn_sinks, v_ffn2_norm, v_final_norm.reshape(1, D)]
    sres = _adamw(sw, sg, sm, sv, 1, "adamw_small")
    supd = [(sres[j], sres[6 + j], sres[12 + j]) for j in range(6)]

    order = [("s", 0), ("b", 0), ("b", 1), ("b", 2), ("s", 1), ("b", 3), ("s", 2), ("s", 3), ("b", 4),
             ("s", 4), ("b", 5), ("b", 6), ("b", 7), ("s", 5)]

    def leaf(kind, i, which):
        if kind == "b":
            a = g_big[i] if which == 0 else upd[i][which - 1]
            return (jnp.swapaxes(a, 0, 1) if transposed[i] else a)[None]
        a = sg[i] if which == 0 else supd[i][which - 1]
        if i == 2:
            return a[None]
        if i == 5:
            return a.reshape(D)
        return a

    outs = [loss, dx0[None]]
    for which in range(4):
        outs += [leaf(kind, i, which) for kind, i in order]
    return tuple(outs)
```

```python
import functools

import jax
import jax.numpy as jnp
import numpy as np
from jax import lax
from jax.experimental import pallas as pl
from jax.experimental.pallas import tpu as pltpu

F32 = jnp.float32
BF16 = jnp.bfloat16
MESH = pl.DeviceIdType.MESH

CONV_WIDTH = 512
N_Q_HEADS = 8
HEAD_DIM = 64
BLOCK = 128
ROPE_THETA = 500000.0
ROT_DIM = 16
RMS_EPS = 1e-5
MASK_VALUE = -1e30
ATTN_SCALE = HEAD_DIM ** -0.5
FFN_RES_SCALE = 0.5
ADAM_LR = 0.001
ADAM_B1 = 0.9
ADAM_B2 = 0.999
ADAM_EPS = 1e-08
ADAM_WD = 0.01
ADAM_STEP = 10

N_CHIPS = 4
N_DEV = 8
LANES = 128
VMEM_LIMIT = 56 * 1024 * 1024

_pcall = pl.pallas_call
HBM_SPEC = pl.BlockSpec(memory_space=pltpu.HBM)
ANY_SPEC = pl.BlockSpec(memory_space=pl.ANY)


def _params(n_axes, vmem=VMEM_LIMIT):
    return pltpu.CompilerParams(dimension_semantics=("arbitrary",) * n_axes, vmem_limit_bytes=vmem)


def _dot(a, b):
    return jnp.dot(a, b, preferred_element_type=F32)


def _dot_nt(a, b):
    return lax.dot_general(a, b, (((1,), (1,)), ((), ())), preferred_element_type=F32)


def _dot_tn(a, b):
    return lax.dot_general(a, b, (((0,), (0,)), ((), ())), preferred_element_type=F32)


def _rms_inv(x):
    return lax.rsqrt(jnp.mean(x * x, axis=-1, keepdims=True) + RMS_EPS)


def _norm_bwd(dh, x, g):
    inv = _rms_inv(x)
    xhat = x * inv
    dg = jnp.sum(dh * xhat, axis=0, keepdims=True)
    dxhat = dh * g
    dx = inv * (dxhat - xhat * jnp.mean(dxhat * xhat, axis=-1, keepdims=True))
    return dx, dg


def _place():
    x, y, c = lax.axis_index("x"), lax.axis_index("y"), lax.axis_index("c")
    chips = [(1 - x, y), (x, 1 - y), (1 - x, 1 - y)]
    return x, y, c, chips


class _Plan:
    def __init__(self, arrays, out_shapes, n_sems, start, finish, middle=None):
        self.arrays, self.out_shapes, self.n_sems = list(arrays), list(out_shapes), n_sems
        self.start, self.finish, self.middle = start, finish, middle

    def specs(self):
        k = len(self.arrays)
        sems = [pltpu.SemaphoreType.DMA((self.n_sems,)), pltpu.SemaphoreType.DMA((self.n_sems,))]
        return [HBM_SPEC] * k, [HBM_SPEC] * len(self.out_shapes), self.out_shapes, sems


def _sibling_plan(grads_b):
    n = len(grads_b)

    def copies(ins, outs, send_sems, recv_sems):
        x, y, c, _ = _place()

        def copy(t):
            half = ins[t].shape[1] // 2
            return pltpu.make_async_remote_copy(
                src_ref=ins[t].at[:, pl.ds(pl.multiple_of((1 - c) * half, 16), half), :], dst_ref=outs[t],
                send_sem=send_sems.at[t], recv_sem=recv_sems.at[t], device_id=(x, y, 1 - c), device_id_type=MESH)

        return [copy(t) for t in range(n)]

    def start(*refs):
        for cp in copies(*refs):
            cp.start()

    def finish(*refs):
        for cp in copies(*refs):
            cp.wait()

    shapes = [jax.ShapeDtypeStruct((g.shape[0], g.shape[1] // 2, g.shape[2]), g.dtype) for g in grads_b]
    return _Plan(grads_b, shapes, n, start, finish)


def _scatter_plan(parts_b):
    n = len(parts_b)

    def copies(ins, outs, send_sems, recv_sems):
        x, y, c, chips = _place()

        def copy(t, j):
            px, py = chips[j]
            return pltpu.make_async_remote_copy(
                src_ref=ins[t].at[2 * px + py], dst_ref=outs[t].at[j], send_sem=send_sems.at[3 * t + j],
                recv_sem=recv_sems.at[3 * t + j], device_id=(px, py, c), device_id_type=MESH)

        return [copy(t, j) for t in range(n) for j in range(3)]

    def start(*refs):
        for cp in copies(*refs):
            cp.start()

    def finish(*refs):
        for cp in copies(*refs):
            cp.wait()

    shapes = [jax.ShapeDtypeStruct((3, *p.shape[1:]), p.dtype) for p in parts_b]
    return _Plan(parts_b, shapes, 3 * n, start, finish)


def _gather_plan(shards, small=()):
    n, ns = len(shards), len(small)

    def parts(ins, outs, send_sems, recv_sems):
        x, y, c, chips = _place()
        me = 2 * x + y

        def rows(t, core):
            half = ins[t].shape[0] // 2
            return pl.ds(pl.multiple_of(core * half, 16), half)

        def first(t, j, block, core):
            return pltpu.make_async_remote_copy(
                src_ref=ins[t].at[rows(t, core), :], dst_ref=outs[t].at[block, rows(t, core), :],
                send_sem=send_sems.at[6 * t + j], recv_sem=recv_sems.at[6 * t + j],
                device_id=(*chips[j], c), device_id_type=MESH)

        def passed(t, j, block, core):
            ref = outs[t].at[block, rows(t, core), :]
            return pltpu.make_async_remote_copy(
                src_ref=ref, dst_ref=ref, send_sem=send_sems.at[6 * t + 3 + j], recv_sem=recv_sems.at[6 * t + 3 + j],
                device_id=(x, y, 1 - c), device_id_type=MESH)

        def whole(s, j, block):
            k = 6 * n + 3 * s + j
            return pltpu.make_async_remote_copy(
                src_ref=ins[n + s], dst_ref=outs[n + s].at[block], send_sem=send_sems.at[k], recv_sem=recv_sems.at[k],
                device_id=(*chips[j], c), device_id_type=MESH)

        blocks = [2 * px + py for px, py in chips]
        return c, me, blocks, first, passed, whole

    def start(*refs):
        c, me, _, first, _, whole = parts(*refs)
        for t in range(n):
            for j in range(3):
                first(t, j, me, c).start()
        for s in range(ns):
            for j in range(3):
                whole(s, j, me).start()

    def middle(*refs):
        c, _, blocks, first, passed, _ = parts(*refs)
        for t in range(n):
            for j in range(3):
                first(t, j, blocks[j], c).wait_recv()
                passed(t, j, blocks[j], c).start()

    def finish(*refs):
        c, me, blocks, first, passed, whole = parts(*refs)
        for t in range(n):
            for j in range(3):
                passed(t, j, blocks[j], 1 - c).wait_recv()
        for s in range(ns):
            for j in range(3):
                whole(s, j, blocks[j]).wait_recv()
        for t in range(n):
            for j in range(3):
                first(t, j, me, c).wait_send()
                passed(t, j, blocks[j], c).wait_send()
        for s in range(ns):
            for j in range(3):
                whole(s, j, me).wait_send()

    arrays = [*shards, *small]
    shapes = [jax.ShapeDtypeStruct((N_CHIPS, *a.shape), a.dtype) for a in arrays]
    return _Plan(arrays, shapes, 6 * n + 3 * ns, start, finish, middle)


def _run_comm(plan, name):
    k = len(plan.arrays)
    in_specs, out_specs, out_shape, sems = plan.specs()

    def body(*refs):
        cr = (refs[:k], refs[k:k + len(out_shape)], refs[-2], refs[-1])
        plan.start(*cr)
        if plan.middle is not None:
            plan.middle(*cr)
        plan.finish(*cr)

    return _pcall(body, name=name, in_specs=in_specs, out_specs=out_specs, out_shape=out_shape,
                  scratch_shapes=sems)(*plan.arrays)


def _carried(plan, in_specs, out_specs, out_shape, scratch):
    if plan is None:
        return in_specs, out_specs, out_shape, scratch
    p_in, p_out, p_shape, p_sems = plan.specs()
    return in_specs + p_in, out_specs + p_out, out_shape + p_shape, scratch + p_sems


def _unpack(refs, n_in, n_out, plan):
    k_in = len(plan.arrays) if plan else 0
    k_out = len(plan.out_shapes) if plan else 0
    ins = refs[:n_in]
    outs = refs[n_in + k_in:n_in + k_in + n_out]
    rest = refs[n_in + k_in + n_out + k_out:]
    if plan is None:
        return ins, outs, rest, None
    cr = (refs[n_in:n_in + k_in], refs[n_in + k_in + n_out:n_in + k_in + n_out + k_out], rest[-2], rest[-1])
    return ins, outs, rest[:-2], cr


def _hook(plan, cr, which, cond):
    fn = getattr(plan, which) if plan is not None else None
    if fn is not None:
        pl.when(cond)(lambda: fn(*cr))


def _join_halves(shards, name):
    n = len(shards)

    def body(*refs):
        ins, outs = refs[:n], refs[n:2 * n]
        send_sems, recv_sems = refs[2 * n:]
        x, y, c, _ = _place()

        def copy(t, core):
            half = ins[t].shape[0] // 2
            rows = pl.ds(pl.multiple_of(core * half, 8), half)
            return pltpu.make_async_remote_copy(
                src_ref=ins[t].at[rows, :], dst_ref=outs[t].at[rows, :], send_sem=send_sems.at[t],
                recv_sem=recv_sems.at[t], device_id=(x, y, 1 - c), device_id_type=MESH)

        sends = [copy(t, c) for t in range(n)]
        for cp in sends:
            cp.start()
        for t in range(n):
            copy(t, 1 - c).wait_recv()
        for cp in sends:
            cp.wait_send()

    return _pcall(
        body, name=name, in_specs=[HBM_SPEC] * n, out_specs=[HBM_SPEC] * n,
        out_shape=[jax.ShapeDtypeStruct(s.shape, s.dtype) for s in shards],
        input_output_aliases={t: t for t in range(n)},
        scratch_shapes=[pltpu.SemaphoreType.DMA((n,)), pltpu.SemaphoreType.DMA((n,))],
    )(*shards)


def _all_reduce_small(vec, name):
    s = vec.shape[1]

    def body(v_ref, o_ref, all_ref, send_sems, recv_sems, local_sem):
        x, y, c, _ = _place()
        me = 4 * x + 2 * y + c

        def rows(dev):
            return all_ref.at[pl.ds(pl.multiple_of(8 * dev, 8), 8), :]

        mine = pltpu.make_async_copy(v_ref, rows(me), local_sem)
        mine.start()
        rel = [((k >> 2) & 1, (k >> 1) & 1, k & 1) for k in range(1, N_DEV)]

        def peer(k):
            fx, fy, fc = rel[k]
            return (x ^ fx, y ^ fy, c ^ fc)

        def copy(k, dev):
            return pltpu.make_async_remote_copy(
                src_ref=v_ref, dst_ref=rows(dev), send_sem=send_sems.at[k], recv_sem=recv_sems.at[k],
                device_id=peer(k), device_id_type=MESH)

        sends = [copy(k, me) for k in range(N_DEV - 1)]
        for cp in sends:
            cp.start()
        for k in range(N_DEV - 1):
            px, py, pc = peer(k)
            copy(k, 4 * px + 2 * py + pc).wait_recv()
        for cp in sends:
            cp.wait_send()
        mine.wait()
        total = all_ref[0:8, :]
        for dev in range(1, N_DEV):
            total = total + all_ref[8 * dev:8 * dev + 8, :]
        o_ref[...] = total

    return _pcall(
        body, name=name,
        in_specs=[pl.BlockSpec(memory_space=pltpu.VMEM)], out_specs=pl.BlockSpec(memory_space=pltpu.VMEM),
        out_shape=jax.ShapeDtypeStruct((8, s), F32),
        scratch_shapes=[pltpu.VMEM((8 * N_DEV, s), F32), pltpu.SemaphoreType.DMA((N_DEV - 1,)),
                        pltpu.SemaphoreType.DMA((N_DEV - 1,)), pltpu.SemaphoreType.DMA],
    )(vec)


TOKEN_TILE = 512
F_SPLIT = 2
MXU_COLS = 256


def _chunks(n):
    out, c0 = [], 0
    while c0 < n:
        size = min(MXU_COLS, n - c0)
        out.append((c0, size))
        c0 += size
    return out


def _load_weights(hbm_refs, vmem_refs, sems):
    copies = [pltpu.make_async_copy(h, v, sems.at[k]) for k, (h, v) in enumerate(zip(hbm_refs, vmem_refs))]
    for cp in copies:
        cp.start()
    for cp in copies:
        cp.wait()


def _ffn_fwd(x, g, wgt, wut, wd, name, plan=None):
    T, D = x.shape
    F = wgt.shape[0]
    tm = min(T, TOKEN_TILE)
    tf = F // F_SPLIT
    ni = T // tm

    def body(*refs):
        (x_ref, g_ref, wg_hbm, wu_hbm, wd_hbm), (xo_ref, h_ref, gate_ref, up_ref, act_ref), scratch, cr = _unpack(refs, 5, 5, plan)
        wg_ref, wu_ref, wd_ref, acc_ref, sems = scratch
        i = pl.program_id(0)
        f = pl.program_id(1)
        _hook(plan, cr, "start", jnp.logical_and(i == 0, f == 0))

        @pl.when(jnp.logical_and(i == 0, f == 0))
        def _():
            _load_weights((wg_hbm, wu_hbm, wd_hbm), (wg_ref, wu_ref, wd_ref), sems)

        @pl.when(f == 0)
        def _():
            xv = x_ref[...]
            h_ref[...] = ((xv * _rms_inv(xv)) * g_ref[...]).astype(BF16)
            acc_ref[...] = jnp.zeros_like(acc_ref)

        h = h_ref[...]
        for c0, size in _chunks(tf):
            rows = pl.ds(pl.multiple_of(f * tf + c0, LANES), size)
            gate = _dot_nt(h, wg_ref[rows, :])
            up = _dot_nt(h, wu_ref[rows, :])
            gate_ref[:, c0:c0 + size] = gate.astype(BF16)
            up_ref[:, c0:c0 + size] = up.astype(BF16)
            act_ref[:, c0:c0 + size] = (gate * jax.nn.sigmoid(gate) * up).astype(BF16)
        acc_ref[...] += _dot(act_ref[...], wd_ref[pl.ds(pl.multiple_of(f * tf, LANES), tf), :])

        @pl.when(f == F_SPLIT - 1)
        def _():
            xo_ref[...] = x_ref[...] + FFN_RES_SCALE * acc_ref[...]

        _hook(plan, cr, "middle", jnp.logical_and(i == (3 * ni) // 4, f == 0))
        _hook(plan, cr, "finish", jnp.logical_and(i == ni - 1, f == F_SPLIT - 1))

    in_specs, out_specs, out_shape, scratch = _carried(
        plan,
        [pl.BlockSpec((tm, D), lambda i, f: (i, 0)), pl.BlockSpec((1, D), lambda i, f: (0, 0)),
         ANY_SPEC, ANY_SPEC, ANY_SPEC],
        [pl.BlockSpec((tm, D), lambda i, f: (i, 0)), pl.BlockSpec((tm, D), lambda i, f: (i, 0)),
         pl.BlockSpec((tm, tf), lambda i, f: (i, f)), pl.BlockSpec((tm, tf), lambda i, f: (i, f)),
         pl.BlockSpec((tm, tf), lambda i, f: (i, f))],
        [jax.ShapeDtypeStruct((T, D), F32), jax.ShapeDtypeStruct((T, D), BF16),
         jax.ShapeDtypeStruct((T, F), BF16), jax.ShapeDtypeStruct((T, F), BF16), jax.ShapeDtypeStruct((T, F), BF16)],
        [pltpu.VMEM((F, D), BF16), pltpu.VMEM((F, D), BF16), pltpu.VMEM((F, D), BF16),
         pltpu.VMEM((tm, D), F32), pltpu.SemaphoreType.DMA((3,))])
    return _pcall(
        body, name=name, grid=(ni, F_SPLIT), in_specs=in_specs, out_specs=out_specs, out_shape=out_shape,
        scratch_shapes=scratch, compiler_params=_params(2),
    )(x, g, wgt, wut, wd, *(plan.arrays if plan else ()))


def _ffn_bwd(dy, x, g, gate, up, wgt, wut, wd, name, plan=None):
    T, D = x.shape
    F = wgt.shape[0]
    tm = min(T, TOKEN_TILE)
    tf = F // F_SPLIT
    ni = T // tm

    def body(*refs):
        ins, outs, scratch, cr = _unpack(refs, 8, 5, plan)
        dy_ref, x_ref, g_ref, gate_ref, up_ref, wg_hbm, wu_hbm, wd_hbm = ins
        dx_ref, dyb_ref, dgate_ref, dup_ref, dg_ref = outs
        wg_ref, wu_ref, wd_ref, dh_ref, sems = scratch
        i = pl.program_id(0)
        f = pl.program_id(1)
        _hook(plan, cr, "start", jnp.logical_and(i == 0, f == 0))

        @pl.when(jnp.logical_and(i == 0, f == 0))
        def _():
            _load_weights((wg_hbm, wu_hbm, wd_hbm), (wg_ref, wu_ref, wd_ref), sems)
            dg_ref[...] = jnp.zeros_like(dg_ref)

        @pl.when(f == 0)
        def _():
            dyb_ref[...] = (FFN_RES_SCALE * dy_ref[...]).astype(BF16)
            dh_ref[...] = jnp.zeros_like(dh_ref)

        dyb = dyb_ref[...]
        for c0, size in _chunks(tf):
            rows = pl.ds(pl.multiple_of(f * tf + c0, LANES), size)
            dact = _dot_nt(dyb, wd_ref[rows, :])
            gt = gate_ref[:, c0:c0 + size].astype(F32)
            u = up_ref[:, c0:c0 + size].astype(F32)
            sig = jax.nn.sigmoid(gt)
            dup_ref[:, c0:c0 + size] = (dact * (gt * sig)).astype(BF16)
            dgate_ref[:, c0:c0 + size] = (dact * u * (sig * (1.0 + gt * (1.0 - sig)))).astype(BF16)
        rows = pl.ds(pl.multiple_of(f * tf, LANES), tf)
        dh_ref[...] += _dot(dgate_ref[...], wg_ref[rows, :]) + _dot(dup_ref[...], wu_ref[rows, :])

        @pl.when(f == F_SPLIT - 1)
        def _():
            dxn, dg = _norm_bwd(dh_ref[...], x_ref[...], g_ref[...])
            dx_ref[...] = dy_ref[...] + dxn
            dg_ref[...] += dg

        _hook(plan, cr, "finish", jnp.logical_and(i == ni - 1, f == F_SPLIT - 1))

    in_specs, out_specs, out_shape, scratch = _carried(
        plan,
        [pl.BlockSpec((tm, D), lambda i, f: (i, 0)), pl.BlockSpec((tm, D), lambda i, f: (i, 0)),
         pl.BlockSpec((1, D), lambda i, f: (0, 0)),
         pl.BlockSpec((tm, tf), lambda i, f: (i, f)), pl.BlockSpec((tm, tf), lambda i, f: (i, f)),
         ANY_SPEC, ANY_SPEC, ANY_SPEC],
        [pl.BlockSpec((tm, D), lambda i, f: (i, 0)), pl.BlockSpec((tm, D), lambda i, f: (i, 0)),
         pl.BlockSpec((tm, tf), lambda i, f: (i, f)), pl.BlockSpec((tm, tf), lambda i, f: (i, f)),
         pl.BlockSpec((1, D), lambda i, f: (0, 0))],
        [jax.ShapeDtypeStruct((T, D), F32), jax.ShapeDtypeStruct((T, D), BF16),
         jax.ShapeDtypeStruct((T, F), BF16), jax.ShapeDtypeStruct((T, F), BF16), jax.ShapeDtypeStruct((1, D), F32)],
        [pltpu.VMEM((F, D), BF16), pltpu.VMEM((F, D), BF16), pltpu.VMEM((F, D), BF16),
         pltpu.VMEM((tm, D), F32), pltpu.SemaphoreType.DMA((3,))])
    return _pcall(
        body, name=name, grid=(ni, F_SPLIT), in_specs=in_specs, out_specs=out_specs, out_shape=out_shape,
        scratch_shapes=scratch, compiler_params=_params(2),
    )(dy, x, g, gate, up, wgt, wut, wd, *(plan.arrays if plan else ()))


def _matmul_tn(a, b, row_split, name, plan=None):
    T, n1 = a.shape
    n2 = b.shape[1]
    tn = n1 // row_split
    tk = min(T, 1024)
    nk = T // tk

    def body(*refs):
        (a_ref, b_ref), (o_ref, ob_ref), _, cr = _unpack(refs, 2, 2, plan)
        j = pl.program_id(0)
        k = pl.program_id(1)
        _hook(plan, cr, "start", jnp.logical_and(j == 0, k == 0))

        @pl.when(k == 0)
        def _():
            o_ref[...] = jnp.zeros_like(o_ref)

        o_ref[...] += _dot_tn(a_ref[...], b_ref[...])

        @pl.when(k == nk - 1)
        def _():
            ob_ref[...] = o_ref[...].astype(BF16)

        _hook(plan, cr, "finish", jnp.logical_and(j == row_split - 1, k == nk - 1))

    in_specs, out_specs, out_shape, scratch = _carried(
        plan,
        [pl.BlockSpec((tk, tn), lambda j, k: (k, j)), pl.BlockSpec((tk, n2), lambda j, k: (k, 0))],
        [pl.BlockSpec((tn, n2), lambda j, k: (j, 0)), pl.BlockSpec((tn, n2), lambda j, k: (j, 0))],
        [jax.ShapeDtypeStruct((n1, n2), F32), jax.ShapeDtypeStruct((n1, n2), BF16)], [])
    return _pcall(
        body, name=name, grid=(row_split, nk), in_specs=in_specs, out_specs=out_specs, out_shape=out_shape,
        scratch_shapes=scratch, compiler_params=_params(2),
    )(a, b, *(plan.arrays if plan else ()))


def _norm_matmul(x, g, wt, name):
    T, D = x.shape
    n = wt.shape[0]
    tm = min(T, TOKEN_TILE)

    def body(x_ref, g_ref, w_ref, z_ref, h_ref):
        xv = x_ref[...]
        h = ((xv * _rms_inv(xv)) * g_ref[...]).astype(BF16)
        h_ref[...] = h
        z_ref[...] = _dot_nt(h, w_ref[...])

    return _pcall(
        body, name=name, grid=(T // tm,),
        in_specs=[pl.BlockSpec((tm, D), lambda i: (i, 0)), pl.BlockSpec((1, D), lambda i: (0, 0)),
                  pl.BlockSpec((n, D), lambda i: (0, 0))],
        out_specs=[pl.BlockSpec((tm, n), lambda i: (i, 0)), pl.BlockSpec((tm, D), lambda i: (i, 0))],
        out_shape=[jax.ShapeDtypeStruct((T, n), F32), jax.ShapeDtypeStruct((T, D), BF16)],
        compiler_params=_params(1),
    )(x, g, wt)


def _matmul_residual(y, w, x, name):
    T, D = x.shape
    kdim = y.shape[1]
    tm = min(T, TOKEN_TILE)

    def body(y_ref, w_ref, x_ref, o_ref):
        o_ref[...] = x_ref[...] + _dot(y_ref[...], w_ref[...])

    return _pcall(
        body, name=name, grid=(T // tm,),
        in_specs=[pl.BlockSpec((tm, kdim), lambda i: (i, 0)), pl.BlockSpec((kdim, D), lambda i: (0, 0)),
                  pl.BlockSpec((tm, D), lambda i: (i, 0))],
        out_specs=pl.BlockSpec((tm, D), lambda i: (i, 0)),
        out_shape=jax.ShapeDtypeStruct((T, D), F32),
        compiler_params=_params(1),
    )(y, w, x)


def _matmul_nt(dx, w, name, plan=None):
    T, D = dx.shape
    kdim = w.shape[0]
    tm = min(T, TOKEN_TILE)
    ni = T // tm

    def body(*refs):
        (dx_ref, w_ref), (dy_ref, dxb_ref), _, cr = _unpack(refs, 2, 2, plan)
        i = pl.program_id(0)
        _hook(plan, cr, "start", i == 0)
        dxb = dx_ref[...].astype(BF16)
        dxb_ref[...] = dxb
        dy_ref[...] = _dot_nt(dxb, w_ref[...])
        _hook(plan, cr, "finish", i == ni - 1)

    in_specs, out_specs, out_shape, scratch = _carried(
        plan,
        [pl.BlockSpec((tm, D), lambda i: (i, 0)), pl.BlockSpec((kdim, D), lambda i: (0, 0))],
        [pl.BlockSpec((tm, kdim), lambda i: (i, 0)), pl.BlockSpec((tm, D), lambda i: (i, 0))],
        [jax.ShapeDtypeStruct((T, kdim), F32), jax.ShapeDtypeStruct((T, D), BF16)], [])
    return _pcall(
        body, name=name, grid=(ni,), in_specs=in_specs, out_specs=out_specs, out_shape=out_shape,
        scratch_shapes=scratch, compiler_params=_params(1),
    )(dx, w, *(plan.arrays if plan else ()))


def _matmul_norm_bwd(dz, wt, x, g, dres, name, plan=None):
    T, D = x.shape
    n = dz.shape[1]
    tm = min(T, TOKEN_TILE)
    ni = T // tm

    def body(*refs):
        (dz_ref, w_ref, x_ref, g_ref, dres_ref), (dx_ref, dg_ref), _, cr = _unpack(refs, 5, 2, plan)
        i = pl.program_id(0)
        _hook(plan, cr, "start", i == 0)

        @pl.when(i == 0)
        def _():
            dg_ref[...] = jnp.zeros_like(dg_ref)

        dh = _dot(dz_ref[...], w_ref[...])
        dxn, dg = _norm_bwd(dh, x_ref[...], g_ref[...])
        dx_ref[...] = dres_ref[...] + dxn
        dg_ref[...] += dg
        _hook(plan, cr, "finish", i == ni - 1)

    in_specs, out_specs, out_shape, scratch = _carried(
        plan,
        [pl.BlockSpec((tm, n), lambda i: (i, 0)), pl.BlockSpec((n, D), lambda i: (0, 0)),
         pl.BlockSpec((tm, D), lambda i: (i, 0)), pl.BlockSpec((1, D), lambda i: (0, 0)),
         pl.BlockSpec((tm, D), lambda i: (i, 0))],
        [pl.BlockSpec((tm, D), lambda i: (i, 0)), pl.BlockSpec((1, D), lambda i: (0, 0))],
        [jax.ShapeDtypeStruct((T, D), F32), jax.ShapeDtypeStruct((1, D), F32)], [])
    return _pcall(
        body, name=name, grid=(ni,), in_specs=in_specs, out_specs=out_specs, out_shape=out_shape,
        scratch_shapes=scratch, compiler_params=_params(1),
    )(dz, wt, x, g, dres, *(plan.arrays if plan else ()))


def _loss_head(x, g, target, name):
    T, D = x.shape
    tm = min(T, TOKEN_TILE)

    def body(x_ref, g_ref, t_ref, dx_ref, dg_ref, loss_ref):
        @pl.when(pl.program_id(0) == 0)
        def _():
            dg_ref[...] = jnp.zeros_like(dg_ref)
            loss_ref[...] = jnp.zeros_like(loss_ref)

        xv = x_ref[...]
        gv = g_ref[...]
        out = (xv * _rms_inv(xv)) * gv
        diff = out - t_ref[...]
        loss_ref[...] += 0.5 * jnp.sum(jnp.mean(diff * diff, axis=-1, keepdims=True))
        dxn, dg = _norm_bwd(diff * (1.0 / D), xv, gv)
        dx_ref[...] = dxn
        dg_ref[...] += dg

    return _pcall(
        body, name=name, grid=(T // tm,),
        in_specs=[pl.BlockSpec((tm, D), lambda i: (i, 0)), pl.BlockSpec((1, D), lambda i: (0, 0)),
                  pl.BlockSpec((tm, D), lambda i: (i, 0))],
        out_specs=[pl.BlockSpec((tm, D), lambda i: (i, 0)), pl.BlockSpec((1, D), lambda i: (0, 0)),
                   pl.BlockSpec((1, LANES), lambda i: (0, 0))],
        out_shape=[jax.ShapeDtypeStruct((T, D), F32), jax.ShapeDtypeStruct((1, D), F32),
                   jax.ShapeDtypeStruct((1, LANES), F32)],
        compiler_params=_params(1),
    )(x, g, target)


Z_Q = 3 * CONV_WIDTH
Z_K = Z_Q + N_Q_HEADS * HEAD_DIM
Z_V = Z_K + LANES
Z_END = Z_V + LANES


def _rope_tables(T):
    half = ROT_DIM // 2
    inv_freq = ROPE_THETA ** (-jnp.arange(0, ROT_DIM, 2, dtype=F32) / ROT_DIM)
    ang = inv_freq[:, None] * jnp.arange(T, dtype=F32)[None, :]
    cos_sin = jnp.concatenate([jnp.cos(ang), jnp.sin(ang)], axis=0)
    select = np.zeros((2 * half, 3 * LANES), np.float32)
    const = np.zeros((1, 3 * LANES), np.float32)
    for lane in range(LANES):
        d = lane % HEAD_DIM
        if d < half:
            select[d, lane] = 1.0
            select[half + d, LANES + lane] = -1.0
        elif d < ROT_DIM:
            select[d - half, lane] = 1.0
            select[d, 2 * LANES + lane] = 1.0
        else:
            const[0, lane] = 1.0
    tab = lax.dot_general(cos_sin, jnp.asarray(select), (((0,), (0,)), ((), ())),
                          precision=lax.Precision.HIGHEST, preferred_element_type=F32)
    return tab + jnp.asarray(const)


def _tab3(tab):
    return tab[:, 0:LANES], tab[:, LANES:2 * LANES], tab[:, 2 * LANES:3 * LANES]


def _rot(x, tab):
    c, s1, s2 = _tab3(tab)
    return x * c + pltpu.roll(x, LANES - ROT_DIM // 2, 1) * s1 + pltpu.roll(x, ROT_DIM // 2, 1) * s2


def _rot_t(d, tab):
    c, s1, s2 = _tab3(tab)
    return d * c + pltpu.roll(d * s1, ROT_DIM // 2, 1) + pltpu.roll(d * s2, LANES - ROT_DIM // 2, 1)


def _head_pads(a):
    lo = lax.broadcasted_iota(jnp.int32, a.shape, 1) < HEAD_DIM
    nat0 = jnp.where(lo, a, 0.0)
    nat1 = jnp.where(lo, 0.0, a)
    return {
        (0, 0): nat0.astype(BF16), (0, 1): pltpu.roll(nat0, HEAD_DIM, 1).astype(BF16),
        (1, 0): pltpu.roll(nat1, HEAD_DIM, 1).astype(BF16), (1, 1): nat1.astype(BF16),
    }


def _from_pads(even, odd, kv):
    lo = lax.broadcasted_iota(jnp.int32, even.shape, 1) < HEAD_DIM
    if kv == 0:
        return jnp.where(lo, even + pltpu.roll(odd, HEAD_DIM, 1), 0.0)
    return jnp.where(lo, 0.0, pltpu.roll(even, HEAD_DIM, 1) + odd)


N_GROUPS = 4


def _group_head(g, r):
    kv, par = divmod(g, 2)
    return 2 * (2 * kv + r) + par


def _window_mask_t(has_prev):
    jj = lax.broadcasted_iota(jnp.int32, (2 * BLOCK, 2 * BLOCK), 0)
    ii = lax.broadcasted_iota(jnp.int32, (2 * BLOCK, 2 * BLOCK), 1) & (BLOCK - 1)
    rel = jj - BLOCK - ii
    return (rel <= 0) & (rel > -BLOCK) & ((jj >= BLOCK) | has_prev)


def _sink_row(sink_ref, g):
    lane = lax.broadcasted_iota(jnp.int32, (1, 2 * BLOCK), 1)
    return jnp.where(lane < BLOCK, sink_ref[0, _group_head(g, 0)], sink_ref[0, _group_head(g, 1)])


def _attn_probs_t(q2, kp, mask, sink_ref):
    out = []
    for kv in range(2):
        q_st = jnp.concatenate([q2[2 * kv], q2[2 * kv + 1]], axis=0)
        for par in range(2):
            s = jnp.where(mask, _dot_nt(kp[(kv, par)], q_st) * ATTN_SCALE, MASK_VALUE)
            sink = _sink_row(sink_ref, 2 * kv + par)
            m = jnp.maximum(jnp.max(s, axis=0, keepdims=True), sink)
            p = jnp.exp(s - m)
            esink = jnp.exp(sink - m)
            rden = 1.0 / (jnp.sum(p, axis=0, keepdims=True) + esink)
            out.append((p * rden, esink * rden))
    return out


def _conv_taps(cg, u, cg_prev, u_prev, has_prev):
    vv = cg * u
    halo = jnp.where(has_prev, cg_prev * u_prev, 0.0)
    ext = jnp.concatenate([halo, vv], axis=0)
    rows = ext.shape[0]
    vv1 = pltpu.roll(ext, 1, 0)[8:rows]
    vv2 = pltpu.roll(ext, 2, 0)[8:rows]
    return vv, vv1, vv2


def _mix_specs(nb):
    cur = lambda n: jnp.minimum(n, nb - 1)
    prev = lambda n: jnp.maximum(jnp.minimum(n, nb - 1) - 1, 0)
    rows8_prev = lambda n: jnp.maximum(16 * jnp.minimum(n, nb - 1) - 1, 0)
    return cur, prev, [
        pl.BlockSpec((BLOCK, Z_END), lambda n: (cur(n), 0)),
        pl.BlockSpec((BLOCK, 2 * LANES), lambda n: (prev(n), Z_K // (2 * LANES))),
        pl.BlockSpec((8, CONV_WIDTH), lambda n: (rows8_prev(n), 1)),
        pl.BlockSpec((8, CONV_WIDTH), lambda n: (rows8_prev(n), 2)),
        pl.BlockSpec((BLOCK, 3 * LANES), lambda n: (cur(n), 0)),
        pl.BlockSpec((BLOCK, 3 * LANES), lambda n: (prev(n), 0)),
        pl.BlockSpec((3, CONV_WIDTH), lambda n: (0, 0)),
        pl.BlockSpec(memory_space=pltpu.SMEM),
    ]


def _mix_core_fwd(z, tab, conv_w, sinks, name):
    T = z.shape[0]
    nb = T // BLOCK
    _, _, specs = _mix_specs(nb)

    def body(z_ref, zkvp_ref, cgp_ref, up_ref, tab_ref, tabp_ref, cw_ref, sink_ref, y_ref):
        has_prev = pl.program_id(0) > 0
        bg = z_ref[:, 0:CONV_WIDTH]
        vv, vv1, vv2 = _conv_taps(z_ref[:, CONV_WIDTH:2 * CONV_WIDTH], z_ref[:, 2 * CONV_WIDTH:Z_Q],
                                  cgp_ref[...], up_ref[...], has_prev)
        conv = cw_ref[0:1, :] * vv2 + cw_ref[1:2, :] * vv1 + cw_ref[2:3, :] * vv
        y_ref[:, 0:CONV_WIDTH] = (bg * conv).astype(BF16)

        tab_c = tab_ref[...]
        tab_p = tabp_ref[...]
        k_all = jnp.concatenate([_rot(zkvp_ref[:, 0:LANES], tab_p), _rot(z_ref[:, Z_K:Z_V], tab_c)], axis=0)
        v_all = jnp.concatenate([zkvp_ref[:, LANES:2 * LANES], z_ref[:, Z_V:Z_END]], axis=0)
        kp = _head_pads(k_all)
        vp = _head_pads(v_all)
        q2 = [_rot(z_ref[:, Z_Q + LANES * c:Z_Q + LANES * (c + 1)], tab_c).astype(BF16) for c in range(N_Q_HEADS // 2)]
        probs = _attn_probs_t(q2, kp, _window_mask_t(has_prev), sink_ref)
        for kv in range(2):
            o_t = (_dot_tn(vp[(kv, 0)], probs[2 * kv][0].astype(BF16))
                   + _dot_tn(vp[(kv, 1)], probs[2 * kv + 1][0].astype(BF16)))
            for r in range(2):
                c = 2 * kv + r
                y_ref[:, CONV_WIDTH + LANES * c:CONV_WIDTH + LANES * (c + 1)] = o_t[:, BLOCK * r:BLOCK * (r + 1)].T.astype(BF16)

    return _pcall(
        body, name=name, grid=(nb,), in_specs=specs,
        out_specs=pl.BlockSpec((BLOCK, 2 * CONV_WIDTH), lambda n: (n, 0)),
        out_shape=jax.ShapeDtypeStruct((T, 2 * CONV_WIDTH), BF16),
        compiler_params=_params(1),
    )(z, z, z, z, tab, tab, conv_w, sinks)


def _mix_core_bwd(z, dy, tab, conv_w, sinks, name, plan=None):
    T = z.shape[0]
    nb = T // BLOCK
    cur, _, specs = _mix_specs(nb)
    rows8_next = lambda n: jnp.minimum(16 * (cur(n) + 1), 16 * nb - 1)
    specs = specs[:4] + [
        pl.BlockSpec((8, CONV_WIDTH), lambda n: (rows8_next(n), 0)),
        pl.BlockSpec((BLOCK, 2 * CONV_WIDTH), lambda n: (cur(n), 0)),
        pl.BlockSpec((8, CONV_WIDTH), lambda n: (rows8_next(n), 0)),
    ] + specs[4:]

    def body(*refs):
        ins, outs, scratch, cr = _unpack(refs, 11, 3, plan)
        z_ref, zkvp_ref, cgp_ref, up_ref, bgn_ref, dy_ref, dyn_ref, tab_ref, tabp_ref, cw_ref, sink_ref = ins
        dz_ref, dcw_ref, dsk_ref = outs
        main_ref, kv_ref = scratch
        n = pl.program_id(0)
        _hook(plan, cr, "start", n == 0)

        @pl.when(n == 0)
        def _():
            main_ref[...] = jnp.zeros_like(main_ref)
            kv_ref[...] = jnp.zeros_like(kv_ref)
            dcw_ref[...] = jnp.zeros_like(dcw_ref)
            dsk_ref[...] = jnp.zeros_like(dsk_ref)

        @pl.when(n < nb)
        def _():
            has_prev = n > 0
            has_next = n < nb - 1
            bg = z_ref[:, 0:CONV_WIDTH]
            cg = z_ref[:, CONV_WIDTH:2 * CONV_WIDTH]
            u = z_ref[:, 2 * CONV_WIDTH:Z_Q]
            vv, vv1, vv2 = _conv_taps(cg, u, cgp_ref[...], up_ref[...], has_prev)
            w0, w1, w2 = cw_ref[0:1, :], cw_ref[1:2, :], cw_ref[2:3, :]
            dyc = dy_ref[:, 0:CONV_WIDTH]
            dbg = dyc * (w0 * vv2 + w1 * vv1 + w2 * vv)
            dconv = dyc * bg
            dconv_next = jnp.where(has_next, dyn_ref[...] * bgn_ref[...], 0.0)
            ext = jnp.concatenate([dconv, dconv_next], axis=0)
            rows = ext.shape[0]
            dvv = w2 * dconv + w1 * pltpu.roll(ext, rows - 1, 0)[0:BLOCK] + w0 * pltpu.roll(ext, rows - 2, 0)[0:BLOCK]
            dcw_ref[0:1, :] += jnp.sum(dconv * vv2, axis=0, keepdims=True)
            dcw_ref[1:2, :] += jnp.sum(dconv * vv1, axis=0, keepdims=True)
            dcw_ref[2:3, :] += jnp.sum(dconv * vv, axis=0, keepdims=True)

            tab_c = tab_ref[...]
            tab_p = tabp_ref[...]
            k_all = jnp.concatenate([_rot(zkvp_ref[:, 0:LANES], tab_p), _rot(z_ref[:, Z_K:Z_V], tab_c)], axis=0)
            v_all = jnp.concatenate([zkvp_ref[:, LANES:2 * LANES], z_ref[:, Z_V:Z_END]], axis=0)
            kp = _head_pads(k_all)
            vp = _head_pads(v_all)
            chunks = range(N_Q_HEADS // 2)
            q2 = [_rot(z_ref[:, Z_Q + LANES * c:Z_Q + LANES * (c + 1)], tab_c).astype(BF16) for c in chunks]
            do2 = [dy_ref[:, CONV_WIDTH + LANES * c:CONV_WIDTH + LANES * (c + 1)].astype(BF16) for c in chunks]
            probs = _attn_probs_t(q2, kp, _window_mask_t(has_prev), sink_ref)
            dq_chunks = []
            dk_nat = jnp.zeros((2 * BLOCK, LANES), F32)
            dv_nat = jnp.zeros((2 * BLOCK, LANES), F32)
            for kv in range(2):
                q_st = jnp.concatenate([q2[2 * kv], q2[2 * kv + 1]], axis=0)
                do_st = jnp.concatenate([do2[2 * kv], do2[2 * kv + 1]], axis=0)
                dq_t = jnp.zeros((LANES, 2 * BLOCK), F32)
                dk_par, dv_par = [], []
                for par in range(2):
                    g = 2 * kv + par
                    pr, psink = probs[g]
                    dp = _dot_nt(vp[(kv, par)], do_st)
                    delta = jnp.sum(dp * pr, axis=0, keepdims=True)
                    ds = (pr * (dp - delta) * ATTN_SCALE).astype(BF16)
                    dsink = -psink * delta
                    for r in range(2):
                        h = _group_head(g, r)
                        dsk_ref[h:h + 1, :] += jnp.sum(dsink[:, BLOCK * r:BLOCK * (r + 1)])
                    dq_t = dq_t + _dot_tn(kp[(kv, par)], ds)
                    dk_par.append(_dot(ds, q_st))
                    dv_par.append(_dot(pr.astype(BF16), do_st))
                for r in range(2):
                    dq_chunks.append(_rot_t(dq_t[:, BLOCK * r:BLOCK * (r + 1)].T, tab_c))
                dk_nat = dk_nat + _from_pads(dk_par[0], dk_par[1], kv)
                dv_nat = dv_nat + _from_pads(dv_par[0], dv_par[1], kv)

            dk_prev = _rot_t(kv_ref[:, 0:LANES] + dk_nat[0:BLOCK], tab_p)
            dv_prev = kv_ref[:, LANES:2 * LANES] + dv_nat[0:BLOCK]
            dz_ref[:, 0:Z_K] = main_ref[...]
            dz_ref[:, Z_K:Z_V] = dk_prev.astype(BF16)
            dz_ref[:, Z_V:Z_END] = dv_prev.astype(BF16)
            main_ref[:, 0:CONV_WIDTH] = dbg.astype(BF16)
            main_ref[:, CONV_WIDTH:2 * CONV_WIDTH] = (dvv * u).astype(BF16)
            main_ref[:, 2 * CONV_WIDTH:Z_Q] = (dvv * cg).astype(BF16)
            for c in range(N_Q_HEADS // 2):
                main_ref[:, Z_Q + LANES * c:Z_Q + LANES * (c + 1)] = dq_chunks[c].astype(BF16)
            kv_ref[:, 0:LANES] = dk_nat[BLOCK:2 * BLOCK]
            kv_ref[:, LANES:2 * LANES] = dv_nat[BLOCK:2 * BLOCK]

        @pl.when(n == nb)
        def _():
            dz_ref[:, 0:Z_K] = main_ref[...]
            dz_ref[:, Z_K:Z_V] = _rot_t(kv_ref[:, 0:LANES], tab_ref[...]).astype(BF16)
            dz_ref[:, Z_V:Z_END] = kv_ref[:, LANES:2 * LANES].astype(BF16)

        _hook(plan, cr, "finish", n == nb)

    in_specs, out_specs, out_shape, scratch = _carried(
        plan, specs,
        [pl.BlockSpec((BLOCK, Z_END), lambda n: (jnp.maximum(n - 1, 0), 0)),
         pl.BlockSpec((8, CONV_WIDTH), lambda n: (0, 0)), pl.BlockSpec((8, LANES), lambda n: (0, 0))],
        [jax.ShapeDtypeStruct((T, Z_END), BF16), jax.ShapeDtypeStruct((8, CONV_WIDTH), F32),
         jax.ShapeDtypeStruct((8, LANES), F32)],
        [pltpu.VMEM((BLOCK, Z_K), BF16), pltpu.VMEM((BLOCK, 2 * LANES), F32)])
    return _pcall(
        body, name=name, grid=(nb + 1,), in_specs=in_specs, out_specs=out_specs, out_shape=out_shape,
        scratch_shapes=scratch, compiler_params=_params(1),
    )(z, z, z, z, z, dy, dy, tab, tab, conv_w, sinks, *(plan.arrays if plan else ()))


ROW_SPLIT = 2


def _pair_sum(grads, recvd, core, name):
    n = len(grads)

    def body(core_ref, *refs):
        g, r = refs[:n], refs[n:2 * n]
        s, sb = refs[2 * n:3 * n], refs[3 * n:]
        for t in range(n):
            tot = g[t][...] + r[t][...].astype(F32)
            s[t][...] = tot
            sb[t][...] = tot.astype(BF16)

    def blk(a):
        return (1, a.shape[1] // ROW_SPLIT, a.shape[2])

    in_specs = [pl.BlockSpec(blk(r), lambda q, i, core_ref: (q, core_ref[0] * ROW_SPLIT + i, 0)) for r in recvd]
    in_specs += [pl.BlockSpec(blk(r), lambda q, i, core_ref: (q, i, 0)) for r in recvd]
    out_specs = [pl.BlockSpec(blk(r), lambda q, i, core_ref: (q, i, 0)) for r in recvd] * 2
    return _pcall(
        body, name=name,
        grid_spec=pltpu.PrefetchScalarGridSpec(num_scalar_prefetch=1, grid=(N_CHIPS, ROW_SPLIT),
                                               in_specs=in_specs, out_specs=out_specs),
        out_shape=[jax.ShapeDtypeStruct(r.shape, F32) for r in recvd] + [jax.ShapeDtypeStruct(r.shape, BF16) for r in recvd],
        compiler_params=_params(2),
    )(core, *grads, *recvd)


def _chip_sum(parts, recvd, place, name):
    n = len(parts)

    def body(place_ref, *refs):
        p, r, o = refs[:n], refs[n:2 * n], refs[2 * n:]
        for t in range(n):
            tot = p[t][0]
            for j in range(3):
                tot = tot + r[t][j].astype(F32)
            o[t][...] = tot

    in_specs = [pl.BlockSpec((1, p.shape[1] // ROW_SPLIT, p.shape[2]), lambda i, place_ref: (place_ref[0], i, 0))
                for p in parts]
    in_specs += [pl.BlockSpec((3, r.shape[1] // ROW_SPLIT, r.shape[2]), lambda i, place_ref: (0, i, 0)) for r in recvd]
    out_specs = [pl.BlockSpec((p.shape[1] // ROW_SPLIT, p.shape[2]),
                              lambda i, place_ref: (place_ref[1] * ROW_SPLIT + i, 0)) for p in parts]
    return _pcall(
        body, name=name,
        grid_spec=pltpu.PrefetchScalarGridSpec(num_scalar_prefetch=1, grid=(ROW_SPLIT,),
                                               in_specs=in_specs, out_specs=out_specs),
        out_shape=[jax.ShapeDtypeStruct((2 * p.shape[1], p.shape[2]), F32) for p in parts],
        compiler_params=_params(1),
    )(place, *parts, *recvd)


def _adamw_math(w, g, m, v):
    m = ADAM_B1 * m + (1.0 - ADAM_B1) * g
    v = ADAM_B2 * v + (1.0 - ADAM_B2) * (g * g)
    m_hat = m / (1.0 - ADAM_B1 ** ADAM_STEP)
    v_hat = v / (1.0 - ADAM_B2 ** ADAM_STEP)
    delta = -ADAM_LR * (m_hat / (jnp.sqrt(v_hat) + ADAM_EPS) + ADAM_WD * w)
    return delta, m, v


def _adamw(ws, gs, ms, vs, row_blocks, name):
    n = len(ws)

    def body(*refs):
        w, g, m, v = refs[:n], refs[n:2 * n], refs[2 * n:3 * n], refs[3 * n:4 * n]
        d, mo, vo = refs[4 * n:5 * n], refs[5 * n:6 * n], refs[6 * n:]
        for t in range(n):
            delta, m_new, v_new = _adamw_math(w[t][...], g[t][...], m[t][...], v[t][...])
            d[t][...] = delta
            mo[t][...] = m_new
            vo[t][...] = v_new

    specs = [pl.BlockSpec((a.shape[0] // row_blocks, a.shape[1]), lambda i: (i, 0)) for a in ws]
    shapes = [jax.ShapeDtypeStruct(a.shape, F32) for a in ws]
    return _pcall(
        body, name=name, grid=(row_blocks,), in_specs=specs * 4, out_specs=specs * 3, out_shape=shapes * 3,
        compiler_params=_params(1),
    )(*ws, *gs, *ms, *vs)


def kernel(x, ffn1_norm, ffn1_w_gate, ffn1_w_up, ffn1_w_down, mix_norm, w_in, conv_w, attn_sinks, w_out, ffn2_norm, ffn2_w_gate, ffn2_w_up, ffn2_w_down, final_norm, loss_target, m_ffn1_norm, m_ffn1_w_gate, m_ffn1_w_up, m_ffn1_w_down, m_mix_norm, m_w_in, m_conv_w, m_attn_sinks, m_w_out, m_ffn2_norm, m_ffn2_w_gate, m_ffn2_w_up, m_ffn2_w_down, m_final_norm, v_ffn1_norm, v_ffn1_w_gate, v_ffn1_w_up, v_ffn1_w_down, v_mix_norm, v_w_in, v_conv_w, v_attn_sinks, v_w_out, v_ffn2_norm, v_ffn2_w_gate, v_ffn2_w_up, v_ffn2_w_down, v_final_norm):
    T, D = x.shape[1], x.shape[2]
    chip = (2 * lax.axis_index("x") + lax.axis_index("y")).astype(jnp.int32)
    core = lax.axis_index("c").astype(jnp.int32)
    place = jnp.stack([chip, core])
    x0 = x[0]
    target = loss_target[0]
    gf = final_norm.reshape(1, D)

    tr = lambda w: jnp.swapaxes(w[0], 0, 1)
    big = [tr(ffn1_w_gate), tr(ffn1_w_up), ffn1_w_down[0], tr(w_in), w_out[0], tr(ffn2_w_gate), tr(ffn2_w_up), ffn2_w_down[0]]
    transposed = [True, True, False, True, False, True, True, False]
    own_b = [w.astype(BF16) for w in big]

    def whole(gathered, own):
        return lax.dynamic_update_slice(gathered, own[None], (chip, 0, 0)).reshape(-1, D)

    got1 = _run_comm(_gather_plan(own_b[0:3]), "gather_ffn1")
    wg1, wu1, wd1 = (whole(g, o) for g, o in zip(got1, own_b[0:3]))
    tab = _rope_tables(T)

    res = _ffn_fwd(x0, ffn1_norm, wg1, wu1, wd1, "ffn1_fwd", _gather_plan(own_b[3:8], [conv_w[0]]))
    x1, h1, gate1, up1, act1 = res[:5]
    win, wout, wg2, wu2, wd2 = (whole(g, o) for g, o in zip(res[5:10], own_b[3:8]))
    convw4 = lax.dynamic_update_slice(res[10], conv_w, (chip, 0, 0))
    convw = jnp.transpose(convw4, (1, 0, 2)).reshape(3, -1)
    z, hm = _norm_matmul(x1, mix_norm, win, "mix_in_fwd")
    ymix = _mix_core_fwd(z, tab, convw, attn_sinks, "mix_core_fwd")
    x2 = _matmul_residual(ymix, wout, x1, "mix_out_fwd")
    x3, h2, gate2, up2, act2 = _ffn_fwd(x2, ffn2_norm, wg2, wu2, wd2, "ffn2_fwd")
    dx3, dgf, loss_part = _loss_head(x3, gf, target, "loss_head")

    def quarters(pairs):
        return [p[0].reshape(N_CHIPS, -1, D) for p in pairs], [p[1].reshape(N_CHIPS, -1, D) for p in pairs]

    def split(res, n):
        return res[:n], res[n:]

    dx2, dyb2, dgate2, dup2, dg2 = _ffn_bwd(dx3, x2, ffn2_norm, gate2, up2, wg2, wu2, wd2, "ffn2_bwd")
    g2, g2b = quarters([_matmul_tn(dgate2, h2, F_SPLIT, "ffn2_dwg"), _matmul_tn(dup2, h2, F_SPLIT, "ffn2_dwu"),
                        _matmul_tn(act2, dyb2, F_SPLIT, "ffn2_dwd")])
    (dymix, dx2b), sib2 = split(_matmul_nt(dx2, wout, "mix_out_bwd", _sibling_plan(g2b)), 2)
    gwout = _matmul_tn(ymix, dx2b, F_SPLIT, "mix_dwout")
    pair2 = _pair_sum(g2, sib2, core.reshape(1), "pair_sum_ffn2")
    (dz, dcw, dsk), chips2 = split(_mix_core_bwd(z, dymix, tab, convw, attn_sinks, "mix_core_bwd", _scatter_plan(pair2[3:])), 3)
    half2 = _chip_sum(pair2[:3], chips2, place, "chip_sum_ffn2")
    gwin = _matmul_tn(dz, hm, F_SPLIT, "mix_dwin")
    gm, gmb = quarters([gwin, gwout])
    (dx1, dgm), sibm = split(_matmul_norm_bwd(dz, win, x1, mix_norm, dx2, "mix_in_bwd", _sibling_plan(gmb)), 2)
    pairm = _pair_sum(gm, sibm, core.reshape(1), "pair_sum_mix")
    dx0, dyb1, dgate1, dup1, dg1 = _ffn_bwd(dx1, x0, ffn1_norm, gate1, up1, wg1, wu1, wd1, "ffn1_bwd")
    dwg1, chipsm = split(_matmul_tn(dgate1, h1, F_SPLIT, "ffn1_dwg", _scatter_plan(pairm[2:])), 2)
    halfm = _chip_sum(pairm[:2], chipsm, place, "chip_sum_mix")
    gg, ggb = quarters([dwg1])
    dwu1, sibg = split(_matmul_tn(dup1, h1, F_SPLIT, "ffn1_dwu", _sibling_plan(ggb)), 2)
    pairg = _pair_sum(gg, sibg, core.reshape(1), "pair_sum_ffn1_gate")
    dwd1, chipsg = split(_matmul_tn(act1, dyb1, F_SPLIT, "ffn1_dwd", _scatter_plan(pairg[1:])), 2)
    halfg = _chip_sum(pairg[:1], chipsg, place, "chip_sum_ffn1_gate")
    g1, g1b = quarters([dwu1, dwd1])
    sib1 = _run_comm(_sibling_plan(g1b), "grads1_to_sibling")
    pair1 = _pair_sum(g1, sib1, core.reshape(1), "pair_sum_ffn1")
    chips1 = _run_comm(_scatter_plan(pair1[2:]), "grads1_to_chips")
    half1 = _chip_sum(pair1[:2], chips1, place, "chip_sum_ffn1")
    g_big = _join_halves([*halfg, *half1, *halfm, *half2], "grads_join")

    pad = lambda a: jnp.pad(a, ((0, 0), (0, LANES - a.shape[1])))
    vec = jnp.concatenate([dg1, dgm, dg2, dgf, dcw[0:3].reshape(1, -1), pad(dsk[:, 0].reshape(1, -1)),
                           pad(loss_part[:, 0:1])], axis=1)
    total = _all_reduce_small(jnp.pad(vec, ((0, 7), (0, 0))), "small_all_reduce")[0:1]
    g_n1, g_nm, g_n2, g_nf = (total[:, k * D:(k + 1) * D] for k in range(4))
    cw_full = total[:, 4 * D:4 * D + 3 * CONV_WIDTH].reshape(3, CONV_WIDTH)
    cq = CONV_WIDTH // N_CHIPS
    g_cw = lax.dynamic_slice(cw_full, (0, chip * cq), (3, cq))
    off = 4 * D + 3 * CONV_WIDTH
    g_sk = total[:, off:off + N_Q_HEADS]
    loss = total[0, off + LANES]

    ws = big
    ms = [tr(m_ffn1_w_gate), tr(m_ffn1_w_up), m_ffn1_w_down[0], tr(m_w_in), m_w_out[0], tr(m_ffn2_w_gate), tr(m_ffn2_w_up), m_ffn2_w_down[0]]
    vs = [tr(v_ffn1_w_gate), tr(v_ffn1_w_up), v_ffn1_w_down[0], tr(v_w_in), v_w_out[0], tr(v_ffn2_w_gate), tr(v_ffn2_w_up), v_ffn2_w_down[0]]
    upd = {}
    for name_, idx in (("adamw_a", [0, 1, 2, 4]), ("adamw_b", [3, 5, 6, 7])):
        res = _adamw([ws[i] for i in idx], [g_big[i] for i in idx], [ms[i] for i in idx], [vs[i] for i in idx], 8, name_)
        k = len(idx)
        for j, i in enumerate(idx):
            upd[i] = (res[j], res[k + j], res[2 * k + j])
    sw = [ffn1_norm, mix_norm, conv_w[0], attn_sinks, ffn2_norm, gf]
    sg = [g_n1, g_nm, g_cw, g_sk, g_n2, g_nf]
    sm = [m_ffn1_norm, m_mix_norm, m_conv_w[0], m_attn_sinks, m_ffn2_norm, m_final_norm.reshape(1, D)]
    sv = [v_ffn1_norm, v_mix_norm, v_conv_w[0], v_attn_sinks, v_ffn2_norm, v_final_norm.reshape(1, D)]
    sres = _adamw(sw, sg, sm, sv, 1, "adamw_small")
    supd = [(sres[j], sres[6 + j], sres[12 + j]) for j in range(6)]

    order = [("s", 0), ("b", 0), ("b", 1), ("b", 2), ("s", 1), ("b", 3), ("s", 2), ("s", 3), ("b", 4),
             ("s", 4), ("b", 5), ("b", 6), ("b", 7), ("s", 5)]

    def leaf(kind, i, which):
        if kind == "b":
            a = g_big[i] if which == 0 else upd[i][which - 1]
            return (jnp.swapaxes(a, 0, 1) if transposed[i] else a)[None]
        a = sg[i] if which == 0 else supd[i][which - 1]
        if i == 2:
            return a[None]
        if i == 5:
            return a.reshape(D)
        return a

    outs = [loss, dx0[None]]
    for which in range(4):
        outs += [leaf(kind, i, which) for kind, i in order]
    return tuple(outs)
```

```python
import functools

import jax
import jax.numpy as jnp
import numpy as np
from jax import lax
from jax.experimental import pallas as pl
from jax.experimental.pallas import tpu as pltpu

F32 = jnp.float32
BF16 = jnp.bfloat16
MESH = pl.DeviceIdType.MESH

CONV_WIDTH = 512
N_Q_HEADS = 8
HEAD_DIM = 64
BLOCK = 128
ROPE_THETA = 500000.0
ROT_DIM = 16
RMS_EPS = 1e-5
MASK_VALUE = -1e30
ATTN_SCALE = HEAD_DIM ** -0.5
FFN_RES_SCALE = 0.5
ADAM_LR = 0.001
ADAM_B1 = 0.9
ADAM_B2 = 0.999
ADAM_EPS = 1e-08
ADAM_WD = 0.01
ADAM_STEP = 10

N_CHIPS = 4
N_DEV = 8
LANES = 128
VMEM_LIMIT = 56 * 1024 * 1024

_pcall = pl.pallas_call
HBM_SPEC = pl.BlockSpec(memory_space=pltpu.HBM)
ANY_SPEC = pl.BlockSpec(memory_space=pl.ANY)


def _params(n_axes, vmem=VMEM_LIMIT):
    return pltpu.CompilerParams(dimension_semantics=("arbitrary",) * n_axes, vmem_limit_bytes=vmem)


def _dot(a, b):
    return jnp.dot(a, b, preferred_element_type=F32)


def _dot_nt(a, b):
    return lax.dot_general(a, b, (((1,), (1,)), ((), ())), preferred_element_type=F32)


def _dot_tn(a, b):
    return lax.dot_general(a, b, (((0,), (0,)), ((), ())), preferred_element_type=F32)


def _rms_inv(x):
    return lax.rsqrt(jnp.mean(x * x, axis=-1, keepdims=True) + RMS_EPS)


def _norm_bwd(dh, x, g):
    inv = _rms_inv(x)
    xhat = x * inv
    dg = jnp.sum(dh * xhat, axis=0, keepdims=True)
    dxhat = dh * g
    dx = inv * (dxhat - xhat * jnp.mean(dxhat * xhat, axis=-1, keepdims=True))
    return dx, dg


def _place():
    x, y, c = lax.axis_index("x"), lax.axis_index("y"), lax.axis_index("c")
    chips = [(1 - x, y), (x, 1 - y), (1 - x, 1 - y)]
    return x, y, c, chips


class _Plan:
    def __init__(self, arrays, out_shapes, n_sems, start, finish, middle=None):
        self.arrays, self.out_shapes, self.n_sems = list(arrays), list(out_shapes), n_sems
        self.start, self.finish, self.middle = start, finish, middle

    def specs(self):
        k = len(self.arrays)
        sems = [pltpu.SemaphoreType.DMA((self.n_sems,)), pltpu.SemaphoreType.DMA((self.n_sems,))]
        return [HBM_SPEC] * k, [HBM_SPEC] * len(self.out_shapes), self.out_shapes, sems


def _sibling_plan(grads_b):
    n = len(grads_b)

    def copies(ins, outs, send_sems, recv_sems):
        x, y, c, _ = _place()

        def copy(t):
            half = ins[t].shape[1] // 2
            return pltpu.make_async_remote_copy(
                src_ref=ins[t].at[:, pl.ds(pl.multiple_of((1 - c) * half, 16), half), :], dst_ref=outs[t],
                send_sem=send_sems.at[t], recv_sem=recv_sems.at[t], device_id=(x, y, 1 - c), device_id_type=MESH)

        return [copy(t) for t in range(n)]

    def start(*refs):
        for cp in copies(*refs):
            cp.start()

    def finish(*refs):
        for cp in copies(*refs):
            cp.wait()

    shapes = [jax.ShapeDtypeStruct((g.shape[0], g.shape[1] // 2, g.shape[2]), g.dtype) for g in grads_b]
    return _Plan(grads_b, shapes, n, start, finish)


def _scatter_plan(parts_b):
    n = len(parts_b)

    def copies(ins, outs, send_sems, recv_sems):
        x, y, c, chips = _place()

        def copy(t, j):
            px, py = chips[j]
            return pltpu.make_async_remote_copy(
                src_ref=ins[t].at[2 * px + py], dst_ref=outs[t].at[j], send_sem=send_sems.at[3 * t + j],
                recv_sem=recv_sems.at[3 * t + j], device_id=(px, py, c), device_id_type=MESH)

        return [copy(t, j) for t in range(n) for j in range(3)]

    def start(*refs):
        for cp in copies(*refs):
            cp.start()

    def finish(*refs):
        for cp in copies(*refs):
            cp.wait()

    shapes = [jax.ShapeDtypeStruct((3, *p.shape[1:]), p.dtype) for p in parts_b]
    return _Plan(parts_b, shapes, 3 * n, start, finish)


def _gather_plan(shards, small=()):
    n, ns = len(shards), len(small)

    def parts(ins, outs, send_sems, recv_sems):
        x, y, c, chips = _place()
        me = 2 * x + y

        def rows(t, core):
            half = ins[t].shape[0] // 2
            return pl.ds(pl.multiple_of(core * half, 16), half)

        def first(t, j, block, core):
            return pltpu.make_async_remote_copy(
                src_ref=ins[t].at[rows(t, core), :], dst_ref=outs[t].at[block, rows(t, core), :],
                send_sem=send_sems.at[6 * t + j], recv_sem=recv_sems.at[6 * t + j],
                device_id=(*chips[j], c), device_id_type=MESH)

        def passed(t, j, block, core):
            ref = outs[t].at[block, rows(t, core), :]
            return pltpu.make_async_remote_copy(
                src_ref=ref, dst_ref=ref, send_sem=send_sems.at[6 * t + 3 + j], recv_sem=recv_sems.at[6 * t + 3 + j],
                device_id=(x, y, 1 - c), device_id_type=MESH)

        def whole(s, j, block):
            k = 6 * n + 3 * s + j
            return pltpu.make_async_remote_copy(
                src_ref=ins[n + s], dst_ref=outs[n + s].at[block], send_sem=send_sems.at[k], recv_sem=recv_sems.at[k],
                device_id=(*chips[j], c), device_id_type=MESH)

        blocks = [2 * px + py for px, py in chips]
        return c, me, blocks, first, passed, whole

    def start(*refs):
        c, me, _, first, _, whole = parts(*refs)
        for t in range(n):
            for j in range(3):
                first(t, j, me, c).start()
        for s in range(ns):
            for j in range(3):
                whole(s, j, me).start()

    def middle(*refs):
        c, _, blocks, first, passed, _ = parts(*refs)
        for t in range(n):
            for j in range(3):
                first(t, j, blocks[j], c).wait_recv()
                passed(t, j, blocks[j], c).start()

    def finish(*refs):
        c, me, blocks, first, passed, whole = parts(*refs)
        for t in range(n):
            for j in range(3):
                passed(t, j, blocks[j], 1 - c).wait_recv()
        for s in range(ns):
            for j in range(3):
                whole(s, j, blocks[j]).wait_recv()
        for t in range(n):
            for j in range(3):
                first(t, j, me, c).wait_send()
                passed(t, j, blocks[j], c).wait_send()
        for s in range(ns):
            for j in range(3):
                whole(s, j, me).wait_send()

    arrays = [*shards, *small]
    shapes = [jax.ShapeDtypeStruct((N_CHIPS, *a.shape), a.dtype) for a in arrays]
    return _Plan(arrays, shapes, 6 * n + 3 * ns, start, finish, middle)


def _run_comm(plan, name):
    k = len(plan.arrays)
    in_specs, out_specs, out_shape, sems = plan.specs()

    def body(*refs):
        cr = (refs[:k], refs[k:k + len(out_shape)], refs[-2], refs[-1])
        plan.start(*cr)
        if plan.middle is not None:
            plan.middle(*cr)
        plan.finish(*cr)

    return _pcall(body, name=name, in_specs=in_specs, out_specs=out_specs, out_shape=out_shape,
                  scratch_shapes=sems)(*plan.arrays)


def _carried(plan, in_specs, out_specs, out_shape, scratch):
    if plan is None:
        return in_specs, out_specs, out_shape, scratch
    p_in, p_out, p_shape, p_sems = plan.specs()
    return in_specs + p_in, out_specs + p_out, out_shape + p_shape, scratch + p_sems


def _unpack(refs, n_in, n_out, plan):
    k_in = len(plan.arrays) if plan else 0
    k_out = len(plan.out_shapes) if plan else 0
    ins = refs[:n_in]
    outs = refs[n_in + k_in:n_in + k_in + n_out]
    rest = refs[n_in + k_in + n_out + k_out:]
    if plan is None:
        return ins, outs, rest, None
    cr = (refs[n_in:n_in + k_in], refs[n_in + k_in + n_out:n_in + k_in + n_out + k_out], rest[-2], rest[-1])
    return ins, outs, rest[:-2], cr


def _hook(plan, cr, which, cond):
    fn = getattr(plan, which) if plan is not None else None
    if fn is not None:
        pl.when(cond)(lambda: fn(*cr))


def _join_halves(shards, name):
    n = len(shards)

    def body(*refs):
        ins, outs = refs[:n], refs[n:2 * n]
        send_sems, recv_sems = refs[2 * n:]
        x, y, c, _ = _place()

        def copy(t, core):
            half = ins[t].shape[0] // 2
            rows = pl.ds(pl.multiple_of(core * half, 8), half)
            return pltpu.make_async_remote_copy(
                src_ref=ins[t].at[rows, :], dst_ref=outs[t].at[rows, :], send_sem=send_sems.at[t],
                recv_sem=recv_sems.at[t], device_id=(x, y, 1 - c), device_id_type=MESH)

        sends = [copy(t, c) for t in range(n)]
        for cp in sends:
            cp.start()
        for t in range(n):
            copy(t, 1 - c).wait_recv()
        for cp in sends:
            cp.wait_send()

    return _pcall(
        body, name=name, in_specs=[HBM_SPEC] * n, out_specs=[HBM_SPEC] * n,
        out_shape=[jax.ShapeDtypeStruct(s.shape, s.dtype) for s in shards],
        input_output_aliases={t: t for t in range(n)},
        scratch_shapes=[pltpu.SemaphoreType.DMA((n,)), pltpu.SemaphoreType.DMA((n,))],
    )(*shards)


def _all_reduce_small(vec, name):
    s = vec.shape[1]

    def body(v_ref, o_ref, all_ref, send_sems, recv_sems, local_sem):
        x, y, c, _ = _place()
        me = 4 * x + 2 * y + c

        def rows(dev):
            return all_ref.at[pl.ds(pl.multiple_of(8 * dev, 8), 8), :]

        mine = pltpu.make_async_copy(v_ref, rows(me), local_sem)
        mine.start()
        rel = [((k >> 2) & 1, (k >> 1) & 1, k & 1) for k in range(1, N_DEV)]

        def peer(k):
            fx, fy, fc = rel[k]
            return (x ^ fx, y ^ fy, c ^ fc)

        def copy(k, dev):
            return pltpu.make_async_remote_copy(
                src_ref=v_ref, dst_ref=rows(dev), send_sem=send_sems.at[k], recv_sem=recv_sems.at[k],
                device_id=peer(k), device_id_type=MESH)

        sends = [copy(k, me) for k in range(N_DEV - 1)]
        for cp in sends:
            cp.start()
        for k in range(N_DEV - 1):
            px, py, pc = peer(k)
            copy(k, 4 * px + 2 * py + pc).wait_recv()
        for cp in sends:
            cp.wait_send()
        mine.wait()
        total = all_ref[0:8, :]
        for dev in range(1, N_DEV):
            total = total + all_ref[8 * dev:8 * dev + 8, :]
        o_ref[...] = total

    return _pcall(
        body, name=name,
        in_specs=[pl.BlockSpec(memory_space=pltpu.VMEM)], out_specs=pl.BlockSpec(memory_space=pltpu.VMEM),
        out_shape=jax.ShapeDtypeStruct((8, s), F32),
        scratch_shapes=[pltpu.VMEM((8 * N_DEV, s), F32), pltpu.SemaphoreType.DMA((N_DEV - 1,)),
                        pltpu.SemaphoreType.DMA((N_DEV - 1,)), pltpu.SemaphoreType.DMA],
    )(vec)


TOKEN_TILE = 512
BWD_TOKEN_TILE = 256
DW_TOKEN_TILE = 2048
DW_ROW_SPLIT = 2
MXU_COLS = 256


def _chunks(n):
    out, c0 = [], 0
    while c0 < n:
        size = min(MXU_COLS, n - c0)
        out.append((c0, size))
        c0 += size
    return out


def _load_weights(hbm_refs, vmem_refs, sems):
    copies = [pltpu.make_async_copy(h, v, sems.at[k]) for k, (h, v) in enumerate(zip(hbm_refs, vmem_refs))]
    for cp in copies:
        cp.start()
    for cp in copies:
        cp.wait()


def _ffn_fwd(x, g, wgt, wut, wd, name, plan=None):
    T, D = x.shape
    F = wgt.shape[0]
    tm = min(T, TOKEN_TILE)
    ni = T // tm

    def body(*refs):
        (x_ref, g_ref, wg_hbm, wu_hbm, wd_hbm), (xo_ref, h_ref, gate_ref, up_ref, act_ref), scratch, cr = _unpack(refs, 5, 5, plan)
        wg_ref, wu_ref, wd_ref, sems = scratch
        i = pl.program_id(0)
        _hook(plan, cr, "start", i == 0)

        @pl.when(i == 0)
        def _():
            _load_weights((wg_hbm, wu_hbm, wd_hbm), (wg_ref, wu_ref, wd_ref), sems)

        xv = x_ref[...]
        h = ((xv * _rms_inv(xv)) * g_ref[...]).astype(BF16)
        h_ref[...] = h
        for c0, size in _chunks(F):
            gate = _dot_nt(h, wg_ref[c0:c0 + size, :])
            up = _dot_nt(h, wu_ref[c0:c0 + size, :])
            gate_ref[:, c0:c0 + size] = gate.astype(BF16)
            up_ref[:, c0:c0 + size] = up.astype(BF16)
            act_ref[:, c0:c0 + size] = (gate * jax.nn.sigmoid(gate) * up).astype(BF16)
        xo_ref[...] = x_ref[...] + FFN_RES_SCALE * _dot(act_ref[...], wd_ref[...])
        _hook(plan, cr, "middle", i == (3 * ni) // 4)
        _hook(plan, cr, "finish", i == ni - 1)

    in_specs, out_specs, out_shape, scratch = _carried(
        plan,
        [pl.BlockSpec((tm, D), lambda i: (i, 0)), pl.BlockSpec((1, D), lambda i: (0, 0)),
         ANY_SPEC, ANY_SPEC, ANY_SPEC],
        [pl.BlockSpec((tm, D), lambda i: (i, 0)), pl.BlockSpec((tm, D), lambda i: (i, 0)),
         pl.BlockSpec((tm, F), lambda i: (i, 0)), pl.BlockSpec((tm, F), lambda i: (i, 0)),
         pl.BlockSpec((tm, F), lambda i: (i, 0))],
        [jax.ShapeDtypeStruct((T, D), F32), jax.ShapeDtypeStruct((T, D), BF16),
         jax.ShapeDtypeStruct((T, F), BF16), jax.ShapeDtypeStruct((T, F), BF16), jax.ShapeDtypeStruct((T, F), BF16)],
        [pltpu.VMEM((F, D), BF16), pltpu.VMEM((F, D), BF16), pltpu.VMEM((F, D), BF16),
         pltpu.SemaphoreType.DMA((3,))])
    return _pcall(
        body, name=name, grid=(ni,), in_specs=in_specs, out_specs=out_specs, out_shape=out_shape,
        scratch_shapes=scratch, compiler_params=_params(1),
    )(x, g, wgt, wut, wd, *(plan.arrays if plan else ()))


def _ffn_bwd(dy, x, g, gate, up, wgt, wut, wd, name, plan=None):
    T, D = x.shape
    F = wgt.shape[0]
    tm = min(T, BWD_TOKEN_TILE)
    ni = T // tm

    def body(*refs):
        ins, outs, scratch, cr = _unpack(refs, 8, 5, plan)
        dy_ref, x_ref, g_ref, gate_ref, up_ref, wg_hbm, wu_hbm, wd_hbm = ins
        dx_ref, dyb_ref, dgate_ref, dup_ref, dg_ref = outs
        wg_ref, wu_ref, wd_ref, sems = scratch
        i = pl.program_id(0)
        _hook(plan, cr, "start", i == 0)

        @pl.when(i == 0)
        def _():
            _load_weights((wg_hbm, wu_hbm, wd_hbm), (wg_ref, wu_ref, wd_ref), sems)
            dg_ref[...] = jnp.zeros_like(dg_ref)

        dyb = (FFN_RES_SCALE * dy_ref[...]).astype(BF16)
        dyb_ref[...] = dyb
        for c0, size in _chunks(F):
            dact = _dot_nt(dyb, wd_ref[c0:c0 + size, :])
            gt = gate_ref[:, c0:c0 + size].astype(F32)
            u = up_ref[:, c0:c0 + size].astype(F32)
            sig = jax.nn.sigmoid(gt)
            dup_ref[:, c0:c0 + size] = (dact * (gt * sig)).astype(BF16)
            dgate_ref[:, c0:c0 + size] = (dact * u * (sig * (1.0 + gt * (1.0 - sig)))).astype(BF16)
        dh = _dot(dgate_ref[...], wg_ref[...]) + _dot(dup_ref[...], wu_ref[...])
        dxn, dg = _norm_bwd(dh, x_ref[...], g_ref[...])
        dx_ref[...] = dy_ref[...] + dxn
        dg_ref[...] += dg
        _hook(plan, cr, "finish", i == ni - 1)

    in_specs, out_specs, out_shape, scratch = _carried(
        plan,
        [pl.BlockSpec((tm, D), lambda i: (i, 0)), pl.BlockSpec((tm, D), lambda i: (i, 0)),
         pl.BlockSpec((1, D), lambda i: (0, 0)),
         pl.BlockSpec((tm, F), lambda i: (i, 0)), pl.BlockSpec((tm, F), lambda i: (i, 0)),
         ANY_SPEC, ANY_SPEC, ANY_SPEC],
        [pl.BlockSpec((tm, D), lambda i: (i, 0)), pl.BlockSpec((tm, D), lambda i: (i, 0)),
         pl.BlockSpec((tm, F), lambda i: (i, 0)), pl.BlockSpec((tm, F), lambda i: (i, 0)),
         pl.BlockSpec((1, D), lambda i: (0, 0))],
        [jax.ShapeDtypeStruct((T, D), F32), jax.ShapeDtypeStruct((T, D), BF16),
         jax.ShapeDtypeStruct((T, F), BF16), jax.ShapeDtypeStruct((T, F), BF16), jax.ShapeDtypeStruct((1, D), F32)],
        [pltpu.VMEM((F, D), BF16), pltpu.VMEM((F, D), BF16), pltpu.VMEM((F, D), BF16),
         pltpu.SemaphoreType.DMA((3,))])
    return _pcall(
        body, name=name, grid=(ni,), in_specs=in_specs, out_specs=out_specs, out_shape=out_shape,
        scratch_shapes=scratch, compiler_params=_params(1),
    )(dy, x, g, gate, up, wgt, wut, wd, *(plan.arrays if plan else ()))


def _matmul_tn(a, b, row_split, name, plan=None):
    T, n1 = a.shape
    n2 = b.shape[1]
    tn = n1 // row_split
    tk = min(T, DW_TOKEN_TILE)
    nk = T // tk

    def body(*refs):
        (a_ref, b_ref), (o_ref, ob_ref), _, cr = _unpack(refs, 2, 2, plan)
        j = pl.program_id(0)
        k = pl.program_id(1)
        _hook(plan, cr, "start", jnp.logical_and(j == 0, k == 0))

        @pl.when(k == 0)
        def _():
            o_ref[...] = jnp.zeros_like(o_ref)

        o_ref[...] += _dot_tn(a_ref[...], b_ref[...])

        @pl.when(k == nk - 1)
        def _():
            ob_ref[...] = o_ref[...].astype(BF16)

        _hook(plan, cr, "finish", jnp.logical_and(j == row_split - 1, k == nk - 1))

    in_specs, out_specs, out_shape, scratch = _carried(
        plan,
        [pl.BlockSpec((tk, tn), lambda j, k: (k, j)), pl.BlockSpec((tk, n2), lambda j, k: (k, 0))],
        [pl.BlockSpec((tn, n2), lambda j, k: (j, 0)), pl.BlockSpec((tn, n2), lambda j, k: (j, 0))],
        [jax.ShapeDtypeStruct((n1, n2), F32), jax.ShapeDtypeStruct((n1, n2), BF16)], [])
    return _pcall(
        body, name=name, grid=(row_split, nk), in_specs=in_specs, out_specs=out_specs, out_shape=out_shape,
        scratch_shapes=scratch, compiler_params=_params(2),
    )(a, b, *(plan.arrays if plan else ()))


def _norm_matmul(x, g, wt, name):
    T, D = x.shape
    n = wt.shape[0]
    tm = min(T, TOKEN_TILE)

    def body(x_ref, g_ref, w_ref, z_ref, h_ref):
        xv = x_ref[...]
        h = ((xv * _rms_inv(xv)) * g_ref[...]).astype(BF16)
        h_ref[...] = h
        z_ref[...] = _dot_nt(h, w_ref[...])

    return _pcall(
        body, name=name, grid=(T // tm,),
        in_specs=[pl.BlockSpec((tm, D), lambda i: (i, 0)), pl.BlockSpec((1, D), lambda i: (0, 0)),
                  pl.BlockSpec((n, D), lambda i: (0, 0))],
        out_specs=[pl.BlockSpec((tm, n), lambda i: (i, 0)), pl.BlockSpec((tm, D), lambda i: (i, 0))],
        out_shape=[jax.ShapeDtypeStruct((T, n), F32), jax.ShapeDtypeStruct((T, D), BF16)],
        compiler_params=_params(1),
    )(x, g, wt)


def _matmul_residual(y, w, x, name):
    T, D = x.shape
    kdim = y.shape[1]
    tm = min(T, TOKEN_TILE)

    def body(y_ref, w_ref, x_ref, o_ref):
        o_ref[...] = x_ref[...] + _dot(y_ref[...], w_ref[...])

    return _pcall(
        body, name=name, grid=(T // tm,),
        in_specs=[pl.BlockSpec((tm, kdim), lambda i: (i, 0)), pl.BlockSpec((kdim, D), lambda i: (0, 0)),
                  pl.BlockSpec((tm, D), lambda i: (i, 0))],
        out_specs=pl.BlockSpec((tm, D), lambda i: (i, 0)),
        out_shape=jax.ShapeDtypeStruct((T, D), F32),
        compiler_params=_params(1),
    )(y, w, x)


def _matmul_nt(dx, w, name, plan=None):
    T, D = dx.shape
    kdim = w.shape[0]
    tm = min(T, TOKEN_TILE)
    ni = T // tm

    def body(*refs):
        (dx_ref, w_ref), (dy_ref, dxb_ref), _, cr = _unpack(refs, 2, 2, plan)
        i = pl.program_id(0)
        _hook(plan, cr, "start", i == 0)
        dxb = dx_ref[...].astype(BF16)
        dxb_ref[...] = dxb
        dy_ref[...] = _dot_nt(dxb, w_ref[...])
        _hook(plan, cr, "finish", i == ni - 1)

    in_specs, out_specs, out_shape, scratch = _carried(
        plan,
        [pl.BlockSpec((tm, D), lambda i: (i, 0)), pl.BlockSpec((kdim, D), lambda i: (0, 0))],
        [pl.BlockSpec((tm, kdim), lambda i: (i, 0)), pl.BlockSpec((tm, D), lambda i: (i, 0))],
        [jax.ShapeDtypeStruct((T, kdim), F32), jax.ShapeDtypeStruct((T, D), BF16)], [])
    return _pcall(
        body, name=name, grid=(ni,), in_specs=in_specs, out_specs=out_specs, out_shape=out_shape,
        scratch_shapes=scratch, compiler_params=_params(1),
    )(dx, w, *(plan.arrays if plan else ()))


def _matmul_norm_bwd(dz, wt, x, g, dres, name, plan=None):
    T, D = x.shape
    n = dz.shape[1]
    tm = min(T, TOKEN_TILE)
    ni = T // tm

    def body(*refs):
        (dz_ref, w_ref, x_ref, g_ref, dres_ref), (dx_ref, dg_ref), _, cr = _unpack(refs, 5, 2, plan)
        i = pl.program_id(0)
        _hook(plan, cr, "start", i == 0)

        @pl.when(i == 0)
        def _():
            dg_ref[...] = jnp.zeros_like(dg_ref)

        dh = _dot(dz_ref[...], w_ref[...])
        dxn, dg = _norm_bwd(dh, x_ref[...], g_ref[...])
        dx_ref[...] = dres_ref[...] + dxn
        dg_ref[...] += dg
        _hook(plan, cr, "finish", i == ni - 1)

    in_specs, out_specs, out_shape, scratch = _carried(
        plan,
        [pl.BlockSpec((tm, n), lambda i: (i, 0)), pl.BlockSpec((n, D), lambda i: (0, 0)),
         pl.BlockSpec((tm, D), lambda i: (i, 0)), pl.BlockSpec((1, D), lambda i: (0, 0)),
         pl.BlockSpec((tm, D), lambda i: (i, 0))],
        [pl.BlockSpec((tm, D), lambda i: (i, 0)), pl.BlockSpec((1, D), lambda i: (0, 0))],
        [jax.ShapeDtypeStruct((T, D), F32), jax.ShapeDtypeStruct((1, D), F32)], [])
    return _pcall(
        body, name=name, grid=(ni,), in_specs=in_specs, out_specs=out_specs, out_shape=out_shape,
        scratch_shapes=scratch, compiler_params=_params(1),
    )(dz, wt, x, g, dres, *(plan.arrays if plan else ()))


def _loss_head(x, g, target, name):
    T, D = x.shape
    tm = min(T, TOKEN_TILE)

    def body(x_ref, g_ref, t_ref, dx_ref, dg_ref, loss_ref):
        @pl.when(pl.program_id(0) == 0)
        def _():
            dg_ref[...] = jnp.zeros_like(dg_ref)
            loss_ref[...] = jnp.zeros_like(loss_ref)

        xv = x_ref[...]
        gv = g_ref[...]
        out = (xv * _rms_inv(xv)) * gv
        diff = out - t_ref[...]
        loss_ref[...] += 0.5 * jnp.sum(jnp.mean(diff * diff, axis=-1, keepdims=True))
        dxn, dg = _norm_bwd(diff * (1.0 / D), xv, gv)
        dx_ref[...] = dxn
        dg_ref[...] += dg

    return _pcall(
        body, name=name, grid=(T // tm,),
        in_specs=[pl.BlockSpec((tm, D), lambda i: (i, 0)), pl.BlockSpec((1, D), lambda i: (0, 0)),
                  pl.BlockSpec((tm, D), lambda i: (i, 0))],
        out_specs=[pl.BlockSpec((tm, D), lambda i: (i, 0)), pl.BlockSpec((1, D), lambda i: (0, 0)),
                   pl.BlockSpec((1, LANES), lambda i: (0, 0))],
        out_shape=[jax.ShapeDtypeStruct((T, D), F32), jax.ShapeDtypeStruct((1, D), F32),
                   jax.ShapeDtypeStruct((1, LANES), F32)],
        compiler_params=_params(1),
    )(x, g, target)


Z_Q = 3 * CONV_WIDTH
Z_K = Z_Q + N_Q_HEADS * HEAD_DIM
Z_V = Z_K + LANES
Z_END = Z_V + LANES


def _rope_tables(T):
    half = ROT_DIM // 2
    inv_freq = ROPE_THETA ** (-jnp.arange(0, ROT_DIM, 2, dtype=F32) / ROT_DIM)
    ang = inv_freq[:, None] * jnp.arange(T, dtype=F32)[None, :]
    cos_sin = jnp.concatenate([jnp.cos(ang), jnp.sin(ang)], axis=0)
    select = np.zeros((2 * half, 3 * LANES), np.float32)
    const = np.zeros((1, 3 * LANES), np.float32)
    for lane in range(LANES):
        d = lane % HEAD_DIM
        if d < half:
            select[d, lane] = 1.0
            select[half + d, LANES + lane] = -1.0
        elif d < ROT_DIM:
            select[d - half, lane] = 1.0
            select[d, 2 * LANES + lane] = 1.0
        else:
            const[0, lane] = 1.0
    tab = lax.dot_general(cos_sin, jnp.asarray(select), (((0,), (0,)), ((), ())),
                          precision=lax.Precision.HIGHEST, preferred_element_type=F32)
    return tab + jnp.asarray(const)


def _tab3(tab):
    return tab[:, 0:LANES], tab[:, LANES:2 * LANES], tab[:, 2 * LANES:3 * LANES]


def _rot(x, tab):
    c, s1, s2 = _tab3(tab)
    return x * c + pltpu.roll(x, LANES - ROT_DIM // 2, 1) * s1 + pltpu.roll(x, ROT_DIM // 2, 1) * s2


def _rot_t(d, tab):
    c, s1, s2 = _tab3(tab)
    return d * c + pltpu.roll(d * s1, ROT_DIM // 2, 1) + pltpu.roll(d * s2, LANES - ROT_DIM // 2, 1)


def _head_pads(a):
    lo = lax.broadcasted_iota(jnp.int32, a.shape, 1) < HEAD_DIM
    nat0 = jnp.where(lo, a, 0.0)
    nat1 = jnp.where(lo, 0.0, a)
    return {
        (0, 0): nat0.astype(BF16), (0, 1): pltpu.roll(nat0, HEAD_DIM, 1).astype(BF16),
        (1, 0): pltpu.roll(nat1, HEAD_DIM, 1).astype(BF16), (1, 1): nat1.astype(BF16),
    }


def _from_pads(even, odd, kv):
    lo = lax.broadcasted_iota(jnp.int32, even.shape, 1) < HEAD_DIM
    if kv == 0:
        return jnp.where(lo, even + pltpu.roll(odd, HEAD_DIM, 1), 0.0)
    return jnp.where(lo, 0.0, pltpu.roll(even, HEAD_DIM, 1) + odd)


N_GROUPS = 4


def _group_head(g, r):
    kv, par = divmod(g, 2)
    return 2 * (2 * kv + r) + par


def _window_mask_t(has_prev):
    jj = lax.broadcasted_iota(jnp.int32, (2 * BLOCK, 2 * BLOCK), 0)
    ii = lax.broadcasted_iota(jnp.int32, (2 * BLOCK, 2 * BLOCK), 1) & (BLOCK - 1)
    rel = jj - BLOCK - ii
    return (rel <= 0) & (rel > -BLOCK) & ((jj >= BLOCK) | has_prev)


def _sink_row(sink_ref, g):
    lane = lax.broadcasted_iota(jnp.int32, (1, 2 * BLOCK), 1)
    return jnp.where(lane < BLOCK, sink_ref[0, _group_head(g, 0)], sink_ref[0, _group_head(g, 1)])


def _attn_probs_t(q2, kp, mask, sink_ref):
    out = []
    for kv in range(2):
        q_st = jnp.concatenate([q2[2 * kv], q2[2 * kv + 1]], axis=0)
        for par in range(2):
            s = jnp.where(mask, _dot_nt(kp[(kv, par)], q_st) * ATTN_SCALE, MASK_VALUE)
            sink = _sink_row(sink_ref, 2 * kv + par)
            m = jnp.maximum(jnp.max(s, axis=0, keepdims=True), sink)
            p = jnp.exp(s - m)
            esink = jnp.exp(sink - m)
            rden = 1.0 / (jnp.sum(p, axis=0, keepdims=True) + esink)
            out.append((p * rden, esink * rden))
    return out


def _conv_taps(cg, u, cg_prev, u_prev, has_prev):
    vv = cg * u
    halo = jnp.where(has_prev, cg_prev * u_prev, 0.0)
    ext = jnp.concatenate([halo, vv], axis=0)
    rows = ext.shape[0]
    vv1 = pltpu.roll(ext, 1, 0)[8:rows]
    vv2 = pltpu.roll(ext, 2, 0)[8:rows]
    return vv, vv1, vv2


def _mix_specs(nb):
    cur = lambda n: jnp.minimum(n, nb - 1)
    prev = lambda n: jnp.maximum(jnp.minimum(n, nb - 1) - 1, 0)
    rows8_prev = lambda n: jnp.maximum(16 * jnp.minimum(n, nb - 1) - 1, 0)
    return cur, prev, [
        pl.BlockSpec((BLOCK, Z_END), lambda n: (cur(n), 0)),
        pl.BlockSpec((BLOCK, 2 * LANES), lambda n: (prev(n), Z_K // (2 * LANES))),
        pl.BlockSpec((8, CONV_WIDTH), lambda n: (rows8_prev(n), 1)),
        pl.BlockSpec((8, CONV_WIDTH), lambda n: (rows8_prev(n), 2)),
        pl.BlockSpec((BLOCK, 3 * LANES), lambda n: (cur(n), 0)),
        pl.BlockSpec((BLOCK, 3 * LANES), lambda n: (prev(n), 0)),
        pl.BlockSpec((3, CONV_WIDTH), lambda n: (0, 0)),
        pl.BlockSpec(memory_space=pltpu.SMEM),
    ]


def _mix_core_fwd(z, tab, conv_w, sinks, name):
    T = z.shape[0]
    nb = T // BLOCK
    _, _, specs = _mix_specs(nb)

    def body(z_ref, zkvp_ref, cgp_ref, up_ref, tab_ref, tabp_ref, cw_ref, sink_ref, y_ref):
        has_prev = pl.program_id(0) > 0
        bg = z_ref[:, 0:CONV_WIDTH]
        vv, vv1, vv2 = _conv_taps(z_ref[:, CONV_WIDTH:2 * CONV_WIDTH], z_ref[:, 2 * CONV_WIDTH:Z_Q],
                                  cgp_ref[...], up_ref[...], has_prev)
        conv = cw_ref[0:1, :] * vv2 + cw_ref[1:2, :] * vv1 + cw_ref[2:3, :] * vv
        y_ref[:, 0:CONV_WIDTH] = (bg * conv).astype(BF16)

        tab_c = tab_ref[...]
        tab_p = tabp_ref[...]
        k_all = jnp.concatenate([_rot(zkvp_ref[:, 0:LANES], tab_p), _rot(z_ref[:, Z_K:Z_V], tab_c)], axis=0)
        v_all = jnp.concatenate([zkvp_ref[:, LANES:2 * LANES], z_ref[:, Z_V:Z_END]], axis=0)
        kp = _head_pads(k_all)
        vp = _head_pads(v_all)
        q2 = [_rot(z_ref[:, Z_Q + LANES * c:Z_Q + LANES * (c + 1)], tab_c).astype(BF16) for c in range(N_Q_HEADS // 2)]
        probs = _attn_probs_t(q2, kp, _window_mask_t(has_prev), sink_ref)
        for kv in range(2):
            o_t = (_dot_tn(vp[(kv, 0)], probs[2 * kv][0].astype(BF16))
                   + _dot_tn(vp[(kv, 1)], probs[2 * kv + 1][0].astype(BF16)))
            for r in range(2):
                c = 2 * kv + r
                y_ref[:, CONV_WIDTH + LANES * c:CONV_WIDTH + LANES * (c + 1)] = o_t[:, BLOCK * r:BLOCK * (r + 1)].T.astype(BF16)

    return _pcall(
        body, name=name, grid=(nb,), in_specs=specs,
        out_specs=pl.BlockSpec((BLOCK, 2 * CONV_WIDTH), lambda n: (n, 0)),
        out_shape=jax.ShapeDtypeStruct((T, 2 * CONV_WIDTH), BF16),
        compiler_params=_params(1),
    )(z, z, z, z, tab, tab, conv_w, sinks)


def _mix_core_bwd(z, dy, tab, conv_w, sinks, name, plan=None):
    T = z.shape[0]
    nb = T // BLOCK
    cur, _, specs = _mix_specs(nb)
    rows8_next = lambda n: jnp.minimum(16 * (cur(n) + 1), 16 * nb - 1)
    specs = specs[:4] + [
        pl.BlockSpec((8, CONV_WIDTH), lambda n: (rows8_next(n), 0)),
        pl.BlockSpec((BLOCK, 2 * CONV_WIDTH), lambda n: (cur(n), 0)),
        pl.BlockSpec((8, CONV_WIDTH), lambda n: (rows8_next(n), 0)),
    ] + specs[4:]

    def body(*refs):
        ins, outs, scratch, cr = _unpack(refs, 11, 3, plan)
        z_ref, zkvp_ref, cgp_ref, up_ref, bgn_ref, dy_ref, dyn_ref, tab_ref, tabp_ref, cw_ref, sink_ref = ins
        dz_ref, dcw_ref, dsk_ref = outs
        main_ref, kv_ref = scratch
        n = pl.program_id(0)
        _hook(plan, cr, "start", n == 0)

        @pl.when(n == 0)
        def _():
            main_ref[...] = jnp.zeros_like(main_ref)
            kv_ref[...] = jnp.zeros_like(kv_ref)
            dcw_ref[...] = jnp.zeros_like(dcw_ref)
            dsk_ref[...] = jnp.zeros_like(dsk_ref)

        @pl.when(n < nb)
        def _():
            has_prev = n > 0
            has_next = n < nb - 1
            bg = z_ref[:, 0:CONV_WIDTH]
            cg = z_ref[:, CONV_WIDTH:2 * CONV_WIDTH]
            u = z_ref[:, 2 * CONV_WIDTH:Z_Q]
            vv, vv1, vv2 = _conv_taps(cg, u, cgp_ref[...], up_ref[...], has_prev)
            w0, w1, w2 = cw_ref[0:1, :], cw_ref[1:2, :], cw_ref[2:3, :]
            dyc = dy_ref[:, 0:CONV_WIDTH]
            dbg = dyc * (w0 * vv2 + w1 * vv1 + w2 * vv)
            dconv = dyc * bg
            dconv_next = jnp.where(has_next, dyn_ref[...] * bgn_ref[...], 0.0)
            ext = jnp.concatenate([dconv, dconv_next], axis=0)
            rows = ext.shape[0]
            dvv = w2 * dconv + w1 * pltpu.roll(ext, rows - 1, 0)[0:BLOCK] + w0 * pltpu.roll(ext, rows - 2, 0)[0:BLOCK]
            dcw_ref[0:1, :] += jnp.sum(dconv * vv2, axis=0, keepdims=True)
            dcw_ref[1:2, :] += jnp.sum(dconv * vv1, axis=0, keepdims=True)
            dcw_ref[2:3, :] += jnp.sum(dconv * vv, axis=0, keepdims=True)

            tab_c = tab_ref[...]
            tab_p = tabp_ref[...]
            k_all = jnp.concatenate([_rot(zkvp_ref[:, 0:LANES], tab_p), _rot(z_ref[:, Z_K:Z_V], tab_c)], axis=0)
            v_all = jnp.concatenate([zkvp_ref[:, LANES:2 * LANES], z_ref[:, Z_V:Z_END]], axis=0)
            kp = _head_pads(k_all)
            vp = _head_pads(v_all)
            chunks = range(N_Q_HEADS // 2)
            q2 = [_rot(z_ref[:, Z_Q + LANES * c:Z_Q + LANES * (c + 1)], tab_c).astype(BF16) for c in chunks]
            do2 = [dy_ref[:, CONV_WIDTH + LANES * c:CONV_WIDTH + LANES * (c + 1)].astype(BF16) for c in chunks]
            probs = _attn_probs_t(q2, kp, _window_mask_t(has_prev), sink_ref)
            dq_chunks = []
            dk_nat = jnp.zeros((2 * BLOCK, LANES), F32)
            dv_nat = jnp.zeros((2 * BLOCK, LANES), F32)
            for kv in range(2):
                q_st = jnp.concatenate([q2[2 * kv], q2[2 * kv + 1]], axis=0)
                do_st = jnp.concatenate([do2[2 * kv], do2[2 * kv + 1]], axis=0)
                dq_t = jnp.zeros((LANES, 2 * BLOCK), F32)
                dk_par, dv_par = [], []
                for par in range(2):
                    g = 2 * kv + par
                    pr, psink = probs[g]
                    dp = _dot_nt(vp[(kv, par)], do_st)
                    delta = jnp.sum(dp * pr, axis=0, keepdims=True)
                    ds = (pr * (dp - delta) * ATTN_SCALE).astype(BF16)
                    dsink = -psink * delta
                    for r in range(2):
                        h = _group_head(g, r)
                        dsk_ref[h:h + 1, :] += jnp.sum(dsink[:, BLOCK * r:BLOCK * (r + 1)])
                    dq_t = dq_t + _dot_tn(kp[(kv, par)], ds)
                    dk_par.append(_dot(ds, q_st))
                    dv_par.append(_dot(pr.astype(BF16), do_st))
                for r in range(2):
                    dq_chunks.append(_rot_t(dq_t[:, BLOCK * r:BLOCK * (r + 1)].T, tab_c))
                dk_nat = dk_nat + _from_pads(dk_par[0], dk_par[1], kv)
                dv_nat = dv_nat + _from_pads(dv_par[0], dv_par[1], kv)

            dk_prev = _rot_t(kv_ref[:, 0:LANES] + dk_nat[0:BLOCK], tab_p)
            dv_prev = kv_ref[:, LANES:2 * LANES] + dv_nat[0:BLOCK]
            dz_ref[:, 0:Z_K] = main_ref[...]
            dz_ref[:, Z_K:Z_V] = dk_prev.astype(BF16)
            dz_ref[:, Z_V:Z_END] = dv_prev.astype(BF16)
            main_ref[:, 0:CONV_WIDTH] = dbg.astype(BF16)
            main_ref[:, CONV_WIDTH:2 * CONV_WIDTH] = (dvv * u).astype(BF16)
            main_ref[:, 2 * CONV_WIDTH:Z_Q] = (dvv * cg).astype(BF16)
            for c in range(N_Q_HEADS // 2):
                main_ref[:, Z_Q + LANES * c:Z_Q + LANES * (c + 1)] = dq_chunks[c].astype(BF16)
            kv_ref[:, 0:LANES] = dk_nat[BLOCK:2 * BLOCK]
            kv_ref[:, LANES:2 * LANES] = dv_nat[BLOCK:2 * BLOCK]

        @pl.when(n == nb)
        def _():
            dz_ref[:, 0:Z_K] = main_ref[...]
            dz_ref[:, Z_K:Z_V] = _rot_t(kv_ref[:, 0:LANES], tab_ref[...]).astype(BF16)
            dz_ref[:, Z_V:Z_END] = kv_ref[:, LANES:2 * LANES].astype(BF16)

        _hook(plan, cr, "finish", n == nb)

    in_specs, out_specs, out_shape, scratch = _carried(
        plan, specs,
        [pl.BlockSpec((BLOCK, Z_END), lambda n: (jnp.maximum(n - 1, 0), 0)),
         pl.BlockSpec((8, CONV_WIDTH), lambda n: (0, 0)), pl.BlockSpec((8, LANES), lambda n: (0, 0))],
        [jax.ShapeDtypeStruct((T, Z_END), BF16), jax.ShapeDtypeStruct((8, CONV_WIDTH), F32),
         jax.ShapeDtypeStruct((8, LANES), F32)],
        [pltpu.VMEM((BLOCK, Z_K), BF16), pltpu.VMEM((BLOCK, 2 * LANES), F32)])
    return _pcall(
        body, name=name, grid=(nb + 1,), in_specs=in_specs, out_specs=out_specs, out_shape=out_shape,
        scratch_shapes=scratch, compiler_params=_params(1),
    )(z, z, z, z, z, dy, dy, tab, tab, conv_w, sinks, *(plan.arrays if plan else ()))


ROW_SPLIT = 2


def _pair_sum(grads, recvd, core, name):
    n = len(grads)

    def body(core_ref, *refs):
        g, r = refs[:n], refs[n:2 * n]
        s, sb = refs[2 * n:3 * n], refs[3 * n:]
        for t in range(n):
            tot = g[t][...] + r[t][...].astype(F32)
            s[t][...] = tot
            sb[t][...] = tot.astype(BF16)

    def blk(a):
        return (1, a.shape[1] // ROW_SPLIT, a.shape[2])

    in_specs = [pl.BlockSpec(blk(r), lambda q, i, core_ref: (q, core_ref[0] * ROW_SPLIT + i, 0)) for r in recvd]
    in_specs += [pl.BlockSpec(blk(r), lambda q, i, core_ref: (q, i, 0)) for r in recvd]
    out_specs = [pl.BlockSpec(blk(r), lambda q, i, core_ref: (q, i, 0)) for r in recvd] * 2
    return _pcall(
        body, name=name,
        grid_spec=pltpu.PrefetchScalarGridSpec(num_scalar_prefetch=1, grid=(N_CHIPS, ROW_SPLIT),
                                               in_specs=in_specs, out_specs=out_specs),
        out_shape=[jax.ShapeDtypeStruct(r.shape, F32) for r in recvd] + [jax.ShapeDtypeStruct(r.shape, BF16) for r in recvd],
        compiler_params=_params(2),
    )(core, *grads, *recvd)


def _chip_sum(parts, recvd, place, name):
    n = len(parts)

    def body(place_ref, *refs):
        p, r, o = refs[:n], refs[n:2 * n], refs[2 * n:]
        for t in range(n):
            tot = p[t][0]
            for j in range(3):
                tot = tot + r[t][j].astype(F32)
            o[t][...] = tot

    in_specs = [pl.BlockSpec((1, p.shape[1] // ROW_SPLIT, p.shape[2]), lambda i, place_ref: (place_ref[0], i, 0))
                for p in parts]
    in_specs += [pl.BlockSpec((3, r.shape[1] // ROW_SPLIT, r.shape[2]), lambda i, place_ref: (0, i, 0)) for r in recvd]
    out_specs = [pl.BlockSpec((p.shape[1] // ROW_SPLIT, p.shape[2]),
                              lambda i, place_ref: (place_ref[1] * ROW_SPLIT + i, 0)) for p in parts]
    return _pcall(
        body, name=name,
        grid_spec=pltpu.PrefetchScalarGridSpec(num_scalar_prefetch=1, grid=(ROW_SPLIT,),
                                               in_specs=in_specs, out_specs=out_specs),
        out_shape=[jax.ShapeDtypeStruct((2 * p.shape[1], p.shape[2]), F32) for p in parts],
        compiler_params=_params(1),
    )(place, *parts, *recvd)


def _adamw_math(w, g, m, v):
    m = ADAM_B1 * m + (1.0 - ADAM_B1) * g
    v = ADAM_B2 * v + (1.0 - ADAM_B2) * (g * g)
    m_hat = m / (1.0 - ADAM_B1 ** ADAM_STEP)
    v_hat = v / (1.0 - ADAM_B2 ** ADAM_STEP)
    delta = -ADAM_LR * (m_hat / (jnp.sqrt(v_hat) + ADAM_EPS) + ADAM_WD * w)
    return delta, m, v


def _adamw(ws, gs, ms, vs, row_blocks, name):
    n = len(ws)

    def body(*refs):
        w, g, m, v = refs[:n], refs[n:2 * n], refs[2 * n:3 * n], refs[3 * n:4 * n]
        d, mo, vo = refs[4 * n:5 * n], refs[5 * n:6 * n], refs[6 * n:]
        for t in range(n):
            delta, m_new, v_new = _adamw_math(w[t][...], g[t][...], m[t][...], v[t][...])
            d[t][...] = delta
            mo[t][...] = m_new
            vo[t][...] = v_new

    specs = [pl.BlockSpec((a.shape[0] // row_blocks, a.shape[1]), lambda i: (i, 0)) for a in ws]
    shapes = [jax.ShapeDtypeStruct(a.shape, F32) for a in ws]
    return _pcall(
        body, name=name, grid=(row_blocks,), in_specs=specs * 4, out_specs=specs * 3, out_shape=shapes * 3,
        compiler_params=_params(1),
    )(*ws, *gs, *ms, *vs)


def kernel(x, ffn1_norm, ffn1_w_gate, ffn1_w_up, ffn1_w_down, mix_norm, w_in, conv_w, attn_sinks, w_out, ffn2_norm, ffn2_w_gate, ffn2_w_up, ffn2_w_down, final_norm, loss_target, m_ffn1_norm, m_ffn1_w_gate, m_ffn1_w_up, m_ffn1_w_down, m_mix_norm, m_w_in, m_conv_w, m_attn_sinks, m_w_out, m_ffn2_norm, m_ffn2_w_gate, m_ffn2_w_up, m_ffn2_w_down, m_final_norm, v_ffn1_norm, v_ffn1_w_gate, v_ffn1_w_up, v_ffn1_w_down, v_mix_norm, v_w_in, v_conv_w, v_attn_sinks, v_w_out, v_ffn2_norm, v_ffn2_w_gate, v_ffn2_w_up, v_ffn2_w_down, v_final_norm):
    T, D = x.shape[1], x.shape[2]
    chip = (2 * lax.axis_index("x") + lax.axis_index("y")).astype(jnp.int32)
    core = lax.axis_index("c").astype(jnp.int32)
    place = jnp.stack([chip, core])
    x0 = x[0]
    target = loss_target[0]
    gf = final_norm.reshape(1, D)

    tr = lambda w: jnp.swapaxes(w[0], 0, 1)
    big = [tr(ffn1_w_gate), tr(ffn1_w_up), ffn1_w_down[0], tr(w_in), w_out[0], tr(ffn2_w_gate), tr(ffn2_w_up), ffn2_w_down[0]]
    transposed = [True, True, False, True, False, True, True, False]
    own_b = [w.astype(BF16) for w in big]

    def whole(gathered, own):
        return lax.dynamic_update_slice(gathered, own[None], (chip, 0, 0)).reshape(-1, D)

    got1 = _run_comm(_gather_plan(own_b[0:3]), "gather_ffn1")
    wg1, wu1, wd1 = (whole(g, o) for g, o in zip(got1, own_b[0:3]))
    tab = _rope_tables(T)

    res = _ffn_fwd(x0, ffn1_norm, wg1, wu1, wd1, "ffn1_fwd", _gather_plan(own_b[3:8], [conv_w[0]]))
    x1, h1, gate1, up1, act1 = res[:5]
    win, wout, wg2, wu2, wd2 = (whole(g, o) for g, o in zip(res[5:10], own_b[3:8]))
    convw4 = lax.dynamic_update_slice(res[10], conv_w, (chip, 0, 0))
    convw = jnp.transpose(convw4, (1, 0, 2)).reshape(3, -1)
    z, hm = _norm_matmul(x1, mix_norm, win, "mix_in_fwd")
    ymix = _mix_core_fwd(z, tab, convw, attn_sinks, "mix_core_fwd")
    x2 = _matmul_residual(ymix, wout, x1, "mix_out_fwd")
    x3, h2, gate2, up2, act2 = _ffn_fwd(x2, ffn2_norm, wg2, wu2, wd2, "ffn2_fwd")
    dx3, dgf, loss_part = _loss_head(x3, gf, target, "loss_head")

    def quarters(pairs):
        return [p[0].reshape(N_CHIPS, -1, D) for p in pairs], [p[1].reshape(N_CHIPS, -1, D) for p in pairs]

    def split(res, n):
        return res[:n], res[n:]

    dx2, dyb2, dgate2, dup2, dg2 = _ffn_bwd(dx3, x2, ffn2_norm, gate2, up2, wg2, wu2, wd2, "ffn2_bwd")
    g2, g2b = quarters([_matmul_tn(dgate2, h2, DW_ROW_SPLIT, "ffn2_dwg"), _matmul_tn(dup2, h2, DW_ROW_SPLIT, "ffn2_dwu"),
                        _matmul_tn(act2, dyb2, DW_ROW_SPLIT, "ffn2_dwd")])
    (dymix, dx2b), sib2 = split(_matmul_nt(dx2, wout, "mix_out_bwd", _sibling_plan(g2b)), 2)
    gwout = _matmul_tn(ymix, dx2b, DW_ROW_SPLIT, "mix_dwout")
    pair2 = _pair_sum(g2, sib2, core.reshape(1), "pair_sum_ffn2")
    (dz, dcw, dsk), chips2 = split(_mix_core_bwd(z, dymix, tab, convw, attn_sinks, "mix_core_bwd", _scatter_plan(pair2[3:])), 3)
    half2 = _chip_sum(pair2[:3], chips2, place, "chip_sum_ffn2")
    gwin = _matmul_tn(dz, hm, DW_ROW_SPLIT, "mix_dwin")
    gm, gmb = quarters([gwin, gwout])
    (dx1, dgm), sibm = split(_matmul_norm_bwd(dz, win, x1, mix_norm, dx2, "mix_in_bwd", _sibling_plan(gmb)), 2)
    pairm = _pair_sum(gm, sibm, core.reshape(1), "pair_sum_mix")
    dx0, dyb1, dgate1, dup1, dg1 = _ffn_bwd(dx1, x0, ffn1_norm, gate1, up1, wg1, wu1, wd1, "ffn1_bwd")
    dwg1, chipsm = split(_matmul_tn(dgate1, h1, DW_ROW_SPLIT, "ffn1_dwg", _scatter_plan(pairm[2:])), 2)
    halfm = _chip_sum(pairm[:2], chipsm, place, "chip_sum_mix")
    gg, ggb = quarters([dwg1])
    dwu1, sibg = split(_matmul_tn(dup1, h1, DW_ROW_SPLIT, "ffn1_dwu", _sibling_plan(ggb)), 2)
    pairg = _pair_sum(gg, sibg, core.reshape(1), "pair_sum_ffn1_gate")
    dwd1, chipsg = split(_matmul_tn(act1, dyb1, DW_ROW_SPLIT, "ffn1_dwd", _scatter_plan(pairg[1:])), 2)
    halfg = _chip_sum(pairg[:1], chipsg, place, "chip_sum_ffn1_gate")
    g1, g1b = quarters([dwu1, dwd1])
    sib1 = _run_comm(_sibling_plan(g1b), "grads1_to_sibling")
    pair1 = _pair_sum(g1, sib1, core.reshape(1), "pair_sum_ffn1")
    chips1 = _run_comm(_scatter_plan(pair1[2:]), "grads1_to_chips")
    half1 = _chip_sum(pair1[:2], chips1, place, "chip_sum_ffn1")
    g_big = _join_halves([*halfg, *half1, *halfm, *half2], "grads_join")

    pad = lambda a: jnp.pad(a, ((0, 0), (0, LANES - a.shape[1])))
    vec = jnp.concatenate([dg1, dgm, dg2, dgf, dcw[0:3].reshape(1, -1), pad(dsk[:, 0].reshape(1, -1)),
                           pad(loss_part[:, 0:1])], axis=1)
    total = _all_reduce_small(jnp.pad(vec, ((0, 7), (0, 0))), "small_all_reduce")[0:1]
    g_n1, g_nm, g_n2, g_nf = (total[:, k * D:(k + 1) * D] for k in range(4))
    cw_full = total[:, 4 * D:4 * D + 3 * CONV_WIDTH].reshape(3, CONV_WIDTH)
    cq = CONV_WIDTH // N_CHIPS
    g_cw = lax.dynamic_slice(cw_full, (0, chip * cq), (3, cq))
    off = 4 * D + 3 * CONV_WIDTH
    g_sk = total[:, off:off + N_Q_HEADS]
    loss = total[0, off + LANES]

    ws = big
    ms = [tr(m_ffn1_w_gate), tr(m_ffn1_w_up), m_ffn1_w_down[0], tr(m_w_in), m_w_out[0], tr(m_ffn2_w_gate), tr(m_ffn2_w_up), m_ffn2_w_down[0]]
    vs = [tr(v_ffn1_w_gate), tr(v_ffn1_w_up), v_ffn1_w_down[0], tr(v_w_in), v_w_out[0], tr(v_ffn2_w_gate), tr(v_ffn2_w_up), v_ffn2_w_down[0]]
    upd = {}
    for name_, idx in (("adamw_a", [0, 1, 2, 4]), ("adamw_b", [3, 5, 6, 7])):
        res = _adamw([ws[i] for i in idx], [g_big[i] for i in idx], [ms[i] for i in idx], [vs[i] for i in idx], 8, name_)
        k = len(idx)
        for j, i in enumerate(idx):
            upd[i] = (res[j], res[k + j], res[2 * k + j])
    sw = [ffn1_norm, mix_norm, conv_w[0], attn_sinks, ffn2_norm, gf]
    sg = [g_n1, g_nm, g_cw, g_sk, g_n2, g_nf]
    sm = [m_ffn1_norm, m_mix_norm, m_conv_w[0], m_attn_sinks, m_ffn2_norm, m_final_norm.reshape(1, D)]
    sv = [v_ffn1_norm, v_mix_norm, v_conv_w[0], v_attn_sinks, v_ffn2_norm, v_final_norm.reshape(1, D)]
    sres = _adamw(sw, sg, sm, sv, 1, "adamw_small")
    supd = [(sres[j], sres[6 + j], sres[12 + j]) for j in range(6)]

    order = [("s", 0), ("b", 0), ("b", 1), ("b", 2), ("s", 1), ("b", 3), ("s", 2), ("s", 3), ("b", 4),
             ("s", 4), ("b", 5), ("b", 6), ("b", 7), ("s", 5)]

    def leaf(kind, i, which):
        if kind == "b":
            a = g_big[i] if which == 0 else upd[i][which - 1]
            return (jnp.swapaxes(a, 0, 1) if transposed[i] else a)[None]
        a = sg[i] if which == 0 else supd[i][which - 1]
        if i == 2:
            return a[None]
        if i == 5:
            return a.reshape(D)
        return a

    outs = [loss, dx0[None]]
    for which in range(4):
        outs += [leaf(kind, i, which) for kind, i in order]
    return tuple(outs)
```

```python
import functools

import jax
import jax.numpy as jnp
import numpy as np
from jax import lax
from jax.experimental import pallas as pl
from jax.experimental.pallas import tpu as pltpu

F32 = jnp.float32
BF16 = jnp.bfloat16
MESH = pl.DeviceIdType.MESH

CONV_WIDTH = 512
N_Q_HEADS = 8
HEAD_DIM = 64
BLOCK = 128
ROPE_THETA = 500000.0
ROT_DIM = 16
RMS_EPS = 1e-5
MASK_VALUE = -1e30
ATTN_SCALE = HEAD_DIM ** -0.5
FFN_RES_SCALE = 0.5
ADAM_LR = 0.001
ADAM_B1 = 0.9
ADAM_B2 = 0.999
ADAM_EPS = 1e-08
ADAM_WD = 0.01
ADAM_STEP = 10

N_CHIPS = 4
N_DEV = 8
LANES = 128
VMEM_LIMIT = 56 * 1024 * 1024

_pcall = pl.pallas_call
HBM_SPEC = pl.BlockSpec(memory_space=pltpu.HBM)
ANY_SPEC = pl.BlockSpec(memory_space=pl.ANY)


def _params(n_axes, vmem=VMEM_LIMIT):
    return pltpu.CompilerParams(dimension_semantics=("arbitrary",) * n_axes, vmem_limit_bytes=vmem)


def _dot(a, b):
    return jnp.dot(a, b, preferred_element_type=F32)


def _dot_nt(a, b):
    return lax.dot_general(a, b, (((1,), (1,)), ((), ())), preferred_element_type=F32)


def _dot_tn(a, b):
    return lax.dot_general(a, b, (((0,), (0,)), ((), ())), preferred_element_type=F32)


def _rms_inv(x):
    return lax.rsqrt(jnp.mean(x * x, axis=-1, keepdims=True) + RMS_EPS)


def _norm_bwd(dh, x, g):
    inv = _rms_inv(x)
    xhat = x * inv
    dg = jnp.sum(dh * xhat, axis=0, keepdims=True)
    dxhat = dh * g
    dx = inv * (dxhat - xhat * jnp.mean(dxhat * xhat, axis=-1, keepdims=True))
    return dx, dg


def _place():
    x, y, c = lax.axis_index("x"), lax.axis_index("y"), lax.axis_index("c")
    chips = [(1 - x, y), (x, 1 - y), (1 - x, 1 - y)]
    return x, y, c, chips


class _Plan:
    def __init__(self, arrays, out_shapes, n_sems, start, finish, middle=None, aliases=None):
        self.arrays, self.out_shapes, self.n_sems = list(arrays), list(out_shapes), n_sems
        self.start, self.finish, self.middle = start, finish, middle
        self.aliases = dict(aliases or {})

    def specs(self):
        k = len(self.arrays)
        sems = [pltpu.SemaphoreType.DMA((self.n_sems,)), pltpu.SemaphoreType.DMA((self.n_sems,))]
        return [HBM_SPEC] * k, [HBM_SPEC] * len(self.out_shapes), self.out_shapes, sems


class _SemSlice:
    def __init__(self, ref, offset):
        self.ref, self.offset = ref, offset

    @property
    def at(self):
        return self

    def __getitem__(self, k):
        return self.ref.at[k + self.offset]


def _merge_plans(plans):
    plans = [p for p in plans if p is not None]
    if len(plans) <= 1:
        return plans[0] if plans else None
    arrays, shapes, aliases, spans, n_sems = [], [], {}, [], 0
    for p in plans:
        a0, o0 = len(arrays), len(shapes)
        spans.append((a0, a0 + len(p.arrays), o0, o0 + len(p.out_shapes), n_sems))
        aliases.update({a0 + i: o0 + j for i, j in p.aliases.items()})
        arrays += p.arrays
        shapes += p.out_shapes
        n_sems += p.n_sems

    def run(which):
        def fn(ins, outs, send_sems, recv_sems):
            for p, (a0, a1, o0, o1, s0) in zip(plans, spans):
                part = getattr(p, which)
                if part is not None:
                    part(ins[a0:a1], outs[o0:o1], _SemSlice(send_sems, s0), _SemSlice(recv_sems, s0))
        return fn

    middle = run("middle") if any(p.middle is not None for p in plans) else None
    return _Plan(arrays, shapes, n_sems, run("start"), run("finish"), middle, aliases)


def _sibling_plan(grads_b):
    n = len(grads_b)

    def copies(ins, outs, send_sems, recv_sems):
        x, y, c, _ = _place()

        def copy(t):
            half = ins[t].shape[1] // 2
            return pltpu.make_async_remote_copy(
                src_ref=ins[t].at[:, pl.ds(pl.multiple_of((1 - c) * half, 16), half), :], dst_ref=outs[t],
                send_sem=send_sems.at[t], recv_sem=recv_sems.at[t], device_id=(x, y, 1 - c), device_id_type=MESH)

        return [copy(t) for t in range(n)]

    def start(*refs):
        for cp in copies(*refs):
            cp.start()

    def finish(*refs):
        for cp in copies(*refs):
            cp.wait()

    shapes = [jax.ShapeDtypeStruct((g.shape[0], g.shape[1] // 2, g.shape[2]), g.dtype) for g in grads_b]
    return _Plan(grads_b, shapes, n, start, finish)


def _scatter_plan(parts_b):
    n = len(parts_b)

    def copies(ins, outs, send_sems, recv_sems):
        x, y, c, chips = _place()

        def copy(t, j):
            px, py = chips[j]
            return pltpu.make_async_remote_copy(
                src_ref=ins[t].at[2 * px + py], dst_ref=outs[t].at[j], send_sem=send_sems.at[3 * t + j],
                recv_sem=recv_sems.at[3 * t + j], device_id=(px, py, c), device_id_type=MESH)

        return [copy(t, j) for t in range(n) for j in range(3)]

    def start(*refs):
        for cp in copies(*refs):
            cp.start()

    def finish(*refs):
        for cp in copies(*refs):
            cp.wait()

    shapes = [jax.ShapeDtypeStruct((3, *p.shape[1:]), p.dtype) for p in parts_b]
    return _Plan(parts_b, shapes, 3 * n, start, finish)


def _gather_plan(shards, small=()):
    n, ns = len(shards), len(small)

    def parts(ins, outs, send_sems, recv_sems):
        x, y, c, chips = _place()
        me = 2 * x + y

        def rows(t, core):
            half = ins[t].shape[0] // 2
            return pl.ds(pl.multiple_of(core * half, 16), half)

        def first(t, j, block, core):
            return pltpu.make_async_remote_copy(
                src_ref=ins[t].at[rows(t, core), :], dst_ref=outs[t].at[block, rows(t, core), :],
                send_sem=send_sems.at[6 * t + j], recv_sem=recv_sems.at[6 * t + j],
                device_id=(*chips[j], c), device_id_type=MESH)

        def passed(t, j, block, core):
            ref = outs[t].at[block, rows(t, core), :]
            return pltpu.make_async_remote_copy(
                src_ref=ref, dst_ref=ref, send_sem=send_sems.at[6 * t + 3 + j], recv_sem=recv_sems.at[6 * t + 3 + j],
                device_id=(x, y, 1 - c), device_id_type=MESH)

        def whole(s, j, block):
            k = 6 * n + 3 * s + j
            return pltpu.make_async_remote_copy(
                src_ref=ins[n + s], dst_ref=outs[n + s].at[block], send_sem=send_sems.at[k], recv_sem=recv_sems.at[k],
                device_id=(*chips[j], c), device_id_type=MESH)

        blocks = [2 * px + py for px, py in chips]
        return c, me, blocks, first, passed, whole

    def start(*refs):
        c, me, _, first, _, whole = parts(*refs)
        for t in range(n):
            for j in range(3):
                first(t, j, me, c).start()
        for s in range(ns):
            for j in range(3):
                whole(s, j, me).start()

    def middle(*refs):
        c, _, blocks, first, passed, _ = parts(*refs)
        for t in range(n):
            for j in range(3):
                first(t, j, blocks[j], c).wait_recv()
                passed(t, j, blocks[j], c).start()

    def finish(*refs):
        c, me, blocks, first, passed, whole = parts(*refs)
        for t in range(n):
            for j in range(3):
                passed(t, j, blocks[j], 1 - c).wait_recv()
        for s in range(ns):
            for j in range(3):
                whole(s, j, blocks[j]).wait_recv()
        for t in range(n):
            for j in range(3):
                first(t, j, me, c).wait_send()
                passed(t, j, blocks[j], c).wait_send()
        for s in range(ns):
            for j in range(3):
                whole(s, j, me).wait_send()

    arrays = [*shards, *small]
    shapes = [jax.ShapeDtypeStruct((N_CHIPS, *a.shape), a.dtype) for a in arrays]
    return _Plan(arrays, shapes, 6 * n + 3 * ns, start, finish, middle)


def _run_comm(plan, name):
    k = len(plan.arrays)
    in_specs, out_specs, out_shape, sems = plan.specs()

    def body(*refs):
        cr = (refs[:k], refs[k:k + len(out_shape)], refs[-2], refs[-1])
        plan.start(*cr)
        if plan.middle is not None:
            plan.middle(*cr)
        plan.finish(*cr)

    return _pcall(body, name=name, in_specs=in_specs, out_specs=out_specs, out_shape=out_shape,
                  input_output_aliases=plan.aliases, scratch_shapes=sems)(*plan.arrays)


def _carried(plan, in_specs, out_specs, out_shape, scratch):
    aliases = {}
    if plan is not None:
        p_in, p_out, p_shape, p_sems = plan.specs()
        aliases = {len(in_specs) + i: len(out_specs) + j for i, j in plan.aliases.items()}
        in_specs, out_specs = in_specs + p_in, out_specs + p_out
        out_shape, scratch = out_shape + p_shape, scratch + p_sems
    return dict(in_specs=in_specs, out_specs=out_specs, out_shape=out_shape, scratch_shapes=scratch,
                input_output_aliases=aliases)


def _unpack(refs, n_in, n_out, plan):
    k_in = len(plan.arrays) if plan else 0
    k_out = len(plan.out_shapes) if plan else 0
    ins = refs[:n_in]
    outs = refs[n_in + k_in:n_in + k_in + n_out]
    rest = refs[n_in + k_in + n_out + k_out:]
    if plan is None:
        return ins, outs, rest, None
    cr = (refs[n_in:n_in + k_in], refs[n_in + k_in + n_out:n_in + k_in + n_out + k_out], rest[-2], rest[-1])
    return ins, outs, rest[:-2], cr


def _hook(plan, cr, which, cond):
    fn = getattr(plan, which) if plan is not None else None
    if fn is not None:
        pl.when(cond)(lambda: fn(*cr))


def _join_plan(shards):
    n = len(shards)

    def copy(ins, outs, send_sems, recv_sems, t, core):
        x, y, c, _ = _place()
        half = ins[t].shape[0] // 2
        rows = pl.ds(pl.multiple_of(core * half, 8), half)
        return pltpu.make_async_remote_copy(
            src_ref=ins[t].at[rows, :], dst_ref=outs[t].at[rows, :], send_sem=send_sems.at[t],
            recv_sem=recv_sems.at[t], device_id=(x, y, 1 - c), device_id_type=MESH)

    def start(*refs):
        c = lax.axis_index("c")
        for t in range(n):
            copy(*refs, t, c).start()

    def finish(*refs):
        c = lax.axis_index("c")
        for t in range(n):
            copy(*refs, t, 1 - c).wait_recv()
        for t in range(n):
            copy(*refs, t, c).wait_send()

    shapes = [jax.ShapeDtypeStruct(s.shape, s.dtype) for s in shards]
    return _Plan(shards, shapes, n, start, finish, aliases={t: t for t in range(n)})


def _all_gather_plan(vec):
    def parts(ins, outs, send_sems, recv_sems):
        x, y, c, _ = _place()
        me = 4 * x + 2 * y + c
        rel = [((k >> 2) & 1, (k >> 1) & 1, k & 1) for k in range(1, N_DEV)]

        def peer(k):
            fx, fy, fc = rel[k]
            return (x ^ fx, y ^ fy, c ^ fc)

        def copy(k, dev):
            return pltpu.make_async_remote_copy(
                src_ref=ins[0], dst_ref=outs[0].at[dev], send_sem=send_sems.at[k], recv_sem=recv_sems.at[k],
                device_id=peer(k), device_id_type=MESH)

        mine = pltpu.make_async_copy(ins[0], outs[0].at[me], send_sems.at[N_DEV - 1])
        return me, peer, copy, mine

    def start(*refs):
        me, _, copy, mine = parts(*refs)
        mine.start()
        for k in range(N_DEV - 1):
            copy(k, me).start()

    def finish(*refs):
        me, peer, copy, mine = parts(*refs)
        for k in range(N_DEV - 1):
            px, py, pc = peer(k)
            copy(k, 4 * px + 2 * py + pc).wait_recv()
        for k in range(N_DEV - 1):
            copy(k, me).wait_send()
        mine.wait()

    return _Plan([vec], [jax.ShapeDtypeStruct((N_DEV, *vec.shape), vec.dtype)], N_DEV, start, finish)


def _sum_devices(blocks, name):
    def body(b_ref, o_ref):
        total = b_ref[0]
        for dev in range(1, N_DEV):
            total = total + b_ref[dev]
        o_ref[...] = total

    return _pcall(body, name=name, in_specs=[pl.BlockSpec(memory_space=pltpu.VMEM)],
                  out_specs=pl.BlockSpec(memory_space=pltpu.VMEM),
                  out_shape=jax.ShapeDtypeStruct(blocks.shape[1:], F32))(blocks)


TOKEN_TILE = 512
BWD_TOKEN_TILE = 256
DW_TOKEN_TILE = 2048
DW_ROW_SPLIT = 2
MXU_COLS = 256


def _chunks(n):
    out, c0 = [], 0
    while c0 < n:
        size = min(MXU_COLS, n - c0)
        out.append((c0, size))
        c0 += size
    return out


def _load_weights(hbm_refs, vmem_refs, sems):
    copies = [pltpu.make_async_copy(h, v, sems.at[k]) for k, (h, v) in enumerate(zip(hbm_refs, vmem_refs))]
    for cp in copies:
        cp.start()
    for cp in copies:
        cp.wait()


def _ffn_fwd(x, g, wgt, wut, wd, name, plan=None):
    T, D = x.shape
    F = wgt.shape[0]
    tm = min(T, TOKEN_TILE)
    ni = T // tm

    def body(*refs):
        (x_ref, g_ref, wg_hbm, wu_hbm, wd_hbm), (xo_ref, h_ref, gate_ref, up_ref, act_ref), scratch, cr = _unpack(refs, 5, 5, plan)
        wg_ref, wu_ref, wd_ref, sems = scratch
        i = pl.program_id(0)
        _hook(plan, cr, "start", i == 0)

        @pl.when(i == 0)
        def _():
            _load_weights((wg_hbm, wu_hbm, wd_hbm), (wg_ref, wu_ref, wd_ref), sems)

        xv = x_ref[...]
        h = ((xv * _rms_inv(xv)) * g_ref[...]).astype(BF16)
        h_ref[...] = h
        for c0, size in _chunks(F):
            gate = _dot_nt(h, wg_ref[c0:c0 + size, :])
            up = _dot_nt(h, wu_ref[c0:c0 + size, :])
            gate_ref[:, c0:c0 + size] = gate.astype(BF16)
            up_ref[:, c0:c0 + size] = up.astype(BF16)
            act_ref[:, c0:c0 + size] = (gate * jax.nn.sigmoid(gate) * up).astype(BF16)
        xo_ref[...] = x_ref[...] + FFN_RES_SCALE * _dot(act_ref[...], wd_ref[...])
        _hook(plan, cr, "middle", i == (3 * ni) // 4)
        _hook(plan, cr, "finish", i == ni - 1)

    io = _carried(
        plan,
        [pl.BlockSpec((tm, D), lambda i: (i, 0)), pl.BlockSpec((1, D), lambda i: (0, 0)),
         ANY_SPEC, ANY_SPEC, ANY_SPEC],
        [pl.BlockSpec((tm, D), lambda i: (i, 0)), pl.BlockSpec((tm, D), lambda i: (i, 0)),
         pl.BlockSpec((tm, F), lambda i: (i, 0)), pl.BlockSpec((tm, F), lambda i: (i, 0)),
         pl.BlockSpec((tm, F), lambda i: (i, 0))],
        [jax.ShapeDtypeStruct((T, D), F32), jax.ShapeDtypeStruct((T, D), BF16),
         jax.ShapeDtypeStruct((T, F), BF16), jax.ShapeDtypeStruct((T, F), BF16), jax.ShapeDtypeStruct((T, F), BF16)],
        [pltpu.VMEM((F, D), BF16), pltpu.VMEM((F, D), BF16), pltpu.VMEM((F, D), BF16),
         pltpu.SemaphoreType.DMA((3,))])
    return _pcall(
        body, name=name, grid=(ni,), compiler_params=_params(1), **io,
    )(x, g, wgt, wut, wd, *(plan.arrays if plan else ()))


def _ffn_bwd(dy, x, g, gate, up, wgt, wut, wd, name, plan=None):
    T, D = x.shape
    F = wgt.shape[0]
    tm = min(T, BWD_TOKEN_TILE)
    ni = T // tm

    def body(*refs):
        ins, outs, scratch, cr = _unpack(refs, 8, 5, plan)
        dy_ref, x_ref, g_ref, gate_ref, up_ref, wg_hbm, wu_hbm, wd_hbm = ins
        dx_ref, dyb_ref, dgate_ref, dup_ref, dg_ref = outs
        wg_ref, wu_ref, wd_ref, sems = scratch
        i = pl.program_id(0)
        _hook(plan, cr, "start", i == 0)

        @pl.when(i == 0)
        def _():
            _load_weights((wg_hbm, wu_hbm, wd_hbm), (wg_ref, wu_ref, wd_ref), sems)
            dg_ref[...] = jnp.zeros_like(dg_ref)

        dyb = (FFN_RES_SCALE * dy_ref[...]).astype(BF16)
        dyb_ref[...] = dyb
        for c0, size in _chunks(F):
            dact = _dot_nt(dyb, wd_ref[c0:c0 + size, :])
            gt = gate_ref[:, c0:c0 + size].astype(F32)
            u = up_ref[:, c0:c0 + size].astype(F32)
            sig = jax.nn.sigmoid(gt)
            dup_ref[:, c0:c0 + size] = (dact * (gt * sig)).astype(BF16)
            dgate_ref[:, c0:c0 + size] = (dact * u * (sig * (1.0 + gt * (1.0 - sig)))).astype(BF16)
        dh = _dot(dgate_ref[...], wg_ref[...]) + _dot(dup_ref[...], wu_ref[...])
        dxn, dg = _norm_bwd(dh, x_ref[...], g_ref[...])
        dx_ref[...] = dy_ref[...] + dxn
        dg_ref[...] += dg
        _hook(plan, cr, "finish", i == ni - 1)

    io = _carried(
        plan,
        [pl.BlockSpec((tm, D), lambda i: (i, 0)), pl.BlockSpec((tm, D), lambda i: (i, 0)),
         pl.BlockSpec((1, D), lambda i: (0, 0)),
         pl.BlockSpec((tm, F), lambda i: (i, 0)), pl.BlockSpec((tm, F), lambda i: (i, 0)),
         ANY_SPEC, ANY_SPEC, ANY_SPEC],
        [pl.BlockSpec((tm, D), lambda i: (i, 0)), pl.BlockSpec((tm, D), lambda i: (i, 0)),
         pl.BlockSpec((tm, F), lambda i: (i, 0)), pl.BlockSpec((tm, F), lambda i: (i, 0)),
         pl.BlockSpec((1, D), lambda i: (0, 0))],
        [jax.ShapeDtypeStruct((T, D), F32), jax.ShapeDtypeStruct((T, D), BF16),
         jax.ShapeDtypeStruct((T, F), BF16), jax.ShapeDtypeStruct((T, F), BF16), jax.ShapeDtypeStruct((1, D), F32)],
        [pltpu.VMEM((F, D), BF16), pltpu.VMEM((F, D), BF16), pltpu.VMEM((F, D), BF16),
         pltpu.SemaphoreType.DMA((3,))])
    return _pcall(
        body, name=name, grid=(ni,), compiler_params=_params(1), **io,
    )(dy, x, g, gate, up, wgt, wut, wd, *(plan.arrays if plan else ()))


def _matmul_tn(a, b, row_split, name, plan=None):
    T, n1 = a.shape
    n2 = b.shape[1]
    tn = n1 // row_split
    tk = min(T, DW_TOKEN_TILE)
    nk = T // tk

    def body(*refs):
        (a_ref, b_ref), (o_ref, ob_ref), _, cr = _unpack(refs, 2, 2, plan)
        j = pl.program_id(0)
        k = pl.program_id(1)
        _hook(plan, cr, "start", jnp.logical_and(j == 0, k == 0))

        @pl.when(k == 0)
        def _():
            o_ref[...] = jnp.zeros_like(o_ref)

        o_ref[...] += _dot_tn(a_ref[...], b_ref[...])

        @pl.when(k == nk - 1)
        def _():
            ob_ref[...] = o_ref[...].astype(BF16)

        _hook(plan, cr, "finish", jnp.logical_and(j == row_split - 1, k == nk - 1))

    io = _carried(
        plan,
        [pl.BlockSpec((tk, tn), lambda j, k: (k, j)), pl.BlockSpec((tk, n2), lambda j, k: (k, 0))],
        [pl.BlockSpec((tn, n2), lambda j, k: (j, 0)), pl.BlockSpec((tn, n2), lambda j, k: (j, 0))],
        [jax.ShapeDtypeStruct((n1, n2), F32), jax.ShapeDtypeStruct((n1, n2), BF16)], [])
    return _pcall(
        body, name=name, grid=(row_split, nk), compiler_params=_params(2), **io,
    )(a, b, *(plan.arrays if plan else ()))


def _norm_matmul(x, g, wt, name):
    T, D = x.shape
    n = wt.shape[0]
    tm = min(T, TOKEN_TILE)

    def body(x_ref, g_ref, w_ref, z_ref, h_ref):
        xv = x_ref[...]
        h = ((xv * _rms_inv(xv)) * g_ref[...]).astype(BF16)
        h_ref[...] = h
        z_ref[...] = _dot_nt(h, w_ref[...])

    return _pcall(
        body, name=name, grid=(T // tm,),
        in_specs=[pl.BlockSpec((tm, D), lambda i: (i, 0)), pl.BlockSpec((1, D), lambda i: (0, 0)),
                  pl.BlockSpec((n, D), lambda i: (0, 0))],
        out_specs=[pl.BlockSpec((tm, n), lambda i: (i, 0)), pl.BlockSpec((tm, D), lambda i: (i, 0))],
        out_shape=[jax.ShapeDtypeStruct((T, n), F32), jax.ShapeDtypeStruct((T, D), BF16)],
        compiler_params=_params(1),
    )(x, g, wt)


def _matmul_residual(y, w, x, name):
    T, D = x.shape
    kdim = y.shape[1]
    tm = min(T, TOKEN_TILE)

    def body(y_ref, w_ref, x_ref, o_ref):
        o_ref[...] = x_ref[...] + _dot(y_ref[...], w_ref[...])

    return _pcall(
        body, name=name, grid=(T // tm,),
        in_specs=[pl.BlockSpec((tm, kdim), lambda i: (i, 0)), pl.BlockSpec((kdim, D), lambda i: (0, 0)),
                  pl.BlockSpec((tm, D), lambda i: (i, 0))],
        out_specs=pl.BlockSpec((tm, D), lambda i: (i, 0)),
        out_shape=jax.ShapeDtypeStruct((T, D), F32),
        compiler_params=_params(1),
    )(y, w, x)


def _matmul_nt(dx, w, name, plan=None):
    T, D = dx.shape
    kdim = w.shape[0]
    tm = min(T, TOKEN_TILE)
    ni = T // tm

    def body(*refs):
        (dx_ref, w_ref), (dy_ref, dxb_ref), _, cr = _unpack(refs, 2, 2, plan)
        i = pl.program_id(0)
        _hook(plan, cr, "start", i == 0)
        dxb = dx_ref[...].astype(BF16)
        dxb_ref[...] = dxb
        dy_ref[...] = _dot_nt(dxb, w_ref[...])
        _hook(plan, cr, "finish", i == ni - 1)

    io = _carried(
        plan,
        [pl.BlockSpec((tm, D), lambda i: (i, 0)), pl.BlockSpec((kdim, D), lambda i: (0, 0))],
        [pl.BlockSpec((tm, kdim), lambda i: (i, 0)), pl.BlockSpec((tm, D), lambda i: (i, 0))],
        [jax.ShapeDtypeStruct((T, kdim), F32), jax.ShapeDtypeStruct((T, D), BF16)], [])
    return _pcall(
        body, name=name, grid=(ni,), compiler_params=_params(1), **io,
    )(dx, w, *(plan.arrays if plan else ()))


def _matmul_norm_bwd(dz, wt, x, g, dres, name, plan=None):
    T, D = x.shape
    n = dz.shape[1]
    tm = min(T, TOKEN_TILE)
    ni = T // tm

    def body(*refs):
        (dz_ref, w_ref, x_ref, g_ref, dres_ref), (dx_ref, dg_ref), _, cr = _unpack(refs, 5, 2, plan)
        i = pl.program_id(0)
        _hook(plan, cr, "start", i == 0)

        @pl.when(i == 0)
        def _():
            dg_ref[...] = jnp.zeros_like(dg_ref)

        dh = _dot(dz_ref[...], w_ref[...])
        dxn, dg = _norm_bwd(dh, x_ref[...], g_ref[...])
        dx_ref[...] = dres_ref[...] + dxn
        dg_ref[...] += dg
        _hook(plan, cr, "finish", i == ni - 1)

    io = _carried(
        plan,
        [pl.BlockSpec((tm, n), lambda i: (i, 0)), pl.BlockSpec((n, D), lambda i: (0, 0)),
         pl.BlockSpec((tm, D), lambda i: (i, 0)), pl.BlockSpec((1, D), lambda i: (0, 0)),
         pl.BlockSpec((tm, D), lambda i: (i, 0))],
        [pl.BlockSpec((tm, D), lambda i: (i, 0)), pl.BlockSpec((1, D), lambda i: (0, 0))],
        [jax.ShapeDtypeStruct((T, D), F32), jax.ShapeDtypeStruct((1, D), F32)], [])
    return _pcall(
        body, name=name, grid=(ni,), compiler_params=_params(1), **io,
    )(dz, wt, x, g, dres, *(plan.arrays if plan else ()))


def _loss_head(x, g, target, name):
    T, D = x.shape
    tm = min(T, TOKEN_TILE)

    def body(x_ref, g_ref, t_ref, dx_ref, dg_ref, loss_ref):
        @pl.when(pl.program_id(0) == 0)
        def _():
            dg_ref[...] = jnp.zeros_like(dg_ref)
            loss_ref[...] = jnp.zeros_like(loss_ref)

        xv = x_ref[...]
        gv = g_ref[...]
        out = (xv * _rms_inv(xv)) * gv
        diff = out - t_ref[...]
        loss_ref[...] += 0.5 * jnp.sum(jnp.mean(diff * diff, axis=-1, keepdims=True))
        dxn, dg = _norm_bwd(diff * (1.0 / D), xv, gv)
        dx_ref[...] = dxn
        dg_ref[...] += dg

    return _pcall(
        body, name=name, grid=(T // tm,),
        in_specs=[pl.BlockSpec((tm, D), lambda i: (i, 0)), pl.BlockSpec((1, D), lambda i: (0, 0)),
                  pl.BlockSpec((tm, D), lambda i: (i, 0))],
        out_specs=[pl.BlockSpec((tm, D), lambda i: (i, 0)), pl.BlockSpec((1, D), lambda i: (0, 0)),
                   pl.BlockSpec((1, LANES), lambda i: (0, 0))],
        out_shape=[jax.ShapeDtypeStruct((T, D), F32), jax.ShapeDtypeStruct((1, D), F32),
                   jax.ShapeDtypeStruct((1, LANES), F32)],
        compiler_params=_params(1),
    )(x, g, target)


Z_Q = 3 * CONV_WIDTH
Z_K = Z_Q + N_Q_HEADS * HEAD_DIM
Z_V = Z_K + LANES
Z_END = Z_V + LANES


def _rope_tables(T):
    half = ROT_DIM // 2
    inv_freq = ROPE_THETA ** (-jnp.arange(0, ROT_DIM, 2, dtype=F32) / ROT_DIM)
    ang = inv_freq[:, None] * jnp.arange(T, dtype=F32)[None, :]
    cos_sin = jnp.concatenate([jnp.cos(ang), jnp.sin(ang)], axis=0)
    select = np.zeros((2 * half, 3 * LANES), np.float32)
    const = np.zeros((1, 3 * LANES), np.float32)
    for lane in range(LANES):
        d = lane % HEAD_DIM
        if d < half:
            select[d, lane] = 1.0
            select[half + d, LANES + lane] = -1.0
        elif d < ROT_DIM:
            select[d - half, lane] = 1.0
            select[d, 2 * LANES + lane] = 1.0
        else:
            const[0, lane] = 1.0
    tab = lax.dot_general(cos_sin, jnp.asarray(select), (((0,), (0,)), ((), ())),
                          precision=lax.Precision.HIGHEST, preferred_element_type=F32)
    return tab + jnp.asarray(const)


def _tab3(tab):
    return tab[:, 0:LANES], tab[:, LANES:2 * LANES], tab[:, 2 * LANES:3 * LANES]


def _rot(x, tab):
    c, s1, s2 = _tab3(tab)
    return x * c + pltpu.roll(x, LANES - ROT_DIM // 2, 1) * s1 + pltpu.roll(x, ROT_DIM // 2, 1) * s2


def _rot_t(d, tab):
    c, s1, s2 = _tab3(tab)
    return d * c + pltpu.roll(d * s1, ROT_DIM // 2, 1) + pltpu.roll(d * s2, LANES - ROT_DIM // 2, 1)


def _head_pads(a):
    lo = lax.broadcasted_iota(jnp.int32, a.shape, 1) < HEAD_DIM
    nat0 = jnp.where(lo, a, 0.0)
    nat1 = jnp.where(lo, 0.0, a)
    return {
        (0, 0): nat0.astype(BF16), (0, 1): pltpu.roll(nat0, HEAD_DIM, 1).astype(BF16),
        (1, 0): pltpu.roll(nat1, HEAD_DIM, 1).astype(BF16), (1, 1): nat1.astype(BF16),
    }


def _from_pads(even, odd, kv):
    lo = lax.broadcasted_iota(jnp.int32, even.shape, 1) < HEAD_DIM
    if kv == 0:
        return jnp.where(lo, even + pltpu.roll(odd, HEAD_DIM, 1), 0.0)
    return jnp.where(lo, 0.0, pltpu.roll(even, HEAD_DIM, 1) + odd)


N_GROUPS = 4


def _group_head(g, r):
    kv, par = divmod(g, 2)
    return 2 * (2 * kv + r) + par


def _window_mask_t(has_prev):
    jj = lax.broadcasted_iota(jnp.int32, (2 * BLOCK, 2 * BLOCK), 0)
    ii = lax.broadcasted_iota(jnp.int32, (2 * BLOCK, 2 * BLOCK), 1) & (BLOCK - 1)
    rel = jj - BLOCK - ii
    return (rel <= 0) & (rel > -BLOCK) & ((jj >= BLOCK) | has_prev)


def _sink_row(sink_ref, g):
    lane = lax.broadcasted_iota(jnp.int32, (1, 2 * BLOCK), 1)
    return jnp.where(lane < BLOCK, sink_ref[0, _group_head(g, 0)], sink_ref[0, _group_head(g, 1)])


def _attn_probs_t(q2, kp, mask, sink_ref):
    out = []
    for kv in range(2):
        q_st = jnp.concatenate([q2[2 * kv], q2[2 * kv + 1]], axis=0)
        for par in range(2):
            s = jnp.where(mask, _dot_nt(kp[(kv, par)], q_st) * ATTN_SCALE, MASK_VALUE)
            sink = _sink_row(sink_ref, 2 * kv + par)
            m = jnp.maximum(jnp.max(s, axis=0, keepdims=True), sink)
            p = jnp.exp(s - m)
            esink = jnp.exp(sink - m)
            rden = 1.0 / (jnp.sum(p, axis=0, keepdims=True) + esink)
            out.append((p * rden, esink * rden))
    return out


def _conv_taps(cg, u, cg_prev, u_prev, has_prev):
    vv = cg * u
    halo = jnp.where(has_prev, cg_prev * u_prev, 0.0)
    ext = jnp.concatenate([halo, vv], axis=0)
    rows = ext.shape[0]
    vv1 = pltpu.roll(ext, 1, 0)[8:rows]
    vv2 = pltpu.roll(ext, 2, 0)[8:rows]
    return vv, vv1, vv2


def _mix_specs(nb):
    cur = lambda n: jnp.minimum(n, nb - 1)
    prev = lambda n: jnp.maximum(jnp.minimum(n, nb - 1) - 1, 0)
    rows8_prev = lambda n: jnp.maximum(16 * jnp.minimum(n, nb - 1) - 1, 0)
    return cur, prev, [
        pl.BlockSpec((BLOCK, Z_END), lambda n: (cur(n), 0)),
        pl.BlockSpec((BLOCK, 2 * LANES), lambda n: (prev(n), Z_K // (2 * LANES))),
        pl.BlockSpec((8, CONV_WIDTH), lambda n: (rows8_prev(n), 1)),
        pl.BlockSpec((8, CONV_WIDTH), lambda n: (rows8_prev(n), 2)),
        pl.BlockSpec((BLOCK, 3 * LANES), lambda n: (cur(n), 0)),
        pl.BlockSpec((BLOCK, 3 * LANES), lambda n: (prev(n), 0)),
        pl.BlockSpec((3, CONV_WIDTH), lambda n: (0, 0)),
        pl.BlockSpec(memory_space=pltpu.SMEM),
    ]


def _mix_core_fwd(z, tab, conv_w, sinks, name):
    T = z.shape[0]
    nb = T // BLOCK
    _, _, specs = _mix_specs(nb)

    def body(z_ref, zkvp_ref, cgp_ref, up_ref, tab_ref, tabp_ref, cw_ref, sink_ref, y_ref):
        has_prev = pl.program_id(0) > 0
        bg = z_ref[:, 0:CONV_WIDTH]
        vv, vv1, vv2 = _conv_taps(z_ref[:, CONV_WIDTH:2 * CONV_WIDTH], z_ref[:, 2 * CONV_WIDTH:Z_Q],
                                  cgp_ref[...], up_ref[...], has_prev)
        conv = cw_ref[0:1, :] * vv2 + cw_ref[1:2, :] * vv1 + cw_ref[2:3, :] * vv
        y_ref[:, 0:CONV_WIDTH] = (bg * conv).astype(BF16)

        tab_c = tab_ref[...]
        tab_p = tabp_ref[...]
        k_all = jnp.concatenate([_rot(zkvp_ref[:, 0:LANES], tab_p), _rot(z_ref[:, Z_K:Z_V], tab_c)], axis=0)
        v_all = jnp.concatenate([zkvp_ref[:, LANES:2 * LANES], z_ref[:, Z_V:Z_END]], axis=0)
        kp = _head_pads(k_all)
        vp = _head_pads(v_all)
        q2 = [_rot(z_ref[:, Z_Q + LANES * c:Z_Q + LANES * (c + 1)], tab_c).astype(BF16) for c in range(N_Q_HEADS // 2)]
        probs = _attn_probs_t(q2, kp, _window_mask_t(has_prev), sink_ref)
        for kv in range(2):
            o_t = (_dot_tn(vp[(kv, 0)], probs[2 * kv][0].astype(BF16))
                   + _dot_tn(vp[(kv, 1)], probs[2 * kv + 1][0].astype(BF16)))
            for r in range(2):
                c = 2 * kv + r
                y_ref[:, CONV_WIDTH + LANES * c:CONV_WIDTH + LANES * (c + 1)] = o_t[:, BLOCK * r:BLOCK * (r + 1)].T.astype(BF16)

    return _pcall(
        body, name=name, grid=(nb,), in_specs=specs,
        out_specs=pl.BlockSpec((BLOCK, 2 * CONV_WIDTH), lambda n: (n, 0)),
        out_shape=jax.ShapeDtypeStruct((T, 2 * CONV_WIDTH), BF16),
        compiler_params=_params(1),
    )(z, z, z, z, tab, tab, conv_w, sinks)


def _mix_core_bwd(z, dy, tab, conv_w, sinks, name, plan=None):
    T = z.shape[0]
    nb = T // BLOCK
    cur, _, specs = _mix_specs(nb)
    rows8_next = lambda n: jnp.minimum(16 * (cur(n) + 1), 16 * nb - 1)
    specs = specs[:4] + [
        pl.BlockSpec((8, CONV_WIDTH), lambda n: (rows8_next(n), 0)),
        pl.BlockSpec((BLOCK, 2 * CONV_WIDTH), lambda n: (cur(n), 0)),
        pl.BlockSpec((8, CONV_WIDTH), lambda n: (rows8_next(n), 0)),
    ] + specs[4:]

    def body(*refs):
        ins, outs, scratch, cr = _unpack(refs, 11, 3, plan)
        z_ref, zkvp_ref, cgp_ref, up_ref, bgn_ref, dy_ref, dyn_ref, tab_ref, tabp_ref, cw_ref, sink_ref = ins
        dz_ref, dcw_ref, dsk_ref = outs
        main_ref, kv_ref = scratch
        n = pl.program_id(0)
        _hook(plan, cr, "start", n == 0)

        @pl.when(n == 0)
        def _():
            main_ref[...] = jnp.zeros_like(main_ref)
            kv_ref[...] = jnp.zeros_like(kv_ref)
            dcw_ref[...] = jnp.zeros_like(dcw_ref)
            dsk_ref[...] = jnp.zeros_like(dsk_ref)

        @pl.when(n < nb)
        def _():
            has_prev = n > 0
            has_next = n < nb - 1
            bg = z_ref[:, 0:CONV_WIDTH]
            cg = z_ref[:, CONV_WIDTH:2 * CONV_WIDTH]
            u = z_ref[:, 2 * CONV_WIDTH:Z_Q]
            vv, vv1, vv2 = _conv_taps(cg, u, cgp_ref[...], up_ref[...], has_prev)
            w0, w1, w2 = cw_ref[0:1, :], cw_ref[1:2, :], cw_ref[2:3, :]
            dyc = dy_ref[:, 0:CONV_WIDTH]
            dbg = dyc * (w0 * vv2 + w1 * vv1 + w2 * vv)
            dconv = dyc * bg
            dconv_next = jnp.where(has_next, dyn_ref[...] * bgn_ref[...], 0.0)
            ext = jnp.concatenate([dconv, dconv_next], axis=0)
            rows = ext.shape[0]
            dvv = w2 * dconv + w1 * pltpu.roll(ext, rows - 1, 0)[0:BLOCK] + w0 * pltpu.roll(ext, rows - 2, 0)[0:BLOCK]
            dcw_ref[0:1, :] += jnp.sum(dconv * vv2, axis=0, keepdims=True)
            dcw_ref[1:2, :] += jnp.sum(dconv * vv1, axis=0, keepdims=True)
            dcw_ref[2:3, :] += jnp.sum(dconv * vv, axis=0, keepdims=True)

            tab_c = tab_ref[...]
            tab_p = tabp_ref[...]
            k_all = jnp.concatenate([_rot(zkvp_ref[:, 0:LANES], tab_p), _rot(z_ref[:, Z_K:Z_V], tab_c)], axis=0)
            v_all = jnp.concatenate([zkvp_ref[:, LANES:2 * LANES], z_ref[:, Z_V:Z_END]], axis=0)
            kp = _head_pads(k_all)
            vp = _head_pads(v_all)
            chunks = range(N_Q_HEADS // 2)
            q2 = [_rot(z_ref[:, Z_Q + LANES * c:Z_Q + LANES * (c + 1)], tab_c).astype(BF16) for c in chunks]
            do2 = [dy_ref[:, CONV_WIDTH + LANES * c:CONV_WIDTH + LANES * (c + 1)].astype(BF16) for c in chunks]
            probs = _attn_probs_t(q2, kp, _window_mask_t(has_prev), sink_ref)
            dq_chunks = []
            dk_nat = jnp.zeros((2 * BLOCK, LANES), F32)
            dv_nat = jnp.zeros((2 * BLOCK, LANES), F32)
            for kv in range(2):
                q_st = jnp.concatenate([q2[2 * kv], q2[2 * kv + 1]], axis=0)
                do_st = jnp.concatenate([do2[2 * kv], do2[2 * kv + 1]], axis=0)
                dq_t = jnp.zeros((LANES, 2 * BLOCK), F32)
                dk_par, dv_par = [], []
                for par in range(2):
                    g = 2 * kv + par
                    pr, psink = probs[g]
                    dp = _dot_nt(vp[(kv, par)], do_st)
                    delta = jnp.sum(dp * pr, axis=0, keepdims=True)
                    ds = (pr * (dp - delta) * ATTN_SCALE).astype(BF16)
                    dsink = -psink * delta
                    for r in range(2):
                        h = _group_head(g, r)
                        dsk_ref[h:h + 1, :] += jnp.sum(dsink[:, BLOCK * r:BLOCK * (r + 1)])
                    dq_t = dq_t + _dot_tn(kp[(kv, par)], ds)
                    dk_par.append(_dot(ds, q_st))
                    dv_par.append(_dot(pr.astype(BF16), do_st))
                for r in range(2):
                    dq_chunks.append(_rot_t(dq_t[:, BLOCK * r:BLOCK * (r + 1)].T, tab_c))
                dk_nat = dk_nat + _from_pads(dk_par[0], dk_par[1], kv)
                dv_nat = dv_nat + _from_pads(dv_par[0], dv_par[1], kv)

            dk_prev = _rot_t(kv_ref[:, 0:LANES] + dk_nat[0:BLOCK], tab_p)
            dv_prev = kv_ref[:, LANES:2 * LANES] + dv_nat[0:BLOCK]
            dz_ref[:, 0:Z_K] = main_ref[...]
            dz_ref[:, Z_K:Z_V] = dk_prev.astype(BF16)
            dz_ref[:, Z_V:Z_END] = dv_prev.astype(BF16)
            main_ref[:, 0:CONV_WIDTH] = dbg.astype(BF16)
            main_ref[:, CONV_WIDTH:2 * CONV_WIDTH] = (dvv * u).astype(BF16)
            main_ref[:, 2 * CONV_WIDTH:Z_Q] = (dvv * cg).astype(BF16)
            for c in range(N_Q_HEADS // 2):
                main_ref[:, Z_Q + LANES * c:Z_Q + LANES * (c + 1)] = dq_chunks[c].astype(BF16)
            kv_ref[:, 0:LANES] = dk_nat[BLOCK:2 * BLOCK]
            kv_ref[:, LANES:2 * LANES] = dv_nat[BLOCK:2 * BLOCK]

        @pl.when(n == nb)
        def _():
            dz_ref[:, 0:Z_K] = main_ref[...]
            dz_ref[:, Z_K:Z_V] = _rot_t(kv_ref[:, 0:LANES], tab_ref[...]).astype(BF16)
            dz_ref[:, Z_V:Z_END] = kv_ref[:, LANES:2 * LANES].astype(BF16)

        _hook(plan, cr, "finish", n == nb)

    io = _carried(
        plan, specs,
        [pl.BlockSpec((BLOCK, Z_END), lambda n: (jnp.maximum(n - 1, 0), 0)),
         pl.BlockSpec((8, CONV_WIDTH), lambda n: (0, 0)), pl.BlockSpec((8, LANES), lambda n: (0, 0))],
        [jax.ShapeDtypeStruct((T, Z_END), BF16), jax.ShapeDtypeStruct((8, CONV_WIDTH), F32),
         jax.ShapeDtypeStruct((8, LANES), F32)],
        [pltpu.VMEM((BLOCK, Z_K), BF16), pltpu.VMEM((BLOCK, 2 * LANES), F32)])
    return _pcall(
        body, name=name, grid=(nb + 1,), compiler_params=_params(1), **io,
    )(z, z, z, z, z, dy, dy, tab, tab, conv_w, sinks, *(plan.arrays if plan else ()))


ROW_SPLIT = 2


def _pair_sum(grads, recvd, core, name):
    n = len(grads)

    def body(core_ref, *refs):
        g, r = refs[:n], refs[n:2 * n]
        s, sb = refs[2 * n:3 * n], refs[3 * n:]
        for t in range(n):
            tot = g[t][...] + r[t][...].astype(F32)
            s[t][...] = tot
            sb[t][...] = tot.astype(BF16)

    def blk(a):
        return (1, a.shape[1] // ROW_SPLIT, a.shape[2])

    in_specs = [pl.BlockSpec(blk(r), lambda q, i, core_ref: (q, core_ref[0] * ROW_SPLIT + i, 0)) for r in recvd]
    in_specs += [pl.BlockSpec(blk(r), lambda q, i, core_ref: (q, i, 0)) for r in recvd]
    out_specs = [pl.BlockSpec(blk(r), lambda q, i, core_ref: (q, i, 0)) for r in recvd] * 2
    return _pcall(
        body, name=name,
        grid_spec=pltpu.PrefetchScalarGridSpec(num_scalar_prefetch=1, grid=(N_CHIPS, ROW_SPLIT),
                                               in_specs=in_specs, out_specs=out_specs),
        out_shape=[jax.ShapeDtypeStruct(r.shape, F32) for r in recvd] + [jax.ShapeDtypeStruct(r.shape, BF16) for r in recvd],
        compiler_params=_params(2),
    )(core, *grads, *recvd)


def _chip_sum(parts, recvd, place, name):
    n = len(parts)

    def body(place_ref, *refs):
        p, r, o = refs[:n], refs[n:2 * n], refs[2 * n:]
        for t in range(n):
            tot = p[t][0]
            for j in range(3):
                tot = tot + r[t][j].astype(F32)
            o[t][...] = tot

    in_specs = [pl.BlockSpec((1, p.shape[1] // ROW_SPLIT, p.shape[2]), lambda i, place_ref: (place_ref[0], i, 0))
                for p in parts]
    in_specs += [pl.BlockSpec((3, r.shape[1] // ROW_SPLIT, r.shape[2]), lambda i, place_ref: (0, i, 0)) for r in recvd]
    out_specs = [pl.BlockSpec((p.shape[1] // ROW_SPLIT, p.shape[2]),
                              lambda i, place_ref: (place_ref[1] * ROW_SPLIT + i, 0)) for p in parts]
    return _pcall(
        body, name=name,
        grid_spec=pltpu.PrefetchScalarGridSpec(num_scalar_prefetch=1, grid=(ROW_SPLIT,),
                                               in_specs=in_specs, out_specs=out_specs),
        out_shape=[jax.ShapeDtypeStruct((2 * p.shape[1], p.shape[2]), F32) for p in parts],
        compiler_params=_params(1),
    )(place, *parts, *recvd)


def _adamw_math(w, g, m, v):
    m = ADAM_B1 * m + (1.0 - ADAM_B1) * g
    v = ADAM_B2 * v + (1.0 - ADAM_B2) * (g * g)
    m_hat = m / (1.0 - ADAM_B1 ** ADAM_STEP)
    v_hat = v / (1.0 - ADAM_B2 ** ADAM_STEP)
    delta = -ADAM_LR * (m_hat / (jnp.sqrt(v_hat) + ADAM_EPS) + ADAM_WD * w)
    return delta, m, v


def _adamw(ws, gs, ms, vs, row_blocks, name, plan=None):
    n = len(ws)

    def body(*refs):
        ins, outs, _, cr = _unpack(refs, 4 * n, 3 * n, plan)
        w, g, m, v = ins[:n], ins[n:2 * n], ins[2 * n:3 * n], ins[3 * n:]
        d, mo, vo = outs[:n], outs[n:2 * n], outs[2 * n:]
        i = pl.program_id(0)
        _hook(plan, cr, "start", i == 0)
        for t in range(n):
            delta, m_new, v_new = _adamw_math(w[t][...], g[t][...], m[t][...], v[t][...])
            d[t][...] = delta
            mo[t][...] = m_new
            vo[t][...] = v_new
        _hook(plan, cr, "finish", i == row_blocks - 1)

    specs = [pl.BlockSpec((a.shape[0] // row_blocks, a.shape[1]), lambda i: (i, 0)) for a in ws]
    shapes = [jax.ShapeDtypeStruct(a.shape, F32) for a in ws]
    io = _carried(plan, specs * 4, specs * 3, shapes * 3, [])
    return _pcall(body, name=name, grid=(row_blocks,), compiler_params=_params(1), **io,
                  )(*ws, *gs, *ms, *vs, *(plan.arrays if plan else ()))


def kernel(x, ffn1_norm, ffn1_w_gate, ffn1_w_up, ffn1_w_down, mix_norm, w_in, conv_w, attn_sinks, w_out, ffn2_norm, ffn2_w_gate, ffn2_w_up, ffn2_w_down, final_norm, loss_target, m_ffn1_norm, m_ffn1_w_gate, m_ffn1_w_up, m_ffn1_w_down, m_mix_norm, m_w_in, m_conv_w, m_attn_sinks, m_w_out, m_ffn2_norm, m_ffn2_w_gate, m_ffn2_w_up, m_ffn2_w_down, m_final_norm, v_ffn1_norm, v_ffn1_w_gate, v_ffn1_w_up, v_ffn1_w_down, v_mix_norm, v_w_in, v_conv_w, v_attn_sinks, v_w_out, v_ffn2_norm, v_ffn2_w_gate, v_ffn2_w_up, v_ffn2_w_down, v_final_norm):
    T, D = x.shape[1], x.shape[2]
    chip = (2 * lax.axis_index("x") + lax.axis_index("y")).astype(jnp.int32)
    core = lax.axis_index("c").astype(jnp.int32)
    place = jnp.stack([chip, core])
    x0 = x[0]
    target = loss_target[0]
    gf = final_norm.reshape(1, D)

    tr = lambda w: jnp.swapaxes(w[0], 0, 1)
    big = [tr(ffn1_w_gate), tr(ffn1_w_up), ffn1_w_down[0], tr(w_in), w_out[0], tr(ffn2_w_gate), tr(ffn2_w_up), ffn2_w_down[0]]
    transposed = [True, True, False, True, False, True, True, False]
    own_b = [w.astype(BF16) for w in big]

    def whole(gathered, own):
        return lax.dynamic_update_slice(gathered, own[None], (chip, 0, 0)).reshape(-1, D)

    got1 = _run_comm(_gather_plan(own_b[0:3]), "gather_ffn1")
    wg1, wu1, wd1 = (whole(g, o) for g, o in zip(got1, own_b[0:3]))
    tab = _rope_tables(T)

    res = _ffn_fwd(x0, ffn1_norm, wg1, wu1, wd1, "ffn1_fwd", _gather_plan(own_b[3:8], [conv_w[0]]))
    x1, h1, gate1, up1, act1 = res[:5]
    win, wout, wg2, wu2, wd2 = (whole(g, o) for g, o in zip(res[5:10], own_b[3:8]))
    convw4 = lax.dynamic_update_slice(res[10], conv_w, (chip, 0, 0))
    convw = jnp.transpose(convw4, (1, 0, 2)).reshape(3, -1)
    z, hm = _norm_matmul(x1, mix_norm, win, "mix_in_fwd")
    ymix = _mix_core_fwd(z, tab, convw, attn_sinks, "mix_core_fwd")
    x2 = _matmul_residual(ymix, wout, x1, "mix_out_fwd")
    x3, h2, gate2, up2, act2 = _ffn_fwd(x2, ffn2_norm, wg2, wu2, wd2, "ffn2_fwd")
    dx3, dgf, loss_part = _loss_head(x3, gf, target, "loss_head")

    dx2, dyb2, dgate2, dup2, dg2 = _ffn_bwd(dx3, x2, ffn2_norm, gate2, up2, wg2, wu2, wd2, "ffn2_bwd")
    dymix, dx2b = _matmul_nt(dx2, wout, "mix_out_bwd")
    dz, dcw, dsk = _mix_core_bwd(z, dymix, tab, convw, attn_sinks, "mix_core_bwd")
    dx1, dgm = _matmul_norm_bwd(dz, win, x1, mix_norm, dx2, "mix_in_bwd")
    dx0, dyb1, dgate1, dup1, dg1 = _ffn_bwd(dx1, x0, ffn1_norm, gate1, up1, wg1, wu1, wd1, "ffn1_bwd")

    pad = lambda a: jnp.pad(a, ((0, 0), (0, LANES - a.shape[1])))
    vec = jnp.concatenate([dg1, dgm, dg2, dgf, dcw[0:3].reshape(1, -1), pad(dsk[:, 0].reshape(1, -1)),
                           pad(loss_part[:, 0:1])], axis=1)

    jobs = [("ffn2_dwg", dgate2, h2, 5), ("ffn2_dwu", dup2, h2, 6), ("ffn2_dwd", act2, dyb2, 7),
            ("ffn1_dwg", dgate1, h1, 0), ("ffn1_dwu", dup1, h1, 1), ("ffn1_dwd", act1, dyb1, 2),
            ("mix_dwin", dz, hm, 3), ("mix_dwout", ymix, dx2b, 4)]
    n_jobs = len(jobs)
    grad, grad_b, from_sib, pair_f, pair_b, from_chips, half, g_big = ({} for _ in range(8))

    def stage_plans(t):
        plans, takers = [], []
        if 0 <= t - 1 < n_jobs:
            plans.append(_sibling_plan([grad_b[t - 1]]))
            takers.append((from_sib, t - 1))
        if 0 <= t - 2 < n_jobs:
            plans.append(_scatter_plan([pair_b[t - 2]]))
            takers.append((from_chips, t - 2))
        if 0 <= t - 3 < n_jobs:
            plans.append(_join_plan([half[t - 3]]))
            takers.append((g_big, jobs[t - 3][3]))
        return plans, takers

    def after_stage(t, landed, takers):
        for (store, key), arr in zip(takers, landed):
            store[key] = arr
        if 0 <= t - 1 < n_jobs:
            pair_f[t - 1], pair_b[t - 1] = _pair_sum([grad[t - 1]], [from_sib[t - 1]], core.reshape(1), f"pair_sum_{t - 1}")
        if 0 <= t - 2 < n_jobs:
            half[t - 2], = _chip_sum([pair_f[t - 2]], [from_chips[t - 2]], place, f"chip_sum_{t - 2}")

    for t, (name_, a, b, _) in enumerate(jobs):
        plans, takers = stage_plans(t)
        if t == 0:
            plans.append(_all_gather_plan(jnp.pad(vec, ((0, 7), (0, 0)))))
        res = _matmul_tn(a, b, DW_ROW_SPLIT, name_, _merge_plans(plans))
        grad[t], grad_b[t] = (r.reshape(N_CHIPS, -1, D) for r in res[:2])
        landed = list(res[2:])
        if t == 0:
            vec_blocks = landed.pop()
        after_stage(t, landed, takers)

    ws = big
    ms = [tr(m_ffn1_w_gate), tr(m_ffn1_w_up), m_ffn1_w_down[0], tr(m_w_in), m_w_out[0], tr(m_ffn2_w_gate), tr(m_ffn2_w_up), m_ffn2_w_down[0]]
    vs = [tr(v_ffn1_w_gate), tr(v_ffn1_w_up), v_ffn1_w_down[0], tr(v_w_in), v_w_out[0], tr(v_ffn2_w_gate), tr(v_ffn2_w_up), v_ffn2_w_down[0]]
    upd = {}
    tail = [(n_jobs, [j[3] for j in jobs[:5]]), (n_jobs + 1, [jobs[5][3]]), (n_jobs + 2, [jobs[6][3]]), (None, [jobs[7][3]])]
    for t, idx in tail:
        plans, takers = stage_plans(t) if t is not None else ([], [])
        k = len(idx)
        res = _adamw([ws[i] for i in idx], [g_big[i] for i in idx], [ms[i] for i in idx], [vs[i] for i in idx], 8,
                     f"adamw_{idx[0]}", _merge_plans(plans))
        for j, i in enumerate(idx):
            upd[i] = (res[j], res[k + j], res[2 * k + j])
        if t is not None:
            after_stage(t, list(res[3 * k:]), takers)

    total = _sum_devices(vec_blocks, "small_sum")[0:1]
    g_n1, g_nm, g_n2, g_nf = (total[:, k * D:(k + 1) * D] for k in range(4))
    cw_full = total[:, 4 * D:4 * D + 3 * CONV_WIDTH].reshape(3, CONV_WIDTH)
    cq = CONV_WIDTH // N_CHIPS
    g_cw = lax.dynamic_slice(cw_full, (0, chip * cq), (3, cq))
    off = 4 * D + 3 * CONV_WIDTH
    g_sk = total[:, off:off + N_Q_HEADS]
    loss = total[0, off + LANES]

    sw = [ffn1_norm, mix_norm, conv_w[0], attn_sinks, ffn2_norm, gf]
    sg = [g_n1, g_nm, g_cw, g_sk, g_n2, g_nf]
    sm = [m_ffn1_norm, m_mix_norm, m_conv_w[0], m_attn_sinks, m_ffn2_norm, m_final_norm.reshape(1, D)]
    sv = [v_ffn1_norm, v_mix_norm, v_conv_w[0], v_attn_sinks, v_ffn2_norm, v_final_norm.reshape(1, D)]
    sres = _adamw(sw, sg, sm, sv, 1, "adamw_small")
    supd = [(sres[j], sres[6 + j], sres[12 + j]) for j in range(6)]

    order = [("s", 0), ("b", 0), ("b", 1), ("b", 2), ("s", 1), ("b", 3), ("s", 2), ("s", 3), ("b", 4),
             ("s", 4), ("b", 5), ("b", 6), ("b", 7), ("s", 5)]

    def leaf(kind, i, which):
        if kind == "b":
            a = g_big[i] if which == 0 else upd[i][which - 1]
            return (jnp.swapaxes(a, 0, 1) if transposed[i] else a)[None]
        a = sg[i] if which == 0 else supd[i][which - 1]
        if i == 2:
            return a[None]
        if i == 5:
            return a.reshape(D)
        return a

    outs = [loss, dx0[None]]
    for which in range(4):
        outs += [leaf(kind, i, which) for kind, i in order]
    return tuple(outs)
```

```python
import functools

import jax
import jax.numpy as jnp
import numpy as np
from jax import lax
from jax.experimental import pallas as pl
from jax.experimental.pallas import tpu as pltpu

F32 = jnp.float32
BF16 = jnp.bfloat16
MESH = pl.DeviceIdType.MESH

CONV_WIDTH = 512
N_Q_HEADS = 8
HEAD_DIM = 64
BLOCK = 128
ROPE_THETA = 500000.0
ROT_DIM = 16
RMS_EPS = 1e-5
MASK_VALUE = -1e30
ATTN_SCALE = HEAD_DIM ** -0.5
FFN_RES_SCALE = 0.5
ADAM_LR = 0.001
ADAM_B1 = 0.9
ADAM_B2 = 0.999
ADAM_EPS = 1e-08
ADAM_WD = 0.01
ADAM_STEP = 10

N_CHIPS = 4
N_DEV = 8
LANES = 128
VMEM_LIMIT = 56 * 1024 * 1024

_pcall = pl.pallas_call
HBM_SPEC = pl.BlockSpec(memory_space=pltpu.HBM)
ANY_SPEC = pl.BlockSpec(memory_space=pl.ANY)


def _params(n_axes, vmem=VMEM_LIMIT):
    return pltpu.CompilerParams(dimension_semantics=("arbitrary",) * n_axes, vmem_limit_bytes=vmem)


def _dot(a, b):
    return jnp.dot(a, b, preferred_element_type=F32)


def _dot_nt(a, b):
    return lax.dot_general(a, b, (((1,), (1,)), ((), ())), preferred_element_type=F32)


def _dot_tn(a, b):
    return lax.dot_general(a, b, (((0,), (0,)), ((), ())), preferred_element_type=F32)


def _rms_inv(x):
    return lax.rsqrt(jnp.mean(x * x, axis=-1, keepdims=True) + RMS_EPS)


def _norm_bwd(dh, x, g):
    inv = _rms_inv(x)
    xhat = x * inv
    dg = jnp.sum(dh * xhat, axis=0, keepdims=True)
    dxhat = dh * g
    dx = inv * (dxhat - xhat * jnp.mean(dxhat * xhat, axis=-1, keepdims=True))
    return dx, dg


def _place():
    x, y, c = lax.axis_index("x"), lax.axis_index("y"), lax.axis_index("c")
    chips = [(1 - x, y), (x, 1 - y), (1 - x, 1 - y)]
    return x, y, c, chips


class _Plan:
    def __init__(self, arrays, out_shapes, n_sems, start, finish, middle=None, aliases=None):
        self.arrays, self.out_shapes, self.n_sems = list(arrays), list(out_shapes), n_sems
        self.start, self.finish, self.middle = start, finish, middle
        self.aliases = dict(aliases or {})

    def specs(self):
        k = len(self.arrays)
        sems = [pltpu.SemaphoreType.DMA((self.n_sems,)), pltpu.SemaphoreType.DMA((self.n_sems,))]
        return [HBM_SPEC] * k, [HBM_SPEC] * len(self.out_shapes), self.out_shapes, sems


class _SemSlice:
    def __init__(self, ref, offset):
        self.ref, self.offset = ref, offset

    @property
    def at(self):
        return self

    def __getitem__(self, k):
        return self.ref.at[k + self.offset]


def _merge_plans(plans):
    plans = [p for p in plans if p is not None]
    if len(plans) <= 1:
        return plans[0] if plans else None
    arrays, shapes, aliases, spans, n_sems = [], [], {}, [], 0
    for p in plans:
        a0, o0 = len(arrays), len(shapes)
        spans.append((a0, a0 + len(p.arrays), o0, o0 + len(p.out_shapes), n_sems))
        aliases.update({a0 + i: o0 + j for i, j in p.aliases.items()})
        arrays += p.arrays
        shapes += p.out_shapes
        n_sems += p.n_sems

    def run(which):
        def fn(ins, outs, send_sems, recv_sems):
            for p, (a0, a1, o0, o1, s0) in zip(plans, spans):
                part = getattr(p, which)
                if part is not None:
                    part(ins[a0:a1], outs[o0:o1], _SemSlice(send_sems, s0), _SemSlice(recv_sems, s0))
        return fn

    middle = run("middle") if any(p.middle is not None for p in plans) else None
    return _Plan(arrays, shapes, n_sems, run("start"), run("finish"), middle, aliases)


def _sibling_plan(grads_b):
    n = len(grads_b)

    def copies(ins, outs, send_sems, recv_sems):
        x, y, c, _ = _place()

        def copy(t):
            half = ins[t].shape[1] // 2
            return pltpu.make_async_remote_copy(
                src_ref=ins[t].at[:, pl.ds(pl.multiple_of((1 - c) * half, 16), half), :], dst_ref=outs[t],
                send_sem=send_sems.at[t], recv_sem=recv_sems.at[t], device_id=(x, y, 1 - c), device_id_type=MESH)

        return [copy(t) for t in range(n)]

    def start(*refs):
        for cp in copies(*refs):
            cp.start()

    def finish(*refs):
        for cp in copies(*refs):
            cp.wait()

    shapes = [jax.ShapeDtypeStruct((g.shape[0], g.shape[1] // 2, g.shape[2]), g.dtype) for g in grads_b]
    return _Plan(grads_b, shapes, n, start, finish)


def _scatter_plan(parts_b):
    n = len(parts_b)

    def copies(ins, outs, send_sems, recv_sems):
        x, y, c, chips = _place()

        def copy(t, j):
            px, py = chips[j]
            return pltpu.make_async_remote_copy(
                src_ref=ins[t].at[2 * px + py], dst_ref=outs[t].at[j], send_sem=send_sems.at[3 * t + j],
                recv_sem=recv_sems.at[3 * t + j], device_id=(px, py, c), device_id_type=MESH)

        return [copy(t, j) for t in range(n) for j in range(3)]

    def start(*refs):
        for cp in copies(*refs):
            cp.start()

    def finish(*refs):
        for cp in copies(*refs):
            cp.wait()

    shapes = [jax.ShapeDtypeStruct((3, *p.shape[1:]), p.dtype) for p in parts_b]
    return _Plan(parts_b, shapes, 3 * n, start, finish)


def _gather_plan(shards, small=()):
    n, ns = len(shards), len(small)

    def parts(ins, outs, send_sems, recv_sems):
        x, y, c, chips = _place()
        me = 2 * x + y

        def rows(t, core):
            half = ins[t].shape[0] // 2
            return pl.ds(pl.multiple_of(core * half, 16), half)

        def first(t, j, block, core):
            return pltpu.make_async_remote_copy(
                src_ref=ins[t].at[rows(t, core), :], dst_ref=outs[t].at[block, rows(t, core), :],
                send_sem=send_sems.at[6 * t + j], recv_sem=recv_sems.at[6 * t + j],
                device_id=(*chips[j], c), device_id_type=MESH)

        def passed(t, j, block, core):
            ref = outs[t].at[block, rows(t, core), :]
            return pltpu.make_async_remote_copy(
                src_ref=ref, dst_ref=ref, send_sem=send_sems.at[6 * t + 3 + j], recv_sem=recv_sems.at[6 * t + 3 + j],
                device_id=(x, y, 1 - c), device_id_type=MESH)

        def whole(s, j, block):
            k = 6 * n + 3 * s + j
            return pltpu.make_async_remote_copy(
                src_ref=ins[n + s], dst_ref=outs[n + s].at[block], send_sem=send_sems.at[k], recv_sem=recv_sems.at[k],
                device_id=(*chips[j], c), device_id_type=MESH)

        blocks = [2 * px + py for px, py in chips]
        return c, me, blocks, first, passed, whole

    def start(*refs):
        c, me, _, first, _, whole = parts(*refs)
        for t in range(n):
            for j in range(3):
                first(t, j, me, c).start()
        for s in range(ns):
            for j in range(3):
                whole(s, j, me).start()

    def middle(*refs):
        c, _, blocks, first, passed, _ = parts(*refs)
        for t in range(n):
            for j in range(3):
                first(t, j, blocks[j], c).wait_recv()
                passed(t, j, blocks[j], c).start()

    def finish(*refs):
        c, me, blocks, first, passed, whole = parts(*refs)
        for t in range(n):
            for j in range(3):
                passed(t, j, blocks[j], 1 - c).wait_recv()
        for s in range(ns):
            for j in range(3):
                whole(s, j, blocks[j]).wait_recv()
        for t in range(n):
            for j in range(3):
                first(t, j, me, c).wait_send()
                passed(t, j, blocks[j], c).wait_send()
        for s in range(ns):
            for j in range(3):
                whole(s, j, me).wait_send()

    arrays = [*shards, *small]
    shapes = [jax.ShapeDtypeStruct((N_CHIPS, *a.shape), a.dtype) for a in arrays]
    return _Plan(arrays, shapes, 6 * n + 3 * ns, start, finish, middle)


def _run_comm(plan, name):
    k = len(plan.arrays)
    in_specs, out_specs, out_shape, sems = plan.specs()

    def body(*refs):
        cr = (refs[:k], refs[k:k + len(out_shape)], refs[-2], refs[-1])
        plan.start(*cr)
        if plan.middle is not None:
            plan.middle(*cr)
        plan.finish(*cr)

    return _pcall(body, name=name, in_specs=in_specs, out_specs=out_specs, out_shape=out_shape,
                  input_output_aliases=plan.aliases, scratch_shapes=sems)(*plan.arrays)


def _carried(plan, in_specs, out_specs, out_shape, scratch):
    aliases = {}
    if plan is not None:
        p_in, p_out, p_shape, p_sems = plan.specs()
        aliases = {len(in_specs) + i: len(out_specs) + j for i, j in plan.aliases.items()}
        in_specs, out_specs = in_specs + p_in, out_specs + p_out
        out_shape, scratch = out_shape + p_shape, scratch + p_sems
    return dict(in_specs=in_specs, out_specs=out_specs, out_shape=out_shape, scratch_shapes=scratch,
                input_output_aliases=aliases)


def _unpack(refs, n_in, n_out, plan):
    k_in = len(plan.arrays) if plan else 0
    k_out = len(plan.out_shapes) if plan else 0
    ins = refs[:n_in]
    outs = refs[n_in + k_in:n_in + k_in + n_out]
    rest = refs[n_in + k_in + n_out + k_out:]
    if plan is None:
        return ins, outs, rest, None
    cr = (refs[n_in:n_in + k_in], refs[n_in + k_in + n_out:n_in + k_in + n_out + k_out], rest[-2], rest[-1])
    return ins, outs, rest[:-2], cr


def _hook(plan, cr, which, cond):
    fn = getattr(plan, which) if plan is not None else None
    if fn is not None:
        pl.when(cond)(lambda: fn(*cr))


def _join_plan(shards):
    n = len(shards)

    def copy(ins, outs, send_sems, recv_sems, t, core):
        x, y, c, _ = _place()
        half = ins[t].shape[0] // 2
        rows = pl.ds(pl.multiple_of(core * half, 8), half)
        return pltpu.make_async_remote_copy(
            src_ref=ins[t].at[rows, :], dst_ref=outs[t].at[rows, :], send_sem=send_sems.at[t],
            recv_sem=recv_sems.at[t], device_id=(x, y, 1 - c), device_id_type=MESH)

    def start(*refs):
        c = lax.axis_index("c")
        for t in range(n):
            copy(*refs, t, c).start()

    def finish(*refs):
        c = lax.axis_index("c")
        for t in range(n):
            copy(*refs, t, 1 - c).wait_recv()
        for t in range(n):
            copy(*refs, t, c).wait_send()

    shapes = [jax.ShapeDtypeStruct(s.shape, s.dtype) for s in shards]
    return _Plan(shards, shapes, n, start, finish, aliases={t: t for t in range(n)})


def _all_gather_plan(vec):
    def parts(ins, outs, send_sems, recv_sems):
        x, y, c, _ = _place()
        me = 4 * x + 2 * y + c
        rel = [((k >> 2) & 1, (k >> 1) & 1, k & 1) for k in range(1, N_DEV)]

        def peer(k):
            fx, fy, fc = rel[k]
            return (x ^ fx, y ^ fy, c ^ fc)

        def copy(k, dev):
            return pltpu.make_async_remote_copy(
                src_ref=ins[0], dst_ref=outs[0].at[dev], send_sem=send_sems.at[k], recv_sem=recv_sems.at[k],
                device_id=peer(k), device_id_type=MESH)

        mine = pltpu.make_async_copy(ins[0], outs[0].at[me], send_sems.at[N_DEV - 1])
        return me, peer, copy, mine

    def start(*refs):
        me, _, copy, mine = parts(*refs)
        mine.start()
        for k in range(N_DEV - 1):
            copy(k, me).start()

    def finish(*refs):
        me, peer, copy, mine = parts(*refs)
        for k in range(N_DEV - 1):
            px, py, pc = peer(k)
            copy(k, 4 * px + 2 * py + pc).wait_recv()
        for k in range(N_DEV - 1):
            copy(k, me).wait_send()
        mine.wait()

    return _Plan([vec], [jax.ShapeDtypeStruct((N_DEV, *vec.shape), vec.dtype)], N_DEV, start, finish)


def _sum_devices(blocks, name):
    def body(b_ref, o_ref):
        total = b_ref[0]
        for dev in range(1, N_DEV):
            total = total + b_ref[dev]
        o_ref[...] = total

    return _pcall(body, name=name, in_specs=[pl.BlockSpec(memory_space=pltpu.VMEM)],
                  out_specs=pl.BlockSpec(memory_space=pltpu.VMEM),
                  out_shape=jax.ShapeDtypeStruct(blocks.shape[1:], F32))(blocks)


TOKEN_TILE = 512
BWD_TOKEN_TILE = 256
DW_TOKEN_TILE = 2048
DW_ROW_SPLIT = 2
MXU_COLS = 256


def _chunks(n):
    out, c0 = [], 0
    while c0 < n:
        size = min(MXU_COLS, n - c0)
        out.append((c0, size))
        c0 += size
    return out


def _load_weights(hbm_refs, vmem_refs, sems):
    copies = [pltpu.make_async_copy(h, v, sems.at[k]) for k, (h, v) in enumerate(zip(hbm_refs, vmem_refs))]
    for cp in copies:
        cp.start()
    for cp in copies:
        cp.wait()


def _ffn_fwd(x, g, wgt, wut, wd, name, plan=None):
    T, D = x.shape
    F = wgt.shape[0]
    tm = min(T, TOKEN_TILE)
    ni = T // tm

    def body(*refs):
        (x_ref, g_ref, wg_hbm, wu_hbm, wd_hbm), (xo_ref, h_ref, gate_ref, up_ref, act_ref), scratch, cr = _unpack(refs, 5, 5, plan)
        wg_ref, wu_ref, wd_ref, sems = scratch
        i = pl.program_id(0)
        _hook(plan, cr, "start", i == 0)

        @pl.when(i == 0)
        def _():
            _load_weights((wg_hbm, wu_hbm, wd_hbm), (wg_ref, wu_ref, wd_ref), sems)

        xv = x_ref[...]
        h = ((xv * _rms_inv(xv)) * g_ref[...]).astype(BF16)
        h_ref[...] = h
        for c0, size in _chunks(F):
            gate = _dot_nt(h, wg_ref[c0:c0 + size, :])
            up = _dot_nt(h, wu_ref[c0:c0 + size, :])
            gate_ref[:, c0:c0 + size] = gate.astype(BF16)
            up_ref[:, c0:c0 + size] = up.astype(BF16)
            act_ref[:, c0:c0 + size] = (gate * jax.nn.sigmoid(gate) * up).astype(BF16)
        xo_ref[...] = x_ref[...] + FFN_RES_SCALE * _dot(act_ref[...], wd_ref[...])
        _hook(plan, cr, "middle", i == (3 * ni) // 4)
        _hook(plan, cr, "finish", i == ni - 1)

    io = _carried(
        plan,
        [pl.BlockSpec((tm, D), lambda i: (i, 0)), pl.BlockSpec((1, D), lambda i: (0, 0)),
         ANY_SPEC, ANY_SPEC, ANY_SPEC],
        [pl.BlockSpec((tm, D), lambda i: (i, 0)), pl.BlockSpec((tm, D), lambda i: (i, 0)),
         pl.BlockSpec((tm, F), lambda i: (i, 0)), pl.BlockSpec((tm, F), lambda i: (i, 0)),
         pl.BlockSpec((tm, F), lambda i: (i, 0))],
        [jax.ShapeDtypeStruct((T, D), F32), jax.ShapeDtypeStruct((T, D), BF16),
         jax.ShapeDtypeStruct((T, F), BF16), jax.ShapeDtypeStruct((T, F), BF16), jax.ShapeDtypeStruct((T, F), BF16)],
        [pltpu.VMEM((F, D), BF16), pltpu.VMEM((F, D), BF16), pltpu.VMEM((F, D), BF16),
         pltpu.SemaphoreType.DMA((3,))])
    return _pcall(
        body, name=name, grid=(ni,), compiler_params=_params(1), **io,
    )(x, g, wgt, wut, wd, *(plan.arrays if plan else ()))


def _ffn_bwd(dy, x, g, gate, up, wgt, wut, wd, name, plan=None):
    T, D = x.shape
    F = wgt.shape[0]
    tm = min(T, BWD_TOKEN_TILE)
    ni = T // tm

    def body(*refs):
        ins, outs, scratch, cr = _unpack(refs, 8, 5, plan)
        dy_ref, x_ref, g_ref, gate_ref, up_ref, wg_hbm, wu_hbm, wd_hbm = ins
        dx_ref, dyb_ref, dgate_ref, dup_ref, dg_ref = outs
        wg_ref, wu_ref, wd_ref, sems = scratch
        i = pl.program_id(0)
        _hook(plan, cr, "start", i == 0)

        @pl.when(i == 0)
        def _():
            _load_weights((wg_hbm, wu_hbm, wd_hbm), (wg_ref, wu_ref, wd_ref), sems)
            dg_ref[...] = jnp.zeros_like(dg_ref)

        dyb = (FFN_RES_SCALE * dy_ref[...]).astype(BF16)
        dyb_ref[...] = dyb
        for c0, size in _chunks(F):
            dact = _dot_nt(dyb, wd_ref[c0:c0 + size, :])
            gt = gate_ref[:, c0:c0 + size].astype(F32)
            u = up_ref[:, c0:c0 + size].astype(F32)
            sig = jax.nn.sigmoid(gt)
            dup_ref[:, c0:c0 + size] = (dact * (gt * sig)).astype(BF16)
            dgate_ref[:, c0:c0 + size] = (dact * u * (sig * (1.0 + gt * (1.0 - sig)))).astype(BF16)
        dh = _dot(dgate_ref[...], wg_ref[...]) + _dot(dup_ref[...], wu_ref[...])
        dxn, dg = _norm_bwd(dh, x_ref[...], g_ref[...])
        dx_ref[...] = dy_ref[...] + dxn
        dg_ref[...] += dg
        _hook(plan, cr, "finish", i == ni - 1)

    io = _carried(
        plan,
        [pl.BlockSpec((tm, D), lambda i: (i, 0)), pl.BlockSpec((tm, D), lambda i: (i, 0)),
         pl.BlockSpec((1, D), lambda i: (0, 0)),
         pl.BlockSpec((tm, F), lambda i: (i, 0)), pl.BlockSpec((tm, F), lambda i: (i, 0)),
         ANY_SPEC, ANY_SPEC, ANY_SPEC],
        [pl.BlockSpec((tm, D), lambda i: (i, 0)), pl.BlockSpec((tm, D), lambda i: (i, 0)),
         pl.BlockSpec((tm, F), lambda i: (i, 0)), pl.BlockSpec((tm, F), lambda i: (i, 0)),
         pl.BlockSpec((1, D), lambda i: (0, 0))],
        [jax.ShapeDtypeStruct((T, D), F32), jax.ShapeDtypeStruct((T, D), BF16),
         jax.ShapeDtypeStruct((T, F), BF16), jax.ShapeDtypeStruct((T, F), BF16), jax.ShapeDtypeStruct((1, D), F32)],
        [pltpu.VMEM((F, D), BF16), pltpu.VMEM((F, D), BF16), pltpu.VMEM((F, D), BF16),
         pltpu.SemaphoreType.DMA((3,))])
    return _pcall(
        body, name=name, grid=(ni,), compiler_params=_params(1), **io,
    )(dy, x, g, gate, up, wgt, wut, wd, *(plan.arrays if plan else ()))


def _matmul_tn(a, b, row_split, name, plan=None):
    T, n1 = a.shape
    n2 = b.shape[1]
    tn = n1 // row_split
    tk = min(T, DW_TOKEN_TILE)
    nk = T // tk

    def body(*refs):
        (a_ref, b_ref), (o_ref, ob_ref), _, cr = _unpack(refs, 2, 2, plan)
        j = pl.program_id(0)
        k = pl.program_id(1)
        _hook(plan, cr, "start", jnp.logical_and(j == 0, k == 0))

        @pl.when(k == 0)
        def _():
            o_ref[...] = jnp.zeros_like(o_ref)

        o_ref[...] += _dot_tn(a_ref[...], b_ref[...])

        @pl.when(k == nk - 1)
        def _():
            ob_ref[...] = o_ref[...].astype(BF16)

        _hook(plan, cr, "finish", jnp.logical_and(j == row_split - 1, k == nk - 1))

    io = _carried(
        plan,
        [pl.BlockSpec((tk, tn), lambda j, k: (k, j)), pl.BlockSpec((tk, n2), lambda j, k: (k, 0))],
        [pl.BlockSpec((tn, n2), lambda j, k: (j, 0)), pl.BlockSpec((tn, n2), lambda j, k: (j, 0))],
        [jax.ShapeDtypeStruct((n1, n2), F32), jax.ShapeDtypeStruct((n1, n2), BF16)], [])
    return _pcall(
        body, name=name, grid=(row_split, nk), compiler_params=_params(2), **io,
    )(a, b, *(plan.arrays if plan else ()))


def _norm_matmul(x, g, wt, name):
    T, D = x.shape
    n = wt.shape[0]
    tm = min(T, TOKEN_TILE)

    def body(x_ref, g_ref, w_ref, z_ref, h_ref):
        xv = x_ref[...]
        h = ((xv * _rms_inv(xv)) * g_ref[...]).astype(BF16)
        h_ref[...] = h
        z_ref[...] = _dot_nt(h, w_ref[...])

    return _pcall(
        body, name=name, grid=(T // tm,),
        in_specs=[pl.BlockSpec((tm, D), lambda i: (i, 0)), pl.BlockSpec((1, D), lambda i: (0, 0)),
                  pl.BlockSpec((n, D), lambda i: (0, 0))],
        out_specs=[pl.BlockSpec((tm, n), lambda i: (i, 0)), pl.BlockSpec((tm, D), lambda i: (i, 0))],
        out_shape=[jax.ShapeDtypeStruct((T, n), F32), jax.ShapeDtypeStruct((T, D), BF16)],
        compiler_params=_params(1),
    )(x, g, wt)


def _matmul_residual(y, w, x, name):
    T, D = x.shape
    kdim = y.shape[1]
    tm = min(T, TOKEN_TILE)

    def body(y_ref, w_ref, x_ref, o_ref):
        o_ref[...] = x_ref[...] + _dot(y_ref[...], w_ref[...])

    return _pcall(
        body, name=name, grid=(T // tm,),
        in_specs=[pl.BlockSpec((tm, kdim), lambda i: (i, 0)), pl.BlockSpec((kdim, D), lambda i: (0, 0)),
                  pl.BlockSpec((tm, D), lambda i: (i, 0))],
        out_specs=pl.BlockSpec((tm, D), lambda i: (i, 0)),
        out_shape=jax.ShapeDtypeStruct((T, D), F32),
        compiler_params=_params(1),
    )(y, w, x)


def _matmul_nt(dx, w, name, plan=None):
    T, D = dx.shape
    kdim = w.shape[0]
    tm = min(T, TOKEN_TILE)
    ni = T // tm

    def body(*refs):
        (dx_ref, w_ref), (dy_ref, dxb_ref), _, cr = _unpack(refs, 2, 2, plan)
        i = pl.program_id(0)
        _hook(plan, cr, "start", i == 0)
        dxb = dx_ref[...].astype(BF16)
        dxb_ref[...] = dxb
        dy_ref[...] = _dot_nt(dxb, w_ref[...])
        _hook(plan, cr, "finish", i == ni - 1)

    io = _carried(
        plan,
        [pl.BlockSpec((tm, D), lambda i: (i, 0)), pl.BlockSpec((kdim, D), lambda i: (0, 0))],
        [pl.BlockSpec((tm, kdim), lambda i: (i, 0)), pl.BlockSpec((tm, D), lambda i: (i, 0))],
        [jax.ShapeDtypeStruct((T, kdim), F32), jax.ShapeDtypeStruct((T, D), BF16)], [])
    return _pcall(
        body, name=name, grid=(ni,), compiler_params=_params(1), **io,
    )(dx, w, *(plan.arrays if plan else ()))


def _matmul_norm_bwd(dz, wt, x, g, dres, name, plan=None):
    T, D = x.shape
    n = dz.shape[1]
    tm = min(T, TOKEN_TILE)
    ni = T // tm

    def body(*refs):
        (dz_ref, w_ref, x_ref, g_ref, dres_ref), (dx_ref, dg_ref), _, cr = _unpack(refs, 5, 2, plan)
        i = pl.program_id(0)
        _hook(plan, cr, "start", i == 0)

        @pl.when(i == 0)
        def _():
            dg_ref[...] = jnp.zeros_like(dg_ref)

        dh = _dot(dz_ref[...], w_ref[...])
        dxn, dg = _norm_bwd(dh, x_ref[...], g_ref[...])
        dx_ref[...] = dres_ref[...] + dxn
        dg_ref[...] += dg
        _hook(plan, cr, "finish", i == ni - 1)

    io = _carried(
        plan,
        [pl.BlockSpec((tm, n), lambda i: (i, 0)), pl.BlockSpec((n, D), lambda i: (0, 0)),
         pl.BlockSpec((tm, D), lambda i: (i, 0)), pl.BlockSpec((1, D), lambda i: (0, 0)),
         pl.BlockSpec((tm, D), lambda i: (i, 0))],
        [pl.BlockSpec((tm, D), lambda i: (i, 0)), pl.BlockSpec((1, D), lambda i: (0, 0))],
        [jax.ShapeDtypeStruct((T, D), F32), jax.ShapeDtypeStruct((1, D), F32)], [])
    return _pcall(
        body, name=name, grid=(ni,), compiler_params=_params(1), **io,
    )(dz, wt, x, g, dres, *(plan.arrays if plan else ()))


def _loss_head(x, g, target, name):
    T, D = x.shape
    tm = min(T, TOKEN_TILE)

    def body(x_ref, g_ref, t_ref, dx_ref, dg_ref, loss_ref):
        @pl.when(pl.program_id(0) == 0)
        def _():
            dg_ref[...] = jnp.zeros_like(dg_ref)
            loss_ref[...] = jnp.zeros_like(loss_ref)

        xv = x_ref[...]
        gv = g_ref[...]
        out = (xv * _rms_inv(xv)) * gv
        diff = out - t_ref[...]
        loss_ref[...] += 0.5 * jnp.sum(jnp.mean(diff * diff, axis=-1, keepdims=True))
        dxn, dg = _norm_bwd(diff * (1.0 / D), xv, gv)
        dx_ref[...] = dxn
        dg_ref[...] += dg

    return _pcall(
        body, name=name, grid=(T // tm,),
        in_specs=[pl.BlockSpec((tm, D), lambda i: (i, 0)), pl.BlockSpec((1, D), lambda i: (0, 0)),
                  pl.BlockSpec((tm, D), lambda i: (i, 0))],
        out_specs=[pl.BlockSpec((tm, D), lambda i: (i, 0)), pl.BlockSpec((1, D), lambda i: (0, 0)),
                   pl.BlockSpec((1, LANES), lambda i: (0, 0))],
        out_shape=[jax.ShapeDtypeStruct((T, D), F32), jax.ShapeDtypeStruct((1, D), F32),
                   jax.ShapeDtypeStruct((1, LANES), F32)],
        compiler_params=_params(1),
    )(x, g, target)


Z_Q = 3 * CONV_WIDTH
Z_K = Z_Q + N_Q_HEADS * HEAD_DIM
Z_V = Z_K + LANES
Z_END = Z_V + LANES


def _rope_tables(T):
    half = ROT_DIM // 2
    inv_freq = ROPE_THETA ** (-jnp.arange(0, ROT_DIM, 2, dtype=F32) / ROT_DIM)
    ang = inv_freq[:, None] * jnp.arange(T, dtype=F32)[None, :]
    cos_sin = jnp.concatenate([jnp.cos(ang), jnp.sin(ang)], axis=0)
    select = np.zeros((2 * half, 3 * LANES), np.float32)
    const = np.zeros((1, 3 * LANES), np.float32)
    for lane in range(LANES):
        d = lane % HEAD_DIM
        if d < half:
            select[d, lane] = 1.0
            select[half + d, LANES + lane] = -1.0
        elif d < ROT_DIM:
            select[d - half, lane] = 1.0
            select[d, 2 * LANES + lane] = 1.0
        else:
            const[0, lane] = 1.0
    tab = lax.dot_general(cos_sin, jnp.asarray(select), (((0,), (0,)), ((), ())),
                          precision=lax.Precision.HIGHEST, preferred_element_type=F32)
    return tab + jnp.asarray(const)


def _tab3(tab):
    return tab[:, 0:LANES], tab[:, LANES:2 * LANES], tab[:, 2 * LANES:3 * LANES]


def _rot(x, tab):
    c, s1, s2 = _tab3(tab)
    return x * c + pltpu.roll(x, LANES - ROT_DIM // 2, 1) * s1 + pltpu.roll(x, ROT_DIM // 2, 1) * s2


def _rot_t(d, tab):
    c, s1, s2 = _tab3(tab)
    return d * c + pltpu.roll(d * s1, ROT_DIM // 2, 1) + pltpu.roll(d * s2, LANES - ROT_DIM // 2, 1)


def _head_pads(a):
    lo = lax.broadcasted_iota(jnp.int32, a.shape, 1) < HEAD_DIM
    nat0 = jnp.where(lo, a, 0.0)
    nat1 = jnp.where(lo, 0.0, a)
    return {
        (0, 0): nat0.astype(BF16), (0, 1): pltpu.roll(nat0, HEAD_DIM, 1).astype(BF16),
        (1, 0): pltpu.roll(nat1, HEAD_DIM, 1).astype(BF16), (1, 1): nat1.astype(BF16),
    }


def _from_pads(even, odd, kv):
    lo = lax.broadcasted_iota(jnp.int32, even.shape, 1) < HEAD_DIM
    if kv == 0:
        return jnp.where(lo, even + pltpu.roll(odd, HEAD_DIM, 1), 0.0)
    return jnp.where(lo, 0.0, pltpu.roll(even, HEAD_DIM, 1) + odd)


N_GROUPS = 4


def _group_head(g, r):
    kv, par = divmod(g, 2)
    return 2 * (2 * kv + r) + par


def _window_mask_t(has_prev):
    jj = lax.broadcasted_iota(jnp.int32, (2 * BLOCK, 2 * BLOCK), 0)
    ii = lax.broadcasted_iota(jnp.int32, (2 * BLOCK, 2 * BLOCK), 1) & (BLOCK - 1)
    rel = jj - BLOCK - ii
    return (rel <= 0) & (rel > -BLOCK) & ((jj >= BLOCK) | has_prev)


def _sink_row(sink_ref, g):
    lane = lax.broadcasted_iota(jnp.int32, (1, 2 * BLOCK), 1)
    return jnp.where(lane < BLOCK, sink_ref[0, _group_head(g, 0)], sink_ref[0, _group_head(g, 1)])


def _attn_probs_t(q2, kp, mask, sink_ref):
    out = []
    for kv in range(2):
        q_st = jnp.concatenate([q2[2 * kv], q2[2 * kv + 1]], axis=0)
        for par in range(2):
            s = jnp.where(mask, _dot_nt(kp[(kv, par)], q_st) * ATTN_SCALE, MASK_VALUE)
            sink = _sink_row(sink_ref, 2 * kv + par)
            m = jnp.maximum(jnp.max(s, axis=0, keepdims=True), sink)
            p = jnp.exp(s - m)
            esink = jnp.exp(sink - m)
            rden = 1.0 / (jnp.sum(p, axis=0, keepdims=True) + esink)
            out.append((p * rden, esink * rden))
    return out


def _conv_taps(cg, u, cg_prev, u_prev, has_prev):
    vv = cg * u
    halo = jnp.where(has_prev, cg_prev * u_prev, 0.0)
    ext = jnp.concatenate([halo, vv], axis=0)
    rows = ext.shape[0]
    vv1 = pltpu.roll(ext, 1, 0)[8:rows]
    vv2 = pltpu.roll(ext, 2, 0)[8:rows]
    return vv, vv1, vv2


def _mix_specs(nb):
    cur = lambda n: jnp.minimum(n, nb - 1)
    prev = lambda n: jnp.maximum(jnp.minimum(n, nb - 1) - 1, 0)
    rows8_prev = lambda n: jnp.maximum(16 * jnp.minimum(n, nb - 1) - 1, 0)
    return cur, prev, [
        pl.BlockSpec((BLOCK, Z_END), lambda n: (cur(n), 0)),
        pl.BlockSpec((BLOCK, 2 * LANES), lambda n: (prev(n), Z_K // (2 * LANES))),
        pl.BlockSpec((8, CONV_WIDTH), lambda n: (rows8_prev(n), 1)),
        pl.BlockSpec((8, CONV_WIDTH), lambda n: (rows8_prev(n), 2)),
        pl.BlockSpec((BLOCK, 3 * LANES), lambda n: (cur(n), 0)),
        pl.BlockSpec((BLOCK, 3 * LANES), lambda n: (prev(n), 0)),
        pl.BlockSpec((3, CONV_WIDTH), lambda n: (0, 0)),
        pl.BlockSpec(memory_space=pltpu.SMEM),
    ]


def _mix_core_fwd(z, tab, conv_w, sinks, name):
    T = z.shape[0]
    nb = T // BLOCK
    _, _, specs = _mix_specs(nb)

    def body(z_ref, zkvp_ref, cgp_ref, up_ref, tab_ref, tabp_ref, cw_ref, sink_ref, y_ref):
        has_prev = pl.program_id(0) > 0
        bg = z_ref[:, 0:CONV_WIDTH]
        vv, vv1, vv2 = _conv_taps(z_ref[:, CONV_WIDTH:2 * CONV_WIDTH], z_ref[:, 2 * CONV_WIDTH:Z_Q],
                                  cgp_ref[...], up_ref[...], has_prev)
        conv = cw_ref[0:1, :] * vv2 + cw_ref[1:2, :] * vv1 + cw_ref[2:3, :] * vv
        y_ref[:, 0:CONV_WIDTH] = (bg * conv).astype(BF16)

        tab_c = tab_ref[...]
        tab_p = tabp_ref[...]
        k_all = jnp.concatenate([_rot(zkvp_ref[:, 0:LANES], tab_p), _rot(z_ref[:, Z_K:Z_V], tab_c)], axis=0)
        v_all = jnp.concatenate([zkvp_ref[:, LANES:2 * LANES], z_ref[:, Z_V:Z_END]], axis=0)
        kp = _head_pads(k_all)
        vp = _head_pads(v_all)
        q2 = [_rot(z_ref[:, Z_Q + LANES * c:Z_Q + LANES * (c + 1)], tab_c).astype(BF16) for c in range(N_Q_HEADS // 2)]
        probs = _attn_probs_t(q2, kp, _window_mask_t(has_prev), sink_ref)
        for kv in range(2):
            o_t = (_dot_tn(vp[(kv, 0)], probs[2 * kv][0].astype(BF16))
                   + _dot_tn(vp[(kv, 1)], probs[2 * kv + 1][0].astype(BF16)))
            for r in range(2):
                c = 2 * kv + r
                y_ref[:, CONV_WIDTH + LANES * c:CONV_WIDTH + LANES * (c + 1)] = o_t[:, BLOCK * r:BLOCK * (r + 1)].T.astype(BF16)

    return _pcall(
        body, name=name, grid=(nb,), in_specs=specs,
        out_specs=pl.BlockSpec((BLOCK, 2 * CONV_WIDTH), lambda n: (n, 0)),
        out_shape=jax.ShapeDtypeStruct((T, 2 * CONV_WIDTH), BF16),
        compiler_params=_params(1),
    )(z, z, z, z, tab, tab, conv_w, sinks)


def _mix_core_bwd(z, dy, tab, conv_w, sinks, name, plan=None):
    T = z.shape[0]
    nb = T // BLOCK
    cur, _, specs = _mix_specs(nb)
    rows8_next = lambda n: jnp.minimum(16 * (cur(n) + 1), 16 * nb - 1)
    specs = specs[:4] + [
        pl.BlockSpec((8, CONV_WIDTH), lambda n: (rows8_next(n), 0)),
        pl.BlockSpec((BLOCK, 2 * CONV_WIDTH), lambda n: (cur(n), 0)),
        pl.BlockSpec((8, CONV_WIDTH), lambda n: (rows8_next(n), 0)),
    ] + specs[4:]

    def body(*refs):
        ins, outs, scratch, cr = _unpack(refs, 11, 3, plan)
        z_ref, zkvp_ref, cgp_ref, up_ref, bgn_ref, dy_ref, dyn_ref, tab_ref, tabp_ref, cw_ref, sink_ref = ins
        dz_ref, dcw_ref, dsk_ref = outs
        main_ref, kv_ref = scratch
        n = pl.program_id(0)
        _hook(plan, cr, "start", n == 0)

        @pl.when(n == 0)
        def _():
            main_ref[...] = jnp.zeros_like(main_ref)
            kv_ref[...] = jnp.zeros_like(kv_ref)
            dcw_ref[...] = jnp.zeros_like(dcw_ref)
            dsk_ref[...] = jnp.zeros_like(dsk_ref)

        @pl.when(n < nb)
        def _():
            has_prev = n > 0
            has_next = n < nb - 1
            bg = z_ref[:, 0:CONV_WIDTH]
            cg = z_ref[:, CONV_WIDTH:2 * CONV_WIDTH]
            u = z_ref[:, 2 * CONV_WIDTH:Z_Q]
            vv, vv1, vv2 = _conv_taps(cg, u, cgp_ref[...], up_ref[...], has_prev)
            w0, w1, w2 = cw_ref[0:1, :], cw_ref[1:2, :], cw_ref[2:3, :]
            dyc = dy_ref[:, 0:CONV_WIDTH]
            dbg = dyc * (w0 * vv2 + w1 * vv1 + w2 * vv)
            dconv = dyc * bg
            dconv_next = jnp.where(has_next, dyn_ref[...] * bgn_ref[...], 0.0)
            ext = jnp.concatenate([dconv, dconv_next], axis=0)
            rows = ext.shape[0]
            dvv = w2 * dconv + w1 * pltpu.roll(ext, rows - 1, 0)[0:BLOCK] + w0 * pltpu.roll(ext, rows - 2, 0)[0:BLOCK]
            dcw_ref[0:1, :] += jnp.sum(dconv * vv2, axis=0, keepdims=True)
            dcw_ref[1:2, :] += jnp.sum(dconv * vv1, axis=0, keepdims=True)
            dcw_ref[2:3, :] += jnp.sum(dconv * vv, axis=0, keepdims=True)

            tab_c = tab_ref[...]
            tab_p = tabp_ref[...]
            k_all = jnp.concatenate([_rot(zkvp_ref[:, 0:LANES], tab_p), _rot(z_ref[:, Z_K:Z_V], tab_c)], axis=0)
            v_all = jnp.concatenate([zkvp_ref[:, LANES:2 * LANES], z_ref[:, Z_V:Z_END]], axis=0)
            kp = _head_pads(k_all)
            vp = _head_pads(v_all)
            chunks = range(N_Q_HEADS // 2)
            q2 = [_rot(z_ref[:, Z_Q + LANES * c:Z_Q + LANES * (c + 1)], tab_c).astype(BF16) for c in chunks]
            do2 = [dy_ref[:, CONV_WIDTH + LANES * c:CONV_WIDTH + LANES * (c + 1)].astype(BF16) for c in chunks]
            probs = _attn_probs_t(q2, kp, _window_mask_t(has_prev), sink_ref)
            dq_chunks = []
            dk_nat = jnp.zeros((2 * BLOCK, LANES), F32)
            dv_nat = jnp.zeros((2 * BLOCK, LANES), F32)
            for kv in range(2):
                q_st = jnp.concatenate([q2[2 * kv], q2[2 * kv + 1]], axis=0)
                do_st = jnp.concatenate([do2[2 * kv], do2[2 * kv + 1]], axis=0)
                dq_t = jnp.zeros((LANES, 2 * BLOCK), F32)
                dk_par, dv_par = [], []
                for par in range(2):
                    g = 2 * kv + par
                    pr, psink = probs[g]
                    dp = _dot_nt(vp[(kv, par)], do_st)
                    delta = jnp.sum(dp * pr, axis=0, keepdims=True)
                    ds = (pr * (dp - delta) * ATTN_SCALE).astype(BF16)
                    dsink = -psink * delta
                    for r in range(2):
                        h = _group_head(g, r)
                        dsk_ref[h:h + 1, :] += jnp.sum(dsink[:, BLOCK * r:BLOCK * (r + 1)])
                    dq_t = dq_t + _dot_tn(kp[(kv, par)], ds)
                    dk_par.append(_dot(ds, q_st))
                    dv_par.append(_dot(pr.astype(BF16), do_st))
                for r in range(2):
                    dq_chunks.append(_rot_t(dq_t[:, BLOCK * r:BLOCK * (r + 1)].T, tab_c))
                dk_nat = dk_nat + _from_pads(dk_par[0], dk_par[1], kv)
                dv_nat = dv_nat + _from_pads(dv_par[0], dv_par[1], kv)

            dk_prev = _rot_t(kv_ref[:, 0:LANES] + dk_nat[0:BLOCK], tab_p)
            dv_prev = kv_ref[:, LANES:2 * LANES] + dv_nat[0:BLOCK]
            dz_ref[:, 0:Z_K] = main_ref[...]
            dz_ref[:, Z_K:Z_V] = dk_prev.astype(BF16)
            dz_ref[:, Z_V:Z_END] = dv_prev.astype(BF16)
            main_ref[:, 0:CONV_WIDTH] = dbg.astype(BF16)
            main_ref[:, CONV_WIDTH:2 * CONV_WIDTH] = (dvv * u).astype(BF16)
            main_ref[:, 2 * CONV_WIDTH:Z_Q] = (dvv * cg).astype(BF16)
            for c in range(N_Q_HEADS // 2):
                main_ref[:, Z_Q + LANES * c:Z_Q + LANES * (c + 1)] = dq_chunks[c].astype(BF16)
            kv_ref[:, 0:LANES] = dk_nat[BLOCK:2 * BLOCK]
            kv_ref[:, LANES:2 * LANES] = dv_nat[BLOCK:2 * BLOCK]

        @pl.when(n == nb)
        def _():
            dz_ref[:, 0:Z_K] = main_ref[...]
            dz_ref[:, Z_K:Z_V] = _rot_t(kv_ref[:, 0:LANES], tab_ref[...]).astype(BF16)
            dz_ref[:, Z_V:Z_END] = kv_ref[:, LANES:2 * LANES].astype(BF16)

        _hook(plan, cr, "finish", n == nb)

    io = _carried(
        plan, specs,
        [pl.BlockSpec((BLOCK, Z_END), lambda n: (jnp.maximum(n - 1, 0), 0)),
         pl.BlockSpec((8, CONV_WIDTH), lambda n: (0, 0)), pl.BlockSpec((8, LANES), lambda n: (0, 0))],
        [jax.ShapeDtypeStruct((T, Z_END), BF16), jax.ShapeDtypeStruct((8, CONV_WIDTH), F32),
         jax.ShapeDtypeStruct((8, LANES), F32)],
        [pltpu.VMEM((BLOCK, Z_K), BF16), pltpu.VMEM((BLOCK, 2 * LANES), F32)])
    return _pcall(
        body, name=name, grid=(nb + 1,), compiler_params=_params(1), **io,
    )(z, z, z, z, z, dy, dy, tab, tab, conv_w, sinks, *(plan.arrays if plan else ()))


ROW_SPLIT = 2


def _pair_sum(grads, recvd, core, name):
    n = len(grads)

    def body(core_ref, *refs):
        g, r = refs[:n], refs[n:2 * n]
        s, sb = refs[2 * n:3 * n], refs[3 * n:]
        for t in range(n):
            tot = g[t][...] + r[t][...].astype(F32)
            s[t][...] = tot
            sb[t][...] = tot.astype(BF16)

    def blk(a):
        return (1, a.shape[1] // ROW_SPLIT, a.shape[2])

    in_specs = [pl.BlockSpec(blk(r), lambda q, i, core_ref: (q, core_ref[0] * ROW_SPLIT + i, 0)) for r in recvd]
    in_specs += [pl.BlockSpec(blk(r), lambda q, i, core_ref: (q, i, 0)) for r in recvd]
    out_specs = [pl.BlockSpec(blk(r), lambda q, i, core_ref: (q, i, 0)) for r in recvd] * 2
    return _pcall(
        body, name=name,
        grid_spec=pltpu.PrefetchScalarGridSpec(num_scalar_prefetch=1, grid=(N_CHIPS, ROW_SPLIT),
                                               in_specs=in_specs, out_specs=out_specs),
        out_shape=[jax.ShapeDtypeStruct(r.shape, F32) for r in recvd] + [jax.ShapeDtypeStruct(r.shape, BF16) for r in recvd],
        compiler_params=_params(2),
    )(core, *grads, *recvd)


def _chip_sum(parts, recvd, place, name):
    n = len(parts)

    def body(place_ref, *refs):
        p, r, o = refs[:n], refs[n:2 * n], refs[2 * n:]
        for t in range(n):
            tot = p[t][0]
            for j in range(3):
                tot = tot + r[t][j].astype(F32)
            o[t][...] = tot

    in_specs = [pl.BlockSpec((1, p.shape[1] // ROW_SPLIT, p.shape[2]), lambda i, place_ref: (place_ref[0], i, 0))
                for p in parts]
    in_specs += [pl.BlockSpec((3, r.shape[1] // ROW_SPLIT, r.shape[2]), lambda i, place_ref: (0, i, 0)) for r in recvd]
    out_specs = [pl.BlockSpec((p.shape[1] // ROW_SPLIT, p.shape[2]),
                              lambda i, place_ref: (place_ref[1] * ROW_SPLIT + i, 0)) for p in parts]
    return _pcall(
        body, name=name,
        grid_spec=pltpu.PrefetchScalarGridSpec(num_scalar_prefetch=1, grid=(ROW_SPLIT,),
                                               in_specs=in_specs, out_specs=out_specs),
        out_shape=[jax.ShapeDtypeStruct((2 * p.shape[1], p.shape[2]), F32) for p in parts],
        compiler_params=_params(1),
    )(place, *parts, *recvd)


def _adamw_math(w, g, m, v):
    m = ADAM_B1 * m + (1.0 - ADAM_B1) * g
    v = ADAM_B2 * v + (1.0 - ADAM_B2) * (g * g)
    m_hat = m / (1.0 - ADAM_B1 ** ADAM_STEP)
    v_hat = v / (1.0 - ADAM_B2 ** ADAM_STEP)
    delta = -ADAM_LR * (m_hat / (jnp.sqrt(v_hat) + ADAM_EPS) + ADAM_WD * w)
    return delta, m, v


def _adamw(ws, gs, ms, vs, row_blocks, name):
    n = len(ws)

    def body(*refs):
        w, g, m, v = refs[:n], refs[n:2 * n], refs[2 * n:3 * n], refs[3 * n:4 * n]
        d, mo, vo, go = refs[4 * n:5 * n], refs[5 * n:6 * n], refs[6 * n:7 * n], refs[7 * n:]
        for t in range(n):
            gv = g[t][...]
            delta, m_new, v_new = _adamw_math(w[t][...], gv, m[t][...], v[t][...])
            d[t][...] = delta
            mo[t][...] = m_new
            vo[t][...] = v_new
            go[t][...] = gv

    specs = [pl.BlockSpec((a.shape[0] // row_blocks, a.shape[1]), lambda i: (i, 0)) for a in ws]
    shapes = [jax.ShapeDtypeStruct(a.shape, F32) for a in ws]
    return _pcall(
        body, name=name, grid=(row_blocks,), in_specs=specs * 4, out_specs=specs * 4, out_shape=shapes * 4,
        compiler_params=_params(1),
    )(*ws, *gs, *ms, *vs)


def kernel(x, ffn1_norm, ffn1_w_gate, ffn1_w_up, ffn1_w_down, mix_norm, w_in, conv_w, attn_sinks, w_out, ffn2_norm, ffn2_w_gate, ffn2_w_up, ffn2_w_down, final_norm, loss_target, m_ffn1_norm, m_ffn1_w_gate, m_ffn1_w_up, m_ffn1_w_down, m_mix_norm, m_w_in, m_conv_w, m_attn_sinks, m_w_out, m_ffn2_norm, m_ffn2_w_gate, m_ffn2_w_up, m_ffn2_w_down, m_final_norm, v_ffn1_norm, v_ffn1_w_gate, v_ffn1_w_up, v_ffn1_w_down, v_mix_norm, v_w_in, v_conv_w, v_attn_sinks, v_w_out, v_ffn2_norm, v_ffn2_w_gate, v_ffn2_w_up, v_ffn2_w_down, v_final_norm):
    T, D = x.shape[1], x.shape[2]
    chip = (2 * lax.axis_index("x") + lax.axis_index("y")).astype(jnp.int32)
    core = lax.axis_index("c").astype(jnp.int32)
    place = jnp.stack([chip, core])
    x0 = x[0]
    target = loss_target[0]
    gf = final_norm.reshape(1, D)

    tr = lambda w: jnp.swapaxes(w[0], 0, 1)
    big = [tr(ffn1_w_gate), tr(ffn1_w_up), ffn1_w_down[0], tr(w_in), w_out[0], tr(ffn2_w_gate), tr(ffn2_w_up), ffn2_w_down[0]]
    transposed = [True, True, False, True, False, True, True, False]
    own_b = [w.astype(BF16) for w in big]

    def whole(gathered, own):
        return lax.dynamic_update_slice(gathered, own[None], (chip, 0, 0)).reshape(-1, D)

    got1 = _run_comm(_gather_plan(own_b[0:3]), "gather_ffn1")
    wg1, wu1, wd1 = (whole(g, o) for g, o in zip(got1, own_b[0:3]))
    tab = _rope_tables(T)

    res = _ffn_fwd(x0, ffn1_norm, wg1, wu1, wd1, "ffn1_fwd", _gather_plan(own_b[3:8], [conv_w[0]]))
    x1, h1, gate1, up1, act1 = res[:5]
    win, wout, wg2, wu2, wd2 = (whole(g, o) for g, o in zip(res[5:10], own_b[3:8]))
    convw4 = lax.dynamic_update_slice(res[10], conv_w, (chip, 0, 0))
    convw = jnp.transpose(convw4, (1, 0, 2)).reshape(3, -1)
    z, hm = _norm_matmul(x1, mix_norm, win, "mix_in_fwd")
    ymix = _mix_core_fwd(z, tab, convw, attn_sinks, "mix_core_fwd")
    x2 = _matmul_residual(ymix, wout, x1, "mix_out_fwd")
    x3, h2, gate2, up2, act2 = _ffn_fwd(x2, ffn2_norm, wg2, wu2, wd2, "ffn2_fwd")
    dx3, dgf, loss_part = _loss_head(x3, gf, target, "loss_head")

    dx2, dyb2, dgate2, dup2, dg2 = _ffn_bwd(dx3, x2, ffn2_norm, gate2, up2, wg2, wu2, wd2, "ffn2_bwd")
    dymix, dx2b = _matmul_nt(dx2, wout, "mix_out_bwd")
    dz, dcw, dsk = _mix_core_bwd(z, dymix, tab, convw, attn_sinks, "mix_core_bwd")
    dx1, dgm = _matmul_norm_bwd(dz, win, x1, mix_norm, dx2, "mix_in_bwd")
    dx0, dyb1, dgate1, dup1, dg1 = _ffn_bwd(dx1, x0, ffn1_norm, gate1, up1, wg1, wu1, wd1, "ffn1_bwd")

    pad = lambda a: jnp.pad(a, ((0, 0), (0, LANES - a.shape[1])))
    vec = jnp.concatenate([dg1, dgm, dg2, dgf, dcw[0:3].reshape(1, -1), pad(dsk[:, 0].reshape(1, -1)),
                           pad(loss_part[:, 0:1])], axis=1)

    jobs = [("ffn2_dwg", dgate2, h2, 5), ("ffn2_dwu", dup2, h2, 6), ("ffn2_dwd", act2, dyb2, 7),
            ("ffn1_dwg", dgate1, h1, 0), ("ffn1_dwu", dup1, h1, 1), ("ffn1_dwd", act1, dyb1, 2),
            ("mix_dwin", dz, hm, 3), ("mix_dwout", ymix, dx2b, 4)]
    n_jobs = len(jobs)
    grad, grad_b, from_sib, pair_f, pair_b, from_chips, half, g_big = ({} for _ in range(8))

    def stage_plans(t):
        plans, takers = [], []
        if 0 <= t - 1 < n_jobs:
            plans.append(_sibling_plan([grad_b[t - 1]]))
            takers.append((from_sib, t - 1))
        if 0 <= t - 2 < n_jobs:
            plans.append(_scatter_plan([pair_b[t - 2]]))
            takers.append((from_chips, t - 2))
        if 0 <= t - 3 < n_jobs:
            plans.append(_join_plan([half[t - 3]]))
            takers.append((g_big, jobs[t - 3][3]))
        return plans, takers

    def after_stage(t, landed, takers):
        for (store, key), arr in zip(takers, landed):
            store[key] = arr
        if 0 <= t - 1 < n_jobs:
            pair_f[t - 1], pair_b[t - 1] = _pair_sum([grad[t - 1]], [from_sib[t - 1]], core.reshape(1), f"pair_sum_{t - 1}")
        if 0 <= t - 2 < n_jobs:
            half[t - 2], = _chip_sum([pair_f[t - 2]], [from_chips[t - 2]], place, f"chip_sum_{t - 2}")

    for t, (name_, a, b, _) in enumerate(jobs):
        plans, takers = stage_plans(t)
        if t == 0:
            plans.append(_all_gather_plan(jnp.pad(vec, ((0, 7), (0, 0)))))
        res = _matmul_tn(a, b, DW_ROW_SPLIT, name_, _merge_plans(plans))
        grad[t], grad_b[t] = (r.reshape(N_CHIPS, -1, D) for r in res[:2])
        landed = list(res[2:])
        if t == 0:
            vec_blocks = landed.pop()
        after_stage(t, landed, takers)

    ws = big
    ms = [tr(m_ffn1_w_gate), tr(m_ffn1_w_up), m_ffn1_w_down[0], tr(m_w_in), m_w_out[0], tr(m_ffn2_w_gate), tr(m_ffn2_w_up), m_ffn2_w_down[0]]
    vs = [tr(v_ffn1_w_gate), tr(v_ffn1_w_up), v_ffn1_w_down[0], tr(v_w_in), v_w_out[0], tr(v_ffn2_w_gate), tr(v_ffn2_w_up), v_ffn2_w_down[0]]
    for t in range(n_jobs, n_jobs + 3):
        plans, takers = stage_plans(t)
        after_stage(t, _run_comm(_merge_plans(plans), f"grads_tail_{t - n_jobs}"), takers)
    upd = {}
    for name_, idx in (("adamw_a", [0, 1, 2, 4]), ("adamw_b", [3, 5, 6, 7])):
        k = len(idx)
        res = _adamw([ws[i] for i in idx], [g_big[i] for i in idx], [ms[i] for i in idx], [vs[i] for i in idx], 8, name_)
        for j, i in enumerate(idx):
            upd[i] = (res[j], res[k + j], res[2 * k + j])
            g_big[i] = res[3 * k + j]

    total = _sum_devices(vec_blocks, "small_sum")[0:1]
    g_n1, g_nm, g_n2, g_nf = (total[:, k * D:(k + 1) * D] for k in range(4))
    cw_full = total[:, 4 * D:4 * D + 3 * CONV_WIDTH].reshape(3, CONV_WIDTH)
    cq = CONV_WIDTH // N_CHIPS
    g_cw = lax.dynamic_slice(cw_full, (0, chip * cq), (3, cq))
    off = 4 * D + 3 * CONV_WIDTH
    g_sk = total[:, off:off + N_Q_HEADS]
    loss = total[0, off + LANES]

    sw = [ffn1_norm, mix_norm, conv_w[0], attn_sinks, ffn2_norm, gf]
    sg = [g_n1, g_nm, g_cw, g_sk, g_n2, g_nf]
    sm = [m_ffn1_norm, m_mix_norm, m_conv_w[0], m_attn_sinks, m_ffn2_norm, m_final_norm.reshape(1, D)]
    sv = [v_ffn1_norm, v_mix_norm, v_conv_w[0], v_attn_sinks, v_ffn2_norm, v_final_norm.reshape(1, D)]
    sres = _adamw(sw, sg, sm, sv, 1, "adamw_small")
    supd = [(sres[j], sres[6 + j], sres[12 + j]) for j in range(6)]

    order = [("s", 0), ("b", 0), ("b", 1), ("b", 2), ("s", 1), ("b", 3), ("s", 2), ("s", 3), ("b", 4),
             ("s", 4), ("b", 5), ("b", 6), ("b", 7), ("s", 5)]

    def leaf(kind, i, which):
        if kind == "b":
            a = g_big[i] if which == 0 else upd[i][which - 1]
            return (jnp.swapaxes(a, 0, 1) if transposed[i] else a)[None]
        a = sg[i] if which == 0 else supd[i][which - 1]
        if i == 2:
            return a[None]
        if i == 5:
            return a.reshape(D)
        return a

    outs = [loss, dx0[None]]
    for which in range(4):
        outs += [leaf(kind, i, which) for kind, i in order]
    return tuple(outs)
```

```python
import functools

import jax
import jax.numpy as jnp
import numpy as np
from jax import lax
from jax.experimental import pallas as pl
from jax.experimental.pallas import tpu as pltpu

F32 = jnp.float32
BF16 = jnp.bfloat16
MESH = pl.DeviceIdType.MESH

CONV_WIDTH = 512
N_Q_HEADS = 8
HEAD_DIM = 64
BLOCK = 128
ROPE_THETA = 500000.0
ROT_DIM = 16
RMS_EPS = 1e-5
MASK_VALUE = -1e30
ATTN_SCALE = HEAD_DIM ** -0.5
FFN_RES_SCALE = 0.5
ADAM_LR = 0.001
ADAM_B1 = 0.9
ADAM_B2 = 0.999
ADAM_EPS = 1e-08
ADAM_WD = 0.01
ADAM_STEP = 10

N_CHIPS = 4
N_DEV = 8
LANES = 128
VMEM_LIMIT = 56 * 1024 * 1024

_pcall = pl.pallas_call
HBM_SPEC = pl.BlockSpec(memory_space=pltpu.HBM)
ANY_SPEC = pl.BlockSpec(memory_space=pl.ANY)


def _params(n_axes, vmem=VMEM_LIMIT):
    return pltpu.CompilerParams(dimension_semantics=("arbitrary",) * n_axes, vmem_limit_bytes=vmem)


def _dot(a, b):
    return jnp.dot(a, b, preferred_element_type=F32)


def _dot_nt(a, b):
    return lax.dot_general(a, b, (((1,), (1,)), ((), ())), preferred_element_type=F32)


def _dot_tn(a, b):
    return lax.dot_general(a, b, (((0,), (0,)), ((), ())), preferred_element_type=F32)


def _rms_inv(x):
    return lax.rsqrt(jnp.mean(x * x, axis=-1, keepdims=True) + RMS_EPS)


def _norm_bwd(dh, x, g):
    inv = _rms_inv(x)
    xhat = x * inv
    dg = jnp.sum(dh * xhat, axis=0, keepdims=True)
    dxhat = dh * g
    dx = inv * (dxhat - xhat * jnp.mean(dxhat * xhat, axis=-1, keepdims=True))
    return dx, dg


def _place():
    x, y, c = lax.axis_index("x"), lax.axis_index("y"), lax.axis_index("c")
    chips = [(1 - x, y), (x, 1 - y), (1 - x, 1 - y)]
    return x, y, c, chips


class _Plan:
    def __init__(self, arrays, out_shapes, n_sems, start, finish, middle=None, aliases=None):
        self.arrays, self.out_shapes, self.n_sems = list(arrays), list(out_shapes), n_sems
        self.start, self.finish, self.middle = start, finish, middle
        self.aliases = dict(aliases or {})

    def specs(self):
        k = len(self.arrays)
        sems = [pltpu.SemaphoreType.DMA((self.n_sems,)), pltpu.SemaphoreType.DMA((self.n_sems,))]
        return [HBM_SPEC] * k, [HBM_SPEC] * len(self.out_shapes), self.out_shapes, sems


class _SemSlice:
    def __init__(self, ref, offset):
        self.ref, self.offset = ref, offset

    @property
    def at(self):
        return self

    def __getitem__(self, k):
        return self.ref.at[k + self.offset]


def _merge_plans(plans):
    plans = [p for p in plans if p is not None]
    if len(plans) <= 1:
        return plans[0] if plans else None
    arrays, shapes, aliases, spans, n_sems = [], [], {}, [], 0
    for p in plans:
        a0, o0 = len(arrays), len(shapes)
        spans.append((a0, a0 + len(p.arrays), o0, o0 + len(p.out_shapes), n_sems))
        aliases.update({a0 + i: o0 + j for i, j in p.aliases.items()})
        arrays += p.arrays
        shapes += p.out_shapes
        n_sems += p.n_sems

    def run(which):
        def fn(ins, outs, send_sems, recv_sems):
            for p, (a0, a1, o0, o1, s0) in zip(plans, spans):
                part = getattr(p, which)
                if part is not None:
                    part(ins[a0:a1], outs[o0:o1], _SemSlice(send_sems, s0), _SemSlice(recv_sems, s0))
        return fn

    middle = run("middle") if any(p.middle is not None for p in plans) else None
    return _Plan(arrays, shapes, n_sems, run("start"), run("finish"), middle, aliases)


def _sibling_plan(grads_b):
    n = len(grads_b)

    def copies(ins, outs, send_sems, recv_sems):
        x, y, c, _ = _place()

        def copy(t):
            half = ins[t].shape[1] // 2
            return pltpu.make_async_remote_copy(
                src_ref=ins[t].at[:, pl.ds(pl.multiple_of((1 - c) * half, 16), half), :], dst_ref=outs[t],
                send_sem=send_sems.at[t], recv_sem=recv_sems.at[t], device_id=(x, y, 1 - c), device_id_type=MESH)

        return [copy(t) for t in range(n)]

    def start(*refs):
        for cp in copies(*refs):
            cp.start()

    def finish(*refs):
        for cp in copies(*refs):
            cp.wait()

    shapes = [jax.ShapeDtypeStruct((g.shape[0], g.shape[1] // 2, g.shape[2]), g.dtype) for g in grads_b]
    return _Plan(grads_b, shapes, n, start, finish)


def _scatter_plan(parts_b):
    n = len(parts_b)

    def copies(ins, outs, send_sems, recv_sems):
        x, y, c, chips = _place()

        def copy(t, j):
            px, py = chips[j]
            return pltpu.make_async_remote_copy(
                src_ref=ins[t].at[2 * px + py], dst_ref=outs[t].at[j], send_sem=send_sems.at[3 * t + j],
                recv_sem=recv_sems.at[3 * t + j], device_id=(px, py, c), device_id_type=MESH)

        return [copy(t, j) for t in range(n) for j in range(3)]

    def start(*refs):
        for cp in copies(*refs):
            cp.start()

    def finish(*refs):
        for cp in copies(*refs):
            cp.wait()

    shapes = [jax.ShapeDtypeStruct((3, *p.shape[1:]), p.dtype) for p in parts_b]
    return _Plan(parts_b, shapes, 3 * n, start, finish)


def _gather_plan(shards, small=()):
    n, ns = len(shards), len(small)

    def parts(ins, outs, send_sems, recv_sems):
        x, y, c, chips = _place()
        me = 2 * x + y

        def rows(t, core):
            half = ins[t].shape[0] // 2
            return pl.ds(pl.multiple_of(core * half, 16), half)

        def first(t, j, block, core):
            return pltpu.make_async_remote_copy(
                src_ref=ins[t].at[rows(t, core), :], dst_ref=outs[t].at[block, rows(t, core), :],
                send_sem=send_sems.at[6 * t + j], recv_sem=recv_sems.at[6 * t + j],
                device_id=(*chips[j], c), device_id_type=MESH)

        def passed(t, j, block, core):
            ref = outs[t].at[block, rows(t, core), :]
            return pltpu.make_async_remote_copy(
                src_ref=ref, dst_ref=ref, send_sem=send_sems.at[6 * t + 3 + j], recv_sem=recv_sems.at[6 * t + 3 + j],
                device_id=(x, y, 1 - c), device_id_type=MESH)

        def whole(s, j, block):
            k = 6 * n + 3 * s + j
            return pltpu.make_async_remote_copy(
                src_ref=ins[n + s], dst_ref=outs[n + s].at[block], send_sem=send_sems.at[k], recv_sem=recv_sems.at[k],
                device_id=(*chips[j], c), device_id_type=MESH)

        blocks = [2 * px + py for px, py in chips]
        return c, me, blocks, first, passed, whole

    def start(*refs):
        c, me, _, first, _, whole = parts(*refs)
        for t in range(n):
            for j in range(3):
                first(t, j, me, c).start()
        for s in range(ns):
            for j in range(3):
                whole(s, j, me).start()

    def middle(*refs):
        c, _, blocks, first, passed, _ = parts(*refs)
        for t in range(n):
            for j in range(3):
                first(t, j, blocks[j], c).wait_recv()
                passed(t, j, blocks[j], c).start()

    def finish(*refs):
        c, me, blocks, first, passed, whole = parts(*refs)
        for t in range(n):
            for j in range(3):
                passed(t, j, blocks[j], 1 - c).wait_recv()
        for s in range(ns):
            for j in range(3):
                whole(s, j, blocks[j]).wait_recv()
        for t in range(n):
            for j in range(3):
                first(t, j, me, c).wait_send()
                passed(t, j, blocks[j], c).wait_send()
        for s in range(ns):
            for j in range(3):
                whole(s, j, me).wait_send()

    arrays = [*shards, *small]
    shapes = [jax.ShapeDtypeStruct((N_CHIPS, *a.shape), a.dtype) for a in arrays]
    return _Plan(arrays, shapes, 6 * n + 3 * ns, start, finish, middle)


def _run_comm(plan, name):
    k = len(plan.arrays)
    in_specs, out_specs, out_shape, sems = plan.specs()

    def body(*refs):
        cr = (refs[:k], refs[k:k + len(out_shape)], refs[-2], refs[-1])
        plan.start(*cr)
        if plan.middle is not None:
            plan.middle(*cr)
        plan.finish(*cr)

    return _pcall(body, name=name, in_specs=in_specs, out_specs=out_specs, out_shape=out_shape,
                  input_output_aliases=plan.aliases, scratch_shapes=sems)(*plan.arrays)


def _carried(plan, in_specs, out_specs, out_shape, scratch):
    aliases = {}
    if plan is not None:
        p_in, p_out, p_shape, p_sems = plan.specs()
        aliases = {len(in_specs) + i: len(out_specs) + j for i, j in plan.aliases.items()}
        in_specs, out_specs = in_specs + p_in, out_specs + p_out
        out_shape, scratch = out_shape + p_shape, scratch + p_sems
    return dict(in_specs=in_specs, out_specs=out_specs, out_shape=out_shape, scratch_shapes=scratch,
                input_output_aliases=aliases)


def _unpack(refs, n_in, n_out, plan):
    k_in = len(plan.arrays) if plan else 0
    k_out = len(plan.out_shapes) if plan else 0
    ins = refs[:n_in]
    outs = refs[n_in + k_in:n_in + k_in + n_out]
    rest = refs[n_in + k_in + n_out + k_out:]
    if plan is None:
        return ins, outs, rest, None
    cr = (refs[n_in:n_in + k_in], refs[n_in + k_in + n_out:n_in + k_in + n_out + k_out], rest[-2], rest[-1])
    return ins, outs, rest[:-2], cr


def _hook(plan, cr, which, cond):
    fn = getattr(plan, which) if plan is not None else None
    if fn is not None:
        pl.when(cond)(lambda: fn(*cr))


def _join_plan(shards):
    n = len(shards)

    def copy(ins, outs, send_sems, recv_sems, t, core):
        x, y, c, _ = _place()
        half = ins[t].shape[0] // 2
        rows = pl.ds(pl.multiple_of(core * half, 8), half)
        return pltpu.make_async_remote_copy(
            src_ref=ins[t].at[rows, :], dst_ref=outs[t].at[rows, :], send_sem=send_sems.at[t],
            recv_sem=recv_sems.at[t], device_id=(x, y, 1 - c), device_id_type=MESH)

    def start(*refs):
        c = lax.axis_index("c")
        for t in range(n):
            copy(*refs, t, c).start()

    def finish(*refs):
        c = lax.axis_index("c")
        for t in range(n):
            copy(*refs, t, 1 - c).wait_recv()
        for t in range(n):
            copy(*refs, t, c).wait_send()

    shapes = [jax.ShapeDtypeStruct(s.shape, s.dtype) for s in shards]
    return _Plan(shards, shapes, n, start, finish, aliases={t: t for t in range(n)})


def _all_gather_plan(vec):
    def parts(ins, outs, send_sems, recv_sems):
        x, y, c, _ = _place()
        me = 4 * x + 2 * y + c
        rel = [((k >> 2) & 1, (k >> 1) & 1, k & 1) for k in range(1, N_DEV)]

        def peer(k):
            fx, fy, fc = rel[k]
            return (x ^ fx, y ^ fy, c ^ fc)

        def copy(k, dev):
            return pltpu.make_async_remote_copy(
                src_ref=ins[0], dst_ref=outs[0].at[dev], send_sem=send_sems.at[k], recv_sem=recv_sems.at[k],
                device_id=peer(k), device_id_type=MESH)

        mine = pltpu.make_async_copy(ins[0], outs[0].at[me], send_sems.at[N_DEV - 1])
        return me, peer, copy, mine

    def start(*refs):
        me, _, copy, mine = parts(*refs)
        mine.start()
        for k in range(N_DEV - 1):
            copy(k, me).start()

    def finish(*refs):
        me, peer, copy, mine = parts(*refs)
        for k in range(N_DEV - 1):
            px, py, pc = peer(k)
            copy(k, 4 * px + 2 * py + pc).wait_recv()
        for k in range(N_DEV - 1):
            copy(k, me).wait_send()
        mine.wait()

    return _Plan([vec], [jax.ShapeDtypeStruct((N_DEV, *vec.shape), vec.dtype)], N_DEV, start, finish)


def _sum_devices(blocks, name):
    def body(b_ref, o_ref):
        total = b_ref[0]
        for dev in range(1, N_DEV):
            total = total + b_ref[dev]
        o_ref[...] = total

    return _pcall(body, name=name, in_specs=[pl.BlockSpec(memory_space=pltpu.VMEM)],
                  out_specs=pl.BlockSpec(memory_space=pltpu.VMEM),
                  out_shape=jax.ShapeDtypeStruct(blocks.shape[1:], F32))(blocks)


TOKEN_TILE = 512
BWD_VMEM_LIMIT = 62 * 1024 * 1024
DW_TOKEN_TILE = 2048
DW_ROW_SPLIT = 2
MXU_COLS = 256


def _chunks(n):
    out, c0 = [], 0
    while c0 < n:
        size = min(MXU_COLS, n - c0)
        out.append((c0, size))
        c0 += size
    return out


def _load_weights(hbm_refs, vmem_refs, sems):
    copies = [pltpu.make_async_copy(h, v, sems.at[k]) for k, (h, v) in enumerate(zip(hbm_refs, vmem_refs))]
    for cp in copies:
        cp.start()
    for cp in copies:
        cp.wait()


def _ffn_fwd(x, g, wgt, wut, wd, name, plan=None, head=None):
    T, D = x.shape
    F = wgt.shape[0]
    tm = min(T, TOKEN_TILE)
    ni = T // tm
    n_head = 2 if head is not None else 0

    def body(*refs):
        ins, outs, scratch, cr = _unpack(refs, 5 + n_head, 5 + n_head, plan)
        x_ref, g_ref, wg_hbm, wu_hbm, wd_hbm = ins[:5]
        xo_ref, h_ref, gate_ref, up_ref, act_ref = outs[:5]
        wg_ref, wu_ref, wd_ref, sems = scratch
        i = pl.program_id(0)
        _hook(plan, cr, "start", i == 0)

        @pl.when(i == 0)
        def _():
            _load_weights((wg_hbm, wu_hbm, wd_hbm), (wg_ref, wu_ref, wd_ref), sems)

        xv = x_ref[...]
        h = ((xv * _rms_inv(xv)) * g_ref[...]).astype(BF16)
        h_ref[...] = h
        for c0, size in _chunks(F):
            gate = _dot_nt(h, wg_ref[c0:c0 + size, :])
            up = _dot_nt(h, wu_ref[c0:c0 + size, :])
            gate_ref[:, c0:c0 + size] = gate.astype(BF16)
            up_ref[:, c0:c0 + size] = up.astype(BF16)
            act_ref[:, c0:c0 + size] = (gate * jax.nn.sigmoid(gate) * up).astype(BF16)
        y = x_ref[...] + FFN_RES_SCALE * _dot(act_ref[...], wd_ref[...])
        if head is None:
            xo_ref[...] = y
        else:
            gf_ref, t_ref = ins[5:]
            dgf_ref, loss_ref = outs[5:]

            @pl.when(i == 0)
            def _():
                dgf_ref[...] = jnp.zeros_like(dgf_ref)
                loss_ref[...] = jnp.zeros_like(loss_ref)

            gf = gf_ref[...]
            diff = (y * _rms_inv(y)) * gf - t_ref[...]
            loss_ref[...] += 0.5 * jnp.sum(jnp.mean(diff * diff, axis=-1, keepdims=True))
            dy, dgf = _norm_bwd(diff * (1.0 / D), y, gf)
            xo_ref[...] = dy
            dgf_ref[...] += dgf
        _hook(plan, cr, "middle", i == (3 * ni) // 4)
        _hook(plan, cr, "finish", i == ni - 1)

    const = lambda shape: pl.BlockSpec(shape, lambda i: (0, 0))
    rows = lambda width: pl.BlockSpec((tm, width), lambda i: (i, 0))
    in_specs = [rows(D), const((1, D)), ANY_SPEC, ANY_SPEC, ANY_SPEC]
    out_specs = [rows(D), rows(D), rows(F), rows(F), rows(F)]
    out_shape = [jax.ShapeDtypeStruct((T, D), F32), jax.ShapeDtypeStruct((T, D), BF16),
                 jax.ShapeDtypeStruct((T, F), BF16), jax.ShapeDtypeStruct((T, F), BF16), jax.ShapeDtypeStruct((T, F), BF16)]
    if head is not None:
        in_specs += [const((1, D)), rows(D)]
        out_specs += [const((1, D)), const((1, LANES))]
        out_shape += [jax.ShapeDtypeStruct((1, D), F32), jax.ShapeDtypeStruct((1, LANES), F32)]
    io = _carried(plan, in_specs, out_specs, out_shape,
                  [pltpu.VMEM((F, D), BF16), pltpu.VMEM((F, D), BF16), pltpu.VMEM((F, D), BF16),
                   pltpu.SemaphoreType.DMA((3,))])
    return _pcall(
        body, name=name, grid=(ni,), compiler_params=_params(1), **io,
    )(x, g, wgt, wut, wd, *(head or ()), *(plan.arrays if plan else ()))


def _ffn_bwd(dy, x, g, gate, up, wgt, wut, wd, name, plan=None):
    T, D = x.shape
    F = wgt.shape[0]
    tm = min(T, TOKEN_TILE)
    ni = T // tm

    def body(*refs):
        ins, outs, scratch, cr = _unpack(refs, 8, 5, plan)
        dy_ref, x_ref, g_ref, gate_ref, up_ref, wg_hbm, wu_hbm, wd_hbm = ins
        dx_ref, dyb_ref, dgate_ref, dup_ref, dg_ref = outs
        wg_ref, wu_ref, wd_ref, sems = scratch
        i = pl.program_id(0)
        _hook(plan, cr, "start", i == 0)

        @pl.when(i == 0)
        def _():
            _load_weights((wg_hbm, wu_hbm, wd_hbm), (wg_ref, wu_ref, wd_ref), sems)
            dg_ref[...] = jnp.zeros_like(dg_ref)

        dyb = (FFN_RES_SCALE * dy_ref[...]).astype(BF16)
        dyb_ref[...] = dyb
        for c0, size in _chunks(F):
            dact = _dot_nt(dyb, wd_ref[c0:c0 + size, :])
            gt = gate_ref[:, c0:c0 + size].astype(F32)
            u = up_ref[:, c0:c0 + size].astype(F32)
            sig = jax.nn.sigmoid(gt)
            dup_ref[:, c0:c0 + size] = (dact * (gt * sig)).astype(BF16)
            dgate_ref[:, c0:c0 + size] = (dact * u * (sig * (1.0 + gt * (1.0 - sig)))).astype(BF16)
        dh = _dot(dgate_ref[...], wg_ref[...]) + _dot(dup_ref[...], wu_ref[...])
        dxn, dg = _norm_bwd(dh, x_ref[...], g_ref[...])
        dx_ref[...] = dy_ref[...] + dxn
        dg_ref[...] += dg
        _hook(plan, cr, "finish", i == ni - 1)

    io = _carried(
        plan,
        [pl.BlockSpec((tm, D), lambda i: (i, 0)), pl.BlockSpec((tm, D), lambda i: (i, 0)),
         pl.BlockSpec((1, D), lambda i: (0, 0)),
         pl.BlockSpec((tm, F), lambda i: (i, 0)), pl.BlockSpec((tm, F), lambda i: (i, 0)),
         ANY_SPEC, ANY_SPEC, ANY_SPEC],
        [pl.BlockSpec((tm, D), lambda i: (i, 0)), pl.BlockSpec((tm, D), lambda i: (i, 0)),
         pl.BlockSpec((tm, F), lambda i: (i, 0)), pl.BlockSpec((tm, F), lambda i: (i, 0)),
         pl.BlockSpec((1, D), lambda i: (0, 0))],
        [jax.ShapeDtypeStruct((T, D), F32), jax.ShapeDtypeStruct((T, D), BF16),
         jax.ShapeDtypeStruct((T, F), BF16), jax.ShapeDtypeStruct((T, F), BF16), jax.ShapeDtypeStruct((1, D), F32)],
        [pltpu.VMEM((F, D), BF16), pltpu.VMEM((F, D), BF16), pltpu.VMEM((F, D), BF16),
         pltpu.SemaphoreType.DMA((3,))])
    return _pcall(
        body, name=name, grid=(ni,), compiler_params=_params(1, BWD_VMEM_LIMIT), **io,
    )(dy, x, g, gate, up, wgt, wut, wd, *(plan.arrays if plan else ()))


def _matmul_tn(a, b, row_split, name, plan=None):
    T, n1 = a.shape
    n2 = b.shape[1]
    tn = n1 // row_split
    tk = min(T, DW_TOKEN_TILE)
    nk = T // tk

    def body(*refs):
        (a_ref, b_ref), (o_ref, ob_ref), _, cr = _unpack(refs, 2, 2, plan)
        j = pl.program_id(0)
        k = pl.program_id(1)
        _hook(plan, cr, "start", jnp.logical_and(j == 0, k == 0))

        @pl.when(k == 0)
        def _():
            o_ref[...] = jnp.zeros_like(o_ref)

        o_ref[...] += _dot_tn(a_ref[...], b_ref[...])

        @pl.when(k == nk - 1)
        def _():
            ob_ref[...] = o_ref[...].astype(BF16)

        _hook(plan, cr, "finish", jnp.logical_and(j == row_split - 1, k == nk - 1))

    io = _carried(
        plan,
        [pl.BlockSpec((tk, tn), lambda j, k: (k, j)), pl.BlockSpec((tk, n2), lambda j, k: (k, 0))],
        [pl.BlockSpec((tn, n2), lambda j, k: (j, 0)), pl.BlockSpec((tn, n2), lambda j, k: (j, 0))],
        [jax.ShapeDtypeStruct((n1, n2), F32), jax.ShapeDtypeStruct((n1, n2), BF16)], [])
    return _pcall(
        body, name=name, grid=(row_split, nk), compiler_params=_params(2), **io,
    )(a, b, *(plan.arrays if plan else ()))


def _norm_matmul(x, g, wt, name):
    T, D = x.shape
    n = wt.shape[0]
    tm = min(T, TOKEN_TILE)

    def body(x_ref, g_ref, w_ref, z_ref, h_ref):
        xv = x_ref[...]
        h = ((xv * _rms_inv(xv)) * g_ref[...]).astype(BF16)
        h_ref[...] = h
        z_ref[...] = _dot_nt(h, w_ref[...])

    return _pcall(
        body, name=name, grid=(T // tm,),
        in_specs=[pl.BlockSpec((tm, D), lambda i: (i, 0)), pl.BlockSpec((1, D), lambda i: (0, 0)),
                  pl.BlockSpec((n, D), lambda i: (0, 0))],
        out_specs=[pl.BlockSpec((tm, n), lambda i: (i, 0)), pl.BlockSpec((tm, D), lambda i: (i, 0))],
        out_shape=[jax.ShapeDtypeStruct((T, n), F32), jax.ShapeDtypeStruct((T, D), BF16)],
        compiler_params=_params(1),
    )(x, g, wt)


def _matmul_residual(y, w, x, name):
    T, D = x.shape
    kdim = y.shape[1]
    tm = min(T, TOKEN_TILE)

    def body(y_ref, w_ref, x_ref, o_ref):
        o_ref[...] = x_ref[...] + _dot(y_ref[...], w_ref[...])

    return _pcall(
        body, name=name, grid=(T // tm,),
        in_specs=[pl.BlockSpec((tm, kdim), lambda i: (i, 0)), pl.BlockSpec((kdim, D), lambda i: (0, 0)),
                  pl.BlockSpec((tm, D), lambda i: (i, 0))],
        out_specs=pl.BlockSpec((tm, D), lambda i: (i, 0)),
        out_shape=jax.ShapeDtypeStruct((T, D), F32),
        compiler_params=_params(1),
    )(y, w, x)


def _matmul_nt(dx, w, name, plan=None):
    T, D = dx.shape
    kdim = w.shape[0]
    tm = min(T, TOKEN_TILE)
    ni = T // tm

    def body(*refs):
        (dx_ref, w_ref), (dy_ref, dxb_ref), _, cr = _unpack(refs, 2, 2, plan)
        i = pl.program_id(0)
        _hook(plan, cr, "start", i == 0)
        dxb = dx_ref[...].astype(BF16)
        dxb_ref[...] = dxb
        dy_ref[...] = _dot_nt(dxb, w_ref[...])
        _hook(plan, cr, "finish", i == ni - 1)

    io = _carried(
        plan,
        [pl.BlockSpec((tm, D), lambda i: (i, 0)), pl.BlockSpec((kdim, D), lambda i: (0, 0))],
        [pl.BlockSpec((tm, kdim), lambda i: (i, 0)), pl.BlockSpec((tm, D), lambda i: (i, 0))],
        [jax.ShapeDtypeStruct((T, kdim), F32), jax.ShapeDtypeStruct((T, D), BF16)], [])
    return _pcall(
        body, name=name, grid=(ni,), compiler_params=_params(1), **io,
    )(dx, w, *(plan.arrays if plan else ()))


def _matmul_norm_bwd(dz, wt, x, g, dres, name, plan=None):
    T, D = x.shape
    n = dz.shape[1]
    tm = min(T, TOKEN_TILE)
    ni = T // tm

    def body(*refs):
        (dz_ref, w_ref, x_ref, g_ref, dres_ref), (dx_ref, dg_ref), _, cr = _unpack(refs, 5, 2, plan)
        i = pl.program_id(0)
        _hook(plan, cr, "start", i == 0)

        @pl.when(i == 0)
        def _():
            dg_ref[...] = jnp.zeros_like(dg_ref)

        dh = _dot(dz_ref[...], w_ref[...])
        dxn, dg = _norm_bwd(dh, x_ref[...], g_ref[...])
        dx_ref[...] = dres_ref[...] + dxn
        dg_ref[...] += dg
        _hook(plan, cr, "finish", i == ni - 1)

    io = _carried(
        plan,
        [pl.BlockSpec((tm, n), lambda i: (i, 0)), pl.BlockSpec((n, D), lambda i: (0, 0)),
         pl.BlockSpec((tm, D), lambda i: (i, 0)), pl.BlockSpec((1, D), lambda i: (0, 0)),
         pl.BlockSpec((tm, D), lambda i: (i, 0))],
        [pl.BlockSpec((tm, D), lambda i: (i, 0)), pl.BlockSpec((1, D), lambda i: (0, 0))],
        [jax.ShapeDtypeStruct((T, D), F32), jax.ShapeDtypeStruct((1, D), F32)], [])
    return _pcall(
        body, name=name, grid=(ni,), compiler_params=_params(1), **io,
    )(dz, wt, x, g, dres, *(plan.arrays if plan else ()))


Z_Q = 3 * CONV_WIDTH
Z_K = Z_Q + N_Q_HEADS * HEAD_DIM
Z_V = Z_K + LANES
Z_END = Z_V + LANES


def _rope_tables(T):
    half = ROT_DIM // 2
    inv_freq = ROPE_THETA ** (-jnp.arange(0, ROT_DIM, 2, dtype=F32) / ROT_DIM)
    ang = inv_freq[:, None] * jnp.arange(T, dtype=F32)[None, :]
    cos_sin = jnp.concatenate([jnp.cos(ang), jnp.sin(ang)], axis=0)
    select = np.zeros((2 * half, 3 * LANES), np.float32)
    const = np.zeros((1, 3 * LANES), np.float32)
    for lane in range(LANES):
        d = lane % HEAD_DIM
        if d < half:
            select[d, lane] = 1.0
            select[half + d, LANES + lane] = -1.0
        elif d < ROT_DIM:
            select[d - half, lane] = 1.0
            select[d, 2 * LANES + lane] = 1.0
        else:
            const[0, lane] = 1.0
    tab = lax.dot_general(cos_sin, jnp.asarray(select), (((0,), (0,)), ((), ())),
                          precision=lax.Precision.HIGHEST, preferred_element_type=F32)
    return tab + jnp.asarray(const)


def _tab3(tab):
    return tab[:, 0:LANES], tab[:, LANES:2 * LANES], tab[:, 2 * LANES:3 * LANES]


def _rot(x, tab):
    c, s1, s2 = _tab3(tab)
    return x * c + pltpu.roll(x, LANES - ROT_DIM // 2, 1) * s1 + pltpu.roll(x, ROT_DIM // 2, 1) * s2


def _rot_t(d, tab):
    c, s1, s2 = _tab3(tab)
    return d * c + pltpu.roll(d * s1, ROT_DIM // 2, 1) + pltpu.roll(d * s2, LANES - ROT_DIM // 2, 1)


def _head_pads(a):
    lo = lax.broadcasted_iota(jnp.int32, a.shape, 1) < HEAD_DIM
    nat0 = jnp.where(lo, a, 0.0)
    nat1 = jnp.where(lo, 0.0, a)
    return {
        (0, 0): nat0.astype(BF16), (0, 1): pltpu.roll(nat0, HEAD_DIM, 1).astype(BF16),
        (1, 0): pltpu.roll(nat1, HEAD_DIM, 1).astype(BF16), (1, 1): nat1.astype(BF16),
    }


def _from_pads(even, odd, kv):
    lo = lax.broadcasted_iota(jnp.int32, even.shape, 1) < HEAD_DIM
    if kv == 0:
        return jnp.where(lo, even + pltpu.roll(odd, HEAD_DIM, 1), 0.0)
    return jnp.where(lo, 0.0, pltpu.roll(even, HEAD_DIM, 1) + odd)


N_GROUPS = 4


def _group_head(g, r):
    kv, par = divmod(g, 2)
    return 2 * (2 * kv + r) + par


def _window_mask_t(has_prev):
    jj = lax.broadcasted_iota(jnp.int32, (2 * BLOCK, 2 * BLOCK), 0)
    ii = lax.broadcasted_iota(jnp.int32, (2 * BLOCK, 2 * BLOCK), 1) & (BLOCK - 1)
    rel = jj - BLOCK - ii
    return (rel <= 0) & (rel > -BLOCK) & ((jj >= BLOCK) | has_prev)


def _sink_row(sink_ref, g):
    lane = lax.broadcasted_iota(jnp.int32, (1, 2 * BLOCK), 1)
    return jnp.where(lane < BLOCK, sink_ref[0, _group_head(g, 0)], sink_ref[0, _group_head(g, 1)])


def _attn_probs_t(q2, kp, mask, sink_ref):
    out = []
    for kv in range(2):
        q_st = jnp.concatenate([q2[2 * kv], q2[2 * kv + 1]], axis=0)
        for par in range(2):
            s = jnp.where(mask, _dot_nt(kp[(kv, par)], q_st), MASK_VALUE)
            sink = _sink_row(sink_ref, 2 * kv + par)
            m = jnp.maximum(jnp.max(s, axis=0, keepdims=True), sink)
            p = jnp.exp(s - m)
            esink = jnp.exp(sink - m)
            rden = 1.0 / (jnp.sum(p, axis=0, keepdims=True) + esink)
            out.append((p * rden, esink * rden))
    return out


def _conv_taps(cg, u, cg_prev, u_prev, has_prev):
    vv = cg * u
    halo = jnp.where(has_prev, cg_prev * u_prev, 0.0)
    ext = jnp.concatenate([halo, vv], axis=0)
    rows = ext.shape[0]
    vv1 = pltpu.roll(ext, 1, 0)[8:rows]
    vv2 = pltpu.roll(ext, 2, 0)[8:rows]
    return vv, vv1, vv2


def _mix_specs(nb):
    cur = lambda n: jnp.minimum(n, nb - 1)
    prev = lambda n: jnp.maximum(jnp.minimum(n, nb - 1) - 1, 0)
    rows8_prev = lambda n: jnp.maximum(16 * jnp.minimum(n, nb - 1) - 1, 0)
    return cur, prev, [
        pl.BlockSpec((BLOCK, Z_END), lambda n: (cur(n), 0)),
        pl.BlockSpec((BLOCK, 2 * LANES), lambda n: (prev(n), Z_K // (2 * LANES))),
        pl.BlockSpec((8, CONV_WIDTH), lambda n: (rows8_prev(n), 1)),
        pl.BlockSpec((8, CONV_WIDTH), lambda n: (rows8_prev(n), 2)),
        pl.BlockSpec((BLOCK, 3 * LANES), lambda n: (cur(n), 0)),
        pl.BlockSpec((BLOCK, 3 * LANES), lambda n: (prev(n), 0)),
        pl.BlockSpec((3, CONV_WIDTH), lambda n: (0, 0)),
        pl.BlockSpec(memory_space=pltpu.SMEM),
    ]


def _mix_core_fwd(z, tab, conv_w, sinks, name):
    T = z.shape[0]
    nb = T // BLOCK
    _, _, specs = _mix_specs(nb)

    def body(z_ref, zkvp_ref, cgp_ref, up_ref, tab_ref, tabp_ref, cw_ref, sink_ref, y_ref):
        has_prev = pl.program_id(0) > 0
        bg = z_ref[:, 0:CONV_WIDTH]
        vv, vv1, vv2 = _conv_taps(z_ref[:, CONV_WIDTH:2 * CONV_WIDTH], z_ref[:, 2 * CONV_WIDTH:Z_Q],
                                  cgp_ref[...], up_ref[...], has_prev)
        conv = cw_ref[0:1, :] * vv2 + cw_ref[1:2, :] * vv1 + cw_ref[2:3, :] * vv
        y_ref[:, 0:CONV_WIDTH] = (bg * conv).astype(BF16)

        tab_c = tab_ref[...]
        tab_p = tabp_ref[...]
        k_all = jnp.concatenate([_rot(zkvp_ref[:, 0:LANES], tab_p), _rot(z_ref[:, Z_K:Z_V], tab_c)], axis=0)
        v_all = jnp.concatenate([zkvp_ref[:, LANES:2 * LANES], z_ref[:, Z_V:Z_END]], axis=0)
        kp = _head_pads(k_all)
        vp = _head_pads(v_all)
        q2 = [(_rot(z_ref[:, Z_Q + LANES * c:Z_Q + LANES * (c + 1)], tab_c) * ATTN_SCALE).astype(BF16)
              for c in range(N_Q_HEADS // 2)]
        probs = _attn_probs_t(q2, kp, _window_mask_t(has_prev), sink_ref)
        for kv in range(2):
            o_t = (_dot_tn(vp[(kv, 0)], probs[2 * kv][0].astype(BF16))
                   + _dot_tn(vp[(kv, 1)], probs[2 * kv + 1][0].astype(BF16)))
            for r in range(2):
                c = 2 * kv + r
                y_ref[:, CONV_WIDTH + LANES * c:CONV_WIDTH + LANES * (c + 1)] = o_t[:, BLOCK * r:BLOCK * (r + 1)].T.astype(BF16)

    return _pcall(
        body, name=name, grid=(nb,), in_specs=specs,
        out_specs=pl.BlockSpec((BLOCK, 2 * CONV_WIDTH), lambda n: (n, 0)),
        out_shape=jax.ShapeDtypeStruct((T, 2 * CONV_WIDTH), BF16),
        compiler_params=_params(1),
    )(z, z, z, z, tab, tab, conv_w, sinks)


def _mix_core_bwd(z, dy, tab, conv_w, sinks, name, plan=None):
    T = z.shape[0]
    nb = T // BLOCK
    cur, _, specs = _mix_specs(nb)
    rows8_next = lambda n: jnp.minimum(16 * (cur(n) + 1), 16 * nb - 1)
    specs = specs[:4] + [
        pl.BlockSpec((8, CONV_WIDTH), lambda n: (rows8_next(n), 0)),
        pl.BlockSpec((BLOCK, 2 * CONV_WIDTH), lambda n: (cur(n), 0)),
        pl.BlockSpec((8, CONV_WIDTH), lambda n: (rows8_next(n), 0)),
    ] + specs[4:]

    def body(*refs):
        ins, outs, scratch, cr = _unpack(refs, 11, 3, plan)
        z_ref, zkvp_ref, cgp_ref, up_ref, bgn_ref, dy_ref, dyn_ref, tab_ref, tabp_ref, cw_ref, sink_ref = ins
        dz_ref, dcw_ref, dsk_ref = outs
        main_ref, kv_ref = scratch
        n = pl.program_id(0)
        _hook(plan, cr, "start", n == 0)

        @pl.when(n == 0)
        def _():
            main_ref[...] = jnp.zeros_like(main_ref)
            kv_ref[...] = jnp.zeros_like(kv_ref)
            dcw_ref[...] = jnp.zeros_like(dcw_ref)
            dsk_ref[...] = jnp.zeros_like(dsk_ref)

        @pl.when(n < nb)
        def _():
            has_prev = n > 0
            has_next = n < nb - 1
            bg = z_ref[:, 0:CONV_WIDTH]
            cg = z_ref[:, CONV_WIDTH:2 * CONV_WIDTH]
            u = z_ref[:, 2 * CONV_WIDTH:Z_Q]
            vv, vv1, vv2 = _conv_taps(cg, u, cgp_ref[...], up_ref[...], has_prev)
            w0, w1, w2 = cw_ref[0:1, :], cw_ref[1:2, :], cw_ref[2:3, :]
            dyc = dy_ref[:, 0:CONV_WIDTH]
            dbg = dyc * (w0 * vv2 + w1 * vv1 + w2 * vv)
            dconv = dyc * bg
            dconv_next = jnp.where(has_next, dyn_ref[...] * bgn_ref[...], 0.0)
            ext = jnp.concatenate([dconv, dconv_next], axis=0)
            rows = ext.shape[0]
            dvv = w2 * dconv + w1 * pltpu.roll(ext, rows - 1, 0)[0:BLOCK] + w0 * pltpu.roll(ext, rows - 2, 0)[0:BLOCK]
            dcw_ref[0:1, :] += jnp.sum(dconv * vv2, axis=0, keepdims=True)
            dcw_ref[1:2, :] += jnp.sum(dconv * vv1, axis=0, keepdims=True)
            dcw_ref[2:3, :] += jnp.sum(dconv * vv, axis=0, keepdims=True)

            tab_c = tab_ref[...]
            tab_p = tabp_ref[...]
            k_all = jnp.concatenate([_rot(zkvp_ref[:, 0:LANES], tab_p), _rot(z_ref[:, Z_K:Z_V], tab_c)], axis=0)
            v_all = jnp.concatenate([zkvp_ref[:, LANES:2 * LANES], z_ref[:, Z_V:Z_END]], axis=0)
            kp = _head_pads(k_all)
            vp = _head_pads(v_all)
            chunks = range(N_Q_HEADS // 2)
            q2 = [(_rot(z_ref[:, Z_Q + LANES * c:Z_Q + LANES * (c + 1)], tab_c) * ATTN_SCALE).astype(BF16) for c in chunks]
            do2 = [dy_ref[:, CONV_WIDTH + LANES * c:CONV_WIDTH + LANES * (c + 1)].astype(BF16) for c in chunks]
            probs = _attn_probs_t(q2, kp, _window_mask_t(has_prev), sink_ref)
            dq_chunks = []
            dk_nat = jnp.zeros((2 * BLOCK, LANES), F32)
            dv_nat = jnp.zeros((2 * BLOCK, LANES), F32)
            for kv in range(2):
                q_st = jnp.concatenate([q2[2 * kv], q2[2 * kv + 1]], axis=0)
                do_st = jnp.concatenate([do2[2 * kv], do2[2 * kv + 1]], axis=0)
                dq_t = jnp.zeros((LANES, 2 * BLOCK), F32)
                dk_par, dv_par = [], []
                for par in range(2):
                    g = 2 * kv + par
                    pr, psink = probs[g]
                    dp = _dot_nt(vp[(kv, par)], do_st)
                    delta = jnp.sum(dp * pr, axis=0, keepdims=True)
                    ds = (pr * (dp - delta)).astype(BF16)
                    dsink = -psink * delta
                    for r in range(2):
                        h = _group_head(g, r)
                        dsk_ref[h:h + 1, :] += jnp.sum(dsink[:, BLOCK * r:BLOCK * (r + 1)])
                    dq_t = dq_t + _dot_tn(kp[(kv, par)], ds)
                    dk_par.append(_dot(ds, q_st))
                    dv_par.append(_dot(pr.astype(BF16), do_st))
                for r in range(2):
                    dq_chunks.append(_rot_t(dq_t[:, BLOCK * r:BLOCK * (r + 1)].T * ATTN_SCALE, tab_c))
                dk_nat = dk_nat + _from_pads(dk_par[0], dk_par[1], kv)
                dv_nat = dv_nat + _from_pads(dv_par[0], dv_par[1], kv)

            dk_prev = _rot_t(kv_ref[:, 0:LANES] + dk_nat[0:BLOCK], tab_p)
            dv_prev = kv_ref[:, LANES:2 * LANES] + dv_nat[0:BLOCK]
            dz_ref[:, 0:Z_K] = main_ref[...]
            dz_ref[:, Z_K:Z_V] = dk_prev.astype(BF16)
            dz_ref[:, Z_V:Z_END] = dv_prev.astype(BF16)
            main_ref[:, 0:CONV_WIDTH] = dbg.astype(BF16)
            main_ref[:, CONV_WIDTH:2 * CONV_WIDTH] = (dvv * u).astype(BF16)
            main_ref[:, 2 * CONV_WIDTH:Z_Q] = (dvv * cg).astype(BF16)
            for c in range(N_Q_HEADS // 2):
                main_ref[:, Z_Q + LANES * c:Z_Q + LANES * (c + 1)] = dq_chunks[c].astype(BF16)
            kv_ref[:, 0:LANES] = dk_nat[BLOCK:2 * BLOCK]
            kv_ref[:, LANES:2 * LANES] = dv_nat[BLOCK:2 * BLOCK]

        @pl.when(n == nb)
        def _():
            dz_ref[:, 0:Z_K] = main_ref[...]
            dz_ref[:, Z_K:Z_V] = _rot_t(kv_ref[:, 0:LANES], tab_ref[...]).astype(BF16)
            dz_ref[:, Z_V:Z_END] = kv_ref[:, LANES:2 * LANES].astype(BF16)

        _hook(plan, cr, "finish", n == nb)

    io = _carried(
        plan, specs,
        [pl.BlockSpec((BLOCK, Z_END), lambda n: (jnp.maximum(n - 1, 0), 0)),
         pl.BlockSpec((8, CONV_WIDTH), lambda n: (0, 0)), pl.BlockSpec((8, LANES), lambda n: (0, 0))],
        [jax.ShapeDtypeStruct((T, Z_END), BF16), jax.ShapeDtypeStruct((8, CONV_WIDTH), F32),
         jax.ShapeDtypeStruct((8, LANES), F32)],
        [pltpu.VMEM((BLOCK, Z_K), BF16), pltpu.VMEM((BLOCK, 2 * LANES), F32)])
    return _pcall(
        body, name=name, grid=(nb + 1,), compiler_params=_params(1), **io,
    )(z, z, z, z, z, dy, dy, tab, tab, conv_w, sinks, *(plan.arrays if plan else ()))


ROW_SPLIT = 1


def _pair_sum(grads, recvd, core, name):
    n = len(grads)

    def body(core_ref, *refs):
        g, r = refs[:n], refs[n:2 * n]
        s, sb = refs[2 * n:3 * n], refs[3 * n:]
        for t in range(n):
            tot = g[t][...] + r[t][...].astype(F32)
            s[t][...] = tot
            sb[t][...] = tot.astype(BF16)

    def blk(a):
        return (1, a.shape[1] // ROW_SPLIT, a.shape[2])

    in_specs = [pl.BlockSpec(blk(r), lambda q, i, core_ref: (q, core_ref[0] * ROW_SPLIT + i, 0)) for r in recvd]
    in_specs += [pl.BlockSpec(blk(r), lambda q, i, core_ref: (q, i, 0)) for r in recvd]
    out_specs = [pl.BlockSpec(blk(r), lambda q, i, core_ref: (q, i, 0)) for r in recvd] * 2
    return _pcall(
        body, name=name,
        grid_spec=pltpu.PrefetchScalarGridSpec(num_scalar_prefetch=1, grid=(N_CHIPS, ROW_SPLIT),
                                               in_specs=in_specs, out_specs=out_specs),
        out_shape=[jax.ShapeDtypeStruct(r.shape, F32) for r in recvd] + [jax.ShapeDtypeStruct(r.shape, BF16) for r in recvd],
        compiler_params=_params(2),
    )(core, *grads, *recvd)


def _chip_sum(parts, recvd, place, name):
    n = len(parts)

    def body(place_ref, *refs):
        p, r, o = refs[:n], refs[n:2 * n], refs[2 * n:]
        for t in range(n):
            tot = p[t][0]
            for j in range(3):
                tot = tot + r[t][j].astype(F32)
            o[t][...] = tot

    in_specs = [pl.BlockSpec((1, p.shape[1] // ROW_SPLIT, p.shape[2]), lambda i, place_ref: (place_ref[0], i, 0))
                for p in parts]
    in_specs += [pl.BlockSpec((3, r.shape[1] // ROW_SPLIT, r.shape[2]), lambda i, place_ref: (0, i, 0)) for r in recvd]
    out_specs = [pl.BlockSpec((p.shape[1] // ROW_SPLIT, p.shape[2]),
                              lambda i, place_ref: (place_ref[1] * ROW_SPLIT + i, 0)) for p in parts]
    return _pcall(
        body, name=name,
        grid_spec=pltpu.PrefetchScalarGridSpec(num_scalar_prefetch=1, grid=(ROW_SPLIT,),
                                               in_specs=in_specs, out_specs=out_specs),
        out_shape=[jax.ShapeDtypeStruct((2 * p.shape[1], p.shape[2]), F32) for p in parts],
        compiler_params=_params(1),
    )(place, *parts, *recvd)


def _adamw_math(w, g, m, v):
    m = ADAM_B1 * m + (1.0 - ADAM_B1) * g
    v = ADAM_B2 * v + (1.0 - ADAM_B2) * (g * g)
    m_hat = m / (1.0 - ADAM_B1 ** ADAM_STEP)
    v_hat = v / (1.0 - ADAM_B2 ** ADAM_STEP)
    delta = -ADAM_LR * (m_hat / (jnp.sqrt(v_hat) + ADAM_EPS) + ADAM_WD * w)
    return delta, m, v


def _adamw(ws, gs, ms, vs, row_blocks, name):
    n = len(ws)

    def body(*refs):
        w, g, m, v = refs[:n], refs[n:2 * n], refs[2 * n:3 * n], refs[3 * n:4 * n]
        d, mo, vo, go = refs[4 * n:5 * n], refs[5 * n:6 * n], refs[6 * n:7 * n], refs[7 * n:]
        for t in range(n):
            gv = g[t][...]
            delta, m_new, v_new = _adamw_math(w[t][...], gv, m[t][...], v[t][...])
            d[t][...] = delta
            mo[t][...] = m_new
            vo[t][...] = v_new
            go[t][...] = gv

    specs = [pl.BlockSpec((a.shape[0] // row_blocks, a.shape[1]), lambda i: (i, 0)) for a in ws]
    shapes = [jax.ShapeDtypeStruct(a.shape, F32) for a in ws]
    return _pcall(
        body, name=name, grid=(row_blocks,), in_specs=specs * 4, out_specs=specs * 4, out_shape=shapes * 4,
        compiler_params=_params(1),
    )(*ws, *gs, *ms, *vs)


def kernel(x, ffn1_norm, ffn1_w_gate, ffn1_w_up, ffn1_w_down, mix_norm, w_in, conv_w, attn_sinks, w_out, ffn2_norm, ffn2_w_gate, ffn2_w_up, ffn2_w_down, final_norm, loss_target, m_ffn1_norm, m_ffn1_w_gate, m_ffn1_w_up, m_ffn1_w_down, m_mix_norm, m_w_in, m_conv_w, m_attn_sinks, m_w_out, m_ffn2_norm, m_ffn2_w_gate, m_ffn2_w_up, m_ffn2_w_down, m_final_norm, v_ffn1_norm, v_ffn1_w_gate, v_ffn1_w_up, v_ffn1_w_down, v_mix_norm, v_w_in, v_conv_w, v_attn_sinks, v_w_out, v_ffn2_norm, v_ffn2_w_gate, v_ffn2_w_up, v_ffn2_w_down, v_final_norm):
    T, D = x.shape[1], x.shape[2]
    chip = (2 * lax.axis_index("x") + lax.axis_index("y")).astype(jnp.int32)
    core = lax.axis_index("c").astype(jnp.int32)
    place = jnp.stack([chip, core])
    x0 = x[0]
    target = loss_target[0]
    gf = final_norm.reshape(1, D)

    tr = lambda w: jnp.swapaxes(w[0], 0, 1)
    big = [tr(ffn1_w_gate), tr(ffn1_w_up), ffn1_w_down[0], tr(w_in), w_out[0], tr(ffn2_w_gate), tr(ffn2_w_up), ffn2_w_down[0]]
    transposed = [True, True, False, True, False, True, True, False]
    own_b = [w.astype(BF16) for w in big]

    def whole(gathered, own):
        return lax.dynamic_update_slice(gathered, own[None], (chip, 0, 0)).reshape(-1, D)

    got1 = _run_comm(_gather_plan(own_b[0:3]), "gather_ffn1")
    wg1, wu1, wd1 = (whole(g, o) for g, o in zip(got1, own_b[0:3]))
    tab = _rope_tables(T)

    res = _ffn_fwd(x0, ffn1_norm, wg1, wu1, wd1, "ffn1_fwd", _gather_plan(own_b[3:8], [conv_w[0]]))
    x1, h1, gate1, up1, act1 = res[:5]
    win, wout, wg2, wu2, wd2 = (whole(g, o) for g, o in zip(res[5:10], own_b[3:8]))
    convw4 = lax.dynamic_update_slice(res[10], conv_w, (chip, 0, 0))
    convw = jnp.transpose(convw4, (1, 0, 2)).reshape(3, -1)
    z, hm = _norm_matmul(x1, mix_norm, win, "mix_in_fwd")
    ymix = _mix_core_fwd(z, tab, convw, attn_sinks, "mix_core_fwd")
    x2 = _matmul_residual(ymix, wout, x1, "mix_out_fwd")
    dx3, h2, gate2, up2, act2, dgf, loss_part = _ffn_fwd(x2, ffn2_norm, wg2, wu2, wd2, "ffn2_fwd", head=(gf, target))

    dx2, dyb2, dgate2, dup2, dg2 = _ffn_bwd(dx3, x2, ffn2_norm, gate2, up2, wg2, wu2, wd2, "ffn2_bwd")
    dymix, dx2b = _matmul_nt(dx2, wout, "mix_out_bwd")
    dz, dcw, dsk = _mix_core_bwd(z, dymix, tab, convw, attn_sinks, "mix_core_bwd")
    dx1, dgm = _matmul_norm_bwd(dz, win, x1, mix_norm, dx2, "mix_in_bwd")
    dx0, dyb1, dgate1, dup1, dg1 = _ffn_bwd(dx1, x0, ffn1_norm, gate1, up1, wg1, wu1, wd1, "ffn1_bwd")

    pad = lambda a: jnp.pad(a, ((0, 0), (0, LANES - a.shape[1])))
    vec = jnp.concatenate([dg1, dgm, dg2, dgf, dcw[0:3].reshape(1, -1), pad(dsk[:, 0].reshape(1, -1)),
                           pad(loss_part[:, 0:1])], axis=1)

    jobs = [("ffn2_dwg", dgate2, h2, 5), ("ffn2_dwu", dup2, h2, 6), ("ffn2_dwd", act2, dyb2, 7),
            ("ffn1_dwg", dgate1, h1, 0), ("ffn1_dwu", dup1, h1, 1), ("ffn1_dwd", act1, dyb1, 2),
            ("mix_dwin", dz, hm, 3), ("mix_dwout", ymix, dx2b, 4)]
    n_jobs = len(jobs)
    grad, grad_b, from_sib, pair_f, pair_b, from_chips, half, g_big = ({} for _ in range(8))

    def stage_plans(t):
        plans, takers = [], []
        if 0 <= t - 1 < n_jobs:
            plans.append(_sibling_plan([grad_b[t - 1]]))
            takers.append((from_sib, t - 1))
        if 0 <= t - 2 < n_jobs:
            plans.append(_scatter_plan([pair_b[t - 2]]))
            takers.append((from_chips, t - 2))
        if 0 <= t - 3 < n_jobs:
            plans.append(_join_plan([half[t - 3]]))
            takers.append((g_big, jobs[t - 3][3]))
        return plans, takers

    def after_stage(t, landed, takers):
        for (store, key), arr in zip(takers, landed):
            store[key] = arr
        if 0 <= t - 1 < n_jobs:
            pair_f[t - 1], pair_b[t - 1] = _pair_sum([grad[t - 1]], [from_sib[t - 1]], core.reshape(1), f"pair_sum_{t - 1}")
        if 0 <= t - 2 < n_jobs:
            half[t - 2], = _chip_sum([pair_f[t - 2]], [from_chips[t - 2]], place, f"chip_sum_{t - 2}")

    for t, (name_, a, b, _) in enumerate(jobs):
        plans, takers = stage_plans(t)
        if t == 0:
            plans.append(_all_gather_plan(jnp.pad(vec, ((0, 7), (0, 0)))))
        res = _matmul_tn(a, b, DW_ROW_SPLIT, name_, _merge_plans(plans))
        grad[t], grad_b[t] = (r.reshape(N_CHIPS, -1, D) for r in res[:2])
        landed = list(res[2:])
        if t == 0:
            vec_blocks = landed.pop()
        after_stage(t, landed, takers)

    ws = big
    ms = [tr(m_ffn1_w_gate), tr(m_ffn1_w_up), m_ffn1_w_down[0], tr(m_w_in), m_w_out[0], tr(m_ffn2_w_gate), tr(m_ffn2_w_up), m_ffn2_w_down[0]]
    vs = [tr(v_ffn1_w_gate), tr(v_ffn1_w_up), v_ffn1_w_down[0], tr(v_w_in), v_w_out[0], tr(v_ffn2_w_gate), tr(v_ffn2_w_up), v_ffn2_w_down[0]]
    for t in range(n_jobs, n_jobs + 3):
        plans, takers = stage_plans(t)
        after_stage(t, _run_comm(_merge_plans(plans), f"grads_tail_{t - n_jobs}"), takers)
    upd = {}
    for name_, idx in (("adamw_a", [0, 1, 2, 4]), ("adamw_b", [3, 5, 6, 7])):
        k = len(idx)
        res = _adamw([ws[i] for i in idx], [g_big[i] for i in idx], [ms[i] for i in idx], [vs[i] for i in idx], 8, name_)
        for j, i in enumerate(idx):
            upd[i] = (res[j], res[k + j], res[2 * k + j])
            g_big[i] = res[3 * k + j]

    total = _sum_devices(vec_blocks, "small_sum")[0:1]
    g_n1, g_nm, g_n2, g_nf = (total[:, k * D:(k + 1) * D] for k in range(4))
    cw_full = total[:, 4 * D:4 * D + 3 * CONV_WIDTH].reshape(3, CONV_WIDTH)
    cq = CONV_WIDTH // N_CHIPS
    g_cw = lax.dynamic_slice(cw_full, (0, chip * cq), (3, cq))
    off = 4 * D + 3 * CONV_WIDTH
    g_sk = total[:, off:off + N_Q_HEADS]
    loss = total[0, off + LANES]

    sw = [ffn1_norm, mix_norm, conv_w[0], attn_sinks, ffn2_norm, gf]
    sg = [g_n1, g_nm, g_cw, g_sk, g_n2, g_nf]
    sm = [m_ffn1_norm, m_mix_norm, m_conv_w[0], m_attn_sinks, m_ffn2_norm, m_final_norm.reshape(1, D)]
    sv = [v_ffn1_norm, v_mix_norm, v_conv_w[0], v_attn_sinks, v_ffn2_norm, v_final_norm.reshape(1, D)]
    sres = _adamw(sw, sg, sm, sv, 1, "adamw_small")
    supd = [(sres[j], sres[6 + j], sres[12 + j]) for j in range(6)]

    order = [("s", 0), ("b", 0), ("b", 1), ("b", 2), ("s", 1), ("b", 3), ("s", 2), ("s", 3), ("b", 4),
             ("s", 4), ("b", 5), ("b", 6), ("b", 7), ("s", 5)]

    def leaf(kind, i, which):
        if kind == "b":
            a = g_big[i] if which == 0 else upd[i][which - 1]
            return (jnp.swapaxes(a, 0, 1) if transposed[i] else a)[None]
        a = sg[i] if which == 0 else supd[i][which - 1]
        if i == 2:
            return a[None]
        if i == 5:
            return a.reshape(D)
        return a

    outs = [loss, dx0[None]]
    for which in range(4):
        outs += [leaf(kind, i, which) for kind, i in order]
    return tuple(outs)
```

```python
import functools

import jax
import jax.numpy as jnp
import numpy as np
from jax import lax
from jax.experimental import pallas as pl
from jax.experimental.pallas import tpu as pltpu

F32 = jnp.float32
BF16 = jnp.bfloat16
MESH = pl.DeviceIdType.MESH

CONV_WIDTH = 512
N_Q_HEADS = 8
HEAD_DIM = 64
BLOCK = 128
ROPE_THETA = 500000.0
ROT_DIM = 16
RMS_EPS = 1e-5
MASK_VALUE = -1e30
ATTN_SCALE = HEAD_DIM ** -0.5
FFN_RES_SCALE = 0.5
ADAM_LR = 0.001
ADAM_B1 = 0.9
ADAM_B2 = 0.999
ADAM_EPS = 1e-08
ADAM_WD = 0.01
ADAM_STEP = 10

N_CHIPS = 4
N_DEV = 8
LANES = 128
VMEM_LIMIT = 56 * 1024 * 1024

_pcall = pl.pallas_call
HBM_SPEC = pl.BlockSpec(memory_space=pltpu.HBM)
ANY_SPEC = pl.BlockSpec(memory_space=pl.ANY)


def _params(n_axes, vmem=VMEM_LIMIT):
    return pltpu.CompilerParams(dimension_semantics=("arbitrary",) * n_axes, vmem_limit_bytes=vmem)


def _dot(a, b):
    return jnp.dot(a, b, preferred_element_type=F32)


def _dot_nt(a, b):
    return lax.dot_general(a, b, (((1,), (1,)), ((), ())), preferred_element_type=F32)


def _dot_tn(a, b):
    return lax.dot_general(a, b, (((0,), (0,)), ((), ())), preferred_element_type=F32)


def _rms_inv(x):
    return lax.rsqrt(jnp.mean(x * x, axis=-1, keepdims=True) + RMS_EPS)


def _norm_bwd(dh, x, g):
    inv = _rms_inv(x)
    xhat = x * inv
    dg = jnp.sum(dh * xhat, axis=0, keepdims=True)
    dxhat = dh * g
    dx = inv * (dxhat - xhat * jnp.mean(dxhat * xhat, axis=-1, keepdims=True))
    return dx, dg


def _place():
    x, y, c = lax.axis_index("x"), lax.axis_index("y"), lax.axis_index("c")
    chips = [(1 - x, y), (x, 1 - y), (1 - x, 1 - y)]
    return x, y, c, chips


class _Plan:
    def __init__(self, arrays, out_shapes, n_sems, start, finish, middle=None, aliases=None):
        self.arrays, self.out_shapes, self.n_sems = list(arrays), list(out_shapes), n_sems
        self.start, self.finish, self.middle = start, finish, middle
        self.aliases = dict(aliases or {})

    def specs(self):
        k = len(self.arrays)
        sems = [pltpu.SemaphoreType.DMA((self.n_sems,)), pltpu.SemaphoreType.DMA((self.n_sems,))]
        return [HBM_SPEC] * k, [HBM_SPEC] * len(self.out_shapes), self.out_shapes, sems


class _SemSlice:
    def __init__(self, ref, offset):
        self.ref, self.offset = ref, offset

    @property
    def at(self):
        return self

    def __getitem__(self, k):
        return self.ref.at[k + self.offset]


def _merge_plans(plans):
    plans = [p for p in plans if p is not None]
    if len(plans) <= 1:
        return plans[0] if plans else None
    arrays, shapes, aliases, spans, n_sems = [], [], {}, [], 0
    for p in plans:
        a0, o0 = len(arrays), len(shapes)
        spans.append((a0, a0 + len(p.arrays), o0, o0 + len(p.out_shapes), n_sems))
        aliases.update({a0 + i: o0 + j for i, j in p.aliases.items()})
        arrays += p.arrays
        shapes += p.out_shapes
        n_sems += p.n_sems

    def run(which):
        def fn(ins, outs, send_sems, recv_sems):
            for p, (a0, a1, o0, o1, s0) in zip(plans, spans):
                part = getattr(p, which)
                if part is not None:
                    part(ins[a0:a1], outs[o0:o1], _SemSlice(send_sems, s0), _SemSlice(recv_sems, s0))
        return fn

    middle = run("middle") if any(p.middle is not None for p in plans) else None
    return _Plan(arrays, shapes, n_sems, run("start"), run("finish"), middle, aliases)


def _sibling_plan(grads_b):
    n = len(grads_b)

    def copies(ins, outs, send_sems, recv_sems):
        x, y, c, _ = _place()

        def copy(t):
            half = ins[t].shape[1] // 2
            return pltpu.make_async_remote_copy(
                src_ref=ins[t].at[:, pl.ds(pl.multiple_of((1 - c) * half, 16), half), :], dst_ref=outs[t],
                send_sem=send_sems.at[t], recv_sem=recv_sems.at[t], device_id=(x, y, 1 - c), device_id_type=MESH)

        return [copy(t) for t in range(n)]

    def start(*refs):
        for cp in copies(*refs):
            cp.start()

    def finish(*refs):
        for cp in copies(*refs):
            cp.wait()

    shapes = [jax.ShapeDtypeStruct((g.shape[0], g.shape[1] // 2, g.shape[2]), g.dtype) for g in grads_b]
    return _Plan(grads_b, shapes, n, start, finish)


def _scatter_plan(parts_b):
    n = len(parts_b)

    def copies(ins, outs, send_sems, recv_sems):
        x, y, c, chips = _place()

        def copy(t, j):
            px, py = chips[j]
            return pltpu.make_async_remote_copy(
                src_ref=ins[t].at[2 * px + py], dst_ref=outs[t].at[j], send_sem=send_sems.at[3 * t + j],
                recv_sem=recv_sems.at[3 * t + j], device_id=(px, py, c), device_id_type=MESH)

        return [copy(t, j) for t in range(n) for j in range(3)]

    def start(*refs):
        for cp in copies(*refs):
            cp.start()

    def finish(*refs):
        for cp in copies(*refs):
            cp.wait()

    shapes = [jax.ShapeDtypeStruct((3, *p.shape[1:]), p.dtype) for p in parts_b]
    return _Plan(parts_b, shapes, 3 * n, start, finish)


def _gather_plan(shards, small=()):
    n, ns = len(shards), len(small)
    per = 8

    def parts(ins, outs, send_sems, recv_sems):
        x, y, c, chips = _place()
        me = 2 * x + y
        blocks = [2 * px + py for px, py in chips]

        def rows(t, core, piece=None):
            half = ins[t].shape[0] // 2
            if piece is None:
                return pl.ds(pl.multiple_of(core * half, 16), half)
            return pl.ds(pl.multiple_of(core * half + piece * (half // 2), 16), half // 2)

        def remote(src, dst, k, device):
            return pltpu.make_async_remote_copy(src_ref=src, dst_ref=dst, send_sem=send_sems.at[k],
                                                recv_sem=recv_sems.at[k], device_id=device, device_id_type=MESH)

        def first(t, j, block, core):
            return remote(ins[t].at[rows(t, core), :], outs[t].at[block, rows(t, core), :], per * t + j, (*chips[j], c))

        def relay(t, j, block, core):
            ref = outs[t].at[block, rows(t, core, j), :]
            return remote(ref, ref, per * t + 2 + j, (*chips[j], c))

        def passed(t, k, block, core, piece=None):
            ref = outs[t].at[block, rows(t, core, piece), :]
            return remote(ref, ref, per * t + 4 + k, (x, y, 1 - c))

        def whole(s, j, block):
            return remote(ins[n + s], outs[n + s].at[block], per * n + 3 * s + j, (*chips[j], c))

        return c, me, blocks, first, relay, passed, whole

    def start(*refs):
        c, me, _, first, _, _, whole = parts(*refs)
        for t in range(n):
            for j in range(2):
                first(t, j, me, c).start()
        for s in range(ns):
            for j in range(3):
                whole(s, j, me).start()

    def middle(*refs):
        c, _, blocks, first, relay, passed, _ = parts(*refs)
        for t in range(n):
            for j in range(2):
                first(t, j, blocks[j], c).wait_recv()
                passed(t, j, blocks[j], c).start()
                relay(t, 1 - j, blocks[j], c).start()

    def finish(*refs):
        c, me, blocks, first, relay, passed, whole = parts(*refs)
        for t in range(n):
            for j in range(2):
                relay(t, j, blocks[2], c).wait_recv()
                passed(t, 2 + j, blocks[2], c, j).start()
        for t in range(n):
            for j in range(2):
                passed(t, j, blocks[j], 1 - c).wait_recv()
                passed(t, 2 + j, blocks[2], 1 - c, j).wait_recv()
        for s in range(ns):
            for j in range(3):
                whole(s, j, blocks[j]).wait_recv()
        for t in range(n):
            for j in range(2):
                first(t, j, me, c).wait_send()
                relay(t, 1 - j, blocks[j], c).wait_send()
                passed(t, j, blocks[j], c).wait_send()
                passed(t, 2 + j, blocks[2], c, j).wait_send()
        for s in range(ns):
            for j in range(3):
                whole(s, j, me).wait_send()

    arrays = [*shards, *small]
    shapes = [jax.ShapeDtypeStruct((N_CHIPS, *a.shape), a.dtype) for a in arrays]
    return _Plan(arrays, shapes, per * n + 3 * ns, start, finish, middle)


def _run_comm(plan, name):
    k = len(plan.arrays)
    in_specs, out_specs, out_shape, sems = plan.specs()

    def body(*refs):
        cr = (refs[:k], refs[k:k + len(out_shape)], refs[-2], refs[-1])
        plan.start(*cr)
        if plan.middle is not None:
            plan.middle(*cr)
        plan.finish(*cr)

    return _pcall(body, name=name, in_specs=in_specs, out_specs=out_specs, out_shape=out_shape,
                  input_output_aliases=plan.aliases, scratch_shapes=sems)(*plan.arrays)


def _carried(plan, in_specs, out_specs, out_shape, scratch):
    aliases = {}
    if plan is not None:
        p_in, p_out, p_shape, p_sems = plan.specs()
        aliases = {len(in_specs) + i: len(out_specs) + j for i, j in plan.aliases.items()}
        in_specs, out_specs = in_specs + p_in, out_specs + p_out
        out_shape, scratch = out_shape + p_shape, scratch + p_sems
    return dict(in_specs=in_specs, out_specs=out_specs, out_shape=out_shape, scratch_shapes=scratch,
                input_output_aliases=aliases)


def _unpack(refs, n_in, n_out, plan):
    k_in = len(plan.arrays) if plan else 0
    k_out = len(plan.out_shapes) if plan else 0
    ins = refs[:n_in]
    outs = refs[n_in + k_in:n_in + k_in + n_out]
    rest = refs[n_in + k_in + n_out + k_out:]
    if plan is None:
        return ins, outs, rest, None
    cr = (refs[n_in:n_in + k_in], refs[n_in + k_in + n_out:n_in + k_in + n_out + k_out], rest[-2], rest[-1])
    return ins, outs, rest[:-2], cr


def _hook(plan, cr, which, cond):
    fn = getattr(plan, which) if plan is not None else None
    if fn is not None:
        pl.when(cond)(lambda: fn(*cr))


def _join_plan(shards):
    n = len(shards)

    def copy(ins, outs, send_sems, recv_sems, t, core):
        x, y, c, _ = _place()
        half = ins[t].shape[0] // 2
        rows = pl.ds(pl.multiple_of(core * half, 8), half)
        return pltpu.make_async_remote_copy(
            src_ref=ins[t].at[rows, :], dst_ref=outs[t].at[rows, :], send_sem=send_sems.at[t],
            recv_sem=recv_sems.at[t], device_id=(x, y, 1 - c), device_id_type=MESH)

    def start(*refs):
        c = lax.axis_index("c")
        for t in range(n):
            copy(*refs, t, c).start()

    def finish(*refs):
        c = lax.axis_index("c")
        for t in range(n):
            copy(*refs, t, 1 - c).wait_recv()
        for t in range(n):
            copy(*refs, t, c).wait_send()

    shapes = [jax.ShapeDtypeStruct(s.shape, s.dtype) for s in shards]
    return _Plan(shards, shapes, n, start, finish, aliases={t: t for t in range(n)})


def _all_gather_plan(vec):
    def parts(ins, outs, send_sems, recv_sems):
        x, y, c, _ = _place()
        me = 4 * x + 2 * y + c
        rel = [((k >> 2) & 1, (k >> 1) & 1, k & 1) for k in range(1, N_DEV)]

        def peer(k):
            fx, fy, fc = rel[k]
            return (x ^ fx, y ^ fy, c ^ fc)

        def copy(k, dev):
            return pltpu.make_async_remote_copy(
                src_ref=ins[0], dst_ref=outs[0].at[dev], send_sem=send_sems.at[k], recv_sem=recv_sems.at[k],
                device_id=peer(k), device_id_type=MESH)

        mine = pltpu.make_async_copy(ins[0], outs[0].at[me], send_sems.at[N_DEV - 1])
        return me, peer, copy, mine

    def start(*refs):
        me, _, copy, mine = parts(*refs)
        mine.start()
        for k in range(N_DEV - 1):
            copy(k, me).start()

    def finish(*refs):
        me, peer, copy, mine = parts(*refs)
        for k in range(N_DEV - 1):
            px, py, pc = peer(k)
            copy(k, 4 * px + 2 * py + pc).wait_recv()
        for k in range(N_DEV - 1):
            copy(k, me).wait_send()
        mine.wait()

    return _Plan([vec], [jax.ShapeDtypeStruct((N_DEV, *vec.shape), vec.dtype)], N_DEV, start, finish)


def _sum_devices(blocks, name):
    def body(b_ref, o_ref):
        total = b_ref[0]
        for dev in range(1, N_DEV):
            total = total + b_ref[dev]
        o_ref[...] = total

    return _pcall(body, name=name, in_specs=[pl.BlockSpec(memory_space=pltpu.VMEM)],
                  out_specs=pl.BlockSpec(memory_space=pltpu.VMEM),
                  out_shape=jax.ShapeDtypeStruct(blocks.shape[1:], F32))(blocks)


TOKEN_TILE = 512
PROJ_TOKEN_TILE = 1024
BWD_VMEM_LIMIT = 62 * 1024 * 1024
DW_TOKEN_TILE = 2048
DW_ROW_SPLIT = 2
MXU_COLS = 256


def _chunks(n):
    out, c0 = [], 0
    while c0 < n:
        size = min(MXU_COLS, n - c0)
        out.append((c0, size))
        c0 += size
    return out


def _load_weights(hbm_refs, vmem_refs, sems):
    copies = [pltpu.make_async_copy(h, v, sems.at[k]) for k, (h, v) in enumerate(zip(hbm_refs, vmem_refs))]
    for cp in copies:
        cp.start()
    for cp in copies:
        cp.wait()


def _ffn_fwd(x, g, wgt, wut, wd, name, plan=None, head=None):
    T, D = x.shape
    F = wgt.shape[0]
    tm = min(T, TOKEN_TILE)
    ni = T // tm
    n_head = 2 if head is not None else 0

    def body(*refs):
        ins, outs, scratch, cr = _unpack(refs, 5 + n_head, 5 + n_head, plan)
        x_ref, g_ref, wg_hbm, wu_hbm, wd_hbm = ins[:5]
        xo_ref, h_ref, gate_ref, up_ref, act_ref = outs[:5]
        wg_ref, wu_ref, wd_ref, sems = scratch
        i = pl.program_id(0)
        _hook(plan, cr, "start", i == 0)

        @pl.when(i == 0)
        def _():
            _load_weights((wg_hbm, wu_hbm, wd_hbm), (wg_ref, wu_ref, wd_ref), sems)

        xv = x_ref[...]
        h = ((xv * _rms_inv(xv)) * g_ref[...]).astype(BF16)
        h_ref[...] = h
        for c0, size in _chunks(F):
            gate = _dot_nt(h, wg_ref[c0:c0 + size, :])
            up = _dot_nt(h, wu_ref[c0:c0 + size, :])
            gate_ref[:, c0:c0 + size] = gate.astype(BF16)
            up_ref[:, c0:c0 + size] = up.astype(BF16)
            act_ref[:, c0:c0 + size] = (gate * jax.nn.sigmoid(gate) * up).astype(BF16)
        y = x_ref[...] + FFN_RES_SCALE * _dot(act_ref[...], wd_ref[...])
        if head is None:
            xo_ref[...] = y
        else:
            gf_ref, t_ref = ins[5:]
            dgf_ref, loss_ref = outs[5:]

            @pl.when(i == 0)
            def _():
                dgf_ref[...] = jnp.zeros_like(dgf_ref)
                loss_ref[...] = jnp.zeros_like(loss_ref)

            gf = gf_ref[...]
            diff = (y * _rms_inv(y)) * gf - t_ref[...]
            loss_ref[...] += 0.5 * jnp.sum(jnp.mean(diff * diff, axis=-1, keepdims=True))
            dy, dgf = _norm_bwd(diff * (1.0 / D), y, gf)
            xo_ref[...] = dy
            dgf_ref[...] += dgf
        _hook(plan, cr, "middle", i == (3 * ni) // 4)
        _hook(plan, cr, "finish", i == ni - 1)

    const = lambda shape: pl.BlockSpec(shape, lambda i: (0, 0))
    rows = lambda width: pl.BlockSpec((tm, width), lambda i: (i, 0))
    in_specs = [rows(D), const((1, D)), ANY_SPEC, ANY_SPEC, ANY_SPEC]
    out_specs = [rows(D), rows(D), rows(F), rows(F), rows(F)]
    out_shape = [jax.ShapeDtypeStruct((T, D), F32), jax.ShapeDtypeStruct((T, D), BF16),
                 jax.ShapeDtypeStruct((T, F), BF16), jax.ShapeDtypeStruct((T, F), BF16), jax.ShapeDtypeStruct((T, F), BF16)]
    if head is not None:
        in_specs += [const((1, D)), rows(D)]
        out_specs += [const((1, D)), const((1, LANES))]
        out_shape += [jax.ShapeDtypeStruct((1, D), F32), jax.ShapeDtypeStruct((1, LANES), F32)]
    io = _carried(plan, in_specs, out_specs, out_shape,
                  [pltpu.VMEM((F, D), BF16), pltpu.VMEM((F, D), BF16), pltpu.VMEM((F, D), BF16),
                   pltpu.SemaphoreType.DMA((3,))])
    return _pcall(
        body, name=name, grid=(ni,), compiler_params=_params(1), **io,
    )(x, g, wgt, wut, wd, *(head or ()), *(plan.arrays if plan else ()))


def _ffn_bwd(dy, x, g, gate, up, wgt, wut, wd, name, plan=None):
    T, D = x.shape
    F = wgt.shape[0]
    tm = min(T, TOKEN_TILE)
    ni = T // tm

    def body(*refs):
        ins, outs, scratch, cr = _unpack(refs, 8, 5, plan)
        dy_ref, x_ref, g_ref, gate_ref, up_ref, wg_hbm, wu_hbm, wd_hbm = ins
        dx_ref, dyb_ref, dgate_ref, dup_ref, dg_ref = outs
        wg_ref, wu_ref, wd_ref, sems = scratch
        i = pl.program_id(0)
        _hook(plan, cr, "start", i == 0)

        @pl.when(i == 0)
        def _():
            _load_weights((wg_hbm, wu_hbm, wd_hbm), (wg_ref, wu_ref, wd_ref), sems)
            dg_ref[...] = jnp.zeros_like(dg_ref)

        dyb = (FFN_RES_SCALE * dy_ref[...]).astype(BF16)
        dyb_ref[...] = dyb
        for c0, size in _chunks(F):
            dact = _dot_nt(dyb, wd_ref[c0:c0 + size, :])
            gt = gate_ref[:, c0:c0 + size].astype(F32)
            u = up_ref[:, c0:c0 + size].astype(F32)
            sig = jax.nn.sigmoid(gt)
            dup_ref[:, c0:c0 + size] = (dact * (gt * sig)).astype(BF16)
            dgate_ref[:, c0:c0 + size] = (dact * u * (sig * (1.0 + gt * (1.0 - sig)))).astype(BF16)
        dh = _dot(dgate_ref[...], wg_ref[...]) + _dot(dup_ref[...], wu_ref[...])
        dxn, dg = _norm_bwd(dh, x_ref[...], g_ref[...])
        dx_ref[...] = dy_ref[...] + dxn
        dg_ref[...] += dg
        _hook(plan, cr, "finish", i == ni - 1)

    io = _carried(
        plan,
        [pl.BlockSpec((tm, D), lambda i: (i, 0)), pl.BlockSpec((tm, D), lambda i: (i, 0)),
         pl.BlockSpec((1, D), lambda i: (0, 0)),
         pl.BlockSpec((tm, F), lambda i: (i, 0)), pl.BlockSpec((tm, F), lambda i: (i, 0)),
         ANY_SPEC, ANY_SPEC, ANY_SPEC],
        [pl.BlockSpec((tm, D), lambda i: (i, 0)), pl.BlockSpec((tm, D), lambda i: (i, 0)),
         pl.BlockSpec((tm, F), lambda i: (i, 0)), pl.BlockSpec((tm, F), lambda i: (i, 0)),
         pl.BlockSpec((1, D), lambda i: (0, 0))],
        [jax.ShapeDtypeStruct((T, D), F32), jax.ShapeDtypeStruct((T, D), BF16),
         jax.ShapeDtypeStruct((T, F), BF16), jax.ShapeDtypeStruct((T, F), BF16), jax.ShapeDtypeStruct((1, D), F32)],
        [pltpu.VMEM((F, D), BF16), pltpu.VMEM((F, D), BF16), pltpu.VMEM((F, D), BF16),
         pltpu.SemaphoreType.DMA((3,))])
    return _pcall(
        body, name=name, grid=(ni,), compiler_params=_params(1, BWD_VMEM_LIMIT), **io,
    )(dy, x, g, gate, up, wgt, wut, wd, *(plan.arrays if plan else ()))


def _matmul_tn(a, b, row_split, name, plan=None):
    T, n1 = a.shape
    n2 = b.shape[1]
    tn = n1 // row_split
    tk = min(T, DW_TOKEN_TILE)
    nk = T // tk

    def body(*refs):
        (a_ref, b_ref), (o_ref, ob_ref), _, cr = _unpack(refs, 2, 2, plan)
        j = pl.program_id(0)
        k = pl.program_id(1)
        _hook(plan, cr, "start", jnp.logical_and(j == 0, k == 0))

        @pl.when(k == 0)
        def _():
            o_ref[...] = jnp.zeros_like(o_ref)

        o_ref[...] += _dot_tn(a_ref[...], b_ref[...])

        @pl.when(k == nk - 1)
        def _():
            ob_ref[...] = o_ref[...].astype(BF16)

        _hook(plan, cr, "finish", jnp.logical_and(j == row_split - 1, k == nk - 1))

    io = _carried(
        plan,
        [pl.BlockSpec((tk, tn), lambda j, k: (k, j)), pl.BlockSpec((tk, n2), lambda j, k: (k, 0))],
        [pl.BlockSpec((tn, n2), lambda j, k: (j, 0)), pl.BlockSpec((tn, n2), lambda j, k: (j, 0))],
        [jax.ShapeDtypeStruct((n1, n2), F32), jax.ShapeDtypeStruct((n1, n2), BF16)], [])
    return _pcall(
        body, name=name, grid=(row_split, nk), compiler_params=_params(2), **io,
    )(a, b, *(plan.arrays if plan else ()))


def _norm_matmul(x, g, wt, name):
    T, D = x.shape
    n = wt.shape[0]
    tm = min(T, PROJ_TOKEN_TILE)

    def body(x_ref, g_ref, w_ref, z_ref, h_ref):
        xv = x_ref[...]
        h = ((xv * _rms_inv(xv)) * g_ref[...]).astype(BF16)
        h_ref[...] = h
        z_ref[...] = _dot_nt(h, w_ref[...])

    return _pcall(
        body, name=name, grid=(T // tm,),
        in_specs=[pl.BlockSpec((tm, D), lambda i: (i, 0)), pl.BlockSpec((1, D), lambda i: (0, 0)),
                  pl.BlockSpec((n, D), lambda i: (0, 0))],
        out_specs=[pl.BlockSpec((tm, n), lambda i: (i, 0)), pl.BlockSpec((tm, D), lambda i: (i, 0))],
        out_shape=[jax.ShapeDtypeStruct((T, n), F32), jax.ShapeDtypeStruct((T, D), BF16)],
        compiler_params=_params(1),
    )(x, g, wt)


def _matmul_residual(y, w, x, name):
    T, D = x.shape
    kdim = y.shape[1]
    tm = min(T, PROJ_TOKEN_TILE)

    def body(y_ref, w_ref, x_ref, o_ref):
        o_ref[...] = x_ref[...] + _dot(y_ref[...], w_ref[...])

    return _pcall(
        body, name=name, grid=(T // tm,),
        in_specs=[pl.BlockSpec((tm, kdim), lambda i: (i, 0)), pl.BlockSpec((kdim, D), lambda i: (0, 0)),
                  pl.BlockSpec((tm, D), lambda i: (i, 0))],
        out_specs=pl.BlockSpec((tm, D), lambda i: (i, 0)),
        out_shape=jax.ShapeDtypeStruct((T, D), F32),
        compiler_params=_params(1),
    )(y, w, x)


def _matmul_nt(dx, w, name, plan=None):
    T, D = dx.shape
    kdim = w.shape[0]
    tm = min(T, PROJ_TOKEN_TILE)
    ni = T // tm

    def body(*refs):
        (dx_ref, w_ref), (dy_ref, dxb_ref), _, cr = _unpack(refs, 2, 2, plan)
        i = pl.program_id(0)
        _hook(plan, cr, "start", i == 0)
        dxb = dx_ref[...].astype(BF16)
        dxb_ref[...] = dxb
        dy_ref[...] = _dot_nt(dxb, w_ref[...])
        _hook(plan, cr, "finish", i == ni - 1)

    io = _carried(
        plan,
        [pl.BlockSpec((tm, D), lambda i: (i, 0)), pl.BlockSpec((kdim, D), lambda i: (0, 0))],
        [pl.BlockSpec((tm, kdim), lambda i: (i, 0)), pl.BlockSpec((tm, D), lambda i: (i, 0))],
        [jax.ShapeDtypeStruct((T, kdim), F32), jax.ShapeDtypeStruct((T, D), BF16)], [])
    return _pcall(
        body, name=name, grid=(ni,), compiler_params=_params(1), **io,
    )(dx, w, *(plan.arrays if plan else ()))


def _matmul_norm_bwd(dz, wt, x, g, dres, name, plan=None):
    T, D = x.shape
    n = dz.shape[1]
    tm = min(T, PROJ_TOKEN_TILE)
    ni = T // tm

    def body(*refs):
        (dz_ref, w_ref, x_ref, g_ref, dres_ref), (dx_ref, dg_ref), _, cr = _unpack(refs, 5, 2, plan)
        i = pl.program_id(0)
        _hook(plan, cr, "start", i == 0)

        @pl.when(i == 0)
        def _():
            dg_ref[...] = jnp.zeros_like(dg_ref)

        dh = _dot(dz_ref[...], w_ref[...])
        dxn, dg = _norm_bwd(dh, x_ref[...], g_ref[...])
        dx_ref[...] = dres_ref[...] + dxn
        dg_ref[...] += dg
        _hook(plan, cr, "finish", i == ni - 1)

    io = _carried(
        plan,
        [pl.BlockSpec((tm, n), lambda i: (i, 0)), pl.BlockSpec((n, D), lambda i: (0, 0)),
         pl.BlockSpec((tm, D), lambda i: (i, 0)), pl.BlockSpec((1, D), lambda i: (0, 0)),
         pl.BlockSpec((tm, D), lambda i: (i, 0))],
        [pl.BlockSpec((tm, D), lambda i: (i, 0)), pl.BlockSpec((1, D), lambda i: (0, 0))],
        [jax.ShapeDtypeStruct((T, D), F32), jax.ShapeDtypeStruct((1, D), F32)], [])
    return _pcall(
        body, name=name, grid=(ni,), compiler_params=_params(1), **io,
    )(dz, wt, x, g, dres, *(plan.arrays if plan else ()))


Z_Q = 3 * CONV_WIDTH
Z_K = Z_Q + N_Q_HEADS * HEAD_DIM
Z_V = Z_K + LANES
Z_END = Z_V + LANES


def _rope_tables(T):
    half = ROT_DIM // 2
    inv_freq = ROPE_THETA ** (-jnp.arange(0, ROT_DIM, 2, dtype=F32) / ROT_DIM)
    ang = inv_freq[:, None] * jnp.arange(T, dtype=F32)[None, :]
    cos_sin = jnp.concatenate([jnp.cos(ang), jnp.sin(ang)], axis=0)
    select = np.zeros((2 * half, 3 * LANES), np.float32)
    const = np.zeros((1, 3 * LANES), np.float32)
    for lane in range(LANES):
        d = lane % HEAD_DIM
        if d < half:
            select[d, lane] = 1.0
            select[half + d, LANES + lane] = -1.0
        elif d < ROT_DIM:
            select[d - half, lane] = 1.0
            select[d, 2 * LANES + lane] = 1.0
        else:
            const[0, lane] = 1.0
    tab = lax.dot_general(cos_sin, jnp.asarray(select), (((0,), (0,)), ((), ())),
                          precision=lax.Precision.HIGHEST, preferred_element_type=F32)
    return tab + jnp.asarray(const)


def _tab3(tab):
    return tab[:, 0:LANES], tab[:, LANES:2 * LANES], tab[:, 2 * LANES:3 * LANES]


def _rot(x, tab):
    c, s1, s2 = _tab3(tab)
    return x * c + pltpu.roll(x, LANES - ROT_DIM // 2, 1) * s1 + pltpu.roll(x, ROT_DIM // 2, 1) * s2


def _rot_t(d, tab):
    c, s1, s2 = _tab3(tab)
    return d * c + pltpu.roll(d * s1, ROT_DIM // 2, 1) + pltpu.roll(d * s2, LANES - ROT_DIM // 2, 1)


def _head_pads(a):
    lo = lax.broadcasted_iota(jnp.int32, a.shape, 1) < HEAD_DIM
    nat0 = jnp.where(lo, a, 0.0)
    nat1 = jnp.where(lo, 0.0, a)
    return {
        (0, 0): nat0.astype(BF16), (0, 1): pltpu.roll(nat0, HEAD_DIM, 1).astype(BF16),
        (1, 0): pltpu.roll(nat1, HEAD_DIM, 1).astype(BF16), (1, 1): nat1.astype(BF16),
    }


def _from_pads(even, odd, kv):
    lo = lax.broadcasted_iota(jnp.int32, even.shape, 1) < HEAD_DIM
    if kv == 0:
        return jnp.where(lo, even + pltpu.roll(odd, HEAD_DIM, 1), 0.0)
    return jnp.where(lo, 0.0, pltpu.roll(even, HEAD_DIM, 1) + odd)


N_GROUPS = 4


def _group_head(g, r):
    kv, par = divmod(g, 2)
    return 2 * (2 * kv + r) + par


def _window_mask_t(has_prev):
    jj = lax.broadcasted_iota(jnp.int32, (2 * BLOCK, 2 * BLOCK), 0)
    ii = lax.broadcasted_iota(jnp.int32, (2 * BLOCK, 2 * BLOCK), 1) & (BLOCK - 1)
    rel = jj - BLOCK - ii
    return (rel <= 0) & (rel > -BLOCK) & ((jj >= BLOCK) | has_prev)


def _sink_row(sink_ref, g):
    lane = lax.broadcasted_iota(jnp.int32, (1, 2 * BLOCK), 1)
    return jnp.where(lane < BLOCK, sink_ref[0, _group_head(g, 0)], sink_ref[0, _group_head(g, 1)])


def _attn_probs_t(q2, kp, mask, sink_ref):
    out = []
    for kv in range(2):
        q_st = jnp.concatenate([q2[2 * kv], q2[2 * kv + 1]], axis=0)
        for par in range(2):
            s = jnp.where(mask, _dot_nt(kp[(kv, par)], q_st), MASK_VALUE)
            sink = _sink_row(sink_ref, 2 * kv + par)
            m = jnp.maximum(jnp.max(s, axis=0, keepdims=True), sink)
            p = jnp.exp(s - m)
            esink = jnp.exp(sink - m)
            rden = 1.0 / (jnp.sum(p, axis=0, keepdims=True) + esink)
            out.append((p * rden, esink * rden))
    return out


def _conv_taps(cg, u, cg_prev, u_prev, has_prev):
    vv = cg * u
    halo = jnp.where(has_prev, cg_prev * u_prev, 0.0)
    ext = jnp.concatenate([halo, vv], axis=0)
    rows = ext.shape[0]
    vv1 = pltpu.roll(ext, 1, 0)[8:rows]
    vv2 = pltpu.roll(ext, 2, 0)[8:rows]
    return vv, vv1, vv2


def _mix_specs(nb):
    cur = lambda n: jnp.minimum(n, nb - 1)
    prev = lambda n: jnp.maximum(jnp.minimum(n, nb - 1) - 1, 0)
    rows8_prev = lambda n: jnp.maximum(16 * jnp.minimum(n, nb - 1) - 1, 0)
    return cur, prev, [
        pl.BlockSpec((BLOCK, Z_END), lambda n: (cur(n), 0)),
        pl.BlockSpec((BLOCK, 2 * LANES), lambda n: (prev(n), Z_K // (2 * LANES))),
        pl.BlockSpec((8, CONV_WIDTH), lambda n: (rows8_prev(n), 1)),
        pl.BlockSpec((8, CONV_WIDTH), lambda n: (rows8_prev(n), 2)),
        pl.BlockSpec((BLOCK, 3 * LANES), lambda n: (cur(n), 0)),
        pl.BlockSpec((BLOCK, 3 * LANES), lambda n: (prev(n), 0)),
        pl.BlockSpec((3, CONV_WIDTH), lambda n: (0, 0)),
        pl.BlockSpec(memory_space=pltpu.SMEM),
    ]


def _mix_core_fwd(z, tab, conv_w, sinks, name):
    T = z.shape[0]
    nb = T // BLOCK
    _, _, specs = _mix_specs(nb)

    def body(z_ref, zkvp_ref, cgp_ref, up_ref, tab_ref, tabp_ref, cw_ref, sink_ref, y_ref):
        has_prev = pl.program_id(0) > 0
        bg = z_ref[:, 0:CONV_WIDTH]
        vv, vv1, vv2 = _conv_taps(z_ref[:, CONV_WIDTH:2 * CONV_WIDTH], z_ref[:, 2 * CONV_WIDTH:Z_Q],
                                  cgp_ref[...], up_ref[...], has_prev)
        conv = cw_ref[0:1, :] * vv2 + cw_ref[1:2, :] * vv1 + cw_ref[2:3, :] * vv
        y_ref[:, 0:CONV_WIDTH] = (bg * conv).astype(BF16)

        tab_c = tab_ref[...]
        tab_p = tabp_ref[...]
        k_all = jnp.concatenate([_rot(zkvp_ref[:, 0:LANES], tab_p), _rot(z_ref[:, Z_K:Z_V], tab_c)], axis=0)
        v_all = jnp.concatenate([zkvp_ref[:, LANES:2 * LANES], z_ref[:, Z_V:Z_END]], axis=0)
        kp = _head_pads(k_all)
        vp = _head_pads(v_all)
        q2 = [(_rot(z_ref[:, Z_Q + LANES * c:Z_Q + LANES * (c + 1)], tab_c) * ATTN_SCALE).astype(BF16)
              for c in range(N_Q_HEADS // 2)]
        probs = _attn_probs_t(q2, kp, _window_mask_t(has_prev), sink_ref)
        for kv in range(2):
            o_t = (_dot_tn(vp[(kv, 0)], probs[2 * kv][0].astype(BF16))
                   + _dot_tn(vp[(kv, 1)], probs[2 * kv + 1][0].astype(BF16)))
            for r in range(2):
                c = 2 * kv + r
                y_ref[:, CONV_WIDTH + LANES * c:CONV_WIDTH + LANES * (c + 1)] = o_t[:, BLOCK * r:BLOCK * (r + 1)].T.astype(BF16)

    return _pcall(
        body, name=name, grid=(nb,), in_specs=specs,
        out_specs=pl.BlockSpec((BLOCK, 2 * CONV_WIDTH), lambda n: (n, 0)),
        out_shape=jax.ShapeDtypeStruct((T, 2 * CONV_WIDTH), BF16),
        compiler_params=_params(1),
    )(z, z, z, z, tab, tab, conv_w, sinks)


def _mix_core_bwd(z, dy, tab, conv_w, sinks, name, plan=None):
    T = z.shape[0]
    nb = T // BLOCK
    cur, _, specs = _mix_specs(nb)
    rows8_next = lambda n: jnp.minimum(16 * (cur(n) + 1), 16 * nb - 1)
    specs = specs[:4] + [
        pl.BlockSpec((8, CONV_WIDTH), lambda n: (rows8_next(n), 0)),
        pl.BlockSpec((BLOCK, 2 * CONV_WIDTH), lambda n: (cur(n), 0)),
        pl.BlockSpec((8, CONV_WIDTH), lambda n: (rows8_next(n), 0)),
    ] + specs[4:]

    def body(*refs):
        ins, outs, scratch, cr = _unpack(refs, 11, 3, plan)
        z_ref, zkvp_ref, cgp_ref, up_ref, bgn_ref, dy_ref, dyn_ref, tab_ref, tabp_ref, cw_ref, sink_ref = ins
        dz_ref, dcw_ref, dsk_ref = outs
        main_ref, kv_ref = scratch
        n = pl.program_id(0)
        _hook(plan, cr, "start", n == 0)

        @pl.when(n == 0)
        def _():
            main_ref[...] = jnp.zeros_like(main_ref)
            kv_ref[...] = jnp.zeros_like(kv_ref)
            dcw_ref[...] = jnp.zeros_like(dcw_ref)
            dsk_ref[...] = jnp.zeros_like(dsk_ref)

        @pl.when(n < nb)
        def _():
            has_prev = n > 0
            has_next = n < nb - 1
            bg = z_ref[:, 0:CONV_WIDTH]
            cg = z_ref[:, CONV_WIDTH:2 * CONV_WIDTH]
            u = z_ref[:, 2 * CONV_WIDTH:Z_Q]
            vv, vv1, vv2 = _conv_taps(cg, u, cgp_ref[...], up_ref[...], has_prev)
            w0, w1, w2 = cw_ref[0:1, :], cw_ref[1:2, :], cw_ref[2:3, :]
            dyc = dy_ref[:, 0:CONV_WIDTH]
            dbg = dyc * (w0 * vv2 + w1 * vv1 + w2 * vv)
            dconv = dyc * bg
            dconv_next = jnp.where(has_next, dyn_ref[...] * bgn_ref[...], 0.0)
            ext = jnp.concatenate([dconv, dconv_next], axis=0)
            rows = ext.shape[0]
            dvv = w2 * dconv + w1 * pltpu.roll(ext, rows - 1, 0)[0:BLOCK] + w0 * pltpu.roll(ext, rows - 2, 0)[0:BLOCK]
            dcw_ref[0:1, :] += jnp.sum(dconv * vv2, axis=0, keepdims=True)
            dcw_ref[1:2, :] += jnp.sum(dconv * vv1, axis=0, keepdims=True)
            dcw_ref[2:3, :] += jnp.sum(dconv * vv, axis=0, keepdims=True)

            tab_c = tab_ref[...]
            tab_p = tabp_ref[...]
            k_all = jnp.concatenate([_rot(zkvp_ref[:, 0:LANES], tab_p), _rot(z_ref[:, Z_K:Z_V], tab_c)], axis=0)
            v_all = jnp.concatenate([zkvp_ref[:, LANES:2 * LANES], z_ref[:, Z_V:Z_END]], axis=0)
            kp = _head_pads(k_all)
            vp = _head_pads(v_all)
            chunks = range(N_Q_HEADS // 2)
            q2 = [(_rot(z_ref[:, Z_Q + LANES * c:Z_Q + LANES * (c + 1)], tab_c) * ATTN_SCALE).astype(BF16) for c in chunks]
            do2 = [dy_ref[:, CONV_WIDTH + LANES * c:CONV_WIDTH + LANES * (c + 1)].astype(BF16) for c in chunks]
            probs = _attn_probs_t(q2, kp, _window_mask_t(has_prev), sink_ref)
            dq_chunks = []
            dk_nat = jnp.zeros((2 * BLOCK, LANES), F32)
            dv_nat = jnp.zeros((2 * BLOCK, LANES), F32)
            for kv in range(2):
                q_st = jnp.concatenate([q2[2 * kv], q2[2 * kv + 1]], axis=0)
                do_st = jnp.concatenate([do2[2 * kv], do2[2 * kv + 1]], axis=0)
                dq_t = jnp.zeros((LANES, 2 * BLOCK), F32)
                dk_par, dv_par = [], []
                for par in range(2):
                    g = 2 * kv + par
                    pr, psink = probs[g]
                    dp = _dot_nt(vp[(kv, par)], do_st)
                    delta = jnp.sum(dp * pr, axis=0, keepdims=True)
                    ds = (pr * (dp - delta)).astype(BF16)
                    dsink = -psink * delta
                    for r in range(2):
                        h = _group_head(g, r)
                        dsk_ref[h:h + 1, :] += jnp.sum(dsink[:, BLOCK * r:BLOCK * (r + 1)])
                    dq_t = dq_t + _dot_tn(kp[(kv, par)], ds)
                    dk_par.append(_dot(ds, q_st))
                    dv_par.append(_dot(pr.astype(BF16), do_st))
                for r in range(2):
                    dq_chunks.append(_rot_t(dq_t[:, BLOCK * r:BLOCK * (r + 1)].T * ATTN_SCALE, tab_c))
                dk_nat = dk_nat + _from_pads(dk_par[0], dk_par[1], kv)
                dv_nat = dv_nat + _from_pads(dv_par[0], dv_par[1], kv)

            dk_prev = _rot_t(kv_ref[:, 0:LANES] + dk_nat[0:BLOCK], tab_p)
            dv_prev = kv_ref[:, LANES:2 * LANES] + dv_nat[0:BLOCK]
            dz_ref[:, 0:Z_K] = main_ref[...]
            dz_ref[:, Z_K:Z_V] = dk_prev.astype(BF16)
            dz_ref[:, Z_V:Z_END] = dv_prev.astype(BF16)
            main_ref[:, 0:CONV_WIDTH] = dbg.astype(BF16)
            main_ref[:, CONV_WIDTH:2 * CONV_WIDTH] = (dvv * u).astype(BF16)
            main_ref[:, 2 * CONV_WIDTH:Z_Q] = (dvv * cg).astype(BF16)
            for c in range(N_Q_HEADS // 2):
                main_ref[:, Z_Q + LANES * c:Z_Q + LANES * (c + 1)] = dq_chunks[c].astype(BF16)
            kv_ref[:, 0:LANES] = dk_nat[BLOCK:2 * BLOCK]
            kv_ref[:, LANES:2 * LANES] = dv_nat[BLOCK:2 * BLOCK]

        @pl.when(n == nb)
        def _():
            dz_ref[:, 0:Z_K] = main_ref[...]
            dz_ref[:, Z_K:Z_V] = _rot_t(kv_ref[:, 0:LANES], tab_ref[...]).astype(BF16)
            dz_ref[:, Z_V:Z_END] = kv_ref[:, LANES:2 * LANES].astype(BF16)

        _hook(plan, cr, "finish", n == nb)

    io = _carried(
        plan, specs,
        [pl.BlockSpec((BLOCK, Z_END), lambda n: (jnp.maximum(n - 1, 0), 0)),
         pl.BlockSpec((8, CONV_WIDTH), lambda n: (0, 0)), pl.BlockSpec((8, LANES), lambda n: (0, 0))],
        [jax.ShapeDtypeStruct((T, Z_END), BF16), jax.ShapeDtypeStruct((8, CONV_WIDTH), F32),
         jax.ShapeDtypeStruct((8, LANES), F32)],
        [pltpu.VMEM((BLOCK, Z_K), BF16), pltpu.VMEM((BLOCK, 2 * LANES), F32)])
    return _pcall(
        body, name=name, grid=(nb + 1,), compiler_params=_params(1), **io,
    )(z, z, z, z, z, dy, dy, tab, tab, conv_w, sinks, *(plan.arrays if plan else ()))


ROW_SPLIT = 1


def _pair_sum(grads, recvd, core, name):
    n = len(grads)

    def body(core_ref, *refs):
        g, r = refs[:n], refs[n:2 * n]
        s, sb = refs[2 * n:3 * n], refs[3 * n:]
        for t in range(n):
            tot = g[t][...] + r[t][...].astype(F32)
            s[t][...] = tot
            sb[t][...] = tot.astype(BF16)

    def blk(a):
        return (1, a.shape[1] // ROW_SPLIT, a.shape[2])

    in_specs = [pl.BlockSpec(blk(r), lambda q, i, core_ref: (q, core_ref[0] * ROW_SPLIT + i, 0)) for r in recvd]
    in_specs += [pl.BlockSpec(blk(r), lambda q, i, core_ref: (q, i, 0)) for r in recvd]
    out_specs = [pl.BlockSpec(blk(r), lambda q, i, core_ref: (q, i, 0)) for r in recvd] * 2
    return _pcall(
        body, name=name,
        grid_spec=pltpu.PrefetchScalarGridSpec(num_scalar_prefetch=1, grid=(N_CHIPS, ROW_SPLIT),
                                               in_specs=in_specs, out_specs=out_specs),
        out_shape=[jax.ShapeDtypeStruct(r.shape, F32) for r in recvd] + [jax.ShapeDtypeStruct(r.shape, BF16) for r in recvd],
        compiler_params=_params(2),
    )(core, *grads, *recvd)


def _chip_sum(parts, recvd, place, name):
    n = len(parts)

    def body(place_ref, *refs):
        p, r, o = refs[:n], refs[n:2 * n], refs[2 * n:]
        for t in range(n):
            tot = p[t][0]
            for j in range(3):
                tot = tot + r[t][j].astype(F32)
            o[t][...] = tot

    in_specs = [pl.BlockSpec((1, p.shape[1] // ROW_SPLIT, p.shape[2]), lambda i, place_ref: (place_ref[0], i, 0))
                for p in parts]
    in_specs += [pl.BlockSpec((3, r.shape[1] // ROW_SPLIT, r.shape[2]), lambda i, place_ref: (0, i, 0)) for r in recvd]
    out_specs = [pl.BlockSpec((p.shape[1] // ROW_SPLIT, p.shape[2]),
                              lambda i, place_ref: (place_ref[1] * ROW_SPLIT + i, 0)) for p in parts]
    return _pcall(
        body, name=name,
        grid_spec=pltpu.PrefetchScalarGridSpec(num_scalar_prefetch=1, grid=(ROW_SPLIT,),
                                               in_specs=in_specs, out_specs=out_specs),
        out_shape=[jax.ShapeDtypeStruct((2 * p.shape[1], p.shape[2]), F32) for p in parts],
        compiler_params=_params(1),
    )(place, *parts, *recvd)


def _adamw_math(w, g, m, v):
    m = ADAM_B1 * m + (1.0 - ADAM_B1) * g
    v = ADAM_B2 * v + (1.0 - ADAM_B2) * (g * g)
    m_hat = m / (1.0 - ADAM_B1 ** ADAM_STEP)
    v_hat = v / (1.0 - ADAM_B2 ** ADAM_STEP)
    delta = -ADAM_LR * (m_hat / (jnp.sqrt(v_hat) + ADAM_EPS) + ADAM_WD * w)
    return delta, m, v


def _adamw(ws, gs, ms, vs, row_blocks, name):
    n = len(ws)

    def body(*refs):
        w, g, m, v = refs[:n], refs[n:2 * n], refs[2 * n:3 * n], refs[3 * n:4 * n]
        d, mo, vo, go = refs[4 * n:5 * n], refs[5 * n:6 * n], refs[6 * n:7 * n], refs[7 * n:]
        for t in range(n):
            gv = g[t][...]
            delta, m_new, v_new = _adamw_math(w[t][...], gv, m[t][...], v[t][...])
            d[t][...] = delta
            mo[t][...] = m_new
            vo[t][...] = v_new
            go[t][...] = gv

    specs = [pl.BlockSpec((a.shape[0] // row_blocks, a.shape[1]), lambda i: (i, 0)) for a in ws]
    shapes = [jax.ShapeDtypeStruct(a.shape, F32) for a in ws]
    return _pcall(
        body, name=name, grid=(row_blocks,), in_specs=specs * 4, out_specs=specs * 4, out_shape=shapes * 4,
        compiler_params=_params(1),
    )(*ws, *gs, *ms, *vs)


def kernel(x, ffn1_norm, ffn1_w_gate, ffn1_w_up, ffn1_w_down, mix_norm, w_in, conv_w, attn_sinks, w_out, ffn2_norm, ffn2_w_gate, ffn2_w_up, ffn2_w_down, final_norm, loss_target, m_ffn1_norm, m_ffn1_w_gate, m_ffn1_w_up, m_ffn1_w_down, m_mix_norm, m_w_in, m_conv_w, m_attn_sinks, m_w_out, m_ffn2_norm, m_ffn2_w_gate, m_ffn2_w_up, m_ffn2_w_down, m_final_norm, v_ffn1_norm, v_ffn1_w_gate, v_ffn1_w_up, v_ffn1_w_down, v_mix_norm, v_w_in, v_conv_w, v_attn_sinks, v_w_out, v_ffn2_norm, v_ffn2_w_gate, v_ffn2_w_up, v_ffn2_w_down, v_final_norm):
    T, D = x.shape[1], x.shape[2]
    chip = (2 * lax.axis_index("x") + lax.axis_index("y")).astype(jnp.int32)
    core = lax.axis_index("c").astype(jnp.int32)
    place = jnp.stack([chip, core])
    x0 = x[0]
    target = loss_target[0]
    gf = final_norm.reshape(1, D)

    tr = lambda w: jnp.swapaxes(w[0], 0, 1)
    big = [tr(ffn1_w_gate), tr(ffn1_w_up), ffn1_w_down[0], tr(w_in), w_out[0], tr(ffn2_w_gate), tr(ffn2_w_up), ffn2_w_down[0]]
    transposed = [True, True, False, True, False, True, True, False]
    own_b = [w.astype(BF16) for w in big]

    def whole(gathered, own):
        return lax.dynamic_update_slice(gathered, own[None], (chip, 0, 0)).reshape(-1, D)

    got1 = _run_comm(_gather_plan(own_b[0:3]), "gather_ffn1")
    wg1, wu1, wd1 = (whole(g, o) for g, o in zip(got1, own_b[0:3]))
    tab = _rope_tables(T)

    res = _ffn_fwd(x0, ffn1_norm, wg1, wu1, wd1, "ffn1_fwd", _gather_plan(own_b[3:8], [conv_w[0]]))
    x1, h1, gate1, up1, act1 = res[:5]
    win, wout, wg2, wu2, wd2 = (whole(g, o) for g, o in zip(res[5:10], own_b[3:8]))
    convw4 = lax.dynamic_update_slice(res[10], conv_w, (chip, 0, 0))
    convw = jnp.transpose(convw4, (1, 0, 2)).reshape(3, -1)
    z, hm = _norm_matmul(x1, mix_norm, win, "mix_in_fwd")
    ymix = _mix_core_fwd(z, tab, convw, attn_sinks, "mix_core_fwd")
    x2 = _matmul_residual(ymix, wout, x1, "mix_out_fwd")
    dx3, h2, gate2, up2, act2, dgf, loss_part = _ffn_fwd(x2, ffn2_norm, wg2, wu2, wd2, "ffn2_fwd", head=(gf, target))

    dx2, dyb2, dgate2, dup2, dg2 = _ffn_bwd(dx3, x2, ffn2_norm, gate2, up2, wg2, wu2, wd2, "ffn2_bwd")
    dymix, dx2b = _matmul_nt(dx2, wout, "mix_out_bwd")
    dz, dcw, dsk = _mix_core_bwd(z, dymix, tab, convw, attn_sinks, "mix_core_bwd")
    dx1, dgm = _matmul_norm_bwd(dz, win, x1, mix_norm, dx2, "mix_in_bwd")
    dx0, dyb1, dgate1, dup1, dg1 = _ffn_bwd(dx1, x0, ffn1_norm, gate1, up1, wg1, wu1, wd1, "ffn1_bwd")

    pad = lambda a: jnp.pad(a, ((0, 0), (0, LANES - a.shape[1])))
    vec = jnp.concatenate([dg1, dgm, dg2, dgf, dcw[0:3].reshape(1, -1), pad(dsk[:, 0].reshape(1, -1)),
                           pad(loss_part[:, 0:1])], axis=1)

    jobs = [("ffn2_dwg", dgate2, h2, 5), ("ffn2_dwu", dup2, h2, 6), ("ffn2_dwd", act2, dyb2, 7),
            ("ffn1_dwg", dgate1, h1, 0), ("ffn1_dwu", dup1, h1, 1), ("ffn1_dwd", act1, dyb1, 2),
            ("mix_dwin", dz, hm, 3), ("mix_dwout", ymix, dx2b, 4)]
    n_jobs = len(jobs)
    grad, grad_b, from_sib, pair_f, pair_b, from_chips, half, g_big = ({} for _ in range(8))

    def stage_plans(t):
        plans, takers = [], []
        if 0 <= t - 1 < n_jobs:
            plans.append(_sibling_plan([grad_b[t - 1]]))
            takers.append((from_sib, t - 1))
        if 0 <= t - 2 < n_jobs:
            plans.append(_scatter_plan([pair_b[t - 2]]))
            takers.append((from_chips, t - 2))
        if 0 <= t - 3 < n_jobs:
            plans.append(_join_plan([half[t - 3]]))
            takers.append((g_big, jobs[t - 3][3]))
        return plans, takers

    def after_stage(t, landed, takers):
        for (store, key), arr in zip(takers, landed):
            store[key] = arr
        if 0 <= t - 1 < n_jobs:
            pair_f[t - 1], pair_b[t - 1] = _pair_sum([grad[t - 1]], [from_sib[t - 1]], core.reshape(1), f"pair_sum_{t - 1}")
        if 0 <= t - 2 < n_jobs:
            half[t - 2], = _chip_sum([pair_f[t - 2]], [from_chips[t - 2]], place, f"chip_sum_{t - 2}")

    for t, (name_, a, b, _) in enumerate(jobs):
        plans, takers = stage_plans(t)
        if t == 0:
            plans.append(_all_gather_plan(jnp.pad(vec, ((0, 7), (0, 0)))))
        res = _matmul_tn(a, b, DW_ROW_SPLIT, name_, _merge_plans(plans))
        grad[t], grad_b[t] = (r.reshape(N_CHIPS, -1, D) for r in res[:2])
        landed = list(res[2:])
        if t == 0:
            vec_blocks = landed.pop()
        after_stage(t, landed, takers)

    ws = big
    ms = [tr(m_ffn1_w_gate), tr(m_ffn1_w_up), m_ffn1_w_down[0], tr(m_w_in), m_w_out[0], tr(m_ffn2_w_gate), tr(m_ffn2_w_up), m_ffn2_w_down[0]]
    vs = [tr(v_ffn1_w_gate), tr(v_ffn1_w_up), v_ffn1_w_down[0], tr(v_w_in), v_w_out[0], tr(v_ffn2_w_gate), tr(v_ffn2_w_up), v_ffn2_w_down[0]]
    for t in range(n_jobs, n_jobs + 3):
        plans, takers = stage_plans(t)
        after_stage(t, _run_comm(_merge_plans(plans), f"grads_tail_{t - n_jobs}"), takers)
    upd = {}
    for name_, idx in (("adamw_a", [0, 1, 2, 4]), ("adamw_b", [3, 5, 6, 7])):
        k = len(idx)
        res = _adamw([ws[i] for i in idx], [g_big[i] for i in idx], [ms[i] for i in idx], [vs[i] for i in idx], 8, name_)
        for j, i in enumerate(idx):
            upd[i] = (res[j], res[k + j], res[2 * k + j])
            g_big[i] = res[3 * k + j]

    total = _sum_devices(vec_blocks, "small_sum")[0:1]
    g_n1, g_nm, g_n2, g_nf = (total[:, k * D:(k + 1) * D] for k in range(4))
    cw_full = total[:, 4 * D:4 * D + 3 * CONV_WIDTH].reshape(3, CONV_WIDTH)
    cq = CONV_WIDTH // N_CHIPS
    g_cw = lax.dynamic_slice(cw_full, (0, chip * cq), (3, cq))
    off = 4 * D + 3 * CONV_WIDTH
    g_sk = total[:, off:off + N_Q_HEADS]
    loss = total[0, off + LANES]

    sw = [ffn1_norm, mix_norm, conv_w[0], attn_sinks, ffn2_norm, gf]
    sg = [g_n1, g_nm, g_cw, g_sk, g_n2, g_nf]
    sm = [m_ffn1_norm, m_mix_norm, m_conv_w[0], m_attn_sinks, m_ffn2_norm, m_final_norm.reshape(1, D)]
    sv = [v_ffn1_norm, v_mix_norm, v_conv_w[0], v_attn_sinks, v_ffn2_norm, v_final_norm.reshape(1, D)]
    sres = _adamw(sw, sg, sm, sv, 1, "adamw_small")
    supd = [(sres[j], sres[6 + j], sres[12 + j]) for j in range(6)]

    order = [("s", 0), ("b", 0), ("b", 1), ("b", 2), ("s", 1), ("b", 3), ("s", 2), ("s", 3), ("b", 4),
             ("s", 4), ("b", 5), ("b", 6), ("b", 7), ("s", 5)]

    def leaf(kind, i, which):
        if kind == "b":
            a = g_big[i] if which == 0 else upd[i][which - 1]
            return (jnp.swapaxes(a, 0, 1) if transposed[i] else a)[None]
        a = sg[i] if which == 0 else supd[i][which - 1]
        if i == 2:
            return a[None]
        if i == 5:
            return a.reshape(D)
        return a

    outs = [loss, dx0[None]]
    for which in range(4):
        outs += [leaf(kind, i, which) for kind, i in order]
    return tuple(outs)
```

```python
import functools

import jax
import jax.numpy as jnp
import numpy as np
from jax import lax
from jax.experimental import pallas as pl
from jax.experimental.pallas import tpu as pltpu

F32 = jnp.float32
BF16 = jnp.bfloat16
MESH = pl.DeviceIdType.MESH

CONV_WIDTH = 512
N_Q_HEADS = 8
HEAD_DIM = 64
BLOCK = 128
ROPE_THETA = 500000.0
ROT_DIM = 16
RMS_EPS = 1e-5
MASK_VALUE = -1e30
ATTN_SCALE = HEAD_DIM ** -0.5
FFN_RES_SCALE = 0.5
ADAM_LR = 0.001
ADAM_B1 = 0.9
ADAM_B2 = 0.999
ADAM_EPS = 1e-08
ADAM_WD = 0.01
ADAM_STEP = 10

N_CHIPS = 4
N_DEV = 8
LANES = 128
VMEM_LIMIT = 56 * 1024 * 1024

_pcall = pl.pallas_call
HBM_SPEC = pl.BlockSpec(memory_space=pltpu.HBM)
ANY_SPEC = pl.BlockSpec(memory_space=pl.ANY)


def _params(n_axes, vmem=VMEM_LIMIT):
    return pltpu.CompilerParams(dimension_semantics=("arbitrary",) * n_axes, vmem_limit_bytes=vmem)


def _dot(a, b):
    return jnp.dot(a, b, preferred_element_type=F32)


def _dot_nt(a, b):
    return lax.dot_general(a, b, (((1,), (1,)), ((), ())), preferred_element_type=F32)


def _dot_tn(a, b):
    return lax.dot_general(a, b, (((0,), (0,)), ((), ())), preferred_element_type=F32)


def _rms_inv(x):
    return lax.rsqrt(jnp.mean(x * x, axis=-1, keepdims=True) + RMS_EPS)


def _norm_bwd(dh, x, g):
    inv = _rms_inv(x)
    xhat = x * inv
    dg = jnp.sum(dh * xhat, axis=0, keepdims=True)
    dxhat = dh * g
    dx = inv * (dxhat - xhat * jnp.mean(dxhat * xhat, axis=-1, keepdims=True))
    return dx, dg


def _place():
    x, y, c = lax.axis_index("x"), lax.axis_index("y"), lax.axis_index("c")
    chips = [(1 - x, y), (x, 1 - y), (1 - x, 1 - y)]
    return x, y, c, chips


class _Plan:
    def __init__(self, arrays, out_shapes, n_sems, start, finish, middle=None, aliases=None):
        self.arrays, self.out_shapes, self.n_sems = list(arrays), list(out_shapes), n_sems
        self.start, self.finish, self.middle = start, finish, middle
        self.aliases = dict(aliases or {})

    def specs(self):
        k = len(self.arrays)
        sems = [pltpu.SemaphoreType.DMA((self.n_sems,)), pltpu.SemaphoreType.DMA((self.n_sems,))]
        return [HBM_SPEC] * k, [HBM_SPEC] * len(self.out_shapes), self.out_shapes, sems


class _SemSlice:
    def __init__(self, ref, offset):
        self.ref, self.offset = ref, offset

    @property
    def at(self):
        return self

    def __getitem__(self, k):
        return self.ref.at[k + self.offset]


def _merge_plans(plans):
    plans = [p for p in plans if p is not None]
    if len(plans) <= 1:
        return plans[0] if plans else None
    arrays, shapes, aliases, spans, n_sems = [], [], {}, [], 0
    for p in plans:
        a0, o0 = len(arrays), len(shapes)
        spans.append((a0, a0 + len(p.arrays), o0, o0 + len(p.out_shapes), n_sems))
        aliases.update({a0 + i: o0 + j for i, j in p.aliases.items()})
        arrays += p.arrays
        shapes += p.out_shapes
        n_sems += p.n_sems

    def run(which):
        def fn(ins, outs, send_sems, recv_sems):
            for p, (a0, a1, o0, o1, s0) in zip(plans, spans):
                part = getattr(p, which)
                if part is not None:
                    part(ins[a0:a1], outs[o0:o1], _SemSlice(send_sems, s0), _SemSlice(recv_sems, s0))
        return fn

    middle = run("middle") if any(p.middle is not None for p in plans) else None
    return _Plan(arrays, shapes, n_sems, run("start"), run("finish"), middle, aliases)


def _sibling_plan(grads_b):
    n = len(grads_b)

    def copies(ins, outs, send_sems, recv_sems):
        x, y, c, _ = _place()

        def copy(t):
            half = ins[t].shape[1] // 2
            return pltpu.make_async_remote_copy(
                src_ref=ins[t].at[:, pl.ds(pl.multiple_of((1 - c) * half, 16), half), :], dst_ref=outs[t],
                send_sem=send_sems.at[t], recv_sem=recv_sems.at[t], device_id=(x, y, 1 - c), device_id_type=MESH)

        return [copy(t) for t in range(n)]

    def start(*refs):
        for cp in copies(*refs):
            cp.start()

    def finish(*refs):
        for cp in copies(*refs):
            cp.wait()

    shapes = [jax.ShapeDtypeStruct((g.shape[0], g.shape[1] // 2, g.shape[2]), g.dtype) for g in grads_b]
    return _Plan(grads_b, shapes, n, start, finish)


def _scatter_plan(parts_b):
    n = len(parts_b)

    def copies(ins, outs, send_sems, recv_sems):
        x, y, c, chips = _place()

        def copy(t, j):
            px, py = chips[j]
            return pltpu.make_async_remote_copy(
                src_ref=ins[t].at[2 * px + py], dst_ref=outs[t].at[j], send_sem=send_sems.at[3 * t + j],
                recv_sem=recv_sems.at[3 * t + j], device_id=(px, py, c), device_id_type=MESH)

        return [copy(t, j) for t in range(n) for j in range(3)]

    def start(*refs):
        for cp in copies(*refs):
            cp.start()

    def finish(*refs):
        for cp in copies(*refs):
            cp.wait()

    shapes = [jax.ShapeDtypeStruct((3, *p.shape[1:]), p.dtype) for p in parts_b]
    return _Plan(parts_b, shapes, 3 * n, start, finish)


def _gather_plan(shards, small=()):
    n, ns = len(shards), len(small)
    per = 8

    def parts(ins, outs, send_sems, recv_sems):
        x, y, c, chips = _place()
        me = 2 * x + y
        blocks = [2 * px + py for px, py in chips]

        def rows(t, core, piece=None):
            half = ins[t].shape[0] // 2
            if piece is None:
                return pl.ds(pl.multiple_of(core * half, 16), half)
            return pl.ds(pl.multiple_of(core * half + piece * (half // 2), 16), half // 2)

        def remote(src, dst, k, device):
            return pltpu.make_async_remote_copy(src_ref=src, dst_ref=dst, send_sem=send_sems.at[k],
                                                recv_sem=recv_sems.at[k], device_id=device, device_id_type=MESH)

        def first(t, j, block, core):
            return remote(ins[t].at[rows(t, core), :], outs[t].at[block, rows(t, core), :], per * t + j, (*chips[j], c))

        def relay(t, j, block, core):
            ref = outs[t].at[block, rows(t, core, j), :]
            return remote(ref, ref, per * t + 2 + j, (*chips[j], c))

        def passed(t, k, block, core, piece=None):
            ref = outs[t].at[block, rows(t, core, piece), :]
            return remote(ref, ref, per * t + 4 + k, (x, y, 1 - c))

        def whole(s, j, block):
            return remote(ins[n + s], outs[n + s].at[block], per * n + 3 * s + j, (*chips[j], c))

        return c, me, blocks, first, relay, passed, whole

    def start(*refs):
        c, me, _, first, _, _, whole = parts(*refs)
        for t in range(n):
            for j in range(2):
                first(t, j, me, c).start()
        for s in range(ns):
            for j in range(3):
                whole(s, j, me).start()

    def middle(*refs):
        c, _, blocks, first, relay, passed, _ = parts(*refs)
        for t in range(n):
            for j in range(2):
                first(t, j, blocks[j], c).wait_recv()
                passed(t, j, blocks[j], c).start()
                relay(t, 1 - j, blocks[j], c).start()

    def finish(*refs):
        c, me, blocks, first, relay, passed, whole = parts(*refs)
        for t in range(n):
            for j in range(2):
                relay(t, j, blocks[2], c).wait_recv()
                passed(t, 2 + j, blocks[2], c, j).start()
        for t in range(n):
            for j in range(2):
                passed(t, j, blocks[j], 1 - c).wait_recv()
                passed(t, 2 + j, blocks[2], 1 - c, j).wait_recv()
        for s in range(ns):
            for j in range(3):
                whole(s, j, blocks[j]).wait_recv()
        for t in range(n):
            for j in range(2):
                first(t, j, me, c).wait_send()
                relay(t, 1 - j, blocks[j], c).wait_send()
                passed(t, j, blocks[j], c).wait_send()
                passed(t, 2 + j, blocks[2], c, j).wait_send()
        for s in range(ns):
            for j in range(3):
                whole(s, j, me).wait_send()

    arrays = [*shards, *small]
    shapes = [jax.ShapeDtypeStruct((N_CHIPS, *a.shape), a.dtype) for a in arrays]
    return _Plan(arrays, shapes, per * n + 3 * ns, start, finish, middle)


def _run_comm(plan, name):
    k = len(plan.arrays)
    in_specs, out_specs, out_shape, sems = plan.specs()

    def body(*refs):
        cr = (refs[:k], refs[k:k + len(out_shape)], refs[-2], refs[-1])
        plan.start(*cr)
        if plan.middle is not None:
            plan.middle(*cr)
        plan.finish(*cr)

    return _pcall(body, name=name, in_specs=in_specs, out_specs=out_specs, out_shape=out_shape,
                  input_output_aliases=plan.aliases, scratch_shapes=sems)(*plan.arrays)


def _carried(plan, in_specs, out_specs, out_shape, scratch):
    aliases = {}
    if plan is not None:
        p_in, p_out, p_shape, p_sems = plan.specs()
        aliases = {len(in_specs) + i: len(out_specs) + j for i, j in plan.aliases.items()}
        in_specs, out_specs = in_specs + p_in, out_specs + p_out
        out_shape, scratch = out_shape + p_shape, scratch + p_sems
    return dict(in_specs=in_specs, out_specs=out_specs, out_shape=out_shape, scratch_shapes=scratch,
                input_output_aliases=aliases)


def _unpack(refs, n_in, n_out, plan):
    k_in = len(plan.arrays) if plan else 0
    k_out = len(plan.out_shapes) if plan else 0
    ins = refs[:n_in]
    outs = refs[n_in + k_in:n_in + k_in + n_out]
    rest = refs[n_in + k_in + n_out + k_out:]
    if plan is None:
        return ins, outs, rest, None
    cr = (refs[n_in:n_in + k_in], refs[n_in + k_in + n_out:n_in + k_in + n_out + k_out], rest[-2], rest[-1])
    return ins, outs, rest[:-2], cr


def _hook(plan, cr, which, cond):
    fn = getattr(plan, which) if plan is not None else None
    if fn is not None:
        pl.when(cond)(lambda: fn(*cr))


def _join_plan(shards):
    n = len(shards)

    def copy(ins, outs, send_sems, recv_sems, t, core):
        x, y, c, _ = _place()
        half = ins[t].shape[0] // 2
        rows = pl.ds(pl.multiple_of(core * half, 8), half)
        return pltpu.make_async_remote_copy(
            src_ref=ins[t].at[rows, :], dst_ref=outs[t].at[rows, :], send_sem=send_sems.at[t],
            recv_sem=recv_sems.at[t], device_id=(x, y, 1 - c), device_id_type=MESH)

    def start(*refs):
        c = lax.axis_index("c")
        for t in range(n):
            copy(*refs, t, c).start()

    def finish(*refs):
        c = lax.axis_index("c")
        for t in range(n):
            copy(*refs, t, 1 - c).wait_recv()
        for t in range(n):
            copy(*refs, t, c).wait_send()

    shapes = [jax.ShapeDtypeStruct(s.shape, s.dtype) for s in shards]
    return _Plan(shards, shapes, n, start, finish, aliases={t: t for t in range(n)})


def _all_gather_plan(vec):
    def parts(ins, outs, send_sems, recv_sems):
        x, y, c, _ = _place()
        me = 4 * x + 2 * y + c
        rel = [((k >> 2) & 1, (k >> 1) & 1, k & 1) for k in range(1, N_DEV)]

        def peer(k):
            fx, fy, fc = rel[k]
            return (x ^ fx, y ^ fy, c ^ fc)

        def copy(k, dev):
            return pltpu.make_async_remote_copy(
                src_ref=ins[0], dst_ref=outs[0].at[dev], send_sem=send_sems.at[k], recv_sem=recv_sems.at[k],
                device_id=peer(k), device_id_type=MESH)

        mine = pltpu.make_async_copy(ins[0], outs[0].at[me], send_sems.at[N_DEV - 1])
        return me, peer, copy, mine

    def start(*refs):
        me, _, copy, mine = parts(*refs)
        mine.start()
        for k in range(N_DEV - 1):
            copy(k, me).start()

    def finish(*refs):
        me, peer, copy, mine = parts(*refs)
        for k in range(N_DEV - 1):
            px, py, pc = peer(k)
            copy(k, 4 * px + 2 * py + pc).wait_recv()
        for k in range(N_DEV - 1):
            copy(k, me).wait_send()
        mine.wait()

    return _Plan([vec], [jax.ShapeDtypeStruct((N_DEV, *vec.shape), vec.dtype)], N_DEV, start, finish)


def _sum_devices(blocks, name):
    def body(b_ref, o_ref):
        total = b_ref[0]
        for dev in range(1, N_DEV):
            total = total + b_ref[dev]
        o_ref[...] = total

    return _pcall(body, name=name, in_specs=[pl.BlockSpec(memory_space=pltpu.VMEM)],
                  out_specs=pl.BlockSpec(memory_space=pltpu.VMEM),
                  out_shape=jax.ShapeDtypeStruct(blocks.shape[1:], F32))(blocks)


TOKEN_TILE = 512
PROJ_TOKEN_TILE = 1024
BWD_VMEM_LIMIT = 62 * 1024 * 1024
DW_TOKEN_TILE = 2048
DW_ROW_SPLIT = 2
MXU_COLS = 256


def _chunks(n):
    out, c0 = [], 0
    while c0 < n:
        size = min(MXU_COLS, n - c0)
        out.append((c0, size))
        c0 += size
    return out


def _load_weights(hbm_refs, vmem_refs, sems):
    copies = [pltpu.make_async_copy(h, v, sems.at[k]) for k, (h, v) in enumerate(zip(hbm_refs, vmem_refs))]
    for cp in copies:
        cp.start()
    for cp in copies:
        cp.wait()


def _ffn_fwd(x, g, wgt, wut, wd, name, plan=None, head=None):
    T, D = x.shape
    F = wgt.shape[0]
    tm = min(T, TOKEN_TILE)
    ni = T // tm
    n_head = 2 if head is not None else 0

    def body(*refs):
        ins, outs, scratch, cr = _unpack(refs, 5 + n_head, 5 + n_head, plan)
        x_ref, g_ref, wg_hbm, wu_hbm, wd_hbm = ins[:5]
        xo_ref, h_ref, gate_ref, up_ref, act_ref = outs[:5]
        wg_ref, wu_ref, wd_ref, sems = scratch
        i = pl.program_id(0)
        _hook(plan, cr, "start", i == 0)

        @pl.when(i == 0)
        def _():
            _load_weights((wg_hbm, wu_hbm, wd_hbm), (wg_ref, wu_ref, wd_ref), sems)

        xv = x_ref[...]
        h = ((xv * _rms_inv(xv)) * g_ref[...]).astype(BF16)
        h_ref[...] = h
        for c0, size in _chunks(F):
            gate = _dot_nt(h, wg_ref[c0:c0 + size, :])
            up = _dot_nt(h, wu_ref[c0:c0 + size, :])
            gate_ref[:, c0:c0 + size] = gate.astype(BF16)
            up_ref[:, c0:c0 + size] = up.astype(BF16)
            act_ref[:, c0:c0 + size] = (gate * jax.nn.sigmoid(gate) * up).astype(BF16)
        y = x_ref[...] + FFN_RES_SCALE * _dot(act_ref[...], wd_ref[...])
        if head is None:
            xo_ref[...] = y
        else:
            gf_ref, t_ref = ins[5:]
            dgf_ref, loss_ref = outs[5:]

            @pl.when(i == 0)
            def _():
                dgf_ref[...] = jnp.zeros_like(dgf_ref)
                loss_ref[...] = jnp.zeros_like(loss_ref)

            gf = gf_ref[...]
            diff = (y * _rms_inv(y)) * gf - t_ref[...]
            loss_ref[...] += 0.5 * jnp.sum(jnp.mean(diff * diff, axis=-1, keepdims=True))
            dy, dgf = _norm_bwd(diff * (1.0 / D), y, gf)
            xo_ref[...] = dy
            dgf_ref[...] += dgf
        _hook(plan, cr, "middle", i == ni // 2)
        _hook(plan, cr, "finish", i == ni - 1)

    const = lambda shape: pl.BlockSpec(shape, lambda i: (0, 0))
    rows = lambda width: pl.BlockSpec((tm, width), lambda i: (i, 0))
    in_specs = [rows(D), const((1, D)), ANY_SPEC, ANY_SPEC, ANY_SPEC]
    out_specs = [rows(D), rows(D), rows(F), rows(F), rows(F)]
    out_shape = [jax.ShapeDtypeStruct((T, D), F32), jax.ShapeDtypeStruct((T, D), BF16),
                 jax.ShapeDtypeStruct((T, F), BF16), jax.ShapeDtypeStruct((T, F), BF16), jax.ShapeDtypeStruct((T, F), BF16)]
    if head is not None:
        in_specs += [const((1, D)), rows(D)]
        out_specs += [const((1, D)), const((1, LANES))]
        out_shape += [jax.ShapeDtypeStruct((1, D), F32), jax.ShapeDtypeStruct((1, LANES), F32)]
    io = _carried(plan, in_specs, out_specs, out_shape,
                  [pltpu.VMEM((F, D), BF16), pltpu.VMEM((F, D), BF16), pltpu.VMEM((F, D), BF16),
                   pltpu.SemaphoreType.DMA((3,))])
    return _pcall(
        body, name=name, grid=(ni,), compiler_params=_params(1), **io,
    )(x, g, wgt, wut, wd, *(head or ()), *(plan.arrays if plan else ()))


def _ffn_bwd(dy, x, g, gate, up, wgt, wut, wd, name, plan=None):
    T, D = x.shape
    F = wgt.shape[0]
    tm = min(T, TOKEN_TILE)
    ni = T // tm

    def body(*refs):
        ins, outs, scratch, cr = _unpack(refs, 8, 5, plan)
        dy_ref, x_ref, g_ref, gate_ref, up_ref, wg_hbm, wu_hbm, wd_hbm = ins
        dx_ref, dyb_ref, dgate_ref, dup_ref, dg_ref = outs
        wg_ref, wu_ref, wd_ref, sems = scratch
        i = pl.program_id(0)
        _hook(plan, cr, "start", i == 0)

        @pl.when(i == 0)
        def _():
            _load_weights((wg_hbm, wu_hbm, wd_hbm), (wg_ref, wu_ref, wd_ref), sems)
            dg_ref[...] = jnp.zeros_like(dg_ref)

        dyb = (FFN_RES_SCALE * dy_ref[...]).astype(BF16)
        dyb_ref[...] = dyb
        for c0, size in _chunks(F):
            dact = _dot_nt(dyb, wd_ref[c0:c0 + size, :])
            gt = gate_ref[:, c0:c0 + size].astype(F32)
            u = up_ref[:, c0:c0 + size].astype(F32)
            sig = jax.nn.sigmoid(gt)
            dup_ref[:, c0:c0 + size] = (dact * (gt * sig)).astype(BF16)
            dgate_ref[:, c0:c0 + size] = (dact * u * (sig * (1.0 + gt * (1.0 - sig)))).astype(BF16)
        dh = _dot(dgate_ref[...], wg_ref[...]) + _dot(dup_ref[...], wu_ref[...])
        dxn, dg = _norm_bwd(dh, x_ref[...], g_ref[...])
        dx_ref[...] = dy_ref[...] + dxn
        dg_ref[...] += dg
        _hook(plan, cr, "finish", i == ni - 1)

    io = _carried(
        plan,
        [pl.BlockSpec((tm, D), lambda i: (i, 0)), pl.BlockSpec((tm, D), lambda i: (i, 0)),
         pl.BlockSpec((1, D), lambda i: (0, 0)),
         pl.BlockSpec((tm, F), lambda i: (i, 0)), pl.BlockSpec((tm, F), lambda i: (i, 0)),
         ANY_SPEC, ANY_SPEC, ANY_SPEC],
        [pl.BlockSpec((tm, D), lambda i: (i, 0)), pl.BlockSpec((tm, D), lambda i: (i, 0)),
         pl.BlockSpec((tm, F), lambda i: (i, 0)), pl.BlockSpec((tm, F), lambda i: (i, 0)),
         pl.BlockSpec((1, D), lambda i: (0, 0))],
        [jax.ShapeDtypeStruct((T, D), F32), jax.ShapeDtypeStruct((T, D), BF16),
         jax.ShapeDtypeStruct((T, F), BF16), jax.ShapeDtypeStruct((T, F), BF16), jax.ShapeDtypeStruct((1, D), F32)],
        [pltpu.VMEM((F, D), BF16), pltpu.VMEM((F, D), BF16), pltpu.VMEM((F, D), BF16),
         pltpu.SemaphoreType.DMA((3,))])
    return _pcall(
        body, name=name, grid=(ni,), compiler_params=_params(1, BWD_VMEM_LIMIT), **io,
    )(dy, x, g, gate, up, wgt, wut, wd, *(plan.arrays if plan else ()))


def _matmul_tn(a, b, row_split, name, plan=None):
    T, n1 = a.shape
    n2 = b.shape[1]
    tn = n1 // row_split
    tk = min(T, DW_TOKEN_TILE)
    nk = T // tk

    def body(*refs):
        (a_ref, b_ref), (o_ref, ob_ref), _, cr = _unpack(refs, 2, 2, plan)
        j = pl.program_id(0)
        k = pl.program_id(1)
        _hook(plan, cr, "start", jnp.logical_and(j == 0, k == 0))

        @pl.when(k == 0)
        def _():
            o_ref[...] = jnp.zeros_like(o_ref)

        o_ref[...] += _dot_tn(a_ref[...], b_ref[...])

        @pl.when(k == nk - 1)
        def _():
            ob_ref[...] = o_ref[...].astype(BF16)

        _hook(plan, cr, "finish", jnp.logical_and(j == row_split - 1, k == nk - 1))

    io = _carried(
        plan,
        [pl.BlockSpec((tk, tn), lambda j, k: (k, j)), pl.BlockSpec((tk, n2), lambda j, k: (k, 0))],
        [pl.BlockSpec((tn, n2), lambda j, k: (j, 0)), pl.BlockSpec((tn, n2), lambda j, k: (j, 0))],
        [jax.ShapeDtypeStruct((n1, n2), F32), jax.ShapeDtypeStruct((n1, n2), BF16)], [])
    return _pcall(
        body, name=name, grid=(row_split, nk), compiler_params=_params(2), **io,
    )(a, b, *(plan.arrays if plan else ()))


def _norm_matmul(x, g, wt, name):
    T, D = x.shape
    n = wt.shape[0]
    tm = min(T, PROJ_TOKEN_TILE)

    def body(x_ref, g_ref, w_ref, z_ref, h_ref):
        xv = x_ref[...]
        h = ((xv * _rms_inv(xv)) * g_ref[...]).astype(BF16)
        h_ref[...] = h
        z_ref[...] = _dot_nt(h, w_ref[...])

    return _pcall(
        body, name=name, grid=(T // tm,),
        in_specs=[pl.BlockSpec((tm, D), lambda i: (i, 0)), pl.BlockSpec((1, D), lambda i: (0, 0)),
                  pl.BlockSpec((n, D), lambda i: (0, 0))],
        out_specs=[pl.BlockSpec((tm, n), lambda i: (i, 0)), pl.BlockSpec((tm, D), lambda i: (i, 0))],
        out_shape=[jax.ShapeDtypeStruct((T, n), F32), jax.ShapeDtypeStruct((T, D), BF16)],
        compiler_params=_params(1),
    )(x, g, wt)


def _matmul_residual(y, w, x, name):
    T, D = x.shape
    kdim = y.shape[1]
    tm = min(T, PROJ_TOKEN_TILE)

    def body(y_ref, w_ref, x_ref, o_ref):
        o_ref[...] = x_ref[...] + _dot(y_ref[...], w_ref[...])

    return _pcall(
        body, name=name, grid=(T // tm,),
        in_specs=[pl.BlockSpec((tm, kdim), lambda i: (i, 0)), pl.BlockSpec((kdim, D), lambda i: (0, 0)),
                  pl.BlockSpec((tm, D), lambda i: (i, 0))],
        out_specs=pl.BlockSpec((tm, D), lambda i: (i, 0)),
        out_shape=jax.ShapeDtypeStruct((T, D), F32),
        compiler_params=_params(1),
    )(y, w, x)


def _matmul_nt(dx, w, name, plan=None):
    T, D = dx.shape
    kdim = w.shape[0]
    tm = min(T, PROJ_TOKEN_TILE)
    ni = T // tm

    def body(*refs):
        (dx_ref, w_ref), (dy_ref, dxb_ref), _, cr = _unpack(refs, 2, 2, plan)
        i = pl.program_id(0)
        _hook(plan, cr, "start", i == 0)
        dxb = dx_ref[...].astype(BF16)
        dxb_ref[...] = dxb
        dy_ref[...] = _dot_nt(dxb, w_ref[...])
        _hook(plan, cr, "finish", i == ni - 1)

    io = _carried(
        plan,
        [pl.BlockSpec((tm, D), lambda i: (i, 0)), pl.BlockSpec((kdim, D), lambda i: (0, 0))],
        [pl.BlockSpec((tm, kdim), lambda i: (i, 0)), pl.BlockSpec((tm, D), lambda i: (i, 0))],
        [jax.ShapeDtypeStruct((T, kdim), F32), jax.ShapeDtypeStruct((T, D), BF16)], [])
    return _pcall(
        body, name=name, grid=(ni,), compiler_params=_params(1), **io,
    )(dx, w, *(plan.arrays if plan else ()))


def _matmul_norm_bwd(dz, wt, x, g, dres, name, plan=None):
    T, D = x.shape
    n = dz.shape[1]
    tm = min(T, PROJ_TOKEN_TILE)
    ni = T // tm

    def body(*refs):
        (dz_ref, w_ref, x_ref, g_ref, dres_ref), (dx_ref, dg_ref), _, cr = _unpack(refs, 5, 2, plan)
        i = pl.program_id(0)
        _hook(plan, cr, "start", i == 0)

        @pl.when(i == 0)
        def _():
            dg_ref[...] = jnp.zeros_like(dg_ref)

        dh = _dot(dz_ref[...], w_ref[...])
        dxn, dg = _norm_bwd(dh, x_ref[...], g_ref[...])
        dx_ref[...] = dres_ref[...] + dxn
        dg_ref[...] += dg
        _hook(plan, cr, "finish", i == ni - 1)

    io = _carried(
        plan,
        [pl.BlockSpec((tm, n), lambda i: (i, 0)), pl.BlockSpec((n, D), lambda i: (0, 0)),
         pl.BlockSpec((tm, D), lambda i: (i, 0)), pl.BlockSpec((1, D), lambda i: (0, 0)),
         pl.BlockSpec((tm, D), lambda i: (i, 0))],
        [pl.BlockSpec((tm, D), lambda i: (i, 0)), pl.BlockSpec((1, D), lambda i: (0, 0))],
        [jax.ShapeDtypeStruct((T, D), F32), jax.ShapeDtypeStruct((1, D), F32)], [])
    return _pcall(
        body, name=name, grid=(ni,), compiler_params=_params(1), **io,
    )(dz, wt, x, g, dres, *(plan.arrays if plan else ()))


Z_Q = 3 * CONV_WIDTH
Z_K = Z_Q + N_Q_HEADS * HEAD_DIM
Z_V = Z_K + LANES
Z_END = Z_V + LANES


def _rope_tables(T):
    half = ROT_DIM // 2
    inv_freq = ROPE_THETA ** (-jnp.arange(0, ROT_DIM, 2, dtype=F32) / ROT_DIM)
    ang = inv_freq[:, None] * jnp.arange(T, dtype=F32)[None, :]
    cos_sin = jnp.concatenate([jnp.cos(ang), jnp.sin(ang)], axis=0)
    select = np.zeros((2 * half, 3 * LANES), np.float32)
    const = np.zeros((1, 3 * LANES), np.float32)
    for lane in range(LANES):
        d = lane % HEAD_DIM
        if d < half:
            select[d, lane] = 1.0
            select[half + d, LANES + lane] = -1.0
        elif d < ROT_DIM:
            select[d - half, lane] = 1.0
            select[d, 2 * LANES + lane] = 1.0
        else:
            const[0, lane] = 1.0
    tab = lax.dot_general(cos_sin, jnp.asarray(select), (((0,), (0,)), ((), ())),
                          precision=lax.Precision.HIGHEST, preferred_element_type=F32)
    return tab + jnp.asarray(const)


def _tab3(tab):
    return tab[:, 0:LANES], tab[:, LANES:2 * LANES], tab[:, 2 * LANES:3 * LANES]


def _rot(x, tab):
    c, s1, s2 = _tab3(tab)
    return x * c + pltpu.roll(x, LANES - ROT_DIM // 2, 1) * s1 + pltpu.roll(x, ROT_DIM // 2, 1) * s2


def _rot_t(d, tab):
    c, s1, s2 = _tab3(tab)
    return d * c + pltpu.roll(d * s1, ROT_DIM // 2, 1) + pltpu.roll(d * s2, LANES - ROT_DIM // 2, 1)


def _head_pads(a):
    lo = lax.broadcasted_iota(jnp.int32, a.shape, 1) < HEAD_DIM
    nat0 = jnp.where(lo, a, 0.0)
    nat1 = jnp.where(lo, 0.0, a)
    return {
        (0, 0): nat0.astype(BF16), (0, 1): pltpu.roll(nat0, HEAD_DIM, 1).astype(BF16),
        (1, 0): pltpu.roll(nat1, HEAD_DIM, 1).astype(BF16), (1, 1): nat1.astype(BF16),
    }


def _from_pads(even, odd, kv):
    lo = lax.broadcasted_iota(jnp.int32, even.shape, 1) < HEAD_DIM
    if kv == 0:
        return jnp.where(lo, even + pltpu.roll(odd, HEAD_DIM, 1), 0.0)
    return jnp.where(lo, 0.0, pltpu.roll(even, HEAD_DIM, 1) + odd)


N_GROUPS = 4


def _group_head(g, r):
    kv, par = divmod(g, 2)
    return 2 * (2 * kv + r) + par


def _window_mask_t(has_prev):
    jj = lax.broadcasted_iota(jnp.int32, (2 * BLOCK, 2 * BLOCK), 0)
    ii = lax.broadcasted_iota(jnp.int32, (2 * BLOCK, 2 * BLOCK), 1) & (BLOCK - 1)
    rel = jj - BLOCK - ii
    return (rel <= 0) & (rel > -BLOCK) & ((jj >= BLOCK) | has_prev)


def _sink_row(sink_ref, g):
    lane = lax.broadcasted_iota(jnp.int32, (1, 2 * BLOCK), 1)
    return jnp.where(lane < BLOCK, sink_ref[0, _group_head(g, 0)], sink_ref[0, _group_head(g, 1)])


def _attn_probs_t(q2, kp, mask, sink_ref):
    out = []
    for kv in range(2):
        q_st = jnp.concatenate([q2[2 * kv], q2[2 * kv + 1]], axis=0)
        for par in range(2):
            s = jnp.where(mask, _dot_nt(kp[(kv, par)], q_st), MASK_VALUE)
            sink = _sink_row(sink_ref, 2 * kv + par)
            m = jnp.maximum(jnp.max(s, axis=0, keepdims=True), sink)
            p = jnp.exp(s - m)
            esink = jnp.exp(sink - m)
            rden = 1.0 / (jnp.sum(p, axis=0, keepdims=True) + esink)
            out.append((p * rden, esink * rden))
    return out


def _conv_taps(cg, u, cg_prev, u_prev, has_prev):
    vv = cg * u
    halo = jnp.where(has_prev, cg_prev * u_prev, 0.0)
    ext = jnp.concatenate([halo, vv], axis=0)
    rows = ext.shape[0]
    vv1 = pltpu.roll(ext, 1, 0)[8:rows]
    vv2 = pltpu.roll(ext, 2, 0)[8:rows]
    return vv, vv1, vv2


def _mix_specs(nb):
    cur = lambda n: jnp.minimum(n, nb - 1)
    prev = lambda n: jnp.maximum(jnp.minimum(n, nb - 1) - 1, 0)
    rows8_prev = lambda n: jnp.maximum(16 * jnp.minimum(n, nb - 1) - 1, 0)
    return cur, prev, [
        pl.BlockSpec((BLOCK, Z_END), lambda n: (cur(n), 0)),
        pl.BlockSpec((BLOCK, 2 * LANES), lambda n: (prev(n), Z_K // (2 * LANES))),
        pl.BlockSpec((8, CONV_WIDTH), lambda n: (rows8_prev(n), 1)),
        pl.BlockSpec((8, CONV_WIDTH), lambda n: (rows8_prev(n), 2)),
        pl.BlockSpec((BLOCK, 3 * LANES), lambda n: (cur(n), 0)),
        pl.BlockSpec((BLOCK, 3 * LANES), lambda n: (prev(n), 0)),
        pl.BlockSpec((3, CONV_WIDTH), lambda n: (0, 0)),
        pl.BlockSpec(memory_space=pltpu.SMEM),
    ]


def _mix_core_fwd(z, tab, conv_w, sinks, name):
    T = z.shape[0]
    nb = T // BLOCK
    _, _, specs = _mix_specs(nb)

    def body(z_ref, zkvp_ref, cgp_ref, up_ref, tab_ref, tabp_ref, cw_ref, sink_ref, y_ref):
        has_prev = pl.program_id(0) > 0
        bg = z_ref[:, 0:CONV_WIDTH]
        vv, vv1, vv2 = _conv_taps(z_ref[:, CONV_WIDTH:2 * CONV_WIDTH], z_ref[:, 2 * CONV_WIDTH:Z_Q],
                                  cgp_ref[...], up_ref[...], has_prev)
        conv = cw_ref[0:1, :] * vv2 + cw_ref[1:2, :] * vv1 + cw_ref[2:3, :] * vv
        y_ref[:, 0:CONV_WIDTH] = (bg * conv).astype(BF16)

        tab_c = tab_ref[...]
        tab_p = tabp_ref[...]
        k_all = jnp.concatenate([_rot(zkvp_ref[:, 0:LANES], tab_p), _rot(z_ref[:, Z_K:Z_V], tab_c)], axis=0)
        v_all = jnp.concatenate([zkvp_ref[:, LANES:2 * LANES], z_ref[:, Z_V:Z_END]], axis=0)
        kp = _head_pads(k_all)
        vp = _head_pads(v_all)
        q2 = [(_rot(z_ref[:, Z_Q + LANES * c:Z_Q + LANES * (c + 1)], tab_c) * ATTN_SCALE).astype(BF16)
              for c in range(N_Q_HEADS // 2)]
        probs = _attn_probs_t(q2, kp, _window_mask_t(has_prev), sink_ref)
        for kv in range(2):
            o_t = (_dot_tn(vp[(kv, 0)], probs[2 * kv][0].astype(BF16))
                   + _dot_tn(vp[(kv, 1)], probs[2 * kv + 1][0].astype(BF16)))
            for r in range(2):
                c = 2 * kv + r
                y_ref[:, CONV_WIDTH + LANES * c:CONV_WIDTH + LANES * (c + 1)] = o_t[:, BLOCK * r:BLOCK * (r + 1)].T.astype(BF16)

    return _pcall(
        body, name=name, grid=(nb,), in_specs=specs,
        out_specs=pl.BlockSpec((BLOCK, 2 * CONV_WIDTH), lambda n: (n, 0)),
        out_shape=jax.ShapeDtypeStruct((T, 2 * CONV_WIDTH), BF16),
        compiler_params=_params(1),
    )(z, z, z, z, tab, tab, conv_w, sinks)


def _mix_core_bwd(z, dy, tab, conv_w, sinks, name, plan=None):
    T = z.shape[0]
    nb = T // BLOCK
    cur, _, specs = _mix_specs(nb)
    rows8_next = lambda n: jnp.minimum(16 * (cur(n) + 1), 16 * nb - 1)
    specs = specs[:4] + [
        pl.BlockSpec((8, CONV_WIDTH), lambda n: (rows8_next(n), 0)),
        pl.BlockSpec((BLOCK, 2 * CONV_WIDTH), lambda n: (cur(n), 0)),
        pl.BlockSpec((8, CONV_WIDTH), lambda n: (rows8_next(n), 0)),
    ] + specs[4:]

    def body(*refs):
        ins, outs, scratch, cr = _unpack(refs, 11, 3, plan)
        z_ref, zkvp_ref, cgp_ref, up_ref, bgn_ref, dy_ref, dyn_ref, tab_ref, tabp_ref, cw_ref, sink_ref = ins
        dz_ref, dcw_ref, dsk_ref = outs
        main_ref, kv_ref = scratch
        n = pl.program_id(0)
        _hook(plan, cr, "start", n == 0)

        @pl.when(n == 0)
        def _():
            main_ref[...] = jnp.zeros_like(main_ref)
            kv_ref[...] = jnp.zeros_like(kv_ref)
            dcw_ref[...] = jnp.zeros_like(dcw_ref)
            dsk_ref[...] = jnp.zeros_like(dsk_ref)

        @pl.when(n < nb)
        def _():
            has_prev = n > 0
            has_next = n < nb - 1
            bg = z_ref[:, 0:CONV_WIDTH]
            cg = z_ref[:, CONV_WIDTH:2 * CONV_WIDTH]
            u = z_ref[:, 2 * CONV_WIDTH:Z_Q]
            vv, vv1, vv2 = _conv_taps(cg, u, cgp_ref[...], up_ref[...], has_prev)
            w0, w1, w2 = cw_ref[0:1, :], cw_ref[1:2, :], cw_ref[2:3, :]
            dyc = dy_ref[:, 0:CONV_WIDTH]
            dbg = dyc * (w0 * vv2 + w1 * vv1 + w2 * vv)
            dconv = dyc * bg
            dconv_next = jnp.where(has_next, dyn_ref[...] * bgn_ref[...], 0.0)
            ext = jnp.concatenate([dconv, dconv_next], axis=0)
            rows = ext.shape[0]
            dvv = w2 * dconv + w1 * pltpu.roll(ext, rows - 1, 0)[0:BLOCK] + w0 * pltpu.roll(ext, rows - 2, 0)[0:BLOCK]
            dcw_ref[0:1, :] += jnp.sum(dconv * vv2, axis=0, keepdims=True)
            dcw_ref[1:2, :] += jnp.sum(dconv * vv1, axis=0, keepdims=True)
            dcw_ref[2:3, :] += jnp.sum(dconv * vv, axis=0, keepdims=True)

            tab_c = tab_ref[...]
            tab_p = tabp_ref[...]
            k_all = jnp.concatenate([_rot(zkvp_ref[:, 0:LANES], tab_p), _rot(z_ref[:, Z_K:Z_V], tab_c)], axis=0)
            v_all = jnp.concatenate([zkvp_ref[:, LANES:2 * LANES], z_ref[:, Z_V:Z_END]], axis=0)
            kp = _head_pads(k_all)
            vp = _head_pads(v_all)
            chunks = range(N_Q_HEADS // 2)
            q2 = [(_rot(z_ref[:, Z_Q + LANES * c:Z_Q + LANES * (c + 1)], tab_c) * ATTN_SCALE).astype(BF16) for c in chunks]
            do2 = [dy_ref[:, CONV_WIDTH + LANES * c:CONV_WIDTH + LANES * (c + 1)].astype(BF16) for c in chunks]
            probs = _attn_probs_t(q2, kp, _window_mask_t(has_prev), sink_ref)
            dq_chunks = []
            dk_nat = jnp.zeros((2 * BLOCK, LANES), F32)
            dv_nat = jnp.zeros((2 * BLOCK, LANES), F32)
            for kv in range(2):
                q_st = jnp.concatenate([q2[2 * kv], q2[2 * kv + 1]], axis=0)
                do_st = jnp.concatenate([do2[2 * kv], do2[2 * kv + 1]], axis=0)
                dq_t = jnp.zeros((LANES, 2 * BLOCK), F32)
                dk_par, dv_par = [], []
                for par in range(2):
                    g = 2 * kv + par
                    pr, psink = probs[g]
                    dp = _dot_nt(vp[(kv, par)], do_st)
                    delta = jnp.sum(dp * pr, axis=0, keepdims=True)
                    ds = (pr * (dp - delta)).astype(BF16)
                    dsink = -psink * delta
                    for r in range(2):
                        h = _group_head(g, r)
                        dsk_ref[h:h + 1, :] += jnp.sum(dsink[:, BLOCK * r:BLOCK * (r + 1)])
                    dq_t = dq_t + _dot_tn(kp[(kv, par)], ds)
                    dk_par.append(_dot(ds, q_st))
                    dv_par.append(_dot(pr.astype(BF16), do_st))
                for r in range(2):
                    dq_chunks.append(_rot_t(dq_t[:, BLOCK * r:BLOCK * (r + 1)].T * ATTN_SCALE, tab_c))
                dk_nat = dk_nat + _from_pads(dk_par[0], dk_par[1], kv)
                dv_nat = dv_nat + _from_pads(dv_par[0], dv_par[1], kv)

            dk_prev = _rot_t(kv_ref[:, 0:LANES] + dk_nat[0:BLOCK], tab_p)
            dv_prev = kv_ref[:, LANES:2 * LANES] + dv_nat[0:BLOCK]
            dz_ref[:, 0:Z_K] = main_ref[...]
            dz_ref[:, Z_K:Z_V] = dk_prev.astype(BF16)
            dz_ref[:, Z_V:Z_END] = dv_prev.astype(BF16)
            main_ref[:, 0:CONV_WIDTH] = dbg.astype(BF16)
            main_ref[:, CONV_WIDTH:2 * CONV_WIDTH] = (dvv * u).astype(BF16)
            main_ref[:, 2 * CONV_WIDTH:Z_Q] = (dvv * cg).astype(BF16)
            for c in range(N_Q_HEADS // 2):
                main_ref[:, Z_Q + LANES * c:Z_Q + LANES * (c + 1)] = dq_chunks[c].astype(BF16)
            kv_ref[:, 0:LANES] = dk_nat[BLOCK:2 * BLOCK]
            kv_ref[:, LANES:2 * LANES] = dv_nat[BLOCK:2 * BLOCK]

        @pl.when(n == nb)
        def _():
            dz_ref[:, 0:Z_K] = main_ref[...]
            dz_ref[:, Z_K:Z_V] = _rot_t(kv_ref[:, 0:LANES], tab_ref[...]).astype(BF16)
            dz_ref[:, Z_V:Z_END] = kv_ref[:, LANES:2 * LANES].astype(BF16)

        _hook(plan, cr, "finish", n == nb)

    io = _carried(
        plan, specs,
        [pl.BlockSpec((BLOCK, Z_END), lambda n: (jnp.maximum(n - 1, 0), 0)),
         pl.BlockSpec((8, CONV_WIDTH), lambda n: (0, 0)), pl.BlockSpec((8, LANES), lambda n: (0, 0))],
        [jax.ShapeDtypeStruct((T, Z_END), BF16), jax.ShapeDtypeStruct((8, CONV_WIDTH), F32),
         jax.ShapeDtypeStruct((8, LANES), F32)],
        [pltpu.VMEM((BLOCK, Z_K), BF16), pltpu.VMEM((BLOCK, 2 * LANES), F32)])
    return _pcall(
        body, name=name, grid=(nb + 1,), compiler_params=_params(1), **io,
    )(z, z, z, z, z, dy, dy, tab, tab, conv_w, sinks, *(plan.arrays if plan else ()))


def _local_sums(pair, chip, place, name):
    arrays, in_specs, out_specs, out_shape = [], [], [], []
    if pair is not None:
        g, sib = pair
        blk = (1, *sib.shape[1:])
        arrays += [g, sib]
        in_specs += [pl.BlockSpec(blk, lambda q, p: (q, p[1], 0)), pl.BlockSpec(blk, lambda q, p: (q, 0, 0))]
        out_specs.append(pl.BlockSpec(blk, lambda q, p: (q, 0, 0)))
        out_shape.append(jax.ShapeDtypeStruct(sib.shape, BF16))
    if chip is not None:
        g2, sib2, recv2 = chip
        blk = (1, *sib2.shape[1:])
        arrays += [g2, sib2, recv2]
        in_specs += [pl.BlockSpec(blk, lambda q, p: (p[0], p[1], 0)), pl.BlockSpec(blk, lambda q, p: (p[0], 0, 0)),
                     pl.BlockSpec(recv2.shape, lambda q, p: (0, 0, 0))]
        out_specs.append(pl.BlockSpec(sib2.shape[1:], lambda q, p: (p[1], 0)))
        out_shape.append(jax.ShapeDtypeStruct(g2.shape[1:], F32))

    def body(place_ref, *refs):
        refs = list(refs)
        ins, outs = refs[:len(arrays)], refs[len(arrays):]
        if pair is not None:
            g_ref, sib_ref = ins[:2]
            outs[0][...] = (g_ref[...] + sib_ref[...].astype(F32)).astype(BF16)
        if chip is not None:
            g_ref, sib_ref, recv_ref = ins[-3:]

            @pl.when(pl.program_id(0) == 0)
            def _():
                total = g_ref[0] + sib_ref[0].astype(F32)
                for j in range(3):
                    total = total + recv_ref[j].astype(F32)
                outs[-1][...] = total

    return _pcall(
        body, name=name,
        grid_spec=pltpu.PrefetchScalarGridSpec(num_scalar_prefetch=1, grid=(N_CHIPS,),
                                               in_specs=in_specs, out_specs=out_specs),
        out_shape=out_shape, compiler_params=_params(1),
    )(place, *arrays)


def _adamw_math(w, g, m, v):
    m = ADAM_B1 * m + (1.0 - ADAM_B1) * g
    v = ADAM_B2 * v + (1.0 - ADAM_B2) * (g * g)
    m_hat = m / (1.0 - ADAM_B1 ** ADAM_STEP)
    v_hat = v / (1.0 - ADAM_B2 ** ADAM_STEP)
    delta = -ADAM_LR * (m_hat / (jnp.sqrt(v_hat) + ADAM_EPS) + ADAM_WD * w)
    return delta, m, v


def _adamw(ws, gs, ms, vs, row_blocks, name):
    n = len(ws)

    def body(*refs):
        w, g, m, v = refs[:n], refs[n:2 * n], refs[2 * n:3 * n], refs[3 * n:4 * n]
        d, mo, vo, go = refs[4 * n:5 * n], refs[5 * n:6 * n], refs[6 * n:7 * n], refs[7 * n:]
        for t in range(n):
            gv = g[t][...]
            delta, m_new, v_new = _adamw_math(w[t][...], gv, m[t][...], v[t][...])
            d[t][...] = delta
            mo[t][...] = m_new
            vo[t][...] = v_new
            go[t][...] = gv

    specs = [pl.BlockSpec((a.shape[0] // row_blocks, a.shape[1]), lambda i: (i, 0)) for a in ws]
    shapes = [jax.ShapeDtypeStruct(a.shape, F32) for a in ws]
    return _pcall(
        body, name=name, grid=(row_blocks,), in_specs=specs * 4, out_specs=specs * 4, out_shape=shapes * 4,
        compiler_params=_params(1),
    )(*ws, *gs, *ms, *vs)


def kernel(x, ffn1_norm, ffn1_w_gate, ffn1_w_up, ffn1_w_down, mix_norm, w_in, conv_w, attn_sinks, w_out, ffn2_norm, ffn2_w_gate, ffn2_w_up, ffn2_w_down, final_norm, loss_target, m_ffn1_norm, m_ffn1_w_gate, m_ffn1_w_up, m_ffn1_w_down, m_mix_norm, m_w_in, m_conv_w, m_attn_sinks, m_w_out, m_ffn2_norm, m_ffn2_w_gate, m_ffn2_w_up, m_ffn2_w_down, m_final_norm, v_ffn1_norm, v_ffn1_w_gate, v_ffn1_w_up, v_ffn1_w_down, v_mix_norm, v_w_in, v_conv_w, v_attn_sinks, v_w_out, v_ffn2_norm, v_ffn2_w_gate, v_ffn2_w_up, v_ffn2_w_down, v_final_norm):
    T, D = x.shape[1], x.shape[2]
    chip = (2 * lax.axis_index("x") + lax.axis_index("y")).astype(jnp.int32)
    core = lax.axis_index("c").astype(jnp.int32)
    place = jnp.stack([chip, core])
    x0 = x[0]
    target = loss_target[0]
    gf = final_norm.reshape(1, D)

    tr = lambda w: jnp.swapaxes(w[0], 0, 1)
    big = [tr(ffn1_w_gate), tr(ffn1_w_up), ffn1_w_down[0], tr(w_in), w_out[0], tr(ffn2_w_gate), tr(ffn2_w_up), ffn2_w_down[0]]
    transposed = [True, True, False, True, False, True, True, False]
    own_b = [w.astype(BF16) for w in big]

    def whole(gathered, own):
        return lax.dynamic_update_slice(gathered, own[None], (chip, 0, 0)).reshape(-1, D)

    got1 = _run_comm(_gather_plan(own_b[0:3]), "gather_ffn1")
    wg1, wu1, wd1 = (whole(g, o) for g, o in zip(got1, own_b[0:3]))
    tab = _rope_tables(T)

    res = _ffn_fwd(x0, ffn1_norm, wg1, wu1, wd1, "ffn1_fwd", _gather_plan(own_b[3:8], [conv_w[0]]))
    x1, h1, gate1, up1, act1 = res[:5]
    win, wout, wg2, wu2, wd2 = (whole(g, o) for g, o in zip(res[5:10], own_b[3:8]))
    convw4 = lax.dynamic_update_slice(res[10], conv_w, (chip, 0, 0))
    convw = jnp.transpose(convw4, (1, 0, 2)).reshape(3, -1)
    z, hm = _norm_matmul(x1, mix_norm, win, "mix_in_fwd")
    ymix = _mix_core_fwd(z, tab, convw, attn_sinks, "mix_core_fwd")
    x2 = _matmul_residual(ymix, wout, x1, "mix_out_fwd")
    dx3, h2, gate2, up2, act2, dgf, loss_part = _ffn_fwd(x2, ffn2_norm, wg2, wu2, wd2, "ffn2_fwd", head=(gf, target))

    dx2, dyb2, dgate2, dup2, dg2 = _ffn_bwd(dx3, x2, ffn2_norm, gate2, up2, wg2, wu2, wd2, "ffn2_bwd")
    dymix, dx2b = _matmul_nt(dx2, wout, "mix_out_bwd")
    dz, dcw, dsk = _mix_core_bwd(z, dymix, tab, convw, attn_sinks, "mix_core_bwd")
    dx1, dgm = _matmul_norm_bwd(dz, win, x1, mix_norm, dx2, "mix_in_bwd")
    dx0, dyb1, dgate1, dup1, dg1 = _ffn_bwd(dx1, x0, ffn1_norm, gate1, up1, wg1, wu1, wd1, "ffn1_bwd")

    pad = lambda a: jnp.pad(a, ((0, 0), (0, LANES - a.shape[1])))
    vec = jnp.concatenate([dg1, dgm, dg2, dgf, dcw[0:3].reshape(1, -1), pad(dsk[:, 0].reshape(1, -1)),
                           pad(loss_part[:, 0:1])], axis=1)

    jobs = [("ffn2_dwg", dgate2, h2, 5), ("ffn2_dwu", dup2, h2, 6), ("ffn2_dwd", act2, dyb2, 7),
            ("ffn1_dwg", dgate1, h1, 0), ("ffn1_dwu", dup1, h1, 1), ("ffn1_dwd", act1, dyb1, 2),
            ("mix_dwin", dz, hm, 3), ("mix_dwout", ymix, dx2b, 4)]
    n_jobs = len(jobs)
    grad, grad_b, from_sib, pair_b, from_chips, half, g_big = ({} for _ in range(7))

    def stage_plans(t):
        plans, takers = [], []
        if 0 <= t - 1 < n_jobs:
            plans.append(_sibling_plan([grad_b[t - 1]]))
            takers.append((from_sib, t - 1))
        if 0 <= t - 2 < n_jobs:
            plans.append(_scatter_plan([pair_b[t - 2]]))
            takers.append((from_chips, t - 2))
        if 0 <= t - 3 < n_jobs:
            plans.append(_join_plan([half[t - 3]]))
            takers.append((g_big, jobs[t - 3][3]))
        return plans, takers

    def after_stage(t, landed, takers):
        for (store, key), arr in zip(takers, landed):
            store[key] = arr
        pair = (grad[t - 1], from_sib[t - 1]) if 0 <= t - 1 < n_jobs else None
        chip = (grad[t - 2], from_sib[t - 2], from_chips[t - 2]) if 0 <= t - 2 < n_jobs else None
        if pair or chip:
            sums = list(_local_sums(pair, chip, place, f"local_sums_{t}"))
            if pair:
                pair_b[t - 1] = sums.pop(0)
            if chip:
                half[t - 2] = sums.pop(0)

    for t, (name_, a, b, _) in enumerate(jobs):
        plans, takers = stage_plans(t)
        if t == 0:
            plans.append(_all_gather_plan(jnp.pad(vec, ((0, 7), (0, 0)))))
        res = _matmul_tn(a, b, DW_ROW_SPLIT, name_, _merge_plans(plans))
        grad[t], grad_b[t] = (r.reshape(N_CHIPS, -1, D) for r in res[:2])
        landed = list(res[2:])
        if t == 0:
            vec_blocks = landed.pop()
        after_stage(t, landed, takers)

    ws = big
    ms = [tr(m_ffn1_w_gate), tr(m_ffn1_w_up), m_ffn1_w_down[0], tr(m_w_in), m_w_out[0], tr(m_ffn2_w_gate), tr(m_ffn2_w_up), m_ffn2_w_down[0]]
    vs = [tr(v_ffn1_w_gate), tr(v_ffn1_w_up), v_ffn1_w_down[0], tr(v_w_in), v_w_out[0], tr(v_ffn2_w_gate), tr(v_ffn2_w_up), v_ffn2_w_down[0]]
    for t in range(n_jobs, n_jobs + 3):
        plans, takers = stage_plans(t)
        after_stage(t, _run_comm(_merge_plans(plans), f"grads_tail_{t - n_jobs}"), takers)
    upd = {}
    for name_, idx in (("adamw_a", [0, 1, 2, 4]), ("adamw_b", [3, 5, 6, 7])):
        k = len(idx)
        res = _adamw([ws[i] for i in idx], [g_big[i] for i in idx], [ms[i] for i in idx], [vs[i] for i in idx], 8, name_)
        for j, i in enumerate(idx):
            upd[i] = (res[j], res[k + j], res[2 * k + j])
            g_big[i] = res[3 * k + j]

    total = _sum_devices(vec_blocks, "small_sum")[0:1]
    g_n1, g_nm, g_n2, g_nf = (total[:, k * D:(k + 1) * D] for k in range(4))
    cw_full = total[:, 4 * D:4 * D + 3 * CONV_WIDTH].reshape(3, CONV_WIDTH)
    cq = CONV_WIDTH // N_CHIPS
    g_cw = lax.dynamic_slice(cw_full, (0, chip * cq), (3, cq))
    off = 4 * D + 3 * CONV_WIDTH
    g_sk = total[:, off:off + N_Q_HEADS]
    loss = total[0, off + LANES]

    sw = [ffn1_norm, mix_norm, conv_w[0], attn_sinks, ffn2_norm, gf]
    sg = [g_n1, g_nm, g_cw, g_sk, g_n2, g_nf]
    sm = [m_ffn1_norm, m_mix_norm, m_conv_w[0], m_attn_sinks, m_ffn2_norm, m_final_norm.reshape(1, D)]
    sv = [v_ffn1_norm, v_mix_norm, v_conv_w[0], v_attn_sinks, v_ffn2_norm, v_final_norm.reshape(1, D)]
    sres = _adamw(sw, sg, sm, sv, 1, "adamw_small")
    supd = [(sres[j], sres[6 + j], sres[12 + j]) for j in range(6)]

    order = [("s", 0), ("b", 0), ("b", 1), ("b", 2), ("s", 1), ("b", 3), ("s", 2), ("s", 3), ("b", 4),
             ("s", 4), ("b", 5), ("b", 6), ("b", 7), ("s", 5)]

    def leaf(kind, i, which):
        if kind == "b":
            a = g_big[i] if which == 0 else upd[i][which - 1]
            return (jnp.swapaxes(a, 0, 1) if transposed[i] else a)[None]
        a = sg[i] if which == 0 else supd[i][which - 1]
        if i == 2:
            return a[None]
        if i == 5:
            return a.reshape(D)
        return a

    outs = [loss, dx0[None]]
    for which in range(4):
        outs += [leaf(kind, i, which) for kind, i in order]
    return tuple(outs)
```

```python
import functools

import jax
import jax.numpy as jnp
import numpy as np
from jax import lax
from jax.experimental import pallas as pl
from jax.experimental.pallas import tpu as pltpu

F32 = jnp.float32
BF16 = jnp.bfloat16
MESH = pl.DeviceIdType.MESH

CONV_WIDTH = 512
N_Q_HEADS = 8
HEAD_DIM = 64
BLOCK = 128
ROPE_THETA = 500000.0
ROT_DIM = 16
RMS_EPS = 1e-5
MASK_VALUE = -1e30
ATTN_SCALE = HEAD_DIM ** -0.5
FFN_RES_SCALE = 0.5
ADAM_LR = 0.001
ADAM_B1 = 0.9
ADAM_B2 = 0.999
ADAM_EPS = 1e-08
ADAM_WD = 0.01
ADAM_STEP = 10

N_CHIPS = 4
N_DEV = 8
LANES = 128
VMEM_LIMIT = 56 * 1024 * 1024

_pcall = pl.pallas_call
HBM_SPEC = pl.BlockSpec(memory_space=pltpu.HBM)
ANY_SPEC = pl.BlockSpec(memory_space=pl.ANY)


def _params(n_axes, vmem=VMEM_LIMIT):
    return pltpu.CompilerParams(dimension_semantics=("arbitrary",) * n_axes, vmem_limit_bytes=vmem)


def _dot(a, b):
    return jnp.dot(a, b, preferred_element_type=F32)


def _dot_nt(a, b):
    return lax.dot_general(a, b, (((1,), (1,)), ((), ())), preferred_element_type=F32)


def _dot_tn(a, b):
    return lax.dot_general(a, b, (((0,), (0,)), ((), ())), preferred_element_type=F32)


def _rms_inv(x):
    return lax.rsqrt(jnp.mean(x * x, axis=-1, keepdims=True) + RMS_EPS)


def _norm_bwd(dh, x, g):
    inv = _rms_inv(x)
    xhat = x * inv
    dg = jnp.sum(dh * xhat, axis=0, keepdims=True)
    dxhat = dh * g
    dx = inv * (dxhat - xhat * jnp.mean(dxhat * xhat, axis=-1, keepdims=True))
    return dx, dg


def _place():
    x, y, c = lax.axis_index("x"), lax.axis_index("y"), lax.axis_index("c")
    chips = [(1 - x, y), (x, 1 - y), (1 - x, 1 - y)]
    return x, y, c, chips


class _Plan:
    def __init__(self, arrays, out_shapes, n_sems, start, finish, middle=None, aliases=None):
        self.arrays, self.out_shapes, self.n_sems = list(arrays), list(out_shapes), n_sems
        self.start, self.finish, self.middle = start, finish, middle
        self.aliases = dict(aliases or {})

    def specs(self):
        k = len(self.arrays)
        sems = [pltpu.SemaphoreType.DMA((self.n_sems,)), pltpu.SemaphoreType.DMA((self.n_sems,))]
        return [HBM_SPEC] * k, [HBM_SPEC] * len(self.out_shapes), self.out_shapes, sems


class _SemSlice:
    def __init__(self, ref, offset):
        self.ref, self.offset = ref, offset

    @property
    def at(self):
        return self

    def __getitem__(self, k):
        return self.ref.at[k + self.offset]


def _merge_plans(plans):
    plans = [p for p in plans if p is not None]
    if len(plans) <= 1:
        return plans[0] if plans else None
    arrays, shapes, aliases, spans, n_sems = [], [], {}, [], 0
    for p in plans:
        a0, o0 = len(arrays), len(shapes)
        spans.append((a0, a0 + len(p.arrays), o0, o0 + len(p.out_shapes), n_sems))
        aliases.update({a0 + i: o0 + j for i, j in p.aliases.items()})
        arrays += p.arrays
        shapes += p.out_shapes
        n_sems += p.n_sems

    def run(which):
        def fn(ins, outs, send_sems, recv_sems):
            for p, (a0, a1, o0, o1, s0) in zip(plans, spans):
                part = getattr(p, which)
                if part is not None:
                    part(ins[a0:a1], outs[o0:o1], _SemSlice(send_sems, s0), _SemSlice(recv_sems, s0))
        return fn

    middle = run("middle") if any(p.middle is not None for p in plans) else None
    return _Plan(arrays, shapes, n_sems, run("start"), run("finish"), middle, aliases)


def _sibling_plan(grads_b):
    n = len(grads_b)

    def copies(ins, outs, send_sems, recv_sems):
        x, y, c, _ = _place()

        def copy(t):
            half = ins[t].shape[1] // 2
            return pltpu.make_async_remote_copy(
                src_ref=ins[t].at[:, pl.ds(pl.multiple_of((1 - c) * half, 16), half), :], dst_ref=outs[t],
                send_sem=send_sems.at[t], recv_sem=recv_sems.at[t], device_id=(x, y, 1 - c), device_id_type=MESH)

        return [copy(t) for t in range(n)]

    def start(*refs):
        for cp in copies(*refs):
            cp.start()

    def finish(*refs):
        for cp in copies(*refs):
            cp.wait()

    shapes = [jax.ShapeDtypeStruct((g.shape[0], g.shape[1] // 2, g.shape[2]), g.dtype) for g in grads_b]
    return _Plan(grads_b, shapes, n, start, finish)


def _scatter_plan(parts_b):
    n = len(parts_b)

    def copies(ins, outs, send_sems, recv_sems):
        x, y, c, chips = _place()

        def copy(t, j):
            px, py = chips[j]
            return pltpu.make_async_remote_copy(
                src_ref=ins[t].at[2 * px + py], dst_ref=outs[t].at[j], send_sem=send_sems.at[3 * t + j],
                recv_sem=recv_sems.at[3 * t + j], device_id=(px, py, c), device_id_type=MESH)

        return [copy(t, j) for t in range(n) for j in range(3)]

    def start(*refs):
        for cp in copies(*refs):
            cp.start()

    def finish(*refs):
        for cp in copies(*refs):
            cp.wait()

    shapes = [jax.ShapeDtypeStruct((3, *p.shape[1:]), p.dtype) for p in parts_b]
    return _Plan(parts_b, shapes, 3 * n, start, finish)


def _gather_plan(shards, small=()):
    n, ns = len(shards), len(small)
    per = 8

    def parts(ins, outs, send_sems, recv_sems):
        x, y, c, chips = _place()
        me = 2 * x + y
        blocks = [2 * px + py for px, py in chips]

        def rows(t, core, piece=None):
            half = ins[t].shape[0] // 2
            if piece is None:
                return pl.ds(pl.multiple_of(core * half, 16), half)
            return pl.ds(pl.multiple_of(core * half + piece * (half // 2), 16), half // 2)

        def remote(src, dst, k, device):
            return pltpu.make_async_remote_copy(src_ref=src, dst_ref=dst, send_sem=send_sems.at[k],
                                                recv_sem=recv_sems.at[k], device_id=device, device_id_type=MESH)

        def first(t, j, block, core):
            return remote(ins[t].at[rows(t, core), :], outs[t].at[block, rows(t, core), :], per * t + j, (*chips[j], c))

        def relay(t, j, block, core):
            ref = outs[t].at[block, rows(t, core, j), :]
            return remote(ref, ref, per * t + 2 + j, (*chips[j], c))

        def passed(t, k, block, core, piece=None):
            ref = outs[t].at[block, rows(t, core, piece), :]
            return remote(ref, ref, per * t + 4 + k, (x, y, 1 - c))

        def whole(s, j, block):
            return remote(ins[n + s], outs[n + s].at[block], per * n + 3 * s + j, (*chips[j], c))

        return c, me, blocks, first, relay, passed, whole

    def start(*refs):
        c, me, _, first, _, _, whole = parts(*refs)
        for t in range(n):
            for j in range(2):
                first(t, j, me, c).start()
        for s in range(ns):
            for j in range(3):
                whole(s, j, me).start()

    def middle(*refs):
        c, _, blocks, first, relay, passed, _ = parts(*refs)
        for t in range(n):
            for j in range(2):
                first(t, j, blocks[j], c).wait_recv()
                passed(t, j, blocks[j], c).start()
                relay(t, 1 - j, blocks[j], c).start()

    def finish(*refs):
        c, me, blocks, first, relay, passed, whole = parts(*refs)
        for t in range(n):
            for j in range(2):
                relay(t, j, blocks[2], c).wait_recv()
                passed(t, 2 + j, blocks[2], c, j).start()
        for t in range(n):
            for j in range(2):
                passed(t, j, blocks[j], 1 - c).wait_recv()
                passed(t, 2 + j, blocks[2], 1 - c, j).wait_recv()
        for s in range(ns):
            for j in range(3):
                whole(s, j, blocks[j]).wait_recv()
        for t in range(n):
            for j in range(2):
                first(t, j, me, c).wait_send()
                relay(t, 1 - j, blocks[j], c).wait_send()
                passed(t, j, blocks[j], c).wait_send()
                passed(t, 2 + j, blocks[2], c, j).wait_send()
        for s in range(ns):
            for j in range(3):
                whole(s, j, me).wait_send()

    arrays = [*shards, *small]
    shapes = [jax.ShapeDtypeStruct((N_CHIPS, *a.shape), a.dtype) for a in arrays]
    return _Plan(arrays, shapes, per * n + 3 * ns, start, finish, middle)


def _run_comm(plan, name):
    k = len(plan.arrays)
    in_specs, out_specs, out_shape, sems = plan.specs()

    def body(*refs):
        cr = (refs[:k], refs[k:k + len(out_shape)], refs[-2], refs[-1])
        plan.start(*cr)
        if plan.middle is not None:
            plan.middle(*cr)
        plan.finish(*cr)

    return _pcall(body, name=name, in_specs=in_specs, out_specs=out_specs, out_shape=out_shape,
                  input_output_aliases=plan.aliases, scratch_shapes=sems)(*plan.arrays)


def _carried(plan, in_specs, out_specs, out_shape, scratch):
    aliases = {}
    if plan is not None:
        p_in, p_out, p_shape, p_sems = plan.specs()
        aliases = {len(in_specs) + i: len(out_specs) + j for i, j in plan.aliases.items()}
        in_specs, out_specs = in_specs + p_in, out_specs + p_out
        out_shape, scratch = out_shape + p_shape, scratch + p_sems
    return dict(in_specs=in_specs, out_specs=out_specs, out_shape=out_shape, scratch_shapes=scratch,
                input_output_aliases=aliases)


def _unpack(refs, n_in, n_out, plan):
    k_in = len(plan.arrays) if plan else 0
    k_out = len(plan.out_shapes) if plan else 0
    ins = refs[:n_in]
    outs = refs[n_in + k_in:n_in + k_in + n_out]
    rest = refs[n_in + k_in + n_out + k_out:]
    if plan is None:
        return ins, outs, rest, None
    cr = (refs[n_in:n_in + k_in], refs[n_in + k_in + n_out:n_in + k_in + n_out + k_out], rest[-2], rest[-1])
    return ins, outs, rest[:-2], cr


def _hook(plan, cr, which, cond):
    fn = getattr(plan, which) if plan is not None else None
    if fn is not None:
        pl.when(cond)(lambda: fn(*cr))


def _join_plan(shards):
    n = len(shards)

    def copy(ins, outs, send_sems, recv_sems, t, core):
        x, y, c, _ = _place()
        half = ins[t].shape[0] // 2
        rows = pl.ds(pl.multiple_of(core * half, 8), half)
        return pltpu.make_async_remote_copy(
            src_ref=ins[t].at[rows, :], dst_ref=outs[t].at[rows, :], send_sem=send_sems.at[t],
            recv_sem=recv_sems.at[t], device_id=(x, y, 1 - c), device_id_type=MESH)

    def start(*refs):
        c = lax.axis_index("c")
        for t in range(n):
            copy(*refs, t, c).start()

    def finish(*refs):
        c = lax.axis_index("c")
        for t in range(n):
            copy(*refs, t, 1 - c).wait_recv()
        for t in range(n):
            copy(*refs, t, c).wait_send()

    shapes = [jax.ShapeDtypeStruct(s.shape, s.dtype) for s in shards]
    return _Plan(shards, shapes, n, start, finish, aliases={t: t for t in range(n)})


def _all_gather_plan(vec):
    def parts(ins, outs, send_sems, recv_sems):
        x, y, c, _ = _place()
        me = 4 * x + 2 * y + c
        rel = [((k >> 2) & 1, (k >> 1) & 1, k & 1) for k in range(1, N_DEV)]

        def peer(k):
            fx, fy, fc = rel[k]
            return (x ^ fx, y ^ fy, c ^ fc)

        def copy(k, dev):
            return pltpu.make_async_remote_copy(
                src_ref=ins[0], dst_ref=outs[0].at[dev], send_sem=send_sems.at[k], recv_sem=recv_sems.at[k],
                device_id=peer(k), device_id_type=MESH)

        mine = pltpu.make_async_copy(ins[0], outs[0].at[me], send_sems.at[N_DEV - 1])
        return me, peer, copy, mine

    def start(*refs):
        me, _, copy, mine = parts(*refs)
        mine.start()
        for k in range(N_DEV - 1):
            copy(k, me).start()

    def finish(*refs):
        me, peer, copy, mine = parts(*refs)
        for k in range(N_DEV - 1):
            px, py, pc = peer(k)
            copy(k, 4 * px + 2 * py + pc).wait_recv()
        for k in range(N_DEV - 1):
            copy(k, me).wait_send()
        mine.wait()

    return _Plan([vec], [jax.ShapeDtypeStruct((N_DEV, *vec.shape), vec.dtype)], N_DEV, start, finish)


def _sum_devices(blocks, name):
    def body(b_ref, o_ref):
        total = b_ref[0]
        for dev in range(1, N_DEV):
            total = total + b_ref[dev]
        o_ref[...] = total

    return _pcall(body, name=name, in_specs=[pl.BlockSpec(memory_space=pltpu.VMEM)],
                  out_specs=pl.BlockSpec(memory_space=pltpu.VMEM),
                  out_shape=jax.ShapeDtypeStruct(blocks.shape[1:], F32))(blocks)


TOKEN_TILE = 512
PROJ_TOKEN_TILE = 1024
BWD_VMEM_LIMIT = 62 * 1024 * 1024
DW_TOKEN_TILE = 2048
DW_ROW_SPLIT = 2
MXU_COLS = 256


def _chunks(n):
    out, c0 = [], 0
    while c0 < n:
        size = min(MXU_COLS, n - c0)
        out.append((c0, size))
        c0 += size
    return out


def _load_weights(hbm_refs, vmem_refs, sems):
    copies = [pltpu.make_async_copy(h, v, sems.at[k]) for k, (h, v) in enumerate(zip(hbm_refs, vmem_refs))]
    for cp in copies:
        cp.start()
    for cp in copies:
        cp.wait()


def _ffn_fwd(x, g, wgt, wut, wd, name, plan=None, head=None):
    T, D = x.shape
    F = wgt.shape[0]
    tm = min(T, TOKEN_TILE)
    ni = T // tm
    n_head = 2 if head is not None else 0

    def body(*refs):
        ins, outs, scratch, cr = _unpack(refs, 5 + n_head, 5 + n_head, plan)
        x_ref, g_ref, wg_hbm, wu_hbm, wd_hbm = ins[:5]
        xo_ref, h_ref, gate_ref, up_ref, act_ref = outs[:5]
        wg_ref, wu_ref, wd_ref, sems = scratch
        i = pl.program_id(0)
        _hook(plan, cr, "start", i == 0)

        @pl.when(i == 0)
        def _():
            _load_weights((wg_hbm, wu_hbm, wd_hbm), (wg_ref, wu_ref, wd_ref), sems)

        xv = x_ref[...]
        h = ((xv * _rms_inv(xv)) * g_ref[...]).astype(BF16)
        h_ref[...] = h
        for c0, size in _chunks(F):
            gate = _dot_nt(h, wg_ref[c0:c0 + size, :])
            up = _dot_nt(h, wu_ref[c0:c0 + size, :])
            gate_ref[:, c0:c0 + size] = gate.astype(BF16)
            up_ref[:, c0:c0 + size] = up.astype(BF16)
            act_ref[:, c0:c0 + size] = (gate * jax.nn.sigmoid(gate) * up).astype(BF16)
        y = x_ref[...] + FFN_RES_SCALE * _dot(act_ref[...], wd_ref[...])
        if head is None:
            xo_ref[...] = y
        else:
            gf_ref, t_ref = ins[5:]
            dgf_ref, loss_ref = outs[5:]

            @pl.when(i == 0)
            def _():
                dgf_ref[...] = jnp.zeros_like(dgf_ref)
                loss_ref[...] = jnp.zeros_like(loss_ref)

            gf = gf_ref[...]
            diff = (y * _rms_inv(y)) * gf - t_ref[...]
            loss_ref[...] += 0.5 * jnp.sum(jnp.mean(diff * diff, axis=-1, keepdims=True))
            dy, dgf = _norm_bwd(diff * (1.0 / D), y, gf)
            xo_ref[...] = dy
            dgf_ref[...] += dgf
        _hook(plan, cr, "middle", i == ni // 2)
        _hook(plan, cr, "finish", i == ni - 1)

    const = lambda shape: pl.BlockSpec(shape, lambda i: (0, 0))
    rows = lambda width: pl.BlockSpec((tm, width), lambda i: (i, 0))
    in_specs = [rows(D), const((1, D)), ANY_SPEC, ANY_SPEC, ANY_SPEC]
    out_specs = [rows(D), rows(D), rows(F), rows(F), rows(F)]
    out_shape = [jax.ShapeDtypeStruct((T, D), F32), jax.ShapeDtypeStruct((T, D), BF16),
                 jax.ShapeDtypeStruct((T, F), BF16), jax.ShapeDtypeStruct((T, F), BF16), jax.ShapeDtypeStruct((T, F), BF16)]
    if head is not None:
        in_specs += [const((1, D)), rows(D)]
        out_specs += [const((1, D)), const((1, LANES))]
        out_shape += [jax.ShapeDtypeStruct((1, D), F32), jax.ShapeDtypeStruct((1, LANES), F32)]
    io = _carried(plan, in_specs, out_specs, out_shape,
                  [pltpu.VMEM((F, D), BF16), pltpu.VMEM((F, D), BF16), pltpu.VMEM((F, D), BF16),
                   pltpu.SemaphoreType.DMA((3,))])
    return _pcall(
        body, name=name, grid=(ni,), compiler_params=_params(1), **io,
    )(x, g, wgt, wut, wd, *(head or ()), *(plan.arrays if plan else ()))


def _ffn_bwd(dy, x, g, gate, up, wgt, wut, wd, name, plan=None):
    T, D = x.shape
    F = wgt.shape[0]
    tm = min(T, TOKEN_TILE)
    ni = T // tm

    def body(*refs):
        ins, outs, scratch, cr = _unpack(refs, 8, 5, plan)
        dy_ref, x_ref, g_ref, gate_ref, up_ref, wg_hbm, wu_hbm, wd_hbm = ins
        dx_ref, dyb_ref, dgate_ref, dup_ref, dg_ref = outs
        wg_ref, wu_ref, wd_ref, sems = scratch
        i = pl.program_id(0)
        _hook(plan, cr, "start", i == 0)

        @pl.when(i == 0)
        def _():
            _load_weights((wg_hbm, wu_hbm, wd_hbm), (wg_ref, wu_ref, wd_ref), sems)
            dg_ref[...] = jnp.zeros_like(dg_ref)

        dyb = (FFN_RES_SCALE * dy_ref[...]).astype(BF16)
        dyb_ref[...] = dyb
        for c0, size in _chunks(F):
            dact = _dot_nt(dyb, wd_ref[c0:c0 + size, :])
            gt = gate_ref[:, c0:c0 + size].astype(F32)
            u = up_ref[:, c0:c0 + size].astype(F32)
            sig = jax.nn.sigmoid(gt)
            dup_ref[:, c0:c0 + size] = (dact * (gt * sig)).astype(BF16)
            dgate_ref[:, c0:c0 + size] = (dact * u * (sig * (1.0 + gt * (1.0 - sig)))).astype(BF16)
        dh = _dot(dgate_ref[...], wg_ref[...]) + _dot(dup_ref[...], wu_ref[...])
        dxn, dg = _norm_bwd(dh, x_ref[...], g_ref[...])
        dx_ref[...] = dy_ref[...] + dxn
        dg_ref[...] += dg
        _hook(plan, cr, "finish", i == ni - 1)

    io = _carried(
        plan,
        [pl.BlockSpec((tm, D), lambda i: (i, 0)), pl.BlockSpec((tm, D), lambda i: (i, 0)),
         pl.BlockSpec((1, D), lambda i: (0, 0)),
         pl.BlockSpec((tm, F), lambda i: (i, 0)), pl.BlockSpec((tm, F), lambda i: (i, 0)),
         ANY_SPEC, ANY_SPEC, ANY_SPEC],
        [pl.BlockSpec((tm, D), lambda i: (i, 0)), pl.BlockSpec((tm, D), lambda i: (i, 0)),
         pl.BlockSpec((tm, F), lambda i: (i, 0)), pl.BlockSpec((tm, F), lambda i: (i, 0)),
         pl.BlockSpec((1, D), lambda i: (0, 0))],
        [jax.ShapeDtypeStruct((T, D), F32), jax.ShapeDtypeStruct((T, D), BF16),
         jax.ShapeDtypeStruct((T, F), BF16), jax.ShapeDtypeStruct((T, F), BF16), jax.ShapeDtypeStruct((1, D), F32)],
        [pltpu.VMEM((F, D), BF16), pltpu.VMEM((F, D), BF16), pltpu.VMEM((F, D), BF16),
         pltpu.SemaphoreType.DMA((3,))])
    return _pcall(
        body, name=name, grid=(ni,), compiler_params=_params(1, BWD_VMEM_LIMIT), **io,
    )(dy, x, g, gate, up, wgt, wut, wd, *(plan.arrays if plan else ()))


def _matmul_tn(a, b, row_split, name, plan=None):
    T, n1 = a.shape
    n2 = b.shape[1]
    tn = n1 // row_split
    tk = min(T, DW_TOKEN_TILE)
    nk = T // tk

    def body(*refs):
        (a_ref, b_ref), (o_ref, ob_ref), _, cr = _unpack(refs, 2, 2, plan)
        j = pl.program_id(0)
        k = pl.program_id(1)
        _hook(plan, cr, "start", jnp.logical_and(j == 0, k == 0))

        @pl.when(k == 0)
        def _():
            o_ref[...] = jnp.zeros_like(o_ref)

        o_ref[...] += _dot_tn(a_ref[...], b_ref[...])

        @pl.when(k == nk - 1)
        def _():
            ob_ref[...] = o_ref[...].astype(BF16)

        _hook(plan, cr, "finish", jnp.logical_and(j == row_split - 1, k == nk - 1))

    io = _carried(
        plan,
        [pl.BlockSpec((tk, tn), lambda j, k: (k, j)), pl.BlockSpec((tk, n2), lambda j, k: (k, 0))],
        [pl.BlockSpec((tn, n2), lambda j, k: (j, 0)), pl.BlockSpec((tn, n2), lambda j, k: (j, 0))],
        [jax.ShapeDtypeStruct((n1, n2), F32), jax.ShapeDtypeStruct((n1, n2), BF16)], [])
    return _pcall(
        body, name=name, grid=(row_split, nk), compiler_params=_params(2), **io,
    )(a, b, *(plan.arrays if plan else ()))


def _norm_matmul(x, g, wt, tab, name):
    T, D = x.shape
    n = wt.shape[0]
    tm = min(T, PROJ_TOKEN_TILE)

    def body(x_ref, g_ref, w_ref, tab_ref, z_ref, h_ref):
        xv = x_ref[...]
        h = ((xv * _rms_inv(xv)) * g_ref[...]).astype(BF16)
        h_ref[...] = h
        z = _dot_nt(h, w_ref[...])
        z_ref[:, 0:Z_Q] = z[:, 0:Z_Q]
        tab_v = tab_ref[...]
        for c0 in range(Z_Q, Z_V, LANES):
            z_ref[:, c0:c0 + LANES] = _rot(z[:, c0:c0 + LANES], tab_v)
        z_ref[:, Z_V:Z_END] = z[:, Z_V:Z_END]

    return _pcall(
        body, name=name, grid=(T // tm,),
        in_specs=[pl.BlockSpec((tm, D), lambda i: (i, 0)), pl.BlockSpec((1, D), lambda i: (0, 0)),
                  pl.BlockSpec((n, D), lambda i: (0, 0)), pl.BlockSpec((tm, 3 * LANES), lambda i: (i, 0))],
        out_specs=[pl.BlockSpec((tm, n), lambda i: (i, 0)), pl.BlockSpec((tm, D), lambda i: (i, 0))],
        out_shape=[jax.ShapeDtypeStruct((T, n), F32), jax.ShapeDtypeStruct((T, D), BF16)],
        compiler_params=_params(1),
    )(x, g, wt, tab)


def _matmul_residual(y, w, x, name):
    T, D = x.shape
    kdim = y.shape[1]
    tm = min(T, PROJ_TOKEN_TILE)

    def body(y_ref, w_ref, x_ref, o_ref):
        o_ref[...] = x_ref[...] + _dot(y_ref[...], w_ref[...])

    return _pcall(
        body, name=name, grid=(T // tm,),
        in_specs=[pl.BlockSpec((tm, kdim), lambda i: (i, 0)), pl.BlockSpec((kdim, D), lambda i: (0, 0)),
                  pl.BlockSpec((tm, D), lambda i: (i, 0))],
        out_specs=pl.BlockSpec((tm, D), lambda i: (i, 0)),
        out_shape=jax.ShapeDtypeStruct((T, D), F32),
        compiler_params=_params(1),
    )(y, w, x)


def _matmul_nt(dx, w, name, plan=None):
    T, D = dx.shape
    kdim = w.shape[0]
    tm = min(T, PROJ_TOKEN_TILE)
    ni = T // tm

    def body(*refs):
        (dx_ref, w_ref), (dy_ref, dxb_ref), _, cr = _unpack(refs, 2, 2, plan)
        i = pl.program_id(0)
        _hook(plan, cr, "start", i == 0)
        dxb = dx_ref[...].astype(BF16)
        dxb_ref[...] = dxb
        dy_ref[...] = _dot_nt(dxb, w_ref[...])
        _hook(plan, cr, "finish", i == ni - 1)

    io = _carried(
        plan,
        [pl.BlockSpec((tm, D), lambda i: (i, 0)), pl.BlockSpec((kdim, D), lambda i: (0, 0))],
        [pl.BlockSpec((tm, kdim), lambda i: (i, 0)), pl.BlockSpec((tm, D), lambda i: (i, 0))],
        [jax.ShapeDtypeStruct((T, kdim), F32), jax.ShapeDtypeStruct((T, D), BF16)], [])
    return _pcall(
        body, name=name, grid=(ni,), compiler_params=_params(1), **io,
    )(dx, w, *(plan.arrays if plan else ()))


def _matmul_norm_bwd(dz, wt, x, g, dres, name, plan=None):
    T, D = x.shape
    n = dz.shape[1]
    tm = min(T, PROJ_TOKEN_TILE)
    ni = T // tm

    def body(*refs):
        (dz_ref, w_ref, x_ref, g_ref, dres_ref), (dx_ref, dg_ref), _, cr = _unpack(refs, 5, 2, plan)
        i = pl.program_id(0)
        _hook(plan, cr, "start", i == 0)

        @pl.when(i == 0)
        def _():
            dg_ref[...] = jnp.zeros_like(dg_ref)

        dh = _dot(dz_ref[...], w_ref[...])
        dxn, dg = _norm_bwd(dh, x_ref[...], g_ref[...])
        dx_ref[...] = dres_ref[...] + dxn
        dg_ref[...] += dg
        _hook(plan, cr, "finish", i == ni - 1)

    io = _carried(
        plan,
        [pl.BlockSpec((tm, n), lambda i: (i, 0)), pl.BlockSpec((n, D), lambda i: (0, 0)),
         pl.BlockSpec((tm, D), lambda i: (i, 0)), pl.BlockSpec((1, D), lambda i: (0, 0)),
         pl.BlockSpec((tm, D), lambda i: (i, 0))],
        [pl.BlockSpec((tm, D), lambda i: (i, 0)), pl.BlockSpec((1, D), lambda i: (0, 0))],
        [jax.ShapeDtypeStruct((T, D), F32), jax.ShapeDtypeStruct((1, D), F32)], [])
    return _pcall(
        body, name=name, grid=(ni,), compiler_params=_params(1), **io,
    )(dz, wt, x, g, dres, *(plan.arrays if plan else ()))


Z_Q = 3 * CONV_WIDTH
Z_K = Z_Q + N_Q_HEADS * HEAD_DIM
Z_V = Z_K + LANES
Z_END = Z_V + LANES


def _rope_tables(T):
    half = ROT_DIM // 2
    inv_freq = ROPE_THETA ** (-jnp.arange(0, ROT_DIM, 2, dtype=F32) / ROT_DIM)
    ang = inv_freq[:, None] * jnp.arange(T, dtype=F32)[None, :]
    cos_sin = jnp.concatenate([jnp.cos(ang), jnp.sin(ang)], axis=0)
    select = np.zeros((2 * half, 3 * LANES), np.float32)
    const = np.zeros((1, 3 * LANES), np.float32)
    for lane in range(LANES):
        d = lane % HEAD_DIM
        if d < half:
            select[d, lane] = 1.0
            select[half + d, LANES + lane] = -1.0
        elif d < ROT_DIM:
            select[d - half, lane] = 1.0
            select[d, 2 * LANES + lane] = 1.0
        else:
            const[0, lane] = 1.0
    tab = lax.dot_general(cos_sin, jnp.asarray(select), (((0,), (0,)), ((), ())),
                          precision=lax.Precision.HIGHEST, preferred_element_type=F32)
    return tab + jnp.asarray(const)


def _tab3(tab):
    return tab[:, 0:LANES], tab[:, LANES:2 * LANES], tab[:, 2 * LANES:3 * LANES]


def _rot(x, tab):
    c, s1, s2 = _tab3(tab)
    return x * c + pltpu.roll(x, LANES - ROT_DIM // 2, 1) * s1 + pltpu.roll(x, ROT_DIM // 2, 1) * s2


def _rot_t(d, tab):
    c, s1, s2 = _tab3(tab)
    return d * c + pltpu.roll(d * s1, ROT_DIM // 2, 1) + pltpu.roll(d * s2, LANES - ROT_DIM // 2, 1)


def _head_pads(a):
    lo = lax.broadcasted_iota(jnp.int32, a.shape, 1) < HEAD_DIM
    nat0 = jnp.where(lo, a, 0.0)
    nat1 = jnp.where(lo, 0.0, a)
    return {
        (0, 0): nat0.astype(BF16), (0, 1): pltpu.roll(nat0, HEAD_DIM, 1).astype(BF16),
        (1, 0): pltpu.roll(nat1, HEAD_DIM, 1).astype(BF16), (1, 1): nat1.astype(BF16),
    }


def _from_pads(even, odd, kv):
    lo = lax.broadcasted_iota(jnp.int32, even.shape, 1) < HEAD_DIM
    if kv == 0:
        return jnp.where(lo, even + pltpu.roll(odd, HEAD_DIM, 1), 0.0)
    return jnp.where(lo, 0.0, pltpu.roll(even, HEAD_DIM, 1) + odd)


N_GROUPS = 4


def _group_head(g, r):
    kv, par = divmod(g, 2)
    return 2 * (2 * kv + r) + par


def _window_mask_t(has_prev):
    jj = lax.broadcasted_iota(jnp.int32, (2 * BLOCK, 2 * BLOCK), 0)
    ii = lax.broadcasted_iota(jnp.int32, (2 * BLOCK, 2 * BLOCK), 1) & (BLOCK - 1)
    rel = jj - BLOCK - ii
    return (rel <= 0) & (rel > -BLOCK) & ((jj >= BLOCK) | has_prev)


def _sink_row(sink_ref, g):
    lane = lax.broadcasted_iota(jnp.int32, (1, 2 * BLOCK), 1)
    return jnp.where(lane < BLOCK, sink_ref[0, _group_head(g, 0)], sink_ref[0, _group_head(g, 1)])


def _attn_probs_t(q2, kp, mask, sink_ref):
    out = []
    for kv in range(2):
        q_st = jnp.concatenate([q2[2 * kv], q2[2 * kv + 1]], axis=0)
        for par in range(2):
            s = jnp.where(mask, _dot_nt(kp[(kv, par)], q_st), MASK_VALUE)
            sink = _sink_row(sink_ref, 2 * kv + par)
            m = jnp.maximum(jnp.max(s, axis=0, keepdims=True), sink)
            p = jnp.exp(s - m)
            esink = jnp.exp(sink - m)
            rden = 1.0 / (jnp.sum(p, axis=0, keepdims=True) + esink)
            out.append((p * rden, esink * rden))
    return out


def _conv_taps(cg, u, cg_prev, u_prev, has_prev):
    vv = cg * u
    halo = jnp.where(has_prev, cg_prev * u_prev, 0.0)
    ext = jnp.concatenate([halo, vv], axis=0)
    rows = ext.shape[0]
    vv1 = pltpu.roll(ext, 1, 0)[8:rows]
    vv2 = pltpu.roll(ext, 2, 0)[8:rows]
    return vv, vv1, vv2


def _mix_specs(nb):
    cur = lambda n: jnp.minimum(n, nb - 1)
    prev = lambda n: jnp.maximum(jnp.minimum(n, nb - 1) - 1, 0)
    rows8_prev = lambda n: jnp.maximum(16 * jnp.minimum(n, nb - 1) - 1, 0)
    return cur, prev, [
        pl.BlockSpec((BLOCK, Z_END), lambda n: (cur(n), 0)),
        pl.BlockSpec((BLOCK, 2 * LANES), lambda n: (prev(n), Z_K // (2 * LANES))),
        pl.BlockSpec((8, CONV_WIDTH), lambda n: (rows8_prev(n), 1)),
        pl.BlockSpec((8, CONV_WIDTH), lambda n: (rows8_prev(n), 2)),
        pl.BlockSpec((BLOCK, 3 * LANES), lambda n: (cur(n), 0)),
        pl.BlockSpec((BLOCK, 3 * LANES), lambda n: (prev(n), 0)),
        pl.BlockSpec((3, CONV_WIDTH), lambda n: (0, 0)),
        pl.BlockSpec(memory_space=pltpu.SMEM),
    ]


def _mix_core_fwd(z, conv_w, sinks, name):
    T = z.shape[0]
    nb = T // BLOCK
    _, _, specs = _mix_specs(nb)
    specs = specs[:4] + specs[6:]

    def body(z_ref, zkvp_ref, cgp_ref, up_ref, cw_ref, sink_ref, y_ref):
        has_prev = pl.program_id(0) > 0
        bg = z_ref[:, 0:CONV_WIDTH]
        vv, vv1, vv2 = _conv_taps(z_ref[:, CONV_WIDTH:2 * CONV_WIDTH], z_ref[:, 2 * CONV_WIDTH:Z_Q],
                                  cgp_ref[...], up_ref[...], has_prev)
        conv = cw_ref[0:1, :] * vv2 + cw_ref[1:2, :] * vv1 + cw_ref[2:3, :] * vv
        y_ref[:, 0:CONV_WIDTH] = (bg * conv).astype(BF16)

        k_all = jnp.concatenate([zkvp_ref[:, 0:LANES], z_ref[:, Z_K:Z_V]], axis=0)
        v_all = jnp.concatenate([zkvp_ref[:, LANES:2 * LANES], z_ref[:, Z_V:Z_END]], axis=0)
        kp = _head_pads(k_all)
        vp = _head_pads(v_all)
        q2 = [(z_ref[:, Z_Q + LANES * c:Z_Q + LANES * (c + 1)] * ATTN_SCALE).astype(BF16) for c in range(N_Q_HEADS // 2)]
        probs = _attn_probs_t(q2, kp, _window_mask_t(has_prev), sink_ref)
        for kv in range(2):
            o_t = (_dot_tn(vp[(kv, 0)], probs[2 * kv][0].astype(BF16))
                   + _dot_tn(vp[(kv, 1)], probs[2 * kv + 1][0].astype(BF16)))
            for r in range(2):
                c = 2 * kv + r
                y_ref[:, CONV_WIDTH + LANES * c:CONV_WIDTH + LANES * (c + 1)] = o_t[:, BLOCK * r:BLOCK * (r + 1)].T.astype(BF16)

    return _pcall(
        body, name=name, grid=(nb,), in_specs=specs,
        out_specs=pl.BlockSpec((BLOCK, 2 * CONV_WIDTH), lambda n: (n, 0)),
        out_shape=jax.ShapeDtypeStruct((T, 2 * CONV_WIDTH), BF16),
        compiler_params=_params(1),
    )(z, z, z, z, conv_w, sinks)


def _mix_core_bwd(z, dy, tab, conv_w, sinks, name, plan=None):
    T = z.shape[0]
    nb = T // BLOCK
    cur, _, specs = _mix_specs(nb)
    rows8_next = lambda n: jnp.minimum(16 * (cur(n) + 1), 16 * nb - 1)
    specs = specs[:4] + [
        pl.BlockSpec((8, CONV_WIDTH), lambda n: (rows8_next(n), 0)),
        pl.BlockSpec((BLOCK, 2 * CONV_WIDTH), lambda n: (cur(n), 0)),
        pl.BlockSpec((8, CONV_WIDTH), lambda n: (rows8_next(n), 0)),
    ] + specs[4:]

    def body(*refs):
        ins, outs, scratch, cr = _unpack(refs, 11, 3, plan)
        z_ref, zkvp_ref, cgp_ref, up_ref, bgn_ref, dy_ref, dyn_ref, tab_ref, tabp_ref, cw_ref, sink_ref = ins
        dz_ref, dcw_ref, dsk_ref = outs
        main_ref, kv_ref = scratch
        n = pl.program_id(0)
        _hook(plan, cr, "start", n == 0)

        @pl.when(n == 0)
        def _():
            main_ref[...] = jnp.zeros_like(main_ref)
            kv_ref[...] = jnp.zeros_like(kv_ref)
            dcw_ref[...] = jnp.zeros_like(dcw_ref)
            dsk_ref[...] = jnp.zeros_like(dsk_ref)

        @pl.when(n < nb)
        def _():
            has_prev = n > 0
            has_next = n < nb - 1
            bg = z_ref[:, 0:CONV_WIDTH]
            cg = z_ref[:, CONV_WIDTH:2 * CONV_WIDTH]
            u = z_ref[:, 2 * CONV_WIDTH:Z_Q]
            vv, vv1, vv2 = _conv_taps(cg, u, cgp_ref[...], up_ref[...], has_prev)
            w0, w1, w2 = cw_ref[0:1, :], cw_ref[1:2, :], cw_ref[2:3, :]
            dyc = dy_ref[:, 0:CONV_WIDTH]
            dbg = dyc * (w0 * vv2 + w1 * vv1 + w2 * vv)
            dconv = dyc * bg
            dconv_next = jnp.where(has_next, dyn_ref[...] * bgn_ref[...], 0.0)
            ext = jnp.concatenate([dconv, dconv_next], axis=0)
            rows = ext.shape[0]
            dvv = w2 * dconv + w1 * pltpu.roll(ext, rows - 1, 0)[0:BLOCK] + w0 * pltpu.roll(ext, rows - 2, 0)[0:BLOCK]
            dcw_ref[0:1, :] += jnp.sum(dconv * vv2, axis=0, keepdims=True)
            dcw_ref[1:2, :] += jnp.sum(dconv * vv1, axis=0, keepdims=True)
            dcw_ref[2:3, :] += jnp.sum(dconv * vv, axis=0, keepdims=True)

            tab_c = tab_ref[...]
            tab_p = tabp_ref[...]
            k_all = jnp.concatenate([zkvp_ref[:, 0:LANES], z_ref[:, Z_K:Z_V]], axis=0)
            v_all = jnp.concatenate([zkvp_ref[:, LANES:2 * LANES], z_ref[:, Z_V:Z_END]], axis=0)
            kp = _head_pads(k_all)
            vp = _head_pads(v_all)
            chunks = range(N_Q_HEADS // 2)
            q2 = [(z_ref[:, Z_Q + LANES * c:Z_Q + LANES * (c + 1)] * ATTN_SCALE).astype(BF16) for c in chunks]
            do2 = [dy_ref[:, CONV_WIDTH + LANES * c:CONV_WIDTH + LANES * (c + 1)].astype(BF16) for c in chunks]
            probs = _attn_probs_t(q2, kp, _window_mask_t(has_prev), sink_ref)
            dq_chunks = []
            dk_nat = jnp.zeros((2 * BLOCK, LANES), F32)
            dv_nat = jnp.zeros((2 * BLOCK, LANES), F32)
            for kv in range(2):
                q_st = jnp.concatenate([q2[2 * kv], q2[2 * kv + 1]], axis=0)
                do_st = jnp.concatenate([do2[2 * kv], do2[2 * kv + 1]], axis=0)
                dq_t = jnp.zeros((LANES, 2 * BLOCK), F32)
                dk_par, dv_par = [], []
                for par in range(2):
                    g = 2 * kv + par
                    pr, psink = probs[g]
                    dp = _dot_nt(vp[(kv, par)], do_st)
                    delta = jnp.sum(dp * pr, axis=0, keepdims=True)
                    ds = (pr * (dp - delta)).astype(BF16)
                    dsink = -psink * delta
                    for r in range(2):
                        h = _group_head(g, r)
                        dsk_ref[h:h + 1, :] += jnp.sum(dsink[:, BLOCK * r:BLOCK * (r + 1)])
                    dq_t = dq_t + _dot_tn(kp[(kv, par)], ds)
                    dk_par.append(_dot(ds, q_st))
                    dv_par.append(_dot(pr.astype(BF16), do_st))
                for r in range(2):
                    dq_chunks.append(_rot_t(dq_t[:, BLOCK * r:BLOCK * (r + 1)].T * ATTN_SCALE, tab_c))
                dk_nat = dk_nat + _from_pads(dk_par[0], dk_par[1], kv)
                dv_nat = dv_nat + _from_pads(dv_par[0], dv_par[1], kv)

            dk_prev = _rot_t(kv_ref[:, 0:LANES] + dk_nat[0:BLOCK], tab_p)
            dv_prev = kv_ref[:, LANES:2 * LANES] + dv_nat[0:BLOCK]
            dz_ref[:, 0:Z_K] = main_ref[...]
            dz_ref[:, Z_K:Z_V] = dk_prev.astype(BF16)
            dz_ref[:, Z_V:Z_END] = dv_prev.astype(BF16)
            main_ref[:, 0:CONV_WIDTH] = dbg.astype(BF16)
            main_ref[:, CONV_WIDTH:2 * CONV_WIDTH] = (dvv * u).astype(BF16)
            main_ref[:, 2 * CONV_WIDTH:Z_Q] = (dvv * cg).astype(BF16)
            for c in range(N_Q_HEADS // 2):
                main_ref[:, Z_Q + LANES * c:Z_Q + LANES * (c + 1)] = dq_chunks[c].astype(BF16)
            kv_ref[:, 0:LANES] = dk_nat[BLOCK:2 * BLOCK]
            kv_ref[:, LANES:2 * LANES] = dv_nat[BLOCK:2 * BLOCK]

        @pl.when(n == nb)
        def _():
            dz_ref[:, 0:Z_K] = main_ref[...]
            dz_ref[:, Z_K:Z_V] = _rot_t(kv_ref[:, 0:LANES], tab_ref[...]).astype(BF16)
            dz_ref[:, Z_V:Z_END] = kv_ref[:, LANES:2 * LANES].astype(BF16)

        _hook(plan, cr, "finish", n == nb)

    io = _carried(
        plan, specs,
        [pl.BlockSpec((BLOCK, Z_END), lambda n: (jnp.maximum(n - 1, 0), 0)),
         pl.BlockSpec((8, CONV_WIDTH), lambda n: (0, 0)), pl.BlockSpec((8, LANES), lambda n: (0, 0))],
        [jax.ShapeDtypeStruct((T, Z_END), BF16), jax.ShapeDtypeStruct((8, CONV_WIDTH), F32),
         jax.ShapeDtypeStruct((8, LANES), F32)],
        [pltpu.VMEM((BLOCK, Z_K), BF16), pltpu.VMEM((BLOCK, 2 * LANES), F32)])
    return _pcall(
        body, name=name, grid=(nb + 1,), compiler_params=_params(1), **io,
    )(z, z, z, z, z, dy, dy, tab, tab, conv_w, sinks, *(plan.arrays if plan else ()))


def _local_sums(pair, chip, place, name):
    arrays, in_specs, out_specs, out_shape = [], [], [], []
    if pair is not None:
        g, sib = pair
        blk = (1, *sib.shape[1:])
        arrays += [g, sib]
        in_specs += [pl.BlockSpec(blk, lambda q, p: (q, p[1], 0)), pl.BlockSpec(blk, lambda q, p: (q, 0, 0))]
        out_specs.append(pl.BlockSpec(blk, lambda q, p: (q, 0, 0)))
        out_shape.append(jax.ShapeDtypeStruct(sib.shape, BF16))
    if chip is not None:
        g2, sib2, recv2 = chip
        blk = (1, *sib2.shape[1:])
        arrays += [g2, sib2, recv2]
        in_specs += [pl.BlockSpec(blk, lambda q, p: (p[0], p[1], 0)), pl.BlockSpec(blk, lambda q, p: (p[0], 0, 0)),
                     pl.BlockSpec(recv2.shape, lambda q, p: (0, 0, 0))]
        out_specs.append(pl.BlockSpec(sib2.shape[1:], lambda q, p: (p[1], 0)))
        out_shape.append(jax.ShapeDtypeStruct(g2.shape[1:], F32))

    def body(place_ref, *refs):
        refs = list(refs)
        ins, outs = refs[:len(arrays)], refs[len(arrays):]
        if pair is not None:
            g_ref, sib_ref = ins[:2]
            outs[0][...] = (g_ref[...] + sib_ref[...].astype(F32)).astype(BF16)
        if chip is not None:
            g_ref, sib_ref, recv_ref = ins[-3:]

            @pl.when(pl.program_id(0) == 0)
            def _():
                total = g_ref[0] + sib_ref[0].astype(F32)
                for j in range(3):
                    total = total + recv_ref[j].astype(F32)
                outs[-1][...] = total

    return _pcall(
        body, name=name,
        grid_spec=pltpu.PrefetchScalarGridSpec(num_scalar_prefetch=1, grid=(N_CHIPS,),
                                               in_specs=in_specs, out_specs=out_specs),
        out_shape=out_shape, compiler_params=_params(1),
    )(place, *arrays)


def _adamw_math(w, g, m, v):
    m = ADAM_B1 * m + (1.0 - ADAM_B1) * g
    v = ADAM_B2 * v + (1.0 - ADAM_B2) * (g * g)
    m_hat = m / (1.0 - ADAM_B1 ** ADAM_STEP)
    v_hat = v / (1.0 - ADAM_B2 ** ADAM_STEP)
    delta = -ADAM_LR * (m_hat / (jnp.sqrt(v_hat) + ADAM_EPS) + ADAM_WD * w)
    return delta, m, v


def _adamw(ws, gs, ms, vs, row_blocks, name):
    n = len(ws)

    def body(*refs):
        w, g, m, v = refs[:n], refs[n:2 * n], refs[2 * n:3 * n], refs[3 * n:4 * n]
        d, mo, vo, go = refs[4 * n:5 * n], refs[5 * n:6 * n], refs[6 * n:7 * n], refs[7 * n:]
        for t in range(n):
            gv = g[t][...]
            delta, m_new, v_new = _adamw_math(w[t][...], gv, m[t][...], v[t][...])
            d[t][...] = delta
            mo[t][...] = m_new
            vo[t][...] = v_new
            go[t][...] = gv

    specs = [pl.BlockSpec((a.shape[0] // row_blocks, a.shape[1]), lambda i: (i, 0)) for a in ws]
    shapes = [jax.ShapeDtypeStruct(a.shape, F32) for a in ws]
    return _pcall(
        body, name=name, grid=(row_blocks,), in_specs=specs * 4, out_specs=specs * 4, out_shape=shapes * 4,
        compiler_params=_params(1),
    )(*ws, *gs, *ms, *vs)


def kernel(x, ffn1_norm, ffn1_w_gate, ffn1_w_up, ffn1_w_down, mix_norm, w_in, conv_w, attn_sinks, w_out, ffn2_norm, ffn2_w_gate, ffn2_w_up, ffn2_w_down, final_norm, loss_target, m_ffn1_norm, m_ffn1_w_gate, m_ffn1_w_up, m_ffn1_w_down, m_mix_norm, m_w_in, m_conv_w, m_attn_sinks, m_w_out, m_ffn2_norm, m_ffn2_w_gate, m_ffn2_w_up, m_ffn2_w_down, m_final_norm, v_ffn1_norm, v_ffn1_w_gate, v_ffn1_w_up, v_ffn1_w_down, v_mix_norm, v_w_in, v_conv_w, v_attn_sinks, v_w_out, v_ffn2_norm, v_ffn2_w_gate, v_ffn2_w_up, v_ffn2_w_down, v_final_norm):
    T, D = x.shape[1], x.shape[2]
    chip = (2 * lax.axis_index("x") + lax.axis_index("y")).astype(jnp.int32)
    core = lax.axis_index("c").astype(jnp.int32)
    place = jnp.stack([chip, core])
    x0 = x[0]
    target = loss_target[0]
    gf = final_norm.reshape(1, D)

    tr = lambda w: jnp.swapaxes(w[0], 0, 1)
    big = [tr(ffn1_w_gate), tr(ffn1_w_up), ffn1_w_down[0], tr(w_in), w_out[0], tr(ffn2_w_gate), tr(ffn2_w_up), ffn2_w_down[0]]
    transposed = [True, True, False, True, False, True, True, False]
    own_b = [w.astype(BF16) for w in big]

    def whole(gathered, own):
        return lax.dynamic_update_slice(gathered, own[None], (chip, 0, 0)).reshape(-1, D)

    got1 = _run_comm(_gather_plan(own_b[0:3]), "gather_ffn1")
    wg1, wu1, wd1 = (whole(g, o) for g, o in zip(got1, own_b[0:3]))
    tab = _rope_tables(T)

    res = _ffn_fwd(x0, ffn1_norm, wg1, wu1, wd1, "ffn1_fwd", _gather_plan(own_b[3:8], [conv_w[0]]))
    x1, h1, gate1, up1, act1 = res[:5]
    win, wout, wg2, wu2, wd2 = (whole(g, o) for g, o in zip(res[5:10], own_b[3:8]))
    convw4 = lax.dynamic_update_slice(res[10], conv_w, (chip, 0, 0))
    convw = jnp.transpose(convw4, (1, 0, 2)).reshape(3, -1)
    z, hm = _norm_matmul(x1, mix_norm, win, tab, "mix_in_fwd")
    ymix = _mix_core_fwd(z, convw, attn_sinks, "mix_core_fwd")
    x2 = _matmul_residual(ymix, wout, x1, "mix_out_fwd")
    dx3, h2, gate2, up2, act2, dgf, loss_part = _ffn_fwd(x2, ffn2_norm, wg2, wu2, wd2, "ffn2_fwd", head=(gf, target))

    dx2, dyb2, dgate2, dup2, dg2 = _ffn_bwd(dx3, x2, ffn2_norm, gate2, up2, wg2, wu2, wd2, "ffn2_bwd")
    dymix, dx2b = _matmul_nt(dx2, wout, "mix_out_bwd")
    dz, dcw, dsk = _mix_core_bwd(z, dymix, tab, convw, attn_sinks, "mix_core_bwd")
    dx1, dgm = _matmul_norm_bwd(dz, win, x1, mix_norm, dx2, "mix_in_bwd")
    dx0, dyb1, dgate1, dup1, dg1 = _ffn_bwd(dx1, x0, ffn1_norm, gate1, up1, wg1, wu1, wd1, "ffn1_bwd")

    pad = lambda a: jnp.pad(a, ((0, 0), (0, LANES - a.shape[1])))
    vec = jnp.concatenate([dg1, dgm, dg2, dgf, dcw[0:3].reshape(1, -1), pad(dsk[:, 0].reshape(1, -1)),
                           pad(loss_part[:, 0:1])], axis=1)

    jobs = [("ffn2_dwg", dgate2, h2, 5), ("ffn2_dwu", dup2, h2, 6), ("ffn2_dwd", act2, dyb2, 7),
            ("ffn1_dwg", dgate1, h1, 0), ("ffn1_dwu", dup1, h1, 1), ("ffn1_dwd", act1, dyb1, 2),
            ("mix_dwin", dz, hm, 3), ("mix_dwout", ymix, dx2b, 4)]
    n_jobs = len(jobs)
    grad, grad_b, from_sib, pair_b, from_chips, half, g_big = ({} for _ in range(7))

    def stage_plans(t):
        plans, takers = [], []
        if 0 <= t - 1 < n_jobs:
            plans.append(_sibling_plan([grad_b[t - 1]]))
            takers.append((from_sib, t - 1))
        if 0 <= t - 2 < n_jobs:
            plans.append(_scatter_plan([pair_b[t - 2]]))
            takers.append((from_chips, t - 2))
        if 0 <= t - 3 < n_jobs:
            plans.append(_join_plan([half[t - 3]]))
            takers.append((g_big, jobs[t - 3][3]))
        return plans, takers

    def after_stage(t, landed, takers):
        for (store, key), arr in zip(takers, landed):
            store[key] = arr
        pair = (grad[t - 1], from_sib[t - 1]) if 0 <= t - 1 < n_jobs else None
        chip = (grad[t - 2], from_sib[t - 2], from_chips[t - 2]) if 0 <= t - 2 < n_jobs else None
        if pair or chip:
            sums = list(_local_sums(pair, chip, place, f"local_sums_{t}"))
            if pair:
                pair_b[t - 1] = sums.pop(0)
            if chip:
                half[t - 2] = sums.pop(0)

    for t, (name_, a, b, _) in enumerate(jobs):
        plans, takers = stage_plans(t)
        if t == 0:
            plans.append(_all_gather_plan(jnp.pad(vec, ((0, 7), (0, 0)))))
        res = _matmul_tn(a, b, DW_ROW_SPLIT, name_, _merge_plans(plans))
        grad[t], grad_b[t] = (r.reshape(N_CHIPS, -1, D) for r in res[:2])
        landed = list(res[2:])
        if t == 0:
            vec_blocks = landed.pop()
        after_stage(t, landed, takers)

    ws = big
    ms = [tr(m_ffn1_w_gate), tr(m_ffn1_w_up), m_ffn1_w_down[0], tr(m_w_in), m_w_out[0], tr(m_ffn2_w_gate), tr(m_ffn2_w_up), m_ffn2_w_down[0]]
    vs = [tr(v_ffn1_w_gate), tr(v_ffn1_w_up), v_ffn1_w_down[0], tr(v_w_in), v_w_out[0], tr(v_ffn2_w_gate), tr(v_ffn2_w_up), v_ffn2_w_down[0]]
    for t in range(n_jobs, n_jobs + 3):
        plans, takers = stage_plans(t)
        after_stage(t, _run_comm(_merge_plans(plans), f"grads_tail_{t - n_jobs}"), takers)
    upd = {}
    for name_, idx in (("adamw_a", [0, 1, 2, 4]), ("adamw_b", [3, 5, 6, 7])):
        k = len(idx)
        res = _adamw([ws[i] for i in idx], [g_big[i] for i in idx], [ms[i] for i in idx], [vs[i] for i in idx], 8, name_)
        for j, i in enumerate(idx):
            upd[i] = (res[j], res[k + j], res[2 * k + j])
            g_big[i] = res[3 * k + j]

    total = _sum_devices(vec_blocks, "small_sum")[0:1]
    g_n1, g_nm, g_n2, g_nf = (total[:, k * D:(k + 1) * D] for k in range(4))
    cw_full = total[:, 4 * D:4 * D + 3 * CONV_WIDTH].reshape(3, CONV_WIDTH)
    cq = CONV_WIDTH // N_CHIPS
    g_cw = lax.dynamic_slice(cw_full, (0, chip * cq), (3, cq))
    off = 4 * D + 3 * CONV_WIDTH
    g_sk = total[:, off:off + N_Q_HEADS]
    loss = total[0, off + LANES]

    sw = [ffn1_norm, mix_norm, conv_w[0], attn_sinks, ffn2_norm, gf]
    sg = [g_n1, g_nm, g_cw, g_sk, g_n2, g_nf]
    sm = [m_ffn1_norm, m_mix_norm, m_conv_w[0], m_attn_sinks, m_ffn2_norm, m_final_norm.reshape(1, D)]
    sv = [v_ffn1_norm, v_mix_norm, v_conv_w[0], v_attn_sinks, v_ffn2_norm, v_final_norm.reshape(1, D)]
    sres = _adamw(sw, sg, sm, sv, 1, "adamw_small")
    supd = [(sres[j], sres[6 + j], sres[12 + j]) for j in range(6)]

    order = [("s", 0), ("b", 0), ("b", 1), ("b", 2), ("s", 1), ("b", 3), ("s", 2), ("s", 3), ("b", 4),
             ("s", 4), ("b", 5), ("b", 6), ("b", 7), ("s", 5)]

    def leaf(kind, i, which):
        if kind == "b":
            a = g_big[i] if which == 0 else upd[i][which - 1]
            return (jnp.swapaxes(a, 0, 1) if transposed[i] else a)[None]
        a = sg[i] if which == 0 else supd[i][which - 1]
        if i == 2:
            return a[None]
        if i == 5:
            return a.reshape(D)
        return a

    outs = [loss, dx0[None]]
    for which in range(4):
        outs += [leaf(kind, i, which) for kind, i in order]
    return tuple(outs)
```

```python
import functools

import jax
import jax.numpy as jnp
import numpy as np
from jax import lax
from jax.experimental import pallas as pl
from jax.experimental.pallas import tpu as pltpu

F32 = jnp.float32
BF16 = jnp.bfloat16
MESH = pl.DeviceIdType.MESH

CONV_WIDTH = 512
N_Q_HEADS = 8
HEAD_DIM = 64
BLOCK = 128
ROPE_THETA = 500000.0
ROT_DIM = 16
RMS_EPS = 1e-5
MASK_VALUE = -1e30
ATTN_SCALE = HEAD_DIM ** -0.5
FFN_RES_SCALE = 0.5
ADAM_LR = 0.001
ADAM_B1 = 0.9
ADAM_B2 = 0.999
ADAM_EPS = 1e-08
ADAM_WD = 0.01
ADAM_STEP = 10

N_CHIPS = 4
N_DEV = 8
LANES = 128
VMEM_LIMIT = 56 * 1024 * 1024

_pcall = pl.pallas_call
HBM_SPEC = pl.BlockSpec(memory_space=pltpu.HBM)
ANY_SPEC = pl.BlockSpec(memory_space=pl.ANY)


def _params(n_axes, vmem=VMEM_LIMIT):
    return pltpu.CompilerParams(dimension_semantics=("arbitrary",) * n_axes, vmem_limit_bytes=vmem)


def _dot(a, b):
    return jnp.dot(a, b, preferred_element_type=F32)


def _dot_nt(a, b):
    return lax.dot_general(a, b, (((1,), (1,)), ((), ())), preferred_element_type=F32)


def _dot_tn(a, b):
    return lax.dot_general(a, b, (((0,), (0,)), ((), ())), preferred_element_type=F32)


def _rms_inv(x):
    return lax.rsqrt(jnp.mean(x * x, axis=-1, keepdims=True) + RMS_EPS)


def _norm_bwd(dh, x, g):
    inv = _rms_inv(x)
    xhat = x * inv
    dg = jnp.sum(dh * xhat, axis=0, keepdims=True)
    dxhat = dh * g
    dx = inv * (dxhat - xhat * jnp.mean(dxhat * xhat, axis=-1, keepdims=True))
    return dx, dg


def _place():
    x, y, c = lax.axis_index("x"), lax.axis_index("y"), lax.axis_index("c")
    chips = [(1 - x, y), (x, 1 - y), (1 - x, 1 - y)]
    return x, y, c, chips


class _Plan:
    def __init__(self, arrays, out_shapes, n_sems, start, finish, middle=None, aliases=None):
        self.arrays, self.out_shapes, self.n_sems = list(arrays), list(out_shapes), n_sems
        self.start, self.finish, self.middle = start, finish, middle
        self.aliases = dict(aliases or {})

    def specs(self):
        k = len(self.arrays)
        sems = [pltpu.SemaphoreType.DMA((self.n_sems,)), pltpu.SemaphoreType.DMA((self.n_sems,))]
        return [HBM_SPEC] * k, [HBM_SPEC] * len(self.out_shapes), self.out_shapes, sems


class _SemSlice:
    def __init__(self, ref, offset):
        self.ref, self.offset = ref, offset

    @property
    def at(self):
        return self

    def __getitem__(self, k):
        return self.ref.at[k + self.offset]


def _merge_plans(plans):
    plans = [p for p in plans if p is not None]
    if len(plans) <= 1:
        return plans[0] if plans else None
    arrays, shapes, aliases, spans, n_sems = [], [], {}, [], 0
    for p in plans:
        a0, o0 = len(arrays), len(shapes)
        spans.append((a0, a0 + len(p.arrays), o0, o0 + len(p.out_shapes), n_sems))
        aliases.update({a0 + i: o0 + j for i, j in p.aliases.items()})
        arrays += p.arrays
        shapes += p.out_shapes
        n_sems += p.n_sems

    def run(which):
        def fn(ins, outs, send_sems, recv_sems):
            for p, (a0, a1, o0, o1, s0) in zip(plans, spans):
                part = getattr(p, which)
                if part is not None:
                    part(ins[a0:a1], outs[o0:o1], _SemSlice(send_sems, s0), _SemSlice(recv_sems, s0))
        return fn

    middle = run("middle") if any(p.middle is not None for p in plans) else None
    return _Plan(arrays, shapes, n_sems, run("start"), run("finish"), middle, aliases)


def _sibling_plan(grads_b):
    n = len(grads_b)

    def copies(ins, outs, send_sems, recv_sems):
        x, y, c, _ = _place()

        def copy(t):
            half = ins[t].shape[1] // 2
            return pltpu.make_async_remote_copy(
                src_ref=ins[t].at[:, pl.ds(pl.multiple_of((1 - c) * half, 16), half), :], dst_ref=outs[t],
                send_sem=send_sems.at[t], recv_sem=recv_sems.at[t], device_id=(x, y, 1 - c), device_id_type=MESH)

        return [copy(t) for t in range(n)]

    def start(*refs):
        for cp in copies(*refs):
            cp.start()

    def finish(*refs):
        for cp in copies(*refs):
            cp.wait()

    shapes = [jax.ShapeDtypeStruct((g.shape[0], g.shape[1] // 2, g.shape[2]), g.dtype) for g in grads_b]
    return _Plan(grads_b, shapes, n, start, finish)


def _scatter_plan(parts_b):
    n = len(parts_b)

    def copies(ins, outs, send_sems, recv_sems):
        x, y, c, chips = _place()

        def copy(t, j):
            px, py = chips[j]
            return pltpu.make_async_remote_copy(
                src_ref=ins[t].at[2 * px + py], dst_ref=outs[t].at[j], send_sem=send_sems.at[3 * t + j],
                recv_sem=recv_sems.at[3 * t + j], device_id=(px, py, c), device_id_type=MESH)

        return [copy(t, j) for t in range(n) for j in range(3)]

    def start(*refs):
        for cp in copies(*refs):
            cp.start()

    def finish(*refs):
        for cp in copies(*refs):
            cp.wait()

    shapes = [jax.ShapeDtypeStruct((3, *p.shape[1:]), p.dtype) for p in parts_b]
    return _Plan(parts_b, shapes, 3 * n, start, finish)


def _gather_plan(shards, small=()):
    n, ns = len(shards), len(small)
    per = 8

    def parts(ins, outs, send_sems, recv_sems):
        x, y, c, chips = _place()
        me = 2 * x + y
        blocks = [2 * px + py for px, py in chips]

        def rows(t, core, piece=None):
            half = ins[t].shape[0] // 2
            if piece is None:
                return pl.ds(pl.multiple_of(core * half, 16), half)
            return pl.ds(pl.multiple_of(core * half + piece * (half // 2), 16), half // 2)

        def remote(src, dst, k, device):
            return pltpu.make_async_remote_copy(src_ref=src, dst_ref=dst, send_sem=send_sems.at[k],
                                                recv_sem=recv_sems.at[k], device_id=device, device_id_type=MESH)

        def first(t, j, block, core):
            return remote(ins[t].at[rows(t, core), :], outs[t].at[block, rows(t, core), :], per * t + j, (*chips[j], c))

        def relay(t, j, block, core):
            ref = outs[t].at[block, rows(t, core, j), :]
            return remote(ref, ref, per * t + 2 + j, (*chips[j], c))

        def passed(t, k, block, core, piece=None):
            ref = outs[t].at[block, rows(t, core, piece), :]
            return remote(ref, ref, per * t + 4 + k, (x, y, 1 - c))

        def whole(s, j, block):
            return remote(ins[n + s], outs[n + s].at[block], per * n + 3 * s + j, (*chips[j], c))

        return c, me, blocks, first, relay, passed, whole

    def start(*refs):
        c, me, _, first, _, _, whole = parts(*refs)
        for t in range(n):
            for j in range(2):
                first(t, j, me, c).start()
        for s in range(ns):
            for j in range(3):
                whole(s, j, me).start()

    def middle(*refs):
        c, _, blocks, first, relay, passed, _ = parts(*refs)
        for t in range(n):
            for j in range(2):
                first(t, j, blocks[j], c).wait_recv()
                passed(t, j, blocks[j], c).start()
                relay(t, 1 - j, blocks[j], c).start()

    def finish(*refs):
        c, me, blocks, first, relay, passed, whole = parts(*refs)
        for t in range(n):
            for j in range(2):
                relay(t, j, blocks[2], c).wait_recv()
                passed(t, 2 + j, blocks[2], c, j).start()
        for t in range(n):
            for j in range(2):
                passed(t, j, blocks[j], 1 - c).wait_recv()
                passed(t, 2 + j, blocks[2], 1 - c, j).wait_recv()
        for s in range(ns):
            for j in range(3):
                whole(s, j, blocks[j]).wait_recv()
        for t in range(n):
            for j in range(2):
                first(t, j, me, c).wait_send()
                relay(t, 1 - j, blocks[j], c).wait_send()
                passed(t, j, blocks[j], c).wait_send()
                passed(t, 2 + j, blocks[2], c, j).wait_send()
        for s in range(ns):
            for j in range(3):
                whole(s, j, me).wait_send()

    arrays = [*shards, *small]
    shapes = [jax.ShapeDtypeStruct((N_CHIPS, *a.shape), a.dtype) for a in arrays]
    return _Plan(arrays, shapes, per * n + 3 * ns, start, finish, middle)


def _run_comm(plan, name):
    k = len(plan.arrays)
    in_specs, out_specs, out_shape, sems = plan.specs()

    def body(*refs):
        cr = (refs[:k], refs[k:k + len(out_shape)], refs[-2], refs[-1])
        plan.start(*cr)
        if plan.middle is not None:
            plan.middle(*cr)
        plan.finish(*cr)

    return _pcall(body, name=name, in_specs=in_specs, out_specs=out_specs, out_shape=out_shape,
                  input_output_aliases=plan.aliases, scratch_shapes=sems)(*plan.arrays)


def _carried(plan, in_specs, out_specs, out_shape, scratch):
    aliases = {}
    if plan is not None:
        p_in, p_out, p_shape, p_sems = plan.specs()
        aliases = {len(in_specs) + i: len(out_specs) + j for i, j in plan.aliases.items()}
        in_specs, out_specs = in_specs + p_in, out_specs + p_out
        out_shape, scratch = out_shape + p_shape, scratch + p_sems
    return dict(in_specs=in_specs, out_specs=out_specs, out_shape=out_shape, scratch_shapes=scratch,
                input_output_aliases=aliases)


def _unpack(refs, n_in, n_out, plan):
    k_in = len(plan.arrays) if plan else 0
    k_out = len(plan.out_shapes) if plan else 0
    ins = refs[:n_in]
    outs = refs[n_in + k_in:n_in + k_in + n_out]
    rest = refs[n_in + k_in + n_out + k_out:]
    if plan is None:
        return ins, outs, rest, None
    cr = (refs[n_in:n_in + k_in], refs[n_in + k_in + n_out:n_in + k_in + n_out + k_out], rest[-2], rest[-1])
    return ins, outs, rest[:-2], cr


def _hook(plan, cr, which, cond):
    fn = getattr(plan, which) if plan is not None else None
    if fn is not None:
        pl.when(cond)(lambda: fn(*cr))


def _join_plan(shards):
    n = len(shards)

    def copy(ins, outs, send_sems, recv_sems, t, core):
        x, y, c, _ = _place()
        half = ins[t].shape[0] // 2
        rows = pl.ds(pl.multiple_of(core * half, 8), half)
        return pltpu.make_async_remote_copy(
            src_ref=ins[t].at[rows, :], dst_ref=outs[t].at[rows, :], send_sem=send_sems.at[t],
            recv_sem=recv_sems.at[t], device_id=(x, y, 1 - c), device_id_type=MESH)

    def start(*refs):
        c = lax.axis_index("c")
        for t in range(n):
            copy(*refs, t, c).start()

    def finish(*refs):
        c = lax.axis_index("c")
        for t in range(n):
            copy(*refs, t, 1 - c).wait_recv()
        for t in range(n):
            copy(*refs, t, c).wait_send()

    shapes = [jax.ShapeDtypeStruct(s.shape, s.dtype) for s in shards]
    return _Plan(shards, shapes, n, start, finish, aliases={t: t for t in range(n)})


def _all_gather_plan(vec):
    def parts(ins, outs, send_sems, recv_sems):
        x, y, c, _ = _place()
        me = 4 * x + 2 * y + c
        rel = [((k >> 2) & 1, (k >> 1) & 1, k & 1) for k in range(1, N_DEV)]

        def peer(k):
            fx, fy, fc = rel[k]
            return (x ^ fx, y ^ fy, c ^ fc)

        def copy(k, dev):
            return pltpu.make_async_remote_copy(
                src_ref=ins[0], dst_ref=outs[0].at[dev], send_sem=send_sems.at[k], recv_sem=recv_sems.at[k],
                device_id=peer(k), device_id_type=MESH)

        mine = pltpu.make_async_copy(ins[0], outs[0].at[me], send_sems.at[N_DEV - 1])
        return me, peer, copy, mine

    def start(*refs):
        me, _, copy, mine = parts(*refs)
        mine.start()
        for k in range(N_DEV - 1):
            copy(k, me).start()

    def finish(*refs):
        me, peer, copy, mine = parts(*refs)
        for k in range(N_DEV - 1):
            px, py, pc = peer(k)
            copy(k, 4 * px + 2 * py + pc).wait_recv()
        for k in range(N_DEV - 1):
            copy(k, me).wait_send()
        mine.wait()

    return _Plan([vec], [jax.ShapeDtypeStruct((N_DEV, *vec.shape), vec.dtype)], N_DEV, start, finish)


def _sum_devices(blocks, name):
    def body(b_ref, o_ref):
        total = b_ref[0]
        for dev in range(1, N_DEV):
            total = total + b_ref[dev]
        o_ref[...] = total

    return _pcall(body, name=name, in_specs=[pl.BlockSpec(memory_space=pltpu.VMEM)],
                  out_specs=pl.BlockSpec(memory_space=pltpu.VMEM),
                  out_shape=jax.ShapeDtypeStruct(blocks.shape[1:], F32))(blocks)


TOKEN_TILE = 512
PROJ_TOKEN_TILE = 1024
BWD_VMEM_LIMIT = 62 * 1024 * 1024
DW_TOKEN_TILE = 2048
DW_ROW_SPLIT = 2
MXU_COLS = 256


def _chunks(n):
    out, c0 = [], 0
    while c0 < n:
        size = min(MXU_COLS, n - c0)
        out.append((c0, size))
        c0 += size
    return out


def _load_weights(hbm_refs, vmem_refs, sems):
    copies = [pltpu.make_async_copy(h, v, sems.at[k]) for k, (h, v) in enumerate(zip(hbm_refs, vmem_refs))]
    for cp in copies:
        cp.start()
    for cp in copies:
        cp.wait()


def _ffn_fwd(x, g, wgt, wut, wd, name, plan=None, head=None):
    T, D = x.shape
    F = wgt.shape[0]
    tm = min(T, TOKEN_TILE)
    ni = T // tm
    n_head = 2 if head is not None else 0

    def body(*refs):
        ins, outs, scratch, cr = _unpack(refs, 5 + n_head, 5 + n_head, plan)
        x_ref, g_ref, wg_hbm, wu_hbm, wd_hbm = ins[:5]
        xo_ref, h_ref, gate_ref, up_ref, act_t_ref = outs[:5]
        wg_ref, wu_ref, wd_ref, act_ref, sems = scratch
        i = pl.program_id(0)
        _hook(plan, cr, "start", i == 0)

        @pl.when(i == 0)
        def _():
            _load_weights((wg_hbm, wu_hbm, wd_hbm), (wg_ref, wu_ref, wd_ref), sems)

        xv = x_ref[...]
        h = ((xv * _rms_inv(xv)) * g_ref[...]).astype(BF16)
        h_ref[...] = h
        for c0, size in _chunks(F):
            gate = _dot_nt(h, wg_ref[c0:c0 + size, :])
            up = _dot_nt(h, wu_ref[c0:c0 + size, :])
            gate_ref[:, c0:c0 + size] = gate.astype(BF16)
            up_ref[:, c0:c0 + size] = up.astype(BF16)
            act = (gate * jax.nn.sigmoid(gate) * up).astype(BF16)
            act_ref[:, c0:c0 + size] = act
            act_t_ref[c0:c0 + size, :] = act.T
        y = x_ref[...] + FFN_RES_SCALE * _dot(act_ref[...], wd_ref[...])
        if head is None:
            xo_ref[...] = y
        else:
            gf_ref, t_ref = ins[5:]
            dgf_ref, loss_ref = outs[5:]

            @pl.when(i == 0)
            def _():
                dgf_ref[...] = jnp.zeros_like(dgf_ref)
                loss_ref[...] = jnp.zeros_like(loss_ref)

            gf = gf_ref[...]
            diff = (y * _rms_inv(y)) * gf - t_ref[...]
            loss_ref[...] += 0.5 * jnp.sum(jnp.mean(diff * diff, axis=-1, keepdims=True))
            dy, dgf = _norm_bwd(diff * (1.0 / D), y, gf)
            xo_ref[...] = dy
            dgf_ref[...] += dgf
        _hook(plan, cr, "middle", i == ni // 2)
        _hook(plan, cr, "finish", i == ni - 1)

    const = lambda shape: pl.BlockSpec(shape, lambda i: (0, 0))
    rows = lambda width: pl.BlockSpec((tm, width), lambda i: (i, 0))
    in_specs = [rows(D), const((1, D)), ANY_SPEC, ANY_SPEC, ANY_SPEC]
    out_specs = [rows(D), rows(D), rows(F), rows(F), pl.BlockSpec((F, tm), lambda i: (0, i))]
    out_shape = [jax.ShapeDtypeStruct((T, D), F32), jax.ShapeDtypeStruct((T, D), BF16),
                 jax.ShapeDtypeStruct((T, F), BF16), jax.ShapeDtypeStruct((T, F), BF16), jax.ShapeDtypeStruct((F, T), BF16)]
    if head is not None:
        in_specs += [const((1, D)), rows(D)]
        out_specs += [const((1, D)), const((1, LANES))]
        out_shape += [jax.ShapeDtypeStruct((1, D), F32), jax.ShapeDtypeStruct((1, LANES), F32)]
    io = _carried(plan, in_specs, out_specs, out_shape,
                  [pltpu.VMEM((F, D), BF16), pltpu.VMEM((F, D), BF16), pltpu.VMEM((F, D), BF16),
                   pltpu.VMEM((tm, F), BF16), pltpu.SemaphoreType.DMA((3,))])
    return _pcall(
        body, name=name, grid=(ni,), compiler_params=_params(1), **io,
    )(x, g, wgt, wut, wd, *(head or ()), *(plan.arrays if plan else ()))


def _ffn_bwd(dy, x, g, gate, up, wgt, wut, wd, name, plan=None):
    T, D = x.shape
    F = wgt.shape[0]
    tm = min(T, TOKEN_TILE)
    ni = T // tm

    def body(*refs):
        ins, outs, scratch, cr = _unpack(refs, 8, 5, plan)
        dy_ref, x_ref, g_ref, gate_ref, up_ref, wg_hbm, wu_hbm, wd_hbm = ins
        dx_ref, dyb_ref, dgate_ref, dup_ref, dg_ref = outs
        wg_ref, wu_ref, wd_ref, sems = scratch
        i = pl.program_id(0)
        _hook(plan, cr, "start", i == 0)

        @pl.when(i == 0)
        def _():
            _load_weights((wg_hbm, wu_hbm, wd_hbm), (wg_ref, wu_ref, wd_ref), sems)
            dg_ref[...] = jnp.zeros_like(dg_ref)

        dyb = (FFN_RES_SCALE * dy_ref[...]).astype(BF16)
        dyb_ref[...] = dyb
        for c0, size in _chunks(F):
            dact = _dot_nt(dyb, wd_ref[c0:c0 + size, :])
            gt = gate_ref[:, c0:c0 + size].astype(F32)
            u = up_ref[:, c0:c0 + size].astype(F32)
            sig = jax.nn.sigmoid(gt)
            dup_ref[:, c0:c0 + size] = (dact * (gt * sig)).astype(BF16)
            dgate_ref[:, c0:c0 + size] = (dact * u * (sig * (1.0 + gt * (1.0 - sig)))).astype(BF16)
        dh = _dot(dgate_ref[...], wg_ref[...]) + _dot(dup_ref[...], wu_ref[...])
        dxn, dg = _norm_bwd(dh, x_ref[...], g_ref[...])
        dx_ref[...] = dy_ref[...] + dxn
        dg_ref[...] += dg
        _hook(plan, cr, "finish", i == ni - 1)

    io = _carried(
        plan,
        [pl.BlockSpec((tm, D), lambda i: (i, 0)), pl.BlockSpec((tm, D), lambda i: (i, 0)),
         pl.BlockSpec((1, D), lambda i: (0, 0)),
         pl.BlockSpec((tm, F), lambda i: (i, 0)), pl.BlockSpec((tm, F), lambda i: (i, 0)),
         ANY_SPEC, ANY_SPEC, ANY_SPEC],
        [pl.BlockSpec((tm, D), lambda i: (i, 0)), pl.BlockSpec((tm, D), lambda i: (i, 0)),
         pl.BlockSpec((tm, F), lambda i: (i, 0)), pl.BlockSpec((tm, F), lambda i: (i, 0)),
         pl.BlockSpec((1, D), lambda i: (0, 0))],
        [jax.ShapeDtypeStruct((T, D), F32), jax.ShapeDtypeStruct((T, D), BF16),
         jax.ShapeDtypeStruct((T, F), BF16), jax.ShapeDtypeStruct((T, F), BF16), jax.ShapeDtypeStruct((1, D), F32)],
        [pltpu.VMEM((F, D), BF16), pltpu.VMEM((F, D), BF16), pltpu.VMEM((F, D), BF16),
         pltpu.SemaphoreType.DMA((3,))])
    return _pcall(
        body, name=name, grid=(ni,), compiler_params=_params(1, BWD_VMEM_LIMIT), **io,
    )(dy, x, g, gate, up, wgt, wut, wd, *(plan.arrays if plan else ()))


def _matmul_tn(a, b, row_split, name, plan=None, a_transposed=False):
    T, n2 = b.shape
    n1 = a.shape[0] if a_transposed else a.shape[1]
    tn = n1 // row_split
    tk = min(T, DW_TOKEN_TILE)
    nk = T // tk

    def body(*refs):
        (a_ref, b_ref), (o_ref, ob_ref), _, cr = _unpack(refs, 2, 2, plan)
        j = pl.program_id(0)
        k = pl.program_id(1)
        _hook(plan, cr, "start", jnp.logical_and(j == 0, k == 0))

        @pl.when(k == 0)
        def _():
            o_ref[...] = jnp.zeros_like(o_ref)

        o_ref[...] += _dot(a_ref[...], b_ref[...]) if a_transposed else _dot_tn(a_ref[...], b_ref[...])

        @pl.when(k == nk - 1)
        def _():
            ob_ref[...] = o_ref[...].astype(BF16)

        _hook(plan, cr, "finish", jnp.logical_and(j == row_split - 1, k == nk - 1))

    a_spec = pl.BlockSpec((tn, tk), lambda j, k: (j, k)) if a_transposed else pl.BlockSpec((tk, tn), lambda j, k: (k, j))
    io = _carried(
        plan,
        [a_spec, pl.BlockSpec((tk, n2), lambda j, k: (k, 0))],
        [pl.BlockSpec((tn, n2), lambda j, k: (j, 0)), pl.BlockSpec((tn, n2), lambda j, k: (j, 0))],
        [jax.ShapeDtypeStruct((n1, n2), F32), jax.ShapeDtypeStruct((n1, n2), BF16)], [])
    return _pcall(
        body, name=name, grid=(row_split, nk), compiler_params=_params(2), **io,
    )(a, b, *(plan.arrays if plan else ()))


def _norm_matmul(x, g, wt, tab, name):
    T, D = x.shape
    n = wt.shape[0]
    tm = min(T, PROJ_TOKEN_TILE)

    def body(x_ref, g_ref, w_ref, tab_ref, z_ref, h_ref):
        xv = x_ref[...]
        h = ((xv * _rms_inv(xv)) * g_ref[...]).astype(BF16)
        h_ref[...] = h
        z = _dot_nt(h, w_ref[...])
        z_ref[:, 0:Z_Q] = z[:, 0:Z_Q]
        tab_v = tab_ref[...]
        for c0 in range(Z_Q, Z_V, LANES):
            z_ref[:, c0:c0 + LANES] = _rot(z[:, c0:c0 + LANES], tab_v)
        z_ref[:, Z_V:Z_END] = z[:, Z_V:Z_END]

    return _pcall(
        body, name=name, grid=(T // tm,),
        in_specs=[pl.BlockSpec((tm, D), lambda i: (i, 0)), pl.BlockSpec((1, D), lambda i: (0, 0)),
                  pl.BlockSpec((n, D), lambda i: (0, 0)), pl.BlockSpec((tm, 3 * LANES), lambda i: (i, 0))],
        out_specs=[pl.BlockSpec((tm, n), lambda i: (i, 0)), pl.BlockSpec((tm, D), lambda i: (i, 0))],
        out_shape=[jax.ShapeDtypeStruct((T, n), F32), jax.ShapeDtypeStruct((T, D), BF16)],
        compiler_params=_params(1),
    )(x, g, wt, tab)


def _matmul_residual(y, w, x, name):
    T, D = x.shape
    kdim = y.shape[1]
    tm = min(T, PROJ_TOKEN_TILE)

    def body(y_ref, w_ref, x_ref, o_ref):
        o_ref[...] = x_ref[...] + _dot(y_ref[...], w_ref[...])

    return _pcall(
        body, name=name, grid=(T // tm,),
        in_specs=[pl.BlockSpec((tm, kdim), lambda i: (i, 0)), pl.BlockSpec((kdim, D), lambda i: (0, 0)),
                  pl.BlockSpec((tm, D), lambda i: (i, 0))],
        out_specs=pl.BlockSpec((tm, D), lambda i: (i, 0)),
        out_shape=jax.ShapeDtypeStruct((T, D), F32),
        compiler_params=_params(1),
    )(y, w, x)


def _matmul_nt(dx, w, name, plan=None):
    T, D = dx.shape
    kdim = w.shape[0]
    tm = min(T, PROJ_TOKEN_TILE)
    ni = T // tm

    def body(*refs):
        (dx_ref, w_ref), (dy_ref, dxb_ref), _, cr = _unpack(refs, 2, 2, plan)
        i = pl.program_id(0)
        _hook(plan, cr, "start", i == 0)
        dxb = dx_ref[...].astype(BF16)
        dxb_ref[...] = dxb
        dy_ref[...] = _dot_nt(dxb, w_ref[...])
        _hook(plan, cr, "finish", i == ni - 1)

    io = _carried(
        plan,
        [pl.BlockSpec((tm, D), lambda i: (i, 0)), pl.BlockSpec((kdim, D), lambda i: (0, 0))],
        [pl.BlockSpec((tm, kdim), lambda i: (i, 0)), pl.BlockSpec((tm, D), lambda i: (i, 0))],
        [jax.ShapeDtypeStruct((T, kdim), F32), jax.ShapeDtypeStruct((T, D), BF16)], [])
    return _pcall(
        body, name=name, grid=(ni,), compiler_params=_params(1), **io,
    )(dx, w, *(plan.arrays if plan else ()))


def _matmul_norm_bwd(dz, wt, x, g, dres, name, plan=None):
    T, D = x.shape
    n = dz.shape[1]
    tm = min(T, PROJ_TOKEN_TILE)
    ni = T // tm

    def body(*refs):
        (dz_ref, w_ref, x_ref, g_ref, dres_ref), (dx_ref, dg_ref), _, cr = _unpack(refs, 5, 2, plan)
        i = pl.program_id(0)
        _hook(plan, cr, "start", i == 0)

        @pl.when(i == 0)
        def _():
            dg_ref[...] = jnp.zeros_like(dg_ref)

        dh = _dot(dz_ref[...], w_ref[...])
        dxn, dg = _norm_bwd(dh, x_ref[...], g_ref[...])
        dx_ref[...] = dres_ref[...] + dxn
        dg_ref[...] += dg
        _hook(plan, cr, "finish", i == ni - 1)

    io = _carried(
        plan,
        [pl.BlockSpec((tm, n), lambda i: (i, 0)), pl.BlockSpec((n, D), lambda i: (0, 0)),
         pl.BlockSpec((tm, D), lambda i: (i, 0)), pl.BlockSpec((1, D), lambda i: (0, 0)),
         pl.BlockSpec((tm, D), lambda i: (i, 0))],
        [pl.BlockSpec((tm, D), lambda i: (i, 0)), pl.BlockSpec((1, D), lambda i: (0, 0))],
        [jax.ShapeDtypeStruct((T, D), F32), jax.ShapeDtypeStruct((1, D), F32)], [])
    return _pcall(
        body, name=name, grid=(ni,), compiler_params=_params(1), **io,
    )(dz, wt, x, g, dres, *(plan.arrays if plan else ()))


Z_Q = 3 * CONV_WIDTH
Z_K = Z_Q + N_Q_HEADS * HEAD_DIM
Z_V = Z_K + LANES
Z_END = Z_V + LANES


def _rope_tables(T):
    half = ROT_DIM // 2
    inv_freq = ROPE_THETA ** (-jnp.arange(0, ROT_DIM, 2, dtype=F32) / ROT_DIM)
    ang = inv_freq[:, None] * jnp.arange(T, dtype=F32)[None, :]
    cos_sin = jnp.concatenate([jnp.cos(ang), jnp.sin(ang)], axis=0)
    select = np.zeros((2 * half, 3 * LANES), np.float32)
    const = np.zeros((1, 3 * LANES), np.float32)
    for lane in range(LANES):
        d = lane % HEAD_DIM
        if d < half:
            select[d, lane] = 1.0
            select[half + d, LANES + lane] = -1.0
        elif d < ROT_DIM:
            select[d - half, lane] = 1.0
            select[d, 2 * LANES + lane] = 1.0
        else:
            const[0, lane] = 1.0
    tab = lax.dot_general(cos_sin, jnp.asarray(select), (((0,), (0,)), ((), ())),
                          precision=lax.Precision.HIGHEST, preferred_element_type=F32)
    return tab + jnp.asarray(const)


def _tab3(tab):
    return tab[:, 0:LANES], tab[:, LANES:2 * LANES], tab[:, 2 * LANES:3 * LANES]


def _rot(x, tab):
    c, s1, s2 = _tab3(tab)
    return x * c + pltpu.roll(x, LANES - ROT_DIM // 2, 1) * s1 + pltpu.roll(x, ROT_DIM // 2, 1) * s2


def _rot_t(d, tab):
    c, s1, s2 = _tab3(tab)
    return d * c + pltpu.roll(d * s1, ROT_DIM // 2, 1) + pltpu.roll(d * s2, LANES - ROT_DIM // 2, 1)


def _head_pads(a):
    lo = lax.broadcasted_iota(jnp.int32, a.shape, 1) < HEAD_DIM
    nat0 = jnp.where(lo, a, 0.0)
    nat1 = jnp.where(lo, 0.0, a)
    return {
        (0, 0): nat0.astype(BF16), (0, 1): pltpu.roll(nat0, HEAD_DIM, 1).astype(BF16),
        (1, 0): pltpu.roll(nat1, HEAD_DIM, 1).astype(BF16), (1, 1): nat1.astype(BF16),
    }


def _from_pads(even, odd, kv):
    lo = lax.broadcasted_iota(jnp.int32, even.shape, 1) < HEAD_DIM
    if kv == 0:
        return jnp.where(lo, even + pltpu.roll(odd, HEAD_DIM, 1), 0.0)
    return jnp.where(lo, 0.0, pltpu.roll(even, HEAD_DIM, 1) + odd)


N_GROUPS = 4


def _group_head(g, r):
    kv, par = divmod(g, 2)
    return 2 * (2 * kv + r) + par


def _window_mask_t(has_prev):
    jj = lax.broadcasted_iota(jnp.int32, (2 * BLOCK, 2 * BLOCK), 0)
    ii = lax.broadcasted_iota(jnp.int32, (2 * BLOCK, 2 * BLOCK), 1) & (BLOCK - 1)
    rel = jj - BLOCK - ii
    return (rel <= 0) & (rel > -BLOCK) & ((jj >= BLOCK) | has_prev)


def _sink_row(sink_ref, g):
    lane = lax.broadcasted_iota(jnp.int32, (1, 2 * BLOCK), 1)
    return jnp.where(lane < BLOCK, sink_ref[0, _group_head(g, 0)], sink_ref[0, _group_head(g, 1)])


def _attn_probs_t(q2, kp, mask, sink_ref):
    out = []
    for kv in range(2):
        q_st = jnp.concatenate([q2[2 * kv], q2[2 * kv + 1]], axis=0)
        for par in range(2):
            s = jnp.where(mask, _dot_nt(kp[(kv, par)], q_st), MASK_VALUE)
            sink = _sink_row(sink_ref, 2 * kv + par)
            m = jnp.maximum(jnp.max(s, axis=0, keepdims=True), sink)
            p = jnp.exp(s - m)
            esink = jnp.exp(sink - m)
            rden = 1.0 / (jnp.sum(p, axis=0, keepdims=True) + esink)
            out.append((p * rden, esink * rden))
    return out


def _conv_taps(cg, u, cg_prev, u_prev, has_prev):
    vv = cg * u
    halo = jnp.where(has_prev, cg_prev * u_prev, 0.0)
    ext = jnp.concatenate([halo, vv], axis=0)
    rows = ext.shape[0]
    vv1 = pltpu.roll(ext, 1, 0)[8:rows]
    vv2 = pltpu.roll(ext, 2, 0)[8:rows]
    return vv, vv1, vv2


def _mix_specs(nb):
    cur = lambda n: jnp.minimum(n, nb - 1)
    prev = lambda n: jnp.maximum(jnp.minimum(n, nb - 1) - 1, 0)
    rows8_prev = lambda n: jnp.maximum(16 * jnp.minimum(n, nb - 1) - 1, 0)
    return cur, prev, [
        pl.BlockSpec((BLOCK, Z_END), lambda n: (cur(n), 0)),
        pl.BlockSpec((BLOCK, 2 * LANES), lambda n: (prev(n), Z_K // (2 * LANES))),
        pl.BlockSpec((8, CONV_WIDTH), lambda n: (rows8_prev(n), 1)),
        pl.BlockSpec((8, CONV_WIDTH), lambda n: (rows8_prev(n), 2)),
        pl.BlockSpec((BLOCK, 3 * LANES), lambda n: (cur(n), 0)),
        pl.BlockSpec((BLOCK, 3 * LANES), lambda n: (prev(n), 0)),
        pl.BlockSpec((3, CONV_WIDTH), lambda n: (0, 0)),
        pl.BlockSpec(memory_space=pltpu.SMEM),
    ]


def _mix_core_fwd(z, conv_w, sinks, name):
    T = z.shape[0]
    nb = T // BLOCK
    _, _, specs = _mix_specs(nb)
    specs = specs[:4] + specs[6:]

    def body(z_ref, zkvp_ref, cgp_ref, up_ref, cw_ref, sink_ref, y_ref):
        has_prev = pl.program_id(0) > 0
        bg = z_ref[:, 0:CONV_WIDTH]
        vv, vv1, vv2 = _conv_taps(z_ref[:, CONV_WIDTH:2 * CONV_WIDTH], z_ref[:, 2 * CONV_WIDTH:Z_Q],
                                  cgp_ref[...], up_ref[...], has_prev)
        conv = cw_ref[0:1, :] * vv2 + cw_ref[1:2, :] * vv1 + cw_ref[2:3, :] * vv
        y_ref[:, 0:CONV_WIDTH] = (bg * conv).astype(BF16)

        k_all = jnp.concatenate([zkvp_ref[:, 0:LANES], z_ref[:, Z_K:Z_V]], axis=0)
        v_all = jnp.concatenate([zkvp_ref[:, LANES:2 * LANES], z_ref[:, Z_V:Z_END]], axis=0)
        kp = _head_pads(k_all)
        vp = _head_pads(v_all)
        q2 = [(z_ref[:, Z_Q + LANES * c:Z_Q + LANES * (c + 1)] * ATTN_SCALE).astype(BF16) for c in range(N_Q_HEADS // 2)]
        probs = _attn_probs_t(q2, kp, _window_mask_t(has_prev), sink_ref)
        for kv in range(2):
            o_t = (_dot_tn(vp[(kv, 0)], probs[2 * kv][0].astype(BF16))
                   + _dot_tn(vp[(kv, 1)], probs[2 * kv + 1][0].astype(BF16)))
            for r in range(2):
                c = 2 * kv + r
                y_ref[:, CONV_WIDTH + LANES * c:CONV_WIDTH + LANES * (c + 1)] = o_t[:, BLOCK * r:BLOCK * (r + 1)].T.astype(BF16)

    return _pcall(
        body, name=name, grid=(nb,), in_specs=specs,
        out_specs=pl.BlockSpec((BLOCK, 2 * CONV_WIDTH), lambda n: (n, 0)),
        out_shape=jax.ShapeDtypeStruct((T, 2 * CONV_WIDTH), BF16),
        compiler_params=_params(1),
    )(z, z, z, z, conv_w, sinks)


def _mix_core_bwd(z, dy, tab, conv_w, sinks, name, plan=None):
    T = z.shape[0]
    nb = T // BLOCK
    cur, _, specs = _mix_specs(nb)
    rows8_next = lambda n: jnp.minimum(16 * (cur(n) + 1), 16 * nb - 1)
    specs = specs[:4] + [
        pl.BlockSpec((8, CONV_WIDTH), lambda n: (rows8_next(n), 0)),
        pl.BlockSpec((BLOCK, 2 * CONV_WIDTH), lambda n: (cur(n), 0)),
        pl.BlockSpec((8, CONV_WIDTH), lambda n: (rows8_next(n), 0)),
    ] + specs[4:]

    def body(*refs):
        ins, outs, scratch, cr = _unpack(refs, 11, 3, plan)
        z_ref, zkvp_ref, cgp_ref, up_ref, bgn_ref, dy_ref, dyn_ref, tab_ref, tabp_ref, cw_ref, sink_ref = ins
        dz_ref, dcw_ref, dsk_ref = outs
        main_ref, kv_ref = scratch
        n = pl.program_id(0)
        _hook(plan, cr, "start", n == 0)

        @pl.when(n == 0)
        def _():
            main_ref[...] = jnp.zeros_like(main_ref)
            kv_ref[...] = jnp.zeros_like(kv_ref)
            dcw_ref[...] = jnp.zeros_like(dcw_ref)
            dsk_ref[...] = jnp.zeros_like(dsk_ref)

        @pl.when(n < nb)
        def _():
            has_prev = n > 0
            has_next = n < nb - 1
            bg = z_ref[:, 0:CONV_WIDTH]
            cg = z_ref[:, CONV_WIDTH:2 * CONV_WIDTH]
            u = z_ref[:, 2 * CONV_WIDTH:Z_Q]
            vv, vv1, vv2 = _conv_taps(cg, u, cgp_ref[...], up_ref[...], has_prev)
            w0, w1, w2 = cw_ref[0:1, :], cw_ref[1:2, :], cw_ref[2:3, :]
            dyc = dy_ref[:, 0:CONV_WIDTH]
            dbg = dyc * (w0 * vv2 + w1 * vv1 + w2 * vv)
            dconv = dyc * bg
            dconv_next = jnp.where(has_next, dyn_ref[...] * bgn_ref[...], 0.0)
            ext = jnp.concatenate([dconv, dconv_next], axis=0)
            rows = ext.shape[0]
            dvv = w2 * dconv + w1 * pltpu.roll(ext, rows - 1, 0)[0:BLOCK] + w0 * pltpu.roll(ext, rows - 2, 0)[0:BLOCK]
            dcw_ref[0:1, :] += jnp.sum(dconv * vv2, axis=0, keepdims=True)
            dcw_ref[1:2, :] += jnp.sum(dconv * vv1, axis=0, keepdims=True)
            dcw_ref[2:3, :] += jnp.sum(dconv * vv, axis=0, keepdims=True)

            tab_c = tab_ref[...]
            tab_p = tabp_ref[...]
            k_all = jnp.concatenate([zkvp_ref[:, 0:LANES], z_ref[:, Z_K:Z_V]], axis=0)
            v_all = jnp.concatenate([zkvp_ref[:, LANES:2 * LANES], z_ref[:, Z_V:Z_END]], axis=0)
            kp = _head_pads(k_all)
            vp = _head_pads(v_all)
            chunks = range(N_Q_HEADS // 2)
            q2 = [(z_ref[:, Z_Q + LANES * c:Z_Q + LANES * (c + 1)] * ATTN_SCALE).astype(BF16) for c in chunks]
            do2 = [dy_ref[:, CONV_WIDTH + LANES * c:CONV_WIDTH + LANES * (c + 1)].astype(BF16) for c in chunks]
            probs = _attn_probs_t(q2, kp, _window_mask_t(has_prev), sink_ref)
            dq_chunks = []
            dk_nat = jnp.zeros((2 * BLOCK, LANES), F32)
            dv_nat = jnp.zeros((2 * BLOCK, LANES), F32)
            for kv in range(2):
                q_st = jnp.concatenate([q2[2 * kv], q2[2 * kv + 1]], axis=0)
                do_st = jnp.concatenate([do2[2 * kv], do2[2 * kv + 1]], axis=0)
                dq_t = jnp.zeros((LANES, 2 * BLOCK), F32)
                dk_par, dv_par = [], []
                for par in range(2):
                    g = 2 * kv + par
                    pr, psink = probs[g]
                    dp = _dot_nt(vp[(kv, par)], do_st)
                    delta = jnp.sum(dp * pr, axis=0, keepdims=True)
                    ds = (pr * (dp - delta)).astype(BF16)
                    dsink = -psink * delta
                    for r in range(2):
                        h = _group_head(g, r)
                        dsk_ref[h:h + 1, :] += jnp.sum(dsink[:, BLOCK * r:BLOCK * (r + 1)])
                    dq_t = dq_t + _dot_tn(kp[(kv, par)], ds)
                    dk_par.append(_dot(ds, q_st))
                    dv_par.append(_dot(pr.astype(BF16), do_st))
                for r in range(2):
                    dq_chunks.append(_rot_t(dq_t[:, BLOCK * r:BLOCK * (r + 1)].T * ATTN_SCALE, tab_c))
                dk_nat = dk_nat + _from_pads(dk_par[0], dk_par[1], kv)
                dv_nat = dv_nat + _from_pads(dv_par[0], dv_par[1], kv)

            dk_prev = _rot_t(kv_ref[:, 0:LANES] + dk_nat[0:BLOCK], tab_p)
            dv_prev = kv_ref[:, LANES:2 * LANES] + dv_nat[0:BLOCK]
            dz_ref[:, 0:Z_K] = main_ref[...]
            dz_ref[:, Z_K:Z_V] = dk_prev.astype(BF16)
            dz_ref[:, Z_V:Z_END] = dv_prev.astype(BF16)
            main_ref[:, 0:CONV_WIDTH] = dbg.astype(BF16)
            main_ref[:, CONV_WIDTH:2 * CONV_WIDTH] = (dvv * u).astype(BF16)
            main_ref[:, 2 * CONV_WIDTH:Z_Q] = (dvv * cg).astype(BF16)
            for c in range(N_Q_HEADS // 2):
                main_ref[:, Z_Q + LANES * c:Z_Q + LANES * (c + 1)] = dq_chunks[c].astype(BF16)
            kv_ref[:, 0:LANES] = dk_nat[BLOCK:2 * BLOCK]
            kv_ref[:, LANES:2 * LANES] = dv_nat[BLOCK:2 * BLOCK]

        @pl.when(n == nb)
        def _():
            dz_ref[:, 0:Z_K] = main_ref[...]
            dz_ref[:, Z_K:Z_V] = _rot_t(kv_ref[:, 0:LANES], tab_ref[...]).astype(BF16)
            dz_ref[:, Z_V:Z_END] = kv_ref[:, LANES:2 * LANES].astype(BF16)

        _hook(plan, cr, "finish", n == nb)

    io = _carried(
        plan, specs,
        [pl.BlockSpec((BLOCK, Z_END), lambda n: (jnp.maximum(n - 1, 0), 0)),
         pl.BlockSpec((8, CONV_WIDTH), lambda n: (0, 0)), pl.BlockSpec((8, LANES), lambda n: (0, 0))],
        [jax.ShapeDtypeStruct((T, Z_END), BF16), jax.ShapeDtypeStruct((8, CONV_WIDTH), F32),
         jax.ShapeDtypeStruct((8, LANES), F32)],
        [pltpu.VMEM((BLOCK, Z_K), BF16), pltpu.VMEM((BLOCK, 2 * LANES), F32)])
    return _pcall(
        body, name=name, grid=(nb + 1,), compiler_params=_params(1), **io,
    )(z, z, z, z, z, dy, dy, tab, tab, conv_w, sinks, *(plan.arrays if plan else ()))


def _local_sums(pair, chip, place, name):
    arrays, in_specs, out_specs, out_shape = [], [], [], []
    if pair is not None:
        g, sib = pair
        blk = (1, *sib.shape[1:])
        arrays += [g, sib]
        in_specs += [pl.BlockSpec(blk, lambda q, p: (q, p[1], 0)), pl.BlockSpec(blk, lambda q, p: (q, 0, 0))]
        out_specs.append(pl.BlockSpec(blk, lambda q, p: (q, 0, 0)))
        out_shape.append(jax.ShapeDtypeStruct(sib.shape, BF16))
    if chip is not None:
        g2, sib2, recv2 = chip
        blk = (1, *sib2.shape[1:])
        arrays += [g2, sib2, recv2]
        in_specs += [pl.BlockSpec(blk, lambda q, p: (p[0], p[1], 0)), pl.BlockSpec(blk, lambda q, p: (p[0], 0, 0)),
                     pl.BlockSpec(recv2.shape, lambda q, p: (0, 0, 0))]
        out_specs.append(pl.BlockSpec(sib2.shape[1:], lambda q, p: (p[1], 0)))
        out_shape.append(jax.ShapeDtypeStruct(g2.shape[1:], F32))

    def body(place_ref, *refs):
        refs = list(refs)
        ins, outs = refs[:len(arrays)], refs[len(arrays):]
        if pair is not None:
            g_ref, sib_ref = ins[:2]
            outs[0][...] = (g_ref[...] + sib_ref[...].astype(F32)).astype(BF16)
        if chip is not None:
            g_ref, sib_ref, recv_ref = ins[-3:]

            @pl.when(pl.program_id(0) == 0)
            def _():
                total = g_ref[0] + sib_ref[0].astype(F32)
                for j in range(3):
                    total = total + recv_ref[j].astype(F32)
                outs[-1][...] = total

    return _pcall(
        body, name=name,
        grid_spec=pltpu.PrefetchScalarGridSpec(num_scalar_prefetch=1, grid=(N_CHIPS,),
                                               in_specs=in_specs, out_specs=out_specs),
        out_shape=out_shape, compiler_params=_params(1),
    )(place, *arrays)


def _adamw_math(w, g, m, v):
    m = ADAM_B1 * m + (1.0 - ADAM_B1) * g
    v = ADAM_B2 * v + (1.0 - ADAM_B2) * (g * g)
    m_hat = m / (1.0 - ADAM_B1 ** ADAM_STEP)
    v_hat = v / (1.0 - ADAM_B2 ** ADAM_STEP)
    delta = -ADAM_LR * (m_hat / (jnp.sqrt(v_hat) + ADAM_EPS) + ADAM_WD * w)
    return delta, m, v


def _adamw(ws, gs, ms, vs, row_blocks, name):
    n = len(ws)

    def body(*refs):
        w, g, m, v = refs[:n], refs[n:2 * n], refs[2 * n:3 * n], refs[3 * n:4 * n]
        d, mo, vo, go = refs[4 * n:5 * n], refs[5 * n:6 * n], refs[6 * n:7 * n], refs[7 * n:]
        for t in range(n):
            gv = g[t][...]
            delta, m_new, v_new = _adamw_math(w[t][...], gv, m[t][...], v[t][...])
            d[t][...] = delta
            mo[t][...] = m_new
            vo[t][...] = v_new
            go[t][...] = gv

    specs = [pl.BlockSpec((a.shape[0] // row_blocks, a.shape[1]), lambda i: (i, 0)) for a in ws]
    shapes = [jax.ShapeDtypeStruct(a.shape, F32) for a in ws]
    return _pcall(
        body, name=name, grid=(row_blocks,), in_specs=specs * 4, out_specs=specs * 4, out_shape=shapes * 4,
        compiler_params=_params(1),
    )(*ws, *gs, *ms, *vs)


def kernel(x, ffn1_norm, ffn1_w_gate, ffn1_w_up, ffn1_w_down, mix_norm, w_in, conv_w, attn_sinks, w_out, ffn2_norm, ffn2_w_gate, ffn2_w_up, ffn2_w_down, final_norm, loss_target, m_ffn1_norm, m_ffn1_w_gate, m_ffn1_w_up, m_ffn1_w_down, m_mix_norm, m_w_in, m_conv_w, m_attn_sinks, m_w_out, m_ffn2_norm, m_ffn2_w_gate, m_ffn2_w_up, m_ffn2_w_down, m_final_norm, v_ffn1_norm, v_ffn1_w_gate, v_ffn1_w_up, v_ffn1_w_down, v_mix_norm, v_w_in, v_conv_w, v_attn_sinks, v_w_out, v_ffn2_norm, v_ffn2_w_gate, v_ffn2_w_up, v_ffn2_w_down, v_final_norm):
    T, D = x.shape[1], x.shape[2]
    chip = (2 * lax.axis_index("x") + lax.axis_index("y")).astype(jnp.int32)
    core = lax.axis_index("c").astype(jnp.int32)
    place = jnp.stack([chip, core])
    x0 = x[0]
    target = loss_target[0]
    gf = final_norm.reshape(1, D)

    tr = lambda w: jnp.swapaxes(w[0], 0, 1)
    big = [tr(ffn1_w_gate), tr(ffn1_w_up), ffn1_w_down[0], tr(w_in), w_out[0], tr(ffn2_w_gate), tr(ffn2_w_up), ffn2_w_down[0]]
    transposed = [True, True, False, True, False, True, True, False]
    own_b = [w.astype(BF16) for w in big]

    def whole(gathered, own):
        return lax.dynamic_update_slice(gathered, own[None], (chip, 0, 0)).reshape(-1, D)

    got1 = _run_comm(_gather_plan(own_b[0:3]), "gather_ffn1")
    wg1, wu1, wd1 = (whole(g, o) for g, o in zip(got1, own_b[0:3]))
    tab = _rope_tables(T)

    res = _ffn_fwd(x0, ffn1_norm, wg1, wu1, wd1, "ffn1_fwd", _gather_plan(own_b[3:8], [conv_w[0]]))
    x1, h1, gate1, up1, act1 = res[:5]
    win, wout, wg2, wu2, wd2 = (whole(g, o) for g, o in zip(res[5:10], own_b[3:8]))
    convw4 = lax.dynamic_update_slice(res[10], conv_w, (chip, 0, 0))
    convw = jnp.transpose(convw4, (1, 0, 2)).reshape(3, -1)
    z, hm = _norm_matmul(x1, mix_norm, win, tab, "mix_in_fwd")
    ymix = _mix_core_fwd(z, convw, attn_sinks, "mix_core_fwd")
    x2 = _matmul_residual(ymix, wout, x1, "mix_out_fwd")
    dx3, h2, gate2, up2, act2, dgf, loss_part = _ffn_fwd(x2, ffn2_norm, wg2, wu2, wd2, "ffn2_fwd", head=(gf, target))

    dx2, dyb2, dgate2, dup2, dg2 = _ffn_bwd(dx3, x2, ffn2_norm, gate2, up2, wg2, wu2, wd2, "ffn2_bwd")
    dymix, dx2b = _matmul_nt(dx2, wout, "mix_out_bwd")
    dz, dcw, dsk = _mix_core_bwd(z, dymix, tab, convw, attn_sinks, "mix_core_bwd")
    dx1, dgm = _matmul_norm_bwd(dz, win, x1, mix_norm, dx2, "mix_in_bwd")
    dx0, dyb1, dgate1, dup1, dg1 = _ffn_bwd(dx1, x0, ffn1_norm, gate1, up1, wg1, wu1, wd1, "ffn1_bwd")

    pad = lambda a: jnp.pad(a, ((0, 0), (0, LANES - a.shape[1])))
    vec = jnp.concatenate([dg1, dgm, dg2, dgf, dcw[0:3].reshape(1, -1), pad(dsk[:, 0].reshape(1, -1)),
                           pad(loss_part[:, 0:1])], axis=1)

    jobs = [("ffn2_dwg", dgate2, h2, 5), ("ffn2_dwu", dup2, h2, 6), ("ffn2_dwd", act2, dyb2, 7),
            ("ffn1_dwg", dgate1, h1, 0), ("ffn1_dwu", dup1, h1, 1), ("ffn1_dwd", act1, dyb1, 2),
            ("mix_dwin", dz, hm, 3), ("mix_dwout", ymix, dx2b, 4)]
    n_jobs = len(jobs)
    grad, grad_b, from_sib, pair_b, from_chips, half, g_big = ({} for _ in range(7))

    def stage_plans(t):
        plans, takers = [], []
        if 0 <= t - 1 < n_jobs:
            plans.append(_sibling_plan([grad_b[t - 1]]))
            takers.append((from_sib, t - 1))
        if 0 <= t - 2 < n_jobs:
            plans.append(_scatter_plan([pair_b[t - 2]]))
            takers.append((from_chips, t - 2))
        if 0 <= t - 3 < n_jobs:
            plans.append(_join_plan([half[t - 3]]))
            takers.append((g_big, jobs[t - 3][3]))
        return plans, takers

    def after_stage(t, landed, takers):
        for (store, key), arr in zip(takers, landed):
            store[key] = arr
        pair = (grad[t - 1], from_sib[t - 1]) if 0 <= t - 1 < n_jobs else None
        chip = (grad[t - 2], from_sib[t - 2], from_chips[t - 2]) if 0 <= t - 2 < n_jobs else None
        if pair or chip:
            sums = list(_local_sums(pair, chip, place, f"local_sums_{t}"))
            if pair:
                pair_b[t - 1] = sums.pop(0)
            if chip:
                half[t - 2] = sums.pop(0)

    for t, (name_, a, b, _) in enumerate(jobs):
        plans, takers = stage_plans(t)
        if t == 0:
            plans.append(_all_gather_plan(jnp.pad(vec, ((0, 7), (0, 0)))))
        res = _matmul_tn(a, b, DW_ROW_SPLIT, name_, _merge_plans(plans), a_transposed=a.shape[0] != T)
        grad[t], grad_b[t] = (r.reshape(N_CHIPS, -1, D) for r in res[:2])
        landed = list(res[2:])
        if t == 0:
            vec_blocks = landed.pop()
        after_stage(t, landed, takers)

    ws = big
    ms = [tr(m_ffn1_w_gate), tr(m_ffn1_w_up), m_ffn1_w_down[0], tr(m_w_in), m_w_out[0], tr(m_ffn2_w_gate), tr(m_ffn2_w_up), m_ffn2_w_down[0]]
    vs = [tr(v_ffn1_w_gate), tr(v_ffn1_w_up), v_ffn1_w_down[0], tr(v_w_in), v_w_out[0], tr(v_ffn2_w_gate), tr(v_ffn2_w_up), v_ffn2_w_down[0]]
    for t in range(n_jobs, n_jobs + 3):
        plans, takers = stage_plans(t)
        after_stage(t, _run_comm(_merge_plans(plans), f"grads_tail_{t - n_jobs}"), takers)
    upd = {}
    for name_, idx in (("adamw_a", [0, 1, 2, 4]), ("adamw_b", [3, 5, 6, 7])):
        k = len(idx)
        res = _adamw([ws[i] for i in idx], [g_big[i] for i in idx], [ms[i] for i in idx], [vs[i] for i in idx], 8, name_)
        for j, i in enumerate(idx):
            upd[i] = (res[j], res[k + j], res[2 * k + j])
            g_big[i] = res[3 * k + j]

    total = _sum_devices(vec_blocks, "small_sum")[0:1]
    g_n1, g_nm, g_n2, g_nf = (total[:, k * D:(k + 1) * D] for k in range(4))
    cw_full = total[:, 4 * D:4 * D + 3 * CONV_WIDTH].reshape(3, CONV_WIDTH)
    cq = CONV_WIDTH // N_CHIPS
    g_cw = lax.dynamic_slice(cw_full, (0, chip * cq), (3, cq))
    off = 4 * D + 3 * CONV_WIDTH
    g_sk = total[:, off:off + N_Q_HEADS]
    loss = total[0, off + LANES]

    sw = [ffn1_norm, mix_norm, conv_w[0], attn_sinks, ffn2_norm, gf]
    sg = [g_n1, g_nm, g_cw, g_sk, g_n2, g_nf]
    sm = [m_ffn1_norm, m_mix_norm, m_conv_w[0], m_attn_sinks, m_ffn2_norm, m_final_norm.reshape(1, D)]
    sv = [v_ffn1_norm, v_mix_norm, v_conv_w[0], v_attn_sinks, v_ffn2_norm, v_final_norm.reshape(1, D)]
    sres = _adamw(sw, sg, sm, sv, 1, "adamw_small")
    supd = [(sres[j], sres[6 + j], sres[12 + j]) for j in range(6)]

    order = [("s", 0), ("b", 0), ("b", 1), ("b", 2), ("s", 1), ("b", 3), ("s", 2), ("s", 3), ("b", 4),
             ("s", 4), ("b", 5), ("b", 6), ("b", 7), ("s", 5)]

    def leaf(kind, i, which):
        if kind == "b":
            a = g_big[i] if which == 0 else upd[i][which - 1]
            return (jnp.swapaxes(a, 0, 1) if transposed[i] else a)[None]
        a = sg[i] if which == 0 else supd[i][which - 1]
        if i == 2:
            return a[None]
        if i == 5:
            return a.reshape(D)
        return a

    outs = [loss, dx0[None]]
    for which in range(4):
        outs += [leaf(kind, i, which) for kind, i in order]
    return tuple(outs)
```

```python
import jax
import jax.numpy as jnp
import numpy as np
from jax import lax
from jax.experimental import pallas as pl
from jax.experimental.pallas import tpu as pltpu

F32 = jnp.float32
BF16 = jnp.bfloat16
MESH = pl.DeviceIdType.MESH

CONV_WIDTH = 512
N_Q_HEADS = 8
HEAD_DIM = 64
BLOCK = 128
ROPE_THETA = 500000.0
ROT_DIM = 16
RMS_EPS = 1e-5
MASK_VALUE = -1e30
ATTN_SCALE = HEAD_DIM ** -0.5
FFN_RES_SCALE = 0.5
ADAM_LR = 0.001
ADAM_B1 = 0.9
ADAM_B2 = 0.999
ADAM_EPS = 1e-08
ADAM_WD = 0.01
ADAM_STEP = 10

N_CHIPS = 4
N_DEV = 8
LANES = 128
VMEM_LIMIT = 56 * 1024 * 1024

_pcall = pl.pallas_call
HBM_SPEC = pl.BlockSpec(memory_space=pltpu.HBM)
ANY_SPEC = pl.BlockSpec(memory_space=pl.ANY)


def _params(n_axes, vmem=VMEM_LIMIT):
    return pltpu.CompilerParams(dimension_semantics=("arbitrary",) * n_axes, vmem_limit_bytes=vmem)


def _dot(a, b):
    return jnp.dot(a, b, preferred_element_type=F32)


def _dot_nt(a, b):
    return lax.dot_general(a, b, (((1,), (1,)), ((), ())), preferred_element_type=F32)


def _dot_tn(a, b):
    return lax.dot_general(a, b, (((0,), (0,)), ((), ())), preferred_element_type=F32)


def _rms_inv(x):
    return lax.rsqrt(jnp.mean(x * x, axis=-1, keepdims=True) + RMS_EPS)


def _norm_bwd(dh, x, g):
    inv = _rms_inv(x)
    xhat = x * inv
    dg = jnp.sum(dh * xhat, axis=0, keepdims=True)
    dxhat = dh * g
    dx = inv * (dxhat - xhat * jnp.mean(dxhat * xhat, axis=-1, keepdims=True))
    return dx, dg


def _place():
    x, y, c = lax.axis_index("x"), lax.axis_index("y"), lax.axis_index("c")
    chips = [(1 - x, y), (x, 1 - y), (1 - x, 1 - y)]
    return x, y, c, chips


class _Plan:
    def __init__(self, arrays, out_shapes, n_sems, start, finish, middle=None, aliases=None):
        self.arrays, self.out_shapes, self.n_sems = list(arrays), list(out_shapes), n_sems
        self.start, self.finish, self.middle = start, finish, middle
        self.aliases = dict(aliases or {})

    def specs(self):
        k = len(self.arrays)
        sems = [pltpu.SemaphoreType.DMA((self.n_sems,)), pltpu.SemaphoreType.DMA((self.n_sems,))]
        return [HBM_SPEC] * k, [HBM_SPEC] * len(self.out_shapes), self.out_shapes, sems


class _SemSlice:
    def __init__(self, ref, offset):
        self.ref, self.offset = ref, offset

    @property
    def at(self):
        return self

    def __getitem__(self, k):
        return self.ref.at[k + self.offset]


def _merge_plans(plans):
    plans = [p for p in plans if p is not None]
    if len(plans) <= 1:
        return plans[0] if plans else None
    arrays, shapes, aliases, spans, n_sems = [], [], {}, [], 0
    for p in plans:
        a0, o0 = len(arrays), len(shapes)
        spans.append((a0, a0 + len(p.arrays), o0, o0 + len(p.out_shapes), n_sems))
        aliases.update({a0 + i: o0 + j for i, j in p.aliases.items()})
        arrays += p.arrays
        shapes += p.out_shapes
        n_sems += p.n_sems

    def run(which):
        def fn(ins, outs, send_sems, recv_sems):
            for p, (a0, a1, o0, o1, s0) in zip(plans, spans):
                part = getattr(p, which)
                if part is not None:
                    part(ins[a0:a1], outs[o0:o1], _SemSlice(send_sems, s0), _SemSlice(recv_sems, s0))
        return fn

    middle = run("middle") if any(p.middle is not None for p in plans) else None
    return _Plan(arrays, shapes, n_sems, run("start"), run("finish"), middle, aliases)


def _sibling_plan(grads_b):
    n = len(grads_b)

    def copies(ins, outs, send_sems, recv_sems):
        x, y, c, _ = _place()

        def copy(t):
            half = ins[t].shape[1] // 2
            return pltpu.make_async_remote_copy(
                src_ref=ins[t].at[:, pl.ds(pl.multiple_of((1 - c) * half, 16), half), :], dst_ref=outs[t],
                send_sem=send_sems.at[t], recv_sem=recv_sems.at[t], device_id=(x, y, 1 - c), device_id_type=MESH)

        return [copy(t) for t in range(n)]

    def start(*refs):
        for cp in copies(*refs):
            cp.start()

    def finish(*refs):
        for cp in copies(*refs):
            cp.wait()

    shapes = [jax.ShapeDtypeStruct((g.shape[0], g.shape[1] // 2, g.shape[2]), g.dtype) for g in grads_b]
    return _Plan(grads_b, shapes, n, start, finish)


def _scatter_plan(parts_b):
    n = len(parts_b)

    def copies(ins, outs, send_sems, recv_sems):
        x, y, c, chips = _place()

        def copy(t, j):
            px, py = chips[j]
            return pltpu.make_async_remote_copy(
                src_ref=ins[t].at[2 * px + py], dst_ref=outs[t].at[j], send_sem=send_sems.at[3 * t + j],
                recv_sem=recv_sems.at[3 * t + j], device_id=(px, py, c), device_id_type=MESH)

        return [copy(t, j) for t in range(n) for j in range(3)]

    def start(*refs):
        for cp in copies(*refs):
            cp.start()

    def finish(*refs):
        for cp in copies(*refs):
            cp.wait()

    shapes = [jax.ShapeDtypeStruct((3, *p.shape[1:]), p.dtype) for p in parts_b]
    return _Plan(parts_b, shapes, 3 * n, start, finish)


def _gather_plan(shards, small=()):
    n, ns = len(shards), len(small)
    per = 8

    def parts(ins, outs, send_sems, recv_sems):
        x, y, c, chips = _place()
        me = 2 * x + y
        blocks = [2 * px + py for px, py in chips]

        def rows(t, core, piece=None):
            half = ins[t].shape[0] // 2
            if piece is None:
                return pl.ds(pl.multiple_of(core * half, 16), half)
            return pl.ds(pl.multiple_of(core * half + piece * (half // 2), 16), half // 2)

        def remote(src, dst, k, device):
            return pltpu.make_async_remote_copy(src_ref=src, dst_ref=dst, send_sem=send_sems.at[k],
                                                recv_sem=recv_sems.at[k], device_id=device, device_id_type=MESH)

        def first(t, j, block, core):
            return remote(ins[t].at[rows(t, core), :], outs[t].at[block, rows(t, core), :], per * t + j, (*chips[j], c))

        def relay(t, j, block, core):
            ref = outs[t].at[block, rows(t, core, j), :]
            return remote(ref, ref, per * t + 2 + j, (*chips[j], c))

        def passed(t, k, block, core, piece=None):
            ref = outs[t].at[block, rows(t, core, piece), :]
            return remote(ref, ref, per * t + 4 + k, (x, y, 1 - c))

        def whole(s, j, block):
            return remote(ins[n + s], outs[n + s].at[block], per * n + 3 * s + j, (*chips[j], c))

        return c, me, blocks, first, relay, passed, whole

    def start(*refs):
        c, me, _, first, _, _, whole = parts(*refs)
        for t in range(n):
            for j in range(2):
                first(t, j, me, c).start()
        for s in range(ns):
            for j in range(3):
                whole(s, j, me).start()

    def middle(*refs):
        c, _, blocks, first, relay, passed, _ = parts(*refs)
        for t in range(n):
            for j in range(2):
                first(t, j, blocks[j], c).wait_recv()
                passed(t, j, blocks[j], c).start()
                relay(t, 1 - j, blocks[j], c).start()

    def finish(*refs):
        c, me, blocks, first, relay, passed, whole = parts(*refs)
        for t in range(n):
            for j in range(2):
                relay(t, j, blocks[2], c).wait_recv()
                passed(t, 2 + j, blocks[2], c, j).start()
        for t in range(n):
            for j in range(2):
                passed(t, j, blocks[j], 1 - c).wait_recv()
                passed(t, 2 + j, blocks[2], 1 - c, j).wait_recv()
        for s in range(ns):
            for j in range(3):
                whole(s, j, blocks[j]).wait_recv()
        for t in range(n):
            for j in range(2):
                first(t, j, me, c).wait_send()
                relay(t, 1 - j, blocks[j], c).wait_send()
                passed(t, j, blocks[j], c).wait_send()
                passed(t, 2 + j, blocks[2], c, j).wait_send()
        for s in range(ns):
            for j in range(3):
                whole(s, j, me).wait_send()

    arrays = [*shards, *small]
    shapes = [jax.ShapeDtypeStruct((N_CHIPS, *a.shape), a.dtype) for a in arrays]
    return _Plan(arrays, shapes, per * n + 3 * ns, start, finish, middle)


def _run_comm(plan, name):
    k = len(plan.arrays)
    in_specs, out_specs, out_shape, sems = plan.specs()

    def body(*refs):
        cr = (refs[:k], refs[k:k + len(out_shape)], refs[-2], refs[-1])
        plan.start(*cr)
        if plan.middle is not None:
            plan.middle(*cr)
        plan.finish(*cr)

    return _pcall(body, name=name, in_specs=in_specs, out_specs=out_specs, out_shape=out_shape,
                  input_output_aliases=plan.aliases, scratch_shapes=sems)(*plan.arrays)


def _carried(plan, in_specs, out_specs, out_shape, scratch):
    aliases = {}
    if plan is not None:
        p_in, p_out, p_shape, p_sems = plan.specs()
        aliases = {len(in_specs) + i: len(out_specs) + j for i, j in plan.aliases.items()}
        in_specs, out_specs = in_specs + p_in, out_specs + p_out
        out_shape, scratch = out_shape + p_shape, scratch + p_sems
    return dict(in_specs=in_specs, out_specs=out_specs, out_shape=out_shape, scratch_shapes=scratch,
                input_output_aliases=aliases)


def _unpack(refs, n_in, n_out, plan):
    k_in = len(plan.arrays) if plan else 0
    k_out = len(plan.out_shapes) if plan else 0
    ins = refs[:n_in]
    outs = refs[n_in + k_in:n_in + k_in + n_out]
    rest = refs[n_in + k_in + n_out + k_out:]
    if plan is None:
        return ins, outs, rest, None
    cr = (refs[n_in:n_in + k_in], refs[n_in + k_in + n_out:n_in + k_in + n_out + k_out], rest[-2], rest[-1])
    return ins, outs, rest[:-2], cr


def _hook(plan, cr, which, cond):
    fn = getattr(plan, which) if plan is not None else None
    if fn is not None:
        pl.when(cond)(lambda: fn(*cr))


def _join_plan(shards):
    n = len(shards)

    def copy(ins, outs, send_sems, recv_sems, t, core):
        x, y, c, _ = _place()
        half = ins[t].shape[0] // 2
        rows = pl.ds(pl.multiple_of(core * half, 8), half)
        return pltpu.make_async_remote_copy(
            src_ref=ins[t].at[rows, :], dst_ref=outs[t].at[rows, :], send_sem=send_sems.at[t],
            recv_sem=recv_sems.at[t], device_id=(x, y, 1 - c), device_id_type=MESH)

    def start(*refs):
        c = lax.axis_index("c")
        for t in range(n):
            copy(*refs, t, c).start()

    def finish(*refs):
        c = lax.axis_index("c")
        for t in range(n):
            copy(*refs, t, 1 - c).wait_recv()
        for t in range(n):
            copy(*refs, t, c).wait_send()

    shapes = [jax.ShapeDtypeStruct(s.shape, s.dtype) for s in shards]
    return _Plan(shards, shapes, n, start, finish, aliases={t: t for t in range(n)})


def _all_gather_plan(vec):
    def parts(ins, outs, send_sems, recv_sems):
        x, y, c, _ = _place()
        me = 4 * x + 2 * y + c
        rel = [((k >> 2) & 1, (k >> 1) & 1, k & 1) for k in range(1, N_DEV)]

        def peer(k):
            fx, fy, fc = rel[k]
            return (x ^ fx, y ^ fy, c ^ fc)

        def copy(k, dev):
            return pltpu.make_async_remote_copy(
                src_ref=ins[0], dst_ref=outs[0].at[dev], send_sem=send_sems.at[k], recv_sem=recv_sems.at[k],
                device_id=peer(k), device_id_type=MESH)

        mine = pltpu.make_async_copy(ins[0], outs[0].at[me], send_sems.at[N_DEV - 1])
        return me, peer, copy, mine

    def start(*refs):
        me, _, copy, mine = parts(*refs)
        mine.start()
        for k in range(N_DEV - 1):
            copy(k, me).start()

    def finish(*refs):
        me, peer, copy, mine = parts(*refs)
        for k in range(N_DEV - 1):
            px, py, pc = peer(k)
            copy(k, 4 * px + 2 * py + pc).wait_recv()
        for k in range(N_DEV - 1):
            copy(k, me).wait_send()
        mine.wait()

    return _Plan([vec], [jax.ShapeDtypeStruct((N_DEV, *vec.shape), vec.dtype)], N_DEV, start, finish)


def _sum_devices(blocks, name):
    def body(b_ref, o_ref):
        total = b_ref[0]
        for dev in range(1, N_DEV):
            total = total + b_ref[dev]
        o_ref[...] = total

    return _pcall(body, name=name, in_specs=[pl.BlockSpec(memory_space=pltpu.VMEM)],
                  out_specs=pl.BlockSpec(memory_space=pltpu.VMEM),
                  out_shape=jax.ShapeDtypeStruct(blocks.shape[1:], F32))(blocks)


TOKEN_TILE = 512
PROJ_TOKEN_TILE = 1024
OUT_PROJ_TOKEN_TILE = 2048
ADAMW_ROW_BLOCKS = 4
BWD_VMEM_LIMIT = 62 * 1024 * 1024
DW_TOKEN_TILE = 2048
DW_ROW_SPLIT = 2
MXU_COLS = 256


def _chunks(n):
    out, c0 = [], 0
    while c0 < n:
        size = min(MXU_COLS, n - c0)
        out.append((c0, size))
        c0 += size
    return out


def _load_weights(hbm_refs, vmem_refs, sems):
    copies = [pltpu.make_async_copy(h, v, sems.at[k]) for k, (h, v) in enumerate(zip(hbm_refs, vmem_refs))]
    for cp in copies:
        cp.start()
    for cp in copies:
        cp.wait()


def _ffn_fwd(x, g, wgt, wut, wd, name, plan=None, head=None):
    T, D = x.shape
    F = wgt.shape[0]
    tm = min(T, TOKEN_TILE)
    ni = T // tm
    n_head = 2 if head is not None else 0

    def body(*refs):
        ins, outs, scratch, cr = _unpack(refs, 5 + n_head, 5 + n_head, plan)
        x_ref, g_ref, wg_hbm, wu_hbm, wd_hbm = ins[:5]
        xo_ref, h_ref, gate_ref, up_ref, act_ref = outs[:5]
        wg_ref, wu_ref, wd_ref, sems = scratch
        i = pl.program_id(0)
        _hook(plan, cr, "start", i == 0)

        @pl.when(i == 0)
        def _():
            _load_weights((wg_hbm, wu_hbm, wd_hbm), (wg_ref, wu_ref, wd_ref), sems)

        xv = x_ref[...]
        h = ((xv * _rms_inv(xv)) * g_ref[...]).astype(BF16)
        h_ref[...] = h
        for c0, size in _chunks(F):
            gate = _dot_nt(h, wg_ref[c0:c0 + size, :])
            up = _dot_nt(h, wu_ref[c0:c0 + size, :])
            gate_ref[:, c0:c0 + size] = gate.astype(BF16)
            up_ref[:, c0:c0 + size] = up.astype(BF16)
            act_ref[:, c0:c0 + size] = (gate * jax.nn.sigmoid(gate) * up).astype(BF16)
        y = x_ref[...] + FFN_RES_SCALE * _dot(act_ref[...], wd_ref[...])
        if head is None:
            xo_ref[...] = y
        else:
            gf_ref, t_ref = ins[5:]
            dgf_ref, loss_ref = outs[5:]

            @pl.when(i == 0)
            def _():
                dgf_ref[...] = jnp.zeros_like(dgf_ref)
                loss_ref[...] = jnp.zeros_like(loss_ref)

            gf = gf_ref[...]
            diff = (y * _rms_inv(y)) * gf - t_ref[...]
            loss_ref[...] += 0.5 * jnp.sum(jnp.mean(diff * diff, axis=-1, keepdims=True))
            dy, dgf = _norm_bwd(diff * (1.0 / D), y, gf)
            xo_ref[...] = dy
            dgf_ref[...] += dgf
        _hook(plan, cr, "middle", i == ni // 2)
        _hook(plan, cr, "finish", i == ni - 1)

    const = lambda shape: pl.BlockSpec(shape, lambda i: (0, 0))
    rows = lambda width: pl.BlockSpec((tm, width), lambda i: (i, 0))
    in_specs = [rows(D), const((1, D)), ANY_SPEC, ANY_SPEC, ANY_SPEC]
    out_specs = [rows(D), rows(D), rows(F), rows(F), rows(F)]
    out_shape = [jax.ShapeDtypeStruct((T, D), F32), jax.ShapeDtypeStruct((T, D), BF16),
                 jax.ShapeDtypeStruct((T, F), BF16), jax.ShapeDtypeStruct((T, F), BF16), jax.ShapeDtypeStruct((T, F), BF16)]
    if head is not None:
        in_specs += [const((1, D)), rows(D)]
        out_specs += [const((1, D)), const((1, LANES))]
        out_shape += [jax.ShapeDtypeStruct((1, D), F32), jax.ShapeDtypeStruct((1, LANES), F32)]
    io = _carried(plan, in_specs, out_specs, out_shape,
                  [pltpu.VMEM((F, D), BF16), pltpu.VMEM((F, D), BF16), pltpu.VMEM((F, D), BF16),
                   pltpu.SemaphoreType.DMA((3,))])
    return _pcall(
        body, name=name, grid=(ni,), compiler_params=_params(1), **io,
    )(x, g, wgt, wut, wd, *(head or ()), *(plan.arrays if plan else ()))


def _ffn_bwd(dy, x, g, gate, up, wgt, wut, wd, name):
    T, D = x.shape
    F = wgt.shape[0]
    tm = min(T, TOKEN_TILE)
    ni = T // tm

    def body(dy_ref, x_ref, g_ref, gate_ref, up_ref, wg_hbm, wu_hbm, wd_hbm,
             dx_ref, dyb_ref, dgate_ref, dup_ref, dg_ref, wg_ref, wu_ref, wd_ref, sems):
        @pl.when(pl.program_id(0) == 0)
        def _():
            _load_weights((wg_hbm, wu_hbm, wd_hbm), (wg_ref, wu_ref, wd_ref), sems)
            dg_ref[...] = jnp.zeros_like(dg_ref)

        dyb = (FFN_RES_SCALE * dy_ref[...]).astype(BF16)
        dyb_ref[...] = dyb
        for c0, size in _chunks(F):
            dact = _dot_nt(dyb, wd_ref[c0:c0 + size, :])
            gt = gate_ref[:, c0:c0 + size].astype(F32)
            u = up_ref[:, c0:c0 + size].astype(F32)
            sig = jax.nn.sigmoid(gt)
            dup_ref[:, c0:c0 + size] = (dact * (gt * sig)).astype(BF16)
            dgate_ref[:, c0:c0 + size] = (dact * u * (sig * (1.0 + gt * (1.0 - sig)))).astype(BF16)
        dh = _dot(dgate_ref[...], wg_ref[...]) + _dot(dup_ref[...], wu_ref[...])
        dxn, dg = _norm_bwd(dh, x_ref[...], g_ref[...])
        dx_ref[...] = dy_ref[...] + dxn
        dg_ref[...] += dg

    return _pcall(
        body, name=name, grid=(ni,),
        in_specs=[pl.BlockSpec((tm, D), lambda i: (i, 0)), pl.BlockSpec((tm, D), lambda i: (i, 0)),
                  pl.BlockSpec((1, D), lambda i: (0, 0)),
                  pl.BlockSpec((tm, F), lambda i: (i, 0)), pl.BlockSpec((tm, F), lambda i: (i, 0)),
                  ANY_SPEC, ANY_SPEC, ANY_SPEC],
        out_specs=[pl.BlockSpec((tm, D), lambda i: (i, 0)), pl.BlockSpec((tm, D), lambda i: (i, 0)),
                   pl.BlockSpec((tm, F), lambda i: (i, 0)), pl.BlockSpec((tm, F), lambda i: (i, 0)),
                   pl.BlockSpec((1, D), lambda i: (0, 0))],
        out_shape=[jax.ShapeDtypeStruct((T, D), F32), jax.ShapeDtypeStruct((T, D), BF16),
                   jax.ShapeDtypeStruct((T, F), BF16), jax.ShapeDtypeStruct((T, F), BF16),
                   jax.ShapeDtypeStruct((1, D), F32)],
        scratch_shapes=[pltpu.VMEM((F, D), BF16), pltpu.VMEM((F, D), BF16), pltpu.VMEM((F, D), BF16),
                        pltpu.SemaphoreType.DMA((3,))],
        compiler_params=_params(1, BWD_VMEM_LIMIT),
    )(dy, x, g, gate, up, wgt, wut, wd)


def _matmul_tn(a, b, row_split, name, plan=None):
    T, n1 = a.shape
    n2 = b.shape[1]
    tn = n1 // row_split
    tk = min(T, DW_TOKEN_TILE)
    nk = T // tk

    def body(*refs):
        (a_ref, b_ref), (o_ref, ob_ref), _, cr = _unpack(refs, 2, 2, plan)
        j = pl.program_id(0)
        k = pl.program_id(1)
        _hook(plan, cr, "start", jnp.logical_and(j == 0, k == 0))

        @pl.when(k == 0)
        def _():
            o_ref[...] = jnp.zeros_like(o_ref)

        o_ref[...] += _dot_tn(a_ref[...], b_ref[...])

        @pl.when(k == nk - 1)
        def _():
            ob_ref[...] = o_ref[...].astype(BF16)

        _hook(plan, cr, "finish", jnp.logical_and(j == row_split - 1, k == nk - 1))

    io = _carried(
        plan,
        [pl.BlockSpec((tk, tn), lambda j, k: (k, j)), pl.BlockSpec((tk, n2), lambda j, k: (k, 0))],
        [pl.BlockSpec((tn, n2), lambda j, k: (j, 0)), pl.BlockSpec((tn, n2), lambda j, k: (j, 0))],
        [jax.ShapeDtypeStruct((n1, n2), F32), jax.ShapeDtypeStruct((n1, n2), BF16)], [])
    return _pcall(
        body, name=name, grid=(row_split, nk), compiler_params=_params(2), **io,
    )(a, b, *(plan.arrays if plan else ()))


def _norm_matmul(x, g, wt, tab, name):
    T, D = x.shape
    n = wt.shape[0]
    tm = min(T, PROJ_TOKEN_TILE)

    def body(x_ref, g_ref, w_ref, tab_ref, z_ref, h_ref):
        xv = x_ref[...]
        h = ((xv * _rms_inv(xv)) * g_ref[...]).astype(BF16)
        h_ref[...] = h
        z = _dot_nt(h, w_ref[...])
        z_ref[:, 0:Z_Q] = z[:, 0:Z_Q]
        tab_v = tab_ref[...]
        for c0 in range(Z_Q, Z_V, LANES):
            z_ref[:, c0:c0 + LANES] = _rot(z[:, c0:c0 + LANES], tab_v)
        z_ref[:, Z_V:Z_END] = z[:, Z_V:Z_END]

    return _pcall(
        body, name=name, grid=(T // tm,),
        in_specs=[pl.BlockSpec((tm, D), lambda i: (i, 0)), pl.BlockSpec((1, D), lambda i: (0, 0)),
                  pl.BlockSpec((n, D), lambda i: (0, 0)), pl.BlockSpec((tm, 3 * LANES), lambda i: (i, 0))],
        out_specs=[pl.BlockSpec((tm, n), lambda i: (i, 0)), pl.BlockSpec((tm, D), lambda i: (i, 0))],
        out_shape=[jax.ShapeDtypeStruct((T, n), F32), jax.ShapeDtypeStruct((T, D), BF16)],
        compiler_params=_params(1),
    )(x, g, wt, tab)


def _matmul_residual(y, w, x, name):
    T, D = x.shape
    kdim = y.shape[1]
    tm = min(T, OUT_PROJ_TOKEN_TILE)

    def body(y_ref, w_ref, x_ref, o_ref):
        o_ref[...] = x_ref[...] + _dot(y_ref[...], w_ref[...])

    return _pcall(
        body, name=name, grid=(T // tm,),
        in_specs=[pl.BlockSpec((tm, kdim), lambda i: (i, 0)), pl.BlockSpec((kdim, D), lambda i: (0, 0)),
                  pl.BlockSpec((tm, D), lambda i: (i, 0))],
        out_specs=pl.BlockSpec((tm, D), lambda i: (i, 0)),
        out_shape=jax.ShapeDtypeStruct((T, D), F32),
        compiler_params=_params(1),
    )(y, w, x)


def _matmul_nt(dx, w, name):
    T, D = dx.shape
    kdim = w.shape[0]
    tm = min(T, OUT_PROJ_TOKEN_TILE)

    def body(dx_ref, w_ref, dy_ref, dxb_ref):
        dxb = dx_ref[...].astype(BF16)
        dxb_ref[...] = dxb
        dy_ref[...] = _dot_nt(dxb, w_ref[...])

    return _pcall(
        body, name=name, grid=(T // tm,),
        in_specs=[pl.BlockSpec((tm, D), lambda i: (i, 0)), pl.BlockSpec((kdim, D), lambda i: (0, 0))],
        out_specs=[pl.BlockSpec((tm, kdim), lambda i: (i, 0)), pl.BlockSpec((tm, D), lambda i: (i, 0))],
        out_shape=[jax.ShapeDtypeStruct((T, kdim), F32), jax.ShapeDtypeStruct((T, D), BF16)],
        compiler_params=_params(1),
    )(dx, w)


def _matmul_norm_bwd(dz, wt, x, g, dres, name):
    T, D = x.shape
    n = dz.shape[1]
    tm = min(T, PROJ_TOKEN_TILE)

    def body(dz_ref, w_ref, x_ref, g_ref, dres_ref, dx_ref, dg_ref):
        @pl.when(pl.program_id(0) == 0)
        def _():
            dg_ref[...] = jnp.zeros_like(dg_ref)

        dh = _dot(dz_ref[...], w_ref[...])
        dxn, dg = _norm_bwd(dh, x_ref[...], g_ref[...])
        dx_ref[...] = dres_ref[...] + dxn
        dg_ref[...] += dg

    return _pcall(
        body, name=name, grid=(T // tm,),
        in_specs=[pl.BlockSpec((tm, n), lambda i: (i, 0)), pl.BlockSpec((n, D), lambda i: (0, 0)),
                  pl.BlockSpec((tm, D), lambda i: (i, 0)), pl.BlockSpec((1, D), lambda i: (0, 0)),
                  pl.BlockSpec((tm, D), lambda i: (i, 0))],
        out_specs=[pl.BlockSpec((tm, D), lambda i: (i, 0)), pl.BlockSpec((1, D), lambda i: (0, 0))],
        out_shape=[jax.ShapeDtypeStruct((T, D), F32), jax.ShapeDtypeStruct((1, D), F32)],
        compiler_params=_params(1),
    )(dz, wt, x, g, dres)


Z_Q = 3 * CONV_WIDTH
Z_K = Z_Q + N_Q_HEADS * HEAD_DIM
Z_V = Z_K + LANES
Z_END = Z_V + LANES


def _rope_tables(T):
    half = ROT_DIM // 2
    inv_freq = ROPE_THETA ** (-jnp.arange(0, ROT_DIM, 2, dtype=F32) / ROT_DIM)
    ang = inv_freq[:, None] * jnp.arange(T, dtype=F32)[None, :]
    cos_sin = jnp.concatenate([jnp.cos(ang), jnp.sin(ang)], axis=0)
    select = np.zeros((2 * half, 3 * LANES), np.float32)
    const = np.zeros((1, 3 * LANES), np.float32)
    for lane in range(LANES):
        d = lane % HEAD_DIM
        if d < half:
            select[d, lane] = 1.0
            select[half + d, LANES + lane] = -1.0
        elif d < ROT_DIM:
            select[d - half, lane] = 1.0
            select[d, 2 * LANES + lane] = 1.0
        else:
            const[0, lane] = 1.0
    tab = lax.dot_general(cos_sin, jnp.asarray(select), (((0,), (0,)), ((), ())),
                          precision=lax.Precision.HIGHEST, preferred_element_type=F32)
    return tab + jnp.asarray(const)


def _tab3(tab):
    return tab[:, 0:LANES], tab[:, LANES:2 * LANES], tab[:, 2 * LANES:3 * LANES]


def _rot(x, tab):
    c, s1, s2 = _tab3(tab)
    return x * c + pltpu.roll(x, LANES - ROT_DIM // 2, 1) * s1 + pltpu.roll(x, ROT_DIM // 2, 1) * s2


def _rot_t(d, tab):
    c, s1, s2 = _tab3(tab)
    return d * c + pltpu.roll(d * s1, ROT_DIM // 2, 1) + pltpu.roll(d * s2, LANES - ROT_DIM // 2, 1)


def _head_pads(a):
    lo = lax.broadcasted_iota(jnp.int32, a.shape, 1) < HEAD_DIM
    nat0 = jnp.where(lo, a, 0.0)
    nat1 = jnp.where(lo, 0.0, a)
    return {
        (0, 0): nat0.astype(BF16), (0, 1): pltpu.roll(nat0, HEAD_DIM, 1).astype(BF16),
        (1, 0): pltpu.roll(nat1, HEAD_DIM, 1).astype(BF16), (1, 1): nat1.astype(BF16),
    }


def _from_pads(even, odd, kv):
    lo = lax.broadcasted_iota(jnp.int32, even.shape, 1) < HEAD_DIM
    if kv == 0:
        return jnp.where(lo, even + pltpu.roll(odd, HEAD_DIM, 1), 0.0)
    return jnp.where(lo, 0.0, pltpu.roll(even, HEAD_DIM, 1) + odd)


N_GROUPS = 4


def _group_head(g, r):
    kv, par = divmod(g, 2)
    return 2 * (2 * kv + r) + par


def _window_mask_t(has_prev):
    jj = lax.broadcasted_iota(jnp.int32, (2 * BLOCK, 2 * BLOCK), 0)
    ii = lax.broadcasted_iota(jnp.int32, (2 * BLOCK, 2 * BLOCK), 1) & (BLOCK - 1)
    rel = jj - BLOCK - ii
    return (rel <= 0) & (rel > -BLOCK) & ((jj >= BLOCK) | has_prev)


def _sink_row(sink_ref, g):
    lane = lax.broadcasted_iota(jnp.int32, (1, 2 * BLOCK), 1)
    return jnp.where(lane < BLOCK, sink_ref[0, _group_head(g, 0)], sink_ref[0, _group_head(g, 1)])


def _attn_probs_t(q2, kp, mask, sink_ref):
    out = []
    for kv in range(2):
        q_st = jnp.concatenate([q2[2 * kv], q2[2 * kv + 1]], axis=0)
        for par in range(2):
            s = jnp.where(mask, _dot_nt(kp[(kv, par)], q_st), MASK_VALUE)
            sink = _sink_row(sink_ref, 2 * kv + par)
            m = jnp.maximum(jnp.max(s, axis=0, keepdims=True), sink)
            p = jnp.exp(s - m)
            esink = jnp.exp(sink - m)
            rden = 1.0 / (jnp.sum(p, axis=0, keepdims=True) + esink)
            out.append((p * rden, esink * rden))
    return out


def _conv_taps(cg, u, cg_prev, u_prev, has_prev):
    vv = cg * u
    halo = jnp.where(has_prev, cg_prev * u_prev, 0.0)
    ext = jnp.concatenate([halo, vv], axis=0)
    rows = ext.shape[0]
    vv1 = pltpu.roll(ext, 1, 0)[8:rows]
    vv2 = pltpu.roll(ext, 2, 0)[8:rows]
    return vv, vv1, vv2


def _mix_specs(nb):
    cur = lambda n: jnp.minimum(n, nb - 1)
    prev = lambda n: jnp.maximum(jnp.minimum(n, nb - 1) - 1, 0)
    rows8_prev = lambda n: jnp.maximum(16 * jnp.minimum(n, nb - 1) - 1, 0)
    return cur, prev, [
        pl.BlockSpec((BLOCK, Z_END), lambda n: (cur(n), 0)),
        pl.BlockSpec((BLOCK, 2 * LANES), lambda n: (prev(n), Z_K // (2 * LANES))),
        pl.BlockSpec((8, CONV_WIDTH), lambda n: (rows8_prev(n), 1)),
        pl.BlockSpec((8, CONV_WIDTH), lambda n: (rows8_prev(n), 2)),
        pl.BlockSpec((BLOCK, 3 * LANES), lambda n: (cur(n), 0)),
        pl.BlockSpec((BLOCK, 3 * LANES), lambda n: (prev(n), 0)),
        pl.BlockSpec((3, CONV_WIDTH), lambda n: (0, 0)),
        pl.BlockSpec(memory_space=pltpu.SMEM),
    ]


def _mix_core_fwd(z, conv_w, sinks, name):
    T = z.shape[0]
    nb = T // BLOCK
    _, _, specs = _mix_specs(nb)
    specs = specs[:4] + specs[6:]

    def body(z_ref, zkvp_ref, cgp_ref, up_ref, cw_ref, sink_ref, y_ref):
        has_prev = pl.program_id(0) > 0
        bg = z_ref[:, 0:CONV_WIDTH]
        vv, vv1, vv2 = _conv_taps(z_ref[:, CONV_WIDTH:2 * CONV_WIDTH], z_ref[:, 2 * CONV_WIDTH:Z_Q],
                                  cgp_ref[...], up_ref[...], has_prev)
        conv = cw_ref[0:1, :] * vv2 + cw_ref[1:2, :] * vv1 + cw_ref[2:3, :] * vv
        y_ref[:, 0:CONV_WIDTH] = (bg * conv).astype(BF16)

        k_all = jnp.concatenate([zkvp_ref[:, 0:LANES], z_ref[:, Z_K:Z_V]], axis=0)
        v_all = jnp.concatenate([zkvp_ref[:, LANES:2 * LANES], z_ref[:, Z_V:Z_END]], axis=0)
        kp = _head_pads(k_all)
        vp = _head_pads(v_all)
        q2 = [(z_ref[:, Z_Q + LANES * c:Z_Q + LANES * (c + 1)] * ATTN_SCALE).astype(BF16) for c in range(N_Q_HEADS // 2)]
        probs = _attn_probs_t(q2, kp, _window_mask_t(has_prev), sink_ref)
        for kv in range(2):
            o_t = (_dot_tn(vp[(kv, 0)], probs[2 * kv][0].astype(BF16))
                   + _dot_tn(vp[(kv, 1)], probs[2 * kv + 1][0].astype(BF16)))
            for r in range(2):
                c = 2 * kv + r
                y_ref[:, CONV_WIDTH + LANES * c:CONV_WIDTH + LANES * (c + 1)] = o_t[:, BLOCK * r:BLOCK * (r + 1)].T.astype(BF16)

    return _pcall(
        body, name=name, grid=(nb,), in_specs=specs,
        out_specs=pl.BlockSpec((BLOCK, 2 * CONV_WIDTH), lambda n: (n, 0)),
        out_shape=jax.ShapeDtypeStruct((T, 2 * CONV_WIDTH), BF16),
        compiler_params=_params(1),
    )(z, z, z, z, conv_w, sinks)


def _mix_core_bwd(z, dy, tab, conv_w, sinks, name):
    T = z.shape[0]
    nb = T // BLOCK
    cur, _, specs = _mix_specs(nb)
    rows8_next = lambda n: jnp.minimum(16 * (cur(n) + 1), 16 * nb - 1)
    specs = specs[:4] + [
        pl.BlockSpec((8, CONV_WIDTH), lambda n: (rows8_next(n), 0)),
        pl.BlockSpec((BLOCK, 2 * CONV_WIDTH), lambda n: (cur(n), 0)),
        pl.BlockSpec((8, CONV_WIDTH), lambda n: (rows8_next(n), 0)),
    ] + specs[4:]

    def body(z_ref, zkvp_ref, cgp_ref, up_ref, bgn_ref, dy_ref, dyn_ref, tab_ref, tabp_ref, cw_ref, sink_ref,
             dz_ref, dcw_ref, dsk_ref, main_ref, kv_ref):
        n = pl.program_id(0)

        @pl.when(n == 0)
        def _():
            main_ref[...] = jnp.zeros_like(main_ref)
            kv_ref[...] = jnp.zeros_like(kv_ref)
            dcw_ref[...] = jnp.zeros_like(dcw_ref)
            dsk_ref[...] = jnp.zeros_like(dsk_ref)

        @pl.when(n < nb)
        def _():
            has_prev = n > 0
            has_next = n < nb - 1
            bg = z_ref[:, 0:CONV_WIDTH]
            cg = z_ref[:, CONV_WIDTH:2 * CONV_WIDTH]
            u = z_ref[:, 2 * CONV_WIDTH:Z_Q]
            vv, vv1, vv2 = _conv_taps(cg, u, cgp_ref[...], up_ref[...], has_prev)
            w0, w1, w2 = cw_ref[0:1, :], cw_ref[1:2, :], cw_ref[2:3, :]
            dyc = dy_ref[:, 0:CONV_WIDTH]
            dbg = dyc * (w0 * vv2 + w1 * vv1 + w2 * vv)
            dconv = dyc * bg
            dconv_next = jnp.where(has_next, dyn_ref[...] * bgn_ref[...], 0.0)
            ext = jnp.concatenate([dconv, dconv_next], axis=0)
            rows = ext.shape[0]
            dvv = w2 * dconv + w1 * pltpu.roll(ext, rows - 1, 0)[0:BLOCK] + w0 * pltpu.roll(ext, rows - 2, 0)[0:BLOCK]
            dcw_ref[0:1, :] += jnp.sum(dconv * vv2, axis=0, keepdims=True)
            dcw_ref[1:2, :] += jnp.sum(dconv * vv1, axis=0, keepdims=True)
            dcw_ref[2:3, :] += jnp.sum(dconv * vv, axis=0, keepdims=True)

            tab_c = tab_ref[...]
            tab_p = tabp_ref[...]
            k_all = jnp.concatenate([zkvp_ref[:, 0:LANES], z_ref[:, Z_K:Z_V]], axis=0)
            v_all = jnp.concatenate([zkvp_ref[:, LANES:2 * LANES], z_ref[:, Z_V:Z_END]], axis=0)
            kp = _head_pads(k_all)
            vp = _head_pads(v_all)
            chunks = range(N_Q_HEADS // 2)
            q2 = [(z_ref[:, Z_Q + LANES * c:Z_Q + LANES * (c + 1)] * ATTN_SCALE).astype(BF16) for c in chunks]
            do2 = [dy_ref[:, CONV_WIDTH + LANES * c:CONV_WIDTH + LANES * (c + 1)].astype(BF16) for c in chunks]
            probs = _attn_probs_t(q2, kp, _window_mask_t(has_prev), sink_ref)
            dq_chunks = []
            dk_nat = jnp.zeros((2 * BLOCK, LANES), F32)
            dv_nat = jnp.zeros((2 * BLOCK, LANES), F32)
            for kv in range(2):
                q_st = jnp.concatenate([q2[2 * kv], q2[2 * kv + 1]], axis=0)
                do_st = jnp.concatenate([do2[2 * kv], do2[2 * kv + 1]], axis=0)
                dq_t = jnp.zeros((LANES, 2 * BLOCK), F32)
                dk_par, dv_par = [], []
                for par in range(2):
                    g = 2 * kv + par
                    pr, psink = probs[g]
                    dp = _dot_nt(vp[(kv, par)], do_st)
                    delta = jnp.sum(dp * pr, axis=0, keepdims=True)
                    ds = (pr * (dp - delta)).astype(BF16)
                    dsink = -psink * delta
                    for r in range(2):
                        h = _group_head(g, r)
                        dsk_ref[h:h + 1, :] += jnp.sum(dsink[:, BLOCK * r:BLOCK * (r + 1)])
                    dq_t = dq_t + _dot_tn(kp[(kv, par)], ds)
                    dk_par.append(_dot(ds, q_st))
                    dv_par.append(_dot(pr.astype(BF16), do_st))
                for r in range(2):
                    dq_chunks.append(_rot_t(dq_t[:, BLOCK * r:BLOCK * (r + 1)].T * ATTN_SCALE, tab_c))
                dk_nat = dk_nat + _from_pads(dk_par[0], dk_par[1], kv)
                dv_nat = dv_nat + _from_pads(dv_par[0], dv_par[1], kv)

            dk_prev = _rot_t(kv_ref[:, 0:LANES] + dk_nat[0:BLOCK], tab_p)
            dv_prev = kv_ref[:, LANES:2 * LANES] + dv_nat[0:BLOCK]
            dz_ref[:, 0:Z_K] = main_ref[...]
            dz_ref[:, Z_K:Z_V] = dk_prev.astype(BF16)
            dz_ref[:, Z_V:Z_END] = dv_prev.astype(BF16)
            main_ref[:, 0:CONV_WIDTH] = dbg.astype(BF16)
            main_ref[:, CONV_WIDTH:2 * CONV_WIDTH] = (dvv * u).astype(BF16)
            main_ref[:, 2 * CONV_WIDTH:Z_Q] = (dvv * cg).astype(BF16)
            for c in range(N_Q_HEADS // 2):
                main_ref[:, Z_Q + LANES * c:Z_Q + LANES * (c + 1)] = dq_chunks[c].astype(BF16)
            kv_ref[:, 0:LANES] = dk_nat[BLOCK:2 * BLOCK]
            kv_ref[:, LANES:2 * LANES] = dv_nat[BLOCK:2 * BLOCK]

        @pl.when(n == nb)
        def _():
            dz_ref[:, 0:Z_K] = main_ref[...]
            dz_ref[:, Z_K:Z_V] = _rot_t(kv_ref[:, 0:LANES], tab_ref[...]).astype(BF16)
            dz_ref[:, Z_V:Z_END] = kv_ref[:, LANES:2 * LANES].astype(BF16)

    return _pcall(
        body, name=name, grid=(nb + 1,), in_specs=specs,
        out_specs=[pl.BlockSpec((BLOCK, Z_END), lambda n: (jnp.maximum(n - 1, 0), 0)),
                   pl.BlockSpec((8, CONV_WIDTH), lambda n: (0, 0)), pl.BlockSpec((8, LANES), lambda n: (0, 0))],
        out_shape=[jax.ShapeDtypeStruct((T, Z_END), BF16), jax.ShapeDtypeStruct((8, CONV_WIDTH), F32),
                   jax.ShapeDtypeStruct((8, LANES), F32)],
        scratch_shapes=[pltpu.VMEM((BLOCK, Z_K), BF16), pltpu.VMEM((BLOCK, 2 * LANES), F32)],
        compiler_params=_params(1),
    )(z, z, z, z, z, dy, dy, tab, tab, conv_w, sinks)


def _local_sums(pair, chip, place, name):
    arrays, in_specs, out_specs, out_shape = [], [], [], []
    if pair is not None:
        g, sib = pair
        blk = (1, *sib.shape[1:])
        arrays += [g, sib]
        in_specs += [pl.BlockSpec(blk, lambda q, p: (q, p[1], 0)), pl.BlockSpec(blk, lambda q, p: (q, 0, 0))]
        out_specs.append(pl.BlockSpec(blk, lambda q, p: (q, 0, 0)))
        out_shape.append(jax.ShapeDtypeStruct(sib.shape, BF16))
    if chip is not None:
        g2, sib2, recv2 = chip
        blk = (1, *sib2.shape[1:])
        arrays += [g2, sib2, recv2]
        in_specs += [pl.BlockSpec(blk, lambda q, p: (p[0], p[1], 0)), pl.BlockSpec(blk, lambda q, p: (p[0], 0, 0)),
                     pl.BlockSpec(recv2.shape, lambda q, p: (0, 0, 0))]
        out_specs.append(pl.BlockSpec(sib2.shape[1:], lambda q, p: (p[1], 0)))
        out_shape.append(jax.ShapeDtypeStruct(g2.shape[1:], F32))

    def body(place_ref, *refs):
        refs = list(refs)
        ins, outs = refs[:len(arrays)], refs[len(arrays):]
        if pair is not None:
            g_ref, sib_ref = ins[:2]
            outs[0][...] = (g_ref[...] + sib_ref[...].astype(F32)).astype(BF16)
        if chip is not None:
            g_ref, sib_ref, recv_ref = ins[-3:]

            @pl.when(pl.program_id(0) == 0)
            def _():
                total = g_ref[0] + sib_ref[0].astype(F32)
                for j in range(3):
                    total = total + recv_ref[j].astype(F32)
                outs[-1][...] = total

    return _pcall(
        body, name=name,
        grid_spec=pltpu.PrefetchScalarGridSpec(num_scalar_prefetch=1, grid=(N_CHIPS,),
                                               in_specs=in_specs, out_specs=out_specs),
        out_shape=out_shape, compiler_params=_params(1),
    )(place, *arrays)


def _adamw_math(w, g, m, v):
    m = ADAM_B1 * m + (1.0 - ADAM_B1) * g
    v = ADAM_B2 * v + (1.0 - ADAM_B2) * (g * g)
    m_hat = m / (1.0 - ADAM_B1 ** ADAM_STEP)
    v_hat = v / (1.0 - ADAM_B2 ** ADAM_STEP)
    delta = -ADAM_LR * (m_hat / (jnp.sqrt(v_hat) + ADAM_EPS) + ADAM_WD * w)
    return delta, m, v


def _adamw(ws, gs, ms, vs, row_blocks, name):
    n = len(ws)

    def body(*refs):
        w, g, m, v = refs[:n], refs[n:2 * n], refs[2 * n:3 * n], refs[3 * n:4 * n]
        d, mo, vo, go = refs[4 * n:5 * n], refs[5 * n:6 * n], refs[6 * n:7 * n], refs[7 * n:]
        for t in range(n):
            gv = g[t][...]
            delta, m_new, v_new = _adamw_math(w[t][...], gv, m[t][...], v[t][...])
            d[t][...] = delta
            mo[t][...] = m_new
            vo[t][...] = v_new
            go[t][...] = gv

    specs = [pl.BlockSpec((a.shape[0] // row_blocks, a.shape[1]), lambda i: (i, 0)) for a in ws]
    shapes = [jax.ShapeDtypeStruct(a.shape, F32) for a in ws]
    return _pcall(
        body, name=name, grid=(row_blocks,), in_specs=specs * 4, out_specs=specs * 4, out_shape=shapes * 4,
        compiler_params=_params(1),
    )(*ws, *gs, *ms, *vs)


def kernel(x, ffn1_norm, ffn1_w_gate, ffn1_w_up, ffn1_w_down, mix_norm, w_in, conv_w, attn_sinks, w_out, ffn2_norm, ffn2_w_gate, ffn2_w_up, ffn2_w_down, final_norm, loss_target, m_ffn1_norm, m_ffn1_w_gate, m_ffn1_w_up, m_ffn1_w_down, m_mix_norm, m_w_in, m_conv_w, m_attn_sinks, m_w_out, m_ffn2_norm, m_ffn2_w_gate, m_ffn2_w_up, m_ffn2_w_down, m_final_norm, v_ffn1_norm, v_ffn1_w_gate, v_ffn1_w_up, v_ffn1_w_down, v_mix_norm, v_w_in, v_conv_w, v_attn_sinks, v_w_out, v_ffn2_norm, v_ffn2_w_gate, v_ffn2_w_up, v_ffn2_w_down, v_final_norm):
    T, D = x.shape[1], x.shape[2]
    chip = (2 * lax.axis_index("x") + lax.axis_index("y")).astype(jnp.int32)
    core = lax.axis_index("c").astype(jnp.int32)
    place = jnp.stack([chip, core])
    x0 = x[0]
    target = loss_target[0]
    gf = final_norm.reshape(1, D)

    tr = lambda w: jnp.swapaxes(w[0], 0, 1)
    big = [tr(ffn1_w_gate), tr(ffn1_w_up), ffn1_w_down[0], tr(w_in), w_out[0], tr(ffn2_w_gate), tr(ffn2_w_up), ffn2_w_down[0]]
    transposed = [True, True, False, True, False, True, True, False]
    own_b = [w.astype(BF16) for w in big]

    def whole(gathered, own):
        return lax.dynamic_update_slice(gathered, own[None], (chip, 0, 0)).reshape(-1, D)

    got1 = _run_comm(_gather_plan(own_b[0:3]), "gather_ffn1")
    wg1, wu1, wd1 = (whole(g, o) for g, o in zip(got1, own_b[0:3]))
    tab = _rope_tables(T)

    res = _ffn_fwd(x0, ffn1_norm, wg1, wu1, wd1, "ffn1_fwd", _gather_plan(own_b[3:8], [conv_w[0]]))
    x1, h1, gate1, up1, act1 = res[:5]
    win, wout, wg2, wu2, wd2 = (whole(g, o) for g, o in zip(res[5:10], own_b[3:8]))
    convw4 = lax.dynamic_update_slice(res[10], conv_w, (chip, 0, 0))
    convw = jnp.transpose(convw4, (1, 0, 2)).reshape(3, -1)
    z, hm = _norm_matmul(x1, mix_norm, win, tab, "mix_in_fwd")
    ymix = _mix_core_fwd(z, convw, attn_sinks, "mix_core_fwd")
    x2 = _matmul_residual(ymix, wout, x1, "mix_out_fwd")
    dx3, h2, gate2, up2, act2, dgf, loss_part = _ffn_fwd(x2, ffn2_norm, wg2, wu2, wd2, "ffn2_fwd", head=(gf, target))

    dx2, dyb2, dgate2, dup2, dg2 = _ffn_bwd(dx3, x2, ffn2_norm, gate2, up2, wg2, wu2, wd2, "ffn2_bwd")
    dymix, dx2b = _matmul_nt(dx2, wout, "mix_out_bwd")
    dz, dcw, dsk = _mix_core_bwd(z, dymix, tab, convw, attn_sinks, "mix_core_bwd")
    dx1, dgm = _matmul_norm_bwd(dz, win, x1, mix_norm, dx2, "mix_in_bwd")
    dx0, dyb1, dgate1, dup1, dg1 = _ffn_bwd(dx1, x0, ffn1_norm, gate1, up1, wg1, wu1, wd1, "ffn1_bwd")

    pad = lambda a: jnp.pad(a, ((0, 0), (0, LANES - a.shape[1])))
    vec = jnp.concatenate([dg1, dgm, dg2, dgf, dcw[0:3].reshape(1, -1), pad(dsk[:, 0].reshape(1, -1)),
                           pad(loss_part[:, 0:1])], axis=1)

    jobs = [("ffn2_dwg", dgate2, h2, 5), ("ffn2_dwu", dup2, h2, 6), ("ffn2_dwd", act2, dyb2, 7),
            ("ffn1_dwg", dgate1, h1, 0), ("ffn1_dwu", dup1, h1, 1), ("ffn1_dwd", act1, dyb1, 2),
            ("mix_dwin", dz, hm, 3), ("mix_dwout", ymix, dx2b, 4)]
    n_jobs = len(jobs)
    grad, grad_b, from_sib, pair_b, from_chips, half, g_big = ({} for _ in range(7))

    def stage_plans(t):
        plans, takers = [], []
        if 0 <= t - 1 < n_jobs:
            plans.append(_sibling_plan([grad_b[t - 1]]))
            takers.append((from_sib, t - 1))
        if 0 <= t - 2 < n_jobs:
            plans.append(_scatter_plan([pair_b[t - 2]]))
            takers.append((from_chips, t - 2))
        if 0 <= t - 3 < n_jobs:
            plans.append(_join_plan([half[t - 3]]))
            takers.append((g_big, jobs[t - 3][3]))
        return plans, takers

    def after_stage(t, landed, takers):
        for (store, key), arr in zip(takers, landed):
            store[key] = arr
        pair = (grad[t - 1], from_sib[t - 1]) if 0 <= t - 1 < n_jobs else None
        chip = (grad[t - 2], from_sib[t - 2], from_chips[t - 2]) if 0 <= t - 2 < n_jobs else None
        if pair or chip:
            sums = list(_local_sums(pair, chip, place, f"local_sums_{t}"))
            if pair:
                pair_b[t - 1] = sums.pop(0)
            if chip:
                half[t - 2] = sums.pop(0)

    for t, (name_, a, b, _) in enumerate(jobs):
        plans, takers = stage_plans(t)
        if t == 0:
            plans.append(_all_gather_plan(jnp.pad(vec, ((0, 7), (0, 0)))))
        res = _matmul_tn(a, b, DW_ROW_SPLIT, name_, _merge_plans(plans))
        grad[t], grad_b[t] = (r.reshape(N_CHIPS, -1, D) for r in res[:2])
        landed = list(res[2:])
        if t == 0:
            vec_blocks = landed.pop()
        after_stage(t, landed, takers)

    ws = big
    ms = [tr(m_ffn1_w_gate), tr(m_ffn1_w_up), m_ffn1_w_down[0], tr(m_w_in), m_w_out[0], tr(m_ffn2_w_gate), tr(m_ffn2_w_up), m_ffn2_w_down[0]]
    vs = [tr(v_ffn1_w_gate), tr(v_ffn1_w_up), v_ffn1_w_down[0], tr(v_w_in), v_w_out[0], tr(v_ffn2_w_gate), tr(v_ffn2_w_up), v_ffn2_w_down[0]]
    for t in range(n_jobs, n_jobs + 3):
        plans, takers = stage_plans(t)
        after_stage(t, _run_comm(_merge_plans(plans), f"grads_tail_{t - n_jobs}"), takers)
    upd = {}
    for name_, idx in (("adamw_a", [0, 1, 2, 4]), ("adamw_b", [3, 5, 6, 7])):
        k = len(idx)
        res = _adamw([ws[i] for i in idx], [g_big[i] for i in idx], [ms[i] for i in idx], [vs[i] for i in idx], ADAMW_ROW_BLOCKS, name_)
        for j, i in enumerate(idx):
            upd[i] = (res[j], res[k + j], res[2 * k + j])
            g_big[i] = res[3 * k + j]

    total = _sum_devices(vec_blocks, "small_sum")[0:1]
    g_n1, g_nm, g_n2, g_nf = (total[:, k * D:(k + 1) * D] for k in range(4))
    cw_full = total[:, 4 * D:4 * D + 3 * CONV_WIDTH].reshape(3, CONV_WIDTH)
    cq = CONV_WIDTH // N_CHIPS
    g_cw = lax.dynamic_slice(cw_full, (0, chip * cq), (3, cq))
    off = 4 * D + 3 * CONV_WIDTH
    g_sk = total[:, off:off + N_Q_HEADS]
    loss = total[0, off + LANES]

    sw = [ffn1_norm, mix_norm, conv_w[0], attn_sinks, ffn2_norm, gf]
    sg = [g_n1, g_nm, g_cw, g_sk, g_n2, g_nf]
    sm = [m_ffn1_norm, m_mix_norm, m_conv_w[0], m_attn_sinks, m_ffn2_norm, m_final_norm.reshape(1, D)]
    sv = [v_ffn1_norm, v_mix_norm, v_conv_w[0], v_attn_sinks, v_ffn2_norm, v_final_norm.reshape(1, D)]
    sres = _adamw(sw, sg, sm, sv, 1, "adamw_small")
    supd = [(sres[j], sres[6 + j], sres[12 + j]) for j in range(6)]

    order = [("s", 0), ("b", 0), ("b", 1), ("b", 2), ("s", 1), ("b", 3), ("s", 2), ("s", 3), ("b", 4),
             ("s", 4), ("b", 5), ("b", 6), ("b", 7), ("s", 5)]

    def leaf(kind, i, which):
        if kind == "b":
            a = g_big[i] if which == 0 else upd[i][which - 1]
            return (jnp.swapaxes(a, 0, 1) if transposed[i] else a)[None]
        a = sg[i] if which == 0 else supd[i][which - 1]
        if i == 2:
            return a[None]
        if i == 5:
            return a.reshape(D)
        return a

    outs = [loss, dx0[None]]
    for which in range(4):
        outs += [leaf(kind, i, which) for kind, i in order]
    return tuple(outs)
```

```python
import jax
import jax.numpy as jnp
import numpy as np
from jax import lax
from jax.experimental import pallas as pl
from jax.experimental.pallas import tpu as pltpu

F32 = jnp.float32
BF16 = jnp.bfloat16
MESH = pl.DeviceIdType.MESH

CONV_WIDTH = 512
N_Q_HEADS = 8
HEAD_DIM = 64
BLOCK = 128
ROPE_THETA = 500000.0
ROT_DIM = 16
RMS_EPS = 1e-5
MASK_VALUE = -1e30
ATTN_SCALE = HEAD_DIM ** -0.5
FFN_RES_SCALE = 0.5
ADAM_LR = 0.001
ADAM_B1 = 0.9
ADAM_B2 = 0.999
ADAM_EPS = 1e-08
ADAM_WD = 0.01
ADAM_STEP = 10

N_CHIPS = 4
N_DEV = 8
LANES = 128
VMEM_LIMIT = 56 * 1024 * 1024

_pcall = pl.pallas_call
HBM_SPEC = pl.BlockSpec(memory_space=pltpu.HBM)
ANY_SPEC = pl.BlockSpec(memory_space=pl.ANY)


def _params(n_axes, vmem=VMEM_LIMIT):
    return pltpu.CompilerParams(dimension_semantics=("arbitrary",) * n_axes, vmem_limit_bytes=vmem)


def _dot(a, b):
    return jnp.dot(a, b, preferred_element_type=F32)


def _dot_nt(a, b):
    return lax.dot_general(a, b, (((1,), (1,)), ((), ())), preferred_element_type=F32)


def _dot_tn(a, b):
    return lax.dot_general(a, b, (((0,), (0,)), ((), ())), preferred_element_type=F32)


def _rms_inv(x):
    return lax.rsqrt(jnp.mean(x * x, axis=-1, keepdims=True) + RMS_EPS)


def _norm_bwd(dh, x, g):
    inv = _rms_inv(x)
    xhat = x * inv
    dg = jnp.sum(dh * xhat, axis=0, keepdims=True)
    dxhat = dh * g
    dx = inv * (dxhat - xhat * jnp.mean(dxhat * xhat, axis=-1, keepdims=True))
    return dx, dg


def _place():
    x, y, c = lax.axis_index("x"), lax.axis_index("y"), lax.axis_index("c")
    chips = [(1 - x, y), (x, 1 - y), (1 - x, 1 - y)]
    return x, y, c, chips


class _Plan:
    def __init__(self, arrays, out_shapes, n_sems, start, finish, middle=None, aliases=None):
        self.arrays, self.out_shapes, self.n_sems = list(arrays), list(out_shapes), n_sems
        self.start, self.finish, self.middle = start, finish, middle
        self.aliases = dict(aliases or {})

    def specs(self):
        k = len(self.arrays)
        sems = [pltpu.SemaphoreType.DMA((self.n_sems,)), pltpu.SemaphoreType.DMA((self.n_sems,))]
        return [HBM_SPEC] * k, [HBM_SPEC] * len(self.out_shapes), self.out_shapes, sems


class _SemSlice:
    def __init__(self, ref, offset):
        self.ref, self.offset = ref, offset

    @property
    def at(self):
        return self

    def __getitem__(self, k):
        return self.ref.at[k + self.offset]


def _merge_plans(plans):
    plans = [p for p in plans if p is not None]
    if len(plans) <= 1:
        return plans[0] if plans else None
    arrays, shapes, aliases, spans, n_sems = [], [], {}, [], 0
    for p in plans:
        a0, o0 = len(arrays), len(shapes)
        spans.append((a0, a0 + len(p.arrays), o0, o0 + len(p.out_shapes), n_sems))
        aliases.update({a0 + i: o0 + j for i, j in p.aliases.items()})
        arrays += p.arrays
        shapes += p.out_shapes
        n_sems += p.n_sems

    def run(which):
        def fn(ins, outs, send_sems, recv_sems):
            for p, (a0, a1, o0, o1, s0) in zip(plans, spans):
                part = getattr(p, which)
                if part is not None:
                    part(ins[a0:a1], outs[o0:o1], _SemSlice(send_sems, s0), _SemSlice(recv_sems, s0))
        return fn

    middle = run("middle") if any(p.middle is not None for p in plans) else None
    return _Plan(arrays, shapes, n_sems, run("start"), run("finish"), middle, aliases)


def _sibling_plan(grads_b):
    n = len(grads_b)

    def copies(ins, outs, send_sems, recv_sems):
        x, y, c, _ = _place()

        def copy(t):
            half = ins[t].shape[1] // 2
            return pltpu.make_async_remote_copy(
                src_ref=ins[t].at[:, pl.ds(pl.multiple_of((1 - c) * half, 16), half), :], dst_ref=outs[t],
                send_sem=send_sems.at[t], recv_sem=recv_sems.at[t], device_id=(x, y, 1 - c), device_id_type=MESH)

        return [copy(t) for t in range(n)]

    def start(*refs):
        for cp in copies(*refs):
            cp.start()

    def finish(*refs):
        for cp in copies(*refs):
            cp.wait()

    shapes = [jax.ShapeDtypeStruct((g.shape[0], g.shape[1] // 2, g.shape[2]), g.dtype) for g in grads_b]
    return _Plan(grads_b, shapes, n, start, finish)


def _scatter_plan(parts_b):
    n = len(parts_b)

    def copies(ins, outs, send_sems, recv_sems):
        x, y, c, chips = _place()

        def copy(t, j):
            px, py = chips[j]
            return pltpu.make_async_remote_copy(
                src_ref=ins[t].at[2 * px + py], dst_ref=outs[t].at[j], send_sem=send_sems.at[3 * t + j],
                recv_sem=recv_sems.at[3 * t + j], device_id=(px, py, c), device_id_type=MESH)

        return [copy(t, j) for t in range(n) for j in range(3)]

    def start(*refs):
        for cp in copies(*refs):
            cp.start()

    def finish(*refs):
        for cp in copies(*refs):
            cp.wait()

    shapes = [jax.ShapeDtypeStruct((3, *p.shape[1:]), p.dtype) for p in parts_b]
    return _Plan(parts_b, shapes, 3 * n, start, finish)


def _gather_plan(shards, small=()):
    n, ns = len(shards), len(small)
    per = 8

    def parts(ins, outs, send_sems, recv_sems):
        x, y, c, chips = _place()
        me = 2 * x + y
        blocks = [2 * px + py for px, py in chips]

        def rows(t, core, piece=None):
            half = ins[t].shape[0] // 2
            if piece is None:
                return pl.ds(pl.multiple_of(core * half, 16), half)
            return pl.ds(pl.multiple_of(core * half + piece * (half // 2), 16), half // 2)

        def remote(src, dst, k, device):
            return pltpu.make_async_remote_copy(src_ref=src, dst_ref=dst, send_sem=send_sems.at[k],
                                                recv_sem=recv_sems.at[k], device_id=device, device_id_type=MESH)

        def first(t, j, block, core):
            return remote(ins[t].at[rows(t, core), :], outs[t].at[block, rows(t, core), :], per * t + j, (*chips[j], c))

        def relay(t, j, block, core):
            ref = outs[t].at[block, rows(t, core, j), :]
            return remote(ref, ref, per * t + 2 + j, (*chips[j], c))

        def passed(t, k, block, core, piece=None):
            ref = outs[t].at[block, rows(t, core, piece), :]
            return remote(ref, ref, per * t + 4 + k, (x, y, 1 - c))

        def whole(s, j, block):
            return remote(ins[n + s], outs[n + s].at[block], per * n + 3 * s + j, (*chips[j], c))

        return c, me, blocks, first, relay, passed, whole

    def start(*refs):
        c, me, _, first, _, _, whole = parts(*refs)
        for t in range(n):
            for j in range(2):
                first(t, j, me, c).start()
        for s in range(ns):
            for j in range(3):
                whole(s, j, me).start()

    def middle(*refs):
        c, _, blocks, first, relay, passed, _ = parts(*refs)
        for t in range(n):
            for j in range(2):
                first(t, j, blocks[j], c).wait_recv()
                passed(t, j, blocks[j], c).start()
                relay(t, 1 - j, blocks[j], c).start()

    def finish(*refs):
        c, me, blocks, first, relay, passed, whole = parts(*refs)
        for t in range(n):
            for j in range(2):
                relay(t, j, blocks[2], c).wait_recv()
                passed(t, 2 + j, blocks[2], c, j).start()
        for t in range(n):
            for j in range(2):
                passed(t, j, blocks[j], 1 - c).wait_recv()
                passed(t, 2 + j, blocks[2], 1 - c, j).wait_recv()
        for s in range(ns):
            for j in range(3):
                whole(s, j, blocks[j]).wait_recv()
        for t in range(n):
            for j in range(2):
                first(t, j, me, c).wait_send()
                relay(t, 1 - j, blocks[j], c).wait_send()
                passed(t, j, blocks[j], c).wait_send()
                passed(t, 2 + j, blocks[2], c, j).wait_send()
        for s in range(ns):
            for j in range(3):
                whole(s, j, me).wait_send()

    arrays = [*shards, *small]
    shapes = [jax.ShapeDtypeStruct((N_CHIPS, *a.shape), a.dtype) for a in arrays]
    return _Plan(arrays, shapes, per * n + 3 * ns, start, finish, middle)


def _run_comm(plan, name):
    k = len(plan.arrays)
    in_specs, out_specs, out_shape, sems = plan.specs()

    def body(*refs):
        cr = (refs[:k], refs[k:k + len(out_shape)], refs[-2], refs[-1])
        plan.start(*cr)
        if plan.middle is not None:
            plan.middle(*cr)
        plan.finish(*cr)

    return _pcall(body, name=name, in_specs=in_specs, out_specs=out_specs, out_shape=out_shape,
                  input_output_aliases=plan.aliases, scratch_shapes=sems)(*plan.arrays)


def _carried(plan, in_specs, out_specs, out_shape, scratch):
    aliases = {}
    if plan is not None:
        p_in, p_out, p_shape, p_sems = plan.specs()
        aliases = {len(in_specs) + i: len(out_specs) + j for i, j in plan.aliases.items()}
        in_specs, out_specs = in_specs + p_in, out_specs + p_out
        out_shape, scratch = out_shape + p_shape, scratch + p_sems
    return dict(in_specs=in_specs, out_specs=out_specs, out_shape=out_shape, scratch_shapes=scratch,
                input_output_aliases=aliases)


def _unpack(refs, n_in, n_out, plan):
    k_in = len(plan.arrays) if plan else 0
    k_out = len(plan.out_shapes) if plan else 0
    ins = refs[:n_in]
    outs = refs[n_in + k_in:n_in + k_in + n_out]
    rest = refs[n_in + k_in + n_out + k_out:]
    if plan is None:
        return ins, outs, rest, None
    cr = (refs[n_in:n_in + k_in], refs[n_in + k_in + n_out:n_in + k_in + n_out + k_out], rest[-2], rest[-1])
    return ins, outs, rest[:-2], cr


def _hook(plan, cr, which, cond):
    fn = getattr(plan, which) if plan is not None else None
    if fn is not None:
        pl.when(cond)(lambda: fn(*cr))


def _join_plan(shards):
    n = len(shards)

    def copy(ins, outs, send_sems, recv_sems, t, core):
        x, y, c, _ = _place()
        half = ins[t].shape[0] // 2
        rows = pl.ds(pl.multiple_of(core * half, 8), half)
        return pltpu.make_async_remote_copy(
            src_ref=ins[t].at[rows, :], dst_ref=outs[t].at[rows, :], send_sem=send_sems.at[t],
            recv_sem=recv_sems.at[t], device_id=(x, y, 1 - c), device_id_type=MESH)

    def start(*refs):
        c = lax.axis_index("c")
        for t in range(n):
            copy(*refs, t, c).start()

    def finish(*refs):
        c = lax.axis_index("c")
        for t in range(n):
            copy(*refs, t, 1 - c).wait_recv()
        for t in range(n):
            copy(*refs, t, c).wait_send()

    shapes = [jax.ShapeDtypeStruct(s.shape, s.dtype) for s in shards]
    return _Plan(shards, shapes, n, start, finish, aliases={t: t for t in range(n)})


def _all_gather_plan(vec):
    def parts(ins, outs, send_sems, recv_sems):
        x, y, c, _ = _place()
        me = 4 * x + 2 * y + c
        rel = [((k >> 2) & 1, (k >> 1) & 1, k & 1) for k in range(1, N_DEV)]

        def peer(k):
            fx, fy, fc = rel[k]
            return (x ^ fx, y ^ fy, c ^ fc)

        def copy(k, dev):
            return pltpu.make_async_remote_copy(
                src_ref=ins[0], dst_ref=outs[0].at[dev], send_sem=send_sems.at[k], recv_sem=recv_sems.at[k],
                device_id=peer(k), device_id_type=MESH)

        mine = pltpu.make_async_copy(ins[0], outs[0].at[me], send_sems.at[N_DEV - 1])
        return me, peer, copy, mine

    def start(*refs):
        me, _, copy, mine = parts(*refs)
        mine.start()
        for k in range(N_DEV - 1):
            copy(k, me).start()

    def finish(*refs):
        me, peer, copy, mine = parts(*refs)
        for k in range(N_DEV - 1):
            px, py, pc = peer(k)
            copy(k, 4 * px + 2 * py + pc).wait_recv()
        for k in range(N_DEV - 1):
            copy(k, me).wait_send()
        mine.wait()

    return _Plan([vec], [jax.ShapeDtypeStruct((N_DEV, *vec.shape), vec.dtype)], N_DEV, start, finish)


def _sum_devices(blocks, name):
    def body(b_ref, o_ref):
        total = b_ref[0]
        for dev in range(1, N_DEV):
            total = total + b_ref[dev]
        o_ref[...] = total

    return _pcall(body, name=name, in_specs=[pl.BlockSpec(memory_space=pltpu.VMEM)],
                  out_specs=pl.BlockSpec(memory_space=pltpu.VMEM),
                  out_shape=jax.ShapeDtypeStruct(blocks.shape[1:], F32))(blocks)


TOKEN_TILE = 512
PROJ_TOKEN_TILE = 1024
OUT_PROJ_TOKEN_TILE = 2048
ADAMW_ROW_BLOCKS = 4
BWD_VMEM_LIMIT = 62 * 1024 * 1024
DW_TOKEN_TILE = 2048
DW_ROW_SPLIT = 2
MXU_COLS = 256
DH_GROUP = 6


def _chunks(n):
    out, c0 = [], 0
    while c0 < n:
        size = min(MXU_COLS, n - c0)
        out.append((c0, size))
        c0 += size
    return out


def _load_weights(hbm_refs, vmem_refs, sems):
    copies = [pltpu.make_async_copy(h, v, sems.at[k]) for k, (h, v) in enumerate(zip(hbm_refs, vmem_refs))]
    for cp in copies:
        cp.start()
    for cp in copies:
        cp.wait()


def _ffn_fwd(x, g, wgt, wut, wd, name, plan=None, head=None):
    T, D = x.shape
    F = wgt.shape[0]
    tm = min(T, TOKEN_TILE)
    ni = T // tm
    n_head = 2 if head is not None else 0

    def body(*refs):
        ins, outs, scratch, cr = _unpack(refs, 5 + n_head, 5 + n_head, plan)
        x_ref, g_ref, wg_hbm, wu_hbm, wd_hbm = ins[:5]
        xo_ref, h_ref, gate_ref, up_ref, act_ref = outs[:5]
        wg_ref, wu_ref, wd_ref, sems = scratch
        i = pl.program_id(0)
        _hook(plan, cr, "start", i == 0)

        @pl.when(i == 0)
        def _():
            _load_weights((wg_hbm, wu_hbm, wd_hbm), (wg_ref, wu_ref, wd_ref), sems)

        xv = x_ref[...]
        h = ((xv * _rms_inv(xv)) * g_ref[...]).astype(BF16)
        h_ref[...] = h
        for c0, size in _chunks(F):
            gate = _dot_nt(h, wg_ref[c0:c0 + size, :])
            up = _dot_nt(h, wu_ref[c0:c0 + size, :])
            gate_ref[:, c0:c0 + size] = gate.astype(BF16)
            up_ref[:, c0:c0 + size] = up.astype(BF16)
            act_ref[:, c0:c0 + size] = (gate * jax.nn.sigmoid(gate) * up).astype(BF16)
        y = x_ref[...] + FFN_RES_SCALE * _dot(act_ref[...], wd_ref[...])
        if head is None:
            xo_ref[...] = y
        else:
            gf_ref, t_ref = ins[5:]
            dgf_ref, loss_ref = outs[5:]

            @pl.when(i == 0)
            def _():
                dgf_ref[...] = jnp.zeros_like(dgf_ref)
                loss_ref[...] = jnp.zeros_like(loss_ref)

            gf = gf_ref[...]
            diff = (y * _rms_inv(y)) * gf - t_ref[...]
            loss_ref[...] += 0.5 * jnp.sum(jnp.mean(diff * diff, axis=-1, keepdims=True))
            dy, dgf = _norm_bwd(diff * (1.0 / D), y, gf)
            xo_ref[...] = dy
            dgf_ref[...] += dgf
        _hook(plan, cr, "middle", i == ni // 2)
        _hook(plan, cr, "finish", i == ni - 1)

    const = lambda shape: pl.BlockSpec(shape, lambda i: (0, 0))
    rows = lambda width: pl.BlockSpec((tm, width), lambda i: (i, 0))
    in_specs = [rows(D), const((1, D)), ANY_SPEC, ANY_SPEC, ANY_SPEC]
    out_specs = [rows(D), rows(D), rows(F), rows(F), rows(F)]
    out_shape = [jax.ShapeDtypeStruct((T, D), F32), jax.ShapeDtypeStruct((T, D), BF16),
                 jax.ShapeDtypeStruct((T, F), BF16), jax.ShapeDtypeStruct((T, F), BF16), jax.ShapeDtypeStruct((T, F), BF16)]
    if head is not None:
        in_specs += [const((1, D)), rows(D)]
        out_specs += [const((1, D)), const((1, LANES))]
        out_shape += [jax.ShapeDtypeStruct((1, D), F32), jax.ShapeDtypeStruct((1, LANES), F32)]
    io = _carried(plan, in_specs, out_specs, out_shape,
                  [pltpu.VMEM((F, D), BF16), pltpu.VMEM((F, D), BF16), pltpu.VMEM((F, D), BF16),
                   pltpu.SemaphoreType.DMA((3,))])
    return _pcall(
        body, name=name, grid=(ni,), compiler_params=_params(1), **io,
    )(x, g, wgt, wut, wd, *(head or ()), *(plan.arrays if plan else ()))


def _ffn_bwd(dy, x, g, gate, up, wgt, wut, wd, name):
    T, D = x.shape
    F = wgt.shape[0]
    tm = min(T, TOKEN_TILE)
    ni = T // tm

    def body(dy_ref, x_ref, g_ref, gate_ref, up_ref, wg_hbm, wu_hbm, wd_hbm,
             dx_ref, dyb_ref, dgate_ref, dup_ref, dg_ref, wg_ref, wu_ref, wd_ref, sems):
        @pl.when(pl.program_id(0) == 0)
        def _():
            _load_weights((wg_hbm, wu_hbm, wd_hbm), (wg_ref, wu_ref, wd_ref), sems)
            dg_ref[...] = jnp.zeros_like(dg_ref)

        dyb = (FFN_RES_SCALE * dy_ref[...]).astype(BF16)
        dyb_ref[...] = dyb
        dh, group_g, group_u, row0 = None, [], [], 0
        chunks = _chunks(F)
        for k, (c0, size) in enumerate(chunks):
            dact = _dot_nt(dyb, wd_ref[c0:c0 + size, :])
            gt = gate_ref[:, c0:c0 + size].astype(F32)
            u = up_ref[:, c0:c0 + size].astype(F32)
            sig = jax.nn.sigmoid(gt)
            dup = (dact * (gt * sig)).astype(BF16)
            dgate = (dact * u * (sig * (1.0 + gt * (1.0 - sig)))).astype(BF16)
            dup_ref[:, c0:c0 + size] = dup
            dgate_ref[:, c0:c0 + size] = dgate
            group_g.append(dgate)
            group_u.append(dup)
            if len(group_g) == DH_GROUP or k == len(chunks) - 1:
                rows = slice(row0, c0 + size)
                part = (_dot(jnp.concatenate(group_g, axis=1), wg_ref[rows, :])
                        + _dot(jnp.concatenate(group_u, axis=1), wu_ref[rows, :]))
                dh = part if dh is None else dh + part
                group_g, group_u, row0 = [], [], c0 + size
        dxn, dg = _norm_bwd(dh, x_ref[...], g_ref[...])
        dx_ref[...] = dy_ref[...] + dxn
        dg_ref[...] += dg

    return _pcall(
        body, name=name, grid=(ni,),
        in_specs=[pl.BlockSpec((tm, D), lambda i: (i, 0)), pl.BlockSpec((tm, D), lambda i: (i, 0)),
                  pl.BlockSpec((1, D), lambda i: (0, 0)),
                  pl.BlockSpec((tm, F), lambda i: (i, 0)), pl.BlockSpec((tm, F), lambda i: (i, 0)),
                  ANY_SPEC, ANY_SPEC, ANY_SPEC],
        out_specs=[pl.BlockSpec((tm, D), lambda i: (i, 0)), pl.BlockSpec((tm, D), lambda i: (i, 0)),
                   pl.BlockSpec((tm, F), lambda i: (i, 0)), pl.BlockSpec((tm, F), lambda i: (i, 0)),
                   pl.BlockSpec((1, D), lambda i: (0, 0))],
        out_shape=[jax.ShapeDtypeStruct((T, D), F32), jax.ShapeDtypeStruct((T, D), BF16),
                   jax.ShapeDtypeStruct((T, F), BF16), jax.ShapeDtypeStruct((T, F), BF16),
                   jax.ShapeDtypeStruct((1, D), F32)],
        scratch_shapes=[pltpu.VMEM((F, D), BF16), pltpu.VMEM((F, D), BF16), pltpu.VMEM((F, D), BF16),
                        pltpu.SemaphoreType.DMA((3,))],
        compiler_params=_params(1, BWD_VMEM_LIMIT),
    )(dy, x, g, gate, up, wgt, wut, wd)


def _matmul_tn(a, b, row_split, name, plan=None):
    T, n1 = a.shape
    n2 = b.shape[1]
    tn = n1 // row_split
    tk = min(T, DW_TOKEN_TILE)
    nk = T // tk

    def body(*refs):
        (a_ref, b_ref), (o_ref, ob_ref), _, cr = _unpack(refs, 2, 2, plan)
        j = pl.program_id(0)
        k = pl.program_id(1)
        _hook(plan, cr, "start", jnp.logical_and(j == 0, k == 0))

        @pl.when(k == 0)
        def _():
            o_ref[...] = jnp.zeros_like(o_ref)

        o_ref[...] += _dot_tn(a_ref[...], b_ref[...])

        @pl.when(k == nk - 1)
        def _():
            ob_ref[...] = o_ref[...].astype(BF16)

        _hook(plan, cr, "finish", jnp.logical_and(j == row_split - 1, k == nk - 1))

    io = _carried(
        plan,
        [pl.BlockSpec((tk, tn), lambda j, k: (k, j)), pl.BlockSpec((tk, n2), lambda j, k: (k, 0))],
        [pl.BlockSpec((tn, n2), lambda j, k: (j, 0)), pl.BlockSpec((tn, n2), lambda j, k: (j, 0))],
        [jax.ShapeDtypeStruct((n1, n2), F32), jax.ShapeDtypeStruct((n1, n2), BF16)], [])
    return _pcall(
        body, name=name, grid=(row_split, nk), compiler_params=_params(2), **io,
    )(a, b, *(plan.arrays if plan else ()))


def _norm_matmul(x, g, wt, tab, name):
    T, D = x.shape
    n = wt.shape[0]
    tm = min(T, PROJ_TOKEN_TILE)

    def body(x_ref, g_ref, w_ref, tab_ref, z_ref, h_ref):
        xv = x_ref[...]
        h = ((xv * _rms_inv(xv)) * g_ref[...]).astype(BF16)
        h_ref[...] = h
        z = _dot_nt(h, w_ref[...])
        z_ref[:, 0:Z_Q] = z[:, 0:Z_Q]
        tab_v = tab_ref[...]
        for c0 in range(Z_Q, Z_V, LANES):
            z_ref[:, c0:c0 + LANES] = _rot(z[:, c0:c0 + LANES], tab_v)
        z_ref[:, Z_V:Z_END] = z[:, Z_V:Z_END]

    return _pcall(
        body, name=name, grid=(T // tm,),
        in_specs=[pl.BlockSpec((tm, D), lambda i: (i, 0)), pl.BlockSpec((1, D), lambda i: (0, 0)),
                  pl.BlockSpec((n, D), lambda i: (0, 0)), pl.BlockSpec((tm, 3 * LANES), lambda i: (i, 0))],
        out_specs=[pl.BlockSpec((tm, n), lambda i: (i, 0)), pl.BlockSpec((tm, D), lambda i: (i, 0))],
        out_shape=[jax.ShapeDtypeStruct((T, n), F32), jax.ShapeDtypeStruct((T, D), BF16)],
        compiler_params=_params(1),
    )(x, g, wt, tab)


def _matmul_residual(y, w, x, name):
    T, D = x.shape
    kdim = y.shape[1]
    tm = min(T, OUT_PROJ_TOKEN_TILE)

    def body(y_ref, w_ref, x_ref, o_ref):
        o_ref[...] = x_ref[...] + _dot(y_ref[...], w_ref[...])

    return _pcall(
        body, name=name, grid=(T // tm,),
        in_specs=[pl.BlockSpec((tm, kdim), lambda i: (i, 0)), pl.BlockSpec((kdim, D), lambda i: (0, 0)),
                  pl.BlockSpec((tm, D), lambda i: (i, 0))],
        out_specs=pl.BlockSpec((tm, D), lambda i: (i, 0)),
        out_shape=jax.ShapeDtypeStruct((T, D), F32),
        compiler_params=_params(1),
    )(y, w, x)


def _matmul_nt(dx, w, name):
    T, D = dx.shape
    kdim = w.shape[0]
    tm = min(T, OUT_PROJ_TOKEN_TILE)

    def body(dx_ref, w_ref, dy_ref, dxb_ref):
        dxb = dx_ref[...].astype(BF16)
        dxb_ref[...] = dxb
        dy_ref[...] = _dot_nt(dxb, w_ref[...])

    return _pcall(
        body, name=name, grid=(T // tm,),
        in_specs=[pl.BlockSpec((tm, D), lambda i: (i, 0)), pl.BlockSpec((kdim, D), lambda i: (0, 0))],
        out_specs=[pl.BlockSpec((tm, kdim), lambda i: (i, 0)), pl.BlockSpec((tm, D), lambda i: (i, 0))],
        out_shape=[jax.ShapeDtypeStruct((T, kdim), F32), jax.ShapeDtypeStruct((T, D), BF16)],
        compiler_params=_params(1),
    )(dx, w)


def _matmul_norm_bwd(dz, wt, x, g, dres, name):
    T, D = x.shape
    n = dz.shape[1]
    tm = min(T, PROJ_TOKEN_TILE)

    def body(dz_ref, w_ref, x_ref, g_ref, dres_ref, dx_ref, dg_ref):
        @pl.when(pl.program_id(0) == 0)
        def _():
            dg_ref[...] = jnp.zeros_like(dg_ref)

        dh = _dot(dz_ref[...], w_ref[...])
        dxn, dg = _norm_bwd(dh, x_ref[...], g_ref[...])
        dx_ref[...] = dres_ref[...] + dxn
        dg_ref[...] += dg

    return _pcall(
        body, name=name, grid=(T // tm,),
        in_specs=[pl.BlockSpec((tm, n), lambda i: (i, 0)), pl.BlockSpec((n, D), lambda i: (0, 0)),
                  pl.BlockSpec((tm, D), lambda i: (i, 0)), pl.BlockSpec((1, D), lambda i: (0, 0)),
                  pl.BlockSpec((tm, D), lambda i: (i, 0))],
        out_specs=[pl.BlockSpec((tm, D), lambda i: (i, 0)), pl.BlockSpec((1, D), lambda i: (0, 0))],
        out_shape=[jax.ShapeDtypeStruct((T, D), F32), jax.ShapeDtypeStruct((1, D), F32)],
        compiler_params=_params(1),
    )(dz, wt, x, g, dres)


Z_Q = 3 * CONV_WIDTH
Z_K = Z_Q + N_Q_HEADS * HEAD_DIM
Z_V = Z_K + LANES
Z_END = Z_V + LANES


def _rope_tables(T):
    half = ROT_DIM // 2
    inv_freq = ROPE_THETA ** (-jnp.arange(0, ROT_DIM, 2, dtype=F32) / ROT_DIM)
    ang = inv_freq[:, None] * jnp.arange(T, dtype=F32)[None, :]
    cos_sin = jnp.concatenate([jnp.cos(ang), jnp.sin(ang)], axis=0)
    select = np.zeros((2 * half, 3 * LANES), np.float32)
    const = np.zeros((1, 3 * LANES), np.float32)
    for lane in range(LANES):
        d = lane % HEAD_DIM
        if d < half:
            select[d, lane] = 1.0
            select[half + d, LANES + lane] = -1.0
        elif d < ROT_DIM:
            select[d - half, lane] = 1.0
            select[d, 2 * LANES + lane] = 1.0
        else:
            const[0, lane] = 1.0
    tab = lax.dot_general(cos_sin, jnp.asarray(select), (((0,), (0,)), ((), ())),
                          precision=lax.Precision.HIGHEST, preferred_element_type=F32)
    return tab + jnp.asarray(const)


def _tab3(tab):
    return tab[:, 0:LANES], tab[:, LANES:2 * LANES], tab[:, 2 * LANES:3 * LANES]


def _rot(x, tab):
    c, s1, s2 = _tab3(tab)
    return x * c + pltpu.roll(x, LANES - ROT_DIM // 2, 1) * s1 + pltpu.roll(x, ROT_DIM // 2, 1) * s2


def _rot_t(d, tab):
    c, s1, s2 = _tab3(tab)
    return d * c + pltpu.roll(d * s1, ROT_DIM // 2, 1) + pltpu.roll(d * s2, LANES - ROT_DIM // 2, 1)


def _head_pads(a):
    lo = lax.broadcasted_iota(jnp.int32, a.shape, 1) < HEAD_DIM
    nat0 = jnp.where(lo, a, 0.0)
    nat1 = jnp.where(lo, 0.0, a)
    return {
        (0, 0): nat0.astype(BF16), (0, 1): pltpu.roll(nat0, HEAD_DIM, 1).astype(BF16),
        (1, 0): pltpu.roll(nat1, HEAD_DIM, 1).astype(BF16), (1, 1): nat1.astype(BF16),
    }


def _from_pads(even, odd, kv):
    lo = lax.broadcasted_iota(jnp.int32, even.shape, 1) < HEAD_DIM
    if kv == 0:
        return jnp.where(lo, even + pltpu.roll(odd, HEAD_DIM, 1), 0.0)
    return jnp.where(lo, 0.0, pltpu.roll(even, HEAD_DIM, 1) + odd)


N_GROUPS = 4


def _group_head(g, r):
    kv, par = divmod(g, 2)
    return 2 * (2 * kv + r) + par


def _window_mask_t(has_prev):
    jj = lax.broadcasted_iota(jnp.int32, (2 * BLOCK, 2 * BLOCK), 0)
    ii = lax.broadcasted_iota(jnp.int32, (2 * BLOCK, 2 * BLOCK), 1) & (BLOCK - 1)
    rel = jj - BLOCK - ii
    return (rel <= 0) & (rel > -BLOCK) & ((jj >= BLOCK) | has_prev)


def _sink_row(sink_ref, g):
    lane = lax.broadcasted_iota(jnp.int32, (1, 2 * BLOCK), 1)
    return jnp.where(lane < BLOCK, sink_ref[0, _group_head(g, 0)], sink_ref[0, _group_head(g, 1)])


def _attn_probs_t(q2, kp, mask, sink_ref):
    out = []
    for kv in range(2):
        q_st = jnp.concatenate([q2[2 * kv], q2[2 * kv + 1]], axis=0)
        for par in range(2):
            s = jnp.where(mask, _dot_nt(kp[(kv, par)], q_st), MASK_VALUE)
            sink = _sink_row(sink_ref, 2 * kv + par)
            m = jnp.maximum(jnp.max(s, axis=0, keepdims=True), sink)
            p = jnp.exp(s - m)
            esink = jnp.exp(sink - m)
            rden = 1.0 / (jnp.sum(p, axis=0, keepdims=True) + esink)
            out.append((p * rden, esink * rden))
    return out


def _conv_taps(cg, u, cg_prev, u_prev, has_prev):
    vv = cg * u
    halo = jnp.where(has_prev, cg_prev * u_prev, 0.0)
    ext = jnp.concatenate([halo, vv], axis=0)
    rows = ext.shape[0]
    vv1 = pltpu.roll(ext, 1, 0)[8:rows]
    vv2 = pltpu.roll(ext, 2, 0)[8:rows]
    return vv, vv1, vv2


def _mix_specs(nb):
    cur = lambda n: jnp.minimum(n, nb - 1)
    prev = lambda n: jnp.maximum(jnp.minimum(n, nb - 1) - 1, 0)
    rows8_prev = lambda n: jnp.maximum(16 * jnp.minimum(n, nb - 1) - 1, 0)
    return cur, prev, [
        pl.BlockSpec((BLOCK, Z_END), lambda n: (cur(n), 0)),
        pl.BlockSpec((BLOCK, 2 * LANES), lambda n: (prev(n), Z_K // (2 * LANES))),
        pl.BlockSpec((8, CONV_WIDTH), lambda n: (rows8_prev(n), 1)),
        pl.BlockSpec((8, CONV_WIDTH), lambda n: (rows8_prev(n), 2)),
        pl.BlockSpec((BLOCK, 3 * LANES), lambda n: (cur(n), 0)),
        pl.BlockSpec((BLOCK, 3 * LANES), lambda n: (prev(n), 0)),
        pl.BlockSpec((3, CONV_WIDTH), lambda n: (0, 0)),
        pl.BlockSpec(memory_space=pltpu.SMEM),
    ]


def _mix_core_fwd(z, conv_w, sinks, name):
    T = z.shape[0]
    nb = T // BLOCK
    _, _, specs = _mix_specs(nb)
    specs = specs[:4] + specs[6:]

    def body(z_ref, zkvp_ref, cgp_ref, up_ref, cw_ref, sink_ref, y_ref):
        has_prev = pl.program_id(0) > 0
        bg = z_ref[:, 0:CONV_WIDTH]
        vv, vv1, vv2 = _conv_taps(z_ref[:, CONV_WIDTH:2 * CONV_WIDTH], z_ref[:, 2 * CONV_WIDTH:Z_Q],
                                  cgp_ref[...], up_ref[...], has_prev)
        conv = cw_ref[0:1, :] * vv2 + cw_ref[1:2, :] * vv1 + cw_ref[2:3, :] * vv
        y_ref[:, 0:CONV_WIDTH] = (bg * conv).astype(BF16)

        k_all = jnp.concatenate([zkvp_ref[:, 0:LANES], z_ref[:, Z_K:Z_V]], axis=0)
        v_all = jnp.concatenate([zkvp_ref[:, LANES:2 * LANES], z_ref[:, Z_V:Z_END]], axis=0)
        kp = _head_pads(k_all)
        vp = _head_pads(v_all)
        q2 = [(z_ref[:, Z_Q + LANES * c:Z_Q + LANES * (c + 1)] * ATTN_SCALE).astype(BF16) for c in range(N_Q_HEADS // 2)]
        probs = _attn_probs_t(q2, kp, _window_mask_t(has_prev), sink_ref)
        for kv in range(2):
            o_t = (_dot_tn(vp[(kv, 0)], probs[2 * kv][0].astype(BF16))
                   + _dot_tn(vp[(kv, 1)], probs[2 * kv + 1][0].astype(BF16)))
            for r in range(2):
                c = 2 * kv + r
                y_ref[:, CONV_WIDTH + LANES * c:CONV_WIDTH + LANES * (c + 1)] = o_t[:, BLOCK * r:BLOCK * (r + 1)].T.astype(BF16)

    return _pcall(
        body, name=name, grid=(nb,), in_specs=specs,
        out_specs=pl.BlockSpec((BLOCK, 2 * CONV_WIDTH), lambda n: (n, 0)),
        out_shape=jax.ShapeDtypeStruct((T, 2 * CONV_WIDTH), BF16),
        compiler_params=_params(1),
    )(z, z, z, z, conv_w, sinks)


def _mix_core_bwd(z, dy, tab, conv_w, sinks, name):
    T = z.shape[0]
    nb = T // BLOCK
    cur, _, specs = _mix_specs(nb)
    rows8_next = lambda n: jnp.minimum(16 * (cur(n) + 1), 16 * nb - 1)
    specs = specs[:4] + [
        pl.BlockSpec((8, CONV_WIDTH), lambda n: (rows8_next(n), 0)),
        pl.BlockSpec((BLOCK, 2 * CONV_WIDTH), lambda n: (cur(n), 0)),
        pl.BlockSpec((8, CONV_WIDTH), lambda n: (rows8_next(n), 0)),
    ] + specs[4:]

    def body(z_ref, zkvp_ref, cgp_ref, up_ref, bgn_ref, dy_ref, dyn_ref, tab_ref, tabp_ref, cw_ref, sink_ref,
             dz_ref, dcw_ref, dsk_ref, main_ref, kv_ref):
        n = pl.program_id(0)

        @pl.when(n == 0)
        def _():
            main_ref[...] = jnp.zeros_like(main_ref)
            kv_ref[...] = jnp.zeros_like(kv_ref)
            dcw_ref[...] = jnp.zeros_like(dcw_ref)
            dsk_ref[...] = jnp.zeros_like(dsk_ref)

        @pl.when(n < nb)
        def _():
            has_prev = n > 0
            has_next = n < nb - 1
            bg = z_ref[:, 0:CONV_WIDTH]
            cg = z_ref[:, CONV_WIDTH:2 * CONV_WIDTH]
            u = z_ref[:, 2 * CONV_WIDTH:Z_Q]
            vv, vv1, vv2 = _conv_taps(cg, u, cgp_ref[...], up_ref[...], has_prev)
            w0, w1, w2 = cw_ref[0:1, :], cw_ref[1:2, :], cw_ref[2:3, :]
            dyc = dy_ref[:, 0:CONV_WIDTH]
            dbg = dyc * (w0 * vv2 + w1 * vv1 + w2 * vv)
            dconv = dyc * bg
            dconv_next = jnp.where(has_next, dyn_ref[...] * bgn_ref[...], 0.0)
            ext = jnp.concatenate([dconv, dconv_next], axis=0)
            rows = ext.shape[0]
            dvv = w2 * dconv + w1 * pltpu.roll(ext, rows - 1, 0)[0:BLOCK] + w0 * pltpu.roll(ext, rows - 2, 0)[0:BLOCK]
            dcw_ref[0:1, :] += jnp.sum(dconv * vv2, axis=0, keepdims=True)
            dcw_ref[1:2, :] += jnp.sum(dconv * vv1, axis=0, keepdims=True)
            dcw_ref[2:3, :] += jnp.sum(dconv * vv, axis=0, keepdims=True)

            tab_c = tab_ref[...]
            tab_p = tabp_ref[...]
            k_all = jnp.concatenate([zkvp_ref[:, 0:LANES], z_ref[:, Z_K:Z_V]], axis=0)
            v_all = jnp.concatenate([zkvp_ref[:, LANES:2 * LANES], z_ref[:, Z_V:Z_END]], axis=0)
            kp = _head_pads(k_all)
            vp = _head_pads(v_all)
            chunks = range(N_Q_HEADS // 2)
            q2 = [(z_ref[:, Z_Q + LANES * c:Z_Q + LANES * (c + 1)] * ATTN_SCALE).astype(BF16) for c in chunks]
            do2 = [dy_ref[:, CONV_WIDTH + LANES * c:CONV_WIDTH + LANES * (c + 1)].astype(BF16) for c in chunks]
            probs = _attn_probs_t(q2, kp, _window_mask_t(has_prev), sink_ref)
            dq_chunks = []
            dk_nat = jnp.zeros((2 * BLOCK, LANES), F32)
            dv_nat = jnp.zeros((2 * BLOCK, LANES), F32)
            for kv in range(2):
                q_st = jnp.concatenate([q2[2 * kv], q2[2 * kv + 1]], axis=0)
                do_st = jnp.concatenate([do2[2 * kv], do2[2 * kv + 1]], axis=0)
                dq_t = jnp.zeros((LANES, 2 * BLOCK), F32)
                dk_par, dv_par = [], []
                for par in range(2):
                    g = 2 * kv + par
                    pr, psink = probs[g]
                    dp = _dot_nt(vp[(kv, par)], do_st)
                    delta = jnp.sum(dp * pr, axis=0, keepdims=True)
                    ds = (pr * (dp - delta)).astype(BF16)
                    dsink = -psink * delta
                    for r in range(2):
                        h = _group_head(g, r)
                        dsk_ref[h:h + 1, :] += jnp.sum(dsink[:, BLOCK * r:BLOCK * (r + 1)])
                    dq_t = dq_t + _dot_tn(kp[(kv, par)], ds)
                    dk_par.append(_dot(ds, q_st))
                    dv_par.append(_dot(pr.astype(BF16), do_st))
                for r in range(2):
                    dq_chunks.append(_rot_t(dq_t[:, BLOCK * r:BLOCK * (r + 1)].T * ATTN_SCALE, tab_c))
                dk_nat = dk_nat + _from_pads(dk_par[0], dk_par[1], kv)
                dv_nat = dv_nat + _from_pads(dv_par[0], dv_par[1], kv)

            dk_prev = _rot_t(kv_ref[:, 0:LANES] + dk_nat[0:BLOCK], tab_p)
            dv_prev = kv_ref[:, LANES:2 * LANES] + dv_nat[0:BLOCK]
            dz_ref[:, 0:Z_K] = main_ref[...]
            dz_ref[:, Z_K:Z_V] = dk_prev.astype(BF16)
            dz_ref[:, Z_V:Z_END] = dv_prev.astype(BF16)
            main_ref[:, 0:CONV_WIDTH] = dbg.astype(BF16)
            main_ref[:, CONV_WIDTH:2 * CONV_WIDTH] = (dvv * u).astype(BF16)
            main_ref[:, 2 * CONV_WIDTH:Z_Q] = (dvv * cg).astype(BF16)
            for c in range(N_Q_HEADS // 2):
                main_ref[:, Z_Q + LANES * c:Z_Q + LANES * (c + 1)] = dq_chunks[c].astype(BF16)
            kv_ref[:, 0:LANES] = dk_nat[BLOCK:2 * BLOCK]
            kv_ref[:, LANES:2 * LANES] = dv_nat[BLOCK:2 * BLOCK]

        @pl.when(n == nb)
        def _():
            dz_ref[:, 0:Z_K] = main_ref[...]
            dz_ref[:, Z_K:Z_V] = _rot_t(kv_ref[:, 0:LANES], tab_ref[...]).astype(BF16)
            dz_ref[:, Z_V:Z_END] = kv_ref[:, LANES:2 * LANES].astype(BF16)

    return _pcall(
        body, name=name, grid=(nb + 1,), in_specs=specs,
        out_specs=[pl.BlockSpec((BLOCK, Z_END), lambda n: (jnp.maximum(n - 1, 0), 0)),
                   pl.BlockSpec((8, CONV_WIDTH), lambda n: (0, 0)), pl.BlockSpec((8, LANES), lambda n: (0, 0))],
        out_shape=[jax.ShapeDtypeStruct((T, Z_END), BF16), jax.ShapeDtypeStruct((8, CONV_WIDTH), F32),
                   jax.ShapeDtypeStruct((8, LANES), F32)],
        scratch_shapes=[pltpu.VMEM((BLOCK, Z_K), BF16), pltpu.VMEM((BLOCK, 2 * LANES), F32)],
        compiler_params=_params(1),
    )(z, z, z, z, z, dy, dy, tab, tab, conv_w, sinks)


def _local_sums(pair, chip, place, name):
    arrays, in_specs, out_specs, out_shape = [], [], [], []
    if pair is not None:
        g, sib = pair
        blk = (1, *sib.shape[1:])
        arrays += [g, sib]
        in_specs += [pl.BlockSpec(blk, lambda q, p: (q, p[1], 0)), pl.BlockSpec(blk, lambda q, p: (q, 0, 0))]
        out_specs.append(pl.BlockSpec(blk, lambda q, p: (q, 0, 0)))
        out_shape.append(jax.ShapeDtypeStruct(sib.shape, BF16))
    if chip is not None:
        g2, sib2, recv2 = chip
        blk = (1, *sib2.shape[1:])
        arrays += [g2, sib2, recv2]
        in_specs += [pl.BlockSpec(blk, lambda q, p: (p[0], p[1], 0)), pl.BlockSpec(blk, lambda q, p: (p[0], 0, 0)),
                     pl.BlockSpec(recv2.shape, lambda q, p: (0, 0, 0))]
        out_specs.append(pl.BlockSpec(sib2.shape[1:], lambda q, p: (p[1], 0)))
        out_shape.append(jax.ShapeDtypeStruct(g2.shape[1:], F32))

    def body(place_ref, *refs):
        refs = list(refs)
        ins, outs = refs[:len(arrays)], refs[len(arrays):]
        if pair is not None:
            g_ref, sib_ref = ins[:2]
            outs[0][...] = (g_ref[...] + sib_ref[...].astype(F32)).astype(BF16)
        if chip is not None:
            g_ref, sib_ref, recv_ref = ins[-3:]

            @pl.when(pl.program_id(0) == 0)
            def _():
                total = g_ref[0] + sib_ref[0].astype(F32)
                for j in range(3):
                    total = total + recv_ref[j].astype(F32)
                outs[-1][...] = total

    return _pcall(
        body, name=name,
        grid_spec=pltpu.PrefetchScalarGridSpec(num_scalar_prefetch=1, grid=(N_CHIPS,),
                                               in_specs=in_specs, out_specs=out_specs),
        out_shape=out_shape, compiler_params=_params(1),
    )(place, *arrays)


def _adamw_math(w, g, m, v):
    m = ADAM_B1 * m + (1.0 - ADAM_B1) * g
    v = ADAM_B2 * v + (1.0 - ADAM_B2) * (g * g)
    m_hat = m / (1.0 - ADAM_B1 ** ADAM_STEP)
    v_hat = v / (1.0 - ADAM_B2 ** ADAM_STEP)
    delta = -ADAM_LR * (m_hat / (jnp.sqrt(v_hat) + ADAM_EPS) + ADAM_WD * w)
    return delta, m, v


def _adamw(ws, gs, ms, vs, row_blocks, name):
    n = len(ws)

    def body(*refs):
        w, g, m, v = refs[:n], refs[n:2 * n], refs[2 * n:3 * n], refs[3 * n:4 * n]
        d, mo, vo, go = refs[4 * n:5 * n], refs[5 * n:6 * n], refs[6 * n:7 * n], refs[7 * n:]
        for t in range(n):
            gv = g[t][...]
            delta, m_new, v_new = _adamw_math(w[t][...], gv, m[t][...], v[t][...])
            d[t][...] = delta
            mo[t][...] = m_new
            vo[t][...] = v_new
            go[t][...] = gv

    specs = [pl.BlockSpec((a.shape[0] // row_blocks, a.shape[1]), lambda i: (i, 0)) for a in ws]
    shapes = [jax.ShapeDtypeStruct(a.shape, F32) for a in ws]
    return _pcall(
        body, name=name, grid=(row_blocks,), in_specs=specs * 4, out_specs=specs * 4, out_shape=shapes * 4,
        compiler_params=_params(1),
    )(*ws, *gs, *ms, *vs)


def kernel(x, ffn1_norm, ffn1_w_gate, ffn1_w_up, ffn1_w_down, mix_norm, w_in, conv_w, attn_sinks, w_out, ffn2_norm, ffn2_w_gate, ffn2_w_up, ffn2_w_down, final_norm, loss_target, m_ffn1_norm, m_ffn1_w_gate, m_ffn1_w_up, m_ffn1_w_down, m_mix_norm, m_w_in, m_conv_w, m_attn_sinks, m_w_out, m_ffn2_norm, m_ffn2_w_gate, m_ffn2_w_up, m_ffn2_w_down, m_final_norm, v_ffn1_norm, v_ffn1_w_gate, v_ffn1_w_up, v_ffn1_w_down, v_mix_norm, v_w_in, v_conv_w, v_attn_sinks, v_w_out, v_ffn2_norm, v_ffn2_w_gate, v_ffn2_w_up, v_ffn2_w_down, v_final_norm):
    T, D = x.shape[1], x.shape[2]
    chip = (2 * lax.axis_index("x") + lax.axis_index("y")).astype(jnp.int32)
    core = lax.axis_index("c").astype(jnp.int32)
    place = jnp.stack([chip, core])
    x0 = x[0]
    target = loss_target[0]
    gf = final_norm.reshape(1, D)

    tr = lambda w: jnp.swapaxes(w[0], 0, 1)
    big = [tr(ffn1_w_gate), tr(ffn1_w_up), ffn1_w_down[0], tr(w_in), w_out[0], tr(ffn2_w_gate), tr(ffn2_w_up), ffn2_w_down[0]]
    transposed = [True, True, False, True, False, True, True, False]
    own_b = [w.astype(BF16) for w in big]

    def whole(gathered, own):
        return lax.dynamic_update_slice(gathered, own[None], (chip, 0, 0)).reshape(-1, D)

    got1 = _run_comm(_gather_plan(own_b[0:3]), "gather_ffn1")
    wg1, wu1, wd1 = (whole(g, o) for g, o in zip(got1, own_b[0:3]))
    tab = _rope_tables(T)

    res = _ffn_fwd(x0, ffn1_norm, wg1, wu1, wd1, "ffn1_fwd", _gather_plan(own_b[3:8], [conv_w[0]]))
    x1, h1, gate1, up1, act1 = res[:5]
    win, wout, wg2, wu2, wd2 = (whole(g, o) for g, o in zip(res[5:10], own_b[3:8]))
    convw4 = lax.dynamic_update_slice(res[10], conv_w, (chip, 0, 0))
    convw = jnp.transpose(convw4, (1, 0, 2)).reshape(3, -1)
    z, hm = _norm_matmul(x1, mix_norm, win, tab, "mix_in_fwd")
    ymix = _mix_core_fwd(z, convw, attn_sinks, "mix_core_fwd")
    x2 = _matmul_residual(ymix, wout, x1, "mix_out_fwd")
    dx3, h2, gate2, up2, act2, dgf, loss_part = _ffn_fwd(x2, ffn2_norm, wg2, wu2, wd2, "ffn2_fwd", head=(gf, target))

    dx2, dyb2, dgate2, dup2, dg2 = _ffn_bwd(dx3, x2, ffn2_norm, gate2, up2, wg2, wu2, wd2, "ffn2_bwd")
    dymix, dx2b = _matmul_nt(dx2, wout, "mix_out_bwd")
    dz, dcw, dsk = _mix_core_bwd(z, dymix, tab, convw, attn_sinks, "mix_core_bwd")
    dx1, dgm = _matmul_norm_bwd(dz, win, x1, mix_norm, dx2, "mix_in_bwd")
    dx0, dyb1, dgate1, dup1, dg1 = _ffn_bwd(dx1, x0, ffn1_norm, gate1, up1, wg1, wu1, wd1, "ffn1_bwd")

    pad = lambda a: jnp.pad(a, ((0, 0), (0, LANES - a.shape[1])))
    vec = jnp.concatenate([dg1, dgm, dg2, dgf, dcw[0:3].reshape(1, -1), pad(dsk[:, 0].reshape(1, -1)),
                           pad(loss_part[:, 0:1])], axis=1)

    jobs = [("ffn2_dwg", dgate2, h2, 5), ("ffn2_dwu", dup2, h2, 6), ("ffn2_dwd", act2, dyb2, 7),
            ("ffn1_dwg", dgate1, h1, 0), ("ffn1_dwu", dup1, h1, 1), ("ffn1_dwd", act1, dyb1, 2),
            ("mix_dwin", dz, hm, 3), ("mix_dwout", ymix, dx2b, 4)]
    n_jobs = len(jobs)
    grad, grad_b, from_sib, pair_b, from_chips, half, g_big = ({} for _ in range(7))

    def stage_plans(t):
        plans, takers = [], []
        if 0 <= t - 1 < n_jobs:
            plans.append(_sibling_plan([grad_b[t - 1]]))
            takers.append((from_sib, t - 1))
        if 0 <= t - 2 < n_jobs:
            plans.append(_scatter_plan([pair_b[t - 2]]))
            takers.append((from_chips, t - 2))
        if 0 <= t - 3 < n_jobs:
            plans.append(_join_plan([half[t - 3]]))
            takers.append((g_big, jobs[t - 3][3]))
        return plans, takers

    def after_stage(t, landed, takers):
        for (store, key), arr in zip(takers, landed):
            store[key] = arr
        pair = (grad[t - 1], from_sib[t - 1]) if 0 <= t - 1 < n_jobs else None
        chip = (grad[t - 2], from_sib[t - 2], from_chips[t - 2]) if 0 <= t - 2 < n_jobs else None
        if pair or chip:
            sums = list(_local_sums(pair, chip, place, f"local_sums_{t}"))
            if pair:
                pair_b[t - 1] = sums.pop(0)
            if chip:
                half[t - 2] = sums.pop(0)

    for t, (name_, a, b, _) in enumerate(jobs):
        plans, takers = stage_plans(t)
        if t == 0:
            plans.append(_all_gather_plan(jnp.pad(vec, ((0, 7), (0, 0)))))
        res = _matmul_tn(a, b, DW_ROW_SPLIT, name_, _merge_plans(plans))
        grad[t], grad_b[t] = (r.reshape(N_CHIPS, -1, D) for r in res[:2])
        landed = list(res[2:])
        if t == 0:
            vec_blocks = landed.pop()
        after_stage(t, landed, takers)

    ws = big
    ms = [tr(m_ffn1_w_gate), tr(m_ffn1_w_up), m_ffn1_w_down[0], tr(m_w_in), m_w_out[0], tr(m_ffn2_w_gate), tr(m_ffn2_w_up), m_ffn2_w_down[0]]
    vs = [tr(v_ffn1_w_gate), tr(v_ffn1_w_up), v_ffn1_w_down[0], tr(v_w_in), v_w_out[0], tr(v_ffn2_w_gate), tr(v_ffn2_w_up), v_ffn2_w_down[0]]
    for t in range(n_jobs, n_jobs + 3):
        plans, takers = stage_plans(t)
        after_stage(t, _run_comm(_merge_plans(plans), f"grads_tail_{t - n_jobs}"), takers)
    upd = {}
    for name_, idx in (("adamw_a", [0, 1, 2, 4]), ("adamw_b", [3, 5, 6, 7])):
        k = len(idx)
        res = _adamw([ws[i] for i in idx], [g_big[i] for i in idx], [ms[i] for i in idx], [vs[i] for i in idx], ADAMW_ROW_BLOCKS, name_)
        for j, i in enumerate(idx):
            upd[i] = (res[j], res[k + j], res[2 * k + j])
            g_big[i] = res[3 * k + j]

    total = _sum_devices(vec_blocks, "small_sum")[0:1]
    g_n1, g_nm, g_n2, g_nf = (total[:, k * D:(k + 1) * D] for k in range(4))
    cw_full = total[:, 4 * D:4 * D + 3 * CONV_WIDTH].reshape(3, CONV_WIDTH)
    cq = CONV_WIDTH // N_CHIPS
    g_cw = lax.dynamic_slice(cw_full, (0, chip * cq), (3, cq))
    off = 4 * D + 3 * CONV_WIDTH
    g_sk = total[:, off:off + N_Q_HEADS]
    loss = total[0, off + LANES]

    sw = [ffn1_norm, mix_norm, conv_w[0], attn_sinks, ffn2_norm, gf]
    sg = [g_n1, g_nm, g_cw, g_sk, g_n2, g_nf]
    sm = [m_ffn1_norm, m_mix_norm, m_conv_w[0], m_attn_sinks, m_ffn2_norm, m_final_norm.reshape(1, D)]
    sv = [v_ffn1_norm, v_mix_norm, v_conv_w[0], v_attn_sinks, v_ffn2_norm, v_final_norm.reshape(1, D)]
    sres = _adamw(sw, sg, sm, sv, 1, "adamw_small")
    supd = [(sres[j], sres[6 + j], sres[12 + j]) for j in range(6)]

    order = [("s", 0), ("b", 0), ("b", 1), ("b", 2), ("s", 1), ("b", 3), ("s", 2), ("s", 3), ("b", 4),
             ("s", 4), ("b", 5), ("b", 6), ("b", 7), ("s", 5)]

    def leaf(kind, i, which):
        if kind == "b":
            a = g_big[i] if which == 0 else upd[i][which - 1]
            return (jnp.swapaxes(a, 0, 1) if transposed[i] else a)[None]
        a = sg[i] if which == 0 else supd[i][which - 1]
        if i == 2:
            return a[None]
        if i == 5:
            return a.reshape(D)
        return a

    outs = [loss, dx0[None]]
    for which in range(4):
        outs += [leaf(kind, i, which) for kind, i in order]
    return tuple(outs)
```

```python
import jax
import jax.numpy as jnp
import numpy as np
from jax import lax
from jax.experimental import pallas as pl
from jax.experimental.pallas import tpu as pltpu

F32 = jnp.float32
BF16 = jnp.bfloat16
MESH = pl.DeviceIdType.MESH

CONV_WIDTH = 512
N_Q_HEADS = 8
HEAD_DIM = 64
BLOCK = 128
ROPE_THETA = 500000.0
ROT_DIM = 16
RMS_EPS = 1e-5
MASK_VALUE = -1e30
ATTN_SCALE = HEAD_DIM ** -0.5
FFN_RES_SCALE = 0.5
ADAM_LR = 0.001
ADAM_B1 = 0.9
ADAM_B2 = 0.999
ADAM_EPS = 1e-08
ADAM_WD = 0.01
ADAM_STEP = 10

N_CHIPS = 4
N_DEV = 8
LANES = 128
VMEM_LIMIT = 56 * 1024 * 1024

_pcall = pl.pallas_call
HBM_SPEC = pl.BlockSpec(memory_space=pltpu.HBM)
ANY_SPEC = pl.BlockSpec(memory_space=pl.ANY)


def _params(n_axes, vmem=VMEM_LIMIT):
    return pltpu.CompilerParams(dimension_semantics=("arbitrary",) * n_axes, vmem_limit_bytes=vmem)


def _dot(a, b):
    return jnp.dot(a, b, preferred_element_type=F32)


def _dot_nt(a, b):
    return lax.dot_general(a, b, (((1,), (1,)), ((), ())), preferred_element_type=F32)


def _dot_tn(a, b):
    return lax.dot_general(a, b, (((0,), (0,)), ((), ())), preferred_element_type=F32)


def _rms_inv(x):
    return lax.rsqrt(jnp.mean(x * x, axis=-1, keepdims=True) + RMS_EPS)


def _norm_bwd(dh, x, g):
    inv = _rms_inv(x)
    xhat = x * inv
    dg = jnp.sum(dh * xhat, axis=0, keepdims=True)
    dxhat = dh * g
    dx = inv * (dxhat - xhat * jnp.mean(dxhat * xhat, axis=-1, keepdims=True))
    return dx, dg


def _place():
    x, y, c = lax.axis_index("x"), lax.axis_index("y"), lax.axis_index("c")
    chips = [(1 - x, y), (x, 1 - y), (1 - x, 1 - y)]
    return x, y, c, chips


class _Plan:
    def __init__(self, arrays, out_shapes, n_sems, start, finish, middle=None, aliases=None):
        self.arrays, self.out_shapes, self.n_sems = list(arrays), list(out_shapes), n_sems
        self.start, self.finish, self.middle = start, finish, middle
        self.aliases = dict(aliases or {})

    def specs(self):
        k = len(self.arrays)
        sems = [pltpu.SemaphoreType.DMA((self.n_sems,)), pltpu.SemaphoreType.DMA((self.n_sems,))]
        return [HBM_SPEC] * k, [HBM_SPEC] * len(self.out_shapes), self.out_shapes, sems


class _SemSlice:
    def __init__(self, ref, offset):
        self.ref, self.offset = ref, offset

    @property
    def at(self):
        return self

    def __getitem__(self, k):
        return self.ref.at[k + self.offset]


def _merge_plans(plans):
    plans = [p for p in plans if p is not None]
    if len(plans) <= 1:
        return plans[0] if plans else None
    arrays, shapes, aliases, spans, n_sems = [], [], {}, [], 0
    for p in plans:
        a0, o0 = len(arrays), len(shapes)
        spans.append((a0, a0 + len(p.arrays), o0, o0 + len(p.out_shapes), n_sems))
        aliases.update({a0 + i: o0 + j for i, j in p.aliases.items()})
        arrays += p.arrays
        shapes += p.out_shapes
        n_sems += p.n_sems

    def run(which):
        def fn(ins, outs, send_sems, recv_sems):
            for p, (a0, a1, o0, o1, s0) in zip(plans, spans):
                part = getattr(p, which)
                if part is not None:
                    part(ins[a0:a1], outs[o0:o1], _SemSlice(send_sems, s0), _SemSlice(recv_sems, s0))
        return fn

    middle = run("middle") if any(p.middle is not None for p in plans) else None
    return _Plan(arrays, shapes, n_sems, run("start"), run("finish"), middle, aliases)


def _sibling_plan(grads_b):
    n = len(grads_b)

    def copies(ins, outs, send_sems, recv_sems):
        x, y, c, _ = _place()

        def copy(t):
            half = ins[t].shape[1] // 2
            return pltpu.make_async_remote_copy(
                src_ref=ins[t].at[:, pl.ds(pl.multiple_of((1 - c) * half, 16), half), :], dst_ref=outs[t],
                send_sem=send_sems.at[t], recv_sem=recv_sems.at[t], device_id=(x, y, 1 - c), device_id_type=MESH)

        return [copy(t) for t in range(n)]

    def start(*refs):
        for cp in copies(*refs):
            cp.start()

    def finish(*refs):
        for cp in copies(*refs):
            cp.wait()

    shapes = [jax.ShapeDtypeStruct((g.shape[0], g.shape[1] // 2, g.shape[2]), g.dtype) for g in grads_b]
    return _Plan(grads_b, shapes, n, start, finish)


def _scatter_plan(parts_b):
    n = len(parts_b)

    def copies(ins, outs, send_sems, recv_sems):
        x, y, c, chips = _place()

        def copy(t, j):
            px, py = chips[j]
            return pltpu.make_async_remote_copy(
                src_ref=ins[t].at[2 * px + py], dst_ref=outs[t].at[j], send_sem=send_sems.at[3 * t + j],
                recv_sem=recv_sems.at[3 * t + j], device_id=(px, py, c), device_id_type=MESH)

        return [copy(t, j) for t in range(n) for j in range(3)]

    def start(*refs):
        for cp in copies(*refs):
            cp.start()

    def finish(*refs):
        for cp in copies(*refs):
            cp.wait()

    shapes = [jax.ShapeDtypeStruct((3, *p.shape[1:]), p.dtype) for p in parts_b]
    return _Plan(parts_b, shapes, 3 * n, start, finish)


def _gather_plan(shards, small=()):
    n, ns = len(shards), len(small)
    per = 8

    def parts(ins, outs, send_sems, recv_sems):
        x, y, c, chips = _place()
        me = 2 * x + y
        blocks = [2 * px + py for px, py in chips]

        def rows(t, core, piece=None):
            half = ins[t].shape[0] // 2
            if piece is None:
                return pl.ds(pl.multiple_of(core * half, 16), half)
            return pl.ds(pl.multiple_of(core * half + piece * (half // 2), 16), half // 2)

        def remote(src, dst, k, device):
            return pltpu.make_async_remote_copy(src_ref=src, dst_ref=dst, send_sem=send_sems.at[k],
                                                recv_sem=recv_sems.at[k], device_id=device, device_id_type=MESH)

        def first(t, j, block, core):
            return remote(ins[t].at[rows(t, core), :], outs[t].at[block, rows(t, core), :], per * t + j, (*chips[j], c))

        def relay(t, j, block, core):
            ref = outs[t].at[block, rows(t, core, j), :]
            return remote(ref, ref, per * t + 2 + j, (*chips[j], c))

        def passed(t, k, block, core, piece=None):
            ref = outs[t].at[block, rows(t, core, piece), :]
            return remote(ref, ref, per * t + 4 + k, (x, y, 1 - c))

        def whole(s, j, block):
            return remote(ins[n + s], outs[n + s].at[block], per * n + 3 * s + j, (*chips[j], c))

        return c, me, blocks, first, relay, passed, whole

    def start(*refs):
        c, me, _, first, _, _, whole = parts(*refs)
        for t in range(n):
            for j in range(2):
                first(t, j, me, c).start()
        for s in range(ns):
            for j in range(3):
                whole(s, j, me).start()

    def middle(*refs):
        c, _, blocks, first, relay, passed, _ = parts(*refs)
        for t in range(n):
            for j in range(2):
                first(t, j, blocks[j], c).wait_recv()
                passed(t, j, blocks[j], c).start()
                relay(t, 1 - j, blocks[j], c).start()

    def finish(*refs):
        c, me, blocks, first, relay, passed, whole = parts(*refs)
        for t in range(n):
            for j in range(2):
                relay(t, j, blocks[2], c).wait_recv()
                passed(t, 2 + j, blocks[2], c, j).start()
        for t in range(n):
            for j in range(2):
                passed(t, j, blocks[j], 1 - c).wait_recv()
                passed(t, 2 + j, blocks[2], 1 - c, j).wait_recv()
        for s in range(ns):
            for j in range(3):
                whole(s, j, blocks[j]).wait_recv()
        for t in range(n):
            for j in range(2):
                first(t, j, me, c).wait_send()
                relay(t, 1 - j, blocks[j], c).wait_send()
                passed(t, j, blocks[j], c).wait_send()
                passed(t, 2 + j, blocks[2], c, j).wait_send()
        for s in range(ns):
            for j in range(3):
                whole(s, j, me).wait_send()

    arrays = [*shards, *small]
    shapes = [jax.ShapeDtypeStruct((N_CHIPS, *a.shape), a.dtype) for a in arrays]
    return _Plan(arrays, shapes, per * n + 3 * ns, start, finish, middle)


def _run_comm(plan, name):
    k = len(plan.arrays)
    in_specs, out_specs, out_shape, sems = plan.specs()

    def body(*refs):
        cr = (refs[:k], refs[k:k + len(out_shape)], refs[-2], refs[-1])
        plan.start(*cr)
        if plan.middle is not None:
            plan.middle(*cr)
        plan.finish(*cr)

    return _pcall(body, name=name, in_specs=in_specs, out_specs=out_specs, out_shape=out_shape,
                  input_output_aliases=plan.aliases, scratch_shapes=sems)(*plan.arrays)


def _carried(plan, in_specs, out_specs, out_shape, scratch):
    aliases = {}
    if plan is not None:
        p_in, p_out, p_shape, p_sems = plan.specs()
        aliases = {len(in_specs) + i: len(out_specs) + j for i, j in plan.aliases.items()}
        in_specs, out_specs = in_specs + p_in, out_specs + p_out
        out_shape, scratch = out_shape + p_shape, scratch + p_sems
    return dict(in_specs=in_specs, out_specs=out_specs, out_shape=out_shape, scratch_shapes=scratch,
                input_output_aliases=aliases)


def _unpack(refs, n_in, n_out, plan):
    k_in = len(plan.arrays) if plan else 0
    k_out = len(plan.out_shapes) if plan else 0
    ins = refs[:n_in]
    outs = refs[n_in + k_in:n_in + k_in + n_out]
    rest = refs[n_in + k_in + n_out + k_out:]
    if plan is None:
        return ins, outs, rest, None
    cr = (refs[n_in:n_in + k_in], refs[n_in + k_in + n_out:n_in + k_in + n_out + k_out], rest[-2], rest[-1])
    return ins, outs, rest[:-2], cr


def _hook(plan, cr, which, cond):
    fn = getattr(plan, which) if plan is not None else None
    if fn is not None:
        pl.when(cond)(lambda: fn(*cr))


def _join_plan(shards):
    n = len(shards)

    def copy(ins, outs, send_sems, recv_sems, t, core):
        x, y, c, _ = _place()
        half = ins[t].shape[0] // 2
        rows = pl.ds(pl.multiple_of(core * half, 8), half)
        return pltpu.make_async_remote_copy(
            src_ref=ins[t].at[rows, :], dst_ref=outs[t].at[rows, :], send_sem=send_sems.at[t],
            recv_sem=recv_sems.at[t], device_id=(x, y, 1 - c), device_id_type=MESH)

    def start(*refs):
        c = lax.axis_index("c")
        for t in range(n):
            copy(*refs, t, c).start()

    def finish(*refs):
        c = lax.axis_index("c")
        for t in range(n):
            copy(*refs, t, 1 - c).wait_recv()
        for t in range(n):
            copy(*refs, t, c).wait_send()

    shapes = [jax.ShapeDtypeStruct(s.shape, s.dtype) for s in shards]
    return _Plan(shards, shapes, n, start, finish, aliases={t: t for t in range(n)})


def _all_gather_plan(vec):
    def parts(ins, outs, send_sems, recv_sems):
        x, y, c, _ = _place()
        me = 4 * x + 2 * y + c
        rel = [((k >> 2) & 1, (k >> 1) & 1, k & 1) for k in range(1, N_DEV)]

        def peer(k):
            fx, fy, fc = rel[k]
            return (x ^ fx, y ^ fy, c ^ fc)

        def copy(k, dev):
            return pltpu.make_async_remote_copy(
                src_ref=ins[0], dst_ref=outs[0].at[dev], send_sem=send_sems.at[k], recv_sem=recv_sems.at[k],
                device_id=peer(k), device_id_type=MESH)

        mine = pltpu.make_async_copy(ins[0], outs[0].at[me], send_sems.at[N_DEV - 1])
        return me, peer, copy, mine

    def start(*refs):
        me, _, copy, mine = parts(*refs)
        mine.start()
        for k in range(N_DEV - 1):
            copy(k, me).start()

    def finish(*refs):
        me, peer, copy, mine = parts(*refs)
        for k in range(N_DEV - 1):
            px, py, pc = peer(k)
            copy(k, 4 * px + 2 * py + pc).wait_recv()
        for k in range(N_DEV - 1):
            copy(k, me).wait_send()
        mine.wait()

    return _Plan([vec], [jax.ShapeDtypeStruct((N_DEV, *vec.shape), vec.dtype)], N_DEV, start, finish)


def _sum_devices(blocks, name):
    def body(b_ref, o_ref):
        total = b_ref[0]
        for dev in range(1, N_DEV):
            total = total + b_ref[dev]
        o_ref[...] = total

    return _pcall(body, name=name, in_specs=[pl.BlockSpec(memory_space=pltpu.VMEM)],
                  out_specs=pl.BlockSpec(memory_space=pltpu.VMEM),
                  out_shape=jax.ShapeDtypeStruct(blocks.shape[1:], F32))(blocks)


TOKEN_TILE = 512
PROJ_TOKEN_TILE = 1024
OUT_PROJ_TOKEN_TILE = 2048
ADAMW_ROW_BLOCKS = 4
BWD_VMEM_LIMIT = 62 * 1024 * 1024
DW_TOKEN_TILE = 2048
DW_ROW_SPLIT = 2
MXU_COLS = 256
DH_GROUP = 6


def _chunks(n):
    out, c0 = [], 0
    while c0 < n:
        size = min(MXU_COLS, n - c0)
        out.append((c0, size))
        c0 += size
    return out


def _load_weights(hbm_refs, vmem_refs, sems):
    copies = [pltpu.make_async_copy(h, v, sems.at[k]) for k, (h, v) in enumerate(zip(hbm_refs, vmem_refs))]
    for cp in copies:
        cp.start()
    for cp in copies:
        cp.wait()


def _ffn_fwd(x, g, wgt, wut, wd, name, plan=None, head=None):
    T, D = x.shape
    F = wgt.shape[0]
    tm = min(T, TOKEN_TILE)
    ni = T // tm
    n_head = 2 if head is not None else 0

    def body(*refs):
        ins, outs, scratch, cr = _unpack(refs, 5 + n_head, 5 + n_head, plan)
        x_ref, g_ref, wg_hbm, wu_hbm, wd_hbm = ins[:5]
        xo_ref, h_ref, gate_ref, up_ref, act_ref = outs[:5]
        wg_ref, wu_ref, wd_ref, sems = scratch
        i = pl.program_id(0)
        _hook(plan, cr, "start", i == 0)

        @pl.when(i == 0)
        def _():
            _load_weights((wg_hbm, wu_hbm, wd_hbm), (wg_ref, wu_ref, wd_ref), sems)

        xv = x_ref[...]
        h = ((xv * _rms_inv(xv)) * g_ref[...]).astype(BF16)
        h_ref[...] = h
        for c0, size in _chunks(F):
            gate = _dot_nt(h, wg_ref[c0:c0 + size, :])
            up = _dot_nt(h, wu_ref[c0:c0 + size, :])
            gate_ref[:, c0:c0 + size] = gate.astype(BF16)
            up_ref[:, c0:c0 + size] = up.astype(BF16)
            act_ref[:, c0:c0 + size] = (gate * jax.nn.sigmoid(gate) * up).astype(BF16)
        y = x_ref[...] + FFN_RES_SCALE * _dot(act_ref[...], wd_ref[...])
        if head is None:
            xo_ref[...] = y
        else:
            gf_ref, t_ref = ins[5:]
            dgf_ref, loss_ref = outs[5:]

            @pl.when(i == 0)
            def _():
                dgf_ref[...] = jnp.zeros_like(dgf_ref)
                loss_ref[...] = jnp.zeros_like(loss_ref)

            gf = gf_ref[...]
            diff = (y * _rms_inv(y)) * gf - t_ref[...]
            loss_ref[...] += 0.5 * jnp.sum(jnp.mean(diff * diff, axis=-1, keepdims=True))
            dy, dgf = _norm_bwd(diff * (1.0 / D), y, gf)
            xo_ref[...] = dy
            dgf_ref[...] += dgf
        _hook(plan, cr, "middle", i == ni // 2)
        _hook(plan, cr, "finish", i == ni - 1)

    const = lambda shape: pl.BlockSpec(shape, lambda i: (0, 0))
    rows = lambda width: pl.BlockSpec((tm, width), lambda i: (i, 0))
    in_specs = [rows(D), const((1, D)), ANY_SPEC, ANY_SPEC, ANY_SPEC]
    out_specs = [rows(D), rows(D), rows(F), rows(F), rows(F)]
    out_shape = [jax.ShapeDtypeStruct((T, D), F32), jax.ShapeDtypeStruct((T, D), BF16),
                 jax.ShapeDtypeStruct((T, F), BF16), jax.ShapeDtypeStruct((T, F), BF16), jax.ShapeDtypeStruct((T, F), BF16)]
    if head is not None:
        in_specs += [const((1, D)), rows(D)]
        out_specs += [const((1, D)), const((1, LANES))]
        out_shape += [jax.ShapeDtypeStruct((1, D), F32), jax.ShapeDtypeStruct((1, LANES), F32)]
    io = _carried(plan, in_specs, out_specs, out_shape,
                  [pltpu.VMEM((F, D), BF16), pltpu.VMEM((F, D), BF16), pltpu.VMEM((F, D), BF16),
                   pltpu.SemaphoreType.DMA((3,))])
    return _pcall(
        body, name=name, grid=(ni,), compiler_params=_params(1), **io,
    )(x, g, wgt, wut, wd, *(head or ()), *(plan.arrays if plan else ()))


def _ffn_bwd(dy, x, g, gate, up, wgt, wut, wd, name):
    T, D = x.shape
    F = wgt.shape[0]
    tm = min(T, TOKEN_TILE)
    ni = T // tm

    def body(dy_ref, x_ref, g_ref, gate_ref, up_ref, wg_hbm, wu_hbm, wd_hbm,
             dx_ref, dyb_ref, dgate_ref, dup_ref, dg_ref, wg_ref, wu_ref, wd_ref, sems):
        @pl.when(pl.program_id(0) == 0)
        def _():
            _load_weights((wg_hbm, wu_hbm, wd_hbm), (wg_ref, wu_ref, wd_ref), sems)
            dg_ref[...] = jnp.zeros_like(dg_ref)

        dyb = (FFN_RES_SCALE * dy_ref[...]).astype(BF16)
        dyb_ref[...] = dyb
        dh, group_g, group_u, row0 = None, [], [], 0
        chunks = _chunks(F)
        for k, (c0, size) in enumerate(chunks):
            dact = _dot_nt(dyb, wd_ref[c0:c0 + size, :])
            gt = gate_ref[:, c0:c0 + size].astype(F32)
            u = up_ref[:, c0:c0 + size].astype(F32)
            sig = jax.nn.sigmoid(gt)
            dup = (dact * (gt * sig)).astype(BF16)
            dgate = (dact * u * (sig * (1.0 + gt * (1.0 - sig)))).astype(BF16)
            dup_ref[:, c0:c0 + size] = dup
            dgate_ref[:, c0:c0 + size] = dgate
            group_g.append(dgate)
            group_u.append(dup)
            if len(group_g) == DH_GROUP or k == len(chunks) - 1:
                rows = slice(row0, c0 + size)
                part = (_dot(jnp.concatenate(group_g, axis=1), wg_ref[rows, :])
                        + _dot(jnp.concatenate(group_u, axis=1), wu_ref[rows, :]))
                dh = part if dh is None else dh + part
                group_g, group_u, row0 = [], [], c0 + size
        dxn, dg = _norm_bwd(dh, x_ref[...], g_ref[...])
        dx_ref[...] = dy_ref[...] + dxn
        dg_ref[...] += dg

    return _pcall(
        body, name=name, grid=(ni,),
        in_specs=[pl.BlockSpec((tm, D), lambda i: (i, 0)), pl.BlockSpec((tm, D), lambda i: (i, 0)),
                  pl.BlockSpec((1, D), lambda i: (0, 0)),
                  pl.BlockSpec((tm, F), lambda i: (i, 0)), pl.BlockSpec((tm, F), lambda i: (i, 0)),
                  ANY_SPEC, ANY_SPEC, ANY_SPEC],
        out_specs=[pl.BlockSpec((tm, D), lambda i: (i, 0)), pl.BlockSpec((tm, D), lambda i: (i, 0)),
                   pl.BlockSpec((tm, F), lambda i: (i, 0)), pl.BlockSpec((tm, F), lambda i: (i, 0)),
                   pl.BlockSpec((1, D), lambda i: (0, 0))],
        out_shape=[jax.ShapeDtypeStruct((T, D), F32), jax.ShapeDtypeStruct((T, D), BF16),
                   jax.ShapeDtypeStruct((T, F), BF16), jax.ShapeDtypeStruct((T, F), BF16),
                   jax.ShapeDtypeStruct((1, D), F32)],
        scratch_shapes=[pltpu.VMEM((F, D), BF16), pltpu.VMEM((F, D), BF16), pltpu.VMEM((F, D), BF16),
                        pltpu.SemaphoreType.DMA((3,))],
        compiler_params=_params(1, BWD_VMEM_LIMIT),
    )(dy, x, g, gate, up, wgt, wut, wd)


def _matmul_tn(a, b, row_split, name, plan=None):
    T, n1 = a.shape
    n2 = b.shape[1]
    tn = n1 // row_split
    tk = min(T, DW_TOKEN_TILE)
    nk = T // tk

    def body(*refs):
        (a_ref, b_ref), (o_ref, ob_ref), _, cr = _unpack(refs, 2, 2, plan)
        j = pl.program_id(0)
        k = pl.program_id(1)
        _hook(plan, cr, "start", jnp.logical_and(j == 0, k == 0))

        @pl.when(k == 0)
        def _():
            o_ref[...] = jnp.zeros_like(o_ref)

        o_ref[...] += _dot_tn(a_ref[...], b_ref[...])

        @pl.when(k == nk - 1)
        def _():
            ob_ref[...] = o_ref[...].astype(BF16)

        _hook(plan, cr, "finish", jnp.logical_and(j == row_split - 1, k == nk - 1))

    io = _carried(
        plan,
        [pl.BlockSpec((tk, tn), lambda j, k: (k, j)), pl.BlockSpec((tk, n2), lambda j, k: (k, 0))],
        [pl.BlockSpec((tn, n2), lambda j, k: (j, 0)), pl.BlockSpec((tn, n2), lambda j, k: (j, 0))],
        [jax.ShapeDtypeStruct((n1, n2), F32), jax.ShapeDtypeStruct((n1, n2), BF16)], [])
    return _pcall(
        body, name=name, grid=(row_split, nk), compiler_params=_params(2), **io,
    )(a, b, *(plan.arrays if plan else ()))


def _norm_matmul(x, g, wt, tab, name):
    T, D = x.shape
    n = wt.shape[0]
    tm = min(T, PROJ_TOKEN_TILE)

    def body(x_ref, g_ref, w_ref, tab_ref, z_ref, h_ref):
        xv = x_ref[...]
        h = ((xv * _rms_inv(xv)) * g_ref[...]).astype(BF16)
        h_ref[...] = h
        z = _dot_nt(h, w_ref[...])
        z_ref[:, 0:Z_Q] = z[:, 0:Z_Q]
        tab_v = tab_ref[...]
        for c0 in range(Z_Q, Z_V, LANES):
            z_ref[:, c0:c0 + LANES] = _rot(z[:, c0:c0 + LANES], tab_v)
        z_ref[:, Z_V:Z_END] = z[:, Z_V:Z_END]

    return _pcall(
        body, name=name, grid=(T // tm,),
        in_specs=[pl.BlockSpec((tm, D), lambda i: (i, 0)), pl.BlockSpec((1, D), lambda i: (0, 0)),
                  pl.BlockSpec((n, D), lambda i: (0, 0)), pl.BlockSpec((tm, 3 * LANES), lambda i: (i, 0))],
        out_specs=[pl.BlockSpec((tm, n), lambda i: (i, 0)), pl.BlockSpec((tm, D), lambda i: (i, 0))],
        out_shape=[jax.ShapeDtypeStruct((T, n), F32), jax.ShapeDtypeStruct((T, D), BF16)],
        compiler_params=_params(1),
    )(x, g, wt, tab)


def _matmul_residual(y, w, x, name):
    T, D = x.shape
    kdim = y.shape[1]
    tm = min(T, OUT_PROJ_TOKEN_TILE)

    def body(y_ref, w_ref, x_ref, o_ref):
        o_ref[...] = x_ref[...] + _dot(y_ref[...], w_ref[...])

    return _pcall(
        body, name=name, grid=(T // tm,),
        in_specs=[pl.BlockSpec((tm, kdim), lambda i: (i, 0)), pl.BlockSpec((kdim, D), lambda i: (0, 0)),
                  pl.BlockSpec((tm, D), lambda i: (i, 0))],
        out_specs=pl.BlockSpec((tm, D), lambda i: (i, 0)),
        out_shape=jax.ShapeDtypeStruct((T, D), F32),
        compiler_params=_params(1),
    )(y, w, x)


def _matmul_nt(dx, w, name):
    T, D = dx.shape
    kdim = w.shape[0]
    tm = min(T, OUT_PROJ_TOKEN_TILE)

    def body(dx_ref, w_ref, dy_ref, dxb_ref):
        dxb = dx_ref[...].astype(BF16)
        dxb_ref[...] = dxb
        dy_ref[...] = _dot_nt(dxb, w_ref[...])

    return _pcall(
        body, name=name, grid=(T // tm,),
        in_specs=[pl.BlockSpec((tm, D), lambda i: (i, 0)), pl.BlockSpec((kdim, D), lambda i: (0, 0))],
        out_specs=[pl.BlockSpec((tm, kdim), lambda i: (i, 0)), pl.BlockSpec((tm, D), lambda i: (i, 0))],
        out_shape=[jax.ShapeDtypeStruct((T, kdim), F32), jax.ShapeDtypeStruct((T, D), BF16)],
        compiler_params=_params(1),
    )(dx, w)


def _matmul_norm_bwd(dz, wt, x, g, dres, name):
    T, D = x.shape
    n = dz.shape[1]
    tm = min(T, PROJ_TOKEN_TILE)

    def body(dz_ref, w_ref, x_ref, g_ref, dres_ref, dx_ref, dg_ref):
        @pl.when(pl.program_id(0) == 0)
        def _():
            dg_ref[...] = jnp.zeros_like(dg_ref)

        dh = _dot(dz_ref[...], w_ref[...])
        dxn, dg = _norm_bwd(dh, x_ref[...], g_ref[...])
        dx_ref[...] = dres_ref[...] + dxn
        dg_ref[...] += dg

    return _pcall(
        body, name=name, grid=(T // tm,),
        in_specs=[pl.BlockSpec((tm, n), lambda i: (i, 0)), pl.BlockSpec((n, D), lambda i: (0, 0)),
                  pl.BlockSpec((tm, D), lambda i: (i, 0)), pl.BlockSpec((1, D), lambda i: (0, 0)),
                  pl.BlockSpec((tm, D), lambda i: (i, 0))],
        out_specs=[pl.BlockSpec((tm, D), lambda i: (i, 0)), pl.BlockSpec((1, D), lambda i: (0, 0))],
        out_shape=[jax.ShapeDtypeStruct((T, D), F32), jax.ShapeDtypeStruct((1, D), F32)],
        compiler_params=_params(1),
    )(dz, wt, x, g, dres)


Z_Q = 3 * CONV_WIDTH
Z_K = Z_Q + N_Q_HEADS * HEAD_DIM
Z_V = Z_K + LANES
Z_END = Z_V + LANES


def _rope_tables(T):
    half = ROT_DIM // 2
    inv_freq = ROPE_THETA ** (-jnp.arange(0, ROT_DIM, 2, dtype=F32) / ROT_DIM)
    ang = inv_freq[:, None] * jnp.arange(T, dtype=F32)[None, :]
    cos_sin = jnp.concatenate([jnp.cos(ang), jnp.sin(ang)], axis=0)
    select = np.zeros((2 * half, 3 * LANES), np.float32)
    const = np.zeros((1, 3 * LANES), np.float32)
    for lane in range(LANES):
        d = lane % HEAD_DIM
        if d < half:
            select[d, lane] = 1.0
            select[half + d, LANES + lane] = -1.0
        elif d < ROT_DIM:
            select[d - half, lane] = 1.0
            select[d, 2 * LANES + lane] = 1.0
        else:
            const[0, lane] = 1.0
    tab = lax.dot_general(cos_sin, jnp.asarray(select), (((0,), (0,)), ((), ())),
                          precision=lax.Precision.HIGHEST, preferred_element_type=F32)
    return tab + jnp.asarray(const)


def _tab3(tab):
    return tab[:, 0:LANES], tab[:, LANES:2 * LANES], tab[:, 2 * LANES:3 * LANES]


def _rot(x, tab):
    c, s1, s2 = _tab3(tab)
    return x * c + pltpu.roll(x, LANES - ROT_DIM // 2, 1) * s1 + pltpu.roll(x, ROT_DIM // 2, 1) * s2


def _rot_t(d, tab):
    c, s1, s2 = _tab3(tab)
    return d * c + pltpu.roll(d * s1, ROT_DIM // 2, 1) + pltpu.roll(d * s2, LANES - ROT_DIM // 2, 1)


def _head_pads(a):
    lo = lax.broadcasted_iota(jnp.int32, a.shape, 1) < HEAD_DIM
    nat0 = jnp.where(lo, a, 0.0)
    nat1 = jnp.where(lo, 0.0, a)
    return {
        (0, 0): nat0.astype(BF16), (0, 1): pltpu.roll(nat0, HEAD_DIM, 1).astype(BF16),
        (1, 0): pltpu.roll(nat1, HEAD_DIM, 1).astype(BF16), (1, 1): nat1.astype(BF16),
    }


def _from_pads(even, odd, kv):
    lo = lax.broadcasted_iota(jnp.int32, even.shape, 1) < HEAD_DIM
    if kv == 0:
        return jnp.where(lo, even + pltpu.roll(odd, HEAD_DIM, 1), 0.0)
    return jnp.where(lo, 0.0, pltpu.roll(even, HEAD_DIM, 1) + odd)


N_GROUPS = 4


def _group_head(g, r):
    kv, par = divmod(g, 2)
    return 2 * (2 * kv + r) + par


def _window_mask_t(has_prev):
    jj = lax.broadcasted_iota(jnp.int32, (2 * BLOCK, 2 * BLOCK), 0)
    ii = lax.broadcasted_iota(jnp.int32, (2 * BLOCK, 2 * BLOCK), 1) & (BLOCK - 1)
    rel = jj - BLOCK - ii
    return (rel <= 0) & (rel > -BLOCK) & ((jj >= BLOCK) | has_prev)


def _sink_row(sink_ref, g):
    lane = lax.broadcasted_iota(jnp.int32, (1, 2 * BLOCK), 1)
    return jnp.where(lane < BLOCK, sink_ref[0, _group_head(g, 0)], sink_ref[0, _group_head(g, 1)])


def _attn_probs_t(q2, kp, mask, sink_ref):
    out = []
    for kv in range(2):
        q_st = jnp.concatenate([q2[2 * kv], q2[2 * kv + 1]], axis=0)
        for par in range(2):
            s = jnp.where(mask, _dot_nt(kp[(kv, par)], q_st), MASK_VALUE)
            sink = _sink_row(sink_ref, 2 * kv + par)
            m = jnp.maximum(jnp.max(s, axis=0, keepdims=True), sink)
            p = jnp.exp(s - m)
            esink = jnp.exp(sink - m)
            rden = 1.0 / (jnp.sum(p, axis=0, keepdims=True) + esink)
            out.append((p * rden, esink * rden))
    return out


def _conv_taps(cg, u, cg_prev, u_prev, has_prev):
    vv = cg * u
    halo = jnp.where(has_prev, cg_prev * u_prev, 0.0)
    ext = jnp.concatenate([halo, vv], axis=0)
    rows = ext.shape[0]
    vv1 = pltpu.roll(ext, 1, 0)[8:rows]
    vv2 = pltpu.roll(ext, 2, 0)[8:rows]
    return vv, vv1, vv2


def _mix_specs(nb):
    cur = lambda n: jnp.minimum(n, nb - 1)
    prev = lambda n: jnp.maximum(jnp.minimum(n, nb - 1) - 1, 0)
    rows8_prev = lambda n: jnp.maximum(16 * jnp.minimum(n, nb - 1) - 1, 0)
    return cur, prev, [
        pl.BlockSpec((BLOCK, Z_END), lambda n: (cur(n), 0)),
        pl.BlockSpec((BLOCK, 2 * LANES), lambda n: (prev(n), Z_K // (2 * LANES))),
        pl.BlockSpec((8, CONV_WIDTH), lambda n: (rows8_prev(n), 1)),
        pl.BlockSpec((8, CONV_WIDTH), lambda n: (rows8_prev(n), 2)),
        pl.BlockSpec((BLOCK, 3 * LANES), lambda n: (cur(n), 0)),
        pl.BlockSpec((BLOCK, 3 * LANES), lambda n: (prev(n), 0)),
        pl.BlockSpec((3, CONV_WIDTH), lambda n: (0, 0)),
        pl.BlockSpec(memory_space=pltpu.SMEM),
    ]


FWD_BLOCKS = 2


def _mix_core_fwd(z, conv_w, sinks, name):
    T = z.shape[0]
    rows = FWD_BLOCKS * BLOCK
    steps = T // rows
    prev_block = lambda n: jnp.maximum(FWD_BLOCKS * n - 1, 0)
    prev_rows8 = lambda n: jnp.maximum((rows // 8) * n - 1, 0)

    def body(z_ref, zkvp_ref, cgp_ref, up_ref, cw_ref, sink_ref, y_ref):
        for b in range(FWD_BLOCKS):
            r0 = b * BLOCK
            blk = slice(r0, r0 + BLOCK)
            if b == 0:
                has_prev = pl.program_id(0) > 0
                kv_prev, cg_prev, u_prev = zkvp_ref[...], cgp_ref[...], up_ref[...]
            else:
                has_prev = True
                kv_prev = z_ref[r0 - BLOCK:r0, Z_K:Z_END]
                cg_prev = z_ref[r0 - 8:r0, CONV_WIDTH:2 * CONV_WIDTH]
                u_prev = z_ref[r0 - 8:r0, 2 * CONV_WIDTH:Z_Q]
            bg = z_ref[blk, 0:CONV_WIDTH]
            vv, vv1, vv2 = _conv_taps(z_ref[blk, CONV_WIDTH:2 * CONV_WIDTH], z_ref[blk, 2 * CONV_WIDTH:Z_Q],
                                      cg_prev, u_prev, has_prev)
            conv = cw_ref[0:1, :] * vv2 + cw_ref[1:2, :] * vv1 + cw_ref[2:3, :] * vv
            y_ref[blk, 0:CONV_WIDTH] = (bg * conv).astype(BF16)

            k_all = jnp.concatenate([kv_prev[:, 0:LANES], z_ref[blk, Z_K:Z_V]], axis=0)
            v_all = jnp.concatenate([kv_prev[:, LANES:2 * LANES], z_ref[blk, Z_V:Z_END]], axis=0)
            kp = _head_pads(k_all)
            vp = _head_pads(v_all)
            q2 = [(z_ref[blk, Z_Q + LANES * c:Z_Q + LANES * (c + 1)] * ATTN_SCALE).astype(BF16)
                  for c in range(N_Q_HEADS // 2)]
            probs = _attn_probs_t(q2, kp, _window_mask_t(has_prev), sink_ref)
            for kv in range(2):
                o_t = (_dot_tn(vp[(kv, 0)], probs[2 * kv][0].astype(BF16))
                       + _dot_tn(vp[(kv, 1)], probs[2 * kv + 1][0].astype(BF16)))
                for r in range(2):
                    c = 2 * kv + r
                    y_ref[blk, CONV_WIDTH + LANES * c:CONV_WIDTH + LANES * (c + 1)] = (
                        o_t[:, BLOCK * r:BLOCK * (r + 1)].T.astype(BF16))

    return _pcall(
        body, name=name, grid=(steps,),
        in_specs=[pl.BlockSpec((rows, Z_END), lambda n: (n, 0)),
                  pl.BlockSpec((BLOCK, 2 * LANES), lambda n: (prev_block(n), Z_K // (2 * LANES))),
                  pl.BlockSpec((8, CONV_WIDTH), lambda n: (prev_rows8(n), 1)),
                  pl.BlockSpec((8, CONV_WIDTH), lambda n: (prev_rows8(n), 2)),
                  pl.BlockSpec((3, CONV_WIDTH), lambda n: (0, 0)),
                  pl.BlockSpec(memory_space=pltpu.SMEM)],
        out_specs=pl.BlockSpec((rows, 2 * CONV_WIDTH), lambda n: (n, 0)),
        out_shape=jax.ShapeDtypeStruct((T, 2 * CONV_WIDTH), BF16),
        compiler_params=_params(1),
    )(z, z, z, z, conv_w, sinks)


def _mix_core_bwd(z, dy, tab, conv_w, sinks, name):
    T = z.shape[0]
    nb = T // BLOCK
    cur, _, specs = _mix_specs(nb)
    rows8_next = lambda n: jnp.minimum(16 * (cur(n) + 1), 16 * nb - 1)
    specs = specs[:4] + [
        pl.BlockSpec((8, CONV_WIDTH), lambda n: (rows8_next(n), 0)),
        pl.BlockSpec((BLOCK, 2 * CONV_WIDTH), lambda n: (cur(n), 0)),
        pl.BlockSpec((8, CONV_WIDTH), lambda n: (rows8_next(n), 0)),
    ] + specs[4:]

    def body(z_ref, zkvp_ref, cgp_ref, up_ref, bgn_ref, dy_ref, dyn_ref, tab_ref, tabp_ref, cw_ref, sink_ref,
             dz_ref, dcw_ref, dsk_ref, main_ref, kv_ref):
        n = pl.program_id(0)

        @pl.when(n == 0)
        def _():
            main_ref[...] = jnp.zeros_like(main_ref)
            kv_ref[...] = jnp.zeros_like(kv_ref)
            dcw_ref[...] = jnp.zeros_like(dcw_ref)
            dsk_ref[...] = jnp.zeros_like(dsk_ref)

        @pl.when(n < nb)
        def _():
            has_prev = n > 0
            has_next = n < nb - 1
            bg = z_ref[:, 0:CONV_WIDTH]
            cg = z_ref[:, CONV_WIDTH:2 * CONV_WIDTH]
            u = z_ref[:, 2 * CONV_WIDTH:Z_Q]
            vv, vv1, vv2 = _conv_taps(cg, u, cgp_ref[...], up_ref[...], has_prev)
            w0, w1, w2 = cw_ref[0:1, :], cw_ref[1:2, :], cw_ref[2:3, :]
            dyc = dy_ref[:, 0:CONV_WIDTH]
            dbg = dyc * (w0 * vv2 + w1 * vv1 + w2 * vv)
            dconv = dyc * bg
            dconv_next = jnp.where(has_next, dyn_ref[...] * bgn_ref[...], 0.0)
            ext = jnp.concatenate([dconv, dconv_next], axis=0)
            rows = ext.shape[0]
            dvv = w2 * dconv + w1 * pltpu.roll(ext, rows - 1, 0)[0:BLOCK] + w0 * pltpu.roll(ext, rows - 2, 0)[0:BLOCK]
            dcw_ref[0:1, :] += jnp.sum(dconv * vv2, axis=0, keepdims=True)
            dcw_ref[1:2, :] += jnp.sum(dconv * vv1, axis=0, keepdims=True)
            dcw_ref[2:3, :] += jnp.sum(dconv * vv, axis=0, keepdims=True)

            tab_c = tab_ref[...]
            tab_p = tabp_ref[...]
            k_all = jnp.concatenate([zkvp_ref[:, 0:LANES], z_ref[:, Z_K:Z_V]], axis=0)
            v_all = jnp.concatenate([zkvp_ref[:, LANES:2 * LANES], z_ref[:, Z_V:Z_END]], axis=0)
            kp = _head_pads(k_all)
            vp = _head_pads(v_all)
            chunks = range(N_Q_HEADS // 2)
            q2 = [(z_ref[:, Z_Q + LANES * c:Z_Q + LANES * (c + 1)] * ATTN_SCALE).astype(BF16) for c in chunks]
            do2 = [dy_ref[:, CONV_WIDTH + LANES * c:CONV_WIDTH + LANES * (c + 1)].astype(BF16) for c in chunks]
            probs = _attn_probs_t(q2, kp, _window_mask_t(has_prev), sink_ref)
            dq_chunks = []
            dk_nat = jnp.zeros((2 * BLOCK, LANES), F32)
            dv_nat = jnp.zeros((2 * BLOCK, LANES), F32)
            for kv in range(2):
                q_st = jnp.concatenate([q2[2 * kv], q2[2 * kv + 1]], axis=0)
                do_st = jnp.concatenate([do2[2 * kv], do2[2 * kv + 1]], axis=0)
                dq_t = jnp.zeros((LANES, 2 * BLOCK), F32)
                dk_par, dv_par = [], []
                for par in range(2):
                    g = 2 * kv + par
                    pr, psink = probs[g]
                    dp = _dot_nt(vp[(kv, par)], do_st)
                    delta = jnp.sum(dp * pr, axis=0, keepdims=True)
                    ds = (pr * (dp - delta)).astype(BF16)
                    dsink = -psink * delta
                    for r in range(2):
                        h = _group_head(g, r)
                        dsk_ref[h:h + 1, :] += jnp.sum(dsink[:, BLOCK * r:BLOCK * (r + 1)])
                    dq_t = dq_t + _dot_tn(kp[(kv, par)], ds)
                    dk_par.append(_dot(ds, q_st))
                    dv_par.append(_dot(pr.astype(BF16), do_st))
                for r in range(2):
                    dq_chunks.append(_rot_t(dq_t[:, BLOCK * r:BLOCK * (r + 1)].T * ATTN_SCALE, tab_c))
                dk_nat = dk_nat + _from_pads(dk_par[0], dk_par[1], kv)
                dv_nat = dv_nat + _from_pads(dv_par[0], dv_par[1], kv)

            dk_prev = _rot_t(kv_ref[:, 0:LANES] + dk_nat[0:BLOCK], tab_p)
            dv_prev = kv_ref[:, LANES:2 * LANES] + dv_nat[0:BLOCK]
            dz_ref[:, 0:Z_K] = main_ref[...]
            dz_ref[:, Z_K:Z_V] = dk_prev.astype(BF16)
            dz_ref[:, Z_V:Z_END] = dv_prev.astype(BF16)
            main_ref[:, 0:CONV_WIDTH] = dbg.astype(BF16)
            main_ref[:, CONV_WIDTH:2 * CONV_WIDTH] = (dvv * u).astype(BF16)
            main_ref[:, 2 * CONV_WIDTH:Z_Q] = (dvv * cg).astype(BF16)
            for c in range(N_Q_HEADS // 2):
                main_ref[:, Z_Q + LANES * c:Z_Q + LANES * (c + 1)] = dq_chunks[c].astype(BF16)
            kv_ref[:, 0:LANES] = dk_nat[BLOCK:2 * BLOCK]
            kv_ref[:, LANES:2 * LANES] = dv_nat[BLOCK:2 * BLOCK]

        @pl.when(n == nb)
        def _():
            dz_ref[:, 0:Z_K] = main_ref[...]
            dz_ref[:, Z_K:Z_V] = _rot_t(kv_ref[:, 0:LANES], tab_ref[...]).astype(BF16)
            dz_ref[:, Z_V:Z_END] = kv_ref[:, LANES:2 * LANES].astype(BF16)

    return _pcall(
        body, name=name, grid=(nb + 1,), in_specs=specs,
        out_specs=[pl.BlockSpec((BLOCK, Z_END), lambda n: (jnp.maximum(n - 1, 0), 0)),
                   pl.BlockSpec((8, CONV_WIDTH), lambda n: (0, 0)), pl.BlockSpec((8, LANES), lambda n: (0, 0))],
        out_shape=[jax.ShapeDtypeStruct((T, Z_END), BF16), jax.ShapeDtypeStruct((8, CONV_WIDTH), F32),
                   jax.ShapeDtypeStruct((8, LANES), F32)],
        scratch_shapes=[pltpu.VMEM((BLOCK, Z_K), BF16), pltpu.VMEM((BLOCK, 2 * LANES), F32)],
        compiler_params=_params(1),
    )(z, z, z, z, z, dy, dy, tab, tab, conv_w, sinks)


def _local_sums(pair, chip, place, name):
    arrays, in_specs, out_specs, out_shape = [], [], [], []
    if pair is not None:
        g, sib = pair
        blk = (1, *sib.shape[1:])
        arrays += [g, sib]
        in_specs += [pl.BlockSpec(blk, lambda q, p: (q, p[1], 0)), pl.BlockSpec(blk, lambda q, p: (q, 0, 0))]
        out_specs.append(pl.BlockSpec(blk, lambda q, p: (q, 0, 0)))
        out_shape.append(jax.ShapeDtypeStruct(sib.shape, BF16))
    if chip is not None:
        g2, sib2, recv2 = chip
        blk = (1, *sib2.shape[1:])
        arrays += [g2, sib2, recv2]
        in_specs += [pl.BlockSpec(blk, lambda q, p: (p[0], p[1], 0)), pl.BlockSpec(blk, lambda q, p: (p[0], 0, 0)),
                     pl.BlockSpec(recv2.shape, lambda q, p: (0, 0, 0))]
        out_specs.append(pl.BlockSpec(sib2.shape[1:], lambda q, p: (p[1], 0)))
        out_shape.append(jax.ShapeDtypeStruct(g2.shape[1:], F32))

    def body(place_ref, *refs):
        refs = list(refs)
        ins, outs = refs[:len(arrays)], refs[len(arrays):]
        if pair is not None:
            g_ref, sib_ref = ins[:2]
            outs[0][...] = (g_ref[...] + sib_ref[...].astype(F32)).astype(BF16)
        if chip is not None:
            g_ref, sib_ref, recv_ref = ins[-3:]

            @pl.when(pl.program_id(0) == 0)
            def _():
                total = g_ref[0] + sib_ref[0].astype(F32)
                for j in range(3):
                    total = total + recv_ref[j].astype(F32)
                outs[-1][...] = total

    return _pcall(
        body, name=name,
        grid_spec=pltpu.PrefetchScalarGridSpec(num_scalar_prefetch=1, grid=(N_CHIPS,),
                                               in_specs=in_specs, out_specs=out_specs),
        out_shape=out_shape, compiler_params=_params(1),
    )(place, *arrays)


def _adamw_math(w, g, m, v):
    m = ADAM_B1 * m + (1.0 - ADAM_B1) * g
    v = ADAM_B2 * v + (1.0 - ADAM_B2) * (g * g)
    m_hat = m / (1.0 - ADAM_B1 ** ADAM_STEP)
    v_hat = v / (1.0 - ADAM_B2 ** ADAM_STEP)
    delta = -ADAM_LR * (m_hat / (jnp.sqrt(v_hat) + ADAM_EPS) + ADAM_WD * w)
    return delta, m, v


def _adamw(ws, gs, ms, vs, row_blocks, name):
    n = len(ws)

    def body(*refs):
        w, g, m, v = refs[:n], refs[n:2 * n], refs[2 * n:3 * n], refs[3 * n:4 * n]
        d, mo, vo, go = refs[4 * n:5 * n], refs[5 * n:6 * n], refs[6 * n:7 * n], refs[7 * n:]
        for t in range(n):
            gv = g[t][...]
            delta, m_new, v_new = _adamw_math(w[t][...], gv, m[t][...], v[t][...])
            d[t][...] = delta
            mo[t][...] = m_new
            vo[t][...] = v_new
            go[t][...] = gv

    specs = [pl.BlockSpec((a.shape[0] // row_blocks, a.shape[1]), lambda i: (i, 0)) for a in ws]
    shapes = [jax.ShapeDtypeStruct(a.shape, F32) for a in ws]
    return _pcall(
        body, name=name, grid=(row_blocks,), in_specs=specs * 4, out_specs=specs * 4, out_shape=shapes * 4,
        compiler_params=_params(1),
    )(*ws, *gs, *ms, *vs)


def kernel(x, ffn1_norm, ffn1_w_gate, ffn1_w_up, ffn1_w_down, mix_norm, w_in, conv_w, attn_sinks, w_out, ffn2_norm, ffn2_w_gate, ffn2_w_up, ffn2_w_down, final_norm, loss_target, m_ffn1_norm, m_ffn1_w_gate, m_ffn1_w_up, m_ffn1_w_down, m_mix_norm, m_w_in, m_conv_w, m_attn_sinks, m_w_out, m_ffn2_norm, m_ffn2_w_gate, m_ffn2_w_up, m_ffn2_w_down, m_final_norm, v_ffn1_norm, v_ffn1_w_gate, v_ffn1_w_up, v_ffn1_w_down, v_mix_norm, v_w_in, v_conv_w, v_attn_sinks, v_w_out, v_ffn2_norm, v_ffn2_w_gate, v_ffn2_w_up, v_ffn2_w_down, v_final_norm):
    T, D = x.shape[1], x.shape[2]
    chip = (2 * lax.axis_index("x") + lax.axis_index("y")).astype(jnp.int32)
    core = lax.axis_index("c").astype(jnp.int32)
    place = jnp.stack([chip, core])
    x0 = x[0]
    target = loss_target[0]
    gf = final_norm.reshape(1, D)

    tr = lambda w: jnp.swapaxes(w[0], 0, 1)
    big = [tr(ffn1_w_gate), tr(ffn1_w_up), ffn1_w_down[0], tr(w_in), w_out[0], tr(ffn2_w_gate), tr(ffn2_w_up), ffn2_w_down[0]]
    transposed = [True, True, False, True, False, True, True, False]
    own_b = [w.astype(BF16) for w in big]

    def whole(gathered, own):
        return lax.dynamic_update_slice(gathered, own[None], (chip, 0, 0)).reshape(-1, D)

    got1 = _run_comm(_gather_plan(own_b[0:3]), "gather_ffn1")
    wg1, wu1, wd1 = (whole(g, o) for g, o in zip(got1, own_b[0:3]))
    tab = _rope_tables(T)

    res = _ffn_fwd(x0, ffn1_norm, wg1, wu1, wd1, "ffn1_fwd", _gather_plan(own_b[3:8], [conv_w[0]]))
    x1, h1, gate1, up1, act1 = res[:5]
    win, wout, wg2, wu2, wd2 = (whole(g, o) for g, o in zip(res[5:10], own_b[3:8]))
    convw4 = lax.dynamic_update_slice(res[10], conv_w, (chip, 0, 0))
    convw = jnp.transpose(convw4, (1, 0, 2)).reshape(3, -1)
    z, hm = _norm_matmul(x1, mix_norm, win, tab, "mix_in_fwd")
    ymix = _mix_core_fwd(z, convw, attn_sinks, "mix_core_fwd")
    x2 = _matmul_residual(ymix, wout, x1, "mix_out_fwd")
    dx3, h2, gate2, up2, act2, dgf, loss_part = _ffn_fwd(x2, ffn2_norm, wg2, wu2, wd2, "ffn2_fwd", head=(gf, target))

    dx2, dyb2, dgate2, dup2, dg2 = _ffn_bwd(dx3, x2, ffn2_norm, gate2, up2, wg2, wu2, wd2, "ffn2_bwd")
    dymix, dx2b = _matmul_nt(dx2, wout, "mix_out_bwd")
    dz, dcw, dsk = _mix_core_bwd(z, dymix, tab, convw, attn_sinks, "mix_core_bwd")
    dx1, dgm = _matmul_norm_bwd(dz, win, x1, mix_norm, dx2, "mix_in_bwd")
    dx0, dyb1, dgate1, dup1, dg1 = _ffn_bwd(dx1, x0, ffn1_norm, gate1, up1, wg1, wu1, wd1, "ffn1_bwd")

    pad = lambda a: jnp.pad(a, ((0, 0), (0, LANES - a.shape[1])))
    vec = jnp.concatenate([dg1, dgm, dg2, dgf, dcw[0:3].reshape(1, -1), pad(dsk[:, 0].reshape(1, -1)),
                           pad(loss_part[:, 0:1])], axis=1)

    jobs = [("ffn2_dwg", dgate2, h2, 5), ("ffn2_dwu", dup2, h2, 6), ("ffn2_dwd", act2, dyb2, 7),
            ("ffn1_dwg", dgate1, h1, 0), ("ffn1_dwu", dup1, h1, 1), ("ffn1_dwd", act1, dyb1, 2),
            ("mix_dwin", dz, hm, 3), ("mix_dwout", ymix, dx2b, 4)]
    n_jobs = len(jobs)
    grad, grad_b, from_sib, pair_b, from_chips, half, g_big = ({} for _ in range(7))

    def stage_plans(t):
        plans, takers = [], []
        if 0 <= t - 1 < n_jobs:
            plans.append(_sibling_plan([grad_b[t - 1]]))
            takers.append((from_sib, t - 1))
        if 0 <= t - 2 < n_jobs:
            plans.append(_scatter_plan([pair_b[t - 2]]))
            takers.append((from_chips, t - 2))
        if 0 <= t - 3 < n_jobs:
            plans.append(_join_plan([half[t - 3]]))
            takers.append((g_big, jobs[t - 3][3]))
        return plans, takers

    def after_stage(t, landed, takers):
        for (store, key), arr in zip(takers, landed):
            store[key] = arr
        pair = (grad[t - 1], from_sib[t - 1]) if 0 <= t - 1 < n_jobs else None
        chip = (grad[t - 2], from_sib[t - 2], from_chips[t - 2]) if 0 <= t - 2 < n_jobs else None
        if pair or chip:
            sums = list(_local_sums(pair, chip, place, f"local_sums_{t}"))
            if pair:
                pair_b[t - 1] = sums.pop(0)
            if chip:
                half[t - 2] = sums.pop(0)

    for t, (name_, a, b, _) in enumerate(jobs):
        plans, takers = stage_plans(t)
        if t == 0:
            plans.append(_all_gather_plan(jnp.pad(vec, ((0, 7), (0, 0)))))
        res = _matmul_tn(a, b, DW_ROW_SPLIT, name_, _merge_plans(plans))
        grad[t], grad_b[t] = (r.reshape(N_CHIPS, -1, D) for r in res[:2])
        landed = list(res[2:])
        if t == 0:
            vec_blocks = landed.pop()
        after_stage(t, landed, takers)

    ws = big
    ms = [tr(m_ffn1_w_gate), tr(m_ffn1_w_up), m_ffn1_w_down[0], tr(m_w_in), m_w_out[0], tr(m_ffn2_w_gate), tr(m_ffn2_w_up), m_ffn2_w_down[0]]
    vs = [tr(v_ffn1_w_gate), tr(v_ffn1_w_up), v_ffn1_w_down[0], tr(v_w_in), v_w_out[0], tr(v_ffn2_w_gate), tr(v_ffn2_w_up), v_ffn2_w_down[0]]
    for t in range(n_jobs, n_jobs + 3):
        plans, takers = stage_plans(t)
        after_stage(t, _run_comm(_merge_plans(plans), f"grads_tail_{t - n_jobs}"), takers)
    upd = {}
    for name_, idx in (("adamw_a", [0, 1, 2, 4]), ("adamw_b", [3, 5, 6, 7])):
        k = len(idx)
        res = _adamw([ws[i] for i in idx], [g_big[i] for i in idx], [ms[i] for i in idx], [vs[i] for i in idx], ADAMW_ROW_BLOCKS, name_)
        for j, i in enumerate(idx):
            upd[i] = (res[j], res[k + j], res[2 * k + j])
            g_big[i] = res[3 * k + j]

    total = _sum_devices(vec_blocks, "small_sum")[0:1]
    g_n1, g_nm, g_n2, g_nf = (total[:, k * D:(k + 1) * D] for k in range(4))
    cw_full = total[:, 4 * D:4 * D + 3 * CONV_WIDTH].reshape(3, CONV_WIDTH)
    cq = CONV_WIDTH // N_CHIPS
    g_cw = lax.dynamic_slice(cw_full, (0, chip * cq), (3, cq))
    off = 4 * D + 3 * CONV_WIDTH
    g_sk = total[:, off:off + N_Q_HEADS]
    loss = total[0, off + LANES]

    sw = [ffn1_norm, mix_norm, conv_w[0], attn_sinks, ffn2_norm, gf]
    sg = [g_n1, g_nm, g_cw, g_sk, g_n2, g_nf]
    sm = [m_ffn1_norm, m_mix_norm, m_conv_w[0], m_attn_sinks, m_ffn2_norm, m_final_norm.reshape(1, D)]
    sv = [v_ffn1_norm, v_mix_norm, v_conv_w[0], v_attn_sinks, v_ffn2_norm, v_final_norm.reshape(1, D)]
    sres = _adamw(sw, sg, sm, sv, 1, "adamw_small")
    supd = [(sres[j], sres[6 + j], sres[12 + j]) for j in range(6)]

    order = [("s", 0), ("b", 0), ("b", 1), ("b", 2), ("s", 1), ("b", 3), ("s", 2), ("s", 3), ("b", 4),
             ("s", 4), ("b", 5), ("b", 6), ("b", 7), ("s", 5)]

    def leaf(kind, i, which):
        if kind == "b":
            a = g_big[i] if which == 0 else upd[i][which - 1]
            return (jnp.swapaxes(a, 0, 1) if transposed[i] else a)[None]
        a = sg[i] if which == 0 else supd[i][which - 1]
        if i == 2:
            return a[None]
        if i == 5:
            return a.reshape(D)
        return a

    outs = [loss, dx0[None]]
    for which in range(4):
        outs += [leaf(kind, i, which) for kind, i in order]
    return tuple(outs)
```

```python
import jax
import jax.numpy as jnp
import numpy as np
from jax import lax
from jax.experimental import pallas as pl
from jax.experimental.pallas import tpu as pltpu

F32 = jnp.float32
BF16 = jnp.bfloat16
MESH = pl.DeviceIdType.MESH

CONV_WIDTH = 512
N_Q_HEADS = 8
HEAD_DIM = 64
BLOCK = 128
ROPE_THETA = 500000.0
ROT_DIM = 16
RMS_EPS = 1e-5
MASK_VALUE = -1e30
ATTN_SCALE = HEAD_DIM ** -0.5
FFN_RES_SCALE = 0.5
ADAM_LR = 0.001
ADAM_B1 = 0.9
ADAM_B2 = 0.999
ADAM_EPS = 1e-08
ADAM_WD = 0.01
ADAM_STEP = 10

N_CHIPS = 4
N_DEV = 8
LANES = 128
VMEM_LIMIT = 56 * 1024 * 1024

_pcall = pl.pallas_call
HBM_SPEC = pl.BlockSpec(memory_space=pltpu.HBM)
ANY_SPEC = pl.BlockSpec(memory_space=pl.ANY)


def _params(n_axes, vmem=VMEM_LIMIT):
    return pltpu.CompilerParams(dimension_semantics=("arbitrary",) * n_axes, vmem_limit_bytes=vmem)


def _dot(a, b):
    return jnp.dot(a, b, preferred_element_type=F32)


def _dot_nt(a, b):
    return lax.dot_general(a, b, (((1,), (1,)), ((), ())), preferred_element_type=F32)


def _dot_tn(a, b):
    return lax.dot_general(a, b, (((0,), (0,)), ((), ())), preferred_element_type=F32)


def _rms_inv(x):
    return lax.rsqrt(jnp.mean(x * x, axis=-1, keepdims=True) + RMS_EPS)


def _norm_bwd(dh, x, g):
    inv = _rms_inv(x)
    xhat = x * inv
    dg = jnp.sum(dh * xhat, axis=0, keepdims=True)
    dxhat = dh * g
    dx = inv * (dxhat - xhat * jnp.mean(dxhat * xhat, axis=-1, keepdims=True))
    return dx, dg


def _place():
    x, y, c = lax.axis_index("x"), lax.axis_index("y"), lax.axis_index("c")
    chips = [(1 - x, y), (x, 1 - y), (1 - x, 1 - y)]
    return x, y, c, chips


class _Plan:
    def __init__(self, arrays, out_shapes, n_sems, start, finish, middle=None, aliases=None):
        self.arrays, self.out_shapes, self.n_sems = list(arrays), list(out_shapes), n_sems
        self.start, self.finish, self.middle = start, finish, middle
        self.aliases = dict(aliases or {})

    def specs(self):
        k = len(self.arrays)
        sems = [pltpu.SemaphoreType.DMA((self.n_sems,)), pltpu.SemaphoreType.DMA((self.n_sems,))]
        return [HBM_SPEC] * k, [HBM_SPEC] * len(self.out_shapes), self.out_shapes, sems


class _SemSlice:
    def __init__(self, ref, offset):
        self.ref, self.offset = ref, offset

    @property
    def at(self):
        return self

    def __getitem__(self, k):
        return self.ref.at[k + self.offset]


def _merge_plans(plans):
    plans = [p for p in plans if p is not None]
    if len(plans) <= 1:
        return plans[0] if plans else None
    arrays, shapes, aliases, spans, n_sems = [], [], {}, [], 0
    for p in plans:
        a0, o0 = len(arrays), len(shapes)
        spans.append((a0, a0 + len(p.arrays), o0, o0 + len(p.out_shapes), n_sems))
        aliases.update({a0 + i: o0 + j for i, j in p.aliases.items()})
        arrays += p.arrays
        shapes += p.out_shapes
        n_sems += p.n_sems

    def run(which):
        def fn(ins, outs, send_sems, recv_sems):
            for p, (a0, a1, o0, o1, s0) in zip(plans, spans):
                part = getattr(p, which)
                if part is not None:
                    part(ins[a0:a1], outs[o0:o1], _SemSlice(send_sems, s0), _SemSlice(recv_sems, s0))
        return fn

    middle = run("middle") if any(p.middle is not None for p in plans) else None
    return _Plan(arrays, shapes, n_sems, run("start"), run("finish"), middle, aliases)


def _sibling_plan(grads_b):
    n = len(grads_b)

    def copies(ins, outs, send_sems, recv_sems):
        x, y, c, _ = _place()

        def copy(t):
            half = ins[t].shape[1] // 2
            return pltpu.make_async_remote_copy(
                src_ref=ins[t].at[:, pl.ds(pl.multiple_of((1 - c) * half, 16), half), :], dst_ref=outs[t],
                send_sem=send_sems.at[t], recv_sem=recv_sems.at[t], device_id=(x, y, 1 - c), device_id_type=MESH)

        return [copy(t) for t in range(n)]

    def start(*refs):
        for cp in copies(*refs):
            cp.start()

    def finish(*refs):
        for cp in copies(*refs):
            cp.wait()

    shapes = [jax.ShapeDtypeStruct((g.shape[0], g.shape[1] // 2, g.shape[2]), g.dtype) for g in grads_b]
    return _Plan(grads_b, shapes, n, start, finish)


def _scatter_plan(parts_b):
    n = len(parts_b)

    def copies(ins, outs, send_sems, recv_sems):
        x, y, c, chips = _place()

        def copy(t, j):
            px, py = chips[j]
            return pltpu.make_async_remote_copy(
                src_ref=ins[t].at[2 * px + py], dst_ref=outs[t].at[j], send_sem=send_sems.at[3 * t + j],
                recv_sem=recv_sems.at[3 * t + j], device_id=(px, py, c), device_id_type=MESH)

        return [copy(t, j) for t in range(n) for j in range(3)]

    def start(*refs):
        for cp in copies(*refs):
            cp.start()

    def finish(*refs):
        for cp in copies(*refs):
            cp.wait()

    shapes = [jax.ShapeDtypeStruct((3, *p.shape[1:]), p.dtype) for p in parts_b]
    return _Plan(parts_b, shapes, 3 * n, start, finish)


def _gather_plan(shards, small=()):
    n, ns = len(shards), len(small)
    per = 8

    def parts(ins, outs, send_sems, recv_sems):
        x, y, c, chips = _place()
        me = 2 * x + y
        blocks = [2 * px + py for px, py in chips]

        def rows(t, core, piece=None):
            half = ins[t].shape[0] // 2
            if piece is None:
                return pl.ds(pl.multiple_of(core * half, 16), half)
            return pl.ds(pl.multiple_of(core * half + piece * (half // 2), 16), half // 2)

        def remote(src, dst, k, device):
            return pltpu.make_async_remote_copy(src_ref=src, dst_ref=dst, send_sem=send_sems.at[k],
                                                recv_sem=recv_sems.at[k], device_id=device, device_id_type=MESH)

        def first(t, j, block, core):
            return remote(ins[t].at[rows(t, core), :], outs[t].at[block, rows(t, core), :], per * t + j, (*chips[j], c))

        def relay(t, j, block, core):
            ref = outs[t].at[block, rows(t, core, j), :]
            return remote(ref, ref, per * t + 2 + j, (*chips[j], c))

        def passed(t, k, block, core, piece=None):
            ref = outs[t].at[block, rows(t, core, piece), :]
            return remote(ref, ref, per * t + 4 + k, (x, y, 1 - c))

        def whole(s, j, block):
            return remote(ins[n + s], outs[n + s].at[block], per * n + 3 * s + j, (*chips[j], c))

        return c, me, blocks, first, relay, passed, whole

    def start(*refs):
        c, me, _, first, _, _, whole = parts(*refs)
        for t in range(n):
            for j in range(2):
                first(t, j, me, c).start()
        for s in range(ns):
            for j in range(3):
                whole(s, j, me).start()

    def middle(*refs):
        c, _, blocks, first, relay, passed, _ = parts(*refs)
        for t in range(n):
            for j in range(2):
                first(t, j, blocks[j], c).wait_recv()
                passed(t, j, blocks[j], c).start()
                relay(t, 1 - j, blocks[j], c).start()

    def finish(*refs):
        c, me, blocks, first, relay, passed, whole = parts(*refs)
        for t in range(n):
            for j in range(2):
                relay(t, j, blocks[2], c).wait_recv()
                passed(t, 2 + j, blocks[2], c, j).start()
        for t in range(n):
            for j in range(2):
                passed(t, j, blocks[j], 1 - c).wait_recv()
                passed(t, 2 + j, blocks[2], 1 - c, j).wait_recv()
        for s in range(ns):
            for j in range(3):
                whole(s, j, blocks[j]).wait_recv()
        for t in range(n):
            for j in range(2):
                first(t, j, me, c).wait_send()
                relay(t, 1 - j, blocks[j], c).wait_send()
                passed(t, j, blocks[j], c).wait_send()
                passed(t, 2 + j, blocks[2], c, j).wait_send()
        for s in range(ns):
            for j in range(3):
                whole(s, j, me).wait_send()

    arrays = [*shards, *small]
    shapes = [jax.ShapeDtypeStruct((N_CHIPS, *a.shape), a.dtype) for a in arrays]
    return _Plan(arrays, shapes, per * n + 3 * ns, start, finish, middle)


def _run_comm(plan, name):
    k = len(plan.arrays)
    in_specs, out_specs, out_shape, sems = plan.specs()

    def body(*refs):
        cr = (refs[:k], refs[k:k + len(out_shape)], refs[-2], refs[-1])
        plan.start(*cr)
        if plan.middle is not None:
            plan.middle(*cr)
        plan.finish(*cr)

    return _pcall(body, name=name, in_specs=in_specs, out_specs=out_specs, out_shape=out_shape,
                  input_output_aliases=plan.aliases, scratch_shapes=sems)(*plan.arrays)


def _carried(plan, in_specs, out_specs, out_shape, scratch):
    aliases = {}
    if plan is not None:
        p_in, p_out, p_shape, p_sems = plan.specs()
        aliases = {len(in_specs) + i: len(out_specs) + j for i, j in plan.aliases.items()}
        in_specs, out_specs = in_specs + p_in, out_specs + p_out
        out_shape, scratch = out_shape + p_shape, scratch + p_sems
    return dict(in_specs=in_specs, out_specs=out_specs, out_shape=out_shape, scratch_shapes=scratch,
                input_output_aliases=aliases)


def _unpack(refs, n_in, n_out, plan):
    k_in = len(plan.arrays) if plan else 0
    k_out = len(plan.out_shapes) if plan else 0
    ins = refs[:n_in]
    outs = refs[n_in + k_in:n_in + k_in + n_out]
    rest = refs[n_in + k_in + n_out + k_out:]
    if plan is None:
        return ins, outs, rest, None
    cr = (refs[n_in:n_in + k_in], refs[n_in + k_in + n_out:n_in + k_in + n_out + k_out], rest[-2], rest[-1])
    return ins, outs, rest[:-2], cr


def _hook(plan, cr, which, cond):
    fn = getattr(plan, which) if plan is not None else None
    if fn is not None:
        pl.when(cond)(lambda: fn(*cr))


def _join_plan(shards):
    n = len(shards)

    def copy(ins, outs, send_sems, recv_sems, t, core):
        x, y, c, _ = _place()
        half = ins[t].shape[0] // 2
        rows = pl.ds(pl.multiple_of(core * half, 8), half)
        return pltpu.make_async_remote_copy(
            src_ref=ins[t].at[rows, :], dst_ref=outs[t].at[rows, :], send_sem=send_sems.at[t],
            recv_sem=recv_sems.at[t], device_id=(x, y, 1 - c), device_id_type=MESH)

    def start(*refs):
        c = lax.axis_index("c")
        for t in range(n):
            copy(*refs, t, c).start()

    def finish(*refs):
        c = lax.axis_index("c")
        for t in range(n):
            copy(*refs, t, 1 - c).wait_recv()
        for t in range(n):
            copy(*refs, t, c).wait_send()

    shapes = [jax.ShapeDtypeStruct(s.shape, s.dtype) for s in shards]
    return _Plan(shards, shapes, n, start, finish, aliases={t: t for t in range(n)})


def _all_gather_plan(vec):
    def parts(ins, outs, send_sems, recv_sems):
        x, y, c, _ = _place()
        me = 4 * x + 2 * y + c
        rel = [((k >> 2) & 1, (k >> 1) & 1, k & 1) for k in range(1, N_DEV)]

        def peer(k):
            fx, fy, fc = rel[k]
            return (x ^ fx, y ^ fy, c ^ fc)

        def copy(k, dev):
            return pltpu.make_async_remote_copy(
                src_ref=ins[0], dst_ref=outs[0].at[dev], send_sem=send_sems.at[k], recv_sem=recv_sems.at[k],
                device_id=peer(k), device_id_type=MESH)

        mine = pltpu.make_async_copy(ins[0], outs[0].at[me], send_sems.at[N_DEV - 1])
        return me, peer, copy, mine

    def start(*refs):
        me, _, copy, mine = parts(*refs)
        mine.start()
        for k in range(N_DEV - 1):
            copy(k, me).start()

    def finish(*refs):
        me, peer, copy, mine = parts(*refs)
        for k in range(N_DEV - 1):
            px, py, pc = peer(k)
            copy(k, 4 * px + 2 * py + pc).wait_recv()
        for k in range(N_DEV - 1):
            copy(k, me).wait_send()
        mine.wait()

    return _Plan([vec], [jax.ShapeDtypeStruct((N_DEV, *vec.shape), vec.dtype)], N_DEV, start, finish)


def _sum_devices(blocks, name):
    def body(b_ref, o_ref):
        total = b_ref[0]
        for dev in range(1, N_DEV):
            total = total + b_ref[dev]
        o_ref[...] = total

    return _pcall(body, name=name, in_specs=[pl.BlockSpec(memory_space=pltpu.VMEM)],
                  out_specs=pl.BlockSpec(memory_space=pltpu.VMEM),
                  out_shape=jax.ShapeDtypeStruct(blocks.shape[1:], F32))(blocks)


TOKEN_TILE = 512
PROJ_TOKEN_TILE = 1024
OUT_PROJ_TOKEN_TILE = 2048
ADAMW_ROW_BLOCKS = 4
BWD_VMEM_LIMIT = 62 * 1024 * 1024
DW_TOKEN_TILE = 2048
DW_ROW_SPLIT = 2
MXU_COLS = 256
DH_GROUP = 6


def _chunks(n):
    out, c0 = [], 0
    while c0 < n:
        size = min(MXU_COLS, n - c0)
        out.append((c0, size))
        c0 += size
    return out


def _load_weights(hbm_refs, vmem_refs, sems):
    copies = [pltpu.make_async_copy(h, v, sems.at[k]) for k, (h, v) in enumerate(zip(hbm_refs, vmem_refs))]
    for cp in copies:
        cp.start()
    for cp in copies:
        cp.wait()


def _ffn_fwd(x, g, wgt, wut, wd, name, plan=None, head=None):
    T, D = x.shape
    F = wgt.shape[0]
    tm = min(T, TOKEN_TILE)
    ni = T // tm
    n_head = 2 if head is not None else 0

    def body(*refs):
        ins, outs, scratch, cr = _unpack(refs, 5 + n_head, 5 + n_head, plan)
        x_ref, g_ref, wg_hbm, wu_hbm, wd_hbm = ins[:5]
        xo_ref, h_ref, gate_ref, up_ref, act_ref = outs[:5]
        wg_ref, wu_ref, wd_ref, sems = scratch
        i = pl.program_id(0)
        _hook(plan, cr, "start", i == 0)

        @pl.when(i == 0)
        def _():
            _load_weights((wg_hbm, wu_hbm, wd_hbm), (wg_ref, wu_ref, wd_ref), sems)

        xv = x_ref[...]
        h = ((xv * _rms_inv(xv)) * g_ref[...]).astype(BF16)
        h_ref[...] = h
        for c0, size in _chunks(F):
            gate = _dot_nt(h, wg_ref[c0:c0 + size, :])
            up = _dot_nt(h, wu_ref[c0:c0 + size, :])
            gate_ref[:, c0:c0 + size] = gate.astype(BF16)
            up_ref[:, c0:c0 + size] = up.astype(BF16)
            act_ref[:, c0:c0 + size] = (gate * jax.nn.sigmoid(gate) * up).astype(BF16)
        y = x_ref[...] + FFN_RES_SCALE * _dot(act_ref[...], wd_ref[...])
        if head is None:
            xo_ref[...] = y
        else:
            gf_ref, t_ref = ins[5:]
            dgf_ref, loss_ref = outs[5:]

            @pl.when(i == 0)
            def _():
                dgf_ref[...] = jnp.zeros_like(dgf_ref)
                loss_ref[...] = jnp.zeros_like(loss_ref)

            gf = gf_ref[...]
            diff = (y * _rms_inv(y)) * gf - t_ref[...]
            loss_ref[...] += 0.5 * jnp.sum(jnp.mean(diff * diff, axis=-1, keepdims=True))
            dy, dgf = _norm_bwd(diff * (1.0 / D), y, gf)
            xo_ref[...] = dy
            dgf_ref[...] += dgf
        _hook(plan, cr, "middle", i == ni // 2)
        _hook(plan, cr, "finish", i == ni - 1)

    const = lambda shape: pl.BlockSpec(shape, lambda i: (0, 0))
    rows = lambda width: pl.BlockSpec((tm, width), lambda i: (i, 0))
    in_specs = [rows(D), const((1, D)), ANY_SPEC, ANY_SPEC, ANY_SPEC]
    out_specs = [rows(D), rows(D), rows(F), rows(F), rows(F)]
    out_shape = [jax.ShapeDtypeStruct((T, D), F32), jax.ShapeDtypeStruct((T, D), BF16),
                 jax.ShapeDtypeStruct((T, F), BF16), jax.ShapeDtypeStruct((T, F), BF16), jax.ShapeDtypeStruct((T, F), BF16)]
    if head is not None:
        in_specs += [const((1, D)), rows(D)]
        out_specs += [const((1, D)), const((1, LANES))]
        out_shape += [jax.ShapeDtypeStruct((1, D), F32), jax.ShapeDtypeStruct((1, LANES), F32)]
    io = _carried(plan, in_specs, out_specs, out_shape,
                  [pltpu.VMEM((F, D), BF16), pltpu.VMEM((F, D), BF16), pltpu.VMEM((F, D), BF16),
                   pltpu.SemaphoreType.DMA((3,))])
    return _pcall(
        body, name=name, grid=(ni,), compiler_params=_params(1), **io,
    )(x, g, wgt, wut, wd, *(head or ()), *(plan.arrays if plan else ()))


def _ffn_bwd(dy, x, g, gate, up, wgt, wut, wd, name):
    T, D = x.shape
    F = wgt.shape[0]
    tm = min(T, TOKEN_TILE)
    ni = T // tm

    def body(dy_ref, x_ref, g_ref, gate_ref, up_ref, wg_hbm, wu_hbm, wd_hbm,
             dx_ref, dyb_ref, dgate_ref, dup_ref, dg_ref, wg_ref, wu_ref, wd_ref, sems):
        @pl.when(pl.program_id(0) == 0)
        def _():
            _load_weights((wg_hbm, wu_hbm, wd_hbm), (wg_ref, wu_ref, wd_ref), sems)
            dg_ref[...] = jnp.zeros_like(dg_ref)

        dyb = (FFN_RES_SCALE * dy_ref[...]).astype(BF16)
        dyb_ref[...] = dyb
        dh, group_g, group_u, row0 = None, [], [], 0
        chunks = _chunks(F)
        for k, (c0, size) in enumerate(chunks):
            dact = _dot_nt(dyb, wd_ref[c0:c0 + size, :])
            gt = gate_ref[:, c0:c0 + size].astype(F32)
            u = up_ref[:, c0:c0 + size].astype(F32)
            sig = jax.nn.sigmoid(gt)
            dup = (dact * (gt * sig)).astype(BF16)
            dgate = (dact * u * (sig * (1.0 + gt * (1.0 - sig)))).astype(BF16)
            dup_ref[:, c0:c0 + size] = dup
            dgate_ref[:, c0:c0 + size] = dgate
            group_g.append(dgate)
            group_u.append(dup)
            if len(group_g) == DH_GROUP or k == len(chunks) - 1:
                rows = slice(row0, c0 + size)
                part = (_dot(jnp.concatenate(group_g, axis=1), wg_ref[rows, :])
                        + _dot(jnp.concatenate(group_u, axis=1), wu_ref[rows, :]))
                dh = part if dh is None else dh + part
                group_g, group_u, row0 = [], [], c0 + size
        dxn, dg = _norm_bwd(dh, x_ref[...], g_ref[...])
        dx_ref[...] = dy_ref[...] + dxn
        dg_ref[...] += dg

    return _pcall(
        body, name=name, grid=(ni,),
        in_specs=[pl.BlockSpec((tm, D), lambda i: (i, 0)), pl.BlockSpec((tm, D), lambda i: (i, 0)),
                  pl.BlockSpec((1, D), lambda i: (0, 0)),
                  pl.BlockSpec((tm, F), lambda i: (i, 0)), pl.BlockSpec((tm, F), lambda i: (i, 0)),
                  ANY_SPEC, ANY_SPEC, ANY_SPEC],
        out_specs=[pl.BlockSpec((tm, D), lambda i: (i, 0)), pl.BlockSpec((tm, D), lambda i: (i, 0)),
                   pl.BlockSpec((tm, F), lambda i: (i, 0)), pl.BlockSpec((tm, F), lambda i: (i, 0)),
                   pl.BlockSpec((1, D), lambda i: (0, 0))],
        out_shape=[jax.ShapeDtypeStruct((T, D), F32), jax.ShapeDtypeStruct((T, D), BF16),
                   jax.ShapeDtypeStruct((T, F), BF16), jax.ShapeDtypeStruct((T, F), BF16),
                   jax.ShapeDtypeStruct((1, D), F32)],
        scratch_shapes=[pltpu.VMEM((F, D), BF16), pltpu.VMEM((F, D), BF16), pltpu.VMEM((F, D), BF16),
                        pltpu.SemaphoreType.DMA((3,))],
        compiler_params=_params(1, BWD_VMEM_LIMIT),
    )(dy, x, g, gate, up, wgt, wut, wd)


def _matmul_tn(a, b, row_split, name, plan=None):
    T, n1 = a.shape
    n2 = b.shape[1]
    tn = n1 // row_split
    tk = min(T, DW_TOKEN_TILE)
    nk = T // tk

    def body(*refs):
        (a_ref, b_ref), (o_ref, ob_ref), _, cr = _unpack(refs, 2, 2, plan)
        j = pl.program_id(0)
        k = pl.program_id(1)
        _hook(plan, cr, "start", jnp.logical_and(j == 0, k == 0))

        @pl.when(k == 0)
        def _():
            o_ref[...] = jnp.zeros_like(o_ref)

        o_ref[...] += _dot_tn(a_ref[...], b_ref[...])

        @pl.when(k == nk - 1)
        def _():
            ob_ref[...] = o_ref[...].astype(BF16)

        _hook(plan, cr, "finish", jnp.logical_and(j == row_split - 1, k == nk - 1))

    io = _carried(
        plan,
        [pl.BlockSpec((tk, tn), lambda j, k: (k, j)), pl.BlockSpec((tk, n2), lambda j, k: (k, 0))],
        [pl.BlockSpec((tn, n2), lambda j, k: (j, 0)), pl.BlockSpec((tn, n2), lambda j, k: (j, 0))],
        [jax.ShapeDtypeStruct((n1, n2), F32), jax.ShapeDtypeStruct((n1, n2), BF16)], [])
    return _pcall(
        body, name=name, grid=(row_split, nk), compiler_params=_params(2), **io,
    )(a, b, *(plan.arrays if plan else ()))


def _norm_matmul(x, g, wt, tab, name):
    T, D = x.shape
    n = wt.shape[0]
    tm = min(T, PROJ_TOKEN_TILE)

    def body(x_ref, g_ref, w_ref, tab_ref, z_ref, h_ref):
        xv = x_ref[...]
        h = ((xv * _rms_inv(xv)) * g_ref[...]).astype(BF16)
        h_ref[...] = h
        z = _dot_nt(h, w_ref[...])
        z_ref[:, 0:Z_Q] = z[:, 0:Z_Q]
        tab_v = tab_ref[...]
        for c0 in range(Z_Q, Z_V, LANES):
            z_ref[:, c0:c0 + LANES] = _rot(z[:, c0:c0 + LANES], tab_v)
        z_ref[:, Z_V:Z_END] = z[:, Z_V:Z_END]

    return _pcall(
        body, name=name, grid=(T // tm,),
        in_specs=[pl.BlockSpec((tm, D), lambda i: (i, 0)), pl.BlockSpec((1, D), lambda i: (0, 0)),
                  pl.BlockSpec((n, D), lambda i: (0, 0)), pl.BlockSpec((tm, 3 * LANES), lambda i: (i, 0))],
        out_specs=[pl.BlockSpec((tm, n), lambda i: (i, 0)), pl.BlockSpec((tm, D), lambda i: (i, 0))],
        out_shape=[jax.ShapeDtypeStruct((T, n), F32), jax.ShapeDtypeStruct((T, D), BF16)],
        compiler_params=_params(1),
    )(x, g, wt, tab)


def _matmul_residual(y, w, x, name):
    T, D = x.shape
    kdim = y.shape[1]
    tm = min(T, OUT_PROJ_TOKEN_TILE)

    def body(y_ref, w_ref, x_ref, o_ref):
        o_ref[...] = x_ref[...] + _dot(y_ref[...], w_ref[...])

    return _pcall(
        body, name=name, grid=(T // tm,),
        in_specs=[pl.BlockSpec((tm, kdim), lambda i: (i, 0)), pl.BlockSpec((kdim, D), lambda i: (0, 0)),
                  pl.BlockSpec((tm, D), lambda i: (i, 0))],
        out_specs=pl.BlockSpec((tm, D), lambda i: (i, 0)),
        out_shape=jax.ShapeDtypeStruct((T, D), F32),
        compiler_params=_params(1),
    )(y, w, x)


def _matmul_nt(dx, w, name):
    T, D = dx.shape
    kdim = w.shape[0]
    tm = min(T, OUT_PROJ_TOKEN_TILE)

    def body(dx_ref, w_ref, dy_ref, dxb_ref):
        dxb = dx_ref[...].astype(BF16)
        dxb_ref[...] = dxb
        dy_ref[...] = _dot_nt(dxb, w_ref[...])

    return _pcall(
        body, name=name, grid=(T // tm,),
        in_specs=[pl.BlockSpec((tm, D), lambda i: (i, 0)), pl.BlockSpec((kdim, D), lambda i: (0, 0))],
        out_specs=[pl.BlockSpec((tm, kdim), lambda i: (i, 0)), pl.BlockSpec((tm, D), lambda i: (i, 0))],
        out_shape=[jax.ShapeDtypeStruct((T, kdim), F32), jax.ShapeDtypeStruct((T, D), BF16)],
        compiler_params=_params(1),
    )(dx, w)


def _matmul_norm_bwd(dz, wt, x, g, dres, name):
    T, D = x.shape
    n = dz.shape[1]
    tm = min(T, PROJ_TOKEN_TILE)

    def body(dz_ref, w_ref, x_ref, g_ref, dres_ref, dx_ref, dg_ref):
        @pl.when(pl.program_id(0) == 0)
        def _():
            dg_ref[...] = jnp.zeros_like(dg_ref)

        dh = _dot(dz_ref[...], w_ref[...])
        dxn, dg = _norm_bwd(dh, x_ref[...], g_ref[...])
        dx_ref[...] = dres_ref[...] + dxn
        dg_ref[...] += dg

    return _pcall(
        body, name=name, grid=(T // tm,),
        in_specs=[pl.BlockSpec((tm, n), lambda i: (i, 0)), pl.BlockSpec((n, D), lambda i: (0, 0)),
                  pl.BlockSpec((tm, D), lambda i: (i, 0)), pl.BlockSpec((1, D), lambda i: (0, 0)),
                  pl.BlockSpec((tm, D), lambda i: (i, 0))],
        out_specs=[pl.BlockSpec((tm, D), lambda i: (i, 0)), pl.BlockSpec((1, D), lambda i: (0, 0))],
        out_shape=[jax.ShapeDtypeStruct((T, D), F32), jax.ShapeDtypeStruct((1, D), F32)],
        compiler_params=_params(1),
    )(dz, wt, x, g, dres)


Z_Q = 3 * CONV_WIDTH
Z_K = Z_Q + N_Q_HEADS * HEAD_DIM
Z_V = Z_K + LANES
Z_END = Z_V + LANES


def _rope_tables(T):
    half = ROT_DIM // 2
    inv_freq = ROPE_THETA ** (-jnp.arange(0, ROT_DIM, 2, dtype=F32) / ROT_DIM)
    ang = inv_freq[:, None] * jnp.arange(T, dtype=F32)[None, :]
    cos_sin = jnp.concatenate([jnp.cos(ang), jnp.sin(ang)], axis=0)
    select = np.zeros((2 * half, 3 * LANES), np.float32)
    const = np.zeros((1, 3 * LANES), np.float32)
    for lane in range(LANES):
        d = lane % HEAD_DIM
        if d < half:
            select[d, lane] = 1.0
            select[half + d, LANES + lane] = -1.0
        elif d < ROT_DIM:
            select[d - half, lane] = 1.0
            select[d, 2 * LANES + lane] = 1.0
        else:
            const[0, lane] = 1.0
    tab = lax.dot_general(cos_sin, jnp.asarray(select), (((0,), (0,)), ((), ())),
                          precision=lax.Precision.HIGHEST, preferred_element_type=F32)
    return tab + jnp.asarray(const)


def _tab3(tab):
    return tab[:, 0:LANES], tab[:, LANES:2 * LANES], tab[:, 2 * LANES:3 * LANES]


def _rot(x, tab):
    c, s1, s2 = _tab3(tab)
    return x * c + pltpu.roll(x, LANES - ROT_DIM // 2, 1) * s1 + pltpu.roll(x, ROT_DIM // 2, 1) * s2


def _rot_t(d, tab):
    c, s1, s2 = _tab3(tab)
    return d * c + pltpu.roll(d * s1, ROT_DIM // 2, 1) + pltpu.roll(d * s2, LANES - ROT_DIM // 2, 1)


def _head_pads(a):
    lo = lax.broadcasted_iota(jnp.int32, a.shape, 1) < HEAD_DIM
    nat0 = jnp.where(lo, a, 0.0)
    nat1 = jnp.where(lo, 0.0, a)
    return {
        (0, 0): nat0.astype(BF16), (0, 1): pltpu.roll(nat0, HEAD_DIM, 1).astype(BF16),
        (1, 0): pltpu.roll(nat1, HEAD_DIM, 1).astype(BF16), (1, 1): nat1.astype(BF16),
    }


def _from_pads(even, odd, kv):
    lo = lax.broadcasted_iota(jnp.int32, even.shape, 1) < HEAD_DIM
    if kv == 0:
        return jnp.where(lo, even + pltpu.roll(odd, HEAD_DIM, 1), 0.0)
    return jnp.where(lo, 0.0, pltpu.roll(even, HEAD_DIM, 1) + odd)


N_GROUPS = 4


def _group_head(g, r):
    kv, par = divmod(g, 2)
    return 2 * (2 * kv + r) + par


def _window_mask_t(has_prev):
    jj = lax.broadcasted_iota(jnp.int32, (2 * BLOCK, 2 * BLOCK), 0)
    ii = lax.broadcasted_iota(jnp.int32, (2 * BLOCK, 2 * BLOCK), 1) & (BLOCK - 1)
    rel = jj - BLOCK - ii
    return (rel <= 0) & (rel > -BLOCK) & ((jj >= BLOCK) | has_prev)


def _sink_row(sink_ref, g):
    lane = lax.broadcasted_iota(jnp.int32, (1, 2 * BLOCK), 1)
    return jnp.where(lane < BLOCK, sink_ref[0, _group_head(g, 0)], sink_ref[0, _group_head(g, 1)])


def _attn_probs_t(q2, kp, mask, sink_ref):
    out = []
    for kv in range(2):
        q_st = jnp.concatenate([q2[2 * kv], q2[2 * kv + 1]], axis=0)
        for par in range(2):
            s = jnp.where(mask, _dot_nt(kp[(kv, par)], q_st), MASK_VALUE)
            sink = _sink_row(sink_ref, 2 * kv + par)
            m = jnp.maximum(jnp.max(s, axis=0, keepdims=True), sink)
            p = jnp.exp(s - m)
            esink = jnp.exp(sink - m)
            rden = 1.0 / (jnp.sum(p, axis=0, keepdims=True) + esink)
            out.append((p * rden, esink * rden))
    return out


def _conv_taps(cg, u, cg_prev, u_prev, has_prev):
    vv = cg * u
    halo = jnp.where(has_prev, cg_prev * u_prev, 0.0)
    ext = jnp.concatenate([halo, vv], axis=0)
    rows = ext.shape[0]
    vv1 = pltpu.roll(ext, 1, 0)[8:rows]
    vv2 = pltpu.roll(ext, 2, 0)[8:rows]
    return vv, vv1, vv2


MIX_BLOCKS = 4


def _mix_core_fwd(z, conv_w, sinks, name):
    T = z.shape[0]
    rows = MIX_BLOCKS * BLOCK
    steps = T // rows
    prev_block = lambda n: jnp.maximum(MIX_BLOCKS * n - 1, 0)
    prev_rows8 = lambda n: jnp.maximum((rows // 8) * n - 1, 0)

    def body(z_ref, zkvp_ref, cgp_ref, up_ref, cw_ref, sink_ref, y_ref):
        for b in range(MIX_BLOCKS):
            r0 = b * BLOCK
            blk = slice(r0, r0 + BLOCK)
            if b == 0:
                has_prev = pl.program_id(0) > 0
                kv_prev, cg_prev, u_prev = zkvp_ref[...], cgp_ref[...], up_ref[...]
            else:
                has_prev = True
                kv_prev = z_ref[r0 - BLOCK:r0, Z_K:Z_END]
                cg_prev = z_ref[r0 - 8:r0, CONV_WIDTH:2 * CONV_WIDTH]
                u_prev = z_ref[r0 - 8:r0, 2 * CONV_WIDTH:Z_Q]
            bg = z_ref[blk, 0:CONV_WIDTH]
            vv, vv1, vv2 = _conv_taps(z_ref[blk, CONV_WIDTH:2 * CONV_WIDTH], z_ref[blk, 2 * CONV_WIDTH:Z_Q],
                                      cg_prev, u_prev, has_prev)
            conv = cw_ref[0:1, :] * vv2 + cw_ref[1:2, :] * vv1 + cw_ref[2:3, :] * vv
            y_ref[blk, 0:CONV_WIDTH] = (bg * conv).astype(BF16)

            k_all = jnp.concatenate([kv_prev[:, 0:LANES], z_ref[blk, Z_K:Z_V]], axis=0)
            v_all = jnp.concatenate([kv_prev[:, LANES:2 * LANES], z_ref[blk, Z_V:Z_END]], axis=0)
            kp = _head_pads(k_all)
            vp = _head_pads(v_all)
            q2 = [(z_ref[blk, Z_Q + LANES * c:Z_Q + LANES * (c + 1)] * ATTN_SCALE).astype(BF16)
                  for c in range(N_Q_HEADS // 2)]
            probs = _attn_probs_t(q2, kp, _window_mask_t(has_prev), sink_ref)
            for kv in range(2):
                o_t = (_dot_tn(vp[(kv, 0)], probs[2 * kv][0].astype(BF16))
                       + _dot_tn(vp[(kv, 1)], probs[2 * kv + 1][0].astype(BF16)))
                for r in range(2):
                    c = 2 * kv + r
                    y_ref[blk, CONV_WIDTH + LANES * c:CONV_WIDTH + LANES * (c + 1)] = (
                        o_t[:, BLOCK * r:BLOCK * (r + 1)].T.astype(BF16))

    return _pcall(
        body, name=name, grid=(steps,),
        in_specs=[pl.BlockSpec((rows, Z_END), lambda n: (n, 0)),
                  pl.BlockSpec((BLOCK, 2 * LANES), lambda n: (prev_block(n), Z_K // (2 * LANES))),
                  pl.BlockSpec((8, CONV_WIDTH), lambda n: (prev_rows8(n), 1)),
                  pl.BlockSpec((8, CONV_WIDTH), lambda n: (prev_rows8(n), 2)),
                  pl.BlockSpec((3, CONV_WIDTH), lambda n: (0, 0)),
                  pl.BlockSpec(memory_space=pltpu.SMEM)],
        out_specs=pl.BlockSpec((rows, 2 * CONV_WIDTH), lambda n: (n, 0)),
        out_shape=jax.ShapeDtypeStruct((T, 2 * CONV_WIDTH), BF16),
        compiler_params=_params(1),
    )(z, z, z, z, conv_w, sinks)


def _mix_core_bwd(z, dy, tab, conv_w, sinks, name):
    T = z.shape[0]
    nsub = MIX_BLOCKS
    rows = nsub * BLOCK
    steps = T // rows
    last = slice(rows - BLOCK, rows)
    cur = lambda n: jnp.minimum(n, steps - 1)
    prev_block = lambda n: jnp.maximum(nsub * cur(n) - 1, 0)
    prev_rows8 = lambda n: jnp.maximum((rows // 8) * cur(n) - 1, 0)
    next_rows8 = lambda n: jnp.minimum((rows // 8) * (cur(n) + 1), T // 8 - 1)

    def body(z_ref, zkvp_ref, cgp_ref, up_ref, bgn_ref, dy_ref, dyn_ref, tab_ref, tabp_ref, cw_ref, sink_ref,
             dz_ref, dcw_ref, dsk_ref, held_ref, kv_ref):
        n = pl.program_id(0)

        @pl.when(n == 0)
        def _():
            held_ref[...] = jnp.zeros_like(held_ref)
            kv_ref[...] = jnp.zeros_like(kv_ref)
            dcw_ref[...] = jnp.zeros_like(dcw_ref)
            dsk_ref[...] = jnp.zeros_like(dsk_ref)

        def emit_held():
            dz_ref[:, 0:Z_K] = held_ref[:, 0:Z_K]
            if nsub > 1:
                dz_ref[0:rows - BLOCK, Z_K:Z_END] = held_ref[0:rows - BLOCK, Z_K:Z_END]

        @pl.when(n < steps)
        def _():
            emit_held()
            w0, w1, w2 = cw_ref[0:1, :], cw_ref[1:2, :], cw_ref[2:3, :]
            dk_open, dv_open = kv_ref[:, 0:LANES], kv_ref[:, LANES:2 * LANES]
            for b in range(nsub):
                r0 = b * BLOCK
                blk = slice(r0, r0 + BLOCK)
                before = slice(r0 - BLOCK, r0)
                after8 = slice(r0 + BLOCK, r0 + BLOCK + 8)
                if b == 0:
                    has_prev = n > 0
                    kv_prev, cg_prev, u_prev, tab_p = zkvp_ref[...], cgp_ref[...], up_ref[...], tabp_ref[...]
                else:
                    has_prev = True
                    kv_prev, tab_p = z_ref[before, Z_K:Z_END], tab_ref[before, :]
                    cg_prev = z_ref[r0 - 8:r0, CONV_WIDTH:2 * CONV_WIDTH]
                    u_prev = z_ref[r0 - 8:r0, 2 * CONV_WIDTH:Z_Q]
                if b == nsub - 1:
                    dconv_next = jnp.where(n < steps - 1, dyn_ref[...] * bgn_ref[...], 0.0)
                else:
                    dconv_next = dy_ref[after8, 0:CONV_WIDTH] * z_ref[after8, 0:CONV_WIDTH]
                bg = z_ref[blk, 0:CONV_WIDTH]
                cg = z_ref[blk, CONV_WIDTH:2 * CONV_WIDTH]
                u = z_ref[blk, 2 * CONV_WIDTH:Z_Q]
                vv, vv1, vv2 = _conv_taps(cg, u, cg_prev, u_prev, has_prev)
                dyc = dy_ref[blk, 0:CONV_WIDTH]
                dbg = dyc * (w0 * vv2 + w1 * vv1 + w2 * vv)
                dconv = dyc * bg
                ext = jnp.concatenate([dconv, dconv_next], axis=0)
                ext_rows = ext.shape[0]
                dvv = (w2 * dconv + w1 * pltpu.roll(ext, ext_rows - 1, 0)[0:BLOCK]
                       + w0 * pltpu.roll(ext, ext_rows - 2, 0)[0:BLOCK])
                dcw_ref[0:1, :] += jnp.sum(dconv * vv2, axis=0, keepdims=True)
                dcw_ref[1:2, :] += jnp.sum(dconv * vv1, axis=0, keepdims=True)
                dcw_ref[2:3, :] += jnp.sum(dconv * vv, axis=0, keepdims=True)

                tab_c = tab_ref[blk, :]
                k_all = jnp.concatenate([kv_prev[:, 0:LANES], z_ref[blk, Z_K:Z_V]], axis=0)
                v_all = jnp.concatenate([kv_prev[:, LANES:2 * LANES], z_ref[blk, Z_V:Z_END]], axis=0)
                kp = _head_pads(k_all)
                vp = _head_pads(v_all)
                chunks = range(N_Q_HEADS // 2)
                q2 = [(z_ref[blk, Z_Q + LANES * c:Z_Q + LANES * (c + 1)] * ATTN_SCALE).astype(BF16) for c in chunks]
                do2 = [dy_ref[blk, CONV_WIDTH + LANES * c:CONV_WIDTH + LANES * (c + 1)].astype(BF16) for c in chunks]
                probs = _attn_probs_t(q2, kp, _window_mask_t(has_prev), sink_ref)
                dq_chunks = []
                dk_nat = jnp.zeros((2 * BLOCK, LANES), F32)
                dv_nat = jnp.zeros((2 * BLOCK, LANES), F32)
                for kv in range(2):
                    q_st = jnp.concatenate([q2[2 * kv], q2[2 * kv + 1]], axis=0)
                    do_st = jnp.concatenate([do2[2 * kv], do2[2 * kv + 1]], axis=0)
                    dq_t = jnp.zeros((LANES, 2 * BLOCK), F32)
                    dk_par, dv_par = [], []
                    for par in range(2):
                        g = 2 * kv + par
                        pr, psink = probs[g]
                        dp = _dot_nt(vp[(kv, par)], do_st)
                        delta = jnp.sum(dp * pr, axis=0, keepdims=True)
                        ds = (pr * (dp - delta)).astype(BF16)
                        dsink = -psink * delta
                        for r in range(2):
                            h = _group_head(g, r)
                            dsk_ref[h:h + 1, :] += jnp.sum(dsink[:, BLOCK * r:BLOCK * (r + 1)])
                        dq_t = dq_t + _dot_tn(kp[(kv, par)], ds)
                        dk_par.append(_dot(ds, q_st))
                        dv_par.append(_dot(pr.astype(BF16), do_st))
                    for r in range(2):
                        dq_chunks.append(_rot_t(dq_t[:, BLOCK * r:BLOCK * (r + 1)].T * ATTN_SCALE, tab_c))
                    dk_nat = dk_nat + _from_pads(dk_par[0], dk_par[1], kv)
                    dv_nat = dv_nat + _from_pads(dv_par[0], dv_par[1], kv)

                done_ref, done = (dz_ref, last) if b == 0 else (held_ref, before)
                done_ref[done, Z_K:Z_V] = _rot_t(dk_open + dk_nat[0:BLOCK], tab_p).astype(BF16)
                done_ref[done, Z_V:Z_END] = (dv_open + dv_nat[0:BLOCK]).astype(BF16)
                dk_open, dv_open = dk_nat[BLOCK:2 * BLOCK], dv_nat[BLOCK:2 * BLOCK]
                held_ref[blk, 0:CONV_WIDTH] = dbg.astype(BF16)
                held_ref[blk, CONV_WIDTH:2 * CONV_WIDTH] = (dvv * u).astype(BF16)
                held_ref[blk, 2 * CONV_WIDTH:Z_Q] = (dvv * cg).astype(BF16)
                for c in range(N_Q_HEADS // 2):
                    held_ref[blk, Z_Q + LANES * c:Z_Q + LANES * (c + 1)] = dq_chunks[c].astype(BF16)
            kv_ref[:, 0:LANES] = dk_open
            kv_ref[:, LANES:2 * LANES] = dv_open

        @pl.when(n == steps)
        def _():
            emit_held()
            dz_ref[last, Z_K:Z_V] = _rot_t(kv_ref[:, 0:LANES], tab_ref[last, :]).astype(BF16)
            dz_ref[last, Z_V:Z_END] = kv_ref[:, LANES:2 * LANES].astype(BF16)

    return _pcall(
        body, name=name, grid=(steps + 1,),
        in_specs=[pl.BlockSpec((rows, Z_END), lambda n: (cur(n), 0)),
                  pl.BlockSpec((BLOCK, 2 * LANES), lambda n: (prev_block(n), Z_K // (2 * LANES))),
                  pl.BlockSpec((8, CONV_WIDTH), lambda n: (prev_rows8(n), 1)),
                  pl.BlockSpec((8, CONV_WIDTH), lambda n: (prev_rows8(n), 2)),
                  pl.BlockSpec((8, CONV_WIDTH), lambda n: (next_rows8(n), 0)),
                  pl.BlockSpec((rows, 2 * CONV_WIDTH), lambda n: (cur(n), 0)),
                  pl.BlockSpec((8, CONV_WIDTH), lambda n: (next_rows8(n), 0)),
                  pl.BlockSpec((rows, 3 * LANES), lambda n: (cur(n), 0)),
                  pl.BlockSpec((BLOCK, 3 * LANES), lambda n: (prev_block(n), 0)),
                  pl.BlockSpec((3, CONV_WIDTH), lambda n: (0, 0)),
                  pl.BlockSpec(memory_space=pltpu.SMEM)],
        out_specs=[pl.BlockSpec((rows, Z_END), lambda n: (jnp.maximum(n - 1, 0), 0)),
                   pl.BlockSpec((8, CONV_WIDTH), lambda n: (0, 0)), pl.BlockSpec((8, LANES), lambda n: (0, 0))],
        out_shape=[jax.ShapeDtypeStruct((T, Z_END), BF16), jax.ShapeDtypeStruct((8, CONV_WIDTH), F32),
                   jax.ShapeDtypeStruct((8, LANES), F32)],
        scratch_shapes=[pltpu.VMEM((rows, Z_END), BF16), pltpu.VMEM((BLOCK, 2 * LANES), F32)],
        compiler_params=_params(1),
    )(z, z, z, z, z, dy, dy, tab, tab, conv_w, sinks)


def _local_sums(pair, chip, place, name):
    arrays, in_specs, out_specs, out_shape = [], [], [], []
    if pair is not None:
        g, sib = pair
        blk = (1, *sib.shape[1:])
        arrays += [g, sib]
        in_specs += [pl.BlockSpec(blk, lambda q, p: (q, p[1], 0)), pl.BlockSpec(blk, lambda q, p: (q, 0, 0))]
        out_specs.append(pl.BlockSpec(blk, lambda q, p: (q, 0, 0)))
        out_shape.append(jax.ShapeDtypeStruct(sib.shape, BF16))
    if chip is not None:
        g2, sib2, recv2 = chip
        blk = (1, *sib2.shape[1:])
        arrays += [g2, sib2, recv2]
        in_specs += [pl.BlockSpec(blk, lambda q, p: (p[0], p[1], 0)), pl.BlockSpec(blk, lambda q, p: (p[0], 0, 0)),
                     pl.BlockSpec(recv2.shape, lambda q, p: (0, 0, 0))]
        out_specs.append(pl.BlockSpec(sib2.shape[1:], lambda q, p: (p[1], 0)))
        out_shape.append(jax.ShapeDtypeStruct(g2.shape[1:], F32))

    def body(place_ref, *refs):
        refs = list(refs)
        ins, outs = refs[:len(arrays)], refs[len(arrays):]
        if pair is not None:
            g_ref, sib_ref = ins[:2]
            outs[0][...] = (g_ref[...] + sib_ref[...].astype(F32)).astype(BF16)
        if chip is not None:
            g_ref, sib_ref, recv_ref = ins[-3:]

            @pl.when(pl.program_id(0) == 0)
            def _():
                total = g_ref[0] + sib_ref[0].astype(F32)
                for j in range(3):
                    total = total + recv_ref[j].astype(F32)
                outs[-1][...] = total

    return _pcall(
        body, name=name,
        grid_spec=pltpu.PrefetchScalarGridSpec(num_scalar_prefetch=1, grid=(N_CHIPS,),
                                               in_specs=in_specs, out_specs=out_specs),
        out_shape=out_shape, compiler_params=_params(1),
    )(place, *arrays)


def _adamw_math(w, g, m, v):
    m = ADAM_B1 * m + (1.0 - ADAM_B1) * g
    v = ADAM_B2 * v + (1.0 - ADAM_B2) * (g * g)
    m_hat = m / (1.0 - ADAM_B1 ** ADAM_STEP)
    v_hat = v / (1.0 - ADAM_B2 ** ADAM_STEP)
    delta = -ADAM_LR * (m_hat / (jnp.sqrt(v_hat) + ADAM_EPS) + ADAM_WD * w)
    return delta, m, v


def _adamw(ws, gs, ms, vs, row_blocks, name):
    n = len(ws)

    def body(*refs):
        w, g, m, v = refs[:n], refs[n:2 * n], refs[2 * n:3 * n], refs[3 * n:4 * n]
        d, mo, vo, go = refs[4 * n:5 * n], refs[5 * n:6 * n], refs[6 * n:7 * n], refs[7 * n:]
        for t in range(n):
            gv = g[t][...]
            delta, m_new, v_new = _adamw_math(w[t][...], gv, m[t][...], v[t][...])
            d[t][...] = delta
            mo[t][...] = m_new
            vo[t][...] = v_new
            go[t][...] = gv

    specs = [pl.BlockSpec((a.shape[0] // row_blocks, a.shape[1]), lambda i: (i, 0)) for a in ws]
    shapes = [jax.ShapeDtypeStruct(a.shape, F32) for a in ws]
    return _pcall(
        body, name=name, grid=(row_blocks,), in_specs=specs * 4, out_specs=specs * 4, out_shape=shapes * 4,
        compiler_params=_params(1),
    )(*ws, *gs, *ms, *vs)


def kernel(x, ffn1_norm, ffn1_w_gate, ffn1_w_up, ffn1_w_down, mix_norm, w_in, conv_w, attn_sinks, w_out, ffn2_norm, ffn2_w_gate, ffn2_w_up, ffn2_w_down, final_norm, loss_target, m_ffn1_norm, m_ffn1_w_gate, m_ffn1_w_up, m_ffn1_w_down, m_mix_norm, m_w_in, m_conv_w, m_attn_sinks, m_w_out, m_ffn2_norm, m_ffn2_w_gate, m_ffn2_w_up, m_ffn2_w_down, m_final_norm, v_ffn1_norm, v_ffn1_w_gate, v_ffn1_w_up, v_ffn1_w_down, v_mix_norm, v_w_in, v_conv_w, v_attn_sinks, v_w_out, v_ffn2_norm, v_ffn2_w_gate, v_ffn2_w_up, v_ffn2_w_down, v_final_norm):
    T, D = x.shape[1], x.shape[2]
    chip = (2 * lax.axis_index("x") + lax.axis_index("y")).astype(jnp.int32)
    core = lax.axis_index("c").astype(jnp.int32)
    place = jnp.stack([chip, core])
    x0 = x[0]
    target = loss_target[0]
    gf = final_norm.reshape(1, D)

    tr = lambda w: jnp.swapaxes(w[0], 0, 1)
    big = [tr(ffn1_w_gate), tr(ffn1_w_up), ffn1_w_down[0], tr(w_in), w_out[0], tr(ffn2_w_gate), tr(ffn2_w_up), ffn2_w_down[0]]
    transposed = [True, True, False, True, False, True, True, False]
    own_b = [w.astype(BF16) for w in big]

    def whole(gathered, own):
        return lax.dynamic_update_slice(gathered, own[None], (chip, 0, 0)).reshape(-1, D)

    got1 = _run_comm(_gather_plan(own_b[0:3]), "gather_ffn1")
    wg1, wu1, wd1 = (whole(g, o) for g, o in zip(got1, own_b[0:3]))
    tab = _rope_tables(T)

    res = _ffn_fwd(x0, ffn1_norm, wg1, wu1, wd1, "ffn1_fwd", _gather_plan(own_b[3:8], [conv_w[0]]))
    x1, h1, gate1, up1, act1 = res[:5]
    win, wout, wg2, wu2, wd2 = (whole(g, o) for g, o in zip(res[5:10], own_b[3:8]))
    convw4 = lax.dynamic_update_slice(res[10], conv_w, (chip, 0, 0))
    convw = jnp.transpose(convw4, (1, 0, 2)).reshape(3, -1)
    z, hm = _norm_matmul(x1, mix_norm, win, tab, "mix_in_fwd")
    ymix = _mix_core_fwd(z, convw, attn_sinks, "mix_core_fwd")
    x2 = _matmul_residual(ymix, wout, x1, "mix_out_fwd")
    dx3, h2, gate2, up2, act2, dgf, loss_part = _ffn_fwd(x2, ffn2_norm, wg2, wu2, wd2, "ffn2_fwd", head=(gf, target))

    dx2, dyb2, dgate2, dup2, dg2 = _ffn_bwd(dx3, x2, ffn2_norm, gate2, up2, wg2, wu2, wd2, "ffn2_bwd")
    dymix, dx2b = _matmul_nt(dx2, wout, "mix_out_bwd")
    dz, dcw, dsk = _mix_core_bwd(z, dymix, tab, convw, attn_sinks, "mix_core_bwd")
    dx1, dgm = _matmul_norm_bwd(dz, win, x1, mix_norm, dx2, "mix_in_bwd")
    dx0, dyb1, dgate1, dup1, dg1 = _ffn_bwd(dx1, x0, ffn1_norm, gate1, up1, wg1, wu1, wd1, "ffn1_bwd")

    pad = lambda a: jnp.pad(a, ((0, 0), (0, LANES - a.shape[1])))
    vec = jnp.concatenate([dg1, dgm, dg2, dgf, dcw[0:3].reshape(1, -1), pad(dsk[:, 0].reshape(1, -1)),
                           pad(loss_part[:, 0:1])], axis=1)

    jobs = [("ffn2_dwg", dgate2, h2, 5), ("ffn2_dwu", dup2, h2, 6), ("ffn2_dwd", act2, dyb2, 7),
            ("ffn1_dwg", dgate1, h1, 0), ("ffn1_dwu", dup1, h1, 1), ("ffn1_dwd", act1, dyb1, 2),
            ("mix_dwin", dz, hm, 3), ("mix_dwout", ymix, dx2b, 4)]
    n_jobs = len(jobs)
    grad, grad_b, from_sib, pair_b, from_chips, half, g_big = ({} for _ in range(7))

    def stage_plans(t):
        plans, takers = [], []
        if 0 <= t - 1 < n_jobs:
            plans.append(_sibling_plan([grad_b[t - 1]]))
            takers.append((from_sib, t - 1))
        if 0 <= t - 2 < n_jobs:
            plans.append(_scatter_plan([pair_b[t - 2]]))
            takers.append((from_chips, t - 2))
        if 0 <= t - 3 < n_jobs:
            plans.append(_join_plan([half[t - 3]]))
            takers.append((g_big, jobs[t - 3][3]))
        return plans, takers

    def after_stage(t, landed, takers):
        for (store, key), arr in zip(takers, landed):
            store[key] = arr
        pair = (grad[t - 1], from_sib[t - 1]) if 0 <= t - 1 < n_jobs else None
        chip = (grad[t - 2], from_sib[t - 2], from_chips[t - 2]) if 0 <= t - 2 < n_jobs else None
        if pair or chip:
            sums = list(_local_sums(pair, chip, place, f"local_sums_{t}"))
            if pair:
                pair_b[t - 1] = sums.pop(0)
            if chip:
                half[t - 2] = sums.pop(0)

    for t, (name_, a, b, _) in enumerate(jobs):
        plans, takers = stage_plans(t)
        if t == 0:
            plans.append(_all_gather_plan(jnp.pad(vec, ((0, 7), (0, 0)))))
        res = _matmul_tn(a, b, DW_ROW_SPLIT, name_, _merge_plans(plans))
        grad[t], grad_b[t] = (r.reshape(N_CHIPS, -1, D) for r in res[:2])
        landed = list(res[2:])
        if t == 0:
            vec_blocks = landed.pop()
        after_stage(t, landed, takers)

    ws = big
    ms = [tr(m_ffn1_w_gate), tr(m_ffn1_w_up), m_ffn1_w_down[0], tr(m_w_in), m_w_out[0], tr(m_ffn2_w_gate), tr(m_ffn2_w_up), m_ffn2_w_down[0]]
    vs = [tr(v_ffn1_w_gate), tr(v_ffn1_w_up), v_ffn1_w_down[0], tr(v_w_in), v_w_out[0], tr(v_ffn2_w_gate), tr(v_ffn2_w_up), v_ffn2_w_down[0]]
    for t in range(n_jobs, n_jobs + 3):
        plans, takers = stage_plans(t)
        after_stage(t, _run_comm(_merge_plans(plans), f"grads_tail_{t - n_jobs}"), takers)
    upd = {}
    for name_, idx in (("adamw_a", [0, 1, 2, 4]), ("adamw_b", [3, 5, 6, 7])):
        k = len(idx)
        res = _adamw([ws[i] for i in idx], [g_big[i] for i in idx], [ms[i] for i in idx], [vs[i] for i in idx], ADAMW_ROW_BLOCKS, name_)
        for j, i in enumerate(idx):
            upd[i] = (res[j], res[k + j], res[2 * k + j])
            g_big[i] = res[3 * k + j]

    total = _sum_devices(vec_blocks, "small_sum")[0:1]
    g_n1, g_nm, g_n2, g_nf = (total[:, k * D:(k + 1) * D] for k in range(4))
    cw_full = total[:, 4 * D:4 * D + 3 * CONV_WIDTH].reshape(3, CONV_WIDTH)
    cq = CONV_WIDTH // N_CHIPS
    g_cw = lax.dynamic_slice(cw_full, (0, chip * cq), (3, cq))
    off = 4 * D + 3 * CONV_WIDTH
    g_sk = total[:, off:off + N_Q_HEADS]
    loss = total[0, off + LANES]

    sw = [ffn1_norm, mix_norm, conv_w[0], attn_sinks, ffn2_norm, gf]
    sg = [g_n1, g_nm, g_cw, g_sk, g_n2, g_nf]
    sm = [m_ffn1_norm, m_mix_norm, m_conv_w[0], m_attn_sinks, m_ffn2_norm, m_final_norm.reshape(1, D)]
    sv = [v_ffn1_norm, v_mix_norm, v_conv_w[0], v_attn_sinks, v_ffn2_norm, v_final_norm.reshape(1, D)]
    sres = _adamw(sw, sg, sm, sv, 1, "adamw_small")
    supd = [(sres[j], sres[6 + j], sres[12 + j]) for j in range(6)]

    order = [("s", 0), ("b", 0), ("b", 1), ("b", 2), ("s", 1), ("b", 3), ("s", 2), ("s", 3), ("b", 4),
             ("s", 4), ("b", 5), ("b", 6), ("b", 7), ("s", 5)]

    def leaf(kind, i, which):
        if kind == "b":
            a = g_big[i] if which == 0 else upd[i][which - 1]
            return (jnp.swapaxes(a, 0, 1) if transposed[i] else a)[None]
        a = sg[i] if which == 0 else supd[i][which - 1]
        if i == 2:
            return a[None]
        if i == 5:
            return a.reshape(D)
        return a

    outs = [loss, dx0[None]]
    for which in range(4):
        outs += [leaf(kind, i, which) for kind, i in order]
    return tuple(outs)
```

```python
import jax
import jax.numpy as jnp
import numpy as np
from jax import lax
from jax.experimental import pallas as pl
from jax.experimental.pallas import tpu as pltpu

F32 = jnp.float32
BF16 = jnp.bfloat16
MESH = pl.DeviceIdType.MESH

CONV_WIDTH = 512
N_Q_HEADS = 8
HEAD_DIM = 64
BLOCK = 128
ROPE_THETA = 500000.0
ROT_DIM = 16
RMS_EPS = 1e-5
MASK_VALUE = -1e30
ATTN_SCALE = HEAD_DIM ** -0.5
FFN_RES_SCALE = 0.5
ADAM_LR = 0.001
ADAM_B1 = 0.9
ADAM_B2 = 0.999
ADAM_EPS = 1e-08
ADAM_WD = 0.01
ADAM_STEP = 10

N_CHIPS = 4
N_DEV = 8
LANES = 128
VMEM_LIMIT = 56 * 1024 * 1024

_pcall = pl.pallas_call
HBM_SPEC = pl.BlockSpec(memory_space=pltpu.HBM)
ANY_SPEC = pl.BlockSpec(memory_space=pl.ANY)


def _params(n_axes, vmem=VMEM_LIMIT):
    return pltpu.CompilerParams(dimension_semantics=("arbitrary",) * n_axes, vmem_limit_bytes=vmem)


def _dot(a, b):
    return jnp.dot(a, b, preferred_element_type=F32)


def _dot_nt(a, b):
    return lax.dot_general(a, b, (((1,), (1,)), ((), ())), preferred_element_type=F32)


def _dot_tn(a, b):
    return lax.dot_general(a, b, (((0,), (0,)), ((), ())), preferred_element_type=F32)


def _rms_inv(x):
    return lax.rsqrt(jnp.mean(x * x, axis=-1, keepdims=True) + RMS_EPS)


def _norm_bwd(dh, x, g):
    inv = _rms_inv(x)
    xhat = x * inv
    dg = jnp.sum(dh * xhat, axis=0, keepdims=True)
    dxhat = dh * g
    dx = inv * (dxhat - xhat * jnp.mean(dxhat * xhat, axis=-1, keepdims=True))
    return dx, dg


def _place():
    x, y, c = lax.axis_index("x"), lax.axis_index("y"), lax.axis_index("c")
    chips = [(1 - x, y), (x, 1 - y), (1 - x, 1 - y)]
    return x, y, c, chips


class _Plan:
    def __init__(self, arrays, out_shapes, n_sems, start, finish, middle=None, aliases=None):
        self.arrays, self.out_shapes, self.n_sems = list(arrays), list(out_shapes), n_sems
        self.start, self.finish, self.middle = start, finish, middle
        self.aliases = dict(aliases or {})

    def specs(self):
        k = len(self.arrays)
        sems = [pltpu.SemaphoreType.DMA((self.n_sems,)), pltpu.SemaphoreType.DMA((self.n_sems,))]
        return [HBM_SPEC] * k, [HBM_SPEC] * len(self.out_shapes), self.out_shapes, sems


class _SemSlice:
    def __init__(self, ref, offset):
        self.ref, self.offset = ref, offset

    @property
    def at(self):
        return self

    def __getitem__(self, k):
        return self.ref.at[k + self.offset]


def _merge_plans(plans):
    plans = [p for p in plans if p is not None]
    if len(plans) <= 1:
        return plans[0] if plans else None
    arrays, shapes, aliases, spans, n_sems = [], [], {}, [], 0
    for p in plans:
        a0, o0 = len(arrays), len(shapes)
        spans.append((a0, a0 + len(p.arrays), o0, o0 + len(p.out_shapes), n_sems))
        aliases.update({a0 + i: o0 + j for i, j in p.aliases.items()})
        arrays += p.arrays
        shapes += p.out_shapes
        n_sems += p.n_sems

    def run(which):
        def fn(ins, outs, send_sems, recv_sems):
            for p, (a0, a1, o0, o1, s0) in zip(plans, spans):
                part = getattr(p, which)
                if part is not None:
                    part(ins[a0:a1], outs[o0:o1], _SemSlice(send_sems, s0), _SemSlice(recv_sems, s0))
        return fn

    middle = run("middle") if any(p.middle is not None for p in plans) else None
    return _Plan(arrays, shapes, n_sems, run("start"), run("finish"), middle, aliases)


def _sibling_plan(grads_b):
    n = len(grads_b)

    def copies(ins, outs, send_sems, recv_sems):
        x, y, c, _ = _place()

        def copy(t):
            half = ins[t].shape[1] // 2
            return pltpu.make_async_remote_copy(
                src_ref=ins[t].at[:, pl.ds(pl.multiple_of((1 - c) * half, 16), half), :], dst_ref=outs[t],
                send_sem=send_sems.at[t], recv_sem=recv_sems.at[t], device_id=(x, y, 1 - c), device_id_type=MESH)

        return [copy(t) for t in range(n)]

    def start(*refs):
        for cp in copies(*refs):
            cp.start()

    def finish(*refs):
        for cp in copies(*refs):
            cp.wait()

    shapes = [jax.ShapeDtypeStruct((g.shape[0], g.shape[1] // 2, g.shape[2]), g.dtype) for g in grads_b]
    return _Plan(grads_b, shapes, n, start, finish)


def _scatter_plan(parts_b):
    n = len(parts_b)

    def copies(ins, outs, send_sems, recv_sems):
        x, y, c, chips = _place()

        def copy(t, j):
            px, py = chips[j]
            return pltpu.make_async_remote_copy(
                src_ref=ins[t].at[2 * px + py], dst_ref=outs[t].at[j], send_sem=send_sems.at[3 * t + j],
                recv_sem=recv_sems.at[3 * t + j], device_id=(px, py, c), device_id_type=MESH)

        return [copy(t, j) for t in range(n) for j in range(3)]

    def start(*refs):
        for cp in copies(*refs):
            cp.start()

    def finish(*refs):
        for cp in copies(*refs):
            cp.wait()

    shapes = [jax.ShapeDtypeStruct((3, *p.shape[1:]), p.dtype) for p in parts_b]
    return _Plan(parts_b, shapes, 3 * n, start, finish)


def _gather_plan(shards, small=()):
    n, ns = len(shards), len(small)
    per = 8

    def parts(ins, outs, send_sems, recv_sems):
        x, y, c, chips = _place()
        me = 2 * x + y
        blocks = [2 * px + py for px, py in chips]

        def rows(t, core, piece=None):
            half = ins[t].shape[0] // 2
            if piece is None:
                return pl.ds(pl.multiple_of(core * half, 16), half)
            return pl.ds(pl.multiple_of(core * half + piece * (half // 2), 16), half // 2)

        def remote(src, dst, k, device):
            return pltpu.make_async_remote_copy(src_ref=src, dst_ref=dst, send_sem=send_sems.at[k],
                                                recv_sem=recv_sems.at[k], device_id=device, device_id_type=MESH)

        def first(t, j, block, core):
            return remote(ins[t].at[rows(t, core), :], outs[t].at[block, rows(t, core), :], per * t + j, (*chips[j], c))

        def relay(t, j, block, core):
            ref = outs[t].at[block, rows(t, core, j), :]
            return remote(ref, ref, per * t + 2 + j, (*chips[j], c))

        def passed(t, k, block, core, piece=None):
            ref = outs[t].at[block, rows(t, core, piece), :]
            return remote(ref, ref, per * t + 4 + k, (x, y, 1 - c))

        def whole(s, j, block):
            return remote(ins[n + s], outs[n + s].at[block], per * n + 3 * s + j, (*chips[j], c))

        return c, me, blocks, first, relay, passed, whole

    def start(*refs):
        c, me, _, first, _, _, whole = parts(*refs)
        for t in range(n):
            for j in range(2):
                first(t, j, me, c).start()
        for s in range(ns):
            for j in range(3):
                whole(s, j, me).start()

    def middle(*refs):
        c, _, blocks, first, relay, passed, _ = parts(*refs)
        for t in range(n):
            for j in range(2):
                first(t, j, blocks[j], c).wait_recv()
                passed(t, j, blocks[j], c).start()
                relay(t, 1 - j, blocks[j], c).start()

    def finish(*refs):
        c, me, blocks, first, relay, passed, whole = parts(*refs)
        for t in range(n):
            for j in range(2):
                relay(t, j, blocks[2], c).wait_recv()
                passed(t, 2 + j, blocks[2], c, j).start()
        for t in range(n):
            for j in range(2):
                passed(t, j, blocks[j], 1 - c).wait_recv()
                passed(t, 2 + j, blocks[2], 1 - c, j).wait_recv()
        for s in range(ns):
            for j in range(3):
                whole(s, j, blocks[j]).wait_recv()
        for t in range(n):
            for j in range(2):
                first(t, j, me, c).wait_send()
                relay(t, 1 - j, blocks[j], c).wait_send()
                passed(t, j, blocks[j], c).wait_send()
                passed(t, 2 + j, blocks[2], c, j).wait_send()
        for s in range(ns):
            for j in range(3):
                whole(s, j, me).wait_send()

    arrays = [*shards, *small]
    shapes = [jax.ShapeDtypeStruct((N_CHIPS, *a.shape), a.dtype) for a in arrays]
    return _Plan(arrays, shapes, per * n + 3 * ns, start, finish, middle)


def _run_comm(plan, name):
    k = len(plan.arrays)
    in_specs, out_specs, out_shape, sems = plan.specs()

    def body(*refs):
        cr = (refs[:k], refs[k:k + len(out_shape)], refs[-2], refs[-1])
        plan.start(*cr)
        if plan.middle is not None:
            plan.middle(*cr)
        plan.finish(*cr)

    return _pcall(body, name=name, in_specs=in_specs, out_specs=out_specs, out_shape=out_shape,
                  input_output_aliases=plan.aliases, scratch_shapes=sems)(*plan.arrays)


def _carried(plan, in_specs, out_specs, out_shape, scratch):
    aliases = {}
    if plan is not None:
        p_in, p_out, p_shape, p_sems = plan.specs()
        aliases = {len(in_specs) + i: len(out_specs) + j for i, j in plan.aliases.items()}
        in_specs, out_specs = in_specs + p_in, out_specs + p_out
        out_shape, scratch = out_shape + p_shape, scratch + p_sems
    return dict(in_specs=in_specs, out_specs=out_specs, out_shape=out_shape, scratch_shapes=scratch,
                input_output_aliases=aliases)


def _unpack(refs, n_in, n_out, plan):
    k_in = len(plan.arrays) if plan else 0
    k_out = len(plan.out_shapes) if plan else 0
    ins = refs[:n_in]
    outs = refs[n_in + k_in:n_in + k_in + n_out]
    rest = refs[n_in + k_in + n_out + k_out:]
    if plan is None:
        return ins, outs, rest, None
    cr = (refs[n_in:n_in + k_in], refs[n_in + k_in + n_out:n_in + k_in + n_out + k_out], rest[-2], rest[-1])
    return ins, outs, rest[:-2], cr


def _hook(plan, cr, which, cond):
    fn = getattr(plan, which) if plan is not None else None
    if fn is not None:
        pl.when(cond)(lambda: fn(*cr))


def _join_plan(shards):
    n = len(shards)

    def copy(ins, outs, send_sems, recv_sems, t, core):
        x, y, c, _ = _place()
        half = ins[t].shape[0] // 2
        rows = pl.ds(pl.multiple_of(core * half, 8), half)
        return pltpu.make_async_remote_copy(
            src_ref=ins[t].at[rows, :], dst_ref=outs[t].at[rows, :], send_sem=send_sems.at[t],
            recv_sem=recv_sems.at[t], device_id=(x, y, 1 - c), device_id_type=MESH)

    def start(*refs):
        c = lax.axis_index("c")
        for t in range(n):
            copy(*refs, t, c).start()

    def finish(*refs):
        c = lax.axis_index("c")
        for t in range(n):
            copy(*refs, t, 1 - c).wait_recv()
        for t in range(n):
            copy(*refs, t, c).wait_send()

    shapes = [jax.ShapeDtypeStruct(s.shape, s.dtype) for s in shards]
    return _Plan(shards, shapes, n, start, finish, aliases={t: t for t in range(n)})


def _all_gather_plan(vec):
    def parts(ins, outs, send_sems, recv_sems):
        x, y, c, _ = _place()
        me = 4 * x + 2 * y + c
        rel = [((k >> 2) & 1, (k >> 1) & 1, k & 1) for k in range(1, N_DEV)]

        def peer(k):
            fx, fy, fc = rel[k]
            return (x ^ fx, y ^ fy, c ^ fc)

        def copy(k, dev):
            return pltpu.make_async_remote_copy(
                src_ref=ins[0], dst_ref=outs[0].at[dev], send_sem=send_sems.at[k], recv_sem=recv_sems.at[k],
                device_id=peer(k), device_id_type=MESH)

        mine = pltpu.make_async_copy(ins[0], outs[0].at[me], send_sems.at[N_DEV - 1])
        return me, peer, copy, mine

    def start(*refs):
        me, _, copy, mine = parts(*refs)
        mine.start()
        for k in range(N_DEV - 1):
            copy(k, me).start()

    def finish(*refs):
        me, peer, copy, mine = parts(*refs)
        for k in range(N_DEV - 1):
            px, py, pc = peer(k)
            copy(k, 4 * px + 2 * py + pc).wait_recv()
        for k in range(N_DEV - 1):
            copy(k, me).wait_send()
        mine.wait()

    return _Plan([vec], [jax.ShapeDtypeStruct((N_DEV, *vec.shape), vec.dtype)], N_DEV, start, finish)


def _sum_devices(blocks, name):
    def body(b_ref, o_ref):
        total = b_ref[0]
        for dev in range(1, N_DEV):
            total = total + b_ref[dev]
        o_ref[...] = total

    return _pcall(body, name=name, in_specs=[pl.BlockSpec(memory_space=pltpu.VMEM)],
                  out_specs=pl.BlockSpec(memory_space=pltpu.VMEM),
                  out_shape=jax.ShapeDtypeStruct(blocks.shape[1:], F32))(blocks)


TOKEN_TILE = 512
PROJ_TOKEN_TILE = 1024
OUT_PROJ_TOKEN_TILE = 2048
ADAMW_ROW_BLOCKS = 4
BWD_VMEM_LIMIT = 62 * 1024 * 1024
DW_TOKEN_TILE = 2048
DW_ROW_SPLIT = 2
MXU_COLS = 256
DH_GROUP = 6


def _chunks(n):
    out, c0 = [], 0
    while c0 < n:
        size = min(MXU_COLS, n - c0)
        out.append((c0, size))
        c0 += size
    return out


def _load_weights(hbm_refs, vmem_refs, sems):
    copies = [pltpu.make_async_copy(h, v, sems.at[k]) for k, (h, v) in enumerate(zip(hbm_refs, vmem_refs))]
    for cp in copies:
        cp.start()
    for cp in copies:
        cp.wait()


def _ffn_fwd(x, g, wgt, wut, wd, name, plan=None, head=None):
    T, D = x.shape
    F = wgt.shape[0]
    tm = min(T, TOKEN_TILE)
    ni = T // tm
    n_head = 2 if head is not None else 0

    def body(*refs):
        ins, outs, scratch, cr = _unpack(refs, 5 + n_head, 5 + n_head, plan)
        x_ref, g_ref, wg_hbm, wu_hbm, wd_hbm = ins[:5]
        xo_ref, h_ref, gate_ref, up_ref, act_ref = outs[:5]
        wg_ref, wu_ref, wd_ref, sems = scratch
        i = pl.program_id(0)
        _hook(plan, cr, "start", i == 0)

        @pl.when(i == 0)
        def _():
            _load_weights((wg_hbm, wu_hbm, wd_hbm), (wg_ref, wu_ref, wd_ref), sems)

        xv = x_ref[...]
        h = ((xv * _rms_inv(xv)) * g_ref[...]).astype(BF16)
        h_ref[...] = h
        for c0, size in _chunks(F):
            gate = _dot_nt(h, wg_ref[c0:c0 + size, :])
            up = _dot_nt(h, wu_ref[c0:c0 + size, :])
            gate_ref[:, c0:c0 + size] = gate.astype(BF16)
            up_ref[:, c0:c0 + size] = up.astype(BF16)
            act_ref[:, c0:c0 + size] = (gate * jax.nn.sigmoid(gate) * up).astype(BF16)
        y = x_ref[...] + FFN_RES_SCALE * _dot(act_ref[...], wd_ref[...])
        if head is None:
            xo_ref[...] = y
        else:
            gf_ref, t_ref = ins[5:]
            dgf_ref, loss_ref = outs[5:]

            @pl.when(i == 0)
            def _():
                dgf_ref[...] = jnp.zeros_like(dgf_ref)
                loss_ref[...] = jnp.zeros_like(loss_ref)

            gf = gf_ref[...]
            diff = (y * _rms_inv(y)) * gf - t_ref[...]
            loss_ref[...] += 0.5 * jnp.sum(jnp.mean(diff * diff, axis=-1, keepdims=True))
            dy, dgf = _norm_bwd(diff * (1.0 / D), y, gf)
            xo_ref[...] = dy
            dgf_ref[...] += dgf
        _hook(plan, cr, "middle", i == ni // 2)
        _hook(plan, cr, "finish", i == ni - 1)

    const = lambda shape: pl.BlockSpec(shape, lambda i: (0, 0))
    rows = lambda width: pl.BlockSpec((tm, width), lambda i: (i, 0))
    in_specs = [rows(D), const((1, D)), ANY_SPEC, ANY_SPEC, ANY_SPEC]
    out_specs = [rows(D), rows(D), rows(F), rows(F), rows(F)]
    out_shape = [jax.ShapeDtypeStruct((T, D), F32), jax.ShapeDtypeStruct((T, D), BF16),
                 jax.ShapeDtypeStruct((T, F), BF16), jax.ShapeDtypeStruct((T, F), BF16), jax.ShapeDtypeStruct((T, F), BF16)]
    if head is not None:
        in_specs += [const((1, D)), rows(D)]
        out_specs += [const((1, D)), const((1, LANES))]
        out_shape += [jax.ShapeDtypeStruct((1, D), F32), jax.ShapeDtypeStruct((1, LANES), F32)]
    io = _carried(plan, in_specs, out_specs, out_shape,
                  [pltpu.VMEM((F, D), BF16), pltpu.VMEM((F, D), BF16), pltpu.VMEM((F, D), BF16),
                   pltpu.SemaphoreType.DMA((3,))])
    return _pcall(
        body, name=name, grid=(ni,), compiler_params=_params(1), **io,
    )(x, g, wgt, wut, wd, *(head or ()), *(plan.arrays if plan else ()))


def _ffn_bwd(dy, x, g, gate, up, wgt, wut, wd, name):
    T, D = x.shape
    F = wgt.shape[0]
    tm = min(T, TOKEN_TILE)
    ni = T // tm

    def body(dy_ref, x_ref, g_ref, gate_ref, up_ref, wg_hbm, wu_hbm, wd_hbm,
             dx_ref, dyb_ref, dgate_ref, dup_ref, dg_ref, wg_ref, wu_ref, wd_ref, sems):
        @pl.when(pl.program_id(0) == 0)
        def _():
            _load_weights((wg_hbm, wu_hbm, wd_hbm), (wg_ref, wu_ref, wd_ref), sems)
            dg_ref[...] = jnp.zeros_like(dg_ref)

        dyb = (FFN_RES_SCALE * dy_ref[...]).astype(BF16)
        dyb_ref[...] = dyb
        dh, group_g, group_u, row0 = None, [], [], 0
        chunks = _chunks(F)
        for k, (c0, size) in enumerate(chunks):
            dact = _dot_nt(dyb, wd_ref[c0:c0 + size, :])
            gt = gate_ref[:, c0:c0 + size].astype(F32)
            u = up_ref[:, c0:c0 + size].astype(F32)
            sig = jax.nn.sigmoid(gt)
            dup = (dact * (gt * sig)).astype(BF16)
            dgate = (dact * u * (sig * (1.0 + gt * (1.0 - sig)))).astype(BF16)
            dup_ref[:, c0:c0 + size] = dup
            dgate_ref[:, c0:c0 + size] = dgate
            group_g.append(dgate)
            group_u.append(dup)
            if len(group_g) == DH_GROUP or k == len(chunks) - 1:
                rows = slice(row0, c0 + size)
                part = (_dot(jnp.concatenate(group_g, axis=1), wg_ref[rows, :])
                        + _dot(jnp.concatenate(group_u, axis=1), wu_ref[rows, :]))
                dh = part if dh is None else dh + part
                group_g, group_u, row0 = [], [], c0 + size
        dxn, dg = _norm_bwd(dh, x_ref[...], g_ref[...])
        dx_ref[...] = dy_ref[...] + dxn
        dg_ref[...] += dg

    return _pcall(
        body, name=name, grid=(ni,),
        in_specs=[pl.BlockSpec((tm, D), lambda i: (i, 0)), pl.BlockSpec((tm, D), lambda i: (i, 0)),
                  pl.BlockSpec((1, D), lambda i: (0, 0)),
                  pl.BlockSpec((tm, F), lambda i: (i, 0)), pl.BlockSpec((tm, F), lambda i: (i, 0)),
                  ANY_SPEC, ANY_SPEC, ANY_SPEC],
        out_specs=[pl.BlockSpec((tm, D), lambda i: (i, 0)), pl.BlockSpec((tm, D), lambda i: (i, 0)),
                   pl.BlockSpec((tm, F), lambda i: (i, 0)), pl.BlockSpec((tm, F), lambda i: (i, 0)),
                   pl.BlockSpec((1, D), lambda i: (0, 0))],
        out_shape=[jax.ShapeDtypeStruct((T, D), F32), jax.ShapeDtypeStruct((T, D), BF16),
                   jax.ShapeDtypeStruct((T, F), BF16), jax.ShapeDtypeStruct((T, F), BF16),
                   jax.ShapeDtypeStruct((1, D), F32)],
        scratch_shapes=[pltpu.VMEM((F, D), BF16), pltpu.VMEM((F, D), BF16), pltpu.VMEM((F, D), BF16),
                        pltpu.SemaphoreType.DMA((3,))],
        compiler_params=_params(1, BWD_VMEM_LIMIT),
    )(dy, x, g, gate, up, wgt, wut, wd)


def _matmul_tn(a, b, row_split, name, plan=None):
    T, n1 = a.shape
    n2 = b.shape[1]
    tn = n1 // row_split
    tk = min(T, DW_TOKEN_TILE)
    nk = T // tk

    def body(*refs):
        (a_ref, b_ref), (o_ref, ob_ref), _, cr = _unpack(refs, 2, 2, plan)
        j = pl.program_id(0)
        k = pl.program_id(1)
        _hook(plan, cr, "start", jnp.logical_and(j == 0, k == 0))

        @pl.when(k == 0)
        def _():
            o_ref[...] = jnp.zeros_like(o_ref)

        o_ref[...] += _dot_tn(a_ref[...], b_ref[...])

        @pl.when(k == nk - 1)
        def _():
            ob_ref[...] = o_ref[...].astype(BF16)

        _hook(plan, cr, "finish", jnp.logical_and(j == row_split - 1, k == nk - 1))

    io = _carried(
        plan,
        [pl.BlockSpec((tk, tn), lambda j, k: (k, j)), pl.BlockSpec((tk, n2), lambda j, k: (k, 0))],
        [pl.BlockSpec((tn, n2), lambda j, k: (j, 0)), pl.BlockSpec((tn, n2), lambda j, k: (j, 0))],
        [jax.ShapeDtypeStruct((n1, n2), F32), jax.ShapeDtypeStruct((n1, n2), BF16)], [])
    return _pcall(
        body, name=name, grid=(row_split, nk), compiler_params=_params(2), **io,
    )(a, b, *(plan.arrays if plan else ()))


def _norm_matmul(x, g, wt, tab, name):
    T, D = x.shape
    n = wt.shape[0]
    tm = min(T, PROJ_TOKEN_TILE)

    def body(x_ref, g_ref, w_ref, tab_ref, z_ref, h_ref):
        xv = x_ref[...]
        h = ((xv * _rms_inv(xv)) * g_ref[...]).astype(BF16)
        h_ref[...] = h
        z = _dot_nt(h, w_ref[...])
        z_ref[:, 0:Z_Q] = z[:, 0:Z_Q]
        tab_v = tab_ref[...]
        for c0 in range(Z_Q, Z_V, LANES):
            z_ref[:, c0:c0 + LANES] = _rot(z[:, c0:c0 + LANES], tab_v)
        z_ref[:, Z_V:Z_END] = z[:, Z_V:Z_END]

    return _pcall(
        body, name=name, grid=(T // tm,),
        in_specs=[pl.BlockSpec((tm, D), lambda i: (i, 0)), pl.BlockSpec((1, D), lambda i: (0, 0)),
                  pl.BlockSpec((n, D), lambda i: (0, 0)), pl.BlockSpec((tm, 3 * LANES), lambda i: (i, 0))],
        out_specs=[pl.BlockSpec((tm, n), lambda i: (i, 0)), pl.BlockSpec((tm, D), lambda i: (i, 0))],
        out_shape=[jax.ShapeDtypeStruct((T, n), F32), jax.ShapeDtypeStruct((T, D), BF16)],
        compiler_params=_params(1),
    )(x, g, wt, tab)


def _matmul_residual(y, w, x, name):
    T, D = x.shape
    kdim = y.shape[1]
    tm = min(T, OUT_PROJ_TOKEN_TILE)

    def body(y_ref, w_ref, x_ref, o_ref):
        o_ref[...] = x_ref[...] + _dot(y_ref[...], w_ref[...])

    return _pcall(
        body, name=name, grid=(T // tm,),
        in_specs=[pl.BlockSpec((tm, kdim), lambda i: (i, 0)), pl.BlockSpec((kdim, D), lambda i: (0, 0)),
                  pl.BlockSpec((tm, D), lambda i: (i, 0))],
        out_specs=pl.BlockSpec((tm, D), lambda i: (i, 0)),
        out_shape=jax.ShapeDtypeStruct((T, D), F32),
        compiler_params=_params(1),
    )(y, w, x)


def _matmul_nt(dx, w, name):
    T, D = dx.shape
    kdim = w.shape[0]
    tm = min(T, OUT_PROJ_TOKEN_TILE)

    def body(dx_ref, w_ref, dy_ref, dxb_ref):
        dxb = dx_ref[...].astype(BF16)
        dxb_ref[...] = dxb
        dy_ref[...] = _dot_nt(dxb, w_ref[...])

    return _pcall(
        body, name=name, grid=(T // tm,),
        in_specs=[pl.BlockSpec((tm, D), lambda i: (i, 0)), pl.BlockSpec((kdim, D), lambda i: (0, 0))],
        out_specs=[pl.BlockSpec((tm, kdim), lambda i: (i, 0)), pl.BlockSpec((tm, D), lambda i: (i, 0))],
        out_shape=[jax.ShapeDtypeStruct((T, kdim), F32), jax.ShapeDtypeStruct((T, D), BF16)],
        compiler_params=_params(1),
    )(dx, w)


def _matmul_norm_bwd(dz, wt, x, g, dres, name):
    T, D = x.shape
    n = dz.shape[1]
    tm = min(T, PROJ_TOKEN_TILE)

    def body(dz_ref, w_ref, x_ref, g_ref, dres_ref, dx_ref, dg_ref):
        @pl.when(pl.program_id(0) == 0)
        def _():
            dg_ref[...] = jnp.zeros_like(dg_ref)

        dh = _dot(dz_ref[...], w_ref[...])
        dxn, dg = _norm_bwd(dh, x_ref[...], g_ref[...])
        dx_ref[...] = dres_ref[...] + dxn
        dg_ref[...] += dg

    return _pcall(
        body, name=name, grid=(T // tm,),
        in_specs=[pl.BlockSpec((tm, n), lambda i: (i, 0)), pl.BlockSpec((n, D), lambda i: (0, 0)),
                  pl.BlockSpec((tm, D), lambda i: (i, 0)), pl.BlockSpec((1, D), lambda i: (0, 0)),
                  pl.BlockSpec((tm, D), lambda i: (i, 0))],
        out_specs=[pl.BlockSpec((tm, D), lambda i: (i, 0)), pl.BlockSpec((1, D), lambda i: (0, 0))],
        out_shape=[jax.ShapeDtypeStruct((T, D), F32), jax.ShapeDtypeStruct((1, D), F32)],
        compiler_params=_params(1),
    )(dz, wt, x, g, dres)


Z_Q = 3 * CONV_WIDTH
Z_K = Z_Q + N_Q_HEADS * HEAD_DIM
Z_V = Z_K + LANES
Z_END = Z_V + LANES


def _rope_tables(T, name, plan):
    half = ROT_DIM // 2
    inv_freq = ROPE_THETA ** (-jnp.arange(0, ROT_DIM, 2, dtype=F32) / ROT_DIM)
    ang = inv_freq[:, None] * jnp.arange(T, dtype=F32)[None, :]
    cos_sin = jnp.concatenate([jnp.cos(ang), jnp.sin(ang)], axis=0)
    select = np.zeros((2 * half, 3 * LANES), np.float32)
    const = np.zeros((1, 3 * LANES), np.float32)
    for lane in range(LANES):
        d = lane % HEAD_DIM
        if d < half:
            select[d, lane] = 1.0
            select[half + d, LANES + lane] = -1.0
        elif d < ROT_DIM:
            select[d - half, lane] = 1.0
            select[d, 2 * LANES + lane] = 1.0
        else:
            const[0, lane] = 1.0
    tm = PROJ_TOKEN_TILE
    ni = T // tm

    def body(*refs):
        (cs_ref, sel_ref, const_ref), (tab_ref,), _, cr = _unpack(refs, 3, 1, plan)
        i = pl.program_id(0)
        _hook(plan, cr, "start", i == 0)
        tab_ref[...] = lax.dot_general(cs_ref[...], sel_ref[...], (((0,), (0,)), ((), ())),
                                       precision=lax.Precision.HIGHEST, preferred_element_type=F32) + const_ref[...]
        _hook(plan, cr, "middle", i == ni - 1)
        _hook(plan, cr, "finish", i == ni - 1)

    io = _carried(
        plan,
        [pl.BlockSpec((2 * half, tm), lambda i: (0, i)), pl.BlockSpec((2 * half, 3 * LANES), lambda i: (0, 0)),
         pl.BlockSpec((1, 3 * LANES), lambda i: (0, 0))],
        [pl.BlockSpec((tm, 3 * LANES), lambda i: (i, 0))], [jax.ShapeDtypeStruct((T, 3 * LANES), F32)], [])
    res = _pcall(body, name=name, grid=(ni,), compiler_params=_params(1), **io)(
        cos_sin, jnp.asarray(select), jnp.asarray(const), *plan.arrays)
    return res[0], res[1:]


def _tab3(tab):
    return tab[:, 0:LANES], tab[:, LANES:2 * LANES], tab[:, 2 * LANES:3 * LANES]


def _rot(x, tab):
    c, s1, s2 = _tab3(tab)
    return x * c + pltpu.roll(x, LANES - ROT_DIM // 2, 1) * s1 + pltpu.roll(x, ROT_DIM // 2, 1) * s2


def _rot_t(d, tab):
    c, s1, s2 = _tab3(tab)
    return d * c + pltpu.roll(d * s1, ROT_DIM // 2, 1) + pltpu.roll(d * s2, LANES - ROT_DIM // 2, 1)


def _head_pads(a):
    lo = lax.broadcasted_iota(jnp.int32, a.shape, 1) < HEAD_DIM
    nat0 = jnp.where(lo, a, 0.0)
    nat1 = jnp.where(lo, 0.0, a)
    return {
        (0, 0): nat0.astype(BF16), (0, 1): pltpu.roll(nat0, HEAD_DIM, 1).astype(BF16),
        (1, 0): pltpu.roll(nat1, HEAD_DIM, 1).astype(BF16), (1, 1): nat1.astype(BF16),
    }


def _from_pads(even, odd, kv):
    lo = lax.broadcasted_iota(jnp.int32, even.shape, 1) < HEAD_DIM
    if kv == 0:
        return jnp.where(lo, even + pltpu.roll(odd, HEAD_DIM, 1), 0.0)
    return jnp.where(lo, 0.0, pltpu.roll(even, HEAD_DIM, 1) + odd)


N_GROUPS = 4


def _group_head(g, r):
    kv, par = divmod(g, 2)
    return 2 * (2 * kv + r) + par


def _window_mask_t(has_prev):
    jj = lax.broadcasted_iota(jnp.int32, (2 * BLOCK, 2 * BLOCK), 0)
    ii = lax.broadcasted_iota(jnp.int32, (2 * BLOCK, 2 * BLOCK), 1) & (BLOCK - 1)
    rel = jj - BLOCK - ii
    return (rel <= 0) & (rel > -BLOCK) & ((jj >= BLOCK) | has_prev)


def _sink_row(sink_ref, g):
    lane = lax.broadcasted_iota(jnp.int32, (1, 2 * BLOCK), 1)
    return jnp.where(lane < BLOCK, sink_ref[0, _group_head(g, 0)], sink_ref[0, _group_head(g, 1)])


def _attn_probs_t(q2, kp, mask, sink_ref):
    out = []
    for kv in range(2):
        q_st = jnp.concatenate([q2[2 * kv], q2[2 * kv + 1]], axis=0)
        for par in range(2):
            s = jnp.where(mask, _dot_nt(kp[(kv, par)], q_st), MASK_VALUE)
            sink = _sink_row(sink_ref, 2 * kv + par)
            m = jnp.maximum(jnp.max(s, axis=0, keepdims=True), sink)
            p = jnp.exp(s - m)
            esink = jnp.exp(sink - m)
            rden = 1.0 / (jnp.sum(p, axis=0, keepdims=True) + esink)
            out.append((p * rden, esink * rden))
    return out


def _conv_taps(cg, u, cg_prev, u_prev, has_prev):
    vv = cg * u
    halo = jnp.where(has_prev, cg_prev * u_prev, 0.0)
    ext = jnp.concatenate([halo, vv], axis=0)
    rows = ext.shape[0]
    vv1 = pltpu.roll(ext, 1, 0)[8:rows]
    vv2 = pltpu.roll(ext, 2, 0)[8:rows]
    return vv, vv1, vv2


MIX_BLOCKS = 4


def _mix_core_fwd(z, conv_w, sinks, name):
    T = z.shape[0]
    rows = MIX_BLOCKS * BLOCK
    steps = T // rows
    prev_block = lambda n: jnp.maximum(MIX_BLOCKS * n - 1, 0)
    prev_rows8 = lambda n: jnp.maximum((rows // 8) * n - 1, 0)

    def body(z_ref, zkvp_ref, cgp_ref, up_ref, cw_ref, sink_ref, y_ref):
        for b in range(MIX_BLOCKS):
            r0 = b * BLOCK
            blk = slice(r0, r0 + BLOCK)
            if b == 0:
                has_prev = pl.program_id(0) > 0
                kv_prev, cg_prev, u_prev = zkvp_ref[...], cgp_ref[...], up_ref[...]
            else:
                has_prev = True
                kv_prev = z_ref[r0 - BLOCK:r0, Z_K:Z_END]
                cg_prev = z_ref[r0 - 8:r0, CONV_WIDTH:2 * CONV_WIDTH]
                u_prev = z_ref[r0 - 8:r0, 2 * CONV_WIDTH:Z_Q]
            bg = z_ref[blk, 0:CONV_WIDTH]
            vv, vv1, vv2 = _conv_taps(z_ref[blk, CONV_WIDTH:2 * CONV_WIDTH], z_ref[blk, 2 * CONV_WIDTH:Z_Q],
                                      cg_prev, u_prev, has_prev)
            conv = cw_ref[0:1, :] * vv2 + cw_ref[1:2, :] * vv1 + cw_ref[2:3, :] * vv
            y_ref[blk, 0:CONV_WIDTH] = (bg * conv).astype(BF16)

            k_all = jnp.concatenate([kv_prev[:, 0:LANES], z_ref[blk, Z_K:Z_V]], axis=0)
            v_all = jnp.concatenate([kv_prev[:, LANES:2 * LANES], z_ref[blk, Z_V:Z_END]], axis=0)
            kp = _head_pads(k_all)
            vp = _head_pads(v_all)
            q2 = [(z_ref[blk, Z_Q + LANES * c:Z_Q + LANES * (c + 1)] * ATTN_SCALE).astype(BF16)
                  for c in range(N_Q_HEADS // 2)]
            probs = _attn_probs_t(q2, kp, _window_mask_t(has_prev), sink_ref)
            for kv in range(2):
                o_t = (_dot_tn(vp[(kv, 0)], probs[2 * kv][0].astype(BF16))
                       + _dot_tn(vp[(kv, 1)], probs[2 * kv + 1][0].astype(BF16)))
                for r in range(2):
                    c = 2 * kv + r
                    y_ref[blk, CONV_WIDTH + LANES * c:CONV_WIDTH + LANES * (c + 1)] = (
                        o_t[:, BLOCK * r:BLOCK * (r + 1)].T.astype(BF16))

    return _pcall(
        body, name=name, grid=(steps,),
        in_specs=[pl.BlockSpec((rows, Z_END), lambda n: (n, 0)),
                  pl.BlockSpec((BLOCK, 2 * LANES), lambda n: (prev_block(n), Z_K // (2 * LANES))),
                  pl.BlockSpec((8, CONV_WIDTH), lambda n: (prev_rows8(n), 1)),
                  pl.BlockSpec((8, CONV_WIDTH), lambda n: (prev_rows8(n), 2)),
                  pl.BlockSpec((3, CONV_WIDTH), lambda n: (0, 0)),
                  pl.BlockSpec(memory_space=pltpu.SMEM)],
        out_specs=pl.BlockSpec((rows, 2 * CONV_WIDTH), lambda n: (n, 0)),
        out_shape=jax.ShapeDtypeStruct((T, 2 * CONV_WIDTH), BF16),
        compiler_params=_params(1),
    )(z, z, z, z, conv_w, sinks)


def _mix_core_bwd(z, dy, tab, conv_w, sinks, name):
    T = z.shape[0]
    nsub = MIX_BLOCKS
    rows = nsub * BLOCK
    steps = T // rows
    last = slice(rows - BLOCK, rows)
    cur = lambda n: jnp.minimum(n, steps - 1)
    prev_block = lambda n: jnp.maximum(nsub * cur(n) - 1, 0)
    prev_rows8 = lambda n: jnp.maximum((rows // 8) * cur(n) - 1, 0)
    next_rows8 = lambda n: jnp.minimum((rows // 8) * (cur(n) + 1), T // 8 - 1)

    def body(z_ref, zkvp_ref, cgp_ref, up_ref, bgn_ref, dy_ref, dyn_ref, tab_ref, tabp_ref, cw_ref, sink_ref,
             dz_ref, dcw_ref, dsk_ref, held_ref, kv_ref):
        n = pl.program_id(0)

        @pl.when(n == 0)
        def _():
            held_ref[...] = jnp.zeros_like(held_ref)
            kv_ref[...] = jnp.zeros_like(kv_ref)
            dcw_ref[...] = jnp.zeros_like(dcw_ref)
            dsk_ref[...] = jnp.zeros_like(dsk_ref)

        def emit_held():
            dz_ref[:, 0:Z_K] = held_ref[:, 0:Z_K]
            if nsub > 1:
                dz_ref[0:rows - BLOCK, Z_K:Z_END] = held_ref[0:rows - BLOCK, Z_K:Z_END]

        @pl.when(n < steps)
        def _():
            emit_held()
            w0, w1, w2 = cw_ref[0:1, :], cw_ref[1:2, :], cw_ref[2:3, :]
            dk_open, dv_open = kv_ref[:, 0:LANES], kv_ref[:, LANES:2 * LANES]
            for b in range(nsub):
                r0 = b * BLOCK
                blk = slice(r0, r0 + BLOCK)
                before = slice(r0 - BLOCK, r0)
                after8 = slice(r0 + BLOCK, r0 + BLOCK + 8)
                if b == 0:
                    has_prev = n > 0
                    kv_prev, cg_prev, u_prev, tab_p = zkvp_ref[...], cgp_ref[...], up_ref[...], tabp_ref[...]
                else:
                    has_prev = True
                    kv_prev, tab_p = z_ref[before, Z_K:Z_END], tab_ref[before, :]
                    cg_prev = z_ref[r0 - 8:r0, CONV_WIDTH:2 * CONV_WIDTH]
                    u_prev = z_ref[r0 - 8:r0, 2 * CONV_WIDTH:Z_Q]
                if b == nsub - 1:
                    dconv_next = jnp.where(n < steps - 1, dyn_ref[...] * bgn_ref[...], 0.0)
                else:
                    dconv_next = dy_ref[after8, 0:CONV_WIDTH] * z_ref[after8, 0:CONV_WIDTH]
                bg = z_ref[blk, 0:CONV_WIDTH]
                cg = z_ref[blk, CONV_WIDTH:2 * CONV_WIDTH]
                u = z_ref[blk, 2 * CONV_WIDTH:Z_Q]
                vv, vv1, vv2 = _conv_taps(cg, u, cg_prev, u_prev, has_prev)
                dyc = dy_ref[blk, 0:CONV_WIDTH]
                dbg = dyc * (w0 * vv2 + w1 * vv1 + w2 * vv)
                dconv = dyc * bg
                ext = jnp.concatenate([dconv, dconv_next], axis=0)
                ext_rows = ext.shape[0]
                dvv = (w2 * dconv + w1 * pltpu.roll(ext, ext_rows - 1, 0)[0:BLOCK]
                       + w0 * pltpu.roll(ext, ext_rows - 2, 0)[0:BLOCK])
                dcw_ref[0:1, :] += jnp.sum(dconv * vv2, axis=0, keepdims=True)
                dcw_ref[1:2, :] += jnp.sum(dconv * vv1, axis=0, keepdims=True)
                dcw_ref[2:3, :] += jnp.sum(dconv * vv, axis=0, keepdims=True)

                tab_c = tab_ref[blk, :]
                k_all = jnp.concatenate([kv_prev[:, 0:LANES], z_ref[blk, Z_K:Z_V]], axis=0)
                v_all = jnp.concatenate([kv_prev[:, LANES:2 * LANES], z_ref[blk, Z_V:Z_END]], axis=0)
                kp = _head_pads(k_all)
                vp = _head_pads(v_all)
                chunks = range(N_Q_HEADS // 2)
                q2 = [(z_ref[blk, Z_Q + LANES * c:Z_Q + LANES * (c + 1)] * ATTN_SCALE).astype(BF16) for c in chunks]
                do2 = [dy_ref[blk, CONV_WIDTH + LANES * c:CONV_WIDTH + LANES * (c + 1)].astype(BF16) for c in chunks]
                probs = _attn_probs_t(q2, kp, _window_mask_t(has_prev), sink_ref)
                dq_chunks = []
                dk_nat = jnp.zeros((2 * BLOCK, LANES), F32)
                dv_nat = jnp.zeros((2 * BLOCK, LANES), F32)
                for kv in range(2):
                    q_st = jnp.concatenate([q2[2 * kv], q2[2 * kv + 1]], axis=0)
                    do_st = jnp.concatenate([do2[2 * kv], do2[2 * kv + 1]], axis=0)
                    dq_t = jnp.zeros((LANES, 2 * BLOCK), F32)
                    dk_par, dv_par = [], []
                    for par in range(2):
                        g = 2 * kv + par
                        pr, psink = probs[g]
                        dp = _dot_nt(vp[(kv, par)], do_st)
                        delta = jnp.sum(dp * pr, axis=0, keepdims=True)
                        ds = (pr * (dp - delta)).astype(BF16)
                        dsink = -psink * delta
                        for r in range(2):
                            h = _group_head(g, r)
                            dsk_ref[h:h + 1, :] += jnp.sum(dsink[:, BLOCK * r:BLOCK * (r + 1)])
                        dq_t = dq_t + _dot_tn(kp[(kv, par)], ds)
                        dk_par.append(_dot(ds, q_st))
                        dv_par.append(_dot(pr.astype(BF16), do_st))
                    for r in range(2):
                        dq_chunks.append(_rot_t(dq_t[:, BLOCK * r:BLOCK * (r + 1)].T * ATTN_SCALE, tab_c))
                    dk_nat = dk_nat + _from_pads(dk_par[0], dk_par[1], kv)
                    dv_nat = dv_nat + _from_pads(dv_par[0], dv_par[1], kv)

                done_ref, done = (dz_ref, last) if b == 0 else (held_ref, before)
                done_ref[done, Z_K:Z_V] = _rot_t(dk_open + dk_nat[0:BLOCK], tab_p).astype(BF16)
                done_ref[done, Z_V:Z_END] = (dv_open + dv_nat[0:BLOCK]).astype(BF16)
                dk_open, dv_open = dk_nat[BLOCK:2 * BLOCK], dv_nat[BLOCK:2 * BLOCK]
                held_ref[blk, 0:CONV_WIDTH] = dbg.astype(BF16)
                held_ref[blk, CONV_WIDTH:2 * CONV_WIDTH] = (dvv * u).astype(BF16)
                held_ref[blk, 2 * CONV_WIDTH:Z_Q] = (dvv * cg).astype(BF16)
                for c in range(N_Q_HEADS // 2):
                    held_ref[blk, Z_Q + LANES * c:Z_Q + LANES * (c + 1)] = dq_chunks[c].astype(BF16)
            kv_ref[:, 0:LANES] = dk_open
            kv_ref[:, LANES:2 * LANES] = dv_open

        @pl.when(n == steps)
        def _():
            emit_held()
            dz_ref[last, Z_K:Z_V] = _rot_t(kv_ref[:, 0:LANES], tab_ref[last, :]).astype(BF16)
            dz_ref[last, Z_V:Z_END] = kv_ref[:, LANES:2 * LANES].astype(BF16)

    return _pcall(
        body, name=name, grid=(steps + 1,),
        in_specs=[pl.BlockSpec((rows, Z_END), lambda n: (cur(n), 0)),
                  pl.BlockSpec((BLOCK, 2 * LANES), lambda n: (prev_block(n), Z_K // (2 * LANES))),
                  pl.BlockSpec((8, CONV_WIDTH), lambda n: (prev_rows8(n), 1)),
                  pl.BlockSpec((8, CONV_WIDTH), lambda n: (prev_rows8(n), 2)),
                  pl.BlockSpec((8, CONV_WIDTH), lambda n: (next_rows8(n), 0)),
                  pl.BlockSpec((rows, 2 * CONV_WIDTH), lambda n: (cur(n), 0)),
                  pl.BlockSpec((8, CONV_WIDTH), lambda n: (next_rows8(n), 0)),
                  pl.BlockSpec((rows, 3 * LANES), lambda n: (cur(n), 0)),
                  pl.BlockSpec((BLOCK, 3 * LANES), lambda n: (prev_block(n), 0)),
                  pl.BlockSpec((3, CONV_WIDTH), lambda n: (0, 0)),
                  pl.BlockSpec(memory_space=pltpu.SMEM)],
        out_specs=[pl.BlockSpec((rows, Z_END), lambda n: (jnp.maximum(n - 1, 0), 0)),
                   pl.BlockSpec((8, CONV_WIDTH), lambda n: (0, 0)), pl.BlockSpec((8, LANES), lambda n: (0, 0))],
        out_shape=[jax.ShapeDtypeStruct((T, Z_END), BF16), jax.ShapeDtypeStruct((8, CONV_WIDTH), F32),
                   jax.ShapeDtypeStruct((8, LANES), F32)],
        scratch_shapes=[pltpu.VMEM((rows, Z_END), BF16), pltpu.VMEM((BLOCK, 2 * LANES), F32)],
        compiler_params=_params(1),
    )(z, z, z, z, z, dy, dy, tab, tab, conv_w, sinks)


def _local_sums(pair, chip, place, name):
    arrays, in_specs, out_specs, out_shape = [], [], [], []
    if pair is not None:
        g, sib = pair
        blk = (1, *sib.shape[1:])
        arrays += [g, sib]
        in_specs += [pl.BlockSpec(blk, lambda q, p: (q, p[1], 0)), pl.BlockSpec(blk, lambda q, p: (q, 0, 0))]
        out_specs.append(pl.BlockSpec(blk, lambda q, p: (q, 0, 0)))
        out_shape.append(jax.ShapeDtypeStruct(sib.shape, BF16))
    if chip is not None:
        g2, sib2, recv2 = chip
        blk = (1, *sib2.shape[1:])
        arrays += [g2, sib2, recv2]
        in_specs += [pl.BlockSpec(blk, lambda q, p: (p[0], p[1], 0)), pl.BlockSpec(blk, lambda q, p: (p[0], 0, 0)),
                     pl.BlockSpec(recv2.shape, lambda q, p: (0, 0, 0))]
        out_specs.append(pl.BlockSpec(sib2.shape[1:], lambda q, p: (p[1], 0)))
        out_shape.append(jax.ShapeDtypeStruct(g2.shape[1:], F32))

    def body(place_ref, *refs):
        refs = list(refs)
        ins, outs = refs[:len(arrays)], refs[len(arrays):]
        if pair is not None:
            g_ref, sib_ref = ins[:2]
            outs[0][...] = (g_ref[...] + sib_ref[...].astype(F32)).astype(BF16)
        if chip is not None:
            g_ref, sib_ref, recv_ref = ins[-3:]

            @pl.when(pl.program_id(0) == 0)
            def _():
                total = g_ref[0] + sib_ref[0].astype(F32)
                for j in range(3):
                    total = total + recv_ref[j].astype(F32)
                outs[-1][...] = total

    return _pcall(
        body, name=name,
        grid_spec=pltpu.PrefetchScalarGridSpec(num_scalar_prefetch=1, grid=(N_CHIPS,),
                                               in_specs=in_specs, out_specs=out_specs),
        out_shape=out_shape, compiler_params=_params(1),
    )(place, *arrays)


def _adamw_math(w, g, m, v):
    m = ADAM_B1 * m + (1.0 - ADAM_B1) * g
    v = ADAM_B2 * v + (1.0 - ADAM_B2) * (g * g)
    m_hat = m / (1.0 - ADAM_B1 ** ADAM_STEP)
    v_hat = v / (1.0 - ADAM_B2 ** ADAM_STEP)
    delta = -ADAM_LR * (m_hat / (jnp.sqrt(v_hat) + ADAM_EPS) + ADAM_WD * w)
    return delta, m, v


def _adamw(ws, gs, ms, vs, row_blocks, name):
    n = len(ws)

    def body(*refs):
        w, g, m, v = refs[:n], refs[n:2 * n], refs[2 * n:3 * n], refs[3 * n:4 * n]
        d, mo, vo, go = refs[4 * n:5 * n], refs[5 * n:6 * n], refs[6 * n:7 * n], refs[7 * n:]
        for t in range(n):
            gv = g[t][...]
            delta, m_new, v_new = _adamw_math(w[t][...], gv, m[t][...], v[t][...])
            d[t][...] = delta
            mo[t][...] = m_new
            vo[t][...] = v_new
            go[t][...] = gv

    specs = [pl.BlockSpec((a.shape[0] // row_blocks, a.shape[1]), lambda i: (i, 0)) for a in ws]
    shapes = [jax.ShapeDtypeStruct(a.shape, F32) for a in ws]
    return _pcall(
        body, name=name, grid=(row_blocks,), in_specs=specs * 4, out_specs=specs * 4, out_shape=shapes * 4,
        compiler_params=_params(1),
    )(*ws, *gs, *ms, *vs)


def kernel(x, ffn1_norm, ffn1_w_gate, ffn1_w_up, ffn1_w_down, mix_norm, w_in, conv_w, attn_sinks, w_out, ffn2_norm, ffn2_w_gate, ffn2_w_up, ffn2_w_down, final_norm, loss_target, m_ffn1_norm, m_ffn1_w_gate, m_ffn1_w_up, m_ffn1_w_down, m_mix_norm, m_w_in, m_conv_w, m_attn_sinks, m_w_out, m_ffn2_norm, m_ffn2_w_gate, m_ffn2_w_up, m_ffn2_w_down, m_final_norm, v_ffn1_norm, v_ffn1_w_gate, v_ffn1_w_up, v_ffn1_w_down, v_mix_norm, v_w_in, v_conv_w, v_attn_sinks, v_w_out, v_ffn2_norm, v_ffn2_w_gate, v_ffn2_w_up, v_ffn2_w_down, v_final_norm):
    T, D = x.shape[1], x.shape[2]
    chip = (2 * lax.axis_index("x") + lax.axis_index("y")).astype(jnp.int32)
    core = lax.axis_index("c").astype(jnp.int32)
    place = jnp.stack([chip, core])
    x0 = x[0]
    target = loss_target[0]
    gf = final_norm.reshape(1, D)

    tr = lambda w: jnp.swapaxes(w[0], 0, 1)
    big = [tr(ffn1_w_gate), tr(ffn1_w_up), ffn1_w_down[0], tr(w_in), w_out[0], tr(ffn2_w_gate), tr(ffn2_w_up), ffn2_w_down[0]]
    transposed = [True, True, False, True, False, True, True, False]
    own_b = [w.astype(BF16) for w in big]

    def whole(gathered, own):
        return lax.dynamic_update_slice(gathered, own[None], (chip, 0, 0)).reshape(-1, D)

    tab, got1 = _rope_tables(T, "rope_gather_ffn1", _gather_plan(own_b[0:3]))
    wg1, wu1, wd1 = (whole(g, o) for g, o in zip(got1, own_b[0:3]))

    res = _ffn_fwd(x0, ffn1_norm, wg1, wu1, wd1, "ffn1_fwd", _gather_plan(own_b[3:8], [conv_w[0]]))
    x1, h1, gate1, up1, act1 = res[:5]
    win, wout, wg2, wu2, wd2 = (whole(g, o) for g, o in zip(res[5:10], own_b[3:8]))
    convw4 = lax.dynamic_update_slice(res[10], conv_w, (chip, 0, 0))
    convw = jnp.transpose(convw4, (1, 0, 2)).reshape(3, -1)
    z, hm = _norm_matmul(x1, mix_norm, win, tab, "mix_in_fwd")
    ymix = _mix_core_fwd(z, convw, attn_sinks, "mix_core_fwd")
    x2 = _matmul_residual(ymix, wout, x1, "mix_out_fwd")
    dx3, h2, gate2, up2, act2, dgf, loss_part = _ffn_fwd(x2, ffn2_norm, wg2, wu2, wd2, "ffn2_fwd", head=(gf, target))

    dx2, dyb2, dgate2, dup2, dg2 = _ffn_bwd(dx3, x2, ffn2_norm, gate2, up2, wg2, wu2, wd2, "ffn2_bwd")
    dymix, dx2b = _matmul_nt(dx2, wout, "mix_out_bwd")
    dz, dcw, dsk = _mix_core_bwd(z, dymix, tab, convw, attn_sinks, "mix_core_bwd")
    dx1, dgm = _matmul_norm_bwd(dz, win, x1, mix_norm, dx2, "mix_in_bwd")
    dx0, dyb1, dgate1, dup1, dg1 = _ffn_bwd(dx1, x0, ffn1_norm, gate1, up1, wg1, wu1, wd1, "ffn1_bwd")

    pad = lambda a: jnp.pad(a, ((0, 0), (0, LANES - a.shape[1])))
    vec = jnp.concatenate([dg1, dgm, dg2, dgf, dcw[0:3].reshape(1, -1), pad(dsk[:, 0].reshape(1, -1)),
                           pad(loss_part[:, 0:1])], axis=1)

    jobs = [("ffn2_dwg", dgate2, h2, 5), ("ffn2_dwu", dup2, h2, 6), ("ffn2_dwd", act2, dyb2, 7),
            ("ffn1_dwg", dgate1, h1, 0), ("ffn1_dwu", dup1, h1, 1), ("ffn1_dwd", act1, dyb1, 2),
            ("mix_dwin", dz, hm, 3), ("mix_dwout", ymix, dx2b, 4)]
    n_jobs = len(jobs)
    grad, grad_b, from_sib, pair_b, from_chips, half, g_big = ({} for _ in range(7))

    def stage_plans(t):
        plans, takers = [], []
        if 0 <= t - 1 < n_jobs:
            plans.append(_sibling_plan([grad_b[t - 1]]))
            takers.append((from_sib, t - 1))
        if 0 <= t - 2 < n_jobs:
            plans.append(_scatter_plan([pair_b[t - 2]]))
            takers.append((from_chips, t - 2))
        if 0 <= t - 3 < n_jobs:
            plans.append(_join_plan([half[t - 3]]))
            takers.append((g_big, jobs[t - 3][3]))
        return plans, takers

    def after_stage(t, landed, takers):
        for (store, key), arr in zip(takers, landed):
            store[key] = arr
        pair = (grad[t - 1], from_sib[t - 1]) if 0 <= t - 1 < n_jobs else None
        chip = (grad[t - 2], from_sib[t - 2], from_chips[t - 2]) if 0 <= t - 2 < n_jobs else None
        if pair or chip:
            sums = list(_local_sums(pair, chip, place, f"local_sums_{t}"))
            if pair:
                pair_b[t - 1] = sums.pop(0)
            if chip:
                half[t - 2] = sums.pop(0)

    for t, (name_, a, b, _) in enumerate(jobs):
        plans, takers = stage_plans(t)
        if t == 0:
            plans.append(_all_gather_plan(jnp.pad(vec, ((0, 7), (0, 0)))))
        res = _matmul_tn(a, b, DW_ROW_SPLIT, name_, _merge_plans(plans))
        grad[t], grad_b[t] = (r.reshape(N_CHIPS, -1, D) for r in res[:2])
        landed = list(res[2:])
        if t == 0:
            vec_blocks = landed.pop()
        after_stage(t, landed, takers)

    ws = big
    ms = [tr(m_ffn1_w_gate), tr(m_ffn1_w_up), m_ffn1_w_down[0], tr(m_w_in), m_w_out[0], tr(m_ffn2_w_gate), tr(m_ffn2_w_up), m_ffn2_w_down[0]]
    vs = [tr(v_ffn1_w_gate), tr(v_ffn1_w_up), v_ffn1_w_down[0], tr(v_w_in), v_w_out[0], tr(v_ffn2_w_gate), tr(v_ffn2_w_up), v_ffn2_w_down[0]]
    for t in range(n_jobs, n_jobs + 3):
        plans, takers = stage_plans(t)
        after_stage(t, _run_comm(_merge_plans(plans), f"grads_tail_{t - n_jobs}"), takers)
    upd = {}
    for name_, idx in (("adamw_a", [0, 1, 2, 4]), ("adamw_b", [3, 5, 6, 7])):
        k = len(idx)
        res = _adamw([ws[i] for i in idx], [g_big[i] for i in idx], [ms[i] for i in idx], [vs[i] for i in idx], ADAMW_ROW_BLOCKS, name_)
        for j, i in enumerate(idx):
            upd[i] = (res[j], res[k + j], res[2 * k + j])
            g_big[i] = res[3 * k + j]

    total = _sum_devices(vec_blocks, "small_sum")[0:1]
    g_n1, g_nm, g_n2, g_nf = (total[:, k * D:(k + 1) * D] for k in range(4))
    cw_full = total[:, 4 * D:4 * D + 3 * CONV_WIDTH].reshape(3, CONV_WIDTH)
    cq = CONV_WIDTH // N_CHIPS
    g_cw = lax.dynamic_slice(cw_full, (0, chip * cq), (3, cq))
    off = 4 * D + 3 * CONV_WIDTH
    g_sk = total[:, off:off + N_Q_HEADS]
    loss = total[0, off + LANES]

    sw = [ffn1_norm, mix_norm, conv_w[0], attn_sinks, ffn2_norm, gf]
    sg = [g_n1, g_nm, g_cw, g_sk, g_n2, g_nf]
    sm = [m_ffn1_norm, m_mix_norm, m_conv_w[0], m_attn_sinks, m_ffn2_norm, m_final_norm.reshape(1, D)]
    sv = [v_ffn1_norm, v_mix_norm, v_conv_w[0], v_attn_sinks, v_ffn2_norm, v_final_norm.reshape(1, D)]
    sres = _adamw(sw, sg, sm, sv, 1, "adamw_small")
    supd = [(sres[j], sres[6 + j], sres[12 + j]) for j in range(6)]

    order = [("s", 0), ("b", 0), ("b", 1), ("b", 2), ("s", 1), ("b", 3), ("s", 2), ("s", 3), ("b", 4),
             ("s", 4), ("b", 5), ("b", 6), ("b", 7), ("s", 5)]

    def leaf(kind, i, which):
        if kind == "b":
            a = g_big[i] if which == 0 else upd[i][which - 1]
            return (jnp.swapaxes(a, 0, 1) if transposed[i] else a)[None]
        a = sg[i] if which == 0 else supd[i][which - 1]
        if i == 2:
            return a[None]
        if i == 5:
            return a.reshape(D)
        return a

    outs = [loss, dx0[None]]
    for which in range(4):
        outs += [leaf(kind, i, which) for kind, i in order]
    return tuple(outs)
```

```python
import jax
import jax.numpy as jnp
import numpy as np
from jax import lax
from jax.experimental import pallas as pl
from jax.experimental.pallas import tpu as pltpu

F32 = jnp.float32
BF16 = jnp.bfloat16
MESH = pl.DeviceIdType.MESH

CONV_WIDTH = 512
N_Q_HEADS = 8
HEAD_DIM = 64
BLOCK = 128
ROPE_THETA = 500000.0
ROT_DIM = 16
RMS_EPS = 1e-5
MASK_VALUE = -1e30
ATTN_SCALE = HEAD_DIM ** -0.5
FFN_RES_SCALE = 0.5
ADAM_LR = 0.001
ADAM_B1 = 0.9
ADAM_B2 = 0.999
ADAM_EPS = 1e-08
ADAM_WD = 0.01
ADAM_STEP = 10

N_CHIPS = 4
N_DEV = 8
LANES = 128
VMEM_LIMIT = 56 * 1024 * 1024

_pcall = pl.pallas_call
HBM_SPEC = pl.BlockSpec(memory_space=pltpu.HBM)
ANY_SPEC = pl.BlockSpec(memory_space=pl.ANY)


def _params(n_axes, vmem=VMEM_LIMIT):
    return pltpu.CompilerParams(dimension_semantics=("arbitrary",) * n_axes, vmem_limit_bytes=vmem)


def _dot(a, b):
    return jnp.dot(a, b, preferred_element_type=F32)


def _dot_nt(a, b):
    return lax.dot_general(a, b, (((1,), (1,)), ((), ())), preferred_element_type=F32)


def _dot_tn(a, b):
    return lax.dot_general(a, b, (((0,), (0,)), ((), ())), preferred_element_type=F32)


def _rms_inv(x):
    return lax.rsqrt(jnp.mean(x * x, axis=-1, keepdims=True) + RMS_EPS)


def _norm_bwd(dh, x, g):
    inv = _rms_inv(x)
    xhat = x * inv
    dg = jnp.sum(dh * xhat, axis=0, keepdims=True)
    dxhat = dh * g
    dx = inv * (dxhat - xhat * jnp.mean(dxhat * xhat, axis=-1, keepdims=True))
    return dx, dg


def _place():
    x, y, c = lax.axis_index("x"), lax.axis_index("y"), lax.axis_index("c")
    chips = [(1 - x, y), (x, 1 - y), (1 - x, 1 - y)]
    return x, y, c, chips


class _Plan:
    def __init__(self, arrays, out_shapes, n_sems, start, finish, middle=None, aliases=None):
        self.arrays, self.out_shapes, self.n_sems = list(arrays), list(out_shapes), n_sems
        self.start, self.finish, self.middle = start, finish, middle
        self.aliases = dict(aliases or {})

    def specs(self):
        k = len(self.arrays)
        sems = [pltpu.SemaphoreType.DMA((self.n_sems,)), pltpu.SemaphoreType.DMA((self.n_sems,))]
        return [HBM_SPEC] * k, [HBM_SPEC] * len(self.out_shapes), self.out_shapes, sems


class _SemSlice:
    def __init__(self, ref, offset):
        self.ref, self.offset = ref, offset

    @property
    def at(self):
        return self

    def __getitem__(self, k):
        return self.ref.at[k + self.offset]


def _merge_plans(plans):
    plans = [p for p in plans if p is not None]
    if len(plans) <= 1:
        return plans[0] if plans else None
    arrays, shapes, aliases, spans, n_sems = [], [], {}, [], 0
    for p in plans:
        a0, o0 = len(arrays), len(shapes)
        spans.append((a0, a0 + len(p.arrays), o0, o0 + len(p.out_shapes), n_sems))
        aliases.update({a0 + i: o0 + j for i, j in p.aliases.items()})
        arrays += p.arrays
        shapes += p.out_shapes
        n_sems += p.n_sems

    def run(which):
        def fn(ins, outs, send_sems, recv_sems):
            for p, (a0, a1, o0, o1, s0) in zip(plans, spans):
                part = getattr(p, which)
                if part is not None:
                    part(ins[a0:a1], outs[o0:o1], _SemSlice(send_sems, s0), _SemSlice(recv_sems, s0))
        return fn

    middle = run("middle") if any(p.middle is not None for p in plans) else None
    return _Plan(arrays, shapes, n_sems, run("start"), run("finish"), middle, aliases)


def _sibling_plan(grads_b):
    n = len(grads_b)

    def copies(ins, outs, send_sems, recv_sems):
        x, y, c, _ = _place()

        def copy(t):
            half = ins[t].shape[1] // 2
            return pltpu.make_async_remote_copy(
                src_ref=ins[t].at[:, pl.ds(pl.multiple_of((1 - c) * half, 16), half), :], dst_ref=outs[t],
                send_sem=send_sems.at[t], recv_sem=recv_sems.at[t], device_id=(x, y, 1 - c), device_id_type=MESH)

        return [copy(t) for t in range(n)]

    def start(*refs):
        for cp in copies(*refs):
            cp.start()

    def finish(*refs):
        for cp in copies(*refs):
            cp.wait()

    shapes = [jax.ShapeDtypeStruct((g.shape[0], g.shape[1] // 2, g.shape[2]), g.dtype) for g in grads_b]
    return _Plan(grads_b, shapes, n, start, finish)


def _scatter_plan(parts_b):
    n = len(parts_b)

    def copies(ins, outs, send_sems, recv_sems):
        x, y, c, chips = _place()

        def copy(t, j):
            px, py = chips[j]
            return pltpu.make_async_remote_copy(
                src_ref=ins[t].at[2 * px + py], dst_ref=outs[t].at[j], send_sem=send_sems.at[3 * t + j],
                recv_sem=recv_sems.at[3 * t + j], device_id=(px, py, c), device_id_type=MESH)

        return [copy(t, j) for t in range(n) for j in range(3)]

    def start(*refs):
        for cp in copies(*refs):
            cp.start()

    def finish(*refs):
        for cp in copies(*refs):
            cp.wait()

    shapes = [jax.ShapeDtypeStruct((3, *p.shape[1:]), p.dtype) for p in parts_b]
    return _Plan(parts_b, shapes, 3 * n, start, finish)


def _gather_plan(shards, small=()):
    n, ns = len(shards), len(small)
    per = 8

    def parts(ins, outs, send_sems, recv_sems):
        x, y, c, chips = _place()
        me = 2 * x + y
        blocks = [2 * px + py for px, py in chips]

        def rows(t, core, piece=None):
            half = ins[t].shape[0] // 2
            if piece is None:
                return pl.ds(pl.multiple_of(core * half, 16), half)
            return pl.ds(pl.multiple_of(core * half + piece * (half // 2), 16), half // 2)

        def remote(src, dst, k, device):
            return pltpu.make_async_remote_copy(src_ref=src, dst_ref=dst, send_sem=send_sems.at[k],
                                                recv_sem=recv_sems.at[k], device_id=device, device_id_type=MESH)

        def first(t, j, block, core):
            return remote(ins[t].at[rows(t, core), :], outs[t].at[block, rows(t, core), :], per * t + j, (*chips[j], c))

        def relay(t, j, block, core):
            ref = outs[t].at[block, rows(t, core, j), :]
            return remote(ref, ref, per * t + 2 + j, (*chips[j], c))

        def passed(t, k, block, core, piece=None):
            ref = outs[t].at[block, rows(t, core, piece), :]
            return remote(ref, ref, per * t + 4 + k, (x, y, 1 - c))

        def whole(s, j, block):
            return remote(ins[n + s], outs[n + s].at[block], per * n + 3 * s + j, (*chips[j], c))

        return c, me, blocks, first, relay, passed, whole

    def start(*refs):
        c, me, _, first, _, _, whole = parts(*refs)
        for t in range(n):
            for j in range(2):
                first(t, j, me, c).start()
        for s in range(ns):
            for j in range(3):
                whole(s, j, me).start()

    def middle(*refs):
        c, _, blocks, first, relay, passed, _ = parts(*refs)
        for t in range(n):
            for j in range(2):
                first(t, j, blocks[j], c).wait_recv()
                passed(t, j, blocks[j], c).start()
                relay(t, 1 - j, blocks[j], c).start()

    def finish(*refs):
        c, me, blocks, first, relay, passed, whole = parts(*refs)
        for t in range(n):
            for j in range(2):
                relay(t, j, blocks[2], c).wait_recv()
                passed(t, 2 + j, blocks[2], c, j).start()
        for t in range(n):
            for j in range(2):
                passed(t, j, blocks[j], 1 - c).wait_recv()
                passed(t, 2 + j, blocks[2], 1 - c, j).wait_recv()
        for s in range(ns):
            for j in range(3):
                whole(s, j, blocks[j]).wait_recv()
        for t in range(n):
            for j in range(2):
                first(t, j, me, c).wait_send()
                relay(t, 1 - j, blocks[j], c).wait_send()
                passed(t, j, blocks[j], c).wait_send()
                passed(t, 2 + j, blocks[2], c, j).wait_send()
        for s in range(ns):
            for j in range(3):
                whole(s, j, me).wait_send()

    arrays = [*shards, *small]
    shapes = [jax.ShapeDtypeStruct((N_CHIPS, *a.shape), a.dtype) for a in arrays]
    return _Plan(arrays, shapes, per * n + 3 * ns, start, finish, middle)


def _run_comm(plan, name):
    k = len(plan.arrays)
    in_specs, out_specs, out_shape, sems = plan.specs()

    def body(*refs):
        cr = (refs[:k], refs[k:k + len(out_shape)], refs[-2], refs[-1])
        plan.start(*cr)
        if plan.middle is not None:
            plan.middle(*cr)
        plan.finish(*cr)

    return _pcall(body, name=name, in_specs=in_specs, out_specs=out_specs, out_shape=out_shape,
                  input_output_aliases=plan.aliases, scratch_shapes=sems)(*plan.arrays)


def _carried(plan, in_specs, out_specs, out_shape, scratch):
    aliases = {}
    if plan is not None:
        p_in, p_out, p_shape, p_sems = plan.specs()
        aliases = {len(in_specs) + i: len(out_specs) + j for i, j in plan.aliases.items()}
        in_specs, out_specs = in_specs + p_in, out_specs + p_out
        out_shape, scratch = out_shape + p_shape, scratch + p_sems
    return dict(in_specs=in_specs, out_specs=out_specs, out_shape=out_shape, scratch_shapes=scratch,
                input_output_aliases=aliases)


def _unpack(refs, n_in, n_out, plan):
    k_in = len(plan.arrays) if plan else 0
    k_out = len(plan.out_shapes) if plan else 0
    ins = refs[:n_in]
    outs = refs[n_in + k_in:n_in + k_in + n_out]
    rest = refs[n_in + k_in + n_out + k_out:]
    if plan is None:
        return ins, outs, rest, None
    cr = (refs[n_in:n_in + k_in], refs[n_in + k_in + n_out:n_in + k_in + n_out + k_out], rest[-2], rest[-1])
    return ins, outs, rest[:-2], cr


def _hook(plan, cr, which, cond):
    fn = getattr(plan, which) if plan is not None else None
    if fn is not None:
        pl.when(cond)(lambda: fn(*cr))


def _join_plan(shards):
    n = len(shards)

    def copy(ins, outs, send_sems, recv_sems, t, core):
        x, y, c, _ = _place()
        half = ins[t].shape[0] // 2
        rows = pl.ds(pl.multiple_of(core * half, 8), half)
        return pltpu.make_async_remote_copy(
            src_ref=ins[t].at[rows, :], dst_ref=outs[t].at[rows, :], send_sem=send_sems.at[t],
            recv_sem=recv_sems.at[t], device_id=(x, y, 1 - c), device_id_type=MESH)

    def start(*refs):
        c = lax.axis_index("c")
        for t in range(n):
            copy(*refs, t, c).start()

    def finish(*refs):
        c = lax.axis_index("c")
        for t in range(n):
            copy(*refs, t, 1 - c).wait_recv()
        for t in range(n):
            copy(*refs, t, c).wait_send()

    shapes = [jax.ShapeDtypeStruct(s.shape, s.dtype) for s in shards]
    return _Plan(shards, shapes, n, start, finish, aliases={t: t for t in range(n)})


def _all_gather_plan(vec):
    def parts(ins, outs, send_sems, recv_sems):
        x, y, c, _ = _place()
        me = 4 * x + 2 * y + c
        rel = [((k >> 2) & 1, (k >> 1) & 1, k & 1) for k in range(1, N_DEV)]

        def peer(k):
            fx, fy, fc = rel[k]
            return (x ^ fx, y ^ fy, c ^ fc)

        def copy(k, dev):
            return pltpu.make_async_remote_copy(
                src_ref=ins[0], dst_ref=outs[0].at[dev], send_sem=send_sems.at[k], recv_sem=recv_sems.at[k],
                device_id=peer(k), device_id_type=MESH)

        mine = pltpu.make_async_copy(ins[0], outs[0].at[me], send_sems.at[N_DEV - 1])
        return me, peer, copy, mine

    def start(*refs):
        me, _, copy, mine = parts(*refs)
        mine.start()
        for k in range(N_DEV - 1):
            copy(k, me).start()

    def finish(*refs):
        me, peer, copy, mine = parts(*refs)
        for k in range(N_DEV - 1):
            px, py, pc = peer(k)
            copy(k, 4 * px + 2 * py + pc).wait_recv()
        for k in range(N_DEV - 1):
            copy(k, me).wait_send()
        mine.wait()

    return _Plan([vec], [jax.ShapeDtypeStruct((N_DEV, *vec.shape), vec.dtype)], N_DEV, start, finish)


def _sum_devices(blocks, name):
    def body(b_ref, o_ref):
        total = b_ref[0]
        for dev in range(1, N_DEV):
            total = total + b_ref[dev]
        o_ref[...] = total

    return _pcall(body, name=name, in_specs=[pl.BlockSpec(memory_space=pltpu.VMEM)],
                  out_specs=pl.BlockSpec(memory_space=pltpu.VMEM),
                  out_shape=jax.ShapeDtypeStruct(blocks.shape[1:], F32))(blocks)


TOKEN_TILE = 512
PROJ_TOKEN_TILE = 1024
OUT_PROJ_TOKEN_TILE = 2048
ADAMW_ROW_BLOCKS = 4
BWD_VMEM_LIMIT = 62 * 1024 * 1024
DW_TOKEN_TILE = 2048
DW_ROW_SPLIT = 2
MXU_COLS = 256
DH_GROUP = 6


def _chunks(n):
    out, c0 = [], 0
    while c0 < n:
        size = min(MXU_COLS, n - c0)
        out.append((c0, size))
        c0 += size
    return out


def _load_weights(hbm_refs, vmem_refs, sems):
    copies = [pltpu.make_async_copy(h, v, sems.at[k]) for k, (h, v) in enumerate(zip(hbm_refs, vmem_refs))]
    for cp in copies:
        cp.start()
    for cp in copies:
        cp.wait()


def _ffn_fwd(x, g, wgt, wut, wd, name, plan=None, head=None):
    T, D = x.shape
    F = wgt.shape[0]
    tm = min(T, TOKEN_TILE)
    ni = T // tm
    n_head = 2 if head is not None else 0

    def body(*refs):
        ins, outs, scratch, cr = _unpack(refs, 5 + n_head, 5 + n_head, plan)
        x_ref, g_ref, wg_hbm, wu_hbm, wd_hbm = ins[:5]
        xo_ref, h_ref, gate_ref, up_ref, act_ref = outs[:5]
        wg_ref, wu_ref, wd_ref, sems = scratch
        i = pl.program_id(0)
        _hook(plan, cr, "start", i == 0)

        @pl.when(i == 0)
        def _():
            _load_weights((wg_hbm, wu_hbm, wd_hbm), (wg_ref, wu_ref, wd_ref), sems)

        xv = x_ref[...]
        h = ((xv * _rms_inv(xv)) * g_ref[...]).astype(BF16)
        h_ref[...] = h
        down, group, row0 = None, [], 0
        chunks = _chunks(F)
        for k, (c0, size) in enumerate(chunks):
            gate = _dot_nt(h, wg_ref[c0:c0 + size, :])
            up = _dot_nt(h, wu_ref[c0:c0 + size, :])
            act = (gate * jax.nn.sigmoid(gate) * up).astype(BF16)
            gate_ref[:, c0:c0 + size] = gate.astype(BF16)
            up_ref[:, c0:c0 + size] = up.astype(BF16)
            act_ref[:, c0:c0 + size] = act
            group.append(act)
            if len(group) == DH_GROUP or k == len(chunks) - 1:
                part = _dot(jnp.concatenate(group, axis=1), wd_ref[row0:c0 + size, :])
                down = part if down is None else down + part
                group, row0 = [], c0 + size
        y = x_ref[...] + FFN_RES_SCALE * down
        if head is None:
            xo_ref[...] = y
        else:
            gf_ref, t_ref = ins[5:]
            dgf_ref, loss_ref = outs[5:]

            @pl.when(i == 0)
            def _():
                dgf_ref[...] = jnp.zeros_like(dgf_ref)
                loss_ref[...] = jnp.zeros_like(loss_ref)

            gf = gf_ref[...]
            diff = (y * _rms_inv(y)) * gf - t_ref[...]
            loss_ref[...] += 0.5 * jnp.sum(jnp.mean(diff * diff, axis=-1, keepdims=True))
            dy, dgf = _norm_bwd(diff * (1.0 / D), y, gf)
            xo_ref[...] = dy
            dgf_ref[...] += dgf
        _hook(plan, cr, "middle", i == ni // 2)
        _hook(plan, cr, "finish", i == ni - 1)

    const = lambda shape: pl.BlockSpec(shape, lambda i: (0, 0))
    rows = lambda width: pl.BlockSpec((tm, width), lambda i: (i, 0))
    in_specs = [rows(D), const((1, D)), ANY_SPEC, ANY_SPEC, ANY_SPEC]
    out_specs = [rows(D), rows(D), rows(F), rows(F), rows(F)]
    out_shape = [jax.ShapeDtypeStruct((T, D), F32), jax.ShapeDtypeStruct((T, D), BF16),
                 jax.ShapeDtypeStruct((T, F), BF16), jax.ShapeDtypeStruct((T, F), BF16), jax.ShapeDtypeStruct((T, F), BF16)]
    if head is not None:
        in_specs += [const((1, D)), rows(D)]
        out_specs += [const((1, D)), const((1, LANES))]
        out_shape += [jax.ShapeDtypeStruct((1, D), F32), jax.ShapeDtypeStruct((1, LANES), F32)]
    io = _carried(plan, in_specs, out_specs, out_shape,
                  [pltpu.VMEM((F, D), BF16), pltpu.VMEM((F, D), BF16), pltpu.VMEM((F, D), BF16),
                   pltpu.SemaphoreType.DMA((3,))])
    return _pcall(
        body, name=name, grid=(ni,), compiler_params=_params(1), **io,
    )(x, g, wgt, wut, wd, *(head or ()), *(plan.arrays if plan else ()))


def _ffn_bwd(dy, x, g, gate, up, wgt, wut, wd, name):
    T, D = x.shape
    F = wgt.shape[0]
    tm = min(T, TOKEN_TILE)
    ni = T // tm

    def body(dy_ref, x_ref, g_ref, gate_ref, up_ref, wg_hbm, wu_hbm, wd_hbm,
             dx_ref, dyb_ref, dgate_ref, dup_ref, dg_ref, wg_ref, wu_ref, wd_ref, sems):
        @pl.when(pl.program_id(0) == 0)
        def _():
            _load_weights((wg_hbm, wu_hbm, wd_hbm), (wg_ref, wu_ref, wd_ref), sems)
            dg_ref[...] = jnp.zeros_like(dg_ref)

        dyb = (FFN_RES_SCALE * dy_ref[...]).astype(BF16)
        dyb_ref[...] = dyb
        dh, group_g, group_u, row0 = None, [], [], 0
        chunks = _chunks(F)
        for k, (c0, size) in enumerate(chunks):
            dact = _dot_nt(dyb, wd_ref[c0:c0 + size, :])
            gt = gate_ref[:, c0:c0 + size].astype(F32)
            u = up_ref[:, c0:c0 + size].astype(F32)
            sig = jax.nn.sigmoid(gt)
            dup = (dact * (gt * sig)).astype(BF16)
            dgate = (dact * u * (sig * (1.0 + gt * (1.0 - sig)))).astype(BF16)
            dup_ref[:, c0:c0 + size] = dup
            dgate_ref[:, c0:c0 + size] = dgate
            group_g.append(dgate)
            group_u.append(dup)
            if len(group_g) == DH_GROUP or k == len(chunks) - 1:
                rows = slice(row0, c0 + size)
                part = (_dot(jnp.concatenate(group_g, axis=1), wg_ref[rows, :])
                        + _dot(jnp.concatenate(group_u, axis=1), wu_ref[rows, :]))
                dh = part if dh is None else dh + part
                group_g, group_u, row0 = [], [], c0 + size
        dxn, dg = _norm_bwd(dh, x_ref[...], g_ref[...])
        dx_ref[...] = dy_ref[...] + dxn
        dg_ref[...] += dg

    return _pcall(
        body, name=name, grid=(ni,),
        in_specs=[pl.BlockSpec((tm, D), lambda i: (i, 0)), pl.BlockSpec((tm, D), lambda i: (i, 0)),
                  pl.BlockSpec((1, D), lambda i: (0, 0)),
                  pl.BlockSpec((tm, F), lambda i: (i, 0)), pl.BlockSpec((tm, F), lambda i: (i, 0)),
                  ANY_SPEC, ANY_SPEC, ANY_SPEC],
        out_specs=[pl.BlockSpec((tm, D), lambda i: (i, 0)), pl.BlockSpec((tm, D), lambda i: (i, 0)),
                   pl.BlockSpec((tm, F), lambda i: (i, 0)), pl.BlockSpec((tm, F), lambda i: (i, 0)),
                   pl.BlockSpec((1, D), lambda i: (0, 0))],
        out_shape=[jax.ShapeDtypeStruct((T, D), F32), jax.ShapeDtypeStruct((T, D), BF16),
                   jax.ShapeDtypeStruct((T, F), BF16), jax.ShapeDtypeStruct((T, F), BF16),
                   jax.ShapeDtypeStruct((1, D), F32)],
        scratch_shapes=[pltpu.VMEM((F, D), BF16), pltpu.VMEM((F, D), BF16), pltpu.VMEM((F, D), BF16),
                        pltpu.SemaphoreType.DMA((3,))],
        compiler_params=_params(1, BWD_VMEM_LIMIT),
    )(dy, x, g, gate, up, wgt, wut, wd)


def _matmul_tn(a, b, row_split, name, plan=None):
    T, n1 = a.shape
    n2 = b.shape[1]
    tn = n1 // row_split
    tk = min(T, DW_TOKEN_TILE)
    nk = T // tk

    def body(*refs):
        (a_ref, b_ref), (o_ref, ob_ref), _, cr = _unpack(refs, 2, 2, plan)
        j = pl.program_id(0)
        k = pl.program_id(1)
        _hook(plan, cr, "start", jnp.logical_and(j == 0, k == 0))

        @pl.when(k == 0)
        def _():
            o_ref[...] = jnp.zeros_like(o_ref)

        o_ref[...] += _dot_tn(a_ref[...], b_ref[...])

        @pl.when(k == nk - 1)
        def _():
            ob_ref[...] = o_ref[...].astype(BF16)

        _hook(plan, cr, "finish", jnp.logical_and(j == row_split - 1, k == nk - 1))

    io = _carried(
        plan,
        [pl.BlockSpec((tk, tn), lambda j, k: (k, j)), pl.BlockSpec((tk, n2), lambda j, k: (k, 0))],
        [pl.BlockSpec((tn, n2), lambda j, k: (j, 0)), pl.BlockSpec((tn, n2), lambda j, k: (j, 0))],
        [jax.ShapeDtypeStruct((n1, n2), F32), jax.ShapeDtypeStruct((n1, n2), BF16)], [])
    return _pcall(
        body, name=name, grid=(row_split, nk), compiler_params=_params(2), **io,
    )(a, b, *(plan.arrays if plan else ()))


def _norm_matmul(x, g, wt, tab, name):
    T, D = x.shape
    n = wt.shape[0]
    tm = min(T, PROJ_TOKEN_TILE)

    def body(x_ref, g_ref, w_ref, tab_ref, z_ref, h_ref):
        xv = x_ref[...]
        h = ((xv * _rms_inv(xv)) * g_ref[...]).astype(BF16)
        h_ref[...] = h
        z = _dot_nt(h, w_ref[...])
        z_ref[:, 0:Z_Q] = z[:, 0:Z_Q]
        tab_v = tab_ref[...]
        for c0 in range(Z_Q, Z_V, LANES):
            z_ref[:, c0:c0 + LANES] = _rot(z[:, c0:c0 + LANES], tab_v)
        z_ref[:, Z_V:Z_END] = z[:, Z_V:Z_END]

    return _pcall(
        body, name=name, grid=(T // tm,),
        in_specs=[pl.BlockSpec((tm, D), lambda i: (i, 0)), pl.BlockSpec((1, D), lambda i: (0, 0)),
                  pl.BlockSpec((n, D), lambda i: (0, 0)), pl.BlockSpec((tm, 3 * LANES), lambda i: (i, 0))],
        out_specs=[pl.BlockSpec((tm, n), lambda i: (i, 0)), pl.BlockSpec((tm, D), lambda i: (i, 0))],
        out_shape=[jax.ShapeDtypeStruct((T, n), F32), jax.ShapeDtypeStruct((T, D), BF16)],
        compiler_params=_params(1),
    )(x, g, wt, tab)


def _matmul_residual(y, w, x, name):
    T, D = x.shape
    kdim = y.shape[1]
    tm = min(T, OUT_PROJ_TOKEN_TILE)

    def body(y_ref, w_ref, x_ref, o_ref):
        o_ref[...] = x_ref[...] + _dot(y_ref[...], w_ref[...])

    return _pcall(
        body, name=name, grid=(T // tm,),
        in_specs=[pl.BlockSpec((tm, kdim), lambda i: (i, 0)), pl.BlockSpec((kdim, D), lambda i: (0, 0)),
                  pl.BlockSpec((tm, D), lambda i: (i, 0))],
        out_specs=pl.BlockSpec((tm, D), lambda i: (i, 0)),
        out_shape=jax.ShapeDtypeStruct((T, D), F32),
        compiler_params=_params(1),
    )(y, w, x)


def _matmul_nt(dx, w, name):
    T, D = dx.shape
    kdim = w.shape[0]
    tm = min(T, OUT_PROJ_TOKEN_TILE)

    def body(dx_ref, w_ref, dy_ref, dxb_ref):
        dxb = dx_ref[...].astype(BF16)
        dxb_ref[...] = dxb
        dy_ref[...] = _dot_nt(dxb, w_ref[...])

    return _pcall(
        body, name=name, grid=(T // tm,),
        in_specs=[pl.BlockSpec((tm, D), lambda i: (i, 0)), pl.BlockSpec((kdim, D), lambda i: (0, 0))],
        out_specs=[pl.BlockSpec((tm, kdim), lambda i: (i, 0)), pl.BlockSpec((tm, D), lambda i: (i, 0))],
        out_shape=[jax.ShapeDtypeStruct((T, kdim), F32), jax.ShapeDtypeStruct((T, D), BF16)],
        compiler_params=_params(1),
    )(dx, w)


def _matmul_norm_bwd(dz, wt, x, g, dres, name):
    T, D = x.shape
    n = dz.shape[1]
    tm = min(T, PROJ_TOKEN_TILE)

    def body(dz_ref, w_ref, x_ref, g_ref, dres_ref, dx_ref, dg_ref):
        @pl.when(pl.program_id(0) == 0)
        def _():
            dg_ref[...] = jnp.zeros_like(dg_ref)

        dh = _dot(dz_ref[...], w_ref[...])
        dxn, dg = _norm_bwd(dh, x_ref[...], g_ref[...])
        dx_ref[...] = dres_ref[...] + dxn
        dg_ref[...] += dg

    return _pcall(
        body, name=name, grid=(T // tm,),
        in_specs=[pl.BlockSpec((tm, n), lambda i: (i, 0)), pl.BlockSpec((n, D), lambda i: (0, 0)),
                  pl.BlockSpec((tm, D), lambda i: (i, 0)), pl.BlockSpec((1, D), lambda i: (0, 0)),
                  pl.BlockSpec((tm, D), lambda i: (i, 0))],
        out_specs=[pl.BlockSpec((tm, D), lambda i: (i, 0)), pl.BlockSpec((1, D), lambda i: (0, 0))],
        out_shape=[jax.ShapeDtypeStruct((T, D), F32), jax.ShapeDtypeStruct((1, D), F32)],
        compiler_params=_params(1),
    )(dz, wt, x, g, dres)


Z_Q = 3 * CONV_WIDTH
Z_K = Z_Q + N_Q_HEADS * HEAD_DIM
Z_V = Z_K + LANES
Z_END = Z_V + LANES


def _rope_tables(T, name, plan):
    half = ROT_DIM // 2
    inv_freq = ROPE_THETA ** (-jnp.arange(0, ROT_DIM, 2, dtype=F32) / ROT_DIM)
    ang = inv_freq[:, None] * jnp.arange(T, dtype=F32)[None, :]
    cos_sin = jnp.concatenate([jnp.cos(ang), jnp.sin(ang)], axis=0)
    select = np.zeros((2 * half, 3 * LANES), np.float32)
    const = np.zeros((1, 3 * LANES), np.float32)
    for lane in range(LANES):
        d = lane % HEAD_DIM
        if d < half:
            select[d, lane] = 1.0
            select[half + d, LANES + lane] = -1.0
        elif d < ROT_DIM:
            select[d - half, lane] = 1.0
            select[d, 2 * LANES + lane] = 1.0
        else:
            const[0, lane] = 1.0
    tm = PROJ_TOKEN_TILE
    ni = T // tm

    def body(*refs):
        (cs_ref, sel_ref, const_ref), (tab_ref,), _, cr = _unpack(refs, 3, 1, plan)
        i = pl.program_id(0)
        _hook(plan, cr, "start", i == 0)
        tab_ref[...] = lax.dot_general(cs_ref[...], sel_ref[...], (((0,), (0,)), ((), ())),
                                       precision=lax.Precision.HIGHEST, preferred_element_type=F32) + const_ref[...]
        _hook(plan, cr, "middle", i == ni - 1)
        _hook(plan, cr, "finish", i == ni - 1)

    io = _carried(
        plan,
        [pl.BlockSpec((2 * half, tm), lambda i: (0, i)), pl.BlockSpec((2 * half, 3 * LANES), lambda i: (0, 0)),
         pl.BlockSpec((1, 3 * LANES), lambda i: (0, 0))],
        [pl.BlockSpec((tm, 3 * LANES), lambda i: (i, 0))], [jax.ShapeDtypeStruct((T, 3 * LANES), F32)], [])
    res = _pcall(body, name=name, grid=(ni,), compiler_params=_params(1), **io)(
        cos_sin, jnp.asarray(select), jnp.asarray(const), *plan.arrays)
    return res[0], res[1:]


def _tab3(tab):
    return tab[:, 0:LANES], tab[:, LANES:2 * LANES], tab[:, 2 * LANES:3 * LANES]


def _rot(x, tab):
    c, s1, s2 = _tab3(tab)
    return x * c + pltpu.roll(x, LANES - ROT_DIM // 2, 1) * s1 + pltpu.roll(x, ROT_DIM // 2, 1) * s2


def _rot_t(d, tab):
    c, s1, s2 = _tab3(tab)
    return d * c + pltpu.roll(d * s1, ROT_DIM // 2, 1) + pltpu.roll(d * s2, LANES - ROT_DIM // 2, 1)


def _head_pads(a):
    lo = lax.broadcasted_iota(jnp.int32, a.shape, 1) < HEAD_DIM
    nat0 = jnp.where(lo, a, 0.0)
    nat1 = jnp.where(lo, 0.0, a)
    return {
        (0, 0): nat0.astype(BF16), (0, 1): pltpu.roll(nat0, HEAD_DIM, 1).astype(BF16),
        (1, 0): pltpu.roll(nat1, HEAD_DIM, 1).astype(BF16), (1, 1): nat1.astype(BF16),
    }


def _from_pads(even, odd, kv):
    lo = lax.broadcasted_iota(jnp.int32, even.shape, 1) < HEAD_DIM
    if kv == 0:
        return jnp.where(lo, even + pltpu.roll(odd, HEAD_DIM, 1), 0.0)
    return jnp.where(lo, 0.0, pltpu.roll(even, HEAD_DIM, 1) + odd)


N_GROUPS = 4


def _group_head(g, r):
    kv, par = divmod(g, 2)
    return 2 * (2 * kv + r) + par


def _window_mask_t(has_prev):
    jj = lax.broadcasted_iota(jnp.int32, (2 * BLOCK, 2 * BLOCK), 0)
    ii = lax.broadcasted_iota(jnp.int32, (2 * BLOCK, 2 * BLOCK), 1) & (BLOCK - 1)
    rel = jj - BLOCK - ii
    return (rel <= 0) & (rel > -BLOCK) & ((jj >= BLOCK) | has_prev)


def _sink_row(sink_ref, g):
    lane = lax.broadcasted_iota(jnp.int32, (1, 2 * BLOCK), 1)
    return jnp.where(lane < BLOCK, sink_ref[0, _group_head(g, 0)], sink_ref[0, _group_head(g, 1)])


def _attn_probs_t(q2, kp, mask, sink_ref):
    out = []
    for kv in range(2):
        q_st = jnp.concatenate([q2[2 * kv], q2[2 * kv + 1]], axis=0)
        for par in range(2):
            s = jnp.where(mask, _dot_nt(kp[(kv, par)], q_st), MASK_VALUE)
            sink = _sink_row(sink_ref, 2 * kv + par)
            m = jnp.maximum(jnp.max(s, axis=0, keepdims=True), sink)
            p = jnp.exp(s - m)
            esink = jnp.exp(sink - m)
            rden = 1.0 / (jnp.sum(p, axis=0, keepdims=True) + esink)
            out.append((p * rden, esink * rden))
    return out


def _conv_taps(cg, u, cg_prev, u_prev, has_prev):
    vv = cg * u
    halo = jnp.where(has_prev, cg_prev * u_prev, 0.0)
    ext = jnp.concatenate([halo, vv], axis=0)
    rows = ext.shape[0]
    vv1 = pltpu.roll(ext, 1, 0)[8:rows]
    vv2 = pltpu.roll(ext, 2, 0)[8:rows]
    return vv, vv1, vv2


MIX_BLOCKS = 4


def _mix_core_fwd(z, conv_w, sinks, name):
    T = z.shape[0]
    rows = MIX_BLOCKS * BLOCK
    steps = T // rows
    prev_block = lambda n: jnp.maximum(MIX_BLOCKS * n - 1, 0)
    prev_rows8 = lambda n: jnp.maximum((rows // 8) * n - 1, 0)

    def body(z_ref, zkvp_ref, cgp_ref, up_ref, cw_ref, sink_ref, y_ref):
        for b in range(MIX_BLOCKS):
            r0 = b * BLOCK
            blk = slice(r0, r0 + BLOCK)
            if b == 0:
                has_prev = pl.program_id(0) > 0
                kv_prev, cg_prev, u_prev = zkvp_ref[...], cgp_ref[...], up_ref[...]
            else:
                has_prev = True
                kv_prev = z_ref[r0 - BLOCK:r0, Z_K:Z_END]
                cg_prev = z_ref[r0 - 8:r0, CONV_WIDTH:2 * CONV_WIDTH]
                u_prev = z_ref[r0 - 8:r0, 2 * CONV_WIDTH:Z_Q]
            bg = z_ref[blk, 0:CONV_WIDTH]
            vv, vv1, vv2 = _conv_taps(z_ref[blk, CONV_WIDTH:2 * CONV_WIDTH], z_ref[blk, 2 * CONV_WIDTH:Z_Q],
                                      cg_prev, u_prev, has_prev)
            conv = cw_ref[0:1, :] * vv2 + cw_ref[1:2, :] * vv1 + cw_ref[2:3, :] * vv
            y_ref[blk, 0:CONV_WIDTH] = (bg * conv).astype(BF16)

            k_all = jnp.concatenate([kv_prev[:, 0:LANES], z_ref[blk, Z_K:Z_V]], axis=0)
            v_all = jnp.concatenate([kv_prev[:, LANES:2 * LANES], z_ref[blk, Z_V:Z_END]], axis=0)
            kp = _head_pads(k_all)
            vp = _head_pads(v_all)
            q2 = [(z_ref[blk, Z_Q + LANES * c:Z_Q + LANES * (c + 1)] * ATTN_SCALE).astype(BF16)
                  for c in range(N_Q_HEADS // 2)]
            probs = _attn_probs_t(q2, kp, _window_mask_t(has_prev), sink_ref)
            for kv in range(2):
                o_t = (_dot_tn(vp[(kv, 0)], probs[2 * kv][0].astype(BF16))
                       + _dot_tn(vp[(kv, 1)], probs[2 * kv + 1][0].astype(BF16)))
                for r in range(2):
                    c = 2 * kv + r
                    y_ref[blk, CONV_WIDTH + LANES * c:CONV_WIDTH + LANES * (c + 1)] = (
                        o_t[:, BLOCK * r:BLOCK * (r + 1)].T.astype(BF16))

    return _pcall(
        body, name=name, grid=(steps,),
        in_specs=[pl.BlockSpec((rows, Z_END), lambda n: (n, 0)),
                  pl.BlockSpec((BLOCK, 2 * LANES), lambda n: (prev_block(n), Z_K // (2 * LANES))),
                  pl.BlockSpec((8, CONV_WIDTH), lambda n: (prev_rows8(n), 1)),
                  pl.BlockSpec((8, CONV_WIDTH), lambda n: (prev_rows8(n), 2)),
                  pl.BlockSpec((3, CONV_WIDTH), lambda n: (0, 0)),
                  pl.BlockSpec(memory_space=pltpu.SMEM)],
        out_specs=pl.BlockSpec((rows, 2 * CONV_WIDTH), lambda n: (n, 0)),
        out_shape=jax.ShapeDtypeStruct((T, 2 * CONV_WIDTH), BF16),
        compiler_params=_params(1),
    )(z, z, z, z, conv_w, sinks)


def _mix_core_bwd(z, dy, tab, conv_w, sinks, name):
    T = z.shape[0]
    nsub = MIX_BLOCKS
    rows = nsub * BLOCK
    steps = T // rows
    last = slice(rows - BLOCK, rows)
    cur = lambda n: jnp.minimum(n, steps - 1)
    prev_block = lambda n: jnp.maximum(nsub * cur(n) - 1, 0)
    prev_rows8 = lambda n: jnp.maximum((rows // 8) * cur(n) - 1, 0)
    next_rows8 = lambda n: jnp.minimum((rows // 8) * (cur(n) + 1), T // 8 - 1)

    def body(z_ref, zkvp_ref, cgp_ref, up_ref, bgn_ref, dy_ref, dyn_ref, tab_ref, tabp_ref, cw_ref, sink_ref,
             dz_ref, dcw_ref, dsk_ref, held_ref, kv_ref):
        n = pl.program_id(0)

        @pl.when(n == 0)
        def _():
            held_ref[...] = jnp.zeros_like(held_ref)
            kv_ref[...] = jnp.zeros_like(kv_ref)
            dcw_ref[...] = jnp.zeros_like(dcw_ref)
            dsk_ref[...] = jnp.zeros_like(dsk_ref)

        def emit_held():
            dz_ref[:, 0:Z_K] = held_ref[:, 0:Z_K]
            if nsub > 1:
                dz_ref[0:rows - BLOCK, Z_K:Z_END] = held_ref[0:rows - BLOCK, Z_K:Z_END]

        @pl.when(n < steps)
        def _():
            emit_held()
            w0, w1, w2 = cw_ref[0:1, :], cw_ref[1:2, :], cw_ref[2:3, :]
            dk_open, dv_open = kv_ref[:, 0:LANES], kv_ref[:, LANES:2 * LANES]
            for b in range(nsub):
                r0 = b * BLOCK
                blk = slice(r0, r0 + BLOCK)
                before = slice(r0 - BLOCK, r0)
                after8 = slice(r0 + BLOCK, r0 + BLOCK + 8)
                if b == 0:
                    has_prev = n > 0
                    kv_prev, cg_prev, u_prev, tab_p = zkvp_ref[...], cgp_ref[...], up_ref[...], tabp_ref[...]
                else:
                    has_prev = True
                    kv_prev, tab_p = z_ref[before, Z_K:Z_END], tab_ref[before, :]
                    cg_prev = z_ref[r0 - 8:r0, CONV_WIDTH:2 * CONV_WIDTH]
                    u_prev = z_ref[r0 - 8:r0, 2 * CONV_WIDTH:Z_Q]
                if b == nsub - 1:
                    dconv_next = jnp.where(n < steps - 1, dyn_ref[...] * bgn_ref[...], 0.0)
                else:
                    dconv_next = dy_ref[after8, 0:CONV_WIDTH] * z_ref[after8, 0:CONV_WIDTH]
                bg = z_ref[blk, 0:CONV_WIDTH]
                cg = z_ref[blk, CONV_WIDTH:2 * CONV_WIDTH]
                u = z_ref[blk, 2 * CONV_WIDTH:Z_Q]
                vv, vv1, vv2 = _conv_taps(cg, u, cg_prev, u_prev, has_prev)
                dyc = dy_ref[blk, 0:CONV_WIDTH]
                dbg = dyc * (w0 * vv2 + w1 * vv1 + w2 * vv)
                dconv = dyc * bg
                ext = jnp.concatenate([dconv, dconv_next], axis=0)
                ext_rows = ext.shape[0]
                dvv = (w2 * dconv + w1 * pltpu.roll(ext, ext_rows - 1, 0)[0:BLOCK]
                       + w0 * pltpu.roll(ext, ext_rows - 2, 0)[0:BLOCK])
                dcw_ref[0:1, :] += jnp.sum(dconv * vv2, axis=0, keepdims=True)
                dcw_ref[1:2, :] += jnp.sum(dconv * vv1, axis=0, keepdims=True)
                dcw_ref[2:3, :] += jnp.sum(dconv * vv, axis=0, keepdims=True)

                tab_c = tab_ref[blk, :]
                k_all = jnp.concatenate([kv_prev[:, 0:LANES], z_ref[blk, Z_K:Z_V]], axis=0)
                v_all = jnp.concatenate([kv_prev[:, LANES:2 * LANES], z_ref[blk, Z_V:Z_END]], axis=0)
                kp = _head_pads(k_all)
                vp = _head_pads(v_all)
                chunks = range(N_Q_HEADS // 2)
                q2 = [(z_ref[blk, Z_Q + LANES * c:Z_Q + LANES * (c + 1)] * ATTN_SCALE).astype(BF16) for c in chunks]
                do2 = [dy_ref[blk, CONV_WIDTH + LANES * c:CONV_WIDTH + LANES * (c + 1)].astype(BF16) for c in chunks]
                probs = _attn_probs_t(q2, kp, _window_mask_t(has_prev), sink_ref)
                dq_chunks = []
                dk_nat = jnp.zeros((2 * BLOCK, LANES), F32)
                dv_nat = jnp.zeros((2 * BLOCK, LANES), F32)
                for kv in range(2):
                    q_st = jnp.concatenate([q2[2 * kv], q2[2 * kv + 1]], axis=0)
                    do_st = jnp.concatenate([do2[2 * kv], do2[2 * kv + 1]], axis=0)
                    dq_t = jnp.zeros((LANES, 2 * BLOCK), F32)
                    dk_par, dv_par = [], []
                    for par in range(2):
                        g = 2 * kv + par
                        pr, psink = probs[g]
                        dp = _dot_nt(vp[(kv, par)], do_st)
                        delta = jnp.sum(dp * pr, axis=0, keepdims=True)
                        ds = (pr * (dp - delta)).astype(BF16)
                        dsink = -psink * delta
                        for r in range(2):
                            h = _group_head(g, r)
                            dsk_ref[h:h + 1, :] += jnp.sum(dsink[:, BLOCK * r:BLOCK * (r + 1)])
                        dq_t = dq_t + _dot_tn(kp[(kv, par)], ds)
                        dk_par.append(_dot(ds, q_st))
                        dv_par.append(_dot(pr.astype(BF16), do_st))
                    for r in range(2):
                        dq_chunks.append(_rot_t(dq_t[:, BLOCK * r:BLOCK * (r + 1)].T * ATTN_SCALE, tab_c))
                    dk_nat = dk_nat + _from_pads(dk_par[0], dk_par[1], kv)
                    dv_nat = dv_nat + _from_pads(dv_par[0], dv_par[1], kv)

                done_ref, done = (dz_ref, last) if b == 0 else (held_ref, before)
                done_ref[done, Z_K:Z_V] = _rot_t(dk_open + dk_nat[0:BLOCK], tab_p).astype(BF16)
                done_ref[done, Z_V:Z_END] = (dv_open + dv_nat[0:BLOCK]).astype(BF16)
                dk_open, dv_open = dk_nat[BLOCK:2 * BLOCK], dv_nat[BLOCK:2 * BLOCK]
                held_ref[blk, 0:CONV_WIDTH] = dbg.astype(BF16)
                held_ref[blk, CONV_WIDTH:2 * CONV_WIDTH] = (dvv * u).astype(BF16)
                held_ref[blk, 2 * CONV_WIDTH:Z_Q] = (dvv * cg).astype(BF16)
                for c in range(N_Q_HEADS // 2):
                    held_ref[blk, Z_Q + LANES * c:Z_Q + LANES * (c + 1)] = dq_chunks[c].astype(BF16)
            kv_ref[:, 0:LANES] = dk_open
            kv_ref[:, LANES:2 * LANES] = dv_open

        @pl.when(n == steps)
        def _():
            emit_held()
            dz_ref[last, Z_K:Z_V] = _rot_t(kv_ref[:, 0:LANES], tab_ref[last, :]).astype(BF16)
            dz_ref[last, Z_V:Z_END] = kv_ref[:, LANES:2 * LANES].astype(BF16)

    return _pcall(
        body, name=name, grid=(steps + 1,),
        in_specs=[pl.BlockSpec((rows, Z_END), lambda n: (cur(n), 0)),
                  pl.BlockSpec((BLOCK, 2 * LANES), lambda n: (prev_block(n), Z_K // (2 * LANES))),
                  pl.BlockSpec((8, CONV_WIDTH), lambda n: (prev_rows8(n), 1)),
                  pl.BlockSpec((8, CONV_WIDTH), lambda n: (prev_rows8(n), 2)),
                  pl.BlockSpec((8, CONV_WIDTH), lambda n: (next_rows8(n), 0)),
                  pl.BlockSpec((rows, 2 * CONV_WIDTH), lambda n: (cur(n), 0)),
                  pl.BlockSpec((8, CONV_WIDTH), lambda n: (next_rows8(n), 0)),
                  pl.BlockSpec((rows, 3 * LANES), lambda n: (cur(n), 0)),
                  pl.BlockSpec((BLOCK, 3 * LANES), lambda n: (prev_block(n), 0)),
                  pl.BlockSpec((3, CONV_WIDTH), lambda n: (0, 0)),
                  pl.BlockSpec(memory_space=pltpu.SMEM)],
        out_specs=[pl.BlockSpec((rows, Z_END), lambda n: (jnp.maximum(n - 1, 0), 0)),
                   pl.BlockSpec((8, CONV_WIDTH), lambda n: (0, 0)), pl.BlockSpec((8, LANES), lambda n: (0, 0))],
        out_shape=[jax.ShapeDtypeStruct((T, Z_END), BF16), jax.ShapeDtypeStruct((8, CONV_WIDTH), F32),
                   jax.ShapeDtypeStruct((8, LANES), F32)],
        scratch_shapes=[pltpu.VMEM((rows, Z_END), BF16), pltpu.VMEM((BLOCK, 2 * LANES), F32)],
        compiler_params=_params(1),
    )(z, z, z, z, z, dy, dy, tab, tab, conv_w, sinks)


def _local_sums(pair, chip, place, name):
    arrays, in_specs, out_specs, out_shape = [], [], [], []
    if pair is not None:
        g, sib = pair
        blk = (1, *sib.shape[1:])
        arrays += [g, sib]
        in_specs += [pl.BlockSpec(blk, lambda q, p: (q, p[1], 0)), pl.BlockSpec(blk, lambda q, p: (q, 0, 0))]
        out_specs.append(pl.BlockSpec(blk, lambda q, p: (q, 0, 0)))
        out_shape.append(jax.ShapeDtypeStruct(sib.shape, BF16))
    if chip is not None:
        g2, sib2, recv2 = chip
        blk = (1, *sib2.shape[1:])
        arrays += [g2, sib2, recv2]
        in_specs += [pl.BlockSpec(blk, lambda q, p: (p[0], p[1], 0)), pl.BlockSpec(blk, lambda q, p: (p[0], 0, 0)),
                     pl.BlockSpec(recv2.shape, lambda q, p: (0, 0, 0))]
        out_specs.append(pl.BlockSpec(sib2.shape[1:], lambda q, p: (p[1], 0)))
        out_shape.append(jax.ShapeDtypeStruct(g2.shape[1:], F32))

    def body(place_ref, *refs):
        refs = list(refs)
        ins, outs = refs[:len(arrays)], refs[len(arrays):]
        if pair is not None:
            g_ref, sib_ref = ins[:2]
            outs[0][...] = (g_ref[...] + sib_ref[...].astype(F32)).astype(BF16)
        if chip is not None:
            g_ref, sib_ref, recv_ref = ins[-3:]

            @pl.when(pl.program_id(0) == 0)
            def _():
                total = g_ref[0] + sib_ref[0].astype(F32)
                for j in range(3):
                    total = total + recv_ref[j].astype(F32)
                outs[-1][...] = total

    return _pcall(
        body, name=name,
        grid_spec=pltpu.PrefetchScalarGridSpec(num_scalar_prefetch=1, grid=(N_CHIPS,),
                                               in_specs=in_specs, out_specs=out_specs),
        out_shape=out_shape, compiler_params=_params(1),
    )(place, *arrays)


def _adamw_math(w, g, m, v):
    m = ADAM_B1 * m + (1.0 - ADAM_B1) * g
    v = ADAM_B2 * v + (1.0 - ADAM_B2) * (g * g)
    m_hat = m / (1.0 - ADAM_B1 ** ADAM_STEP)
    v_hat = v / (1.0 - ADAM_B2 ** ADAM_STEP)
    delta = -ADAM_LR * (m_hat / (jnp.sqrt(v_hat) + ADAM_EPS) + ADAM_WD * w)
    return delta, m, v


def _adamw(ws, gs, ms, vs, row_blocks, name):
    n = len(ws)

    def body(*refs):
        w, g, m, v = refs[:n], refs[n:2 * n], refs[2 * n:3 * n], refs[3 * n:4 * n]
        d, mo, vo, go = refs[4 * n:5 * n], refs[5 * n:6 * n], refs[6 * n:7 * n], refs[7 * n:]
        for t in range(n):
            gv = g[t][...]
            delta, m_new, v_new = _adamw_math(w[t][...], gv, m[t][...], v[t][...])
            d[t][...] = delta
            mo[t][...] = m_new
            vo[t][...] = v_new
            go[t][...] = gv

    specs = [pl.BlockSpec((a.shape[0] // row_blocks, a.shape[1]), lambda i: (i, 0)) for a in ws]
    shapes = [jax.ShapeDtypeStruct(a.shape, F32) for a in ws]
    return _pcall(
        body, name=name, grid=(row_blocks,), in_specs=specs * 4, out_specs=specs * 4, out_shape=shapes * 4,
        compiler_params=_params(1),
    )(*ws, *gs, *ms, *vs)


def kernel(x, ffn1_norm, ffn1_w_gate, ffn1_w_up, ffn1_w_down, mix_norm, w_in, conv_w, attn_sinks, w_out, ffn2_norm, ffn2_w_gate, ffn2_w_up, ffn2_w_down, final_norm, loss_target, m_ffn1_norm, m_ffn1_w_gate, m_ffn1_w_up, m_ffn1_w_down, m_mix_norm, m_w_in, m_conv_w, m_attn_sinks, m_w_out, m_ffn2_norm, m_ffn2_w_gate, m_ffn2_w_up, m_ffn2_w_down, m_final_norm, v_ffn1_norm, v_ffn1_w_gate, v_ffn1_w_up, v_ffn1_w_down, v_mix_norm, v_w_in, v_conv_w, v_attn_sinks, v_w_out, v_ffn2_norm, v_ffn2_w_gate, v_ffn2_w_up, v_ffn2_w_down, v_final_norm):
    T, D = x.shape[1], x.shape[2]
    chip = (2 * lax.axis_index("x") + lax.axis_index("y")).astype(jnp.int32)
    core = lax.axis_index("c").astype(jnp.int32)
    place = jnp.stack([chip, core])
    x0 = x[0]
    target = loss_target[0]
    gf = final_norm.reshape(1, D)

    tr = lambda w: jnp.swapaxes(w[0], 0, 1)
    big = [tr(ffn1_w_gate), tr(ffn1_w_up), ffn1_w_down[0], tr(w_in), w_out[0], tr(ffn2_w_gate), tr(ffn2_w_up), ffn2_w_down[0]]
    transposed = [True, True, False, True, False, True, True, False]
    own_b = [w.astype(BF16) for w in big]

    def whole(gathered, own):
        return lax.dynamic_update_slice(gathered, own[None], (chip, 0, 0)).reshape(-1, D)

    tab, got1 = _rope_tables(T, "rope_gather_ffn1", _gather_plan(own_b[0:3]))
    wg1, wu1, wd1 = (whole(g, o) for g, o in zip(got1, own_b[0:3]))

    res = _ffn_fwd(x0, ffn1_norm, wg1, wu1, wd1, "ffn1_fwd", _gather_plan(own_b[3:8], [conv_w[0]]))
    x1, h1, gate1, up1, act1 = res[:5]
    win, wout, wg2, wu2, wd2 = (whole(g, o) for g, o in zip(res[5:10], own_b[3:8]))
    convw4 = lax.dynamic_update_slice(res[10], conv_w, (chip, 0, 0))
    convw = jnp.transpose(convw4, (1, 0, 2)).reshape(3, -1)
    z, hm = _norm_matmul(x1, mix_norm, win, tab, "mix_in_fwd")
    ymix = _mix_core_fwd(z, convw, attn_sinks, "mix_core_fwd")
    x2 = _matmul_residual(ymix, wout, x1, "mix_out_fwd")
    dx3, h2, gate2, up2, act2, dgf, loss_part = _ffn_fwd(x2, ffn2_norm, wg2, wu2, wd2, "ffn2_fwd", head=(gf, target))

    dx2, dyb2, dgate2, dup2, dg2 = _ffn_bwd(dx3, x2, ffn2_norm, gate2, up2, wg2, wu2, wd2, "ffn2_bwd")
    dymix, dx2b = _matmul_nt(dx2, wout, "mix_out_bwd")
    dz, dcw, dsk = _mix_core_bwd(z, dymix, tab, convw, attn_sinks, "mix_core_bwd")
    dx1, dgm = _matmul_norm_bwd(dz, win, x1, mix_norm, dx2, "mix_in_bwd")
    dx0, dyb1, dgate1, dup1, dg1 = _ffn_bwd(dx1, x0, ffn1_norm, gate1, up1, wg1, wu1, wd1, "ffn1_bwd")

    pad = lambda a: jnp.pad(a, ((0, 0), (0, LANES - a.shape[1])))
    vec = jnp.concatenate([dg1, dgm, dg2, dgf, dcw[0:3].reshape(1, -1), pad(dsk[:, 0].reshape(1, -1)),
                           pad(loss_part[:, 0:1])], axis=1)

    jobs = [("ffn2_dwg", dgate2, h2, 5), ("ffn2_dwu", dup2, h2, 6), ("ffn2_dwd", act2, dyb2, 7),
            ("ffn1_dwg", dgate1, h1, 0), ("ffn1_dwu", dup1, h1, 1), ("ffn1_dwd", act1, dyb1, 2),
            ("mix_dwin", dz, hm, 3), ("mix_dwout", ymix, dx2b, 4)]
    n_jobs = len(jobs)
    grad, grad_b, from_sib, pair_b, from_chips, half, g_big = ({} for _ in range(7))

    def stage_plans(t):
        plans, takers = [], []
        if 0 <= t - 1 < n_jobs:
            plans.append(_sibling_plan([grad_b[t - 1]]))
            takers.append((from_sib, t - 1))
        if 0 <= t - 2 < n_jobs:
            plans.append(_scatter_plan([pair_b[t - 2]]))
            takers.append((from_chips, t - 2))
        if 0 <= t - 3 < n_jobs:
            plans.append(_join_plan([half[t - 3]]))
            takers.append((g_big, jobs[t - 3][3]))
        return plans, takers

    def after_stage(t, landed, takers):
        for (store, key), arr in zip(takers, landed):
            store[key] = arr
        pair = (grad[t - 1], from_sib[t - 1]) if 0 <= t - 1 < n_jobs else None
        chip = (grad[t - 2], from_sib[t - 2], from_chips[t - 2]) if 0 <= t - 2 < n_jobs else None
        if pair or chip:
            sums = list(_local_sums(pair, chip, place, f"local_sums_{t}"))
            if pair:
                pair_b[t - 1] = sums.pop(0)
            if chip:
                half[t - 2] = sums.pop(0)

    for t, (name_, a, b, _) in enumerate(jobs):
        plans, takers = stage_plans(t)
        if t == 0:
            plans.append(_all_gather_plan(jnp.pad(vec, ((0, 7), (0, 0)))))
        res = _matmul_tn(a, b, DW_ROW_SPLIT, name_, _merge_plans(plans))
        grad[t], grad_b[t] = (r.reshape(N_CHIPS, -1, D) for r in res[:2])
        landed = list(res[2:])
        if t == 0:
            vec_blocks = landed.pop()
        after_stage(t, landed, takers)

    ws = big
    ms = [tr(m_ffn1_w_gate), tr(m_ffn1_w_up), m_ffn1_w_down[0], tr(m_w_in), m_w_out[0], tr(m_ffn2_w_gate), tr(m_ffn2_w_up), m_ffn2_w_down[0]]
    vs = [tr(v_ffn1_w_gate), tr(v_ffn1_w_up), v_ffn1_w_down[0], tr(v_w_in), v_w_out[0], tr(v_ffn2_w_gate), tr(v_ffn2_w_up), v_ffn2_w_down[0]]
    for t in range(n_jobs, n_jobs + 3):
        plans, takers = stage_plans(t)
        after_stage(t, _run_comm(_merge_plans(plans), f"grads_tail_{t - n_jobs}"), takers)
    upd = {}
    for name_, idx in (("adamw_a", [0, 1, 2, 4]), ("adamw_b", [3, 5, 6, 7])):
        k = len(idx)
        res = _adamw([ws[i] for i in idx], [g_big[i] for i in idx], [ms[i] for i in idx], [vs[i] for i in idx], ADAMW_ROW_BLOCKS, name_)
        for j, i in enumerate(idx):
            upd[i] = (res[j], res[k + j], res[2 * k + j])
            g_big[i] = res[3 * k + j]

    total = _sum_devices(vec_blocks, "small_sum")[0:1]
    g_n1, g_nm, g_n2, g_nf = (total[:, k * D:(k + 1) * D] for k in range(4))
    cw_full = total[:, 4 * D:4 * D + 3 * CONV_WIDTH].reshape(3, CONV_WIDTH)
    cq = CONV_WIDTH // N_CHIPS
    g_cw = lax.dynamic_slice(cw_full, (0, chip * cq), (3, cq))
    off = 4 * D + 3 * CONV_WIDTH
    g_sk = total[:, off:off + N_Q_HEADS]
    loss = total[0, off + LANES]

    sw = [ffn1_norm, mix_norm, conv_w[0], attn_sinks, ffn2_norm, gf]
    sg = [g_n1, g_nm, g_cw, g_sk, g_n2, g_nf]
    sm = [m_ffn1_norm, m_mix_norm, m_conv_w[0], m_attn_sinks, m_ffn2_norm, m_final_norm.reshape(1, D)]
    sv = [v_ffn1_norm, v_mix_norm, v_conv_w[0], v_attn_sinks, v_ffn2_norm, v_final_norm.reshape(1, D)]
    sres = _adamw(sw, sg, sm, sv, 1, "adamw_small")
    supd = [(sres[j], sres[6 + j], sres[12 + j]) for j in range(6)]

    order = [("s", 0), ("b", 0), ("b", 1), ("b", 2), ("s", 1), ("b", 3), ("s", 2), ("s", 3), ("b", 4),
             ("s", 4), ("b", 5), ("b", 6), ("b", 7), ("s", 5)]

    def leaf(kind, i, which):
        if kind == "b":
            a = g_big[i] if which == 0 else upd[i][which - 1]
            return (jnp.swapaxes(a, 0, 1) if transposed[i] else a)[None]
        a = sg[i] if which == 0 else supd[i][which - 1]
        if i == 2:
            return a[None]
        if i == 5:
            return a.reshape(D)
        return a

    outs = [loss, dx0[None]]
    for which in range(4):
        outs += [leaf(kind, i, which) for kind, i in order]
    return tuple(outs)
```

```python
import jax
import jax.numpy as jnp
import numpy as np
from jax import lax
from jax.experimental import pallas as pl
from jax.experimental.pallas import tpu as pltpu

F32 = jnp.float32
BF16 = jnp.bfloat16
MESH = pl.DeviceIdType.MESH

CONV_WIDTH = 512
N_Q_HEADS = 8
HEAD_DIM = 64
BLOCK = 128
ROPE_THETA = 500000.0
ROT_DIM = 16
RMS_EPS = 1e-5
MASK_VALUE = -1e30
ATTN_SCALE = HEAD_DIM ** -0.5
FFN_RES_SCALE = 0.5
ADAM_LR = 0.001
ADAM_B1 = 0.9
ADAM_B2 = 0.999
ADAM_EPS = 1e-08
ADAM_WD = 0.01
ADAM_STEP = 10

N_CHIPS = 4
N_DEV = 8
LANES = 128
VMEM_LIMIT = 56 * 1024 * 1024

_pcall = pl.pallas_call
HBM_SPEC = pl.BlockSpec(memory_space=pltpu.HBM)
ANY_SPEC = pl.BlockSpec(memory_space=pl.ANY)


def _params(n_axes, vmem=VMEM_LIMIT):
    return pltpu.CompilerParams(dimension_semantics=("arbitrary",) * n_axes, vmem_limit_bytes=vmem)


def _dot(a, b):
    return jnp.dot(a, b, preferred_element_type=F32)


def _dot_nt(a, b):
    return lax.dot_general(a, b, (((1,), (1,)), ((), ())), preferred_element_type=F32)


def _dot_tn(a, b):
    return lax.dot_general(a, b, (((0,), (0,)), ((), ())), preferred_element_type=F32)


def _rms_inv(x):
    return lax.rsqrt(jnp.mean(x * x, axis=-1, keepdims=True) + RMS_EPS)


def _norm_bwd(dh, x, g):
    inv = _rms_inv(x)
    xhat = x * inv
    dg = jnp.sum(dh * xhat, axis=0, keepdims=True)
    dxhat = dh * g
    dx = inv * (dxhat - xhat * jnp.mean(dxhat * xhat, axis=-1, keepdims=True))
    return dx, dg


def _place():
    x, y, c = lax.axis_index("x"), lax.axis_index("y"), lax.axis_index("c")
    chips = [(1 - x, y), (x, 1 - y), (1 - x, 1 - y)]
    return x, y, c, chips


class _Plan:
    def __init__(self, arrays, out_shapes, n_sems, start, finish, middle=None, aliases=None):
        self.arrays, self.out_shapes, self.n_sems = list(arrays), list(out_shapes), n_sems
        self.start, self.finish, self.middle = start, finish, middle
        self.aliases = dict(aliases or {})

    def specs(self):
        k = len(self.arrays)
        sems = [pltpu.SemaphoreType.DMA((self.n_sems,)), pltpu.SemaphoreType.DMA((self.n_sems,))]
        return [HBM_SPEC] * k, [HBM_SPEC] * len(self.out_shapes), self.out_shapes, sems


class _SemSlice:
    def __init__(self, ref, offset):
        self.ref, self.offset = ref, offset

    @property
    def at(self):
        return self

    def __getitem__(self, k):
        return self.ref.at[k + self.offset]


def _merge_plans(plans):
    plans = [p for p in plans if p is not None]
    if len(plans) <= 1:
        return plans[0] if plans else None
    arrays, shapes, aliases, spans, n_sems = [], [], {}, [], 0
    for p in plans:
        a0, o0 = len(arrays), len(shapes)
        spans.append((a0, a0 + len(p.arrays), o0, o0 + len(p.out_shapes), n_sems))
        aliases.update({a0 + i: o0 + j for i, j in p.aliases.items()})
        arrays += p.arrays
        shapes += p.out_shapes
        n_sems += p.n_sems

    def run(which):
        def fn(ins, outs, send_sems, recv_sems):
            for p, (a0, a1, o0, o1, s0) in zip(plans, spans):
                part = getattr(p, which)
                if part is not None:
                    part(ins[a0:a1], outs[o0:o1], _SemSlice(send_sems, s0), _SemSlice(recv_sems, s0))
        return fn

    middle = run("middle") if any(p.middle is not None for p in plans) else None
    return _Plan(arrays, shapes, n_sems, run("start"), run("finish"), middle, aliases)


def _sibling_plan(grads_b):
    n = len(grads_b)

    def copies(ins, outs, send_sems, recv_sems):
        x, y, c, _ = _place()

        def copy(t):
            half = ins[t].shape[1] // 2
            return pltpu.make_async_remote_copy(
                src_ref=ins[t].at[:, pl.ds(pl.multiple_of((1 - c) * half, 16), half), :], dst_ref=outs[t],
                send_sem=send_sems.at[t], recv_sem=recv_sems.at[t], device_id=(x, y, 1 - c), device_id_type=MESH)

        return [copy(t) for t in range(n)]

    def start(*refs):
        for cp in copies(*refs):
            cp.start()

    def finish(*refs):
        for cp in copies(*refs):
            cp.wait()

    shapes = [jax.ShapeDtypeStruct((g.shape[0], g.shape[1] // 2, g.shape[2]), g.dtype) for g in grads_b]
    return _Plan(grads_b, shapes, n, start, finish)


def _scatter_plan(parts_b):
    n = len(parts_b)

    def copies(ins, outs, send_sems, recv_sems):
        x, y, c, chips = _place()

        def copy(t, j):
            px, py = chips[j]
            return pltpu.make_async_remote_copy(
                src_ref=ins[t].at[2 * px + py], dst_ref=outs[t].at[j], send_sem=send_sems.at[3 * t + j],
                recv_sem=recv_sems.at[3 * t + j], device_id=(px, py, c), device_id_type=MESH)

        return [copy(t, j) for t in range(n) for j in range(3)]

    def start(*refs):
        for cp in copies(*refs):
            cp.start()

    def finish(*refs):
        for cp in copies(*refs):
            cp.wait()

    shapes = [jax.ShapeDtypeStruct((3, *p.shape[1:]), p.dtype) for p in parts_b]
    return _Plan(parts_b, shapes, 3 * n, start, finish)


def _gather_plan(shards, small=()):
    n, ns = len(shards), len(small)
    per = 8

    def parts(ins, outs, send_sems, recv_sems):
        x, y, c, chips = _place()
        me = 2 * x + y
        blocks = [2 * px + py for px, py in chips]

        def rows(t, core, piece=None):
            half = ins[t].shape[0] // 2
            if piece is None:
                return pl.ds(pl.multiple_of(core * half, 16), half)
            return pl.ds(pl.multiple_of(core * half + piece * (half // 2), 16), half // 2)

        def remote(src, dst, k, device):
            return pltpu.make_async_remote_copy(src_ref=src, dst_ref=dst, send_sem=send_sems.at[k],
                                                recv_sem=recv_sems.at[k], device_id=device, device_id_type=MESH)

        def first(t, j, block, core):
            return remote(ins[t].at[rows(t, core), :], outs[t].at[block, rows(t, core), :], per * t + j, (*chips[j], c))

        def relay(t, j, block, core):
            ref = outs[t].at[block, rows(t, core, j), :]
            return remote(ref, ref, per * t + 2 + j, (*chips[j], c))

        def passed(t, k, block, core, piece=None):
            ref = outs[t].at[block, rows(t, core, piece), :]
            return remote(ref, ref, per * t + 4 + k, (x, y, 1 - c))

        def whole(s, j, block):
            return remote(ins[n + s], outs[n + s].at[block], per * n + 3 * s + j, (*chips[j], c))

        return c, me, blocks, first, relay, passed, whole

    def start(*refs):
        c, me, _, first, _, _, whole = parts(*refs)
        for t in range(n):
            for j in range(2):
                first(t, j, me, c).start()
        for s in range(ns):
            for j in range(3):
                whole(s, j, me).start()

    def middle(*refs):
        c, _, blocks, first, relay, passed, _ = parts(*refs)
        for t in range(n):
            for j in range(2):
                first(t, j, blocks[j], c).wait_recv()
                passed(t, j, blocks[j], c).start()
                relay(t, 1 - j, blocks[j], c).start()

    def finish(*refs):
        c, me, blocks, first, relay, passed, whole = parts(*refs)
        for t in range(n):
            for j in range(2):
                relay(t, j, blocks[2], c).wait_recv()
                passed(t, 2 + j, blocks[2], c, j).start()
        for t in range(n):
            for j in range(2):
                passed(t, j, blocks[j], 1 - c).wait_recv()
                passed(t, 2 + j, blocks[2], 1 - c, j).wait_recv()
        for s in range(ns):
            for j in range(3):
                whole(s, j, blocks[j]).wait_recv()
        for t in range(n):
            for j in range(2):
                first(t, j, me, c).wait_send()
                relay(t, 1 - j, blocks[j], c).wait_send()
                passed(t, j, blocks[j], c).wait_send()
                passed(t, 2 + j, blocks[2], c, j).wait_send()
        for s in range(ns):
            for j in range(3):
                whole(s, j, me).wait_send()

    arrays = [*shards, *small]
    shapes = [jax.ShapeDtypeStruct((N_CHIPS, *a.shape), a.dtype) for a in arrays]
    return _Plan(arrays, shapes, per * n + 3 * ns, start, finish, middle)


def _run_comm(plan, name):
    k = len(plan.arrays)
    in_specs, out_specs, out_shape, sems = plan.specs()

    def body(*refs):
        cr = (refs[:k], refs[k:k + len(out_shape)], refs[-2], refs[-1])
        plan.start(*cr)
        if plan.middle is not None:
            plan.middle(*cr)
        plan.finish(*cr)

    return _pcall(body, name=name, in_specs=in_specs, out_specs=out_specs, out_shape=out_shape,
                  input_output_aliases=plan.aliases, scratch_shapes=sems)(*plan.arrays)


def _carried(plan, in_specs, out_specs, out_shape, scratch):
    aliases = {}
    if plan is not None:
        p_in, p_out, p_shape, p_sems = plan.specs()
        aliases = {len(in_specs) + i: len(out_specs) + j for i, j in plan.aliases.items()}
        in_specs, out_specs = in_specs + p_in, out_specs + p_out
        out_shape, scratch = out_shape + p_shape, scratch + p_sems
    return dict(in_specs=in_specs, out_specs=out_specs, out_shape=out_shape, scratch_shapes=scratch,
                input_output_aliases=aliases)


def _unpack(refs, n_in, n_out, plan):
    k_in = len(plan.arrays) if plan else 0
    k_out = len(plan.out_shapes) if plan else 0
    ins = refs[:n_in]
    outs = refs[n_in + k_in:n_in + k_in + n_out]
    rest = refs[n_in + k_in + n_out + k_out:]
    if plan is None:
        return ins, outs, rest, None
    cr = (refs[n_in:n_in + k_in], refs[n_in + k_in + n_out:n_in + k_in + n_out + k_out], rest[-2], rest[-1])
    return ins, outs, rest[:-2], cr


def _hook(plan, cr, which, cond):
    fn = getattr(plan, which) if plan is not None else None
    if fn is not None:
        pl.when(cond)(lambda: fn(*cr))


def _join_plan(shards):
    n = len(shards)

    def copy(ins, outs, send_sems, recv_sems, t, core):
        x, y, c, _ = _place()
        half = ins[t].shape[0] // 2
        rows = pl.ds(pl.multiple_of(core * half, 8), half)
        return pltpu.make_async_remote_copy(
            src_ref=ins[t].at[rows, :], dst_ref=outs[t].at[rows, :], send_sem=send_sems.at[t],
            recv_sem=recv_sems.at[t], device_id=(x, y, 1 - c), device_id_type=MESH)

    def start(*refs):
        c = lax.axis_index("c")
        for t in range(n):
            copy(*refs, t, c).start()

    def finish(*refs):
        c = lax.axis_index("c")
        for t in range(n):
            copy(*refs, t, 1 - c).wait_recv()
        for t in range(n):
            copy(*refs, t, c).wait_send()

    shapes = [jax.ShapeDtypeStruct(s.shape, s.dtype) for s in shards]
    return _Plan(shards, shapes, n, start, finish, aliases={t: t for t in range(n)})


def _all_gather_plan(vec):
    def parts(ins, outs, send_sems, recv_sems):
        x, y, c, _ = _place()
        me = 4 * x + 2 * y + c
        rel = [((k >> 2) & 1, (k >> 1) & 1, k & 1) for k in range(1, N_DEV)]

        def peer(k):
            fx, fy, fc = rel[k]
            return (x ^ fx, y ^ fy, c ^ fc)

        def copy(k, dev):
            return pltpu.make_async_remote_copy(
                src_ref=ins[0], dst_ref=outs[0].at[dev], send_sem=send_sems.at[k], recv_sem=recv_sems.at[k],
                device_id=peer(k), device_id_type=MESH)

        mine = pltpu.make_async_copy(ins[0], outs[0].at[me], send_sems.at[N_DEV - 1])
        return me, peer, copy, mine

    def start(*refs):
        me, _, copy, mine = parts(*refs)
        mine.start()
        for k in range(N_DEV - 1):
            copy(k, me).start()

    def finish(*refs):
        me, peer, copy, mine = parts(*refs)
        for k in range(N_DEV - 1):
            px, py, pc = peer(k)
            copy(k, 4 * px + 2 * py + pc).wait_recv()
        for k in range(N_DEV - 1):
            copy(k, me).wait_send()
        mine.wait()

    return _Plan([vec], [jax.ShapeDtypeStruct((N_DEV, *vec.shape), vec.dtype)], N_DEV, start, finish)


def _sum_devices(blocks, name):
    def body(b_ref, o_ref):
        total = b_ref[0]
        for dev in range(1, N_DEV):
            total = total + b_ref[dev]
        o_ref[...] = total

    return _pcall(body, name=name, in_specs=[pl.BlockSpec(memory_space=pltpu.VMEM)],
                  out_specs=pl.BlockSpec(memory_space=pltpu.VMEM),
                  out_shape=jax.ShapeDtypeStruct(blocks.shape[1:], F32))(blocks)


TOKEN_TILE = 512
PROJ_TOKEN_TILE = 1024
OUT_PROJ_TOKEN_TILE = 2048
ADAMW_ROW_BLOCKS = 4
LARGE_VMEM_LIMIT = 62 * 1024 * 1024
DW_TOKEN_TILE = 2048
DW_ROW_SPLIT = 2
MXU_COLS = 256
DH_GROUP = 6


def _chunks(n):
    out, c0 = [], 0
    while c0 < n:
        size = min(MXU_COLS, n - c0)
        out.append((c0, size))
        c0 += size
    return out


def _load_weights(hbm_refs, vmem_refs, sems):
    copies = [pltpu.make_async_copy(h, v, sems.at[k]) for k, (h, v) in enumerate(zip(hbm_refs, vmem_refs))]
    for cp in copies:
        cp.start()
    for cp in copies:
        cp.wait()


def _ffn_fwd(x, g, wgt, wut, wd, name, plan=None, head=None, pre=None):
    T, D = x.shape
    F = wgt.shape[0]
    tm = min(T, TOKEN_TILE)
    ni = T // tm
    n_head = 2 if head is not None else 0
    n_pre = 1 if pre is not None else 0

    def body(*refs):
        ins, outs, scratch, cr = _unpack(refs, 5 + n_head + 2 * n_pre, 5 + n_head + n_pre, plan)
        x_ref, g_ref, wg_hbm, wu_hbm, wd_hbm = ins[:5]
        xo_ref, h_ref, gate_ref, up_ref, act_ref = outs[:5]
        wg_ref, wu_ref, wd_ref, sems = scratch
        i = pl.program_id(0)
        _hook(plan, cr, "start", i == 0)

        @pl.when(i == 0)
        def _():
            _load_weights((wg_hbm, wu_hbm, wd_hbm), (wg_ref, wu_ref, wd_ref), sems)

        if pre is not None:
            a_ref, w_ref = ins[5 + n_head:]
            x_ref = outs[5 + n_head]
            x_ref[...] = ins[0][...] + _dot(a_ref[...], w_ref[...])
        xv = x_ref[...]
        h = ((xv * _rms_inv(xv)) * g_ref[...]).astype(BF16)
        h_ref[...] = h
        for c0, size in _chunks(F):
            gate = _dot_nt(h, wg_ref[c0:c0 + size, :])
            up = _dot_nt(h, wu_ref[c0:c0 + size, :])
            gate_ref[:, c0:c0 + size] = gate.astype(BF16)
            up_ref[:, c0:c0 + size] = up.astype(BF16)
            act_ref[:, c0:c0 + size] = (gate * jax.nn.sigmoid(gate) * up).astype(BF16)
        y = x_ref[...] + FFN_RES_SCALE * _dot(act_ref[...], wd_ref[...])
        if head is None:
            xo_ref[...] = y
        else:
            gf_ref, t_ref = ins[5:7]
            dgf_ref, loss_ref = outs[5:7]

            @pl.when(i == 0)
            def _():
                dgf_ref[...] = jnp.zeros_like(dgf_ref)
                loss_ref[...] = jnp.zeros_like(loss_ref)

            gf = gf_ref[...]
            diff = (y * _rms_inv(y)) * gf - t_ref[...]
            loss_ref[...] += 0.5 * jnp.sum(jnp.mean(diff * diff, axis=-1, keepdims=True))
            dy, dgf = _norm_bwd(diff * (1.0 / D), y, gf)
            xo_ref[...] = dy
            dgf_ref[...] += dgf
        _hook(plan, cr, "middle", i == ni // 2)
        _hook(plan, cr, "finish", i == ni - 1)

    const = lambda shape: pl.BlockSpec(shape, lambda i: (0, 0))
    rows = lambda width: pl.BlockSpec((tm, width), lambda i: (i, 0))
    in_specs = [rows(D), const((1, D)), ANY_SPEC, ANY_SPEC, ANY_SPEC]
    out_specs = [rows(D), rows(D), rows(F), rows(F), rows(F)]
    out_shape = [jax.ShapeDtypeStruct((T, D), F32), jax.ShapeDtypeStruct((T, D), BF16),
                 jax.ShapeDtypeStruct((T, F), BF16), jax.ShapeDtypeStruct((T, F), BF16), jax.ShapeDtypeStruct((T, F), BF16)]
    if head is not None:
        in_specs += [const((1, D)), rows(D)]
        out_specs += [const((1, D)), const((1, LANES))]
        out_shape += [jax.ShapeDtypeStruct((1, D), F32), jax.ShapeDtypeStruct((1, LANES), F32)]
    if pre is not None:
        in_specs += [rows(pre[0].shape[1]), const(pre[1].shape)]
        out_specs += [rows(D)]
        out_shape += [jax.ShapeDtypeStruct((T, D), F32)]
    io = _carried(plan, in_specs, out_specs, out_shape,
                  [pltpu.VMEM((F, D), BF16), pltpu.VMEM((F, D), BF16), pltpu.VMEM((F, D), BF16),
                   pltpu.SemaphoreType.DMA((3,))])
    return _pcall(
        body, name=name, grid=(ni,), compiler_params=_params(1, VMEM_LIMIT if pre is None else LARGE_VMEM_LIMIT), **io,
    )(x, g, wgt, wut, wd, *(head or ()), *(pre or ()), *(plan.arrays if plan else ()))


def _ffn_bwd(dy, x, g, gate, up, wgt, wut, wd, name):
    T, D = x.shape
    F = wgt.shape[0]
    tm = min(T, TOKEN_TILE)
    ni = T // tm

    def body(dy_ref, x_ref, g_ref, gate_ref, up_ref, wg_hbm, wu_hbm, wd_hbm,
             dx_ref, dyb_ref, dgate_ref, dup_ref, dg_ref, wg_ref, wu_ref, wd_ref, sems):
        @pl.when(pl.program_id(0) == 0)
        def _():
            _load_weights((wg_hbm, wu_hbm, wd_hbm), (wg_ref, wu_ref, wd_ref), sems)
            dg_ref[...] = jnp.zeros_like(dg_ref)

        dyb = (FFN_RES_SCALE * dy_ref[...]).astype(BF16)
        dyb_ref[...] = dyb
        dh, group_g, group_u, row0 = None, [], [], 0
        chunks = _chunks(F)
        for k, (c0, size) in enumerate(chunks):
            dact = _dot_nt(dyb, wd_ref[c0:c0 + size, :])
            gt = gate_ref[:, c0:c0 + size].astype(F32)
            u = up_ref[:, c0:c0 + size].astype(F32)
            sig = jax.nn.sigmoid(gt)
            dup = (dact * (gt * sig)).astype(BF16)
            dgate = (dact * u * (sig * (1.0 + gt * (1.0 - sig)))).astype(BF16)
            dup_ref[:, c0:c0 + size] = dup
            dgate_ref[:, c0:c0 + size] = dgate
            group_g.append(dgate)
            group_u.append(dup)
            if len(group_g) == DH_GROUP or k == len(chunks) - 1:
                rows = slice(row0, c0 + size)
                part = (_dot(jnp.concatenate(group_g, axis=1), wg_ref[rows, :])
                        + _dot(jnp.concatenate(group_u, axis=1), wu_ref[rows, :]))
                dh = part if dh is None else dh + part
                group_g, group_u, row0 = [], [], c0 + size
        dxn, dg = _norm_bwd(dh, x_ref[...], g_ref[...])
        dx_ref[...] = dy_ref[...] + dxn
        dg_ref[...] += dg

    return _pcall(
        body, name=name, grid=(ni,),
        in_specs=[pl.BlockSpec((tm, D), lambda i: (i, 0)), pl.BlockSpec((tm, D), lambda i: (i, 0)),
                  pl.BlockSpec((1, D), lambda i: (0, 0)),
                  pl.BlockSpec((tm, F), lambda i: (i, 0)), pl.BlockSpec((tm, F), lambda i: (i, 0)),
                  ANY_SPEC, ANY_SPEC, ANY_SPEC],
        out_specs=[pl.BlockSpec((tm, D), lambda i: (i, 0)), pl.BlockSpec((tm, D), lambda i: (i, 0)),
                   pl.BlockSpec((tm, F), lambda i: (i, 0)), pl.BlockSpec((tm, F), lambda i: (i, 0)),
                   pl.BlockSpec((1, D), lambda i: (0, 0))],
        out_shape=[jax.ShapeDtypeStruct((T, D), F32), jax.ShapeDtypeStruct((T, D), BF16),
                   jax.ShapeDtypeStruct((T, F), BF16), jax.ShapeDtypeStruct((T, F), BF16),
                   jax.ShapeDtypeStruct((1, D), F32)],
        scratch_shapes=[pltpu.VMEM((F, D), BF16), pltpu.VMEM((F, D), BF16), pltpu.VMEM((F, D), BF16),
                        pltpu.SemaphoreType.DMA((3,))],
        compiler_params=_params(1, LARGE_VMEM_LIMIT),
    )(dy, x, g, gate, up, wgt, wut, wd)


def _matmul_tn(a, b, row_split, name, plan=None):
    T, n1 = a.shape
    n2 = b.shape[1]
    tn = n1 // row_split
    tk = min(T, DW_TOKEN_TILE)
    nk = T // tk

    def body(*refs):
        (a_ref, b_ref), (o_ref, ob_ref), _, cr = _unpack(refs, 2, 2, plan)
        j = pl.program_id(0)
        k = pl.program_id(1)
        _hook(plan, cr, "start", jnp.logical_and(j == 0, k == 0))

        @pl.when(k == 0)
        def _():
            o_ref[...] = jnp.zeros_like(o_ref)

        o_ref[...] += _dot_tn(a_ref[...], b_ref[...])

        @pl.when(k == nk - 1)
        def _():
            ob_ref[...] = o_ref[...].astype(BF16)

        _hook(plan, cr, "finish", jnp.logical_and(j == row_split - 1, k == nk - 1))

    io = _carried(
        plan,
        [pl.BlockSpec((tk, tn), lambda j, k: (k, j)), pl.BlockSpec((tk, n2), lambda j, k: (k, 0))],
        [pl.BlockSpec((tn, n2), lambda j, k: (j, 0)), pl.BlockSpec((tn, n2), lambda j, k: (j, 0))],
        [jax.ShapeDtypeStruct((n1, n2), F32), jax.ShapeDtypeStruct((n1, n2), BF16)], [])
    return _pcall(
        body, name=name, grid=(row_split, nk), compiler_params=_params(2), **io,
    )(a, b, *(plan.arrays if plan else ()))


def _norm_matmul(x, g, wt, tab, name):
    T, D = x.shape
    n = wt.shape[0]
    tm = min(T, PROJ_TOKEN_TILE)

    def body(x_ref, g_ref, w_ref, tab_ref, z_ref, h_ref):
        xv = x_ref[...]
        h = ((xv * _rms_inv(xv)) * g_ref[...]).astype(BF16)
        h_ref[...] = h
        z = _dot_nt(h, w_ref[...])
        z_ref[:, 0:Z_Q] = z[:, 0:Z_Q]
        tab_v = tab_ref[...]
        for c0 in range(Z_Q, Z_V, LANES):
            z_ref[:, c0:c0 + LANES] = _rot(z[:, c0:c0 + LANES], tab_v)
        z_ref[:, Z_V:Z_END] = z[:, Z_V:Z_END]

    return _pcall(
        body, name=name, grid=(T // tm,),
        in_specs=[pl.BlockSpec((tm, D), lambda i: (i, 0)), pl.BlockSpec((1, D), lambda i: (0, 0)),
                  pl.BlockSpec((n, D), lambda i: (0, 0)), pl.BlockSpec((tm, 3 * LANES), lambda i: (i, 0))],
        out_specs=[pl.BlockSpec((tm, n), lambda i: (i, 0)), pl.BlockSpec((tm, D), lambda i: (i, 0))],
        out_shape=[jax.ShapeDtypeStruct((T, n), F32), jax.ShapeDtypeStruct((T, D), BF16)],
        compiler_params=_params(1),
    )(x, g, wt, tab)


def _matmul_nt(dx, w, name):
    T, D = dx.shape
    kdim = w.shape[0]
    tm = min(T, OUT_PROJ_TOKEN_TILE)

    def body(dx_ref, w_ref, dy_ref, dxb_ref):
        dxb = dx_ref[...].astype(BF16)
        dxb_ref[...] = dxb
        dy_ref[...] = _dot_nt(dxb, w_ref[...])

    return _pcall(
        body, name=name, grid=(T // tm,),
        in_specs=[pl.BlockSpec((tm, D), lambda i: (i, 0)), pl.BlockSpec((kdim, D), lambda i: (0, 0))],
        out_specs=[pl.BlockSpec((tm, kdim), lambda i: (i, 0)), pl.BlockSpec((tm, D), lambda i: (i, 0))],
        out_shape=[jax.ShapeDtypeStruct((T, kdim), F32), jax.ShapeDtypeStruct((T, D), BF16)],
        compiler_params=_params(1),
    )(dx, w)


def _matmul_norm_bwd(dz, wt, x, g, dres, name):
    T, D = x.shape
    n = dz.shape[1]
    tm = min(T, PROJ_TOKEN_TILE)

    def body(dz_ref, w_ref, x_ref, g_ref, dres_ref, dx_ref, dg_ref):
        @pl.when(pl.program_id(0) == 0)
        def _():
            dg_ref[...] = jnp.zeros_like(dg_ref)

        dh = _dot(dz_ref[...], w_ref[...])
        dxn, dg = _norm_bwd(dh, x_ref[...], g_ref[...])
        dx_ref[...] = dres_ref[...] + dxn
        dg_ref[...] += dg

    return _pcall(
        body, name=name, grid=(T // tm,),
        in_specs=[pl.BlockSpec((tm, n), lambda i: (i, 0)), pl.BlockSpec((n, D), lambda i: (0, 0)),
                  pl.BlockSpec((tm, D), lambda i: (i, 0)), pl.BlockSpec((1, D), lambda i: (0, 0)),
                  pl.BlockSpec((tm, D), lambda i: (i, 0))],
        out_specs=[pl.BlockSpec((tm, D), lambda i: (i, 0)), pl.BlockSpec((1, D), lambda i: (0, 0))],
        out_shape=[jax.ShapeDtypeStruct((T, D), F32), jax.ShapeDtypeStruct((1, D), F32)],
        compiler_params=_params(1),
    )(dz, wt, x, g, dres)


Z_Q = 3 * CONV_WIDTH
Z_K = Z_Q + N_Q_HEADS * HEAD_DIM
Z_V = Z_K + LANES
Z_END = Z_V + LANES


def _rope_tables(T, name, plan):
    half = ROT_DIM // 2
    inv_freq = ROPE_THETA ** (-jnp.arange(0, ROT_DIM, 2, dtype=F32) / ROT_DIM)
    ang = inv_freq[:, None] * jnp.arange(T, dtype=F32)[None, :]
    cos_sin = jnp.concatenate([jnp.cos(ang), jnp.sin(ang)], axis=0)
    select = np.zeros((2 * half, 3 * LANES), np.float32)
    const = np.zeros((1, 3 * LANES), np.float32)
    for lane in range(LANES):
        d = lane % HEAD_DIM
        if d < half:
            select[d, lane] = 1.0
            select[half + d, LANES + lane] = -1.0
        elif d < ROT_DIM:
            select[d - half, lane] = 1.0
            select[d, 2 * LANES + lane] = 1.0
        else:
            const[0, lane] = 1.0
    tm = PROJ_TOKEN_TILE
    ni = T // tm

    def body(*refs):
        (cs_ref, sel_ref, const_ref), (tab_ref,), _, cr = _unpack(refs, 3, 1, plan)
        i = pl.program_id(0)
        _hook(plan, cr, "start", i == 0)
        tab_ref[...] = lax.dot_general(cs_ref[...], sel_ref[...], (((0,), (0,)), ((), ())),
                                       precision=lax.Precision.HIGHEST, preferred_element_type=F32) + const_ref[...]
        _hook(plan, cr, "middle", i == ni - 1)
        _hook(plan, cr, "finish", i == ni - 1)

    io = _carried(
        plan,
        [pl.BlockSpec((2 * half, tm), lambda i: (0, i)), pl.BlockSpec((2 * half, 3 * LANES), lambda i: (0, 0)),
         pl.BlockSpec((1, 3 * LANES), lambda i: (0, 0))],
        [pl.BlockSpec((tm, 3 * LANES), lambda i: (i, 0))], [jax.ShapeDtypeStruct((T, 3 * LANES), F32)], [])
    res = _pcall(body, name=name, grid=(ni,), compiler_params=_params(1), **io)(
        cos_sin, jnp.asarray(select), jnp.asarray(const), *plan.arrays)
    return res[0], res[1:]


def _tab3(tab):
    return tab[:, 0:LANES], tab[:, LANES:2 * LANES], tab[:, 2 * LANES:3 * LANES]


def _rot(x, tab):
    c, s1, s2 = _tab3(tab)
    return x * c + pltpu.roll(x, LANES - ROT_DIM // 2, 1) * s1 + pltpu.roll(x, ROT_DIM // 2, 1) * s2


def _rot_t(d, tab):
    c, s1, s2 = _tab3(tab)
    return d * c + pltpu.roll(d * s1, ROT_DIM // 2, 1) + pltpu.roll(d * s2, LANES - ROT_DIM // 2, 1)


def _head_pads(a):
    lo = lax.broadcasted_iota(jnp.int32, a.shape, 1) < HEAD_DIM
    nat0 = jnp.where(lo, a, 0.0)
    nat1 = jnp.where(lo, 0.0, a)
    return {
        (0, 0): nat0.astype(BF16), (0, 1): pltpu.roll(nat0, HEAD_DIM, 1).astype(BF16),
        (1, 0): pltpu.roll(nat1, HEAD_DIM, 1).astype(BF16), (1, 1): nat1.astype(BF16),
    }


def _from_pads(even, odd, kv):
    lo = lax.broadcasted_iota(jnp.int32, even.shape, 1) < HEAD_DIM
    if kv == 0:
        return jnp.where(lo, even + pltpu.roll(odd, HEAD_DIM, 1), 0.0)
    return jnp.where(lo, 0.0, pltpu.roll(even, HEAD_DIM, 1) + odd)


N_GROUPS = 4


def _group_head(g, r):
    kv, par = divmod(g, 2)
    return 2 * (2 * kv + r) + par


def _window_mask_t(has_prev):
    jj = lax.broadcasted_iota(jnp.int32, (2 * BLOCK, 2 * BLOCK), 0)
    ii = lax.broadcasted_iota(jnp.int32, (2 * BLOCK, 2 * BLOCK), 1) & (BLOCK - 1)
    rel = jj - BLOCK - ii
    return (rel <= 0) & (rel > -BLOCK) & ((jj >= BLOCK) | has_prev)


def _sink_row(sink_ref, g):
    lane = lax.broadcasted_iota(jnp.int32, (1, 2 * BLOCK), 1)
    return jnp.where(lane < BLOCK, sink_ref[0, _group_head(g, 0)], sink_ref[0, _group_head(g, 1)])


def _attn_probs_t(q2, kp, mask, sink_ref):
    out = []
    for kv in range(2):
        q_st = jnp.concatenate([q2[2 * kv], q2[2 * kv + 1]], axis=0)
        for par in range(2):
            s = jnp.where(mask, _dot_nt(kp[(kv, par)], q_st), MASK_VALUE)
            sink = _sink_row(sink_ref, 2 * kv + par)
            m = jnp.maximum(jnp.max(s, axis=0, keepdims=True), sink)
            p = jnp.exp(s - m)
            esink = jnp.exp(sink - m)
            rden = 1.0 / (jnp.sum(p, axis=0, keepdims=True) + esink)
            out.append((p * rden, esink * rden))
    return out


def _conv_taps(cg, u, cg_prev, u_prev, has_prev):
    vv = cg * u
    halo = jnp.where(has_prev, cg_prev * u_prev, 0.0)
    ext = jnp.concatenate([halo, vv], axis=0)
    rows = ext.shape[0]
    vv1 = pltpu.roll(ext, 1, 0)[8:rows]
    vv2 = pltpu.roll(ext, 2, 0)[8:rows]
    return vv, vv1, vv2


MIX_BLOCKS = 4


def _mix_core_fwd(z, conv_w, sinks, name):
    T = z.shape[0]
    rows = MIX_BLOCKS * BLOCK
    steps = T // rows
    prev_block = lambda n: jnp.maximum(MIX_BLOCKS * n - 1, 0)
    prev_rows8 = lambda n: jnp.maximum((rows // 8) * n - 1, 0)

    def body(z_ref, zkvp_ref, cgp_ref, up_ref, cw_ref, sink_ref, y_ref):
        for b in range(MIX_BLOCKS):
            r0 = b * BLOCK
            blk = slice(r0, r0 + BLOCK)
            if b == 0:
                has_prev = pl.program_id(0) > 0
                kv_prev, cg_prev, u_prev = zkvp_ref[...], cgp_ref[...], up_ref[...]
            else:
                has_prev = True
                kv_prev = z_ref[r0 - BLOCK:r0, Z_K:Z_END]
                cg_prev = z_ref[r0 - 8:r0, CONV_WIDTH:2 * CONV_WIDTH]
                u_prev = z_ref[r0 - 8:r0, 2 * CONV_WIDTH:Z_Q]
            bg = z_ref[blk, 0:CONV_WIDTH]
            vv, vv1, vv2 = _conv_taps(z_ref[blk, CONV_WIDTH:2 * CONV_WIDTH], z_ref[blk, 2 * CONV_WIDTH:Z_Q],
                                      cg_prev, u_prev, has_prev)
            conv = cw_ref[0:1, :] * vv2 + cw_ref[1:2, :] * vv1 + cw_ref[2:3, :] * vv
            y_ref[blk, 0:CONV_WIDTH] = (bg * conv).astype(BF16)

            k_all = jnp.concatenate([kv_prev[:, 0:LANES], z_ref[blk, Z_K:Z_V]], axis=0)
            v_all = jnp.concatenate([kv_prev[:, LANES:2 * LANES], z_ref[blk, Z_V:Z_END]], axis=0)
            kp = _head_pads(k_all)
            vp = _head_pads(v_all)
            q2 = [(z_ref[blk, Z_Q + LANES * c:Z_Q + LANES * (c + 1)] * ATTN_SCALE).astype(BF16)
                  for c in range(N_Q_HEADS // 2)]
            probs = _attn_probs_t(q2, kp, _window_mask_t(has_prev), sink_ref)
            for kv in range(2):
                o_t = (_dot_tn(vp[(kv, 0)], probs[2 * kv][0].astype(BF16))
                       + _dot_tn(vp[(kv, 1)], probs[2 * kv + 1][0].astype(BF16)))
                for r in range(2):
                    c = 2 * kv + r
                    y_ref[blk, CONV_WIDTH + LANES * c:CONV_WIDTH + LANES * (c + 1)] = (
                        o_t[:, BLOCK * r:BLOCK * (r + 1)].T.astype(BF16))

    return _pcall(
        body, name=name, grid=(steps,),
        in_specs=[pl.BlockSpec((rows, Z_END), lambda n: (n, 0)),
                  pl.BlockSpec((BLOCK, 2 * LANES), lambda n: (prev_block(n), Z_K // (2 * LANES))),
                  pl.BlockSpec((8, CONV_WIDTH), lambda n: (prev_rows8(n), 1)),
                  pl.BlockSpec((8, CONV_WIDTH), lambda n: (prev_rows8(n), 2)),
                  pl.BlockSpec((3, CONV_WIDTH), lambda n: (0, 0)),
                  pl.BlockSpec(memory_space=pltpu.SMEM)],
        out_specs=pl.BlockSpec((rows, 2 * CONV_WIDTH), lambda n: (n, 0)),
        out_shape=jax.ShapeDtypeStruct((T, 2 * CONV_WIDTH), BF16),
        compiler_params=_params(1),
    )(z, z, z, z, conv_w, sinks)


def _mix_core_bwd(z, dy, tab, conv_w, sinks, name):
    T = z.shape[0]
    nsub = MIX_BLOCKS
    rows = nsub * BLOCK
    steps = T // rows
    last = slice(rows - BLOCK, rows)
    cur = lambda n: jnp.minimum(n, steps - 1)
    prev_block = lambda n: jnp.maximum(nsub * cur(n) - 1, 0)
    prev_rows8 = lambda n: jnp.maximum((rows // 8) * cur(n) - 1, 0)
    next_rows8 = lambda n: jnp.minimum((rows // 8) * (cur(n) + 1), T // 8 - 1)

    def body(z_ref, zkvp_ref, cgp_ref, up_ref, bgn_ref, dy_ref, dyn_ref, tab_ref, tabp_ref, cw_ref, sink_ref,
             dz_ref, dcw_ref, dsk_ref, held_ref, kv_ref):
        n = pl.program_id(0)

        @pl.when(n == 0)
        def _():
            held_ref[...] = jnp.zeros_like(held_ref)
            kv_ref[...] = jnp.zeros_like(kv_ref)
            dcw_ref[...] = jnp.zeros_like(dcw_ref)
            dsk_ref[...] = jnp.zeros_like(dsk_ref)

        def emit_held():
            dz_ref[:, 0:Z_K] = held_ref[:, 0:Z_K]
            if nsub > 1:
                dz_ref[0:rows - BLOCK, Z_K:Z_END] = held_ref[0:rows - BLOCK, Z_K:Z_END]

        @pl.when(n < steps)
        def _():
            emit_held()
            w0, w1, w2 = cw_ref[0:1, :], cw_ref[1:2, :], cw_ref[2:3, :]
            dk_open, dv_open = kv_ref[:, 0:LANES], kv_ref[:, LANES:2 * LANES]
            for b in range(nsub):
                r0 = b * BLOCK
                blk = slice(r0, r0 + BLOCK)
                before = slice(r0 - BLOCK, r0)
                after8 = slice(r0 + BLOCK, r0 + BLOCK + 8)
                if b == 0:
                    has_prev = n > 0
                    kv_prev, cg_prev, u_prev, tab_p = zkvp_ref[...], cgp_ref[...], up_ref[...], tabp_ref[...]
                else:
                    has_prev = True
                    kv_prev, tab_p = z_ref[before, Z_K:Z_END], tab_ref[before, :]
                    cg_prev = z_ref[r0 - 8:r0, CONV_WIDTH:2 * CONV_WIDTH]
                    u_prev = z_ref[r0 - 8:r0, 2 * CONV_WIDTH:Z_Q]
                if b == nsub - 1:
                    dconv_next = jnp.where(n < steps - 1, dyn_ref[...] * bgn_ref[...], 0.0)
                else:
                    dconv_next = dy_ref[after8, 0:CONV_WIDTH] * z_ref[after8, 0:CONV_WIDTH]
                bg = z_ref[blk, 0:CONV_WIDTH]
                cg = z_ref[blk, CONV_WIDTH:2 * CONV_WIDTH]
                u = z_ref[blk, 2 * CONV_WIDTH:Z_Q]
                vv, vv1, vv2 = _conv_taps(cg, u, cg_prev, u_prev, has_prev)
                dyc = dy_ref[blk, 0:CONV_WIDTH]
                dbg = dyc * (w0 * vv2 + w1 * vv1 + w2 * vv)
                dconv = dyc * bg
                ext = jnp.concatenate([dconv, dconv_next], axis=0)
                ext_rows = ext.shape[0]
                dvv = (w2 * dconv + w1 * pltpu.roll(ext, ext_rows - 1, 0)[0:BLOCK]
                       + w0 * pltpu.roll(ext, ext_rows - 2, 0)[0:BLOCK])
                dcw_ref[0:1, :] += jnp.sum(dconv * vv2, axis=0, keepdims=True)
                dcw_ref[1:2, :] += jnp.sum(dconv * vv1, axis=0, keepdims=True)
                dcw_ref[2:3, :] += jnp.sum(dconv * vv, axis=0, keepdims=True)

                tab_c = tab_ref[blk, :]
                k_all = jnp.concatenate([kv_prev[:, 0:LANES], z_ref[blk, Z_K:Z_V]], axis=0)
                v_all = jnp.concatenate([kv_prev[:, LANES:2 * LANES], z_ref[blk, Z_V:Z_END]], axis=0)
                kp = _head_pads(k_all)
                vp = _head_pads(v_all)
                chunks = range(N_Q_HEADS // 2)
                q2 = [(z_ref[blk, Z_Q + LANES * c:Z_Q + LANES * (c + 1)] * ATTN_SCALE).astype(BF16) for c in chunks]
                do2 = [dy_ref[blk, CONV_WIDTH + LANES * c:CONV_WIDTH + LANES * (c + 1)].astype(BF16) for c in chunks]
                probs = _attn_probs_t(q2, kp, _window_mask_t(has_prev), sink_ref)
                dq_chunks = []
                dk_nat = jnp.zeros((2 * BLOCK, LANES), F32)
                dv_nat = jnp.zeros((2 * BLOCK, LANES), F32)
                for kv in range(2):
                    q_st = jnp.concatenate([q2[2 * kv], q2[2 * kv + 1]], axis=0)
                    do_st = jnp.concatenate([do2[2 * kv], do2[2 * kv + 1]], axis=0)
                    dq_t = jnp.zeros((LANES, 2 * BLOCK), F32)
                    dk_par, dv_par = [], []
                    for par in range(2):
                        g = 2 * kv + par
                        pr, psink = probs[g]
                        dp = _dot_nt(vp[(kv, par)], do_st)
                        delta = jnp.sum(dp * pr, axis=0, keepdims=True)
                        ds = (pr * (dp - delta)).astype(BF16)
                        dsink = -psink * delta
                        for r in range(2):
                            h = _group_head(g, r)
                            dsk_ref[h:h + 1, :] += jnp.sum(dsink[:, BLOCK * r:BLOCK * (r + 1)])
                        dq_t = dq_t + _dot_tn(kp[(kv, par)], ds)
                        dk_par.append(_dot(ds, q_st))
                        dv_par.append(_dot(pr.astype(BF16), do_st))
                    for r in range(2):
                        dq_chunks.append(_rot_t(dq_t[:, BLOCK * r:BLOCK * (r + 1)].T * ATTN_SCALE, tab_c))
                    dk_nat = dk_nat + _from_pads(dk_par[0], dk_par[1], kv)
                    dv_nat = dv_nat + _from_pads(dv_par[0], dv_par[1], kv)

                done_ref, done = (dz_ref, last) if b == 0 else (held_ref, before)
                done_ref[done, Z_K:Z_V] = _rot_t(dk_open + dk_nat[0:BLOCK], tab_p).astype(BF16)
                done_ref[done, Z_V:Z_END] = (dv_open + dv_nat[0:BLOCK]).astype(BF16)
                dk_open, dv_open = dk_nat[BLOCK:2 * BLOCK], dv_nat[BLOCK:2 * BLOCK]
                held_ref[blk, 0:CONV_WIDTH] = dbg.astype(BF16)
                held_ref[blk, CONV_WIDTH:2 * CONV_WIDTH] = (dvv * u).astype(BF16)
                held_ref[blk, 2 * CONV_WIDTH:Z_Q] = (dvv * cg).astype(BF16)
                for c in range(N_Q_HEADS // 2):
                    held_ref[blk, Z_Q + LANES * c:Z_Q + LANES * (c + 1)] = dq_chunks[c].astype(BF16)
            kv_ref[:, 0:LANES] = dk_open
            kv_ref[:, LANES:2 * LANES] = dv_open

        @pl.when(n == steps)
        def _():
            emit_held()
            dz_ref[last, Z_K:Z_V] = _rot_t(kv_ref[:, 0:LANES], tab_ref[last, :]).astype(BF16)
            dz_ref[last, Z_V:Z_END] = kv_ref[:, LANES:2 * LANES].astype(BF16)

    return _pcall(
        body, name=name, grid=(steps + 1,),
        in_specs=[pl.BlockSpec((rows, Z_END), lambda n: (cur(n), 0)),
                  pl.BlockSpec((BLOCK, 2 * LANES), lambda n: (prev_block(n), Z_K // (2 * LANES))),
                  pl.BlockSpec((8, CONV_WIDTH), lambda n: (prev_rows8(n), 1)),
                  pl.BlockSpec((8, CONV_WIDTH), lambda n: (prev_rows8(n), 2)),
                  pl.BlockSpec((8, CONV_WIDTH), lambda n: (next_rows8(n), 0)),
                  pl.BlockSpec((rows, 2 * CONV_WIDTH), lambda n: (cur(n), 0)),
                  pl.BlockSpec((8, CONV_WIDTH), lambda n: (next_rows8(n), 0)),
                  pl.BlockSpec((rows, 3 * LANES), lambda n: (cur(n), 0)),
                  pl.BlockSpec((BLOCK, 3 * LANES), lambda n: (prev_block(n), 0)),
                  pl.BlockSpec((3, CONV_WIDTH), lambda n: (0, 0)),
                  pl.BlockSpec(memory_space=pltpu.SMEM)],
        out_specs=[pl.BlockSpec((rows, Z_END), lambda n: (jnp.maximum(n - 1, 0), 0)),
                   pl.BlockSpec((8, CONV_WIDTH), lambda n: (0, 0)), pl.BlockSpec((8, LANES), lambda n: (0, 0))],
        out_shape=[jax.ShapeDtypeStruct((T, Z_END), BF16), jax.ShapeDtypeStruct((8, CONV_WIDTH), F32),
                   jax.ShapeDtypeStruct((8, LANES), F32)],
        scratch_shapes=[pltpu.VMEM((rows, Z_END), BF16), pltpu.VMEM((BLOCK, 2 * LANES), F32)],
        compiler_params=_params(1),
    )(z, z, z, z, z, dy, dy, tab, tab, conv_w, sinks)


def _local_sums(pair, chip, place, name):
    arrays, in_specs, out_specs, out_shape = [], [], [], []
    if pair is not None:
        g, sib = pair
        blk = (1, *sib.shape[1:])
        arrays += [g, sib]
        in_specs += [pl.BlockSpec(blk, lambda q, p: (q, p[1], 0)), pl.BlockSpec(blk, lambda q, p: (q, 0, 0))]
        out_specs.append(pl.BlockSpec(blk, lambda q, p: (q, 0, 0)))
        out_shape.append(jax.ShapeDtypeStruct(sib.shape, BF16))
    if chip is not None:
        g2, sib2, recv2 = chip
        blk = (1, *sib2.shape[1:])
        arrays += [g2, sib2, recv2]
        in_specs += [pl.BlockSpec(blk, lambda q, p: (p[0], p[1], 0)), pl.BlockSpec(blk, lambda q, p: (p[0], 0, 0)),
                     pl.BlockSpec(recv2.shape, lambda q, p: (0, 0, 0))]
        out_specs.append(pl.BlockSpec(sib2.shape[1:], lambda q, p: (p[1], 0)))
        out_shape.append(jax.ShapeDtypeStruct(g2.shape[1:], F32))

    def body(place_ref, *refs):
        refs = list(refs)
        ins, outs = refs[:len(arrays)], refs[len(arrays):]
        if pair is not None:
            g_ref, sib_ref = ins[:2]
            outs[0][...] = (g_ref[...] + sib_ref[...].astype(F32)).astype(BF16)
        if chip is not None:
            g_ref, sib_ref, recv_ref = ins[-3:]

            @pl.when(pl.program_id(0) == 0)
            def _():
                total = g_ref[0] + sib_ref[0].astype(F32)
                for j in range(3):
                    total = total + recv_ref[j].astype(F32)
                outs[-1][...] = total

    return _pcall(
        body, name=name,
        grid_spec=pltpu.PrefetchScalarGridSpec(num_scalar_prefetch=1, grid=(N_CHIPS,),
                                               in_specs=in_specs, out_specs=out_specs),
        out_shape=out_shape, compiler_params=_params(1),
    )(place, *arrays)


def _adamw_math(w, g, m, v):
    m = ADAM_B1 * m + (1.0 - ADAM_B1) * g
    v = ADAM_B2 * v + (1.0 - ADAM_B2) * (g * g)
    m_hat = m / (1.0 - ADAM_B1 ** ADAM_STEP)
    v_hat = v / (1.0 - ADAM_B2 ** ADAM_STEP)
    delta = -ADAM_LR * (m_hat / (jnp.sqrt(v_hat) + ADAM_EPS) + ADAM_WD * w)
    return delta, m, v


def _adamw(ws, gs, ms, vs, row_blocks, name):
    n = len(ws)

    def body(*refs):
        w, g, m, v = refs[:n], refs[n:2 * n], refs[2 * n:3 * n], refs[3 * n:4 * n]
        d, mo, vo, go = refs[4 * n:5 * n], refs[5 * n:6 * n], refs[6 * n:7 * n], refs[7 * n:]
        for t in range(n):
            gv = g[t][...]
            delta, m_new, v_new = _adamw_math(w[t][...], gv, m[t][...], v[t][...])
            d[t][...] = delta
            mo[t][...] = m_new
            vo[t][...] = v_new
            go[t][...] = gv

    specs = [pl.BlockSpec((a.shape[0] // row_blocks, a.shape[1]), lambda i: (i, 0)) for a in ws]
    shapes = [jax.ShapeDtypeStruct(a.shape, F32) for a in ws]
    return _pcall(
        body, name=name, grid=(row_blocks,), in_specs=specs * 4, out_specs=specs * 4, out_shape=shapes * 4,
        compiler_params=_params(1),
    )(*ws, *gs, *ms, *vs)


def kernel(x, ffn1_norm, ffn1_w_gate, ffn1_w_up, ffn1_w_down, mix_norm, w_in, conv_w, attn_sinks, w_out, ffn2_norm, ffn2_w_gate, ffn2_w_up, ffn2_w_down, final_norm, loss_target, m_ffn1_norm, m_ffn1_w_gate, m_ffn1_w_up, m_ffn1_w_down, m_mix_norm, m_w_in, m_conv_w, m_attn_sinks, m_w_out, m_ffn2_norm, m_ffn2_w_gate, m_ffn2_w_up, m_ffn2_w_down, m_final_norm, v_ffn1_norm, v_ffn1_w_gate, v_ffn1_w_up, v_ffn1_w_down, v_mix_norm, v_w_in, v_conv_w, v_attn_sinks, v_w_out, v_ffn2_norm, v_ffn2_w_gate, v_ffn2_w_up, v_ffn2_w_down, v_final_norm):
    T, D = x.shape[1], x.shape[2]
    chip = (2 * lax.axis_index("x") + lax.axis_index("y")).astype(jnp.int32)
    core = lax.axis_index("c").astype(jnp.int32)
    place = jnp.stack([chip, core])
    x0 = x[0]
    target = loss_target[0]
    gf = final_norm.reshape(1, D)

    tr = lambda w: jnp.swapaxes(w[0], 0, 1)
    big = [tr(ffn1_w_gate), tr(ffn1_w_up), ffn1_w_down[0], tr(w_in), w_out[0], tr(ffn2_w_gate), tr(ffn2_w_up), ffn2_w_down[0]]
    transposed = [True, True, False, True, False, True, True, False]
    own_b = [w.astype(BF16) for w in big]

    def whole(gathered, own):
        return lax.dynamic_update_slice(gathered, own[None], (chip, 0, 0)).reshape(-1, D)

    tab, got1 = _rope_tables(T, "rope_gather_ffn1", _gather_plan(own_b[0:3]))
    wg1, wu1, wd1 = (whole(g, o) for g, o in zip(got1, own_b[0:3]))

    res = _ffn_fwd(x0, ffn1_norm, wg1, wu1, wd1, "ffn1_fwd", _gather_plan(own_b[3:8], [conv_w[0]]))
    x1, h1, gate1, up1, act1 = res[:5]
    win, wout, wg2, wu2, wd2 = (whole(g, o) for g, o in zip(res[5:10], own_b[3:8]))
    convw4 = lax.dynamic_update_slice(res[10], conv_w, (chip, 0, 0))
    convw = jnp.transpose(convw4, (1, 0, 2)).reshape(3, -1)
    z, hm = _norm_matmul(x1, mix_norm, win, tab, "mix_in_fwd")
    ymix = _mix_core_fwd(z, convw, attn_sinks, "mix_core_fwd")
    dx3, h2, gate2, up2, act2, dgf, loss_part, x2 = _ffn_fwd(x1, ffn2_norm, wg2, wu2, wd2, "ffn2_fwd",
                                                             head=(gf, target), pre=(ymix, wout))

    dx2, dyb2, dgate2, dup2, dg2 = _ffn_bwd(dx3, x2, ffn2_norm, gate2, up2, wg2, wu2, wd2, "ffn2_bwd")
    dymix, dx2b = _matmul_nt(dx2, wout, "mix_out_bwd")
    dz, dcw, dsk = _mix_core_bwd(z, dymix, tab, convw, attn_sinks, "mix_core_bwd")
    dx1, dgm = _matmul_norm_bwd(dz, win, x1, mix_norm, dx2, "mix_in_bwd")
    dx0, dyb1, dgate1, dup1, dg1 = _ffn_bwd(dx1, x0, ffn1_norm, gate1, up1, wg1, wu1, wd1, "ffn1_bwd")

    pad = lambda a: jnp.pad(a, ((0, 0), (0, LANES - a.shape[1])))
    vec = jnp.concatenate([dg1, dgm, dg2, dgf, dcw[0:3].reshape(1, -1), pad(dsk[:, 0].reshape(1, -1)),
                           pad(loss_part[:, 0:1])], axis=1)

    jobs = [("ffn2_dwg", dgate2, h2, 5), ("ffn2_dwu", dup2, h2, 6), ("ffn2_dwd", act2, dyb2, 7),
            ("ffn1_dwg", dgate1, h1, 0), ("ffn1_dwu", dup1, h1, 1), ("ffn1_dwd", act1, dyb1, 2),
            ("mix_dwin", dz, hm, 3), ("mix_dwout", ymix, dx2b, 4)]
    n_jobs = len(jobs)
    grad, grad_b, from_sib, pair_b, from_chips, half, g_big = ({} for _ in range(7))

    def stage_plans(t):
        plans, takers = [], []
        if 0 <= t - 1 < n_jobs:
            plans.append(_sibling_plan([grad_b[t - 1]]))
            takers.append((from_sib, t - 1))
        if 0 <= t - 2 < n_jobs:
            plans.append(_scatter_plan([pair_b[t - 2]]))
            takers.append((from_chips, t - 2))
        if 0 <= t - 3 < n_jobs:
            plans.append(_join_plan([half[t - 3]]))
            takers.append((g_big, jobs[t - 3][3]))
        return plans, takers

    def after_stage(t, landed, takers):
        for (store, key), arr in zip(takers, landed):
            store[key] = arr
        pair = (grad[t - 1], from_sib[t - 1]) if 0 <= t - 1 < n_jobs else None
        chip = (grad[t - 2], from_sib[t - 2], from_chips[t - 2]) if 0 <= t - 2 < n_jobs else None
        if pair or chip:
            sums = list(_local_sums(pair, chip, place, f"local_sums_{t}"))
            if pair:
                pair_b[t - 1] = sums.pop(0)
            if chip:
                half[t - 2] = sums.pop(0)

    for t, (name_, a, b, _) in enumerate(jobs):
        plans, takers = stage_plans(t)
        if t == 0:
            plans.append(_all_gather_plan(jnp.pad(vec, ((0, 7), (0, 0)))))
        res = _matmul_tn(a, b, DW_ROW_SPLIT, name_, _merge_plans(plans))
        grad[t], grad_b[t] = (r.reshape(N_CHIPS, -1, D) for r in res[:2])
        landed = list(res[2:])
        if t == 0:
            vec_blocks = landed.pop()
        after_stage(t, landed, takers)

    ws = big
    ms = [tr(m_ffn1_w_gate), tr(m_ffn1_w_up), m_ffn1_w_down[0], tr(m_w_in), m_w_out[0], tr(m_ffn2_w_gate), tr(m_ffn2_w_up), m_ffn2_w_down[0]]
    vs = [tr(v_ffn1_w_gate), tr(v_ffn1_w_up), v_ffn1_w_down[0], tr(v_w_in), v_w_out[0], tr(v_ffn2_w_gate), tr(v_ffn2_w_up), v_ffn2_w_down[0]]
    for t in range(n_jobs, n_jobs + 3):
        plans, takers = stage_plans(t)
        after_stage(t, _run_comm(_merge_plans(plans), f"grads_tail_{t - n_jobs}"), takers)
    upd = {}
    for name_, idx in (("adamw_a", [0, 1, 2, 4]), ("adamw_b", [3, 5, 6, 7])):
        k = len(idx)
        res = _adamw([ws[i] for i in idx], [g_big[i] for i in idx], [ms[i] for i in idx], [vs[i] for i in idx], ADAMW_ROW_BLOCKS, name_)
        for j, i in enumerate(idx):
            upd[i] = (res[j], res[k + j], res[2 * k + j])
            g_big[i] = res[3 * k + j]

    total = _sum_devices(vec_blocks, "small_sum")[0:1]
    g_n1, g_nm, g_n2, g_nf = (total[:, k * D:(k + 1) * D] for k in range(4))
    cw_full = total[:, 4 * D:4 * D + 3 * CONV_WIDTH].reshape(3, CONV_WIDTH)
    cq = CONV_WIDTH // N_CHIPS
    g_cw = lax.dynamic_slice(cw_full, (0, chip * cq), (3, cq))
    off = 4 * D + 3 * CONV_WIDTH
    g_sk = total[:, off:off + N_Q_HEADS]
    loss = total[0, off + LANES]

    sw = [ffn1_norm, mix_norm, conv_w[0], attn_sinks, ffn2_norm, gf]
    sg = [g_n1, g_nm, g_cw, g_sk, g_n2, g_nf]
    sm = [m_ffn1_norm, m_mix_norm, m_conv_w[0], m_attn_sinks, m_ffn2_norm, m_final_norm.reshape(1, D)]
    sv = [v_ffn1_norm, v_mix_norm, v_conv_w[0], v_attn_sinks, v_ffn2_norm, v_final_norm.reshape(1, D)]
    sres = _adamw(sw, sg, sm, sv, 1, "adamw_small")
    supd = [(sres[j], sres[6 + j], sres[12 + j]) for j in range(6)]

    order = [("s", 0), ("b", 0), ("b", 1), ("b", 2), ("s", 1), ("b", 3), ("s", 2), ("s", 3), ("b", 4),
             ("s", 4), ("b", 5), ("b", 6), ("b", 7), ("s", 5)]

    def leaf(kind, i, which):
        if kind == "b":
            a = g_big[i] if which == 0 else upd[i][which - 1]
            return (jnp.swapaxes(a, 0, 1) if transposed[i] else a)[None]
        a = sg[i] if which == 0 else supd[i][which - 1]
        if i == 2:
            return a[None]
        if i == 5:
            return a.reshape(D)
        return a

    outs = [loss, dx0[None]]
    for which in range(4):
        outs += [leaf(kind, i, which) for kind, i in order]
    return tuple(outs)
```

```python
import jax
import jax.numpy as jnp
import numpy as np
from jax import lax
from jax.experimental import pallas as pl
from jax.experimental.pallas import tpu as pltpu

F32 = jnp.float32
BF16 = jnp.bfloat16
MESH = pl.DeviceIdType.MESH

CONV_WIDTH = 512
N_Q_HEADS = 8
HEAD_DIM = 64
BLOCK = 128
ROPE_THETA = 500000.0
ROT_DIM = 16
RMS_EPS = 1e-5
MASK_VALUE = -1e30
ATTN_SCALE = HEAD_DIM ** -0.5
FFN_RES_SCALE = 0.5
ADAM_LR = 0.001
ADAM_B1 = 0.9
ADAM_B2 = 0.999
ADAM_EPS = 1e-08
ADAM_WD = 0.01
ADAM_STEP = 10

N_CHIPS = 4
N_DEV = 8
LANES = 128
VMEM_LIMIT = 56 * 1024 * 1024

_pcall = pl.pallas_call
HBM_SPEC = pl.BlockSpec(memory_space=pltpu.HBM)
ANY_SPEC = pl.BlockSpec(memory_space=pl.ANY)


def _params(n_axes, vmem=VMEM_LIMIT):
    return pltpu.CompilerParams(dimension_semantics=("arbitrary",) * n_axes, vmem_limit_bytes=vmem)


def _dot(a, b):
    return jnp.dot(a, b, preferred_element_type=F32)


def _dot_nt(a, b):
    return lax.dot_general(a, b, (((1,), (1,)), ((), ())), preferred_element_type=F32)


def _dot_tn(a, b):
    return lax.dot_general(a, b, (((0,), (0,)), ((), ())), preferred_element_type=F32)


def _rms_inv(x):
    return lax.rsqrt(jnp.mean(x * x, axis=-1, keepdims=True) + RMS_EPS)


def _norm_bwd(dh, x, g):
    inv = _rms_inv(x)
    xhat = x * inv
    dg = jnp.sum(dh * xhat, axis=0, keepdims=True)
    dxhat = dh * g
    dx = inv * (dxhat - xhat * jnp.mean(dxhat * xhat, axis=-1, keepdims=True))
    return dx, dg


def _place():
    x, y, c = lax.axis_index("x"), lax.axis_index("y"), lax.axis_index("c")
    chips = [(1 - x, y), (x, 1 - y), (1 - x, 1 - y)]
    return x, y, c, chips


class _Plan:
    def __init__(self, arrays, out_shapes, n_sems, start, finish, middle=None, aliases=None):
        self.arrays, self.out_shapes, self.n_sems = list(arrays), list(out_shapes), n_sems
        self.start, self.finish, self.middle = start, finish, middle
        self.aliases = dict(aliases or {})

    def specs(self):
        k = len(self.arrays)
        sems = [pltpu.SemaphoreType.DMA((self.n_sems,)), pltpu.SemaphoreType.DMA((self.n_sems,))]
        return [HBM_SPEC] * k, [HBM_SPEC] * len(self.out_shapes), self.out_shapes, sems


class _SemSlice:
    def __init__(self, ref, offset):
        self.ref, self.offset = ref, offset

    @property
    def at(self):
        return self

    def __getitem__(self, k):
        return self.ref.at[k + self.offset]


def _merge_plans(plans):
    plans = [p for p in plans if p is not None]
    if len(plans) <= 1:
        return plans[0] if plans else None
    arrays, shapes, aliases, spans, n_sems = [], [], {}, [], 0
    for p in plans:
        a0, o0 = len(arrays), len(shapes)
        spans.append((a0, a0 + len(p.arrays), o0, o0 + len(p.out_shapes), n_sems))
        aliases.update({a0 + i: o0 + j for i, j in p.aliases.items()})
        arrays += p.arrays
        shapes += p.out_shapes
        n_sems += p.n_sems

    def run(which):
        def fn(ins, outs, send_sems, recv_sems):
            for p, (a0, a1, o0, o1, s0) in zip(plans, spans):
                part = getattr(p, which)
                if part is not None:
                    part(ins[a0:a1], outs[o0:o1], _SemSlice(send_sems, s0), _SemSlice(recv_sems, s0))
        return fn

    middle = run("middle") if any(p.middle is not None for p in plans) else None
    return _Plan(arrays, shapes, n_sems, run("start"), run("finish"), middle, aliases)


def _sibling_plan(grads_b):
    n = len(grads_b)

    def copies(ins, outs, send_sems, recv_sems):
        x, y, c, _ = _place()

        def copy(t):
            half = ins[t].shape[1] // 2
            return pltpu.make_async_remote_copy(
                src_ref=ins[t].at[:, pl.ds(pl.multiple_of((1 - c) * half, 16), half), :], dst_ref=outs[t],
                send_sem=send_sems.at[t], recv_sem=recv_sems.at[t], device_id=(x, y, 1 - c), device_id_type=MESH)

        return [copy(t) for t in range(n)]

    def start(*refs):
        for cp in copies(*refs):
            cp.start()

    def finish(*refs):
        for cp in copies(*refs):
            cp.wait()

    shapes = [jax.ShapeDtypeStruct((g.shape[0], g.shape[1] // 2, g.shape[2]), g.dtype) for g in grads_b]
    return _Plan(grads_b, shapes, n, start, finish)


def _scatter_plan(parts_b):
    n = len(parts_b)

    def copies(ins, outs, send_sems, recv_sems):
        x, y, c, chips = _place()

        def copy(t, j):
            px, py = chips[j]
            return pltpu.make_async_remote_copy(
                src_ref=ins[t].at[2 * px + py], dst_ref=outs[t].at[j], send_sem=send_sems.at[3 * t + j],
                recv_sem=recv_sems.at[3 * t + j], device_id=(px, py, c), device_id_type=MESH)

        return [copy(t, j) for t in range(n) for j in range(3)]

    def start(*refs):
        for cp in copies(*refs):
            cp.start()

    def finish(*refs):
        for cp in copies(*refs):
            cp.wait()

    shapes = [jax.ShapeDtypeStruct((3, *p.shape[1:]), p.dtype) for p in parts_b]
    return _Plan(parts_b, shapes, 3 * n, start, finish)


def _gather_plan(shards, small=()):
    n, ns = len(shards), len(small)
    per = 8

    def parts(ins, outs, send_sems, recv_sems):
        x, y, c, chips = _place()
        me = 2 * x + y
        blocks = [2 * px + py for px, py in chips]

        def rows(t, core, piece=None):
            half = ins[t].shape[0] // 2
            if piece is None:
                return pl.ds(pl.multiple_of(core * half, 16), half)
            return pl.ds(pl.multiple_of(core * half + piece * (half // 2), 16), half // 2)

        def remote(src, dst, k, device):
            return pltpu.make_async_remote_copy(src_ref=src, dst_ref=dst, send_sem=send_sems.at[k],
                                                recv_sem=recv_sems.at[k], device_id=device, device_id_type=MESH)

        def first(t, j, block, core):
            return remote(ins[t].at[rows(t, core), :], outs[t].at[block, rows(t, core), :], per * t + j, (*chips[j], c))

        def relay(t, j, block, core):
            ref = outs[t].at[block, rows(t, core, j), :]
            return remote(ref, ref, per * t + 2 + j, (*chips[j], c))

        def passed(t, k, block, core, piece=None):
            ref = outs[t].at[block, rows(t, core, piece), :]
            return remote(ref, ref, per * t + 4 + k, (x, y, 1 - c))

        def whole(s, j, block):
            return remote(ins[n + s], outs[n + s].at[block], per * n + 3 * s + j, (*chips[j], c))

        return c, me, blocks, first, relay, passed, whole

    def start(*refs):
        c, me, _, first, _, _, whole = parts(*refs)
        for t in range(n):
            for j in range(2):
                first(t, j, me, c).start()
        for s in range(ns):
            for j in range(3):
                whole(s, j, me).start()

    def middle(*refs):
        c, _, blocks, first, relay, passed, _ = parts(*refs)
        for t in range(n):
            for j in range(2):
                first(t, j, blocks[j], c).wait_recv()
                passed(t, j, blocks[j], c).start()
                relay(t, 1 - j, blocks[j], c).start()

    def finish(*refs):
        c, me, blocks, first, relay, passed, whole = parts(*refs)
        for t in range(n):
            for j in range(2):
                relay(t, j, blocks[2], c).wait_recv()
                passed(t, 2 + j, blocks[2], c, j).start()
        for t in range(n):
            for j in range(2):
                passed(t, j, blocks[j], 1 - c).wait_recv()
                passed(t, 2 + j, blocks[2], 1 - c, j).wait_recv()
        for s in range(ns):
            for j in range(3):
                whole(s, j, blocks[j]).wait_recv()
        for t in range(n):
            for j in range(2):
                first(t, j, me, c).wait_send()
                relay(t, 1 - j, blocks[j], c).wait_send()
                passed(t, j, blocks[j], c).wait_send()
                passed(t, 2 + j, blocks[2], c, j).wait_send()
        for s in range(ns):
            for j in range(3):
                whole(s, j, me).wait_send()

    arrays = [*shards, *small]
    shapes = [jax.ShapeDtypeStruct((N_CHIPS, *a.shape), a.dtype) for a in arrays]
    return _Plan(arrays, shapes, per * n + 3 * ns, start, finish, middle)


def _run_comm(plan, name):
    k = len(plan.arrays)
    in_specs, out_specs, out_shape, sems = plan.specs()

    def body(*refs):
        cr = (refs[:k], refs[k:k + len(out_shape)], refs[-2], refs[-1])
        plan.start(*cr)
        if plan.middle is not None:
            plan.middle(*cr)
        plan.finish(*cr)

    return _pcall(body, name=name, in_specs=in_specs, out_specs=out_specs, out_shape=out_shape,
                  input_output_aliases=plan.aliases, scratch_shapes=sems)(*plan.arrays)


def _carried(plan, in_specs, out_specs, out_shape, scratch):
    aliases = {}
    if plan is not None:
        p_in, p_out, p_shape, p_sems = plan.specs()
        aliases = {len(in_specs) + i: len(out_specs) + j for i, j in plan.aliases.items()}
        in_specs, out_specs = in_specs + p_in, out_specs + p_out
        out_shape, scratch = out_shape + p_shape, scratch + p_sems
    return dict(in_specs=in_specs, out_specs=out_specs, out_shape=out_shape, scratch_shapes=scratch,
                input_output_aliases=aliases)


def _unpack(refs, n_in, n_out, plan):
    k_in = len(plan.arrays) if plan else 0
    k_out = len(plan.out_shapes) if plan else 0
    ins = refs[:n_in]
    outs = refs[n_in + k_in:n_in + k_in + n_out]
    rest = refs[n_in + k_in + n_out + k_out:]
    if plan is None:
        return ins, outs, rest, None
    cr = (refs[n_in:n_in + k_in], refs[n_in + k_in + n_out:n_in + k_in + n_out + k_out], rest[-2], rest[-1])
    return ins, outs, rest[:-2], cr


def _hook(plan, cr, which, cond):
    fn = getattr(plan, which) if plan is not None else None
    if fn is not None:
        pl.when(cond)(lambda: fn(*cr))


def _join_plan(shards):
    n = len(shards)

    def copy(ins, outs, send_sems, recv_sems, t, core):
        x, y, c, _ = _place()
        half = ins[t].shape[0] // 2
        rows = pl.ds(pl.multiple_of(core * half, 8), half)
        return pltpu.make_async_remote_copy(
            src_ref=ins[t].at[rows, :], dst_ref=outs[t].at[rows, :], send_sem=send_sems.at[t],
            recv_sem=recv_sems.at[t], device_id=(x, y, 1 - c), device_id_type=MESH)

    def start(*refs):
        c = lax.axis_index("c")
        for t in range(n):
            copy(*refs, t, c).start()

    def finish(*refs):
        c = lax.axis_index("c")
        for t in range(n):
            copy(*refs, t, 1 - c).wait_recv()
        for t in range(n):
            copy(*refs, t, c).wait_send()

    shapes = [jax.ShapeDtypeStruct(s.shape, s.dtype) for s in shards]
    return _Plan(shards, shapes, n, start, finish, aliases={t: t for t in range(n)})


def _all_gather_plan(vec):
    def parts(ins, outs, send_sems, recv_sems):
        x, y, c, _ = _place()
        me = 4 * x + 2 * y + c
        rel = [((k >> 2) & 1, (k >> 1) & 1, k & 1) for k in range(1, N_DEV)]

        def peer(k):
            fx, fy, fc = rel[k]
            return (x ^ fx, y ^ fy, c ^ fc)

        def copy(k, dev):
            return pltpu.make_async_remote_copy(
                src_ref=ins[0], dst_ref=outs[0].at[dev], send_sem=send_sems.at[k], recv_sem=recv_sems.at[k],
                device_id=peer(k), device_id_type=MESH)

        mine = pltpu.make_async_copy(ins[0], outs[0].at[me], send_sems.at[N_DEV - 1])
        return me, peer, copy, mine

    def start(*refs):
        me, _, copy, mine = parts(*refs)
        mine.start()
        for k in range(N_DEV - 1):
            copy(k, me).start()

    def finish(*refs):
        me, peer, copy, mine = parts(*refs)
        for k in range(N_DEV - 1):
            px, py, pc = peer(k)
            copy(k, 4 * px + 2 * py + pc).wait_recv()
        for k in range(N_DEV - 1):
            copy(k, me).wait_send()
        mine.wait()

    return _Plan([vec], [jax.ShapeDtypeStruct((N_DEV, *vec.shape), vec.dtype)], N_DEV, start, finish)


def _sum_devices(blocks, name):
    def body(b_ref, o_ref):
        total = b_ref[0]
        for dev in range(1, N_DEV):
            total = total + b_ref[dev]
        o_ref[...] = total

    return _pcall(body, name=name, in_specs=[pl.BlockSpec(memory_space=pltpu.VMEM)],
                  out_specs=pl.BlockSpec(memory_space=pltpu.VMEM),
                  out_shape=jax.ShapeDtypeStruct(blocks.shape[1:], F32))(blocks)


TOKEN_TILE = 512
PROJ_TOKEN_TILE = 1024
ADAMW_ROW_BLOCKS = 4
LARGE_VMEM_LIMIT = 62 * 1024 * 1024
DW_TOKEN_TILE = 2048
DW_ROW_SPLIT = 2
MXU_COLS = 256
DH_GROUP = 6


def _chunks(n):
    out, c0 = [], 0
    while c0 < n:
        size = min(MXU_COLS, n - c0)
        out.append((c0, size))
        c0 += size
    return out


def _load_weights(hbm_refs, vmem_refs, sems):
    copies = [pltpu.make_async_copy(h, v, sems.at[k]) for k, (h, v) in enumerate(zip(hbm_refs, vmem_refs))]
    for cp in copies:
        cp.start()
    for cp in copies:
        cp.wait()


def _ffn_fwd(x, g, wgt, wut, wd, name, plan=None, head=None, pre=None):
    T, D = x.shape
    F = wgt.shape[0]
    tm = min(T, TOKEN_TILE)
    ni = T // tm
    n_head = 2 if head is not None else 0
    n_pre = 1 if pre is not None else 0

    def body(*refs):
        ins, outs, scratch, cr = _unpack(refs, 5 + n_head + 2 * n_pre, 5 + n_head + n_pre, plan)
        x_ref, g_ref, wg_hbm, wu_hbm, wd_hbm = ins[:5]
        xo_ref, h_ref, gate_ref, up_ref, act_ref = outs[:5]
        wg_ref, wu_ref, wd_ref, sems = scratch
        i = pl.program_id(0)
        _hook(plan, cr, "start", i == 0)

        @pl.when(i == 0)
        def _():
            _load_weights((wg_hbm, wu_hbm, wd_hbm), (wg_ref, wu_ref, wd_ref), sems)

        if pre is not None:
            a_ref, w_ref = ins[5 + n_head:]
            x_ref = outs[5 + n_head]
            x_ref[...] = ins[0][...] + _dot(a_ref[...], w_ref[...])
        xv = x_ref[...]
        h = ((xv * _rms_inv(xv)) * g_ref[...]).astype(BF16)
        h_ref[...] = h
        for c0, size in _chunks(F):
            gate = _dot_nt(h, wg_ref[c0:c0 + size, :])
            up = _dot_nt(h, wu_ref[c0:c0 + size, :])
            gate_ref[:, c0:c0 + size] = gate.astype(BF16)
            up_ref[:, c0:c0 + size] = up.astype(BF16)
            act_ref[:, c0:c0 + size] = (gate * jax.nn.sigmoid(gate) * up).astype(BF16)
        y = x_ref[...] + FFN_RES_SCALE * _dot(act_ref[...], wd_ref[...])
        if head is None:
            xo_ref[...] = y
        else:
            gf_ref, t_ref = ins[5:7]
            dgf_ref, loss_ref = outs[5:7]

            @pl.when(i == 0)
            def _():
                dgf_ref[...] = jnp.zeros_like(dgf_ref)
                loss_ref[...] = jnp.zeros_like(loss_ref)

            gf = gf_ref[...]
            diff = (y * _rms_inv(y)) * gf - t_ref[...]
            loss_ref[...] += 0.5 * jnp.sum(jnp.mean(diff * diff, axis=-1, keepdims=True))
            dy, dgf = _norm_bwd(diff * (1.0 / D), y, gf)
            xo_ref[...] = dy
            dgf_ref[...] += dgf
        _hook(plan, cr, "middle", i == ni // 2)
        _hook(plan, cr, "finish", i == ni - 1)

    const = lambda shape: pl.BlockSpec(shape, lambda i: (0, 0))
    rows = lambda width: pl.BlockSpec((tm, width), lambda i: (i, 0))
    in_specs = [rows(D), const((1, D)), ANY_SPEC, ANY_SPEC, ANY_SPEC]
    out_specs = [rows(D), rows(D), rows(F), rows(F), rows(F)]
    out_shape = [jax.ShapeDtypeStruct((T, D), F32), jax.ShapeDtypeStruct((T, D), BF16),
                 jax.ShapeDtypeStruct((T, F), BF16), jax.ShapeDtypeStruct((T, F), BF16), jax.ShapeDtypeStruct((T, F), BF16)]
    if head is not None:
        in_specs += [const((1, D)), rows(D)]
        out_specs += [const((1, D)), const((1, LANES))]
        out_shape += [jax.ShapeDtypeStruct((1, D), F32), jax.ShapeDtypeStruct((1, LANES), F32)]
    if pre is not None:
        in_specs += [rows(pre[0].shape[1]), const(pre[1].shape)]
        out_specs += [rows(D)]
        out_shape += [jax.ShapeDtypeStruct((T, D), F32)]
    io = _carried(plan, in_specs, out_specs, out_shape,
                  [pltpu.VMEM((F, D), BF16), pltpu.VMEM((F, D), BF16), pltpu.VMEM((F, D), BF16),
                   pltpu.SemaphoreType.DMA((3,))])
    return _pcall(
        body, name=name, grid=(ni,), compiler_params=_params(1, VMEM_LIMIT if pre is None else LARGE_VMEM_LIMIT), **io,
    )(x, g, wgt, wut, wd, *(head or ()), *(pre or ()), *(plan.arrays if plan else ()))


def _ffn_bwd(dy, x, g, gate, up, wgt, wut, wd, name):
    T, D = x.shape
    F = wgt.shape[0]
    tm = min(T, TOKEN_TILE)
    ni = T // tm

    def body(dy_ref, x_ref, g_ref, gate_ref, up_ref, wg_hbm, wu_hbm, wd_hbm,
             dx_ref, dyb_ref, dgate_ref, dup_ref, dg_ref, wg_ref, wu_ref, wd_ref, sems):
        @pl.when(pl.program_id(0) == 0)
        def _():
            _load_weights((wg_hbm, wu_hbm, wd_hbm), (wg_ref, wu_ref, wd_ref), sems)
            dg_ref[...] = jnp.zeros_like(dg_ref)

        dyb = (FFN_RES_SCALE * dy_ref[...]).astype(BF16)
        dyb_ref[...] = dyb
        dh, group_g, group_u, row0 = None, [], [], 0
        chunks = _chunks(F)
        for k, (c0, size) in enumerate(chunks):
            dact = _dot_nt(dyb, wd_ref[c0:c0 + size, :])
            gt = gate_ref[:, c0:c0 + size].astype(F32)
            u = up_ref[:, c0:c0 + size].astype(F32)
            sig = jax.nn.sigmoid(gt)
            dup = (dact * (gt * sig)).astype(BF16)
            dgate = (dact * u * (sig * (1.0 + gt * (1.0 - sig)))).astype(BF16)
            dup_ref[:, c0:c0 + size] = dup
            dgate_ref[:, c0:c0 + size] = dgate
            group_g.append(dgate)
            group_u.append(dup)
            if len(group_g) == DH_GROUP or k == len(chunks) - 1:
                rows = slice(row0, c0 + size)
                part = (_dot(jnp.concatenate(group_g, axis=1), wg_ref[rows, :])
                        + _dot(jnp.concatenate(group_u, axis=1), wu_ref[rows, :]))
                dh = part if dh is None else dh + part
                group_g, group_u, row0 = [], [], c0 + size
        dxn, dg = _norm_bwd(dh, x_ref[...], g_ref[...])
        dx_ref[...] = dy_ref[...] + dxn
        dg_ref[...] += dg

    return _pcall(
        body, name=name, grid=(ni,),
        in_specs=[pl.BlockSpec((tm, D), lambda i: (i, 0)), pl.BlockSpec((tm, D), lambda i: (i, 0)),
                  pl.BlockSpec((1, D), lambda i: (0, 0)),
                  pl.BlockSpec((tm, F), lambda i: (i, 0)), pl.BlockSpec((tm, F), lambda i: (i, 0)),
                  ANY_SPEC, ANY_SPEC, ANY_SPEC],
        out_specs=[pl.BlockSpec((tm, D), lambda i: (i, 0)), pl.BlockSpec((tm, D), lambda i: (i, 0)),
                   pl.BlockSpec((tm, F), lambda i: (i, 0)), pl.BlockSpec((tm, F), lambda i: (i, 0)),
                   pl.BlockSpec((1, D), lambda i: (0, 0))],
        out_shape=[jax.ShapeDtypeStruct((T, D), F32), jax.ShapeDtypeStruct((T, D), BF16),
                   jax.ShapeDtypeStruct((T, F), BF16), jax.ShapeDtypeStruct((T, F), BF16),
                   jax.ShapeDtypeStruct((1, D), F32)],
        scratch_shapes=[pltpu.VMEM((F, D), BF16), pltpu.VMEM((F, D), BF16), pltpu.VMEM((F, D), BF16),
                        pltpu.SemaphoreType.DMA((3,))],
        compiler_params=_params(1, LARGE_VMEM_LIMIT),
    )(dy, x, g, gate, up, wgt, wut, wd)


def _matmul_tn(a, b, row_split, name, plan=None):
    T, n1 = a.shape
    n2 = b.shape[1]
    tn = n1 // row_split
    tk = min(T, DW_TOKEN_TILE)
    nk = T // tk

    def body(*refs):
        (a_ref, b_ref), (o_ref, ob_ref), _, cr = _unpack(refs, 2, 2, plan)
        j = pl.program_id(0)
        k = pl.program_id(1)
        _hook(plan, cr, "start", jnp.logical_and(j == 0, k == 0))

        @pl.when(k == 0)
        def _():
            o_ref[...] = jnp.zeros_like(o_ref)

        o_ref[...] += _dot_tn(a_ref[...], b_ref[...])

        @pl.when(k == nk - 1)
        def _():
            ob_ref[...] = o_ref[...].astype(BF16)

        _hook(plan, cr, "finish", jnp.logical_and(j == row_split - 1, k == nk - 1))

    io = _carried(
        plan,
        [pl.BlockSpec((tk, tn), lambda j, k: (k, j)), pl.BlockSpec((tk, n2), lambda j, k: (k, 0))],
        [pl.BlockSpec((tn, n2), lambda j, k: (j, 0)), pl.BlockSpec((tn, n2), lambda j, k: (j, 0))],
        [jax.ShapeDtypeStruct((n1, n2), F32), jax.ShapeDtypeStruct((n1, n2), BF16)], [])
    return _pcall(
        body, name=name, grid=(row_split, nk), compiler_params=_params(2), **io,
    )(a, b, *(plan.arrays if plan else ()))


def _norm_matmul(x, g, wt, tab, name):
    T, D = x.shape
    n = wt.shape[0]
    tm = min(T, PROJ_TOKEN_TILE)

    def body(x_ref, g_ref, w_ref, tab_ref, z_ref, h_ref):
        xv = x_ref[...]
        h = ((xv * _rms_inv(xv)) * g_ref[...]).astype(BF16)
        h_ref[...] = h
        z = _dot_nt(h, w_ref[...])
        z_ref[:, 0:Z_Q] = z[:, 0:Z_Q]
        tab_v = tab_ref[...]
        for c0 in range(Z_Q, Z_V, LANES):
            z_ref[:, c0:c0 + LANES] = _rot(z[:, c0:c0 + LANES], tab_v)
        z_ref[:, Z_V:Z_END] = z[:, Z_V:Z_END]

    return _pcall(
        body, name=name, grid=(T // tm,),
        in_specs=[pl.BlockSpec((tm, D), lambda i: (i, 0)), pl.BlockSpec((1, D), lambda i: (0, 0)),
                  pl.BlockSpec((n, D), lambda i: (0, 0)), pl.BlockSpec((tm, 3 * LANES), lambda i: (i, 0))],
        out_specs=[pl.BlockSpec((tm, n), lambda i: (i, 0)), pl.BlockSpec((tm, D), lambda i: (i, 0))],
        out_shape=[jax.ShapeDtypeStruct((T, n), F32), jax.ShapeDtypeStruct((T, D), BF16)],
        compiler_params=_params(1),
    )(x, g, wt, tab)


def _matmul_norm_bwd(dz, wt, x, g, dres, name):
    T, D = x.shape
    n = dz.shape[1]
    tm = min(T, PROJ_TOKEN_TILE)

    def body(dz_ref, w_ref, x_ref, g_ref, dres_ref, dx_ref, dg_ref):
        @pl.when(pl.program_id(0) == 0)
        def _():
            dg_ref[...] = jnp.zeros_like(dg_ref)

        dh = _dot(dz_ref[...], w_ref[...])
        dxn, dg = _norm_bwd(dh, x_ref[...], g_ref[...])
        dx_ref[...] = dres_ref[...] + dxn
        dg_ref[...] += dg

    return _pcall(
        body, name=name, grid=(T // tm,),
        in_specs=[pl.BlockSpec((tm, n), lambda i: (i, 0)), pl.BlockSpec((n, D), lambda i: (0, 0)),
                  pl.BlockSpec((tm, D), lambda i: (i, 0)), pl.BlockSpec((1, D), lambda i: (0, 0)),
                  pl.BlockSpec((tm, D), lambda i: (i, 0))],
        out_specs=[pl.BlockSpec((tm, D), lambda i: (i, 0)), pl.BlockSpec((1, D), lambda i: (0, 0))],
        out_shape=[jax.ShapeDtypeStruct((T, D), F32), jax.ShapeDtypeStruct((1, D), F32)],
        compiler_params=_params(1),
    )(dz, wt, x, g, dres)


Z_Q = 3 * CONV_WIDTH
Z_K = Z_Q + N_Q_HEADS * HEAD_DIM
Z_V = Z_K + LANES
Z_END = Z_V + LANES


def _rope_tables(T, name, plan):
    half = ROT_DIM // 2
    inv_freq = ROPE_THETA ** (-jnp.arange(0, ROT_DIM, 2, dtype=F32) / ROT_DIM)
    ang = inv_freq[:, None] * jnp.arange(T, dtype=F32)[None, :]
    cos_sin = jnp.concatenate([jnp.cos(ang), jnp.sin(ang)], axis=0)
    select = np.zeros((2 * half, 3 * LANES), np.float32)
    const = np.zeros((1, 3 * LANES), np.float32)
    for lane in range(LANES):
        d = lane % HEAD_DIM
        if d < half:
            select[d, lane] = 1.0
            select[half + d, LANES + lane] = -1.0
        elif d < ROT_DIM:
            select[d - half, lane] = 1.0
            select[d, 2 * LANES + lane] = 1.0
        else:
            const[0, lane] = 1.0
    tm = PROJ_TOKEN_TILE
    ni = T // tm

    def body(*refs):
        (cs_ref, sel_ref, const_ref), (tab_ref,), _, cr = _unpack(refs, 3, 1, plan)
        i = pl.program_id(0)
        _hook(plan, cr, "start", i == 0)
        tab_ref[...] = lax.dot_general(cs_ref[...], sel_ref[...], (((0,), (0,)), ((), ())),
                                       precision=lax.Precision.HIGHEST, preferred_element_type=F32) + const_ref[...]
        _hook(plan, cr, "middle", i == ni - 1)
        _hook(plan, cr, "finish", i == ni - 1)

    io = _carried(
        plan,
        [pl.BlockSpec((2 * half, tm), lambda i: (0, i)), pl.BlockSpec((2 * half, 3 * LANES), lambda i: (0, 0)),
         pl.BlockSpec((1, 3 * LANES), lambda i: (0, 0))],
        [pl.BlockSpec((tm, 3 * LANES), lambda i: (i, 0))], [jax.ShapeDtypeStruct((T, 3 * LANES), F32)], [])
    res = _pcall(body, name=name, grid=(ni,), compiler_params=_params(1), **io)(
        cos_sin, jnp.asarray(select), jnp.asarray(const), *plan.arrays)
    return res[0], res[1:]


def _tab3(tab):
    return tab[:, 0:LANES], tab[:, LANES:2 * LANES], tab[:, 2 * LANES:3 * LANES]


def _rot(x, tab):
    c, s1, s2 = _tab3(tab)
    return x * c + pltpu.roll(x, LANES - ROT_DIM // 2, 1) * s1 + pltpu.roll(x, ROT_DIM // 2, 1) * s2


def _rot_t(d, tab):
    c, s1, s2 = _tab3(tab)
    return d * c + pltpu.roll(d * s1, ROT_DIM // 2, 1) + pltpu.roll(d * s2, LANES - ROT_DIM // 2, 1)


def _head_pads(a):
    lo = lax.broadcasted_iota(jnp.int32, a.shape, 1) < HEAD_DIM
    nat0 = jnp.where(lo, a, 0.0)
    nat1 = jnp.where(lo, 0.0, a)
    return {
        (0, 0): nat0.astype(BF16), (0, 1): pltpu.roll(nat0, HEAD_DIM, 1).astype(BF16),
        (1, 0): pltpu.roll(nat1, HEAD_DIM, 1).astype(BF16), (1, 1): nat1.astype(BF16),
    }


def _from_pads(even, odd, kv):
    lo = lax.broadcasted_iota(jnp.int32, even.shape, 1) < HEAD_DIM
    if kv == 0:
        return jnp.where(lo, even + pltpu.roll(odd, HEAD_DIM, 1), 0.0)
    return jnp.where(lo, 0.0, pltpu.roll(even, HEAD_DIM, 1) + odd)


N_GROUPS = 4


def _group_head(g, r):
    kv, par = divmod(g, 2)
    return 2 * (2 * kv + r) + par


def _window_mask_t(has_prev):
    jj = lax.broadcasted_iota(jnp.int32, (2 * BLOCK, 2 * BLOCK), 0)
    ii = lax.broadcasted_iota(jnp.int32, (2 * BLOCK, 2 * BLOCK), 1) & (BLOCK - 1)
    rel = jj - BLOCK - ii
    return (rel <= 0) & (rel > -BLOCK) & ((jj >= BLOCK) | has_prev)


def _sink_row(sink_ref, g):
    lane = lax.broadcasted_iota(jnp.int32, (1, 2 * BLOCK), 1)
    return jnp.where(lane < BLOCK, sink_ref[0, _group_head(g, 0)], sink_ref[0, _group_head(g, 1)])


def _attn_probs_t(q2, kp, mask, sink_ref):
    out = []
    for kv in range(2):
        q_st = jnp.concatenate([q2[2 * kv], q2[2 * kv + 1]], axis=0)
        for par in range(2):
            s = jnp.where(mask, _dot_nt(kp[(kv, par)], q_st), MASK_VALUE)
            sink = _sink_row(sink_ref, 2 * kv + par)
            m = jnp.maximum(jnp.max(s, axis=0, keepdims=True), sink)
            p = jnp.exp(s - m)
            esink = jnp.exp(sink - m)
            rden = 1.0 / (jnp.sum(p, axis=0, keepdims=True) + esink)
            out.append((p * rden, esink * rden))
    return out


def _conv_taps(cg, u, cg_prev, u_prev, has_prev):
    vv = cg * u
    halo = jnp.where(has_prev, cg_prev * u_prev, 0.0)
    ext = jnp.concatenate([halo, vv], axis=0)
    rows = ext.shape[0]
    vv1 = pltpu.roll(ext, 1, 0)[8:rows]
    vv2 = pltpu.roll(ext, 2, 0)[8:rows]
    return vv, vv1, vv2


MIX_BLOCKS = 4


def _mix_core_fwd(z, conv_w, sinks, name):
    T = z.shape[0]
    rows = MIX_BLOCKS * BLOCK
    steps = T // rows
    prev_block = lambda n: jnp.maximum(MIX_BLOCKS * n - 1, 0)
    prev_rows8 = lambda n: jnp.maximum((rows // 8) * n - 1, 0)

    def body(z_ref, zkvp_ref, cgp_ref, up_ref, cw_ref, sink_ref, y_ref):
        for b in range(MIX_BLOCKS):
            r0 = b * BLOCK
            blk = slice(r0, r0 + BLOCK)
            if b == 0:
                has_prev = pl.program_id(0) > 0
                kv_prev, cg_prev, u_prev = zkvp_ref[...], cgp_ref[...], up_ref[...]
            else:
                has_prev = True
                kv_prev = z_ref[r0 - BLOCK:r0, Z_K:Z_END]
                cg_prev = z_ref[r0 - 8:r0, CONV_WIDTH:2 * CONV_WIDTH]
                u_prev = z_ref[r0 - 8:r0, 2 * CONV_WIDTH:Z_Q]
            bg = z_ref[blk, 0:CONV_WIDTH]
            vv, vv1, vv2 = _conv_taps(z_ref[blk, CONV_WIDTH:2 * CONV_WIDTH], z_ref[blk, 2 * CONV_WIDTH:Z_Q],
                                      cg_prev, u_prev, has_prev)
            conv = cw_ref[0:1, :] * vv2 + cw_ref[1:2, :] * vv1 + cw_ref[2:3, :] * vv
            y_ref[blk, 0:CONV_WIDTH] = (bg * conv).astype(BF16)

            k_all = jnp.concatenate([kv_prev[:, 0:LANES], z_ref[blk, Z_K:Z_V]], axis=0)
            v_all = jnp.concatenate([kv_prev[:, LANES:2 * LANES], z_ref[blk, Z_V:Z_END]], axis=0)
            kp = _head_pads(k_all)
            vp = _head_pads(v_all)
            q2 = [(z_ref[blk, Z_Q + LANES * c:Z_Q + LANES * (c + 1)] * ATTN_SCALE).astype(BF16)
                  for c in range(N_Q_HEADS // 2)]
            probs = _attn_probs_t(q2, kp, _window_mask_t(has_prev), sink_ref)
            for kv in range(2):
                o_t = (_dot_tn(vp[(kv, 0)], probs[2 * kv][0].astype(BF16))
                       + _dot_tn(vp[(kv, 1)], probs[2 * kv + 1][0].astype(BF16)))
                for r in range(2):
                    c = 2 * kv + r
                    y_ref[blk, CONV_WIDTH + LANES * c:CONV_WIDTH + LANES * (c + 1)] = (
                        o_t[:, BLOCK * r:BLOCK * (r + 1)].T.astype(BF16))

    return _pcall(
        body, name=name, grid=(steps,),
        in_specs=[pl.BlockSpec((rows, Z_END), lambda n: (n, 0)),
                  pl.BlockSpec((BLOCK, 2 * LANES), lambda n: (prev_block(n), Z_K // (2 * LANES))),
                  pl.BlockSpec((8, CONV_WIDTH), lambda n: (prev_rows8(n), 1)),
                  pl.BlockSpec((8, CONV_WIDTH), lambda n: (prev_rows8(n), 2)),
                  pl.BlockSpec((3, CONV_WIDTH), lambda n: (0, 0)),
                  pl.BlockSpec(memory_space=pltpu.SMEM)],
        out_specs=pl.BlockSpec((rows, 2 * CONV_WIDTH), lambda n: (n, 0)),
        out_shape=jax.ShapeDtypeStruct((T, 2 * CONV_WIDTH), BF16),
        compiler_params=_params(1),
    )(z, z, z, z, conv_w, sinks)


def _mix_core_bwd(z, dx, wout, tab, conv_w, sinks, name):
    T = z.shape[0]
    nsub = MIX_BLOCKS
    rows = nsub * BLOCK
    steps = T // rows
    last = slice(rows - BLOCK, rows)
    cur = lambda n: jnp.minimum(n, steps - 1)
    prev_block = lambda n: jnp.maximum(nsub * cur(n) - 1, 0)
    prev_rows8 = lambda n: jnp.maximum((rows // 8) * cur(n) - 1, 0)
    next_rows8 = lambda n: jnp.minimum((rows // 8) * (cur(n) + 1), T // 8 - 1)

    def body(z_ref, zkvp_ref, cgp_ref, up_ref, bgn_ref, dx_ref, dxn_ref, wo_ref, tab_ref, tabp_ref, cw_ref, sink_ref,
             dz_ref, dcw_ref, dsk_ref, dxb_ref, held_ref, kv_ref):
        n = pl.program_id(0)

        @pl.when(n == 0)
        def _():
            held_ref[...] = jnp.zeros_like(held_ref)
            kv_ref[...] = jnp.zeros_like(kv_ref)
            dcw_ref[...] = jnp.zeros_like(dcw_ref)
            dsk_ref[...] = jnp.zeros_like(dsk_ref)

        def emit_held():
            dz_ref[:, 0:Z_K] = held_ref[:, 0:Z_K]
            if nsub > 1:
                dz_ref[0:rows - BLOCK, Z_K:Z_END] = held_ref[0:rows - BLOCK, Z_K:Z_END]

        @pl.when(n < steps)
        def _():
            emit_held()
            dxb = dx_ref[...].astype(BF16)
            dxb_ref[...] = dxb
            dy = _dot_nt(dxb, wo_ref[...])
            dy_next = _dot_nt(dxn_ref[...].astype(BF16), wo_ref[0:CONV_WIDTH, :])
            w0, w1, w2 = cw_ref[0:1, :], cw_ref[1:2, :], cw_ref[2:3, :]
            dk_open, dv_open = kv_ref[:, 0:LANES], kv_ref[:, LANES:2 * LANES]
            for b in range(nsub):
                r0 = b * BLOCK
                blk = slice(r0, r0 + BLOCK)
                before = slice(r0 - BLOCK, r0)
                after8 = slice(r0 + BLOCK, r0 + BLOCK + 8)
                if b == 0:
                    has_prev = n > 0
                    kv_prev, cg_prev, u_prev, tab_p = zkvp_ref[...], cgp_ref[...], up_ref[...], tabp_ref[...]
                else:
                    has_prev = True
                    kv_prev, tab_p = z_ref[before, Z_K:Z_END], tab_ref[before, :]
                    cg_prev = z_ref[r0 - 8:r0, CONV_WIDTH:2 * CONV_WIDTH]
                    u_prev = z_ref[r0 - 8:r0, 2 * CONV_WIDTH:Z_Q]
                if b == nsub - 1:
                    dconv_next = jnp.where(n < steps - 1, dy_next * bgn_ref[...], 0.0)
                else:
                    dconv_next = dy[after8, 0:CONV_WIDTH] * z_ref[after8, 0:CONV_WIDTH]
                bg = z_ref[blk, 0:CONV_WIDTH]
                cg = z_ref[blk, CONV_WIDTH:2 * CONV_WIDTH]
                u = z_ref[blk, 2 * CONV_WIDTH:Z_Q]
                vv, vv1, vv2 = _conv_taps(cg, u, cg_prev, u_prev, has_prev)
                dyc = dy[blk, 0:CONV_WIDTH]
                dbg = dyc * (w0 * vv2 + w1 * vv1 + w2 * vv)
                dconv = dyc * bg
                ext = jnp.concatenate([dconv, dconv_next], axis=0)
                ext_rows = ext.shape[0]
                dvv = (w2 * dconv + w1 * pltpu.roll(ext, ext_rows - 1, 0)[0:BLOCK]
                       + w0 * pltpu.roll(ext, ext_rows - 2, 0)[0:BLOCK])
                dcw_ref[0:1, :] += jnp.sum(dconv * vv2, axis=0, keepdims=True)
                dcw_ref[1:2, :] += jnp.sum(dconv * vv1, axis=0, keepdims=True)
                dcw_ref[2:3, :] += jnp.sum(dconv * vv, axis=0, keepdims=True)

                tab_c = tab_ref[blk, :]
                k_all = jnp.concatenate([kv_prev[:, 0:LANES], z_ref[blk, Z_K:Z_V]], axis=0)
                v_all = jnp.concatenate([kv_prev[:, LANES:2 * LANES], z_ref[blk, Z_V:Z_END]], axis=0)
                kp = _head_pads(k_all)
                vp = _head_pads(v_all)
                chunks = range(N_Q_HEADS // 2)
                q2 = [(z_ref[blk, Z_Q + LANES * c:Z_Q + LANES * (c + 1)] * ATTN_SCALE).astype(BF16) for c in chunks]
                do2 = [dy[blk, CONV_WIDTH + LANES * c:CONV_WIDTH + LANES * (c + 1)].astype(BF16) for c in chunks]
                probs = _attn_probs_t(q2, kp, _window_mask_t(has_prev), sink_ref)
                dq_chunks = []
                dk_nat = jnp.zeros((2 * BLOCK, LANES), F32)
                dv_nat = jnp.zeros((2 * BLOCK, LANES), F32)
                for kv in range(2):
                    q_st = jnp.concatenate([q2[2 * kv], q2[2 * kv + 1]], axis=0)
                    do_st = jnp.concatenate([do2[2 * kv], do2[2 * kv + 1]], axis=0)
                    dq_t = jnp.zeros((LANES, 2 * BLOCK), F32)
                    dk_par, dv_par = [], []
                    for par in range(2):
                        g = 2 * kv + par
                        pr, psink = probs[g]
                        dp = _dot_nt(vp[(kv, par)], do_st)
                        delta = jnp.sum(dp * pr, axis=0, keepdims=True)
                        ds = (pr * (dp - delta)).astype(BF16)
                        dsink = -psink * delta
                        for r in range(2):
                            h = _group_head(g, r)
                            dsk_ref[h:h + 1, :] += jnp.sum(dsink[:, BLOCK * r:BLOCK * (r + 1)])
                        dq_t = dq_t + _dot_tn(kp[(kv, par)], ds)
                        dk_par.append(_dot(ds, q_st))
                        dv_par.append(_dot(pr.astype(BF16), do_st))
                    for r in range(2):
                        dq_chunks.append(_rot_t(dq_t[:, BLOCK * r:BLOCK * (r + 1)].T * ATTN_SCALE, tab_c))
                    dk_nat = dk_nat + _from_pads(dk_par[0], dk_par[1], kv)
                    dv_nat = dv_nat + _from_pads(dv_par[0], dv_par[1], kv)

                done_ref, done = (dz_ref, last) if b == 0 else (held_ref, before)
                done_ref[done, Z_K:Z_V] = _rot_t(dk_open + dk_nat[0:BLOCK], tab_p).astype(BF16)
                done_ref[done, Z_V:Z_END] = (dv_open + dv_nat[0:BLOCK]).astype(BF16)
                dk_open, dv_open = dk_nat[BLOCK:2 * BLOCK], dv_nat[BLOCK:2 * BLOCK]
                held_ref[blk, 0:CONV_WIDTH] = dbg.astype(BF16)
                held_ref[blk, CONV_WIDTH:2 * CONV_WIDTH] = (dvv * u).astype(BF16)
                held_ref[blk, 2 * CONV_WIDTH:Z_Q] = (dvv * cg).astype(BF16)
                for c in range(N_Q_HEADS // 2):
                    held_ref[blk, Z_Q + LANES * c:Z_Q + LANES * (c + 1)] = dq_chunks[c].astype(BF16)
            kv_ref[:, 0:LANES] = dk_open
            kv_ref[:, LANES:2 * LANES] = dv_open

        @pl.when(n == steps)
        def _():
            emit_held()
            dz_ref[last, Z_K:Z_V] = _rot_t(kv_ref[:, 0:LANES], tab_ref[last, :]).astype(BF16)
            dz_ref[last, Z_V:Z_END] = kv_ref[:, LANES:2 * LANES].astype(BF16)

    return _pcall(
        body, name=name, grid=(steps + 1,),
        in_specs=[pl.BlockSpec((rows, Z_END), lambda n: (cur(n), 0)),
                  pl.BlockSpec((BLOCK, 2 * LANES), lambda n: (prev_block(n), Z_K // (2 * LANES))),
                  pl.BlockSpec((8, CONV_WIDTH), lambda n: (prev_rows8(n), 1)),
                  pl.BlockSpec((8, CONV_WIDTH), lambda n: (prev_rows8(n), 2)),
                  pl.BlockSpec((8, CONV_WIDTH), lambda n: (next_rows8(n), 0)),
                  pl.BlockSpec((rows, dx.shape[1]), lambda n: (cur(n), 0)),
                  pl.BlockSpec((8, dx.shape[1]), lambda n: (next_rows8(n), 0)),
                  pl.BlockSpec(wout.shape, lambda n: (0, 0)),
                  pl.BlockSpec((rows, 3 * LANES), lambda n: (cur(n), 0)),
                  pl.BlockSpec((BLOCK, 3 * LANES), lambda n: (prev_block(n), 0)),
                  pl.BlockSpec((3, CONV_WIDTH), lambda n: (0, 0)),
                  pl.BlockSpec(memory_space=pltpu.SMEM)],
        out_specs=[pl.BlockSpec((rows, Z_END), lambda n: (jnp.maximum(n - 1, 0), 0)),
                   pl.BlockSpec((8, CONV_WIDTH), lambda n: (0, 0)), pl.BlockSpec((8, LANES), lambda n: (0, 0)),
                   pl.BlockSpec((rows, dx.shape[1]), lambda n: (cur(n), 0))],
        out_shape=[jax.ShapeDtypeStruct((T, Z_END), BF16), jax.ShapeDtypeStruct((8, CONV_WIDTH), F32),
                   jax.ShapeDtypeStruct((8, LANES), F32), jax.ShapeDtypeStruct(dx.shape, BF16)],
        scratch_shapes=[pltpu.VMEM((rows, Z_END), BF16), pltpu.VMEM((BLOCK, 2 * LANES), F32)],
        compiler_params=_params(1),
    )(z, z, z, z, z, dx, dx, wout, tab, tab, conv_w, sinks)


def _local_sums(pair, chip, place, name):
    arrays, in_specs, out_specs, out_shape = [], [], [], []
    if pair is not None:
        g, sib = pair
        blk = (1, *sib.shape[1:])
        arrays += [g, sib]
        in_specs += [pl.BlockSpec(blk, lambda q, p: (q, p[1], 0)), pl.BlockSpec(blk, lambda q, p: (q, 0, 0))]
        out_specs.append(pl.BlockSpec(blk, lambda q, p: (q, 0, 0)))
        out_shape.append(jax.ShapeDtypeStruct(sib.shape, BF16))
    if chip is not None:
        g2, sib2, recv2 = chip
        blk = (1, *sib2.shape[1:])
        arrays += [g2, sib2, recv2]
        in_specs += [pl.BlockSpec(blk, lambda q, p: (p[0], p[1], 0)), pl.BlockSpec(blk, lambda q, p: (p[0], 0, 0)),
                     pl.BlockSpec(recv2.shape, lambda q, p: (0, 0, 0))]
        out_specs.append(pl.BlockSpec(sib2.shape[1:], lambda q, p: (p[1], 0)))
        out_shape.append(jax.ShapeDtypeStruct(g2.shape[1:], F32))

    def body(place_ref, *refs):
        refs = list(refs)
        ins, outs = refs[:len(arrays)], refs[len(arrays):]
        if pair is not None:
            g_ref, sib_ref = ins[:2]
            outs[0][...] = (g_ref[...] + sib_ref[...].astype(F32)).astype(BF16)
        if chip is not None:
            g_ref, sib_ref, recv_ref = ins[-3:]

            @pl.when(pl.program_id(0) == 0)
            def _():
                total = g_ref[0] + sib_ref[0].astype(F32)
                for j in range(3):
                    total = total + recv_ref[j].astype(F32)
                outs[-1][...] = total

    return _pcall(
        body, name=name,
        grid_spec=pltpu.PrefetchScalarGridSpec(num_scalar_prefetch=1, grid=(N_CHIPS,),
                                               in_specs=in_specs, out_specs=out_specs),
        out_shape=out_shape, compiler_params=_params(1),
    )(place, *arrays)


def _adamw_math(w, g, m, v):
    m = ADAM_B1 * m + (1.0 - ADAM_B1) * g
    v = ADAM_B2 * v + (1.0 - ADAM_B2) * (g * g)
    m_hat = m / (1.0 - ADAM_B1 ** ADAM_STEP)
    v_hat = v / (1.0 - ADAM_B2 ** ADAM_STEP)
    delta = -ADAM_LR * (m_hat / (jnp.sqrt(v_hat) + ADAM_EPS) + ADAM_WD * w)
    return delta, m, v


def _adamw(ws, gs, ms, vs, row_blocks, name):
    n = len(ws)

    def body(*refs):
        w, g, m, v = refs[:n], refs[n:2 * n], refs[2 * n:3 * n], refs[3 * n:4 * n]
        d, mo, vo, go = refs[4 * n:5 * n], refs[5 * n:6 * n], refs[6 * n:7 * n], refs[7 * n:]
        for t in range(n):
            gv = g[t][...]
            delta, m_new, v_new = _adamw_math(w[t][...], gv, m[t][...], v[t][...])
            d[t][...] = delta
            mo[t][...] = m_new
            vo[t][...] = v_new
            go[t][...] = gv

    specs = [pl.BlockSpec((a.shape[0] // row_blocks, a.shape[1]), lambda i: (i, 0)) for a in ws]
    shapes = [jax.ShapeDtypeStruct(a.shape, F32) for a in ws]
    return _pcall(
        body, name=name, grid=(row_blocks,), in_specs=specs * 4, out_specs=specs * 4, out_shape=shapes * 4,
        compiler_params=_params(1),
    )(*ws, *gs, *ms, *vs)


def kernel(x, ffn1_norm, ffn1_w_gate, ffn1_w_up, ffn1_w_down, mix_norm, w_in, conv_w, attn_sinks, w_out, ffn2_norm, ffn2_w_gate, ffn2_w_up, ffn2_w_down, final_norm, loss_target, m_ffn1_norm, m_ffn1_w_gate, m_ffn1_w_up, m_ffn1_w_down, m_mix_norm, m_w_in, m_conv_w, m_attn_sinks, m_w_out, m_ffn2_norm, m_ffn2_w_gate, m_ffn2_w_up, m_ffn2_w_down, m_final_norm, v_ffn1_norm, v_ffn1_w_gate, v_ffn1_w_up, v_ffn1_w_down, v_mix_norm, v_w_in, v_conv_w, v_attn_sinks, v_w_out, v_ffn2_norm, v_ffn2_w_gate, v_ffn2_w_up, v_ffn2_w_down, v_final_norm):
    T, D = x.shape[1], x.shape[2]
    chip = (2 * lax.axis_index("x") + lax.axis_index("y")).astype(jnp.int32)
    core = lax.axis_index("c").astype(jnp.int32)
    place = jnp.stack([chip, core])
    x0 = x[0]
    target = loss_target[0]
    gf = final_norm.reshape(1, D)

    tr = lambda w: jnp.swapaxes(w[0], 0, 1)
    big = [tr(ffn1_w_gate), tr(ffn1_w_up), ffn1_w_down[0], tr(w_in), w_out[0], tr(ffn2_w_gate), tr(ffn2_w_up), ffn2_w_down[0]]
    transposed = [True, True, False, True, False, True, True, False]
    own_b = [w.astype(BF16) for w in big]

    def whole(gathered, own):
        return lax.dynamic_update_slice(gathered, own[None], (chip, 0, 0)).reshape(-1, D)

    tab, got1 = _rope_tables(T, "rope_gather_ffn1", _gather_plan(own_b[0:3]))
    wg1, wu1, wd1 = (whole(g, o) for g, o in zip(got1, own_b[0:3]))

    res = _ffn_fwd(x0, ffn1_norm, wg1, wu1, wd1, "ffn1_fwd", _gather_plan(own_b[3:8], [conv_w[0]]))
    x1, h1, gate1, up1, act1 = res[:5]
    win, wout, wg2, wu2, wd2 = (whole(g, o) for g, o in zip(res[5:10], own_b[3:8]))
    convw4 = lax.dynamic_update_slice(res[10], conv_w, (chip, 0, 0))
    convw = jnp.transpose(convw4, (1, 0, 2)).reshape(3, -1)
    z, hm = _norm_matmul(x1, mix_norm, win, tab, "mix_in_fwd")
    ymix = _mix_core_fwd(z, convw, attn_sinks, "mix_core_fwd")
    dx3, h2, gate2, up2, act2, dgf, loss_part, x2 = _ffn_fwd(x1, ffn2_norm, wg2, wu2, wd2, "ffn2_fwd",
                                                             head=(gf, target), pre=(ymix, wout))

    dx2, dyb2, dgate2, dup2, dg2 = _ffn_bwd(dx3, x2, ffn2_norm, gate2, up2, wg2, wu2, wd2, "ffn2_bwd")
    dz, dcw, dsk, dx2b = _mix_core_bwd(z, dx2, wout, tab, convw, attn_sinks, "mix_core_bwd")
    dx1, dgm = _matmul_norm_bwd(dz, win, x1, mix_norm, dx2, "mix_in_bwd")
    dx0, dyb1, dgate1, dup1, dg1 = _ffn_bwd(dx1, x0, ffn1_norm, gate1, up1, wg1, wu1, wd1, "ffn1_bwd")

    pad = lambda a: jnp.pad(a, ((0, 0), (0, LANES - a.shape[1])))
    vec = jnp.concatenate([dg1, dgm, dg2, dgf, dcw[0:3].reshape(1, -1), pad(dsk[:, 0].reshape(1, -1)),
                           pad(loss_part[:, 0:1])], axis=1)

    jobs = [("ffn2_dwg", dgate2, h2, 5), ("ffn2_dwu", dup2, h2, 6), ("ffn2_dwd", act2, dyb2, 7),
            ("ffn1_dwg", dgate1, h1, 0), ("ffn1_dwu", dup1, h1, 1), ("ffn1_dwd", act1, dyb1, 2),
            ("mix_dwin", dz, hm, 3), ("mix_dwout", ymix, dx2b, 4)]
    n_jobs = len(jobs)
    grad, grad_b, from_sib, pair_b, from_chips, half, g_big = ({} for _ in range(7))

    def stage_plans(t):
        plans, takers = [], []
        if 0 <= t - 1 < n_jobs:
            plans.append(_sibling_plan([grad_b[t - 1]]))
            takers.append((from_sib, t - 1))
        if 0 <= t - 2 < n_jobs:
            plans.append(_scatter_plan([pair_b[t - 2]]))
            takers.append((from_chips, t - 2))
        if 0 <= t - 3 < n_jobs:
            plans.append(_join_plan([half[t - 3]]))
            takers.append((g_big, jobs[t - 3][3]))
        return plans, takers

    def after_stage(t, landed, takers):
        for (store, key), arr in zip(takers, landed):
            store[key] = arr
        pair = (grad[t - 1], from_sib[t - 1]) if 0 <= t - 1 < n_jobs else None
        chip = (grad[t - 2], from_sib[t - 2], from_chips[t - 2]) if 0 <= t - 2 < n_jobs else None
        if pair or chip:
            sums = list(_local_sums(pair, chip, place, f"local_sums_{t}"))
            if pair:
                pair_b[t - 1] = sums.pop(0)
            if chip:
                half[t - 2] = sums.pop(0)

    for t, (name_, a, b, _) in enumerate(jobs):
        plans, takers = stage_plans(t)
        if t == 0:
            plans.append(_all_gather_plan(jnp.pad(vec, ((0, 7), (0, 0)))))
        res = _matmul_tn(a, b, DW_ROW_SPLIT, name_, _merge_plans(plans))
        grad[t], grad_b[t] = (r.reshape(N_CHIPS, -1, D) for r in res[:2])
        landed = list(res[2:])
        if t == 0:
            vec_blocks = landed.pop()
        after_stage(t, landed, takers)

    ws = big
    ms = [tr(m_ffn1_w_gate), tr(m_ffn1_w_up), m_ffn1_w_down[0], tr(m_w_in), m_w_out[0], tr(m_ffn2_w_gate), tr(m_ffn2_w_up), m_ffn2_w_down[0]]
    vs = [tr(v_ffn1_w_gate), tr(v_ffn1_w_up), v_ffn1_w_down[0], tr(v_w_in), v_w_out[0], tr(v_ffn2_w_gate), tr(v_ffn2_w_up), v_ffn2_w_down[0]]
    for t in range(n_jobs, n_jobs + 3):
        plans, takers = stage_plans(t)
        after_stage(t, _run_comm(_merge_plans(plans), f"grads_tail_{t - n_jobs}"), takers)
    upd = {}
    for name_, idx in (("adamw_a", [0, 1, 2, 4]), ("adamw_b", [3, 5, 6, 7])):
        k = len(idx)
        res = _adamw([ws[i] for i in idx], [g_big[i] for i in idx], [ms[i] for i in idx], [vs[i] for i in idx], ADAMW_ROW_BLOCKS, name_)
        for j, i in enumerate(idx):
            upd[i] = (res[j], res[k + j], res[2 * k + j])
            g_big[i] = res[3 * k + j]

    total = _sum_devices(vec_blocks, "small_sum")[0:1]
    g_n1, g_nm, g_n2, g_nf = (total[:, k * D:(k + 1) * D] for k in range(4))
    cw_full = total[:, 4 * D:4 * D + 3 * CONV_WIDTH].reshape(3, CONV_WIDTH)
    cq = CONV_WIDTH // N_CHIPS
    g_cw = lax.dynamic_slice(cw_full, (0, chip * cq), (3, cq))
    off = 4 * D + 3 * CONV_WIDTH
    g_sk = total[:, off:off + N_Q_HEADS]
    loss = total[0, off + LANES]

    sw = [ffn1_norm, mix_norm, conv_w[0], attn_sinks, ffn2_norm, gf]
    sg = [g_n1, g_nm, g_cw, g_sk, g_n2, g_nf]
    sm = [m_ffn1_norm, m_mix_norm, m_conv_w[0], m_attn_sinks, m_ffn2_norm, m_final_norm.reshape(1, D)]
    sv = [v_ffn1_norm, v_mix_norm, v_conv_w[0], v_attn_sinks, v_ffn2_norm, v_final_norm.reshape(1, D)]
    sres = _adamw(sw, sg, sm, sv, 1, "adamw_small")
    supd = [(sres[j], sres[6 + j], sres[12 + j]) for j in range(6)]

    order = [("s", 0), ("b", 0), ("b", 1), ("b", 2), ("s", 1), ("b", 3), ("s", 2), ("s", 3), ("b", 4),
             ("s", 4), ("b", 5), ("b", 6), ("b", 7), ("s", 5)]

    def leaf(kind, i, which):
        if kind == "b":
            a = g_big[i] if which == 0 else upd[i][which - 1]
            return (jnp.swapaxes(a, 0, 1) if transposed[i] else a)[None]
        a = sg[i] if which == 0 else supd[i][which - 1]
        if i == 2:
            return a[None]
        if i == 5:
            return a.reshape(D)
        return a

    outs = [loss, dx0[None]]
    for which in range(4):
        outs += [leaf(kind, i, which) for kind, i in order]
    return tuple(outs)
```

```python
import jax
import jax.numpy as jnp
import numpy as np
from jax import lax
from jax.experimental import pallas as pl
from jax.experimental.pallas import tpu as pltpu

F32 = jnp.float32
BF16 = jnp.bfloat16
MESH = pl.DeviceIdType.MESH

CONV_WIDTH = 512
N_Q_HEADS = 8
HEAD_DIM = 64
BLOCK = 128
ROPE_THETA = 500000.0
ROT_DIM = 16
RMS_EPS = 1e-5
MASK_VALUE = -1e30
ATTN_SCALE = HEAD_DIM ** -0.5
FFN_RES_SCALE = 0.5
ADAM_LR = 0.001
ADAM_B1 = 0.9
ADAM_B2 = 0.999
ADAM_EPS = 1e-08
ADAM_WD = 0.01
ADAM_STEP = 10

N_CHIPS = 4
N_DEV = 8
LANES = 128
VMEM_LIMIT = 56 * 1024 * 1024

_pcall = pl.pallas_call
HBM_SPEC = pl.BlockSpec(memory_space=pltpu.HBM)
ANY_SPEC = pl.BlockSpec(memory_space=pl.ANY)


def _params(n_axes, vmem=VMEM_LIMIT):
    return pltpu.CompilerParams(dimension_semantics=("arbitrary",) * n_axes, vmem_limit_bytes=vmem)


def _dot(a, b):
    return jnp.dot(a, b, preferred_element_type=F32)


def _dot_nt(a, b):
    return lax.dot_general(a, b, (((1,), (1,)), ((), ())), preferred_element_type=F32)


def _dot_tn(a, b):
    return lax.dot_general(a, b, (((0,), (0,)), ((), ())), preferred_element_type=F32)


def _rms_inv(x):
    return lax.rsqrt(jnp.mean(x * x, axis=-1, keepdims=True) + RMS_EPS)


def _norm_bwd(dh, x, g):
    inv = _rms_inv(x)
    xhat = x * inv
    dg = jnp.sum(dh * xhat, axis=0, keepdims=True)
    dxhat = dh * g
    dx = inv * (dxhat - xhat * jnp.mean(dxhat * xhat, axis=-1, keepdims=True))
    return dx, dg


def _place():
    x, y, c = lax.axis_index("x"), lax.axis_index("y"), lax.axis_index("c")
    chips = [(1 - x, y), (x, 1 - y), (1 - x, 1 - y)]
    return x, y, c, chips


class _Plan:
    def __init__(self, arrays, out_shapes, n_sems, start, finish, middle=None, aliases=None):
        self.arrays, self.out_shapes, self.n_sems = list(arrays), list(out_shapes), n_sems
        self.start, self.finish, self.middle = start, finish, middle
        self.aliases = dict(aliases or {})

    def specs(self):
        k = len(self.arrays)
        sems = [pltpu.SemaphoreType.DMA((self.n_sems,)), pltpu.SemaphoreType.DMA((self.n_sems,))]
        return [HBM_SPEC] * k, [HBM_SPEC] * len(self.out_shapes), self.out_shapes, sems


class _SemSlice:
    def __init__(self, ref, offset):
        self.ref, self.offset = ref, offset

    @property
    def at(self):
        return self

    def __getitem__(self, k):
        return self.ref.at[k + self.offset]


def _merge_plans(plans):
    plans = [p for p in plans if p is not None]
    if len(plans) <= 1:
        return plans[0] if plans else None
    arrays, shapes, aliases, spans, n_sems = [], [], {}, [], 0
    for p in plans:
        a0, o0 = len(arrays), len(shapes)
        spans.append((a0, a0 + len(p.arrays), o0, o0 + len(p.out_shapes), n_sems))
        aliases.update({a0 + i: o0 + j for i, j in p.aliases.items()})
        arrays += p.arrays
        shapes += p.out_shapes
        n_sems += p.n_sems

    def run(which):
        def fn(ins, outs, send_sems, recv_sems):
            for p, (a0, a1, o0, o1, s0) in zip(plans, spans):
                part = getattr(p, which)
                if part is not None:
                    part(ins[a0:a1], outs[o0:o1], _SemSlice(send_sems, s0), _SemSlice(recv_sems, s0))
        return fn

    middle = run("middle") if any(p.middle is not None for p in plans) else None
    return _Plan(arrays, shapes, n_sems, run("start"), run("finish"), middle, aliases)


def _sibling_plan(grads_b):
    n = len(grads_b)

    def copies(ins, outs, send_sems, recv_sems):
        x, y, c, _ = _place()

        def copy(t):
            half = ins[t].shape[1] // 2
            return pltpu.make_async_remote_copy(
                src_ref=ins[t].at[:, pl.ds(pl.multiple_of((1 - c) * half, 16), half), :], dst_ref=outs[t],
                send_sem=send_sems.at[t], recv_sem=recv_sems.at[t], device_id=(x, y, 1 - c), device_id_type=MESH)

        return [copy(t) for t in range(n)]

    def start(*refs):
        for cp in copies(*refs):
            cp.start()

    def finish(*refs):
        for cp in copies(*refs):
            cp.wait()

    shapes = [jax.ShapeDtypeStruct((g.shape[0], g.shape[1] // 2, g.shape[2]), g.dtype) for g in grads_b]
    return _Plan(grads_b, shapes, n, start, finish)


def _scatter_plan(parts_b):
    n = len(parts_b)

    def copies(ins, outs, send_sems, recv_sems):
        x, y, c, chips = _place()

        def copy(t, j):
            px, py = chips[j]
            return pltpu.make_async_remote_copy(
                src_ref=ins[t].at[2 * px + py], dst_ref=outs[t].at[j], send_sem=send_sems.at[3 * t + j],
                recv_sem=recv_sems.at[3 * t + j], device_id=(px, py, c), device_id_type=MESH)

        return [copy(t, j) for t in range(n) for j in range(3)]

    def start(*refs):
        for cp in copies(*refs):
            cp.start()

    def finish(*refs):
        for cp in copies(*refs):
            cp.wait()

    shapes = [jax.ShapeDtypeStruct((3, *p.shape[1:]), p.dtype) for p in parts_b]
    return _Plan(parts_b, shapes, 3 * n, start, finish)


def _gather_plan(shards, small=()):
    n, ns = len(shards), len(small)
    per = 8

    def parts(ins, outs, send_sems, recv_sems):
        x, y, c, chips = _place()
        me = 2 * x + y
        blocks = [2 * px + py for px, py in chips]

        def rows(t, core, piece=None):
            half = ins[t].shape[0] // 2
            if piece is None:
                return pl.ds(pl.multiple_of(core * half, 16), half)
            return pl.ds(pl.multiple_of(core * half + piece * (half // 2), 16), half // 2)

        def remote(src, dst, k, device):
            return pltpu.make_async_remote_copy(src_ref=src, dst_ref=dst, send_sem=send_sems.at[k],
                                                recv_sem=recv_sems.at[k], device_id=device, device_id_type=MESH)

        def first(t, j, block, core):
            return remote(ins[t].at[rows(t, core), :], outs[t].at[block, rows(t, core), :], per * t + j, (*chips[j], c))

        def relay(t, j, block, core):
            ref = outs[t].at[block, rows(t, core, j), :]
            return remote(ref, ref, per * t + 2 + j, (*chips[j], c))

        def passed(t, k, block, core, piece=None):
            ref = outs[t].at[block, rows(t, core, piece), :]
            return remote(ref, ref, per * t + 4 + k, (x, y, 1 - c))

        def whole(s, j, block):
            return remote(ins[n + s], outs[n + s].at[block], per * n + 3 * s + j, (*chips[j], c))

        return c, me, blocks, first, relay, passed, whole

    def start(*refs):
        c, me, _, first, _, _, whole = parts(*refs)
        for t in range(n):
            for j in range(2):
                first(t, j, me, c).start()
        for s in range(ns):
            for j in range(3):
                whole(s, j, me).start()

    def middle(*refs):
        c, _, blocks, first, relay, passed, _ = parts(*refs)
        for t in range(n):
            for j in range(2):
                first(t, j, blocks[j], c).wait_recv()
                passed(t, j, blocks[j], c).start()
                relay(t, 1 - j, blocks[j], c).start()

    def finish(*refs):
        c, me, blocks, first, relay, passed, whole = parts(*refs)
        for t in range(n):
            for j in range(2):
                relay(t, j, blocks[2], c).wait_recv()
                passed(t, 2 + j, blocks[2], c, j).start()
        for t in range(n):
            for j in range(2):
                passed(t, j, blocks[j], 1 - c).wait_recv()
                passed(t, 2 + j, blocks[2], 1 - c, j).wait_recv()
        for s in range(ns):
            for j in range(3):
                whole(s, j, blocks[j]).wait_recv()
        for t in range(n):
            for j in range(2):
                first(t, j, me, c).wait_send()
                relay(t, 1 - j, blocks[j], c).wait_send()
                passed(t, j, blocks[j], c).wait_send()
                passed(t, 2 + j, blocks[2], c, j).wait_send()
        for s in range(ns):
            for j in range(3):
                whole(s, j, me).wait_send()

    arrays = [*shards, *small]
    shapes = [jax.ShapeDtypeStruct((N_CHIPS, *a.shape), a.dtype) for a in arrays]
    return _Plan(arrays, shapes, per * n + 3 * ns, start, finish, middle)


def _run_comm(plan, name):
    k = len(plan.arrays)
    in_specs, out_specs, out_shape, sems = plan.specs()

    def body(*refs):
        cr = (refs[:k], refs[k:k + len(out_shape)], refs[-2], refs[-1])
        plan.start(*cr)
        if plan.middle is not None:
            plan.middle(*cr)
        plan.finish(*cr)

    return _pcall(body, name=name, in_specs=in_specs, out_specs=out_specs, out_shape=out_shape,
                  input_output_aliases=plan.aliases, scratch_shapes=sems)(*plan.arrays)


def _carried(plan, in_specs, out_specs, out_shape, scratch):
    aliases = {}
    if plan is not None:
        p_in, p_out, p_shape, p_sems = plan.specs()
        aliases = {len(in_specs) + i: len(out_specs) + j for i, j in plan.aliases.items()}
        in_specs, out_specs = in_specs + p_in, out_specs + p_out
        out_shape, scratch = out_shape + p_shape, scratch + p_sems
    return dict(in_specs=in_specs, out_specs=out_specs, out_shape=out_shape, scratch_shapes=scratch,
                input_output_aliases=aliases)


def _unpack(refs, n_in, n_out, plan):
    k_in = len(plan.arrays) if plan else 0
    k_out = len(plan.out_shapes) if plan else 0
    ins = refs[:n_in]
    outs = refs[n_in + k_in:n_in + k_in + n_out]
    rest = refs[n_in + k_in + n_out + k_out:]
    if plan is None:
        return ins, outs, rest, None
    cr = (refs[n_in:n_in + k_in], refs[n_in + k_in + n_out:n_in + k_in + n_out + k_out], rest[-2], rest[-1])
    return ins, outs, rest[:-2], cr


def _hook(plan, cr, which, cond):
    fn = getattr(plan, which) if plan is not None else None
    if fn is not None:
        pl.when(cond)(lambda: fn(*cr))


def _join_plan(shards):
    n = len(shards)

    def copy(ins, outs, send_sems, recv_sems, t, core):
        x, y, c, _ = _place()
        half = ins[t].shape[0] // 2
        rows = pl.ds(pl.multiple_of(core * half, 8), half)
        return pltpu.make_async_remote_copy(
            src_ref=ins[t].at[rows, :], dst_ref=outs[t].at[rows, :], send_sem=send_sems.at[t],
            recv_sem=recv_sems.at[t], device_id=(x, y, 1 - c), device_id_type=MESH)

    def start(*refs):
        c = lax.axis_index("c")
        for t in range(n):
            copy(*refs, t, c).start()

    def finish(*refs):
        c = lax.axis_index("c")
        for t in range(n):
            copy(*refs, t, 1 - c).wait_recv()
        for t in range(n):
            copy(*refs, t, c).wait_send()

    shapes = [jax.ShapeDtypeStruct(s.shape, s.dtype) for s in shards]
    return _Plan(shards, shapes, n, start, finish, aliases={t: t for t in range(n)})


def _all_gather_plan(vec):
    def parts(ins, outs, send_sems, recv_sems):
        x, y, c, _ = _place()
        me = 4 * x + 2 * y + c
        rel = [((k >> 2) & 1, (k >> 1) & 1, k & 1) for k in range(1, N_DEV)]

        def peer(k):
            fx, fy, fc = rel[k]
            return (x ^ fx, y ^ fy, c ^ fc)

        def copy(k, dev):
            return pltpu.make_async_remote_copy(
                src_ref=ins[0], dst_ref=outs[0].at[dev], send_sem=send_sems.at[k], recv_sem=recv_sems.at[k],
                device_id=peer(k), device_id_type=MESH)

        mine = pltpu.make_async_copy(ins[0], outs[0].at[me], send_sems.at[N_DEV - 1])
        return me, peer, copy, mine

    def start(*refs):
        me, _, copy, mine = parts(*refs)
        mine.start()
        for k in range(N_DEV - 1):
            copy(k, me).start()

    def finish(*refs):
        me, peer, copy, mine = parts(*refs)
        for k in range(N_DEV - 1):
            px, py, pc = peer(k)
            copy(k, 4 * px + 2 * py + pc).wait_recv()
        for k in range(N_DEV - 1):
            copy(k, me).wait_send()
        mine.wait()

    return _Plan([vec], [jax.ShapeDtypeStruct((N_DEV, *vec.shape), vec.dtype)], N_DEV, start, finish)


def _sum_devices(blocks, name):
    def body(b_ref, o_ref):
        total = b_ref[0]
        for dev in range(1, N_DEV):
            total = total + b_ref[dev]
        o_ref[...] = total

    return _pcall(body, name=name, in_specs=[pl.BlockSpec(memory_space=pltpu.VMEM)],
                  out_specs=pl.BlockSpec(memory_space=pltpu.VMEM),
                  out_shape=jax.ShapeDtypeStruct(blocks.shape[1:], F32))(blocks)


TOKEN_TILE = 512
PROJ_TOKEN_TILE = 1024
ADAMW_ROW_BLOCKS = 4
LARGE_VMEM_LIMIT = 62 * 1024 * 1024
DW_TOKEN_TILE = 2048
DW_ROW_SPLIT = 2
MXU_COLS = 256
DH_GROUP = 6


def _chunks(n):
    out, c0 = [], 0
    while c0 < n:
        size = min(MXU_COLS, n - c0)
        out.append((c0, size))
        c0 += size
    return out


def _load_weights(hbm_refs, vmem_refs, sems):
    copies = [pltpu.make_async_copy(h, v, sems.at[k]) for k, (h, v) in enumerate(zip(hbm_refs, vmem_refs))]
    for cp in copies:
        cp.start()
    for cp in copies:
        cp.wait()


def _ffn_fwd(x, g, wgt, wut, wd, name, plan=None, head=None, pre=None):
    T, D = x.shape
    F = wgt.shape[0]
    tm = min(T, TOKEN_TILE)
    ni = T // tm
    n_head = 2 if head is not None else 0
    n_pre = 1 if pre is not None else 0

    def body(*refs):
        ins, outs, scratch, cr = _unpack(refs, 5 + n_head + 2 * n_pre, 5 + n_head + n_pre, plan)
        x_ref, g_ref, wg_hbm, wu_hbm, wd_hbm = ins[:5]
        xo_ref, h_ref, gate_ref, up_ref, act_ref = outs[:5]
        wg_ref, wu_ref, wd_ref, sems = scratch
        i = pl.program_id(0)
        _hook(plan, cr, "start", i == 0)

        @pl.when(i == 0)
        def _():
            _load_weights((wg_hbm, wu_hbm, wd_hbm), (wg_ref, wu_ref, wd_ref), sems)

        if pre is not None:
            a_ref, w_ref = ins[5 + n_head:]
            x_ref = outs[5 + n_head]
            x_ref[...] = ins[0][...] + _dot(a_ref[...], w_ref[...])
        xv = x_ref[...]
        h = ((xv * _rms_inv(xv)) * g_ref[...]).astype(BF16)
        h_ref[...] = h
        for c0, size in _chunks(F):
            gate = _dot_nt(h, wg_ref[c0:c0 + size, :])
            up = _dot_nt(h, wu_ref[c0:c0 + size, :])
            gate_ref[:, c0:c0 + size] = gate.astype(BF16)
            up_ref[:, c0:c0 + size] = up.astype(BF16)
            act_ref[:, c0:c0 + size] = (gate * jax.nn.sigmoid(gate) * up).astype(BF16)
        y = x_ref[...] + FFN_RES_SCALE * _dot(act_ref[...], wd_ref[...])
        if head is None:
            xo_ref[...] = y
        else:
            gf_ref, t_ref = ins[5:7]
            dgf_ref, loss_ref = outs[5:7]

            @pl.when(i == 0)
            def _():
                dgf_ref[...] = jnp.zeros_like(dgf_ref)
                loss_ref[...] = jnp.zeros_like(loss_ref)

            gf = gf_ref[...]
            diff = (y * _rms_inv(y)) * gf - t_ref[...]
            loss_ref[...] += 0.5 * jnp.sum(jnp.mean(diff * diff, axis=-1, keepdims=True))
            dy, dgf = _norm_bwd(diff * (1.0 / D), y, gf)
            xo_ref[...] = dy
            dgf_ref[...] += dgf
        _hook(plan, cr, "middle", i == ni // 2)
        _hook(plan, cr, "finish", i == ni - 1)

    const = lambda shape: pl.BlockSpec(shape, lambda i: (0, 0))
    rows = lambda width: pl.BlockSpec((tm, width), lambda i: (i, 0))
    in_specs = [rows(D), const((1, D)), ANY_SPEC, ANY_SPEC, ANY_SPEC]
    out_specs = [rows(D), rows(D), rows(F), rows(F), rows(F)]
    out_shape = [jax.ShapeDtypeStruct((T, D), F32), jax.ShapeDtypeStruct((T, D), BF16),
                 jax.ShapeDtypeStruct((T, F), BF16), jax.ShapeDtypeStruct((T, F), BF16), jax.ShapeDtypeStruct((T, F), BF16)]
    if head is not None:
        in_specs += [const((1, D)), rows(D)]
        out_specs += [const((1, D)), const((1, LANES))]
        out_shape += [jax.ShapeDtypeStruct((1, D), F32), jax.ShapeDtypeStruct((1, LANES), F32)]
    if pre is not None:
        in_specs += [rows(pre[0].shape[1]), const(pre[1].shape)]
        out_specs += [rows(D)]
        out_shape += [jax.ShapeDtypeStruct((T, D), F32)]
    io = _carried(plan, in_specs, out_specs, out_shape,
                  [pltpu.VMEM((F, D), BF16), pltpu.VMEM((F, D), BF16), pltpu.VMEM((F, D), BF16),
                   pltpu.SemaphoreType.DMA((3,))])
    return _pcall(
        body, name=name, grid=(ni,), compiler_params=_params(1, VMEM_LIMIT if pre is None else LARGE_VMEM_LIMIT), **io,
    )(x, g, wgt, wut, wd, *(head or ()), *(pre or ()), *(plan.arrays if plan else ()))


def _ffn_bwd(dy, x, g, gate, up, wgt, wut, wd, name):
    T, D = x.shape
    F = wgt.shape[0]
    tm = min(T, TOKEN_TILE)
    ni = T // tm

    def body(dy_ref, x_ref, g_ref, gate_ref, up_ref, wg_hbm, wu_hbm, wd_hbm,
             dx_ref, dyb_ref, dgate_ref, dup_ref, dg_ref, wg_ref, wu_ref, wd_ref, sems):
        @pl.when(pl.program_id(0) == 0)
        def _():
            _load_weights((wg_hbm, wu_hbm, wd_hbm), (wg_ref, wu_ref, wd_ref), sems)
            dg_ref[...] = jnp.zeros_like(dg_ref)

        dyb = (FFN_RES_SCALE * dy_ref[...]).astype(BF16)
        dyb_ref[...] = dyb
        dh, group_g, group_u, row0 = None, [], [], 0
        chunks = _chunks(F)
        for k, (c0, size) in enumerate(chunks):
            dact = _dot_nt(dyb, wd_ref[c0:c0 + size, :])
            gt = gate_ref[:, c0:c0 + size].astype(F32)
            u = up_ref[:, c0:c0 + size].astype(F32)
            sig = jax.nn.sigmoid(gt)
            dup = (dact * (gt * sig)).astype(BF16)
            dgate = (dact * u * (sig * (1.0 + gt * (1.0 - sig)))).astype(BF16)
            dup_ref[:, c0:c0 + size] = dup
            dgate_ref[:, c0:c0 + size] = dgate
            group_g.append(dgate)
            group_u.append(dup)
            if len(group_g) == DH_GROUP or k == len(chunks) - 1:
                rows = slice(row0, c0 + size)
                part = (_dot(jnp.concatenate(group_g, axis=1), wg_ref[rows, :])
                        + _dot(jnp.concatenate(group_u, axis=1), wu_ref[rows, :]))
                dh = part if dh is None else dh + part
                group_g, group_u, row0 = [], [], c0 + size
        dxn, dg = _norm_bwd(dh, x_ref[...], g_ref[...])
        dx_ref[...] = dy_ref[...] + dxn
        dg_ref[...] += dg

    return _pcall(
        body, name=name, grid=(ni,),
        in_specs=[pl.BlockSpec((tm, D), lambda i: (i, 0)), pl.BlockSpec((tm, D), lambda i: (i, 0)),
                  pl.BlockSpec((1, D), lambda i: (0, 0)),
                  pl.BlockSpec((tm, F), lambda i: (i, 0)), pl.BlockSpec((tm, F), lambda i: (i, 0)),
                  ANY_SPEC, ANY_SPEC, ANY_SPEC],
        out_specs=[pl.BlockSpec((tm, D), lambda i: (i, 0)), pl.BlockSpec((tm, D), lambda i: (i, 0)),
                   pl.BlockSpec((tm, F), lambda i: (i, 0)), pl.BlockSpec((tm, F), lambda i: (i, 0)),
                   pl.BlockSpec((1, D), lambda i: (0, 0))],
        out_shape=[jax.ShapeDtypeStruct((T, D), F32), jax.ShapeDtypeStruct((T, D), BF16),
                   jax.ShapeDtypeStruct((T, F), BF16), jax.ShapeDtypeStruct((T, F), BF16),
                   jax.ShapeDtypeStruct((1, D), F32)],
        scratch_shapes=[pltpu.VMEM((F, D), BF16), pltpu.VMEM((F, D), BF16), pltpu.VMEM((F, D), BF16),
                        pltpu.SemaphoreType.DMA((3,))],
        compiler_params=_params(1, LARGE_VMEM_LIMIT),
    )(dy, x, g, gate, up, wgt, wut, wd)


def _matmul_tn(a, b, row_split, name, plan=None):
    T, n1 = a.shape
    n2 = b.shape[1]
    tn = n1 // row_split
    tk = min(T, DW_TOKEN_TILE)
    nk = T // tk

    def body(*refs):
        (a_ref, b_ref), (o_ref, ob_ref), _, cr = _unpack(refs, 2, 2, plan)
        j = pl.program_id(0)
        k = pl.program_id(1)
        _hook(plan, cr, "start", jnp.logical_and(j == 0, k == 0))

        @pl.when(k == 0)
        def _():
            o_ref[...] = jnp.zeros_like(o_ref)

        o_ref[...] += _dot_tn(a_ref[...], b_ref[...])

        @pl.when(k == nk - 1)
        def _():
            ob_ref[...] = o_ref[...].astype(BF16)

        _hook(plan, cr, "finish", jnp.logical_and(j == row_split - 1, k == nk - 1))

    io = _carried(
        plan,
        [pl.BlockSpec((tk, tn), lambda j, k: (k, j)), pl.BlockSpec((tk, n2), lambda j, k: (k, 0))],
        [pl.BlockSpec((tn, n2), lambda j, k: (j, 0)), pl.BlockSpec((tn, n2), lambda j, k: (j, 0))],
        [jax.ShapeDtypeStruct((n1, n2), F32), jax.ShapeDtypeStruct((n1, n2), BF16)], [])
    return _pcall(
        body, name=name, grid=(row_split, nk), compiler_params=_params(2), **io,
    )(a, b, *(plan.arrays if plan else ()))


def _norm_matmul(x, g, wt, tab, name):
    T, D = x.shape
    n = wt.shape[0]
    tm = min(T, PROJ_TOKEN_TILE)

    def body(x_ref, g_ref, w_ref, tab_ref, z_ref, h_ref):
        xv = x_ref[...]
        h = ((xv * _rms_inv(xv)) * g_ref[...]).astype(BF16)
        h_ref[...] = h
        z = _dot_nt(h, w_ref[...])
        z_ref[:, 0:Z_Q] = z[:, 0:Z_Q]
        tab_v = tab_ref[...]
        for c0 in range(Z_Q, Z_V, LANES):
            z_ref[:, c0:c0 + LANES] = _rot(z[:, c0:c0 + LANES], tab_v)
        z_ref[:, Z_V:Z_END] = z[:, Z_V:Z_END]

    return _pcall(
        body, name=name, grid=(T // tm,),
        in_specs=[pl.BlockSpec((tm, D), lambda i: (i, 0)), pl.BlockSpec((1, D), lambda i: (0, 0)),
                  pl.BlockSpec((n, D), lambda i: (0, 0)), pl.BlockSpec((tm, 3 * LANES), lambda i: (i, 0))],
        out_specs=[pl.BlockSpec((tm, n), lambda i: (i, 0)), pl.BlockSpec((tm, D), lambda i: (i, 0))],
        out_shape=[jax.ShapeDtypeStruct((T, n), F32), jax.ShapeDtypeStruct((T, D), BF16)],
        compiler_params=_params(1),
    )(x, g, wt, tab)


Z_Q = 3 * CONV_WIDTH
Z_K = Z_Q + N_Q_HEADS * HEAD_DIM
Z_V = Z_K + LANES
Z_END = Z_V + LANES


def _rope_tables(T, name, plan):
    half = ROT_DIM // 2
    inv_freq = ROPE_THETA ** (-jnp.arange(0, ROT_DIM, 2, dtype=F32) / ROT_DIM)
    ang = inv_freq[:, None] * jnp.arange(T, dtype=F32)[None, :]
    cos_sin = jnp.concatenate([jnp.cos(ang), jnp.sin(ang)], axis=0)
    select = np.zeros((2 * half, 3 * LANES), np.float32)
    const = np.zeros((1, 3 * LANES), np.float32)
    for lane in range(LANES):
        d = lane % HEAD_DIM
        if d < half:
            select[d, lane] = 1.0
            select[half + d, LANES + lane] = -1.0
        elif d < ROT_DIM:
            select[d - half, lane] = 1.0
            select[d, 2 * LANES + lane] = 1.0
        else:
            const[0, lane] = 1.0
    tm = PROJ_TOKEN_TILE
    ni = T // tm

    def body(*refs):
        (cs_ref, sel_ref, const_ref), (tab_ref,), _, cr = _unpack(refs, 3, 1, plan)
        i = pl.program_id(0)
        _hook(plan, cr, "start", i == 0)
        tab_ref[...] = lax.dot_general(cs_ref[...], sel_ref[...], (((0,), (0,)), ((), ())),
                                       precision=lax.Precision.HIGHEST, preferred_element_type=F32) + const_ref[...]
        _hook(plan, cr, "middle", i == ni - 1)
        _hook(plan, cr, "finish", i == ni - 1)

    io = _carried(
        plan,
        [pl.BlockSpec((2 * half, tm), lambda i: (0, i)), pl.BlockSpec((2 * half, 3 * LANES), lambda i: (0, 0)),
         pl.BlockSpec((1, 3 * LANES), lambda i: (0, 0))],
        [pl.BlockSpec((tm, 3 * LANES), lambda i: (i, 0))], [jax.ShapeDtypeStruct((T, 3 * LANES), F32)], [])
    res = _pcall(body, name=name, grid=(ni,), compiler_params=_params(1), **io)(
        cos_sin, jnp.asarray(select), jnp.asarray(const), *plan.arrays)
    return res[0], res[1:]


def _tab3(tab):
    return tab[:, 0:LANES], tab[:, LANES:2 * LANES], tab[:, 2 * LANES:3 * LANES]


def _rot(x, tab):
    c, s1, s2 = _tab3(tab)
    return x * c + pltpu.roll(x, LANES - ROT_DIM // 2, 1) * s1 + pltpu.roll(x, ROT_DIM // 2, 1) * s2


def _rot_t(d, tab):
    c, s1, s2 = _tab3(tab)
    return d * c + pltpu.roll(d * s1, ROT_DIM // 2, 1) + pltpu.roll(d * s2, LANES - ROT_DIM // 2, 1)


def _head_pads(a):
    lo = lax.broadcasted_iota(jnp.int32, a.shape, 1) < HEAD_DIM
    nat0 = jnp.where(lo, a, 0.0)
    nat1 = jnp.where(lo, 0.0, a)
    return {
        (0, 0): nat0.astype(BF16), (0, 1): pltpu.roll(nat0, HEAD_DIM, 1).astype(BF16),
        (1, 0): pltpu.roll(nat1, HEAD_DIM, 1).astype(BF16), (1, 1): nat1.astype(BF16),
    }


def _from_pads(even, odd, kv):
    lo = lax.broadcasted_iota(jnp.int32, even.shape, 1) < HEAD_DIM
    if kv == 0:
        return jnp.where(lo, even + pltpu.roll(odd, HEAD_DIM, 1), 0.0)
    return jnp.where(lo, 0.0, pltpu.roll(even, HEAD_DIM, 1) + odd)


N_GROUPS = 4


def _group_head(g, r):
    kv, par = divmod(g, 2)
    return 2 * (2 * kv + r) + par


def _window_mask_t(has_prev):
    jj = lax.broadcasted_iota(jnp.int32, (2 * BLOCK, 2 * BLOCK), 0)
    ii = lax.broadcasted_iota(jnp.int32, (2 * BLOCK, 2 * BLOCK), 1) & (BLOCK - 1)
    rel = jj - BLOCK - ii
    return (rel <= 0) & (rel > -BLOCK) & ((jj >= BLOCK) | has_prev)


def _sink_row(sink_ref, g):
    lane = lax.broadcasted_iota(jnp.int32, (1, 2 * BLOCK), 1)
    return jnp.where(lane < BLOCK, sink_ref[0, _group_head(g, 0)], sink_ref[0, _group_head(g, 1)])


def _attn_probs_t(q2, kp, mask, sink_ref):
    out = []
    for kv in range(2):
        q_st = jnp.concatenate([q2[2 * kv], q2[2 * kv + 1]], axis=0)
        for par in range(2):
            s = jnp.where(mask, _dot_nt(kp[(kv, par)], q_st), MASK_VALUE)
            sink = _sink_row(sink_ref, 2 * kv + par)
            m = jnp.maximum(jnp.max(s, axis=0, keepdims=True), sink)
            p = jnp.exp(s - m)
            esink = jnp.exp(sink - m)
            rden = 1.0 / (jnp.sum(p, axis=0, keepdims=True) + esink)
            out.append((p * rden, esink * rden))
    return out


def _conv_taps(cg, u, cg_prev, u_prev, has_prev):
    vv = cg * u
    halo = jnp.where(has_prev, cg_prev * u_prev, 0.0)
    ext = jnp.concatenate([halo, vv], axis=0)
    rows = ext.shape[0]
    vv1 = pltpu.roll(ext, 1, 0)[8:rows]
    vv2 = pltpu.roll(ext, 2, 0)[8:rows]
    return vv, vv1, vv2


MIX_BLOCKS = 4


def _mix_core_fwd(z, conv_w, sinks, name):
    T = z.shape[0]
    rows = MIX_BLOCKS * BLOCK
    steps = T // rows
    prev_block = lambda n: jnp.maximum(MIX_BLOCKS * n - 1, 0)
    prev_rows8 = lambda n: jnp.maximum((rows // 8) * n - 1, 0)

    def body(z_ref, zkvp_ref, cgp_ref, up_ref, cw_ref, sink_ref, y_ref):
        for b in range(MIX_BLOCKS):
            r0 = b * BLOCK
            blk = slice(r0, r0 + BLOCK)
            if b == 0:
                has_prev = pl.program_id(0) > 0
                kv_prev, cg_prev, u_prev = zkvp_ref[...], cgp_ref[...], up_ref[...]
            else:
                has_prev = True
                kv_prev = z_ref[r0 - BLOCK:r0, Z_K:Z_END]
                cg_prev = z_ref[r0 - 8:r0, CONV_WIDTH:2 * CONV_WIDTH]
                u_prev = z_ref[r0 - 8:r0, 2 * CONV_WIDTH:Z_Q]
            bg = z_ref[blk, 0:CONV_WIDTH]
            vv, vv1, vv2 = _conv_taps(z_ref[blk, CONV_WIDTH:2 * CONV_WIDTH], z_ref[blk, 2 * CONV_WIDTH:Z_Q],
                                      cg_prev, u_prev, has_prev)
            conv = cw_ref[0:1, :] * vv2 + cw_ref[1:2, :] * vv1 + cw_ref[2:3, :] * vv
            y_ref[blk, 0:CONV_WIDTH] = (bg * conv).astype(BF16)

            k_all = jnp.concatenate([kv_prev[:, 0:LANES], z_ref[blk, Z_K:Z_V]], axis=0)
            v_all = jnp.concatenate([kv_prev[:, LANES:2 * LANES], z_ref[blk, Z_V:Z_END]], axis=0)
            kp = _head_pads(k_all)
            vp = _head_pads(v_all)
            q2 = [(z_ref[blk, Z_Q + LANES * c:Z_Q + LANES * (c + 1)] * ATTN_SCALE).astype(BF16)
                  for c in range(N_Q_HEADS // 2)]
            probs = _attn_probs_t(q2, kp, _window_mask_t(has_prev), sink_ref)
            for kv in range(2):
                o_t = (_dot_tn(vp[(kv, 0)], probs[2 * kv][0].astype(BF16))
                       + _dot_tn(vp[(kv, 1)], probs[2 * kv + 1][0].astype(BF16)))
                for r in range(2):
                    c = 2 * kv + r
                    y_ref[blk, CONV_WIDTH + LANES * c:CONV_WIDTH + LANES * (c + 1)] = (
                        o_t[:, BLOCK * r:BLOCK * (r + 1)].T.astype(BF16))

    return _pcall(
        body, name=name, grid=(steps,),
        in_specs=[pl.BlockSpec((rows, Z_END), lambda n: (n, 0)),
                  pl.BlockSpec((BLOCK, 2 * LANES), lambda n: (prev_block(n), Z_K // (2 * LANES))),
                  pl.BlockSpec((8, CONV_WIDTH), lambda n: (prev_rows8(n), 1)),
                  pl.BlockSpec((8, CONV_WIDTH), lambda n: (prev_rows8(n), 2)),
                  pl.BlockSpec((3, CONV_WIDTH), lambda n: (0, 0)),
                  pl.BlockSpec(memory_space=pltpu.SMEM)],
        out_specs=pl.BlockSpec((rows, 2 * CONV_WIDTH), lambda n: (n, 0)),
        out_shape=jax.ShapeDtypeStruct((T, 2 * CONV_WIDTH), BF16),
        compiler_params=_params(1),
    )(z, z, z, z, conv_w, sinks)


def _mix_core_bwd(z, dx, wout, win, x, g, tab, conv_w, sinks, name):
    T = z.shape[0]
    D = dx.shape[1]
    nsub = MIX_BLOCKS
    rows = nsub * BLOCK
    steps = T // rows
    last = slice(rows - BLOCK, rows)
    cur = lambda n: jnp.minimum(n, steps - 1)
    prev_block = lambda n: jnp.maximum(nsub * cur(n) - 1, 0)
    prev_rows8 = lambda n: jnp.maximum((rows // 8) * cur(n) - 1, 0)
    next_rows8 = lambda n: jnp.minimum((rows // 8) * (cur(n) + 1), T // 8 - 1)

    def body(z_ref, zkvp_ref, cgp_ref, up_ref, bgn_ref, dx_ref, dxn_ref, wo_ref, tab_ref, tabp_ref, cw_ref, sink_ref,
             wi_ref, xe_ref, g_ref, dxe_ref,
             dz_ref, dcw_ref, dsk_ref, dxb_ref, dxo_ref, dg_ref, held_ref, kv_ref):
        n = pl.program_id(0)

        @pl.when(n == 0)
        def _():
            held_ref[...] = jnp.zeros_like(held_ref)
            kv_ref[...] = jnp.zeros_like(kv_ref)
            dcw_ref[...] = jnp.zeros_like(dcw_ref)
            dsk_ref[...] = jnp.zeros_like(dsk_ref)
            dg_ref[...] = jnp.zeros_like(dg_ref)

        def emit_held():
            dz_ref[:, 0:Z_K] = held_ref[:, 0:Z_K]
            if nsub > 1:
                dz_ref[0:rows - BLOCK, Z_K:Z_END] = held_ref[0:rows - BLOCK, Z_K:Z_END]

        def project_emitted():
            dxn, dg = _norm_bwd(_dot(dz_ref[...], wi_ref[...]), xe_ref[...], g_ref[...])
            dxo_ref[...] = dxe_ref[...] + dxn
            dg_ref[...] += dg

        @pl.when(n < steps)
        def _():
            emit_held()
            dxb = dx_ref[...].astype(BF16)
            dxb_ref[...] = dxb
            dy = _dot_nt(dxb, wo_ref[...])
            dy_next = _dot_nt(dxn_ref[...].astype(BF16), wo_ref[0:CONV_WIDTH, :])
            w0, w1, w2 = cw_ref[0:1, :], cw_ref[1:2, :], cw_ref[2:3, :]
            dk_open, dv_open = kv_ref[:, 0:LANES], kv_ref[:, LANES:2 * LANES]
            for b in range(nsub):
                r0 = b * BLOCK
                blk = slice(r0, r0 + BLOCK)
                before = slice(r0 - BLOCK, r0)
                after8 = slice(r0 + BLOCK, r0 + BLOCK + 8)
                if b == 0:
                    has_prev = n > 0
                    kv_prev, cg_prev, u_prev, tab_p = zkvp_ref[...], cgp_ref[...], up_ref[...], tabp_ref[...]
                else:
                    has_prev = True
                    kv_prev, tab_p = z_ref[before, Z_K:Z_END], tab_ref[before, :]
                    cg_prev = z_ref[r0 - 8:r0, CONV_WIDTH:2 * CONV_WIDTH]
                    u_prev = z_ref[r0 - 8:r0, 2 * CONV_WIDTH:Z_Q]
                if b == nsub - 1:
                    dconv_next = jnp.where(n < steps - 1, dy_next * bgn_ref[...], 0.0)
                else:
                    dconv_next = dy[after8, 0:CONV_WIDTH] * z_ref[after8, 0:CONV_WIDTH]
                bg = z_ref[blk, 0:CONV_WIDTH]
                cg = z_ref[blk, CONV_WIDTH:2 * CONV_WIDTH]
                u = z_ref[blk, 2 * CONV_WIDTH:Z_Q]
                vv, vv1, vv2 = _conv_taps(cg, u, cg_prev, u_prev, has_prev)
                dyc = dy[blk, 0:CONV_WIDTH]
                dbg = dyc * (w0 * vv2 + w1 * vv1 + w2 * vv)
                dconv = dyc * bg
                ext = jnp.concatenate([dconv, dconv_next], axis=0)
                ext_rows = ext.shape[0]
                dvv = (w2 * dconv + w1 * pltpu.roll(ext, ext_rows - 1, 0)[0:BLOCK]
                       + w0 * pltpu.roll(ext, ext_rows - 2, 0)[0:BLOCK])
                dcw_ref[0:1, :] += jnp.sum(dconv * vv2, axis=0, keepdims=True)
                dcw_ref[1:2, :] += jnp.sum(dconv * vv1, axis=0, keepdims=True)
                dcw_ref[2:3, :] += jnp.sum(dconv * vv, axis=0, keepdims=True)

                tab_c = tab_ref[blk, :]
                k_all = jnp.concatenate([kv_prev[:, 0:LANES], z_ref[blk, Z_K:Z_V]], axis=0)
                v_all = jnp.concatenate([kv_prev[:, LANES:2 * LANES], z_ref[blk, Z_V:Z_END]], axis=0)
                kp = _head_pads(k_all)
                vp = _head_pads(v_all)
                chunks = range(N_Q_HEADS // 2)
                q2 = [(z_ref[blk, Z_Q + LANES * c:Z_Q + LANES * (c + 1)] * ATTN_SCALE).astype(BF16) for c in chunks]
                do2 = [dy[blk, CONV_WIDTH + LANES * c:CONV_WIDTH + LANES * (c + 1)].astype(BF16) for c in chunks]
                probs = _attn_probs_t(q2, kp, _window_mask_t(has_prev), sink_ref)
                dq_chunks = []
                dk_nat = jnp.zeros((2 * BLOCK, LANES), F32)
                dv_nat = jnp.zeros((2 * BLOCK, LANES), F32)
                for kv in range(2):
                    q_st = jnp.concatenate([q2[2 * kv], q2[2 * kv + 1]], axis=0)
                    do_st = jnp.concatenate([do2[2 * kv], do2[2 * kv + 1]], axis=0)
                    dq_t = jnp.zeros((LANES, 2 * BLOCK), F32)
                    dk_par, dv_par = [], []
                    for par in range(2):
                        g = 2 * kv + par
                        pr, psink = probs[g]
                        dp = _dot_nt(vp[(kv, par)], do_st)
                        delta = jnp.sum(dp * pr, axis=0, keepdims=True)
                        ds = (pr * (dp - delta)).astype(BF16)
                        dsink = -psink * delta
                        for r in range(2):
                            h = _group_head(g, r)
                            dsk_ref[h:h + 1, :] += jnp.sum(dsink[:, BLOCK * r:BLOCK * (r + 1)])
                        dq_t = dq_t + _dot_tn(kp[(kv, par)], ds)
                        dk_par.append(_dot(ds, q_st))
                        dv_par.append(_dot(pr.astype(BF16), do_st))
                    for r in range(2):
                        dq_chunks.append(_rot_t(dq_t[:, BLOCK * r:BLOCK * (r + 1)].T * ATTN_SCALE, tab_c))
                    dk_nat = dk_nat + _from_pads(dk_par[0], dk_par[1], kv)
                    dv_nat = dv_nat + _from_pads(dv_par[0], dv_par[1], kv)

                done_ref, done = (dz_ref, last) if b == 0 else (held_ref, before)
                done_ref[done, Z_K:Z_V] = _rot_t(dk_open + dk_nat[0:BLOCK], tab_p).astype(BF16)
                done_ref[done, Z_V:Z_END] = (dv_open + dv_nat[0:BLOCK]).astype(BF16)
                dk_open, dv_open = dk_nat[BLOCK:2 * BLOCK], dv_nat[BLOCK:2 * BLOCK]
                held_ref[blk, 0:CONV_WIDTH] = dbg.astype(BF16)
                held_ref[blk, CONV_WIDTH:2 * CONV_WIDTH] = (dvv * u).astype(BF16)
                held_ref[blk, 2 * CONV_WIDTH:Z_Q] = (dvv * cg).astype(BF16)
                for c in range(N_Q_HEADS // 2):
                    held_ref[blk, Z_Q + LANES * c:Z_Q + LANES * (c + 1)] = dq_chunks[c].astype(BF16)
            kv_ref[:, 0:LANES] = dk_open
            kv_ref[:, LANES:2 * LANES] = dv_open
            project_emitted()

        @pl.when(n == steps)
        def _():
            emit_held()
            dz_ref[last, Z_K:Z_V] = _rot_t(kv_ref[:, 0:LANES], tab_ref[last, :]).astype(BF16)
            dz_ref[last, Z_V:Z_END] = kv_ref[:, LANES:2 * LANES].astype(BF16)
            project_emitted()

    emitted = lambda n: (jnp.maximum(n - 1, 0), 0)
    return _pcall(
        body, name=name, grid=(steps + 1,),
        in_specs=[pl.BlockSpec((rows, Z_END), lambda n: (cur(n), 0)),
                  pl.BlockSpec((BLOCK, 2 * LANES), lambda n: (prev_block(n), Z_K // (2 * LANES))),
                  pl.BlockSpec((8, CONV_WIDTH), lambda n: (prev_rows8(n), 1)),
                  pl.BlockSpec((8, CONV_WIDTH), lambda n: (prev_rows8(n), 2)),
                  pl.BlockSpec((8, CONV_WIDTH), lambda n: (next_rows8(n), 0)),
                  pl.BlockSpec((rows, dx.shape[1]), lambda n: (cur(n), 0)),
                  pl.BlockSpec((8, dx.shape[1]), lambda n: (next_rows8(n), 0)),
                  pl.BlockSpec(wout.shape, lambda n: (0, 0)),
                  pl.BlockSpec((rows, 3 * LANES), lambda n: (cur(n), 0)),
                  pl.BlockSpec((BLOCK, 3 * LANES), lambda n: (prev_block(n), 0)),
                  pl.BlockSpec((3, CONV_WIDTH), lambda n: (0, 0)),
                  pl.BlockSpec(memory_space=pltpu.SMEM),
                  pl.BlockSpec(win.shape, lambda n: (0, 0)), pl.BlockSpec((rows, D), emitted),
                  pl.BlockSpec((1, D), lambda n: (0, 0)), pl.BlockSpec((rows, D), emitted)],
        out_specs=[pl.BlockSpec((rows, Z_END), emitted),
                   pl.BlockSpec((8, CONV_WIDTH), lambda n: (0, 0)), pl.BlockSpec((8, LANES), lambda n: (0, 0)),
                   pl.BlockSpec((rows, D), lambda n: (cur(n), 0)),
                   pl.BlockSpec((rows, D), emitted), pl.BlockSpec((1, D), lambda n: (0, 0))],
        out_shape=[jax.ShapeDtypeStruct((T, Z_END), BF16), jax.ShapeDtypeStruct((8, CONV_WIDTH), F32),
                   jax.ShapeDtypeStruct((8, LANES), F32), jax.ShapeDtypeStruct(dx.shape, BF16),
                   jax.ShapeDtypeStruct((T, D), F32), jax.ShapeDtypeStruct((1, D), F32)],
        scratch_shapes=[pltpu.VMEM((rows, Z_END), BF16), pltpu.VMEM((BLOCK, 2 * LANES), F32)],
        compiler_params=_params(1, LARGE_VMEM_LIMIT),
    )(z, z, z, z, z, dx, dx, wout, tab, tab, conv_w, sinks, win, x, g, dx)


def _local_sums(pair, chip, place, name):
    arrays, in_specs, out_specs, out_shape = [], [], [], []
    if pair is not None:
        g, sib = pair
        blk = (1, *sib.shape[1:])
        arrays += [g, sib]
        in_specs += [pl.BlockSpec(blk, lambda q, p: (q, p[1], 0)), pl.BlockSpec(blk, lambda q, p: (q, 0, 0))]
        out_specs.append(pl.BlockSpec(blk, lambda q, p: (q, 0, 0)))
        out_shape.append(jax.ShapeDtypeStruct(sib.shape, BF16))
    if chip is not None:
        g2, sib2, recv2 = chip
        blk = (1, *sib2.shape[1:])
        arrays += [g2, sib2, recv2]
        in_specs += [pl.BlockSpec(blk, lambda q, p: (p[0], p[1], 0)), pl.BlockSpec(blk, lambda q, p: (p[0], 0, 0)),
                     pl.BlockSpec(recv2.shape, lambda q, p: (0, 0, 0))]
        out_specs.append(pl.BlockSpec(sib2.shape[1:], lambda q, p: (p[1], 0)))
        out_shape.append(jax.ShapeDtypeStruct(g2.shape[1:], F32))

    def body(place_ref, *refs):
        refs = list(refs)
        ins, outs = refs[:len(arrays)], refs[len(arrays):]
        if pair is not None:
            g_ref, sib_ref = ins[:2]
            outs[0][...] = (g_ref[...] + sib_ref[...].astype(F32)).astype(BF16)
        if chip is not None:
            g_ref, sib_ref, recv_ref = ins[-3:]

            @pl.when(pl.program_id(0) == 0)
            def _():
                total = g_ref[0] + sib_ref[0].astype(F32)
                for j in range(3):
                    total = total + recv_ref[j].astype(F32)
                outs[-1][...] = total

    return _pcall(
        body, name=name,
        grid_spec=pltpu.PrefetchScalarGridSpec(num_scalar_prefetch=1, grid=(N_CHIPS,),
                                               in_specs=in_specs, out_specs=out_specs),
        out_shape=out_shape, compiler_params=_params(1),
    )(place, *arrays)


def _adamw_math(w, g, m, v):
    m = ADAM_B1 * m + (1.0 - ADAM_B1) * g
    v = ADAM_B2 * v + (1.0 - ADAM_B2) * (g * g)
    m_hat = m / (1.0 - ADAM_B1 ** ADAM_STEP)
    v_hat = v / (1.0 - ADAM_B2 ** ADAM_STEP)
    delta = -ADAM_LR * (m_hat / (jnp.sqrt(v_hat) + ADAM_EPS) + ADAM_WD * w)
    return delta, m, v


def _adamw(ws, gs, ms, vs, row_blocks, name):
    n = len(ws)

    def body(*refs):
        w, g, m, v = refs[:n], refs[n:2 * n], refs[2 * n:3 * n], refs[3 * n:4 * n]
        d, mo, vo, go = refs[4 * n:5 * n], refs[5 * n:6 * n], refs[6 * n:7 * n], refs[7 * n:]
        for t in range(n):
            gv = g[t][...]
            delta, m_new, v_new = _adamw_math(w[t][...], gv, m[t][...], v[t][...])
            d[t][...] = delta
            mo[t][...] = m_new
            vo[t][...] = v_new
            go[t][...] = gv

    specs = [pl.BlockSpec((a.shape[0] // row_blocks, a.shape[1]), lambda i: (i, 0)) for a in ws]
    shapes = [jax.ShapeDtypeStruct(a.shape, F32) for a in ws]
    return _pcall(
        body, name=name, grid=(row_blocks,), in_specs=specs * 4, out_specs=specs * 4, out_shape=shapes * 4,
        compiler_params=_params(1),
    )(*ws, *gs, *ms, *vs)


def kernel(x, ffn1_norm, ffn1_w_gate, ffn1_w_up, ffn1_w_down, mix_norm, w_in, conv_w, attn_sinks, w_out, ffn2_norm, ffn2_w_gate, ffn2_w_up, ffn2_w_down, final_norm, loss_target, m_ffn1_norm, m_ffn1_w_gate, m_ffn1_w_up, m_ffn1_w_down, m_mix_norm, m_w_in, m_conv_w, m_attn_sinks, m_w_out, m_ffn2_norm, m_ffn2_w_gate, m_ffn2_w_up, m_ffn2_w_down, m_final_norm, v_ffn1_norm, v_ffn1_w_gate, v_ffn1_w_up, v_ffn1_w_down, v_mix_norm, v_w_in, v_conv_w, v_attn_sinks, v_w_out, v_ffn2_norm, v_ffn2_w_gate, v_ffn2_w_up, v_ffn2_w_down, v_final_norm):
    T, D = x.shape[1], x.shape[2]
    chip = (2 * lax.axis_index("x") + lax.axis_index("y")).astype(jnp.int32)
    core = lax.axis_index("c").astype(jnp.int32)
    place = jnp.stack([chip, core])
    x0 = x[0]
    target = loss_target[0]
    gf = final_norm.reshape(1, D)

    tr = lambda w: jnp.swapaxes(w[0], 0, 1)
    big = [tr(ffn1_w_gate), tr(ffn1_w_up), ffn1_w_down[0], tr(w_in), w_out[0], tr(ffn2_w_gate), tr(ffn2_w_up), ffn2_w_down[0]]
    transposed = [True, True, False, True, False, True, True, False]
    own_b = [w.astype(BF16) for w in big]

    def whole(gathered, own):
        return lax.dynamic_update_slice(gathered, own[None], (chip, 0, 0)).reshape(-1, D)

    tab, got1 = _rope_tables(T, "rope_gather_ffn1", _gather_plan(own_b[0:3]))
    wg1, wu1, wd1 = (whole(g, o) for g, o in zip(got1, own_b[0:3]))

    res = _ffn_fwd(x0, ffn1_norm, wg1, wu1, wd1, "ffn1_fwd", _gather_plan(own_b[3:8], [conv_w[0]]))
    x1, h1, gate1, up1, act1 = res[:5]
    win, wout, wg2, wu2, wd2 = (whole(g, o) for g, o in zip(res[5:10], own_b[3:8]))
    convw4 = lax.dynamic_update_slice(res[10], conv_w, (chip, 0, 0))
    convw = jnp.transpose(convw4, (1, 0, 2)).reshape(3, -1)
    z, hm = _norm_matmul(x1, mix_norm, win, tab, "mix_in_fwd")
    ymix = _mix_core_fwd(z, convw, attn_sinks, "mix_core_fwd")
    dx3, h2, gate2, up2, act2, dgf, loss_part, x2 = _ffn_fwd(x1, ffn2_norm, wg2, wu2, wd2, "ffn2_fwd",
                                                             head=(gf, target), pre=(ymix, wout))

    dx2, dyb2, dgate2, dup2, dg2 = _ffn_bwd(dx3, x2, ffn2_norm, gate2, up2, wg2, wu2, wd2, "ffn2_bwd")
    dz, dcw, dsk, dx2b, dx1, dgm = _mix_core_bwd(z, dx2, wout, win, x1, mix_norm, tab, convw, attn_sinks, "mix_bwd")
    dx0, dyb1, dgate1, dup1, dg1 = _ffn_bwd(dx1, x0, ffn1_norm, gate1, up1, wg1, wu1, wd1, "ffn1_bwd")

    pad = lambda a: jnp.pad(a, ((0, 0), (0, LANES - a.shape[1])))
    vec = jnp.concatenate([dg1, dgm, dg2, dgf, dcw[0:3].reshape(1, -1), pad(dsk[:, 0].reshape(1, -1)),
                           pad(loss_part[:, 0:1])], axis=1)

    jobs = [("ffn2_dwg", dgate2, h2, 5), ("ffn2_dwu", dup2, h2, 6), ("ffn2_dwd", act2, dyb2, 7),
            ("ffn1_dwg", dgate1, h1, 0), ("ffn1_dwu", dup1, h1, 1), ("ffn1_dwd", act1, dyb1, 2),
            ("mix_dwin", dz, hm, 3), ("mix_dwout", ymix, dx2b, 4)]
    n_jobs = len(jobs)
    grad, grad_b, from_sib, pair_b, from_chips, half, g_big = ({} for _ in range(7))

    def stage_plans(t):
        plans, takers = [], []
        if 0 <= t - 1 < n_jobs:
            plans.append(_sibling_plan([grad_b[t - 1]]))
            takers.append((from_sib, t - 1))
        if 0 <= t - 2 < n_jobs:
            plans.append(_scatter_plan([pair_b[t - 2]]))
            takers.append((from_chips, t - 2))
        if 0 <= t - 3 < n_jobs:
            plans.append(_join_plan([half[t - 3]]))
            takers.append((g_big, jobs[t - 3][3]))
        return plans, takers

    def after_stage(t, landed, takers):
        for (store, key), arr in zip(takers, landed):
            store[key] = arr
        pair = (grad[t - 1], from_sib[t - 1]) if 0 <= t - 1 < n_jobs else None
        chip = (grad[t - 2], from_sib[t - 2], from_chips[t - 2]) if 0 <= t - 2 < n_jobs else None
        if pair or chip:
            sums = list(_local_sums(pair, chip, place, f"local_sums_{t}"))
            if pair:
                pair_b[t - 1] = sums.pop(0)
            if chip:
                half[t - 2] = sums.pop(0)

    for t, (name_, a, b, _) in enumerate(jobs):
        plans, takers = stage_plans(t)
        if t == 0:
            plans.append(_all_gather_plan(jnp.pad(vec, ((0, 7), (0, 0)))))
        res = _matmul_tn(a, b, DW_ROW_SPLIT, name_, _merge_plans(plans))
        grad[t], grad_b[t] = (r.reshape(N_CHIPS, -1, D) for r in res[:2])
        landed = list(res[2:])
        if t == 0:
            vec_blocks = landed.pop()
        after_stage(t, landed, takers)

    ws = big
    ms = [tr(m_ffn1_w_gate), tr(m_ffn1_w_up), m_ffn1_w_down[0], tr(m_w_in), m_w_out[0], tr(m_ffn2_w_gate), tr(m_ffn2_w_up), m_ffn2_w_down[0]]
    vs = [tr(v_ffn1_w_gate), tr(v_ffn1_w_up), v_ffn1_w_down[0], tr(v_w_in), v_w_out[0], tr(v_ffn2_w_gate), tr(v_ffn2_w_up), v_ffn2_w_down[0]]
    for t in range(n_jobs, n_jobs + 3):
        plans, takers = stage_plans(t)
        after_stage(t, _run_comm(_merge_plans(plans), f"grads_tail_{t - n_jobs}"), takers)
    upd = {}
    for name_, idx in (("adamw_a", [0, 1, 2, 4]), ("adamw_b", [3, 5, 6, 7])):
        k = len(idx)
        res = _adamw([ws[i] for i in idx], [g_big[i] for i in idx], [ms[i] for i in idx], [vs[i] for i in idx], ADAMW_ROW_BLOCKS, name_)
        for j, i in enumerate(idx):
            upd[i] = (res[j], res[k + j], res[2 * k + j])
            g_big[i] = res[3 * k + j]

    total = _sum_devices(vec_blocks, "small_sum")[0:1]
    g_n1, g_nm, g_n2, g_nf = (total[:, k * D:(k + 1) * D] for k in range(4))
    cw_full = total[:, 4 * D:4 * D + 3 * CONV_WIDTH].reshape(3, CONV_WIDTH)
    cq = CONV_WIDTH // N_CHIPS
    g_cw = lax.dynamic_slice(cw_full, (0, chip * cq), (3, cq))
    off = 4 * D + 3 * CONV_WIDTH
    g_sk = total[:, off:off + N_Q_HEADS]
    loss = total[0, off + LANES]

    sw = [ffn1_norm, mix_norm, conv_w[0], attn_sinks, ffn2_norm, gf]
    sg = [g_n1, g_nm, g_cw, g_sk, g_n2, g_nf]
    sm = [m_ffn1_norm, m_mix_norm, m_conv_w[0], m_attn_sinks, m_ffn2_norm, m_final_norm.reshape(1, D)]
    sv = [v_ffn1_norm, v_mix_norm, v_conv_w[0], v_attn_sinks, v_ffn2_norm, v_final_norm.reshape(1, D)]
    sres = _adamw(sw, sg, sm, sv, 1, "adamw_small")
    supd = [(sres[j], sres[6 + j], sres[12 + j]) for j in range(6)]

    order = [("s", 0), ("b", 0), ("b", 1), ("b", 2), ("s", 1), ("b", 3), ("s", 2), ("s", 3), ("b", 4),
             ("s", 4), ("b", 5), ("b", 6), ("b", 7), ("s", 5)]

    def leaf(kind, i, which):
        if kind == "b":
            a = g_big[i] if which == 0 else upd[i][which - 1]
            return (jnp.swapaxes(a, 0, 1) if transposed[i] else a)[None]
        a = sg[i] if which == 0 else supd[i][which - 1]
        if i == 2:
            return a[None]
        if i == 5:
            return a.reshape(D)
        return a

    outs = [loss, dx0[None]]
    for which in range(4):
        outs += [leaf(kind, i, which) for kind, i in order]
    return tuple(outs)
```

```python
import jax
import jax.numpy as jnp
import numpy as np
from jax import lax
from jax.experimental import pallas as pl
from jax.experimental.pallas import tpu as pltpu

F32 = jnp.float32
BF16 = jnp.bfloat16
MESH = pl.DeviceIdType.MESH

CONV_WIDTH = 512
N_Q_HEADS = 8
HEAD_DIM = 64
BLOCK = 128
ROPE_THETA = 500000.0
ROT_DIM = 16
RMS_EPS = 1e-5
MASK_VALUE = -1e30
ATTN_SCALE = HEAD_DIM ** -0.5
FFN_RES_SCALE = 0.5
ADAM_LR = 0.001
ADAM_B1 = 0.9
ADAM_B2 = 0.999
ADAM_EPS = 1e-08
ADAM_WD = 0.01
ADAM_STEP = 10

N_CHIPS = 4
N_DEV = 8
LANES = 128
VMEM_LIMIT = 56 * 1024 * 1024

_pcall = pl.pallas_call
HBM_SPEC = pl.BlockSpec(memory_space=pltpu.HBM)
ANY_SPEC = pl.BlockSpec(memory_space=pl.ANY)


def _params(n_axes, vmem=VMEM_LIMIT):
    return pltpu.CompilerParams(dimension_semantics=("arbitrary",) * n_axes, vmem_limit_bytes=vmem)


def _dot(a, b):
    return jnp.dot(a, b, preferred_element_type=F32)


def _dot_nt(a, b):
    return lax.dot_general(a, b, (((1,), (1,)), ((), ())), preferred_element_type=F32)


def _dot_tn(a, b):
    return lax.dot_general(a, b, (((0,), (0,)), ((), ())), preferred_element_type=F32)


def _rms_inv(x):
    return lax.rsqrt(jnp.mean(x * x, axis=-1, keepdims=True) + RMS_EPS)


def _norm_bwd(dh, x, g):
    inv = _rms_inv(x)
    xhat = x * inv
    dg = jnp.sum(dh * xhat, axis=0, keepdims=True)
    dxhat = dh * g
    dx = inv * (dxhat - xhat * jnp.mean(dxhat * xhat, axis=-1, keepdims=True))
    return dx, dg


def _place():
    x, y, c = lax.axis_index("x"), lax.axis_index("y"), lax.axis_index("c")
    chips = [(1 - x, y), (x, 1 - y), (1 - x, 1 - y)]
    return x, y, c, chips


class _Plan:
    def __init__(self, arrays, out_shapes, n_sems, start, finish, middle=None, aliases=None):
        self.arrays, self.out_shapes, self.n_sems = list(arrays), list(out_shapes), n_sems
        self.start, self.finish, self.middle = start, finish, middle
        self.aliases = dict(aliases or {})

    def specs(self):
        k = len(self.arrays)
        sems = [pltpu.SemaphoreType.DMA((self.n_sems,)), pltpu.SemaphoreType.DMA((self.n_sems,))]
        return [HBM_SPEC] * k, [HBM_SPEC] * len(self.out_shapes), self.out_shapes, sems


class _SemSlice:
    def __init__(self, ref, offset):
        self.ref, self.offset = ref, offset

    @property
    def at(self):
        return self

    def __getitem__(self, k):
        return self.ref.at[k + self.offset]


def _merge_plans(plans):
    plans = [p for p in plans if p is not None]
    if len(plans) <= 1:
        return plans[0] if plans else None
    arrays, shapes, aliases, spans, n_sems = [], [], {}, [], 0
    for p in plans:
        a0, o0 = len(arrays), len(shapes)
        spans.append((a0, a0 + len(p.arrays), o0, o0 + len(p.out_shapes), n_sems))
        aliases.update({a0 + i: o0 + j for i, j in p.aliases.items()})
        arrays += p.arrays
        shapes += p.out_shapes
        n_sems += p.n_sems

    def run(which):
        def fn(ins, outs, send_sems, recv_sems):
            for p, (a0, a1, o0, o1, s0) in zip(plans, spans):
                part = getattr(p, which)
                if part is not None:
                    part(ins[a0:a1], outs[o0:o1], _SemSlice(send_sems, s0), _SemSlice(recv_sems, s0))
        return fn

    middle = run("middle") if any(p.middle is not None for p in plans) else None
    return _Plan(arrays, shapes, n_sems, run("start"), run("finish"), middle, aliases)


def _sibling_plan(grads_b):
    n = len(grads_b)

    def copies(ins, outs, send_sems, recv_sems):
        x, y, c, _ = _place()

        def copy(t):
            half = ins[t].shape[1] // 2
            return pltpu.make_async_remote_copy(
                src_ref=ins[t].at[:, pl.ds(pl.multiple_of((1 - c) * half, 16), half), :], dst_ref=outs[t],
                send_sem=send_sems.at[t], recv_sem=recv_sems.at[t], device_id=(x, y, 1 - c), device_id_type=MESH)

        return [copy(t) for t in range(n)]

    def start(*refs):
        for cp in copies(*refs):
            cp.start()

    def finish(*refs):
        for cp in copies(*refs):
            cp.wait()

    shapes = [jax.ShapeDtypeStruct((g.shape[0], g.shape[1] // 2, g.shape[2]), g.dtype) for g in grads_b]
    return _Plan(grads_b, shapes, n, start, finish)


def _scatter_plan(parts_b):
    n = len(parts_b)

    def copies(ins, outs, send_sems, recv_sems):
        x, y, c, chips = _place()

        def copy(t, j):
            px, py = chips[j]
            return pltpu.make_async_remote_copy(
                src_ref=ins[t].at[2 * px + py], dst_ref=outs[t].at[j], send_sem=send_sems.at[3 * t + j],
                recv_sem=recv_sems.at[3 * t + j], device_id=(px, py, c), device_id_type=MESH)

        return [copy(t, j) for t in range(n) for j in range(3)]

    def start(*refs):
        for cp in copies(*refs):
            cp.start()

    def finish(*refs):
        for cp in copies(*refs):
            cp.wait()

    shapes = [jax.ShapeDtypeStruct((3, *p.shape[1:]), p.dtype) for p in parts_b]
    return _Plan(parts_b, shapes, 3 * n, start, finish)


def _gather_plan(shards, small=()):
    n, ns = len(shards), len(small)
    per = 8

    def parts(ins, outs, send_sems, recv_sems):
        x, y, c, chips = _place()
        me = 2 * x + y
        blocks = [2 * px + py for px, py in chips]

        def rows(t, core, piece=None):
            half = ins[t].shape[0] // 2
            if piece is None:
                return pl.ds(pl.multiple_of(core * half, 16), half)
            return pl.ds(pl.multiple_of(core * half + piece * (half // 2), 16), half // 2)

        def remote(src, dst, k, device):
            return pltpu.make_async_remote_copy(src_ref=src, dst_ref=dst, send_sem=send_sems.at[k],
                                                recv_sem=recv_sems.at[k], device_id=device, device_id_type=MESH)

        def first(t, j, block, core):
            return remote(ins[t].at[rows(t, core), :], outs[t].at[block, rows(t, core), :], per * t + j, (*chips[j], c))

        def relay(t, j, block, core):
            ref = outs[t].at[block, rows(t, core, j), :]
            return remote(ref, ref, per * t + 2 + j, (*chips[j], c))

        def passed(t, k, block, core, piece=None):
            ref = outs[t].at[block, rows(t, core, piece), :]
            return remote(ref, ref, per * t + 4 + k, (x, y, 1 - c))

        def whole(s, j, block):
            return remote(ins[n + s], outs[n + s].at[block], per * n + 3 * s + j, (*chips[j], c))

        return c, me, blocks, first, relay, passed, whole

    def start(*refs):
        c, me, _, first, _, _, whole = parts(*refs)
        for t in range(n):
            for j in range(2):
                first(t, j, me, c).start()
        for s in range(ns):
            for j in range(3):
                whole(s, j, me).start()

    def middle(*refs):
        c, _, blocks, first, relay, passed, _ = parts(*refs)
        for t in range(n):
            for j in range(2):
                first(t, j, blocks[j], c).wait_recv()
                passed(t, j, blocks[j], c).start()
                relay(t, 1 - j, blocks[j], c).start()

    def finish(*refs):
        c, me, blocks, first, relay, passed, whole = parts(*refs)
        for t in range(n):
            for j in range(2):
                relay(t, j, blocks[2], c).wait_recv()
                passed(t, 2 + j, blocks[2], c, j).start()
        for t in range(n):
            for j in range(2):
                passed(t, j, blocks[j], 1 - c).wait_recv()
                passed(t, 2 + j, blocks[2], 1 - c, j).wait_recv()
        for s in range(ns):
            for j in range(3):
                whole(s, j, blocks[j]).wait_recv()
        for t in range(n):
            for j in range(2):
                first(t, j, me, c).wait_send()
                relay(t, 1 - j, blocks[j], c).wait_send()
                passed(t, j, blocks[j], c).wait_send()
                passed(t, 2 + j, blocks[2], c, j).wait_send()
        for s in range(ns):
            for j in range(3):
                whole(s, j, me).wait_send()

    arrays = [*shards, *small]
    shapes = [jax.ShapeDtypeStruct((N_CHIPS, *a.shape), a.dtype) for a in arrays]
    return _Plan(arrays, shapes, per * n + 3 * ns, start, finish, middle)


def _run_comm(plan, name):
    k = len(plan.arrays)
    in_specs, out_specs, out_shape, sems = plan.specs()

    def body(*refs):
        cr = (refs[:k], refs[k:k + len(out_shape)], refs[-2], refs[-1])
        plan.start(*cr)
        if plan.middle is not None:
            plan.middle(*cr)
        plan.finish(*cr)

    return _pcall(body, name=name, in_specs=in_specs, out_specs=out_specs, out_shape=out_shape,
                  input_output_aliases=plan.aliases, scratch_shapes=sems)(*plan.arrays)


def _carried(plan, in_specs, out_specs, out_shape, scratch):
    aliases = {}
    if plan is not None:
        p_in, p_out, p_shape, p_sems = plan.specs()
        aliases = {len(in_specs) + i: len(out_specs) + j for i, j in plan.aliases.items()}
        in_specs, out_specs = in_specs + p_in, out_specs + p_out
        out_shape, scratch = out_shape + p_shape, scratch + p_sems
    return dict(in_specs=in_specs, out_specs=out_specs, out_shape=out_shape, scratch_shapes=scratch,
                input_output_aliases=aliases)


def _unpack(refs, n_in, n_out, plan):
    k_in = len(plan.arrays) if plan else 0
    k_out = len(plan.out_shapes) if plan else 0
    ins = refs[:n_in]
    outs = refs[n_in + k_in:n_in + k_in + n_out]
    rest = refs[n_in + k_in + n_out + k_out:]
    if plan is None:
        return ins, outs, rest, None
    cr = (refs[n_in:n_in + k_in], refs[n_in + k_in + n_out:n_in + k_in + n_out + k_out], rest[-2], rest[-1])
    return ins, outs, rest[:-2], cr


def _hook(plan, cr, which, cond):
    fn = getattr(plan, which) if plan is not None else None
    if fn is not None:
        pl.when(cond)(lambda: fn(*cr))


def _join_plan(shards):
    n = len(shards)

    def copy(ins, outs, send_sems, recv_sems, t, core):
        x, y, c, _ = _place()
        half = ins[t].shape[0] // 2
        rows = pl.ds(pl.multiple_of(core * half, 8), half)
        return pltpu.make_async_remote_copy(
            src_ref=ins[t].at[rows, :], dst_ref=outs[t].at[rows, :], send_sem=send_sems.at[t],
            recv_sem=recv_sems.at[t], device_id=(x, y, 1 - c), device_id_type=MESH)

    def start(*refs):
        c = lax.axis_index("c")
        for t in range(n):
            copy(*refs, t, c).start()

    def finish(*refs):
        c = lax.axis_index("c")
        for t in range(n):
            copy(*refs, t, 1 - c).wait_recv()
        for t in range(n):
            copy(*refs, t, c).wait_send()

    shapes = [jax.ShapeDtypeStruct(s.shape, s.dtype) for s in shards]
    return _Plan(shards, shapes, n, start, finish, aliases={t: t for t in range(n)})


def _all_gather_plan(vec):
    def parts(ins, outs, send_sems, recv_sems):
        x, y, c, _ = _place()
        me = 4 * x + 2 * y + c
        rel = [((k >> 2) & 1, (k >> 1) & 1, k & 1) for k in range(1, N_DEV)]

        def peer(k):
            fx, fy, fc = rel[k]
            return (x ^ fx, y ^ fy, c ^ fc)

        def copy(k, dev):
            return pltpu.make_async_remote_copy(
                src_ref=ins[0], dst_ref=outs[0].at[dev], send_sem=send_sems.at[k], recv_sem=recv_sems.at[k],
                device_id=peer(k), device_id_type=MESH)

        mine = pltpu.make_async_copy(ins[0], outs[0].at[me], send_sems.at[N_DEV - 1])
        return me, peer, copy, mine

    def start(*refs):
        me, _, copy, mine = parts(*refs)
        mine.start()
        for k in range(N_DEV - 1):
            copy(k, me).start()

    def finish(*refs):
        me, peer, copy, mine = parts(*refs)
        for k in range(N_DEV - 1):
            px, py, pc = peer(k)
            copy(k, 4 * px + 2 * py + pc).wait_recv()
        for k in range(N_DEV - 1):
            copy(k, me).wait_send()
        mine.wait()

    return _Plan([vec], [jax.ShapeDtypeStruct((N_DEV, *vec.shape), vec.dtype)], N_DEV, start, finish)


def _sum_devices(blocks, name):
    def body(b_ref, o_ref):
        total = b_ref[0]
        for dev in range(1, N_DEV):
            total = total + b_ref[dev]
        o_ref[...] = total

    return _pcall(body, name=name, in_specs=[pl.BlockSpec(memory_space=pltpu.VMEM)],
                  out_specs=pl.BlockSpec(memory_space=pltpu.VMEM),
                  out_shape=jax.ShapeDtypeStruct(blocks.shape[1:], F32))(blocks)


TOKEN_TILE = 512
PROJ_TOKEN_TILE = 1024
ADAMW_ROW_BLOCKS = 4
LARGE_VMEM_LIMIT = 62 * 1024 * 1024
DW_TOKEN_TILE = 2048
DW_ROW_SPLIT = 2
MXU_COLS = 256
DH_GROUP = 6


def _chunks(n):
    out, c0 = [], 0
    while c0 < n:
        size = min(MXU_COLS, n - c0)
        out.append((c0, size))
        c0 += size
    return out


def _load_weights(hbm_refs, vmem_refs, sems):
    copies = [pltpu.make_async_copy(h, v, sems.at[k]) for k, (h, v) in enumerate(zip(hbm_refs, vmem_refs))]
    for cp in copies:
        cp.start()
    for cp in copies:
        cp.wait()


def _ffn_fwd(x, g, wgt, wut, wd, name, plan=None, head=None, pre=None):
    T, D = x.shape
    F = wgt.shape[0]
    tm = min(T, TOKEN_TILE)
    ni = T // tm
    n_head = 2 if head is not None else 0
    n_pre = 1 if pre is not None else 0

    def body(*refs):
        ins, outs, scratch, cr = _unpack(refs, 5 + n_head + 2 * n_pre, 5 + n_head + n_pre, plan)
        x_ref, g_ref, wg_hbm, wu_hbm, wd_hbm = ins[:5]
        xo_ref, h_ref, gate_ref, up_ref, act_ref = outs[:5]
        wg_ref, wu_ref, wd_ref, sems = scratch
        i = pl.program_id(0)
        _hook(plan, cr, "start", i == 0)

        @pl.when(i == 0)
        def _():
            _load_weights((wg_hbm, wu_hbm, wd_hbm), (wg_ref, wu_ref, wd_ref), sems)

        if pre is not None:
            a_ref, w_ref = ins[5 + n_head:]
            x_ref = outs[5 + n_head]
            x_ref[...] = ins[0][...] + _dot(a_ref[...], w_ref[...])
        xv = x_ref[...]
        h = ((xv * _rms_inv(xv)) * g_ref[...]).astype(BF16)
        h_ref[...] = h
        for c0, size in _chunks(F):
            gate = _dot_nt(h, wg_ref[c0:c0 + size, :])
            up = _dot_nt(h, wu_ref[c0:c0 + size, :])
            gate_ref[:, c0:c0 + size] = gate.astype(BF16)
            up_ref[:, c0:c0 + size] = up.astype(BF16)
            act_ref[:, c0:c0 + size] = (gate * jax.nn.sigmoid(gate) * up).astype(BF16)
        y = x_ref[...] + FFN_RES_SCALE * _dot(act_ref[...], wd_ref[...])
        if head is None:
            xo_ref[...] = y
        else:
            gf_ref, t_ref = ins[5:7]
            dgf_ref, loss_ref = outs[5:7]

            @pl.when(i == 0)
            def _():
                dgf_ref[...] = jnp.zeros_like(dgf_ref)
                loss_ref[...] = jnp.zeros_like(loss_ref)

            gf = gf_ref[...]
            diff = (y * _rms_inv(y)) * gf - t_ref[...]
            loss_ref[...] += 0.5 * jnp.sum(jnp.mean(diff * diff, axis=-1, keepdims=True))
            dy, dgf = _norm_bwd(diff * (1.0 / D), y, gf)
            xo_ref[...] = dy
            dgf_ref[...] += dgf
        _hook(plan, cr, "middle", i == ni // 2)
        _hook(plan, cr, "finish", i == ni - 1)

    const = lambda shape: pl.BlockSpec(shape, lambda i: (0, 0))
    rows = lambda width: pl.BlockSpec((tm, width), lambda i: (i, 0))
    in_specs = [rows(D), const((1, D)), ANY_SPEC, ANY_SPEC, ANY_SPEC]
    out_specs = [rows(D), rows(D), rows(F), rows(F), rows(F)]
    out_shape = [jax.ShapeDtypeStruct((T, D), F32), jax.ShapeDtypeStruct((T, D), BF16),
                 jax.ShapeDtypeStruct((T, F), BF16), jax.ShapeDtypeStruct((T, F), BF16), jax.ShapeDtypeStruct((T, F), BF16)]
    if head is not None:
        in_specs += [const((1, D)), rows(D)]
        out_specs += [const((1, D)), const((1, LANES))]
        out_shape += [jax.ShapeDtypeStruct((1, D), F32), jax.ShapeDtypeStruct((1, LANES), F32)]
    if pre is not None:
        in_specs += [rows(pre[0].shape[1]), const(pre[1].shape)]
        out_specs += [rows(D)]
        out_shape += [jax.ShapeDtypeStruct((T, D), F32)]
    io = _carried(plan, in_specs, out_specs, out_shape,
                  [pltpu.VMEM((F, D), BF16), pltpu.VMEM((F, D), BF16), pltpu.VMEM((F, D), BF16),
                   pltpu.SemaphoreType.DMA((3,))])
    return _pcall(
        body, name=name, grid=(ni,), compiler_params=_params(1, VMEM_LIMIT if pre is None else LARGE_VMEM_LIMIT), **io,
    )(x, g, wgt, wut, wd, *(head or ()), *(pre or ()), *(plan.arrays if plan else ()))


def _ffn_bwd(dy, x, g, gate, up, wgt, wut, wd, name):
    T, D = x.shape
    F = wgt.shape[0]
    tm = min(T, TOKEN_TILE)
    ni = T // tm

    def body(dy_ref, x_ref, g_ref, gate_ref, up_ref, wg_hbm, wu_hbm, wd_hbm,
             dx_ref, dyb_ref, dgate_ref, dup_ref, dg_ref, wg_ref, wu_ref, wd_ref, sems):
        @pl.when(pl.program_id(0) == 0)
        def _():
            _load_weights((wg_hbm, wu_hbm, wd_hbm), (wg_ref, wu_ref, wd_ref), sems)
            dg_ref[...] = jnp.zeros_like(dg_ref)

        dyb = (FFN_RES_SCALE * dy_ref[...]).astype(BF16)
        dyb_ref[...] = dyb
        dh, group_g, group_u, row0 = None, [], [], 0
        chunks = _chunks(F)
        for k, (c0, size) in enumerate(chunks):
            dact = _dot_nt(dyb, wd_ref[c0:c0 + size, :])
            gt = gate_ref[:, c0:c0 + size].astype(F32)
            u = up_ref[:, c0:c0 + size].astype(F32)
            sig = jax.nn.sigmoid(gt)
            dup = (dact * (gt * sig)).astype(BF16)
            dgate = (dact * u * (sig * (1.0 + gt * (1.0 - sig)))).astype(BF16)
            dup_ref[:, c0:c0 + size] = dup
            dgate_ref[:, c0:c0 + size] = dgate
            group_g.append(dgate)
            group_u.append(dup)
            if len(group_g) == DH_GROUP or k == len(chunks) - 1:
                rows = slice(row0, c0 + size)
                part = (_dot(jnp.concatenate(group_g, axis=1), wg_ref[rows, :])
                        + _dot(jnp.concatenate(group_u, axis=1), wu_ref[rows, :]))
                dh = part if dh is None else dh + part
                group_g, group_u, row0 = [], [], c0 + size
        dxn, dg = _norm_bwd(dh, x_ref[...], g_ref[...])
        dx_ref[...] = dy_ref[...] + dxn
        dg_ref[...] += dg

    return _pcall(
        body, name=name, grid=(ni,),
        in_specs=[pl.BlockSpec((tm, D), lambda i: (i, 0)), pl.BlockSpec((tm, D), lambda i: (i, 0)),
                  pl.BlockSpec((1, D), lambda i: (0, 0)),
                  pl.BlockSpec((tm, F), lambda i: (i, 0)), pl.BlockSpec((tm, F), lambda i: (i, 0)),
                  ANY_SPEC, ANY_SPEC, ANY_SPEC],
        out_specs=[pl.BlockSpec((tm, D), lambda i: (i, 0)), pl.BlockSpec((tm, D), lambda i: (i, 0)),
                   pl.BlockSpec((tm, F), lambda i: (i, 0)), pl.BlockSpec((tm, F), lambda i: (i, 0)),
                   pl.BlockSpec((1, D), lambda i: (0, 0))],
        out_shape=[jax.ShapeDtypeStruct((T, D), F32), jax.ShapeDtypeStruct((T, D), BF16),
                   jax.ShapeDtypeStruct((T, F), BF16), jax.ShapeDtypeStruct((T, F), BF16),
                   jax.ShapeDtypeStruct((1, D), F32)],
        scratch_shapes=[pltpu.VMEM((F, D), BF16), pltpu.VMEM((F, D), BF16), pltpu.VMEM((F, D), BF16),
                        pltpu.SemaphoreType.DMA((3,))],
        compiler_params=_params(1, LARGE_VMEM_LIMIT),
    )(dy, x, g, gate, up, wgt, wut, wd)


def _matmul_tn(a, b, row_split, name, plan=None):
    T, n1 = a.shape
    n2 = b.shape[1]
    tn = n1 // row_split
    tk = min(T, DW_TOKEN_TILE)
    nk = T // tk

    def body(*refs):
        (a_ref, b_ref), (o_ref, ob_ref), _, cr = _unpack(refs, 2, 2, plan)
        j = pl.program_id(0)
        k = pl.program_id(1)
        _hook(plan, cr, "start", jnp.logical_and(j == 0, k == 0))

        @pl.when(k == 0)
        def _():
            o_ref[...] = jnp.zeros_like(o_ref)

        o_ref[...] += _dot_tn(a_ref[...], b_ref[...])

        @pl.when(k == nk - 1)
        def _():
            ob_ref[...] = o_ref[...].astype(BF16)

        _hook(plan, cr, "finish", jnp.logical_and(j == row_split - 1, k == nk - 1))

    io = _carried(
        plan,
        [pl.BlockSpec((tk, tn), lambda j, k: (k, j)), pl.BlockSpec((tk, n2), lambda j, k: (k, 0))],
        [pl.BlockSpec((tn, n2), lambda j, k: (j, 0)), pl.BlockSpec((tn, n2), lambda j, k: (j, 0))],
        [jax.ShapeDtypeStruct((n1, n2), F32), jax.ShapeDtypeStruct((n1, n2), BF16)], [])
    return _pcall(
        body, name=name, grid=(row_split, nk), compiler_params=_params(2), **io,
    )(a, b, *(plan.arrays if plan else ()))


def _norm_matmul(x, g, wt, tab, name):
    T, D = x.shape
    n = wt.shape[0]
    tm = min(T, PROJ_TOKEN_TILE)

    def body(x_ref, g_ref, w_ref, tab_ref, z_ref, h_ref):
        xv = x_ref[...]
        h = ((xv * _rms_inv(xv)) * g_ref[...]).astype(BF16)
        h_ref[...] = h
        z = _dot_nt(h, w_ref[...])
        z_ref[:, 0:Z_Q] = z[:, 0:Z_Q]
        tab_v = tab_ref[...]
        for c0 in range(Z_Q, Z_V, LANES):
            z_ref[:, c0:c0 + LANES] = _rot(z[:, c0:c0 + LANES], tab_v)
        z_ref[:, Z_V:Z_END] = z[:, Z_V:Z_END]

    return _pcall(
        body, name=name, grid=(T // tm,),
        in_specs=[pl.BlockSpec((tm, D), lambda i: (i, 0)), pl.BlockSpec((1, D), lambda i: (0, 0)),
                  pl.BlockSpec((n, D), lambda i: (0, 0)), pl.BlockSpec((tm, 3 * LANES), lambda i: (i, 0))],
        out_specs=[pl.BlockSpec((tm, n), lambda i: (i, 0)), pl.BlockSpec((tm, D), lambda i: (i, 0))],
        out_shape=[jax.ShapeDtypeStruct((T, n), F32), jax.ShapeDtypeStruct((T, D), BF16)],
        compiler_params=_params(1),
    )(x, g, wt, tab)


Z_Q = 3 * CONV_WIDTH
Z_K = Z_Q + N_Q_HEADS * HEAD_DIM
Z_V = Z_K + LANES
Z_END = Z_V + LANES


def _rope_tables(T, name, plan):
    half = ROT_DIM // 2
    inv_freq = ROPE_THETA ** (-jnp.arange(0, ROT_DIM, 2, dtype=F32) / ROT_DIM)
    ang = inv_freq[:, None] * jnp.arange(T, dtype=F32)[None, :]
    cos_sin = jnp.concatenate([jnp.cos(ang), jnp.sin(ang)], axis=0)
    select = np.zeros((2 * half, 3 * LANES), np.float32)
    const = np.zeros((1, 3 * LANES), np.float32)
    for lane in range(LANES):
        d = lane % HEAD_DIM
        if d < half:
            select[d, lane] = 1.0
            select[half + d, LANES + lane] = -1.0
        elif d < ROT_DIM:
            select[d - half, lane] = 1.0
            select[d, 2 * LANES + lane] = 1.0
        else:
            const[0, lane] = 1.0
    tm = PROJ_TOKEN_TILE
    ni = T // tm

    def body(*refs):
        (cs_ref, sel_ref, const_ref), (tab_ref,), _, cr = _unpack(refs, 3, 1, plan)
        i = pl.program_id(0)
        _hook(plan, cr, "start", i == 0)
        tab_ref[...] = lax.dot_general(cs_ref[...], sel_ref[...], (((0,), (0,)), ((), ())),
                                       precision=lax.Precision.HIGHEST, preferred_element_type=F32) + const_ref[...]
        _hook(plan, cr, "middle", i == ni - 1)
        _hook(plan, cr, "finish", i == ni - 1)

    io = _carried(
        plan,
        [pl.BlockSpec((2 * half, tm), lambda i: (0, i)), pl.BlockSpec((2 * half, 3 * LANES), lambda i: (0, 0)),
         pl.BlockSpec((1, 3 * LANES), lambda i: (0, 0))],
        [pl.BlockSpec((tm, 3 * LANES), lambda i: (i, 0))], [jax.ShapeDtypeStruct((T, 3 * LANES), F32)], [])
    res = _pcall(body, name=name, grid=(ni,), compiler_params=_params(1), **io)(
        cos_sin, jnp.asarray(select), jnp.asarray(const), *plan.arrays)
    return res[0], res[1:]


def _tab3(tab):
    return tab[:, 0:LANES], tab[:, LANES:2 * LANES], tab[:, 2 * LANES:3 * LANES]


def _rot(x, tab):
    c, s1, s2 = _tab3(tab)
    return x * c + pltpu.roll(x, LANES - ROT_DIM // 2, 1) * s1 + pltpu.roll(x, ROT_DIM // 2, 1) * s2


def _rot_t(d, tab):
    c, s1, s2 = _tab3(tab)
    return d * c + pltpu.roll(d * s1, ROT_DIM // 2, 1) + pltpu.roll(d * s2, LANES - ROT_DIM // 2, 1)


def _head_pads(a):
    lo = lax.broadcasted_iota(jnp.int32, a.shape, 1) < HEAD_DIM
    nat0 = jnp.where(lo, a, 0.0)
    nat1 = jnp.where(lo, 0.0, a)
    return {
        (0, 0): nat0.astype(BF16), (0, 1): pltpu.roll(nat0, HEAD_DIM, 1).astype(BF16),
        (1, 0): pltpu.roll(nat1, HEAD_DIM, 1).astype(BF16), (1, 1): nat1.astype(BF16),
    }


def _from_pads(even, odd, kv):
    lo = lax.broadcasted_iota(jnp.int32, even.shape, 1) < HEAD_DIM
    if kv == 0:
        return jnp.where(lo, even + pltpu.roll(odd, HEAD_DIM, 1), 0.0)
    return jnp.where(lo, 0.0, pltpu.roll(even, HEAD_DIM, 1) + odd)


N_GROUPS = 4


def _group_head(g, r):
    kv, par = divmod(g, 2)
    return 2 * (2 * kv + r) + par


def _window_mask_t(has_prev):
    jj = lax.broadcasted_iota(jnp.int32, (2 * BLOCK, 2 * BLOCK), 0)
    ii = lax.broadcasted_iota(jnp.int32, (2 * BLOCK, 2 * BLOCK), 1) & (BLOCK - 1)
    rel = jj - BLOCK - ii
    return (rel <= 0) & (rel > -BLOCK) & ((jj >= BLOCK) | has_prev)


def _sink_row(sink_ref, g):
    lane = lax.broadcasted_iota(jnp.int32, (1, 2 * BLOCK), 1)
    return jnp.where(lane < BLOCK, sink_ref[0, _group_head(g, 0)], sink_ref[0, _group_head(g, 1)])


def _attn_probs_t(q2, kp, mask, sink_ref):
    out = []
    for kv in range(2):
        q_st = jnp.concatenate([q2[2 * kv], q2[2 * kv + 1]], axis=0)
        for par in range(2):
            s = jnp.where(mask, _dot_nt(kp[(kv, par)], q_st), MASK_VALUE)
            sink = _sink_row(sink_ref, 2 * kv + par)
            m = jnp.maximum(jnp.max(s, axis=0, keepdims=True), sink)
            p = jnp.exp(s - m)
            esink = jnp.exp(sink - m)
            rden = 1.0 / (jnp.sum(p, axis=0, keepdims=True) + esink)
            out.append((p * rden, esink * rden))
    return out


def _conv_taps(cg, u, cg_prev, u_prev, has_prev):
    vv = cg * u
    halo = jnp.where(has_prev, cg_prev * u_prev, 0.0)
    ext = jnp.concatenate([halo, vv], axis=0)
    rows = ext.shape[0]
    vv1 = pltpu.roll(ext, 1, 0)[8:rows]
    vv2 = pltpu.roll(ext, 2, 0)[8:rows]
    return vv, vv1, vv2


MIX_BLOCKS = 4


def _mix_core_fwd(z, conv_w, sinks, name):
    T = z.shape[0]
    rows = MIX_BLOCKS * BLOCK
    steps = T // rows
    prev_block = lambda n: jnp.maximum(MIX_BLOCKS * n - 1, 0)
    prev_rows8 = lambda n: jnp.maximum((rows // 8) * n - 1, 0)

    def body(z_ref, zkvp_ref, cgp_ref, up_ref, cw_ref, sink_ref, y_ref):
        for b in range(MIX_BLOCKS):
            r0 = b * BLOCK
            blk = slice(r0, r0 + BLOCK)
            if b == 0:
                has_prev = pl.program_id(0) > 0
                kv_prev, cg_prev, u_prev = zkvp_ref[...], cgp_ref[...], up_ref[...]
            else:
                has_prev = True
                kv_prev = z_ref[r0 - BLOCK:r0, Z_K:Z_END]
                cg_prev = z_ref[r0 - 8:r0, CONV_WIDTH:2 * CONV_WIDTH]
                u_prev = z_ref[r0 - 8:r0, 2 * CONV_WIDTH:Z_Q]
            bg = z_ref[blk, 0:CONV_WIDTH]
            vv, vv1, vv2 = _conv_taps(z_ref[blk, CONV_WIDTH:2 * CONV_WIDTH], z_ref[blk, 2 * CONV_WIDTH:Z_Q],
                                      cg_prev, u_prev, has_prev)
            conv = cw_ref[0:1, :] * vv2 + cw_ref[1:2, :] * vv1 + cw_ref[2:3, :] * vv
            y_ref[blk, 0:CONV_WIDTH] = (bg * conv).astype(BF16)

            k_all = jnp.concatenate([kv_prev[:, 0:LANES], z_ref[blk, Z_K:Z_V]], axis=0)
            v_all = jnp.concatenate([kv_prev[:, LANES:2 * LANES], z_ref[blk, Z_V:Z_END]], axis=0)
            kp = _head_pads(k_all)
            vp = _head_pads(v_all)
            q2 = [(z_ref[blk, Z_Q + LANES * c:Z_Q + LANES * (c + 1)] * ATTN_SCALE).astype(BF16)
                  for c in range(N_Q_HEADS // 2)]
            probs = _attn_probs_t(q2, kp, _window_mask_t(has_prev), sink_ref)
            for kv in range(2):
                o_t = (_dot_tn(vp[(kv, 0)], probs[2 * kv][0].astype(BF16))
                       + _dot_tn(vp[(kv, 1)], probs[2 * kv + 1][0].astype(BF16)))
                for r in range(2):
                    c = 2 * kv + r
                    y_ref[blk, CONV_WIDTH + LANES * c:CONV_WIDTH + LANES * (c + 1)] = (
                        o_t[:, BLOCK * r:BLOCK * (r + 1)].T.astype(BF16))

    return _pcall(
        body, name=name, grid=(steps,),
        in_specs=[pl.BlockSpec((rows, Z_END), lambda n: (n, 0)),
                  pl.BlockSpec((BLOCK, 2 * LANES), lambda n: (prev_block(n), Z_K // (2 * LANES))),
                  pl.BlockSpec((8, CONV_WIDTH), lambda n: (prev_rows8(n), 1)),
                  pl.BlockSpec((8, CONV_WIDTH), lambda n: (prev_rows8(n), 2)),
                  pl.BlockSpec((3, CONV_WIDTH), lambda n: (0, 0)),
                  pl.BlockSpec(memory_space=pltpu.SMEM)],
        out_specs=pl.BlockSpec((rows, 2 * CONV_WIDTH), lambda n: (n, 0)),
        out_shape=jax.ShapeDtypeStruct((T, 2 * CONV_WIDTH), BF16),
        compiler_params=_params(1),
    )(z, z, z, z, conv_w, sinks)


def _mix_core_bwd(z, dx, wout, win, x, g, tab, conv_w, sinks, name):
    T = z.shape[0]
    D = dx.shape[1]
    nsub = MIX_BLOCKS
    rows = nsub * BLOCK
    steps = T // rows
    last = slice(rows - BLOCK, rows)
    cur = lambda n: jnp.minimum(n, steps - 1)
    prev_block = lambda n: jnp.maximum(nsub * cur(n) - 1, 0)
    prev_rows8 = lambda n: jnp.maximum((rows // 8) * cur(n) - 1, 0)
    next_rows8 = lambda n: jnp.minimum((rows // 8) * (cur(n) + 1), T // 8 - 1)

    def body(z_ref, zkvp_ref, cgp_ref, up_ref, bgn_ref, dx_ref, dxn_ref, wo_ref, tab_ref, tabp_ref, cw_ref, sink_ref,
             wi_ref, xe_ref, g_ref, dxe_ref,
             dz_ref, dcw_ref, dsk_ref, dxb_ref, dxo_ref, dg_ref, held_ref, kv_ref):
        n = pl.program_id(0)

        @pl.when(n == 0)
        def _():
            held_ref[...] = jnp.zeros_like(held_ref)
            kv_ref[...] = jnp.zeros_like(kv_ref)
            dcw_ref[...] = jnp.zeros_like(dcw_ref)
            dsk_ref[...] = jnp.zeros_like(dsk_ref)
            dg_ref[...] = jnp.zeros_like(dg_ref)

        def emit_held():
            dz_ref[:, 0:Z_K] = held_ref[:, 0:Z_K]
            if nsub > 1:
                dz_ref[0:rows - BLOCK, Z_K:Z_END] = held_ref[0:rows - BLOCK, Z_K:Z_END]

        def project_emitted():
            dxn, dg = _norm_bwd(_dot(dz_ref[...], wi_ref[...]), xe_ref[...], g_ref[...])
            dxo_ref[...] = dxe_ref[...] + dxn
            dg_ref[...] += dg

        @pl.when(n < steps)
        def _():
            emit_held()
            dxb = dx_ref[...].astype(BF16)
            dxb_ref[...] = dxb
            dy = _dot_nt(dxb, wo_ref[...])
            dy_next = _dot_nt(dxn_ref[...].astype(BF16), wo_ref[0:CONV_WIDTH, :])
            w0, w1, w2 = cw_ref[0:1, :], cw_ref[1:2, :], cw_ref[2:3, :]
            dk_open, dv_open = kv_ref[:, 0:LANES], kv_ref[:, LANES:2 * LANES]
            for b in range(nsub):
                r0 = b * BLOCK
                blk = slice(r0, r0 + BLOCK)
                before = slice(r0 - BLOCK, r0)
                after8 = slice(r0 + BLOCK, r0 + BLOCK + 8)
                if b == 0:
                    has_prev = n > 0
                    kv_prev, cg_prev, u_prev, tab_p = zkvp_ref[...], cgp_ref[...], up_ref[...], tabp_ref[...]
                else:
                    has_prev = True
                    kv_prev, tab_p = z_ref[before, Z_K:Z_END], tab_ref[before, :]
                    cg_prev = z_ref[r0 - 8:r0, CONV_WIDTH:2 * CONV_WIDTH]
                    u_prev = z_ref[r0 - 8:r0, 2 * CONV_WIDTH:Z_Q]
                if b == nsub - 1:
                    dconv_next = jnp.where(n < steps - 1, dy_next * bgn_ref[...], 0.0)
                else:
                    dconv_next = dy[after8, 0:CONV_WIDTH] * z_ref[after8, 0:CONV_WIDTH]
                bg = z_ref[blk, 0:CONV_WIDTH]
                cg = z_ref[blk, CONV_WIDTH:2 * CONV_WIDTH]
                u = z_ref[blk, 2 * CONV_WIDTH:Z_Q]
                vv, vv1, vv2 = _conv_taps(cg, u, cg_prev, u_prev, has_prev)
                dyc = dy[blk, 0:CONV_WIDTH]
                dbg = dyc * (w0 * vv2 + w1 * vv1 + w2 * vv)
                dconv = dyc * bg
                ext = jnp.concatenate([dconv, dconv_next], axis=0)
                ext_rows = ext.shape[0]
                dvv = (w2 * dconv + w1 * pltpu.roll(ext, ext_rows - 1, 0)[0:BLOCK]
                       + w0 * pltpu.roll(ext, ext_rows - 2, 0)[0:BLOCK])
                dcw_ref[0:1, :] += jnp.sum(dconv * vv2, axis=0, keepdims=True)
                dcw_ref[1:2, :] += jnp.sum(dconv * vv1, axis=0, keepdims=True)
                dcw_ref[2:3, :] += jnp.sum(dconv * vv, axis=0, keepdims=True)

                tab_c = tab_ref[blk, :]
                k_all = jnp.concatenate([kv_prev[:, 0:LANES], z_ref[blk, Z_K:Z_V]], axis=0)
                v_all = jnp.concatenate([kv_prev[:, LANES:2 * LANES], z_ref[blk, Z_V:Z_END]], axis=0)
                kp = _head_pads(k_all)
                vp = _head_pads(v_all)
                chunks = range(N_Q_HEADS // 2)
                q2 = [(z_ref[blk, Z_Q + LANES * c:Z_Q + LANES * (c + 1)] * ATTN_SCALE).astype(BF16) for c in chunks]
                do2 = [dy[blk, CONV_WIDTH + LANES * c:CONV_WIDTH + LANES * (c + 1)].astype(BF16) for c in chunks]
                probs = _attn_probs_t(q2, kp, _window_mask_t(has_prev), sink_ref)
                dq_chunks = []
                dk_nat = jnp.zeros((2 * BLOCK, LANES), F32)
                dv_nat = jnp.zeros((2 * BLOCK, LANES), F32)
                for kv in range(2):
                    q_st = jnp.concatenate([q2[2 * kv], q2[2 * kv + 1]], axis=0)
                    do_st = jnp.concatenate([do2[2 * kv], do2[2 * kv + 1]], axis=0)
                    dq_t = jnp.zeros((LANES, 2 * BLOCK), F32)
                    dk_par, dv_par = [], []
                    for par in range(2):
                        g = 2 * kv + par
                        pr, psink = probs[g]
                        dp = _dot_nt(vp[(kv, par)], do_st)
                        delta = jnp.sum(dp * pr, axis=0, keepdims=True)
                        ds = (pr * (dp - delta)).astype(BF16)
                        dsink = -psink * delta
                        for r in range(2):
                            h = _group_head(g, r)
                            dsk_ref[h:h + 1, :] += jnp.sum(dsink[:, BLOCK * r:BLOCK * (r + 1)])
                        dq_t = dq_t + _dot_tn(kp[(kv, par)], ds)
                        dk_par.append(_dot(ds, q_st))
                        dv_par.append(_dot(pr.astype(BF16), do_st))
                    for r in range(2):
                        dq_chunks.append(_rot_t(dq_t[:, BLOCK * r:BLOCK * (r + 1)].T * ATTN_SCALE, tab_c))
                    dk_nat = dk_nat + _from_pads(dk_par[0], dk_par[1], kv)
                    dv_nat = dv_nat + _from_pads(dv_par[0], dv_par[1], kv)

                done_ref, done = (dz_ref, last) if b == 0 else (held_ref, before)
                done_ref[done, Z_K:Z_V] = _rot_t(dk_open + dk_nat[0:BLOCK], tab_p).astype(BF16)
                done_ref[done, Z_V:Z_END] = (dv_open + dv_nat[0:BLOCK]).astype(BF16)
                dk_open, dv_open = dk_nat[BLOCK:2 * BLOCK], dv_nat[BLOCK:2 * BLOCK]
                held_ref[blk, 0:CONV_WIDTH] = dbg.astype(BF16)
                held_ref[blk, CONV_WIDTH:2 * CONV_WIDTH] = (dvv * u).astype(BF16)
                held_ref[blk, 2 * CONV_WIDTH:Z_Q] = (dvv * cg).astype(BF16)
                for c in range(N_Q_HEADS // 2):
                    held_ref[blk, Z_Q + LANES * c:Z_Q + LANES * (c + 1)] = dq_chunks[c].astype(BF16)
            kv_ref[:, 0:LANES] = dk_open
            kv_ref[:, LANES:2 * LANES] = dv_open
            project_emitted()

        @pl.when(n == steps)
        def _():
            emit_held()
            dz_ref[last, Z_K:Z_V] = _rot_t(kv_ref[:, 0:LANES], tab_ref[last, :]).astype(BF16)
            dz_ref[last, Z_V:Z_END] = kv_ref[:, LANES:2 * LANES].astype(BF16)
            project_emitted()

    emitted = lambda n: (jnp.maximum(n - 1, 0), 0)
    return _pcall(
        body, name=name, grid=(steps + 1,),
        in_specs=[pl.BlockSpec((rows, Z_END), lambda n: (cur(n), 0)),
                  pl.BlockSpec((BLOCK, 2 * LANES), lambda n: (prev_block(n), Z_K // (2 * LANES))),
                  pl.BlockSpec((8, CONV_WIDTH), lambda n: (prev_rows8(n), 1)),
                  pl.BlockSpec((8, CONV_WIDTH), lambda n: (prev_rows8(n), 2)),
                  pl.BlockSpec((8, CONV_WIDTH), lambda n: (next_rows8(n), 0)),
                  pl.BlockSpec((rows, dx.shape[1]), lambda n: (cur(n), 0)),
                  pl.BlockSpec((8, dx.shape[1]), lambda n: (next_rows8(n), 0)),
                  pl.BlockSpec(wout.shape, lambda n: (0, 0)),
                  pl.BlockSpec((rows, 3 * LANES), lambda n: (cur(n), 0)),
                  pl.BlockSpec((BLOCK, 3 * LANES), lambda n: (prev_block(n), 0)),
                  pl.BlockSpec((3, CONV_WIDTH), lambda n: (0, 0)),
                  pl.BlockSpec(memory_space=pltpu.SMEM),
                  pl.BlockSpec(win.shape, lambda n: (0, 0)), pl.BlockSpec((rows, D), emitted),
                  pl.BlockSpec((1, D), lambda n: (0, 0)), pl.BlockSpec((rows, D), emitted)],
        out_specs=[pl.BlockSpec((rows, Z_END), emitted),
                   pl.BlockSpec((8, CONV_WIDTH), lambda n: (0, 0)), pl.BlockSpec((8, LANES), lambda n: (0, 0)),
                   pl.BlockSpec((rows, D), lambda n: (cur(n), 0)),
                   pl.BlockSpec((rows, D), emitted), pl.BlockSpec((1, D), lambda n: (0, 0))],
        out_shape=[jax.ShapeDtypeStruct((T, Z_END), BF16), jax.ShapeDtypeStruct((8, CONV_WIDTH), F32),
                   jax.ShapeDtypeStruct((8, LANES), F32), jax.ShapeDtypeStruct(dx.shape, BF16),
                   jax.ShapeDtypeStruct((T, D), F32), jax.ShapeDtypeStruct((1, D), F32)],
        scratch_shapes=[pltpu.VMEM((rows, Z_END), BF16), pltpu.VMEM((BLOCK, 2 * LANES), F32)],
        compiler_params=_params(1, LARGE_VMEM_LIMIT),
    )(z, z, z, z, z, dx, dx, wout, tab, tab, conv_w, sinks, win, x, g, dx)


def _local_sums(pair, chip, place, name):
    arrays, in_specs, out_specs, out_shape = [], [], [], []
    if pair is not None:
        g, sib = pair
        blk = (1, *sib.shape[1:])
        arrays += [g, sib]
        in_specs += [pl.BlockSpec(blk, lambda q, p: (q, p[1], 0)), pl.BlockSpec(blk, lambda q, p: (q, 0, 0))]
        out_specs.append(pl.BlockSpec(blk, lambda q, p: (q, 0, 0)))
        out_shape.append(jax.ShapeDtypeStruct(sib.shape, BF16))
    if chip is not None:
        g2, sib2, recv2 = chip
        blk = (1, *sib2.shape[1:])
        arrays += [g2, sib2, recv2]
        in_specs += [pl.BlockSpec(blk, lambda q, p: (p[0], p[1], 0)), pl.BlockSpec(blk, lambda q, p: (p[0], 0, 0)),
                     pl.BlockSpec(recv2.shape, lambda q, p: (0, 0, 0))]
        out_specs.append(pl.BlockSpec(sib2.shape[1:], lambda q, p: (p[1], 0)))
        out_shape.append(jax.ShapeDtypeStruct(g2.shape[1:], F32))

    def body(place_ref, *refs):
        refs = list(refs)
        ins, outs = refs[:len(arrays)], refs[len(arrays):]
        if pair is not None:
            g_ref, sib_ref = ins[:2]
            outs[0][...] = (g_ref[...] + sib_ref[...].astype(F32)).astype(BF16)
        if chip is not None:
            g_ref, sib_ref, recv_ref = ins[-3:]

            @pl.when(pl.program_id(0) == 0)
            def _():
                total = g_ref[0] + sib_ref[0].astype(F32)
                for j in range(3):
                    total = total + recv_ref[j].astype(F32)
                outs[-1][...] = total

    return _pcall(
        body, name=name,
        grid_spec=pltpu.PrefetchScalarGridSpec(num_scalar_prefetch=1, grid=(N_CHIPS,),
                                               in_specs=in_specs, out_specs=out_specs),
        out_shape=out_shape, compiler_params=_params(1),
    )(place, *arrays)


def _adamw_math(w, g, m, v):
    m = ADAM_B1 * m + (1.0 - ADAM_B1) * g
    v = ADAM_B2 * v + (1.0 - ADAM_B2) * (g * g)
    m_hat = m / (1.0 - ADAM_B1 ** ADAM_STEP)
    v_hat = v / (1.0 - ADAM_B2 ** ADAM_STEP)
    delta = -ADAM_LR * (m_hat / (jnp.sqrt(v_hat) + ADAM_EPS) + ADAM_WD * w)
    return delta, m, v


def _adamw(ws, gs, ms, vs, row_blocks, name):
    n = len(ws)

    def body(*refs):
        w, g, m, v = refs[:n], refs[n:2 * n], refs[2 * n:3 * n], refs[3 * n:4 * n]
        d, mo, vo, go = refs[4 * n:5 * n], refs[5 * n:6 * n], refs[6 * n:7 * n], refs[7 * n:]
        for t in range(n):
            gv = g[t][...]
            delta, m_new, v_new = _adamw_math(w[t][...], gv, m[t][...], v[t][...])
            d[t][...] = delta
            mo[t][...] = m_new
            vo[t][...] = v_new
            go[t][...] = gv

    specs = [pl.BlockSpec((a.shape[0] // row_blocks, a.shape[1]), lambda i: (i, 0)) for a in ws]
    shapes = [jax.ShapeDtypeStruct(a.shape, F32) for a in ws]
    return _pcall(
        body, name=name, grid=(row_blocks,), in_specs=specs * 4, out_specs=specs * 4, out_shape=shapes * 4,
        compiler_params=_params(1),
    )(*ws, *gs, *ms, *vs)


def kernel(x, ffn1_norm, ffn1_w_gate, ffn1_w_up, ffn1_w_down, mix_norm, w_in, conv_w, attn_sinks, w_out, ffn2_norm, ffn2_w_gate, ffn2_w_up, ffn2_w_down, final_norm, loss_target, m_ffn1_norm, m_ffn1_w_gate, m_ffn1_w_up, m_ffn1_w_down, m_mix_norm, m_w_in, m_conv_w, m_attn_sinks, m_w_out, m_ffn2_norm, m_ffn2_w_gate, m_ffn2_w_up, m_ffn2_w_down, m_final_norm, v_ffn1_norm, v_ffn1_w_gate, v_ffn1_w_up, v_ffn1_w_down, v_mix_norm, v_w_in, v_conv_w, v_attn_sinks, v_w_out, v_ffn2_norm, v_ffn2_w_gate, v_ffn2_w_up, v_ffn2_w_down, v_final_norm):
    T, D = x.shape[1], x.shape[2]
    chip = (2 * lax.axis_index("x") + lax.axis_index("y")).astype(jnp.int32)
    core = lax.axis_index("c").astype(jnp.int32)
    place = jnp.stack([chip, core])
    x0 = x[0]
    target = loss_target[0]
    gf = final_norm.reshape(1, D)

    tr = lambda w: jnp.swapaxes(w[0], 0, 1)
    big = [tr(ffn1_w_gate), tr(ffn1_w_up), ffn1_w_down[0], tr(w_in), w_out[0], tr(ffn2_w_gate), tr(ffn2_w_up), ffn2_w_down[0]]
    transposed = [True, True, False, True, False, True, True, False]
    own_b = [w.astype(BF16) for w in big]

    def whole(gathered, own):
        return lax.dynamic_update_slice(gathered, own[None], (chip, 0, 0)).reshape(-1, D)

    tab, got1 = _rope_tables(T, "rope_gather_ffn1", _gather_plan(own_b[0:3]))
    wg1, wu1, wd1 = (whole(g, o) for g, o in zip(got1, own_b[0:3]))

    res = _ffn_fwd(x0, ffn1_norm, wg1, wu1, wd1, "ffn1_fwd", _gather_plan(own_b[3:8], [conv_w[0]]))
    x1, h1, gate1, up1, act1 = res[:5]
    win, wout, wg2, wu2, wd2 = (whole(g, o) for g, o in zip(res[5:10], own_b[3:8]))
    convw4 = lax.dynamic_update_slice(res[10], conv_w, (chip, 0, 0))
    convw = jnp.transpose(convw4, (1, 0, 2)).reshape(3, -1)
    z, hm = _norm_matmul(x1, mix_norm, win, tab, "mix_in_fwd")
    ymix = _mix_core_fwd(z, convw, attn_sinks, "mix_core_fwd")
    dx3, h2, gate2, up2, act2, dgf, loss_part, x2 = _ffn_fwd(x1, ffn2_norm, wg2, wu2, wd2, "ffn2_fwd",
                                                             head=(gf, target), pre=(ymix, wout))

    dx2, dyb2, dgate2, dup2, dg2 = _ffn_bwd(dx3, x2, ffn2_norm, gate2, up2, wg2, wu2, wd2, "ffn2_bwd")
    dz, dcw, dsk, dx2b, dx1, dgm = _mix_core_bwd(z, dx2, wout, win, x1, mix_norm, tab, convw, attn_sinks, "mix_bwd")
    dx0, dyb1, dgate1, dup1, dg1 = _ffn_bwd(dx1, x0, ffn1_norm, gate1, up1, wg1, wu1, wd1, "ffn1_bwd")

    pad = lambda a: jnp.pad(a, ((0, 0), (0, LANES - a.shape[1])))
    vec = jnp.concatenate([dg1, dgm, dg2, dgf, dcw[0:3].reshape(1, -1), pad(dsk[:, 0].reshape(1, -1)),
                           pad(loss_part[:, 0:1])], axis=1)

    jobs = [("ffn2_dwg", dgate2, h2, 5), ("ffn2_dwu", dup2, h2, 6), ("ffn2_dwd", act2, dyb2, 7),
            ("ffn1_dwg", dgate1, h1, 0), ("ffn1_dwu", dup1, h1, 1), ("mix_dwin", dz, hm, 3),
            ("ffn1_dwd", act1, dyb1, 2), ("mix_dwout", ymix, dx2b, 4)]
    n_jobs = len(jobs)
    grad, grad_b, from_sib, pair_b, from_chips, half, g_big = ({} for _ in range(7))

    def stage_plans(t):
        plans, takers = [], []
        if 0 <= t - 1 < n_jobs:
            plans.append(_sibling_plan([grad_b[t - 1]]))
            takers.append((from_sib, t - 1))
        if 0 <= t - 2 < n_jobs:
            plans.append(_scatter_plan([pair_b[t - 2]]))
            takers.append((from_chips, t - 2))
        if 0 <= t - 3 < n_jobs:
            plans.append(_join_plan([half[t - 3]]))
            takers.append((g_big, jobs[t - 3][3]))
        return plans, takers

    def after_stage(t, landed, takers):
        for (store, key), arr in zip(takers, landed):
            store[key] = arr
        pair = (grad[t - 1], from_sib[t - 1]) if 0 <= t - 1 < n_jobs else None
        chip = (grad[t - 2], from_sib[t - 2], from_chips[t - 2]) if 0 <= t - 2 < n_jobs else None
        if pair or chip:
            sums = list(_local_sums(pair, chip, place, f"local_sums_{t}"))
            if pair:
                pair_b[t - 1] = sums.pop(0)
            if chip:
                half[t - 2] = sums.pop(0)

    for t, (name_, a, b, _) in enumerate(jobs):
        plans, takers = stage_plans(t)
        if t == 0:
            plans.append(_all_gather_plan(jnp.pad(vec, ((0, 7), (0, 0)))))
        res = _matmul_tn(a, b, DW_ROW_SPLIT, name_, _merge_plans(plans))
        grad[t], grad_b[t] = (r.reshape(N_CHIPS, -1, D) for r in res[:2])
        landed = list(res[2:])
        if t == 0:
            vec_blocks = landed.pop()
        after_stage(t, landed, takers)

    ws = big
    ms = [tr(m_ffn1_w_gate), tr(m_ffn1_w_up), m_ffn1_w_down[0], tr(m_w_in), m_w_out[0], tr(m_ffn2_w_gate), tr(m_ffn2_w_up), m_ffn2_w_down[0]]
    vs = [tr(v_ffn1_w_gate), tr(v_ffn1_w_up), v_ffn1_w_down[0], tr(v_w_in), v_w_out[0], tr(v_ffn2_w_gate), tr(v_ffn2_w_up), v_ffn2_w_down[0]]
    for t in range(n_jobs, n_jobs + 3):
        plans, takers = stage_plans(t)
        after_stage(t, _run_comm(_merge_plans(plans), f"grads_tail_{t - n_jobs}"), takers)
    upd = {}
    for name_, idx in (("adamw_a", [0, 1, 2, 4]), ("adamw_b", [3, 5, 6, 7])):
        k = len(idx)
        res = _adamw([ws[i] for i in idx], [g_big[i] for i in idx], [ms[i] for i in idx], [vs[i] for i in idx], ADAMW_ROW_BLOCKS, name_)
        for j, i in enumerate(idx):
            upd[i] = (res[j], res[k + j], res[2 * k + j])
            g_big[i] = res[3 * k + j]

    total = _sum_devices(vec_blocks, "small_sum")[0:1]
    g_n1, g_nm, g_n2, g_nf = (total[:, k * D:(k + 1) * D] for k in range(4))
    cw_full = total[:, 4 * D:4 * D + 3 * CONV_WIDTH].reshape(3, CONV_WIDTH)
    cq = CONV_WIDTH // N_CHIPS
    g_cw = lax.dynamic_slice(cw_full, (0, chip * cq), (3, cq))
    off = 4 * D + 3 * CONV_WIDTH
    g_sk = total[:, off:off + N_Q_HEADS]
    loss = total[0, off + LANES]

    sw = [ffn1_norm, mix_norm, conv_w[0], attn_sinks, ffn2_norm, gf]
    sg = [g_n1, g_nm, g_cw, g_sk, g_n2, g_nf]
    sm = [m_ffn1_norm, m_mix_norm, m_conv_w[0], m_attn_sinks, m_ffn2_norm, m_final_norm.reshape(1, D)]
    sv = [v_ffn1_norm, v_mix_norm, v_conv_w[0], v_attn_sinks, v_ffn2_norm, v_final_norm.reshape(1, D)]
    sres = _adamw(sw, sg, sm, sv, 1, "adamw_small")
    supd = [(sres[j], sres[6 + j], sres[12 + j]) for j in range(6)]

    order = [("s", 0), ("b", 0), ("b", 1), ("b", 2), ("s", 1), ("b", 3), ("s", 2), ("s", 3), ("b", 4),
             ("s", 4), ("b", 5), ("b", 6), ("b", 7), ("s", 5)]

    def leaf(kind, i, which):
        if kind == "b":
            a = g_big[i] if which == 0 else upd[i][which - 1]
            return (jnp.swapaxes(a, 0, 1) if transposed[i] else a)[None]
        a = sg[i] if which == 0 else supd[i][which - 1]
        if i == 2:
            return a[None]
        if i == 5:
            return a.reshape(D)
        return a

    outs = [loss, dx0[None]]
    for which in range(4):
        outs += [leaf(kind, i, which) for kind, i in order]
    return tuple(outs)
```

```python
import jax
import jax.numpy as jnp
import numpy as np
from jax import lax
from jax.experimental import pallas as pl
from jax.experimental.pallas import tpu as pltpu

F32 = jnp.float32
BF16 = jnp.bfloat16
MESH = pl.DeviceIdType.MESH

CONV_WIDTH = 512
N_Q_HEADS = 8
HEAD_DIM = 64
BLOCK = 128
ROPE_THETA = 500000.0
ROT_DIM = 16
RMS_EPS = 1e-5
MASK_VALUE = -1e30
ATTN_SCALE = HEAD_DIM ** -0.5
FFN_RES_SCALE = 0.5
ADAM_LR = 0.001
ADAM_B1 = 0.9
ADAM_B2 = 0.999
ADAM_EPS = 1e-08
ADAM_WD = 0.01
ADAM_STEP = 10

N_CHIPS = 4
N_DEV = 8
LANES = 128
VMEM_LIMIT = 56 * 1024 * 1024

_pcall = pl.pallas_call
HBM_SPEC = pl.BlockSpec(memory_space=pltpu.HBM)
ANY_SPEC = pl.BlockSpec(memory_space=pl.ANY)


def _params(n_axes, vmem=VMEM_LIMIT):
    return pltpu.CompilerParams(dimension_semantics=("arbitrary",) * n_axes, vmem_limit_bytes=vmem)


def _dot(a, b):
    return jnp.dot(a, b, preferred_element_type=F32)


def _dot_nt(a, b):
    return lax.dot_general(a, b, (((1,), (1,)), ((), ())), preferred_element_type=F32)


def _dot_tn(a, b):
    return lax.dot_general(a, b, (((0,), (0,)), ((), ())), preferred_element_type=F32)


def _rms_inv(x):
    return lax.rsqrt(jnp.mean(x * x, axis=-1, keepdims=True) + RMS_EPS)


def _norm_bwd(dh, x, g):
    inv = _rms_inv(x)
    xhat = x * inv
    dg = jnp.sum(dh * xhat, axis=0, keepdims=True)
    dxhat = dh * g
    dx = inv * (dxhat - xhat * jnp.mean(dxhat * xhat, axis=-1, keepdims=True))
    return dx, dg


def _place():
    x, y, c = lax.axis_index("x"), lax.axis_index("y"), lax.axis_index("c")
    chips = [(1 - x, y), (x, 1 - y), (1 - x, 1 - y)]
    return x, y, c, chips


class _Plan:
    def __init__(self, arrays, out_shapes, n_sems, start, finish, middle=None, aliases=None):
        self.arrays, self.out_shapes, self.n_sems = list(arrays), list(out_shapes), n_sems
        self.start, self.finish, self.middle = start, finish, middle
        self.aliases = dict(aliases or {})

    def specs(self):
        k = len(self.arrays)
        sems = [pltpu.SemaphoreType.DMA((self.n_sems,)), pltpu.SemaphoreType.DMA((self.n_sems,))]
        return [HBM_SPEC] * k, [HBM_SPEC] * len(self.out_shapes), self.out_shapes, sems


class _SemSlice:
    def __init__(self, ref, offset):
        self.ref, self.offset = ref, offset

    @property
    def at(self):
        return self

    def __getitem__(self, k):
        return self.ref.at[k + self.offset]


def _merge_plans(plans):
    plans = [p for p in plans if p is not None]
    if len(plans) <= 1:
        return plans[0] if plans else None
    arrays, shapes, aliases, spans, n_sems = [], [], {}, [], 0
    for p in plans:
        a0, o0 = len(arrays), len(shapes)
        spans.append((a0, a0 + len(p.arrays), o0, o0 + len(p.out_shapes), n_sems))
        aliases.update({a0 + i: o0 + j for i, j in p.aliases.items()})
        arrays += p.arrays
        shapes += p.out_shapes
        n_sems += p.n_sems

    def run(which):
        def fn(ins, outs, send_sems, recv_sems):
            for p, (a0, a1, o0, o1, s0) in zip(plans, spans):
                part = getattr(p, which)
                if part is not None:
                    part(ins[a0:a1], outs[o0:o1], _SemSlice(send_sems, s0), _SemSlice(recv_sems, s0))
        return fn

    middle = run("middle") if any(p.middle is not None for p in plans) else None
    return _Plan(arrays, shapes, n_sems, run("start"), run("finish"), middle, aliases)


def _sibling_plan(grads_b):
    n = len(grads_b)

    def copies(ins, outs, send_sems, recv_sems):
        x, y, c, _ = _place()

        def copy(t):
            half = ins[t].shape[1] // 2
            return pltpu.make_async_remote_copy(
                src_ref=ins[t].at[:, pl.ds(pl.multiple_of((1 - c) * half, 16), half), :], dst_ref=outs[t],
                send_sem=send_sems.at[t], recv_sem=recv_sems.at[t], device_id=(x, y, 1 - c), device_id_type=MESH)

        return [copy(t) for t in range(n)]

    def start(*refs):
        for cp in copies(*refs):
            cp.start()

    def finish(*refs):
        for cp in copies(*refs):
            cp.wait()

    shapes = [jax.ShapeDtypeStruct((g.shape[0], g.shape[1] // 2, g.shape[2]), g.dtype) for g in grads_b]
    return _Plan(grads_b, shapes, n, start, finish)


def _scatter_plan(parts_b):
    n = len(parts_b)

    def copies(ins, outs, send_sems, recv_sems):
        x, y, c, chips = _place()

        def copy(t, j):
            px, py = chips[j]
            return pltpu.make_async_remote_copy(
                src_ref=ins[t].at[2 * px + py], dst_ref=outs[t].at[j], send_sem=send_sems.at[3 * t + j],
                recv_sem=recv_sems.at[3 * t + j], device_id=(px, py, c), device_id_type=MESH)

        return [copy(t, j) for t in range(n) for j in range(3)]

    def start(*refs):
        for cp in copies(*refs):
            cp.start()

    def finish(*refs):
        for cp in copies(*refs):
            cp.wait()

    shapes = [jax.ShapeDtypeStruct((3, *p.shape[1:]), p.dtype) for p in parts_b]
    return _Plan(parts_b, shapes, 3 * n, start, finish)


def _gather_plan(shards, small=()):
    n, ns = len(shards), len(small)
    per = 9

    def parts(ins, outs, send_sems, recv_sems):
        x, y, c, chips = _place()
        me = 2 * x + y
        blocks = [2 * px + py for px, py in chips]

        def rows(t, core, piece=None):
            half = ins[t].shape[0] // 2
            if piece is None:
                return pl.ds(pl.multiple_of(core * half, 16), half)
            return pl.ds(pl.multiple_of(core * half + piece * (half // 2), 16), half // 2)

        def remote(src, dst, k, device):
            return pltpu.make_async_remote_copy(src_ref=src, dst_ref=dst, send_sem=send_sems.at[k],
                                                recv_sem=recv_sems.at[k], device_id=device, device_id_type=MESH)

        def first(t, j, block, core):
            return remote(ins[t].at[rows(t, core), :], outs[t].at[block, rows(t, core), :], per * t + j, (*chips[j], c))

        def relay(t, j, block, core):
            ref = outs[t].at[block, rows(t, core, j), :]
            return remote(ref, ref, per * t + 2 + j, (*chips[j], c))

        def passed(t, k, block, core, piece=None):
            ref = outs[t].at[block, rows(t, core, piece), :]
            return remote(ref, ref, per * t + 4 + k, (x, y, 1 - c))

        def own(t):
            return remote(ins[t], outs[t].at[me], per * t + 8, (x, y, 1 - c))

        def whole(s, j, block):
            return remote(ins[n + s], outs[n + s].at[block], per * n + 3 * s + j, (*chips[j], c))

        return c, me, blocks, first, relay, passed, whole, own

    def start(*refs):
        c, me, _, first, _, _, whole, own = parts(*refs)
        for t in range(n):
            for j in range(2):
                first(t, j, me, c).start()
            own(t).start()
        for s in range(ns):
            for j in range(3):
                whole(s, j, me).start()

    def middle(*refs):
        c, _, blocks, first, relay, passed, _, _ = parts(*refs)
        for t in range(n):
            for j in range(2):
                first(t, j, blocks[j], c).wait_recv()
                passed(t, j, blocks[j], c).start()
                relay(t, 1 - j, blocks[j], c).start()

    def finish(*refs):
        c, me, blocks, first, relay, passed, whole, own = parts(*refs)
        for t in range(n):
            for j in range(2):
                relay(t, j, blocks[2], c).wait_recv()
                passed(t, 2 + j, blocks[2], c, j).start()
        for t in range(n):
            for j in range(2):
                passed(t, j, blocks[j], 1 - c).wait_recv()
                passed(t, 2 + j, blocks[2], 1 - c, j).wait_recv()
            own(t).wait_recv()
        for s in range(ns):
            for j in range(3):
                whole(s, j, blocks[j]).wait_recv()
        for t in range(n):
            for j in range(2):
                first(t, j, me, c).wait_send()
                relay(t, 1 - j, blocks[j], c).wait_send()
                passed(t, j, blocks[j], c).wait_send()
                passed(t, 2 + j, blocks[2], c, j).wait_send()
            own(t).wait_send()
        for s in range(ns):
            for j in range(3):
                whole(s, j, me).wait_send()

    arrays = [*shards, *small]
    shapes = [jax.ShapeDtypeStruct((N_CHIPS, *a.shape), a.dtype) for a in arrays]
    return _Plan(arrays, shapes, per * n + 3 * ns, start, finish, middle)


def _run_comm(plan, name):
    k = len(plan.arrays)
    in_specs, out_specs, out_shape, sems = plan.specs()

    def body(*refs):
        cr = (refs[:k], refs[k:k + len(out_shape)], refs[-2], refs[-1])
        plan.start(*cr)
        if plan.middle is not None:
            plan.middle(*cr)
        plan.finish(*cr)

    return _pcall(body, name=name, in_specs=in_specs, out_specs=out_specs, out_shape=out_shape,
                  input_output_aliases=plan.aliases, scratch_shapes=sems)(*plan.arrays)


def _carried(plan, in_specs, out_specs, out_shape, scratch):
    aliases = {}
    if plan is not None:
        p_in, p_out, p_shape, p_sems = plan.specs()
        aliases = {len(in_specs) + i: len(out_specs) + j for i, j in plan.aliases.items()}
        in_specs, out_specs = in_specs + p_in, out_specs + p_out
        out_shape, scratch = out_shape + p_shape, scratch + p_sems
    return dict(in_specs=in_specs, out_specs=out_specs, out_shape=out_shape, scratch_shapes=scratch,
                input_output_aliases=aliases)


def _unpack(refs, n_in, n_out, plan):
    k_in = len(plan.arrays) if plan else 0
    k_out = len(plan.out_shapes) if plan else 0
    ins = refs[:n_in]
    outs = refs[n_in + k_in:n_in + k_in + n_out]
    rest = refs[n_in + k_in + n_out + k_out:]
    if plan is None:
        return ins, outs, rest, None
    cr = (refs[n_in:n_in + k_in], refs[n_in + k_in + n_out:n_in + k_in + n_out + k_out], rest[-2], rest[-1])
    return ins, outs, rest[:-2], cr


def _hook(plan, cr, which, cond):
    fn = getattr(plan, which) if plan is not None else None
    if fn is not None:
        pl.when(cond)(lambda: fn(*cr))


def _join_plan(shards):
    n = len(shards)

    def copy(ins, outs, send_sems, recv_sems, t, core):
        x, y, c, _ = _place()
        half = ins[t].shape[0] // 2
        rows = pl.ds(pl.multiple_of(core * half, 8), half)
        return pltpu.make_async_remote_copy(
            src_ref=ins[t].at[rows, :], dst_ref=outs[t].at[rows, :], send_sem=send_sems.at[t],
            recv_sem=recv_sems.at[t], device_id=(x, y, 1 - c), device_id_type=MESH)

    def start(*refs):
        c = lax.axis_index("c")
        for t in range(n):
            copy(*refs, t, c).start()

    def finish(*refs):
        c = lax.axis_index("c")
        for t in range(n):
            copy(*refs, t, 1 - c).wait_recv()
        for t in range(n):
            copy(*refs, t, c).wait_send()

    shapes = [jax.ShapeDtypeStruct(s.shape, s.dtype) for s in shards]
    return _Plan(shards, shapes, n, start, finish, aliases={t: t for t in range(n)})


def _all_gather_plan(vec):
    def parts(ins, outs, send_sems, recv_sems):
        x, y, c, _ = _place()
        me = 4 * x + 2 * y + c
        rel = [((k >> 2) & 1, (k >> 1) & 1, k & 1) for k in range(1, N_DEV)]

        def peer(k):
            fx, fy, fc = rel[k]
            return (x ^ fx, y ^ fy, c ^ fc)

        def copy(k, dev):
            return pltpu.make_async_remote_copy(
                src_ref=ins[0], dst_ref=outs[0].at[dev], send_sem=send_sems.at[k], recv_sem=recv_sems.at[k],
                device_id=peer(k), device_id_type=MESH)

        mine = pltpu.make_async_copy(ins[0], outs[0].at[me], send_sems.at[N_DEV - 1])
        return me, peer, copy, mine

    def start(*refs):
        me, _, copy, mine = parts(*refs)
        mine.start()
        for k in range(N_DEV - 1):
            copy(k, me).start()

    def finish(*refs):
        me, peer, copy, mine = parts(*refs)
        for k in range(N_DEV - 1):
            px, py, pc = peer(k)
            copy(k, 4 * px + 2 * py + pc).wait_recv()
        for k in range(N_DEV - 1):
            copy(k, me).wait_send()
        mine.wait()

    return _Plan([vec], [jax.ShapeDtypeStruct((N_DEV, *vec.shape), vec.dtype)], N_DEV, start, finish)


def _sum_devices(blocks, name):
    def body(b_ref, o_ref):
        total = b_ref[0]
        for dev in range(1, N_DEV):
            total = total + b_ref[dev]
        o_ref[...] = total

    return _pcall(body, name=name, in_specs=[pl.BlockSpec(memory_space=pltpu.VMEM)],
                  out_specs=pl.BlockSpec(memory_space=pltpu.VMEM),
                  out_shape=jax.ShapeDtypeStruct(blocks.shape[1:], F32))(blocks)


TOKEN_TILE = 512
PROJ_TOKEN_TILE = 1024
ADAMW_ROW_BLOCKS = 4
LARGE_VMEM_LIMIT = 62 * 1024 * 1024
DW_TOKEN_TILE = 2048
DW_ROW_SPLIT = 2
MXU_COLS = 256
DH_GROUP = 6


def _chunks(n):
    out, c0 = [], 0
    while c0 < n:
        size = min(MXU_COLS, n - c0)
        out.append((c0, size))
        c0 += size
    return out


def _load_weights(hbm_refs, vmem_refs, sems):
    copies = [pltpu.make_async_copy(h, v, sems.at[k]) for k, (h, v) in enumerate(zip(hbm_refs, vmem_refs))]
    for cp in copies:
        cp.start()
    for cp in copies:
        cp.wait()


def _ffn_fwd(x, g, wgt, wut, wd, name, plan=None, head=None, pre=None):
    T, D = x.shape
    F = wgt.shape[0]
    tm = min(T, TOKEN_TILE)
    ni = T // tm
    n_head = 2 if head is not None else 0
    n_pre = 1 if pre is not None else 0

    def body(*refs):
        ins, outs, scratch, cr = _unpack(refs, 5 + n_head + 2 * n_pre, 5 + n_head + n_pre, plan)
        x_ref, g_ref, wg_hbm, wu_hbm, wd_hbm = ins[:5]
        xo_ref, h_ref, gate_ref, up_ref, act_ref = outs[:5]
        wg_ref, wu_ref, wd_ref, sems = scratch
        i = pl.program_id(0)
        _hook(plan, cr, "start", i == 0)

        @pl.when(i == 0)
        def _():
            _load_weights((wg_hbm, wu_hbm, wd_hbm), (wg_ref, wu_ref, wd_ref), sems)

        if pre is not None:
            a_ref, w_ref = ins[5 + n_head:]
            x_ref = outs[5 + n_head]
            x_ref[...] = ins[0][...] + _dot(a_ref[...], w_ref[...])
        xv = x_ref[...]
        h = ((xv * _rms_inv(xv)) * g_ref[...]).astype(BF16)
        h_ref[...] = h
        for c0, size in _chunks(F):
            gate = _dot_nt(h, wg_ref[c0:c0 + size, :])
            up = _dot_nt(h, wu_ref[c0:c0 + size, :])
            gate_ref[:, c0:c0 + size] = gate.astype(BF16)
            up_ref[:, c0:c0 + size] = up.astype(BF16)
            act_ref[:, c0:c0 + size] = (gate * jax.nn.sigmoid(gate) * up).astype(BF16)
        y = x_ref[...] + FFN_RES_SCALE * _dot(act_ref[...], wd_ref[...])
        if head is None:
            xo_ref[...] = y
        else:
            gf_ref, t_ref = ins[5:7]
            dgf_ref, loss_ref = outs[5:7]

            @pl.when(i == 0)
            def _():
                dgf_ref[...] = jnp.zeros_like(dgf_ref)
                loss_ref[...] = jnp.zeros_like(loss_ref)

            gf = gf_ref[...]
            diff = (y * _rms_inv(y)) * gf - t_ref[...]
            loss_ref[...] += 0.5 * jnp.sum(jnp.mean(diff * diff, axis=-1, keepdims=True))
            dy, dgf = _norm_bwd(diff * (1.0 / D), y, gf)
            xo_ref[...] = dy
            dgf_ref[...] += dgf
        _hook(plan, cr, "middle", i == ni // 2)
        _hook(plan, cr, "finish", i == ni - 1)

    const = lambda shape: pl.BlockSpec(shape, lambda i: (0, 0))
    rows = lambda width: pl.BlockSpec((tm, width), lambda i: (i, 0))
    in_specs = [rows(D), const((1, D)), ANY_SPEC, ANY_SPEC, ANY_SPEC]
    out_specs = [rows(D), rows(D), rows(F), rows(F), rows(F)]
    out_shape = [jax.ShapeDtypeStruct((T, D), F32), jax.ShapeDtypeStruct((T, D), BF16),
                 jax.ShapeDtypeStruct((T, F), BF16), jax.ShapeDtypeStruct((T, F), BF16), jax.ShapeDtypeStruct((T, F), BF16)]
    if head is not None:
        in_specs += [const((1, D)), rows(D)]
        out_specs += [const((1, D)), const((1, LANES))]
        out_shape += [jax.ShapeDtypeStruct((1, D), F32), jax.ShapeDtypeStruct((1, LANES), F32)]
    if pre is not None:
        in_specs += [rows(pre[0].shape[1]), const(pre[1].shape)]
        out_specs += [rows(D)]
        out_shape += [jax.ShapeDtypeStruct((T, D), F32)]
    io = _carried(plan, in_specs, out_specs, out_shape,
                  [pltpu.VMEM((F, D), BF16), pltpu.VMEM((F, D), BF16), pltpu.VMEM((F, D), BF16),
                   pltpu.SemaphoreType.DMA((3,))])
    return _pcall(
        body, name=name, grid=(ni,), compiler_params=_params(1, VMEM_LIMIT if pre is None else LARGE_VMEM_LIMIT), **io,
    )(x, g, wgt, wut, wd, *(head or ()), *(pre or ()), *(plan.arrays if plan else ()))


def _ffn_bwd(dy, x, g, gate, up, wgt, wut, wd, name):
    T, D = x.shape
    F = wgt.shape[0]
    tm = min(T, TOKEN_TILE)
    ni = T // tm

    def body(dy_ref, x_ref, g_ref, gate_ref, up_ref, wg_hbm, wu_hbm, wd_hbm,
             dx_ref, dyb_ref, dgate_ref, dup_ref, dg_ref, wg_ref, wu_ref, wd_ref, sems):
        @pl.when(pl.program_id(0) == 0)
        def _():
            _load_weights((wg_hbm, wu_hbm, wd_hbm), (wg_ref, wu_ref, wd_ref), sems)
            dg_ref[...] = jnp.zeros_like(dg_ref)

        dyb = (FFN_RES_SCALE * dy_ref[...]).astype(BF16)
        dyb_ref[...] = dyb
        dh, group_g, group_u, row0 = None, [], [], 0
        chunks = _chunks(F)
        for k, (c0, size) in enumerate(chunks):
            dact = _dot_nt(dyb, wd_ref[c0:c0 + size, :])
            gt = gate_ref[:, c0:c0 + size].astype(F32)
            u = up_ref[:, c0:c0 + size].astype(F32)
            sig = jax.nn.sigmoid(gt)
            dup = (dact * (gt * sig)).astype(BF16)
            dgate = (dact * u * (sig * (1.0 + gt * (1.0 - sig)))).astype(BF16)
            dup_ref[:, c0:c0 + size] = dup
            dgate_ref[:, c0:c0 + size] = dgate
            group_g.append(dgate)
            group_u.append(dup)
            if len(group_g) == DH_GROUP or k == len(chunks) - 1:
                rows = slice(row0, c0 + size)
                part = (_dot(jnp.concatenate(group_g, axis=1), wg_ref[rows, :])
                        + _dot(jnp.concatenate(group_u, axis=1), wu_ref[rows, :]))
                dh = part if dh is None else dh + part
                group_g, group_u, row0 = [], [], c0 + size
        dxn, dg = _norm_bwd(dh, x_ref[...], g_ref[...])
        dx_ref[...] = dy_ref[...] + dxn
        dg_ref[...] += dg

    return _pcall(
        body, name=name, grid=(ni,),
        in_specs=[pl.BlockSpec((tm, D), lambda i: (i, 0)), pl.BlockSpec((tm, D), lambda i: (i, 0)),
                  pl.BlockSpec((1, D), lambda i: (0, 0)),
                  pl.BlockSpec((tm, F), lambda i: (i, 0)), pl.BlockSpec((tm, F), lambda i: (i, 0)),
                  ANY_SPEC, ANY_SPEC, ANY_SPEC],
        out_specs=[pl.BlockSpec((tm, D), lambda i: (i, 0)), pl.BlockSpec((tm, D), lambda i: (i, 0)),
                   pl.BlockSpec((tm, F), lambda i: (i, 0)), pl.BlockSpec((tm, F), lambda i: (i, 0)),
                   pl.BlockSpec((1, D), lambda i: (0, 0))],
        out_shape=[jax.ShapeDtypeStruct((T, D), F32), jax.ShapeDtypeStruct((T, D), BF16),
                   jax.ShapeDtypeStruct((T, F), BF16), jax.ShapeDtypeStruct((T, F), BF16),
                   jax.ShapeDtypeStruct((1, D), F32)],
        scratch_shapes=[pltpu.VMEM((F, D), BF16), pltpu.VMEM((F, D), BF16), pltpu.VMEM((F, D), BF16),
                        pltpu.SemaphoreType.DMA((3,))],
        compiler_params=_params(1, LARGE_VMEM_LIMIT),
    )(dy, x, g, gate, up, wgt, wut, wd)


def _matmul_tn(a, b, row_split, name, plan=None):
    T, n1 = a.shape
    n2 = b.shape[1]
    tn = n1 // row_split
    tk = min(T, DW_TOKEN_TILE)
    nk = T // tk

    def body(*refs):
        (a_ref, b_ref), (o_ref, ob_ref), _, cr = _unpack(refs, 2, 2, plan)
        j = pl.program_id(0)
        k = pl.program_id(1)
        _hook(plan, cr, "start", jnp.logical_and(j == 0, k == 0))

        @pl.when(k == 0)
        def _():
            o_ref[...] = jnp.zeros_like(o_ref)

        o_ref[...] += _dot_tn(a_ref[...], b_ref[...])

        @pl.when(k == nk - 1)
        def _():
            ob_ref[...] = o_ref[...].astype(BF16)

        _hook(plan, cr, "finish", jnp.logical_and(j == row_split - 1, k == nk - 1))

    io = _carried(
        plan,
        [pl.BlockSpec((tk, tn), lambda j, k: (k, j)), pl.BlockSpec((tk, n2), lambda j, k: (k, 0))],
        [pl.BlockSpec((tn, n2), lambda j, k: (j, 0)), pl.BlockSpec((tn, n2), lambda j, k: (j, 0))],
        [jax.ShapeDtypeStruct((n1, n2), F32), jax.ShapeDtypeStruct((n1, n2), BF16)], [])
    return _pcall(
        body, name=name, grid=(row_split, nk), compiler_params=_params(2), **io,
    )(a, b, *(plan.arrays if plan else ()))


def _norm_matmul(x, g, wt, tab, name):
    T, D = x.shape
    n = wt.shape[0]
    tm = min(T, PROJ_TOKEN_TILE)

    def body(x_ref, g_ref, w_ref, tab_ref, z_ref, h_ref):
        xv = x_ref[...]
        h = ((xv * _rms_inv(xv)) * g_ref[...]).astype(BF16)
        h_ref[...] = h
        z = _dot_nt(h, w_ref[...])
        z_ref[:, 0:Z_Q] = z[:, 0:Z_Q]
        tab_v = tab_ref[...]
        for c0 in range(Z_Q, Z_V, LANES):
            z_ref[:, c0:c0 + LANES] = _rot(z[:, c0:c0 + LANES], tab_v)
        z_ref[:, Z_V:Z_END] = z[:, Z_V:Z_END]

    return _pcall(
        body, name=name, grid=(T // tm,),
        in_specs=[pl.BlockSpec((tm, D), lambda i: (i, 0)), pl.BlockSpec((1, D), lambda i: (0, 0)),
                  pl.BlockSpec((n, D), lambda i: (0, 0)), pl.BlockSpec((tm, 3 * LANES), lambda i: (i, 0))],
        out_specs=[pl.BlockSpec((tm, n), lambda i: (i, 0)), pl.BlockSpec((tm, D), lambda i: (i, 0))],
        out_shape=[jax.ShapeDtypeStruct((T, n), F32), jax.ShapeDtypeStruct((T, D), BF16)],
        compiler_params=_params(1),
    )(x, g, wt, tab)


Z_Q = 3 * CONV_WIDTH
Z_K = Z_Q + N_Q_HEADS * HEAD_DIM
Z_V = Z_K + LANES
Z_END = Z_V + LANES


def _rope_tables(T, name, plan):
    half = ROT_DIM // 2
    inv_freq = ROPE_THETA ** (-jnp.arange(0, ROT_DIM, 2, dtype=F32) / ROT_DIM)
    ang = inv_freq[:, None] * jnp.arange(T, dtype=F32)[None, :]
    cos_sin = jnp.concatenate([jnp.cos(ang), jnp.sin(ang)], axis=0)
    select = np.zeros((2 * half, 3 * LANES), np.float32)
    const = np.zeros((1, 3 * LANES), np.float32)
    for lane in range(LANES):
        d = lane % HEAD_DIM
        if d < half:
            select[d, lane] = 1.0
            select[half + d, LANES + lane] = -1.0
        elif d < ROT_DIM:
            select[d - half, lane] = 1.0
            select[d, 2 * LANES + lane] = 1.0
        else:
            const[0, lane] = 1.0
    tm = PROJ_TOKEN_TILE
    ni = T // tm

    def body(*refs):
        (cs_ref, sel_ref, const_ref), (tab_ref,), _, cr = _unpack(refs, 3, 1, plan)
        i = pl.program_id(0)
        _hook(plan, cr, "start", i == 0)
        tab_ref[...] = lax.dot_general(cs_ref[...], sel_ref[...], (((0,), (0,)), ((), ())),
                                       precision=lax.Precision.HIGHEST, preferred_element_type=F32) + const_ref[...]
        _hook(plan, cr, "middle", i == ni - 1)
        _hook(plan, cr, "finish", i == ni - 1)

    io = _carried(
        plan,
        [pl.BlockSpec((2 * half, tm), lambda i: (0, i)), pl.BlockSpec((2 * half, 3 * LANES), lambda i: (0, 0)),
         pl.BlockSpec((1, 3 * LANES), lambda i: (0, 0))],
        [pl.BlockSpec((tm, 3 * LANES), lambda i: (i, 0))], [jax.ShapeDtypeStruct((T, 3 * LANES), F32)], [])
    res = _pcall(body, name=name, grid=(ni,), compiler_params=_params(1), **io)(
        cos_sin, jnp.asarray(select), jnp.asarray(const), *plan.arrays)
    return res[0], res[1:]


def _tab3(tab):
    return tab[:, 0:LANES], tab[:, LANES:2 * LANES], tab[:, 2 * LANES:3 * LANES]


def _rot(x, tab):
    c, s1, s2 = _tab3(tab)
    return x * c + pltpu.roll(x, LANES - ROT_DIM // 2, 1) * s1 + pltpu.roll(x, ROT_DIM // 2, 1) * s2


def _rot_t(d, tab):
    c, s1, s2 = _tab3(tab)
    return d * c + pltpu.roll(d * s1, ROT_DIM // 2, 1) + pltpu.roll(d * s2, LANES - ROT_DIM // 2, 1)


def _head_pads(a):
    lo = lax.broadcasted_iota(jnp.int32, a.shape, 1) < HEAD_DIM
    nat0 = jnp.where(lo, a, 0.0)
    nat1 = jnp.where(lo, 0.0, a)
    return {
        (0, 0): nat0.astype(BF16), (0, 1): pltpu.roll(nat0, HEAD_DIM, 1).astype(BF16),
        (1, 0): pltpu.roll(nat1, HEAD_DIM, 1).astype(BF16), (1, 1): nat1.astype(BF16),
    }


def _from_pads(even, odd, kv):
    lo = lax.broadcasted_iota(jnp.int32, even.shape, 1) < HEAD_DIM
    if kv == 0:
        return jnp.where(lo, even + pltpu.roll(odd, HEAD_DIM, 1), 0.0)
    return jnp.where(lo, 0.0, pltpu.roll(even, HEAD_DIM, 1) + odd)


N_GROUPS = 4


def _group_head(g, r):
    kv, par = divmod(g, 2)
    return 2 * (2 * kv + r) + par


def _window_mask_t(has_prev):
    jj = lax.broadcasted_iota(jnp.int32, (2 * BLOCK, 2 * BLOCK), 0)
    ii = lax.broadcasted_iota(jnp.int32, (2 * BLOCK, 2 * BLOCK), 1) & (BLOCK - 1)
    rel = jj - BLOCK - ii
    return (rel <= 0) & (rel > -BLOCK) & ((jj >= BLOCK) | has_prev)


def _sink_row(sink_ref, g):
    lane = lax.broadcasted_iota(jnp.int32, (1, 2 * BLOCK), 1)
    return jnp.where(lane < BLOCK, sink_ref[0, _group_head(g, 0)], sink_ref[0, _group_head(g, 1)])


def _attn_probs_t(q2, kp, mask, sink_ref):
    out = []
    for kv in range(2):
        q_st = jnp.concatenate([q2[2 * kv], q2[2 * kv + 1]], axis=0)
        for par in range(2):
            s = jnp.where(mask, _dot_nt(kp[(kv, par)], q_st), MASK_VALUE)
            sink = _sink_row(sink_ref, 2 * kv + par)
            m = jnp.maximum(jnp.max(s, axis=0, keepdims=True), sink)
            p = jnp.exp(s - m)
            esink = jnp.exp(sink - m)
            rden = 1.0 / (jnp.sum(p, axis=0, keepdims=True) + esink)
            out.append((p * rden, esink * rden))
    return out


def _conv_taps(cg, u, cg_prev, u_prev, has_prev):
    vv = cg * u
    halo = jnp.where(has_prev, cg_prev * u_prev, 0.0)
    ext = jnp.concatenate([halo, vv], axis=0)
    rows = ext.shape[0]
    vv1 = pltpu.roll(ext, 1, 0)[8:rows]
    vv2 = pltpu.roll(ext, 2, 0)[8:rows]
    return vv, vv1, vv2


MIX_BLOCKS = 4


def _mix_core_fwd(z, conv_w, sinks, name):
    T = z.shape[0]
    rows = MIX_BLOCKS * BLOCK
    steps = T // rows
    prev_block = lambda n: jnp.maximum(MIX_BLOCKS * n - 1, 0)
    prev_rows8 = lambda n: jnp.maximum((rows // 8) * n - 1, 0)

    def body(z_ref, zkvp_ref, cgp_ref, up_ref, cw_ref, sink_ref, y_ref):
        for b in range(MIX_BLOCKS):
            r0 = b * BLOCK
            blk = slice(r0, r0 + BLOCK)
            if b == 0:
                has_prev = pl.program_id(0) > 0
                kv_prev, cg_prev, u_prev = zkvp_ref[...], cgp_ref[...], up_ref[...]
            else:
                has_prev = True
                kv_prev = z_ref[r0 - BLOCK:r0, Z_K:Z_END]
                cg_prev = z_ref[r0 - 8:r0, CONV_WIDTH:2 * CONV_WIDTH]
                u_prev = z_ref[r0 - 8:r0, 2 * CONV_WIDTH:Z_Q]
            bg = z_ref[blk, 0:CONV_WIDTH]
            vv, vv1, vv2 = _conv_taps(z_ref[blk, CONV_WIDTH:2 * CONV_WIDTH], z_ref[blk, 2 * CONV_WIDTH:Z_Q],
                                      cg_prev, u_prev, has_prev)
            conv = cw_ref[0:1, :] * vv2 + cw_ref[1:2, :] * vv1 + cw_ref[2:3, :] * vv
            y_ref[blk, 0:CONV_WIDTH] = (bg * conv).astype(BF16)

            k_all = jnp.concatenate([kv_prev[:, 0:LANES], z_ref[blk, Z_K:Z_V]], axis=0)
            v_all = jnp.concatenate([kv_prev[:, LANES:2 * LANES], z_ref[blk, Z_V:Z_END]], axis=0)
            kp = _head_pads(k_all)
            vp = _head_pads(v_all)
            q2 = [(z_ref[blk, Z_Q + LANES * c:Z_Q + LANES * (c + 1)] * ATTN_SCALE).astype(BF16)
                  for c in range(N_Q_HEADS // 2)]
            probs = _attn_probs_t(q2, kp, _window_mask_t(has_prev), sink_ref)
            for kv in range(2):
                o_t = (_dot_tn(vp[(kv, 0)], probs[2 * kv][0].astype(BF16))
                       + _dot_tn(vp[(kv, 1)], probs[2 * kv + 1][0].astype(BF16)))
                for r in range(2):
                    c = 2 * kv + r
                    y_ref[blk, CONV_WIDTH + LANES * c:CONV_WIDTH + LANES * (c + 1)] = (
                        o_t[:, BLOCK * r:BLOCK * (r + 1)].T.astype(BF16))

    return _pcall(
        body, name=name, grid=(steps,),
        in_specs=[pl.BlockSpec((rows, Z_END), lambda n: (n, 0)),
                  pl.BlockSpec((BLOCK, 2 * LANES), lambda n: (prev_block(n), Z_K // (2 * LANES))),
                  pl.BlockSpec((8, CONV_WIDTH), lambda n: (prev_rows8(n), 1)),
                  pl.BlockSpec((8, CONV_WIDTH), lambda n: (prev_rows8(n), 2)),
                  pl.BlockSpec((3, CONV_WIDTH), lambda n: (0, 0)),
                  pl.BlockSpec(memory_space=pltpu.SMEM)],
        out_specs=pl.BlockSpec((rows, 2 * CONV_WIDTH), lambda n: (n, 0)),
        out_shape=jax.ShapeDtypeStruct((T, 2 * CONV_WIDTH), BF16),
        compiler_params=_params(1),
    )(z, z, z, z, conv_w, sinks)


def _mix_core_bwd(z, dx, wout, win, x, g, tab, conv_w, sinks, name):
    T = z.shape[0]
    D = dx.shape[1]
    nsub = MIX_BLOCKS
    rows = nsub * BLOCK
    steps = T // rows
    last = slice(rows - BLOCK, rows)
    cur = lambda n: jnp.minimum(n, steps - 1)
    prev_block = lambda n: jnp.maximum(nsub * cur(n) - 1, 0)
    prev_rows8 = lambda n: jnp.maximum((rows // 8) * cur(n) - 1, 0)
    next_rows8 = lambda n: jnp.minimum((rows // 8) * (cur(n) + 1), T // 8 - 1)

    def body(z_ref, zkvp_ref, cgp_ref, up_ref, bgn_ref, dx_ref, dxn_ref, wo_ref, tab_ref, tabp_ref, cw_ref, sink_ref,
             wi_ref, xe_ref, g_ref, dxe_ref,
             dz_ref, dcw_ref, dsk_ref, dxb_ref, dxo_ref, dg_ref, held_ref, kv_ref):
        n = pl.program_id(0)

        @pl.when(n == 0)
        def _():
            held_ref[...] = jnp.zeros_like(held_ref)
            kv_ref[...] = jnp.zeros_like(kv_ref)
            dcw_ref[...] = jnp.zeros_like(dcw_ref)
            dsk_ref[...] = jnp.zeros_like(dsk_ref)
            dg_ref[...] = jnp.zeros_like(dg_ref)

        def emit_held():
            dz_ref[:, 0:Z_K] = held_ref[:, 0:Z_K]
            if nsub > 1:
                dz_ref[0:rows - BLOCK, Z_K:Z_END] = held_ref[0:rows - BLOCK, Z_K:Z_END]

        def project_emitted():
            dxn, dg = _norm_bwd(_dot(dz_ref[...], wi_ref[...]), xe_ref[...], g_ref[...])
            dxo_ref[...] = dxe_ref[...] + dxn
            dg_ref[...] += dg

        @pl.when(n < steps)
        def _():
            emit_held()
            dxb = dx_ref[...].astype(BF16)
            dxb_ref[...] = dxb
            dy = _dot_nt(dxb, wo_ref[...])
            dy_next = _dot_nt(dxn_ref[...].astype(BF16), wo_ref[0:CONV_WIDTH, :])
            w0, w1, w2 = cw_ref[0:1, :], cw_ref[1:2, :], cw_ref[2:3, :]
            dk_open, dv_open = kv_ref[:, 0:LANES], kv_ref[:, LANES:2 * LANES]
            for b in range(nsub):
                r0 = b * BLOCK
                blk = slice(r0, r0 + BLOCK)
                before = slice(r0 - BLOCK, r0)
                after8 = slice(r0 + BLOCK, r0 + BLOCK + 8)
                if b == 0:
                    has_prev = n > 0
                    kv_prev, cg_prev, u_prev, tab_p = zkvp_ref[...], cgp_ref[...], up_ref[...], tabp_ref[...]
                else:
                    has_prev = True
                    kv_prev, tab_p = z_ref[before, Z_K:Z_END], tab_ref[before, :]
                    cg_prev = z_ref[r0 - 8:r0, CONV_WIDTH:2 * CONV_WIDTH]
                    u_prev = z_ref[r0 - 8:r0, 2 * CONV_WIDTH:Z_Q]
                if b == nsub - 1:
                    dconv_next = jnp.where(n < steps - 1, dy_next * bgn_ref[...], 0.0)
                else:
                    dconv_next = dy[after8, 0:CONV_WIDTH] * z_ref[after8, 0:CONV_WIDTH]
                bg = z_ref[blk, 0:CONV_WIDTH]
                cg = z_ref[blk, CONV_WIDTH:2 * CONV_WIDTH]
                u = z_ref[blk, 2 * CONV_WIDTH:Z_Q]
                vv, vv1, vv2 = _conv_taps(cg, u, cg_prev, u_prev, has_prev)
                dyc = dy[blk, 0:CONV_WIDTH]
                dbg = dyc * (w0 * vv2 + w1 * vv1 + w2 * vv)
                dconv = dyc * bg
                ext = jnp.concatenate([dconv, dconv_next], axis=0)
                ext_rows = ext.shape[0]
                dvv = (w2 * dconv + w1 * pltpu.roll(ext, ext_rows - 1, 0)[0:BLOCK]
                       + w0 * pltpu.roll(ext, ext_rows - 2, 0)[0:BLOCK])
                dcw_ref[0:1, :] += jnp.sum(dconv * vv2, axis=0, keepdims=True)
                dcw_ref[1:2, :] += jnp.sum(dconv * vv1, axis=0, keepdims=True)
                dcw_ref[2:3, :] += jnp.sum(dconv * vv, axis=0, keepdims=True)

                tab_c = tab_ref[blk, :]
                k_all = jnp.concatenate([kv_prev[:, 0:LANES], z_ref[blk, Z_K:Z_V]], axis=0)
                v_all = jnp.concatenate([kv_prev[:, LANES:2 * LANES], z_ref[blk, Z_V:Z_END]], axis=0)
                kp = _head_pads(k_all)
                vp = _head_pads(v_all)
                chunks = range(N_Q_HEADS // 2)
                q2 = [(z_ref[blk, Z_Q + LANES * c:Z_Q + LANES * (c + 1)] * ATTN_SCALE).astype(BF16) for c in chunks]
                do2 = [dy[blk, CONV_WIDTH + LANES * c:CONV_WIDTH + LANES * (c + 1)].astype(BF16) for c in chunks]
                probs = _attn_probs_t(q2, kp, _window_mask_t(has_prev), sink_ref)
                dq_chunks = []
                dk_nat = jnp.zeros((2 * BLOCK, LANES), F32)
                dv_nat = jnp.zeros((2 * BLOCK, LANES), F32)
                for kv in range(2):
                    q_st = jnp.concatenate([q2[2 * kv], q2[2 * kv + 1]], axis=0)
                    do_st = jnp.concatenate([do2[2 * kv], do2[2 * kv + 1]], axis=0)
                    dq_t = jnp.zeros((LANES, 2 * BLOCK), F32)
                    dk_par, dv_par = [], []
                    for par in range(2):
                        g = 2 * kv + par
                        pr, psink = probs[g]
                        dp = _dot_nt(vp[(kv, par)], do_st)
                        delta = jnp.sum(dp * pr, axis=0, keepdims=True)
                        ds = (pr * (dp - delta)).astype(BF16)
                        dsink = -psink * delta
                        for r in range(2):
                            h = _group_head(g, r)
                            dsk_ref[h:h + 1, :] += jnp.sum(dsink[:, BLOCK * r:BLOCK * (r + 1)])
                        dq_t = dq_t + _dot_tn(kp[(kv, par)], ds)
                        dk_par.append(_dot(ds, q_st))
                        dv_par.append(_dot(pr.astype(BF16), do_st))
                    for r in range(2):
                        dq_chunks.append(_rot_t(dq_t[:, BLOCK * r:BLOCK * (r + 1)].T * ATTN_SCALE, tab_c))
                    dk_nat = dk_nat + _from_pads(dk_par[0], dk_par[1], kv)
                    dv_nat = dv_nat + _from_pads(dv_par[0], dv_par[1], kv)

                done_ref, done = (dz_ref, last) if b == 0 else (held_ref, before)
                done_ref[done, Z_K:Z_V] = _rot_t(dk_open + dk_nat[0:BLOCK], tab_p).astype(BF16)
                done_ref[done, Z_V:Z_END] = (dv_open + dv_nat[0:BLOCK]).astype(BF16)
                dk_open, dv_open = dk_nat[BLOCK:2 * BLOCK], dv_nat[BLOCK:2 * BLOCK]
                held_ref[blk, 0:CONV_WIDTH] = dbg.astype(BF16)
                held_ref[blk, CONV_WIDTH:2 * CONV_WIDTH] = (dvv * u).astype(BF16)
                held_ref[blk, 2 * CONV_WIDTH:Z_Q] = (dvv * cg).astype(BF16)
                for c in range(N_Q_HEADS // 2):
                    held_ref[blk, Z_Q + LANES * c:Z_Q + LANES * (c + 1)] = dq_chunks[c].astype(BF16)
            kv_ref[:, 0:LANES] = dk_open
            kv_ref[:, LANES:2 * LANES] = dv_open
            project_emitted()

        @pl.when(n == steps)
        def _():
            emit_held()
            dz_ref[last, Z_K:Z_V] = _rot_t(kv_ref[:, 0:LANES], tab_ref[last, :]).astype(BF16)
            dz_ref[last, Z_V:Z_END] = kv_ref[:, LANES:2 * LANES].astype(BF16)
            project_emitted()

    emitted = lambda n: (jnp.maximum(n - 1, 0), 0)
    return _pcall(
        body, name=name, grid=(steps + 1,),
        in_specs=[pl.BlockSpec((rows, Z_END), lambda n: (cur(n), 0)),
                  pl.BlockSpec((BLOCK, 2 * LANES), lambda n: (prev_block(n), Z_K // (2 * LANES))),
                  pl.BlockSpec((8, CONV_WIDTH), lambda n: (prev_rows8(n), 1)),
                  pl.BlockSpec((8, CONV_WIDTH), lambda n: (prev_rows8(n), 2)),
                  pl.BlockSpec((8, CONV_WIDTH), lambda n: (next_rows8(n), 0)),
                  pl.BlockSpec((rows, dx.shape[1]), lambda n: (cur(n), 0)),
                  pl.BlockSpec((8, dx.shape[1]), lambda n: (next_rows8(n), 0)),
                  pl.BlockSpec(wout.shape, lambda n: (0, 0)),
                  pl.BlockSpec((rows, 3 * LANES), lambda n: (cur(n), 0)),
                  pl.BlockSpec((BLOCK, 3 * LANES), lambda n: (prev_block(n), 0)),
                  pl.BlockSpec((3, CONV_WIDTH), lambda n: (0, 0)),
                  pl.BlockSpec(memory_space=pltpu.SMEM),
                  pl.BlockSpec(win.shape, lambda n: (0, 0)), pl.BlockSpec((rows, D), emitted),
                  pl.BlockSpec((1, D), lambda n: (0, 0)), pl.BlockSpec((rows, D), emitted)],
        out_specs=[pl.BlockSpec((rows, Z_END), emitted),
                   pl.BlockSpec((8, CONV_WIDTH), lambda n: (0, 0)), pl.BlockSpec((8, LANES), lambda n: (0, 0)),
                   pl.BlockSpec((rows, D), lambda n: (cur(n), 0)),
                   pl.BlockSpec((rows, D), emitted), pl.BlockSpec((1, D), lambda n: (0, 0))],
        out_shape=[jax.ShapeDtypeStruct((T, Z_END), BF16), jax.ShapeDtypeStruct((8, CONV_WIDTH), F32),
                   jax.ShapeDtypeStruct((8, LANES), F32), jax.ShapeDtypeStruct(dx.shape, BF16),
                   jax.ShapeDtypeStruct((T, D), F32), jax.ShapeDtypeStruct((1, D), F32)],
        scratch_shapes=[pltpu.VMEM((rows, Z_END), BF16), pltpu.VMEM((BLOCK, 2 * LANES), F32)],
        compiler_params=_params(1, LARGE_VMEM_LIMIT),
    )(z, z, z, z, z, dx, dx, wout, tab, tab, conv_w, sinks, win, x, g, dx)


def _local_sums(pair, chip, place, name):
    arrays, in_specs, out_specs, out_shape = [], [], [], []
    if pair is not None:
        g, sib = pair
        blk = (1, *sib.shape[1:])
        arrays += [g, sib]
        in_specs += [pl.BlockSpec(blk, lambda q, p: (q, p[1], 0)), pl.BlockSpec(blk, lambda q, p: (q, 0, 0))]
        out_specs.append(pl.BlockSpec(blk, lambda q, p: (q, 0, 0)))
        out_shape.append(jax.ShapeDtypeStruct(sib.shape, BF16))
    if chip is not None:
        g2, sib2, recv2 = chip
        blk = (1, *sib2.shape[1:])
        arrays += [g2, sib2, recv2]
        in_specs += [pl.BlockSpec(blk, lambda q, p: (p[0], p[1], 0)), pl.BlockSpec(blk, lambda q, p: (p[0], 0, 0)),
                     pl.BlockSpec(recv2.shape, lambda q, p: (0, 0, 0))]
        out_specs.append(pl.BlockSpec(sib2.shape[1:], lambda q, p: (p[1], 0)))
        out_shape.append(jax.ShapeDtypeStruct(g2.shape[1:], F32))

    def body(place_ref, *refs):
        refs = list(refs)
        ins, outs = refs[:len(arrays)], refs[len(arrays):]
        if pair is not None:
            g_ref, sib_ref = ins[:2]
            outs[0][...] = (g_ref[...] + sib_ref[...].astype(F32)).astype(BF16)
        if chip is not None:
            g_ref, sib_ref, recv_ref = ins[-3:]

            @pl.when(pl.program_id(0) == 0)
            def _():
                total = g_ref[0] + sib_ref[0].astype(F32)
                for j in range(3):
                    total = total + recv_ref[j].astype(F32)
                outs[-1][...] = total

    return _pcall(
        body, name=name,
        grid_spec=pltpu.PrefetchScalarGridSpec(num_scalar_prefetch=1, grid=(N_CHIPS,),
                                               in_specs=in_specs, out_specs=out_specs),
        out_shape=out_shape, compiler_params=_params(1),
    )(place, *arrays)


def _adamw_math(w, g, m, v):
    m = ADAM_B1 * m + (1.0 - ADAM_B1) * g
    v = ADAM_B2 * v + (1.0 - ADAM_B2) * (g * g)
    m_hat = m / (1.0 - ADAM_B1 ** ADAM_STEP)
    v_hat = v / (1.0 - ADAM_B2 ** ADAM_STEP)
    delta = -ADAM_LR * (m_hat / (jnp.sqrt(v_hat) + ADAM_EPS) + ADAM_WD * w)
    return delta, m, v


def _adamw(ws, gs, ms, vs, row_blocks, name):
    n = len(ws)

    def body(*refs):
        w, g, m, v = refs[:n], refs[n:2 * n], refs[2 * n:3 * n], refs[3 * n:4 * n]
        d, mo, vo, go = refs[4 * n:5 * n], refs[5 * n:6 * n], refs[6 * n:7 * n], refs[7 * n:]
        for t in range(n):
            gv = g[t][...]
            delta, m_new, v_new = _adamw_math(w[t][...], gv, m[t][...], v[t][...])
            d[t][...] = delta
            mo[t][...] = m_new
            vo[t][...] = v_new
            go[t][...] = gv

    specs = [pl.BlockSpec((a.shape[0] // row_blocks, a.shape[1]), lambda i: (i, 0)) for a in ws]
    shapes = [jax.ShapeDtypeStruct(a.shape, F32) for a in ws]
    return _pcall(
        body, name=name, grid=(row_blocks,), in_specs=specs * 4, out_specs=specs * 4, out_shape=shapes * 4,
        compiler_params=_params(1),
    )(*ws, *gs, *ms, *vs)


def kernel(x, ffn1_norm, ffn1_w_gate, ffn1_w_up, ffn1_w_down, mix_norm, w_in, conv_w, attn_sinks, w_out, ffn2_norm, ffn2_w_gate, ffn2_w_up, ffn2_w_down, final_norm, loss_target, m_ffn1_norm, m_ffn1_w_gate, m_ffn1_w_up, m_ffn1_w_down, m_mix_norm, m_w_in, m_conv_w, m_attn_sinks, m_w_out, m_ffn2_norm, m_ffn2_w_gate, m_ffn2_w_up, m_ffn2_w_down, m_final_norm, v_ffn1_norm, v_ffn1_w_gate, v_ffn1_w_up, v_ffn1_w_down, v_mix_norm, v_w_in, v_conv_w, v_attn_sinks, v_w_out, v_ffn2_norm, v_ffn2_w_gate, v_ffn2_w_up, v_ffn2_w_down, v_final_norm):
    T, D = x.shape[1], x.shape[2]
    chip = (2 * lax.axis_index("x") + lax.axis_index("y")).astype(jnp.int32)
    core = lax.axis_index("c").astype(jnp.int32)
    place = jnp.stack([chip, core])
    x0 = x[0]
    target = loss_target[0]
    gf = final_norm.reshape(1, D)

    tr = lambda w: jnp.swapaxes(w[0], 0, 1)
    big = [tr(ffn1_w_gate), tr(ffn1_w_up), ffn1_w_down[0], tr(w_in), w_out[0], tr(ffn2_w_gate), tr(ffn2_w_up), ffn2_w_down[0]]
    transposed = [True, True, False, True, False, True, True, False]
    own_b = [w.astype(BF16) for w in big]

    whole = lambda gathered: gathered.reshape(-1, D)

    tab, got1 = _rope_tables(T, "rope_gather_ffn1", _gather_plan(own_b[0:3]))
    wg1, wu1, wd1 = (whole(g) for g in got1)

    res = _ffn_fwd(x0, ffn1_norm, wg1, wu1, wd1, "ffn1_fwd", _gather_plan(own_b[3:8], [conv_w[0]]))
    x1, h1, gate1, up1, act1 = res[:5]
    win, wout, wg2, wu2, wd2 = (whole(g) for g in res[5:10])
    convw4 = lax.dynamic_update_slice(res[10], conv_w, (chip, 0, 0))
    convw = jnp.transpose(convw4, (1, 0, 2)).reshape(3, -1)
    z, hm = _norm_matmul(x1, mix_norm, win, tab, "mix_in_fwd")
    ymix = _mix_core_fwd(z, convw, attn_sinks, "mix_core_fwd")
    dx3, h2, gate2, up2, act2, dgf, loss_part, x2 = _ffn_fwd(x1, ffn2_norm, wg2, wu2, wd2, "ffn2_fwd",
                                                             head=(gf, target), pre=(ymix, wout))

    dx2, dyb2, dgate2, dup2, dg2 = _ffn_bwd(dx3, x2, ffn2_norm, gate2, up2, wg2, wu2, wd2, "ffn2_bwd")
    dz, dcw, dsk, dx2b, dx1, dgm = _mix_core_bwd(z, dx2, wout, win, x1, mix_norm, tab, convw, attn_sinks, "mix_bwd")
    dx0, dyb1, dgate1, dup1, dg1 = _ffn_bwd(dx1, x0, ffn1_norm, gate1, up1, wg1, wu1, wd1, "ffn1_bwd")

    pad = lambda a: jnp.pad(a, ((0, 0), (0, LANES - a.shape[1])))
    vec = jnp.concatenate([dg1, dgm, dg2, dgf, dcw[0:3].reshape(1, -1), pad(dsk[:, 0].reshape(1, -1)),
                           pad(loss_part[:, 0:1])], axis=1)

    jobs = [("ffn2_dwg", dgate2, h2, 5), ("ffn2_dwu", dup2, h2, 6), ("ffn2_dwd", act2, dyb2, 7),
            ("ffn1_dwg", dgate1, h1, 0), ("ffn1_dwu", dup1, h1, 1), ("ffn1_dwd", act1, dyb1, 2),
            ("mix_dwin", dz, hm, 3), ("mix_dwout", ymix, dx2b, 4)]
    n_jobs = len(jobs)
    grad, grad_b, from_sib, pair_b, from_chips, half, g_big = ({} for _ in range(7))

    def stage_plans(t):
        plans, takers = [], []
        if 0 <= t - 1 < n_jobs:
            plans.append(_sibling_plan([grad_b[t - 1]]))
            takers.append((from_sib, t - 1))
        if 0 <= t - 2 < n_jobs:
            plans.append(_scatter_plan([pair_b[t - 2]]))
            takers.append((from_chips, t - 2))
        if 0 <= t - 3 < n_jobs:
            plans.append(_join_plan([half[t - 3]]))
            takers.append((g_big, jobs[t - 3][3]))
        return plans, takers

    def after_stage(t, landed, takers):
        for (store, key), arr in zip(takers, landed):
            store[key] = arr
        pair = (grad[t - 1], from_sib[t - 1]) if 0 <= t - 1 < n_jobs else None
        chip = (grad[t - 2], from_sib[t - 2], from_chips[t - 2]) if 0 <= t - 2 < n_jobs else None
        if pair or chip:
            sums = list(_local_sums(pair, chip, place, f"local_sums_{t}"))
            if pair:
                pair_b[t - 1] = sums.pop(0)
            if chip:
                half[t - 2] = sums.pop(0)

    for t, (name_, a, b, _) in enumerate(jobs):
        plans, takers = stage_plans(t)
        if t == 0:
            plans.append(_all_gather_plan(jnp.pad(vec, ((0, 7), (0, 0)))))
        res = _matmul_tn(a, b, DW_ROW_SPLIT, name_, _merge_plans(plans))
        grad[t], grad_b[t] = (r.reshape(N_CHIPS, -1, D) for r in res[:2])
        landed = list(res[2:])
        if t == 0:
            vec_blocks = landed.pop()
        after_stage(t, landed, takers)

    ws = big
    ms = [tr(m_ffn1_w_gate), tr(m_ffn1_w_up), m_ffn1_w_down[0], tr(m_w_in), m_w_out[0], tr(m_ffn2_w_gate), tr(m_ffn2_w_up), m_ffn2_w_down[0]]
    vs = [tr(v_ffn1_w_gate), tr(v_ffn1_w_up), v_ffn1_w_down[0], tr(v_w_in), v_w_out[0], tr(v_ffn2_w_gate), tr(v_ffn2_w_up), v_ffn2_w_down[0]]
    for t in range(n_jobs, n_jobs + 3):
        plans, takers = stage_plans(t)
        after_stage(t, _run_comm(_merge_plans(plans), f"grads_tail_{t - n_jobs}"), takers)
    upd = {}
    for name_, idx in (("adamw_a", [0, 1, 2, 4]), ("adamw_b", [3, 5, 6, 7])):
        k = len(idx)
        res = _adamw([ws[i] for i in idx], [g_big[i] for i in idx], [ms[i] for i in idx], [vs[i] for i in idx], ADAMW_ROW_BLOCKS, name_)
        for j, i in enumerate(idx):
            upd[i] = (res[j], res[k + j], res[2 * k + j])
            g_big[i] = res[3 * k + j]

    total = _sum_devices(vec_blocks, "small_sum")[0:1]
    g_n1, g_nm, g_n2, g_nf = (total[:, k * D:(k + 1) * D] for k in range(4))
    cw_full = total[:, 4 * D:4 * D + 3 * CONV_WIDTH].reshape(3, CONV_WIDTH)
    cq = CONV_WIDTH // N_CHIPS
    g_cw = lax.dynamic_slice(cw_full, (0, chip * cq), (3, cq))
    off = 4 * D + 3 * CONV_WIDTH
    g_sk = total[:, off:off + N_Q_HEADS]
    loss = total[0, off + LANES]

    sw = [ffn1_norm, mix_norm, conv_w[0], attn_sinks, ffn2_norm, gf]
    sg = [g_n1, g_nm, g_cw, g_sk, g_n2, g_nf]
    sm = [m_ffn1_norm, m_mix_norm, m_conv_w[0], m_attn_sinks, m_ffn2_norm, m_final_norm.reshape(1, D)]
    sv = [v_ffn1_norm, v_mix_norm, v_conv_w[0], v_attn_sinks, v_ffn2_norm, v_final_norm.reshape(1, D)]
    sres = _adamw(sw, sg, sm, sv, 1, "adamw_small")
    supd = [(sres[j], sres[6 + j], sres[12 + j]) for j in range(6)]

    order = [("s", 0), ("b", 0), ("b", 1), ("b", 2), ("s", 1), ("b", 3), ("s", 2), ("s", 3), ("b", 4),
             ("s", 4), ("b", 5), ("b", 6), ("b", 7), ("s", 5)]

    def leaf(kind, i, which):
        if kind == "b":
            a = g_big[i] if which == 0 else upd[i][which - 1]
            return (jnp.swapaxes(a, 0, 1) if transposed[i] else a)[None]
        a = sg[i] if which == 0 else supd[i][which - 1]
        if i == 2:
            return a[None]
        if i == 5:
            return a.reshape(D)
        return a

    outs = [loss, dx0[None]]
    for which in range(4):
        outs += [leaf(kind, i, which) for kind, i in order]
    return tuple(outs)
```

```python
import jax
import jax.numpy as jnp
import numpy as np
from jax import lax
from jax.experimental import pallas as pl
from jax.experimental.pallas import tpu as pltpu

F32 = jnp.float32
BF16 = jnp.bfloat16
MESH = pl.DeviceIdType.MESH

CONV_WIDTH = 512
N_Q_HEADS = 8
HEAD_DIM = 64
BLOCK = 128
ROPE_THETA = 500000.0
ROT_DIM = 16
RMS_EPS = 1e-5
MASK_VALUE = -1e30
ATTN_SCALE = HEAD_DIM ** -0.5
FFN_RES_SCALE = 0.5
ADAM_LR = 0.001
ADAM_B1 = 0.9
ADAM_B2 = 0.999
ADAM_EPS = 1e-08
ADAM_WD = 0.01
ADAM_STEP = 10

N_CHIPS = 4
N_DEV = 8
LANES = 128
VMEM_LIMIT = 56 * 1024 * 1024

_pcall = pl.pallas_call
HBM_SPEC = pl.BlockSpec(memory_space=pltpu.HBM)
ANY_SPEC = pl.BlockSpec(memory_space=pl.ANY)


def _params(n_axes, vmem=VMEM_LIMIT):
    return pltpu.CompilerParams(dimension_semantics=("arbitrary",) * n_axes, vmem_limit_bytes=vmem)


def _dot(a, b):
    return jnp.dot(a, b, preferred_element_type=F32)


def _dot_nt(a, b):
    return lax.dot_general(a, b, (((1,), (1,)), ((), ())), preferred_element_type=F32)


def _dot_tn(a, b):
    return lax.dot_general(a, b, (((0,), (0,)), ((), ())), preferred_element_type=F32)


def _rms_inv(x):
    return lax.rsqrt(jnp.mean(x * x, axis=-1, keepdims=True) + RMS_EPS)


def _norm_bwd(dh, x, g):
    inv = _rms_inv(x)
    xhat = x * inv
    dg = jnp.sum(dh * xhat, axis=0, keepdims=True)
    dxhat = dh * g
    dx = inv * (dxhat - xhat * jnp.mean(dxhat * xhat, axis=-1, keepdims=True))
    return dx, dg


def _place():
    x, y, c = lax.axis_index("x"), lax.axis_index("y"), lax.axis_index("c")
    chips = [(1 - x, y), (x, 1 - y), (1 - x, 1 - y)]
    return x, y, c, chips


class _Plan:
    def __init__(self, arrays, out_shapes, n_sems, start, finish, middle=None, aliases=None):
        self.arrays, self.out_shapes, self.n_sems = list(arrays), list(out_shapes), n_sems
        self.start, self.finish, self.middle = start, finish, middle
        self.aliases = dict(aliases or {})

    def specs(self):
        k = len(self.arrays)
        sems = [pltpu.SemaphoreType.DMA((self.n_sems,)), pltpu.SemaphoreType.DMA((self.n_sems,))]
        return [HBM_SPEC] * k, [HBM_SPEC] * len(self.out_shapes), self.out_shapes, sems


class _SemSlice:
    def __init__(self, ref, offset):
        self.ref, self.offset = ref, offset

    @property
    def at(self):
        return self

    def __getitem__(self, k):
        return self.ref.at[k + self.offset]


def _merge_plans(plans):
    plans = [p for p in plans if p is not None]
    if len(plans) <= 1:
        return plans[0] if plans else None
    arrays, shapes, aliases, spans, n_sems = [], [], {}, [], 0
    for p in plans:
        a0, o0 = len(arrays), len(shapes)
        spans.append((a0, a0 + len(p.arrays), o0, o0 + len(p.out_shapes), n_sems))
        aliases.update({a0 + i: o0 + j for i, j in p.aliases.items()})
        arrays += p.arrays
        shapes += p.out_shapes
        n_sems += p.n_sems

    def run(which):
        def fn(ins, outs, send_sems, recv_sems):
            for p, (a0, a1, o0, o1, s0) in zip(plans, spans):
                part = getattr(p, which)
                if part is not None:
                    part(ins[a0:a1], outs[o0:o1], _SemSlice(send_sems, s0), _SemSlice(recv_sems, s0))
        return fn

    middle = run("middle") if any(p.middle is not None for p in plans) else None
    return _Plan(arrays, shapes, n_sems, run("start"), run("finish"), middle, aliases)


def _sibling_plan(grads_b):
    n = len(grads_b)

    def copies(ins, outs, send_sems, recv_sems):
        x, y, c, _ = _place()

        def copy(t):
            half = ins[t].shape[1] // 2
            return pltpu.make_async_remote_copy(
                src_ref=ins[t].at[:, pl.ds(pl.multiple_of((1 - c) * half, 16), half), :], dst_ref=outs[t],
                send_sem=send_sems.at[t], recv_sem=recv_sems.at[t], device_id=(x, y, 1 - c), device_id_type=MESH)

        return [copy(t) for t in range(n)]

    def start(*refs):
        for cp in copies(*refs):
            cp.start()

    def finish(*refs):
        for cp in copies(*refs):
            cp.wait()

    shapes = [jax.ShapeDtypeStruct((g.shape[0], g.shape[1] // 2, g.shape[2]), g.dtype) for g in grads_b]
    return _Plan(grads_b, shapes, n, start, finish)


def _scatter_plan(parts_b):
    n = len(parts_b)

    def copies(ins, outs, send_sems, recv_sems):
        x, y, c, chips = _place()

        def copy(t, j):
            px, py = chips[j]
            return pltpu.make_async_remote_copy(
                src_ref=ins[t].at[2 * px + py], dst_ref=outs[t].at[j], send_sem=send_sems.at[3 * t + j],
                recv_sem=recv_sems.at[3 * t + j], device_id=(px, py, c), device_id_type=MESH)

        return [copy(t, j) for t in range(n) for j in range(3)]

    def start(*refs):
        for cp in copies(*refs):
            cp.start()

    def finish(*refs):
        for cp in copies(*refs):
            cp.wait()

    shapes = [jax.ShapeDtypeStruct((3, *p.shape[1:]), p.dtype) for p in parts_b]
    return _Plan(parts_b, shapes, 3 * n, start, finish)


def _gather_plan(shards, small=()):
    n, ns = len(shards), len(small)
    per = 9

    def parts(ins, outs, send_sems, recv_sems):
        x, y, c, chips = _place()
        me = 2 * x + y
        blocks = [2 * px + py for px, py in chips]

        def rows(t, core, piece=None):
            half = ins[t].shape[0] // 2
            if piece is None:
                return pl.ds(pl.multiple_of(core * half, 16), half)
            return pl.ds(pl.multiple_of(core * half + piece * (half // 2), 16), half // 2)

        def remote(src, dst, k, device):
            return pltpu.make_async_remote_copy(src_ref=src, dst_ref=dst, send_sem=send_sems.at[k],
                                                recv_sem=recv_sems.at[k], device_id=device, device_id_type=MESH)

        def first(t, j, block, core):
            return remote(ins[t].at[rows(t, core), :], outs[t].at[block, rows(t, core), :], per * t + j, (*chips[j], c))

        def relay(t, j, block, core):
            ref = outs[t].at[block, rows(t, core, j), :]
            return remote(ref, ref, per * t + 2 + j, (*chips[j], c))

        def passed(t, k, block, core, piece=None):
            ref = outs[t].at[block, rows(t, core, piece), :]
            return remote(ref, ref, per * t + 4 + k, (x, y, 1 - c))

        def own(t):
            return remote(ins[t], outs[t].at[me], per * t + 8, (x, y, 1 - c))

        def whole(s, j, block):
            return remote(ins[n + s], outs[n + s].at[block], per * n + 3 * s + j, (*chips[j], c))

        return c, me, blocks, first, relay, passed, whole, own

    def start(*refs):
        c, me, _, first, _, _, whole, own = parts(*refs)
        for t in range(n):
            for j in range(2):
                first(t, j, me, c).start()
        for s in range(ns):
            for j in range(3):
                whole(s, j, me).start()

    def middle(*refs):
        c, _, blocks, first, relay, passed, _, own = parts(*refs)
        for t in range(n):
            for j in range(2):
                first(t, j, blocks[j], c).wait_recv()
                passed(t, j, blocks[j], c).start()
                relay(t, 1 - j, blocks[j], c).start()
            own(t).start()

    def finish(*refs):
        c, me, blocks, first, relay, passed, whole, own = parts(*refs)
        for t in range(n):
            for j in range(2):
                relay(t, j, blocks[2], c).wait_recv()
                passed(t, 2 + j, blocks[2], c, j).start()
        for t in range(n):
            for j in range(2):
                passed(t, j, blocks[j], 1 - c).wait_recv()
                passed(t, 2 + j, blocks[2], 1 - c, j).wait_recv()
            own(t).wait_recv()
        for s in range(ns):
            for j in range(3):
                whole(s, j, blocks[j]).wait_recv()
        for t in range(n):
            for j in range(2):
                first(t, j, me, c).wait_send()
                relay(t, 1 - j, blocks[j], c).wait_send()
                passed(t, j, blocks[j], c).wait_send()
                passed(t, 2 + j, blocks[2], c, j).wait_send()
            own(t).wait_send()
        for s in range(ns):
            for j in range(3):
                whole(s, j, me).wait_send()

    arrays = [*shards, *small]
    shapes = [jax.ShapeDtypeStruct((N_CHIPS, *a.shape), a.dtype) for a in arrays]
    return _Plan(arrays, shapes, per * n + 3 * ns, start, finish, middle)


def _run_comm(plan, name):
    k = len(plan.arrays)
    in_specs, out_specs, out_shape, sems = plan.specs()

    def body(*refs):
        cr = (refs[:k], refs[k:k + len(out_shape)], refs[-2], refs[-1])
        plan.start(*cr)
        if plan.middle is not None:
            plan.middle(*cr)
        plan.finish(*cr)

    return _pcall(body, name=name, in_specs=in_specs, out_specs=out_specs, out_shape=out_shape,
                  input_output_aliases=plan.aliases, scratch_shapes=sems)(*plan.arrays)


def _carried(plan, in_specs, out_specs, out_shape, scratch):
    aliases = {}
    if plan is not None:
        p_in, p_out, p_shape, p_sems = plan.specs()
        aliases = {len(in_specs) + i: len(out_specs) + j for i, j in plan.aliases.items()}
        in_specs, out_specs = in_specs + p_in, out_specs + p_out
        out_shape, scratch = out_shape + p_shape, scratch + p_sems
    return dict(in_specs=in_specs, out_specs=out_specs, out_shape=out_shape, scratch_shapes=scratch,
                input_output_aliases=aliases)


def _unpack(refs, n_in, n_out, plan):
    k_in = len(plan.arrays) if plan else 0
    k_out = len(plan.out_shapes) if plan else 0
    ins = refs[:n_in]
    outs = refs[n_in + k_in:n_in + k_in + n_out]
    rest = refs[n_in + k_in + n_out + k_out:]
    if plan is None:
        return ins, outs, rest, None
    cr = (refs[n_in:n_in + k_in], refs[n_in + k_in + n_out:n_in + k_in + n_out + k_out], rest[-2], rest[-1])
    return ins, outs, rest[:-2], cr


def _hook(plan, cr, which, cond):
    fn = getattr(plan, which) if plan is not None else None
    if fn is not None:
        pl.when(cond)(lambda: fn(*cr))


def _join_plan(shards):
    n = len(shards)

    def copy(ins, outs, send_sems, recv_sems, t, core):
        x, y, c, _ = _place()
        half = ins[t].shape[0] // 2
        rows = pl.ds(pl.multiple_of(core * half, 8), half)
        return pltpu.make_async_remote_copy(
            src_ref=ins[t].at[rows, :], dst_ref=outs[t].at[rows, :], send_sem=send_sems.at[t],
            recv_sem=recv_sems.at[t], device_id=(x, y, 1 - c), device_id_type=MESH)

    def start(*refs):
        c = lax.axis_index("c")
        for t in range(n):
            copy(*refs, t, c).start()

    def finish(*refs):
        c = lax.axis_index("c")
        for t in range(n):
            copy(*refs, t, 1 - c).wait_recv()
        for t in range(n):
            copy(*refs, t, c).wait_send()

    shapes = [jax.ShapeDtypeStruct(s.shape, s.dtype) for s in shards]
    return _Plan(shards, shapes, n, start, finish, aliases={t: t for t in range(n)})


def _all_gather_plan(vec):
    def parts(ins, outs, send_sems, recv_sems):
        x, y, c, _ = _place()
        me = 4 * x + 2 * y + c
        rel = [((k >> 2) & 1, (k >> 1) & 1, k & 1) for k in range(1, N_DEV)]

        def peer(k):
            fx, fy, fc = rel[k]
            return (x ^ fx, y ^ fy, c ^ fc)

        def copy(k, dev):
            return pltpu.make_async_remote_copy(
                src_ref=ins[0], dst_ref=outs[0].at[dev], send_sem=send_sems.at[k], recv_sem=recv_sems.at[k],
                device_id=peer(k), device_id_type=MESH)

        mine = pltpu.make_async_copy(ins[0], outs[0].at[me], send_sems.at[N_DEV - 1])
        return me, peer, copy, mine

    def start(*refs):
        me, _, copy, mine = parts(*refs)
        mine.start()
        for k in range(N_DEV - 1):
            copy(k, me).start()

    def finish(*refs):
        me, peer, copy, mine = parts(*refs)
        for k in range(N_DEV - 1):
            px, py, pc = peer(k)
            copy(k, 4 * px + 2 * py + pc).wait_recv()
        for k in range(N_DEV - 1):
            copy(k, me).wait_send()
        mine.wait()

    return _Plan([vec], [jax.ShapeDtypeStruct((N_DEV, *vec.shape), vec.dtype)], N_DEV, start, finish)


def _sum_devices(blocks, name):
    def body(b_ref, o_ref):
        total = b_ref[0]
        for dev in range(1, N_DEV):
            total = total + b_ref[dev]
        o_ref[...] = total

    return _pcall(body, name=name, in_specs=[pl.BlockSpec(memory_space=pltpu.VMEM)],
                  out_specs=pl.BlockSpec(memory_space=pltpu.VMEM),
                  out_shape=jax.ShapeDtypeStruct(blocks.shape[1:], F32))(blocks)


TOKEN_TILE = 512
PROJ_TOKEN_TILE = 1024
ADAMW_ROW_BLOCKS = 4
LARGE_VMEM_LIMIT = 62 * 1024 * 1024
DW_TOKEN_TILE = 2048
DW_ROW_SPLIT = 2
MXU_COLS = 256
DH_GROUP = 6


def _chunks(n):
    out, c0 = [], 0
    while c0 < n:
        size = min(MXU_COLS, n - c0)
        out.append((c0, size))
        c0 += size
    return out


def _load_weights(hbm_refs, vmem_refs, sems):
    copies = [pltpu.make_async_copy(h, v, sems.at[k]) for k, (h, v) in enumerate(zip(hbm_refs, vmem_refs))]
    for cp in copies:
        cp.start()
    for cp in copies:
        cp.wait()


def _ffn_fwd(x, g, wgt, wut, wd, name, plan=None, head=None, pre=None):
    T, D = x.shape
    F = wgt.shape[0]
    tm = min(T, TOKEN_TILE)
    ni = T // tm
    n_head = 2 if head is not None else 0
    n_pre = 1 if pre is not None else 0

    def body(*refs):
        ins, outs, scratch, cr = _unpack(refs, 5 + n_head + 2 * n_pre, 5 + n_head + n_pre, plan)
        x_ref, g_ref, wg_hbm, wu_hbm, wd_hbm = ins[:5]
        xo_ref, h_ref, gate_ref, up_ref, act_ref = outs[:5]
        wg_ref, wu_ref, wd_ref, sems = scratch
        i = pl.program_id(0)
        _hook(plan, cr, "start", i == 0)

        @pl.when(i == 0)
        def _():
            _load_weights((wg_hbm, wu_hbm, wd_hbm), (wg_ref, wu_ref, wd_ref), sems)

        if pre is not None:
            a_ref, w_ref = ins[5 + n_head:]
            x_ref = outs[5 + n_head]
            x_ref[...] = ins[0][...] + _dot(a_ref[...], w_ref[...])
        xv = x_ref[...]
        h = ((xv * _rms_inv(xv)) * g_ref[...]).astype(BF16)
        h_ref[...] = h
        for c0, size in _chunks(F):
            gate = _dot_nt(h, wg_ref[c0:c0 + size, :])
            up = _dot_nt(h, wu_ref[c0:c0 + size, :])
            gate_ref[:, c0:c0 + size] = gate.astype(BF16)
            up_ref[:, c0:c0 + size] = up.astype(BF16)
            act_ref[:, c0:c0 + size] = (gate * jax.nn.sigmoid(gate) * up).astype(BF16)
        y = x_ref[...] + FFN_RES_SCALE * _dot(act_ref[...], wd_ref[...])
        if head is None:
            xo_ref[...] = y
        else:
            gf_ref, t_ref = ins[5:7]
            dgf_ref, loss_ref = outs[5:7]

            @pl.when(i == 0)
            def _():
                dgf_ref[...] = jnp.zeros_like(dgf_ref)
                loss_ref[...] = jnp.zeros_like(loss_ref)

            gf = gf_ref[...]
            diff = (y * _rms_inv(y)) * gf - t_ref[...]
            loss_ref[...] += 0.5 * jnp.sum(jnp.mean(diff * diff, axis=-1, keepdims=True))
            dy, dgf = _norm_bwd(diff * (1.0 / D), y, gf)
            xo_ref[...] = dy
            dgf_ref[...] += dgf
        _hook(plan, cr, "middle", i == ni // 2)
        _hook(plan, cr, "finish", i == ni - 1)

    const = lambda shape: pl.BlockSpec(shape, lambda i: (0, 0))
    rows = lambda width: pl.BlockSpec((tm, width), lambda i: (i, 0))
    in_specs = [rows(D), const((1, D)), ANY_SPEC, ANY_SPEC, ANY_SPEC]
    out_specs = [rows(D), rows(D), rows(F), rows(F), rows(F)]
    out_shape = [jax.ShapeDtypeStruct((T, D), F32), jax.ShapeDtypeStruct((T, D), BF16),
                 jax.ShapeDtypeStruct((T, F), BF16), jax.ShapeDtypeStruct((T, F), BF16), jax.ShapeDtypeStruct((T, F), BF16)]
    if head is not None:
        in_specs += [const((1, D)), rows(D)]
        out_specs += [const((1, D)), const((1, LANES))]
        out_shape += [jax.ShapeDtypeStruct((1, D), F32), jax.ShapeDtypeStruct((1, LANES), F32)]
    if pre is not None:
        in_specs += [rows(pre[0].shape[1]), const(pre[1].shape)]
        out_specs += [rows(D)]
        out_shape += [jax.ShapeDtypeStruct((T, D), F32)]
    io = _carried(plan, in_specs, out_specs, out_shape,
                  [pltpu.VMEM((F, D), BF16), pltpu.VMEM((F, D), BF16), pltpu.VMEM((F, D), BF16),
                   pltpu.SemaphoreType.DMA((3,))])
    return _pcall(
        body, name=name, grid=(ni,), compiler_params=_params(1, VMEM_LIMIT if pre is None else LARGE_VMEM_LIMIT), **io,
    )(x, g, wgt, wut, wd, *(head or ()), *(pre or ()), *(plan.arrays if plan else ()))


def _ffn_bwd(dy, x, g, gate, up, wgt, wut, wd, name):
    T, D = x.shape
    F = wgt.shape[0]
    tm = min(T, TOKEN_TILE)
    ni = T // tm

    def body(dy_ref, x_ref, g_ref, gate_ref, up_ref, wg_hbm, wu_hbm, wd_hbm,
             dx_ref, dyb_ref, dgate_ref, dup_ref, dg_ref, wg_ref, wu_ref, wd_ref, sems):
        @pl.when(pl.program_id(0) == 0)
        def _():
            _load_weights((wg_hbm, wu_hbm, wd_hbm), (wg_ref, wu_ref, wd_ref), sems)
            dg_ref[...] = jnp.zeros_like(dg_ref)

        dyb = (FFN_RES_SCALE * dy_ref[...]).astype(BF16)
        dyb_ref[...] = dyb
        dh, group_g, group_u, row0 = None, [], [], 0
        chunks = _chunks(F)
        for k, (c0, size) in enumerate(chunks):
            dact = _dot_nt(dyb, wd_ref[c0:c0 + size, :])
            gt = gate_ref[:, c0:c0 + size].astype(F32)
            u = up_ref[:, c0:c0 + size].astype(F32)
            sig = jax.nn.sigmoid(gt)
            dup = (dact * (gt * sig)).astype(BF16)
            dgate = (dact * u * (sig * (1.0 + gt * (1.0 - sig)))).astype(BF16)
            dup_ref[:, c0:c0 + size] = dup
            dgate_ref[:, c0:c0 + size] = dgate
            group_g.append(dgate)
            group_u.append(dup)
            if len(group_g) == DH_GROUP or k == len(chunks) - 1:
                rows = slice(row0, c0 + size)
                part = (_dot(jnp.concatenate(group_g, axis=1), wg_ref[rows, :])
                        + _dot(jnp.concatenate(group_u, axis=1), wu_ref[rows, :]))
                dh = part if dh is None else dh + part
                group_g, group_u, row0 = [], [], c0 + size
        dxn, dg = _norm_bwd(dh, x_ref[...], g_ref[...])
        dx_ref[...] = dy_ref[...] + dxn
        dg_ref[...] += dg

    return _pcall(
        body, name=name, grid=(ni,),
        in_specs=[pl.BlockSpec((tm, D), lambda i: (i, 0)), pl.BlockSpec((tm, D), lambda i: (i, 0)),
                  pl.BlockSpec((1, D), lambda i: (0, 0)),
                  pl.BlockSpec((tm, F), lambda i: (i, 0)), pl.BlockSpec((tm, F), lambda i: (i, 0)),
                  ANY_SPEC, ANY_SPEC, ANY_SPEC],
        out_specs=[pl.BlockSpec((tm, D), lambda i: (i, 0)), pl.BlockSpec((tm, D), lambda i: (i, 0)),
                   pl.BlockSpec((tm, F), lambda i: (i, 0)), pl.BlockSpec((tm, F), lambda i: (i, 0)),
                   pl.BlockSpec((1, D), lambda i: (0, 0))],
        out_shape=[jax.ShapeDtypeStruct((T, D), F32), jax.ShapeDtypeStruct((T, D), BF16),
                   jax.ShapeDtypeStruct((T, F), BF16), jax.ShapeDtypeStruct((T, F), BF16),
                   jax.ShapeDtypeStruct((1, D), F32)],
        scratch_shapes=[pltpu.VMEM((F, D), BF16), pltpu.VMEM((F, D), BF16), pltpu.VMEM((F, D), BF16),
                        pltpu.SemaphoreType.DMA((3,))],
        compiler_params=_params(1, LARGE_VMEM_LIMIT),
    )(dy, x, g, gate, up, wgt, wut, wd)


def _matmul_tn(a, b, row_split, name, plan=None):
    T, n1 = a.shape
    n2 = b.shape[1]
    tn = n1 // row_split
    tk = min(T, DW_TOKEN_TILE)
    nk = T // tk

    def body(*refs):
        (a_ref, b_ref), (o_ref, ob_ref), _, cr = _unpack(refs, 2, 2, plan)
        j = pl.program_id(0)
        k = pl.program_id(1)
        _hook(plan, cr, "start", jnp.logical_and(j == 0, k == 0))

        @pl.when(k == 0)
        def _():
            o_ref[...] = jnp.zeros_like(o_ref)

        o_ref[...] += _dot_tn(a_ref[...], b_ref[...])

        @pl.when(k == nk - 1)
        def _():
            ob_ref[...] = o_ref[...].astype(BF16)

        _hook(plan, cr, "finish", jnp.logical_and(j == row_split - 1, k == nk - 1))

    io = _carried(
        plan,
        [pl.BlockSpec((tk, tn), lambda j, k: (k, j)), pl.BlockSpec((tk, n2), lambda j, k: (k, 0))],
        [pl.BlockSpec((tn, n2), lambda j, k: (j, 0)), pl.BlockSpec((tn, n2), lambda j, k: (j, 0))],
        [jax.ShapeDtypeStruct((n1, n2), F32), jax.ShapeDtypeStruct((n1, n2), BF16)], [])
    return _pcall(
        body, name=name, grid=(row_split, nk), compiler_params=_params(2), **io,
    )(a, b, *(plan.arrays if plan else ()))


def _norm_matmul(x, g, wt, tab, name):
    T, D = x.shape
    n = wt.shape[0]
    tm = min(T, PROJ_TOKEN_TILE)

    def body(x_ref, g_ref, w_ref, tab_ref, z_ref, h_ref):
        xv = x_ref[...]
        h = ((xv * _rms_inv(xv)) * g_ref[...]).astype(BF16)
        h_ref[...] = h
        z = _dot_nt(h, w_ref[...])
        z_ref[:, 0:Z_Q] = z[:, 0:Z_Q]
        tab_v = tab_ref[...]
        for c0 in range(Z_Q, Z_V, LANES):
            z_ref[:, c0:c0 + LANES] = _rot(z[:, c0:c0 + LANES], tab_v)
        z_ref[:, Z_V:Z_END] = z[:, Z_V:Z_END]

    return _pcall(
        body, name=name, grid=(T // tm,),
        in_specs=[pl.BlockSpec((tm, D), lambda i: (i, 0)), pl.BlockSpec((1, D), lambda i: (0, 0)),
                  pl.BlockSpec((n, D), lambda i: (0, 0)), pl.BlockSpec((tm, 3 * LANES), lambda i: (i, 0))],
        out_specs=[pl.BlockSpec((tm, n), lambda i: (i, 0)), pl.BlockSpec((tm, D), lambda i: (i, 0))],
        out_shape=[jax.ShapeDtypeStruct((T, n), F32), jax.ShapeDtypeStruct((T, D), BF16)],
        compiler_params=_params(1),
    )(x, g, wt, tab)


Z_Q = 3 * CONV_WIDTH
Z_K = Z_Q + N_Q_HEADS * HEAD_DIM
Z_V = Z_K + LANES
Z_END = Z_V + LANES


def _rope_tables(T, name, plan):
    half = ROT_DIM // 2
    inv_freq = ROPE_THETA ** (-jnp.arange(0, ROT_DIM, 2, dtype=F32) / ROT_DIM)
    ang = inv_freq[:, None] * jnp.arange(T, dtype=F32)[None, :]
    cos_sin = jnp.concatenate([jnp.cos(ang), jnp.sin(ang)], axis=0)
    select = np.zeros((2 * half, 3 * LANES), np.float32)
    const = np.zeros((1, 3 * LANES), np.float32)
    for lane in range(LANES):
        d = lane % HEAD_DIM
        if d < half:
            select[d, lane] = 1.0
            select[half + d, LANES + lane] = -1.0
        elif d < ROT_DIM:
            select[d - half, lane] = 1.0
            select[d, 2 * LANES + lane] = 1.0
        else:
            const[0, lane] = 1.0
    tm = PROJ_TOKEN_TILE
    ni = T // tm

    def body(*refs):
        (cs_ref, sel_ref, const_ref), (tab_ref,), _, cr = _unpack(refs, 3, 1, plan)
        i = pl.program_id(0)
        _hook(plan, cr, "start", i == 0)
        tab_ref[...] = lax.dot_general(cs_ref[...], sel_ref[...], (((0,), (0,)), ((), ())),
                                       precision=lax.Precision.HIGHEST, preferred_element_type=F32) + const_ref[...]
        _hook(plan, cr, "middle", i == ni - 1)
        _hook(plan, cr, "finish", i == ni - 1)

    io = _carried(
        plan,
        [pl.BlockSpec((2 * half, tm), lambda i: (0, i)), pl.BlockSpec((2 * half, 3 * LANES), lambda i: (0, 0)),
         pl.BlockSpec((1, 3 * LANES), lambda i: (0, 0))],
        [pl.BlockSpec((tm, 3 * LANES), lambda i: (i, 0))], [jax.ShapeDtypeStruct((T, 3 * LANES), F32)], [])
    res = _pcall(body, name=name, grid=(ni,), compiler_params=_params(1), **io)(
        cos_sin, jnp.asarray(select), jnp.asarray(const), *plan.arrays)
    return res[0], res[1:]


def _tab3(tab):
    return tab[:, 0:LANES], tab[:, LANES:2 * LANES], tab[:, 2 * LANES:3 * LANES]


def _rot(x, tab):
    c, s1, s2 = _tab3(tab)
    return x * c + pltpu.roll(x, LANES - ROT_DIM // 2, 1) * s1 + pltpu.roll(x, ROT_DIM // 2, 1) * s2


def _rot_t(d, tab):
    c, s1, s2 = _tab3(tab)
    return d * c + pltpu.roll(d * s1, ROT_DIM // 2, 1) + pltpu.roll(d * s2, LANES - ROT_DIM // 2, 1)


def _head_pads(a):
    lo = lax.broadcasted_iota(jnp.int32, a.shape, 1) < HEAD_DIM
    nat0 = jnp.where(lo, a, 0.0)
    nat1 = jnp.where(lo, 0.0, a)
    return {
        (0, 0): nat0.astype(BF16), (0, 1): pltpu.roll(nat0, HEAD_DIM, 1).astype(BF16),
        (1, 0): pltpu.roll(nat1, HEAD_DIM, 1).astype(BF16), (1, 1): nat1.astype(BF16),
    }


def _from_pads(even, odd, kv):
    lo = lax.broadcasted_iota(jnp.int32, even.shape, 1) < HEAD_DIM
    if kv == 0:
        return jnp.where(lo, even + pltpu.roll(odd, HEAD_DIM, 1), 0.0)
    return jnp.where(lo, 0.0, pltpu.roll(even, HEAD_DIM, 1) + odd)


N_GROUPS = 4


def _group_head(g, r):
    kv, par = divmod(g, 2)
    return 2 * (2 * kv + r) + par


def _window_mask_t(has_prev):
    jj = lax.broadcasted_iota(jnp.int32, (2 * BLOCK, 2 * BLOCK), 0)
    ii = lax.broadcasted_iota(jnp.int32, (2 * BLOCK, 2 * BLOCK), 1) & (BLOCK - 1)
    rel = jj - BLOCK - ii
    return (rel <= 0) & (rel > -BLOCK) & ((jj >= BLOCK) | has_prev)


def _sink_row(sink_ref, g):
    lane = lax.broadcasted_iota(jnp.int32, (1, 2 * BLOCK), 1)
    return jnp.where(lane < BLOCK, sink_ref[0, _group_head(g, 0)], sink_ref[0, _group_head(g, 1)])


def _attn_probs_t(q2, kp, mask, sink_ref):
    out = []
    for kv in range(2):
        q_st = jnp.concatenate([q2[2 * kv], q2[2 * kv + 1]], axis=0)
        for par in range(2):
            s = jnp.where(mask, _dot_nt(kp[(kv, par)], q_st), MASK_VALUE)
            sink = _sink_row(sink_ref, 2 * kv + par)
            m = jnp.maximum(jnp.max(s, axis=0, keepdims=True), sink)
            p = jnp.exp(s - m)
            esink = jnp.exp(sink - m)
            rden = 1.0 / (jnp.sum(p, axis=0, keepdims=True) + esink)
            out.append((p * rden, esink * rden))
    return out


def _conv_taps(cg, u, cg_prev, u_prev, has_prev):
    vv = cg * u
    halo = jnp.where(has_prev, cg_prev * u_prev, 0.0)
    ext = jnp.concatenate([halo, vv], axis=0)
    rows = ext.shape[0]
    vv1 = pltpu.roll(ext, 1, 0)[8:rows]
    vv2 = pltpu.roll(ext, 2, 0)[8:rows]
    return vv, vv1, vv2


MIX_BLOCKS = 4


def _mix_core_fwd(z, conv_w, sinks, name):
    T = z.shape[0]
    rows = MIX_BLOCKS * BLOCK
    steps = T // rows
    prev_block = lambda n: jnp.maximum(MIX_BLOCKS * n - 1, 0)
    prev_rows8 = lambda n: jnp.maximum((rows // 8) * n - 1, 0)

    def body(z_ref, zkvp_ref, cgp_ref, up_ref, cw_ref, sink_ref, y_ref):
        for b in range(MIX_BLOCKS):
            r0 = b * BLOCK
            blk = slice(r0, r0 + BLOCK)
            if b == 0:
                has_prev = pl.program_id(0) > 0
                kv_prev, cg_prev, u_prev = zkvp_ref[...], cgp_ref[...], up_ref[...]
            else:
                has_prev = True
                kv_prev = z_ref[r0 - BLOCK:r0, Z_K:Z_END]
                cg_prev = z_ref[r0 - 8:r0, CONV_WIDTH:2 * CONV_WIDTH]
                u_prev = z_ref[r0 - 8:r0, 2 * CONV_WIDTH:Z_Q]
            bg = z_ref[blk, 0:CONV_WIDTH]
            vv, vv1, vv2 = _conv_taps(z_ref[blk, CONV_WIDTH:2 * CONV_WIDTH], z_ref[blk, 2 * CONV_WIDTH:Z_Q],
                                      cg_prev, u_prev, has_prev)
            conv = cw_ref[0:1, :] * vv2 + cw_ref[1:2, :] * vv1 + cw_ref[2:3, :] * vv
            y_ref[blk, 0:CONV_WIDTH] = (bg * conv).astype(BF16)

            k_all = jnp.concatenate([kv_prev[:, 0:LANES], z_ref[blk, Z_K:Z_V]], axis=0)
            v_all = jnp.concatenate([kv_prev[:, LANES:2 * LANES], z_ref[blk, Z_V:Z_END]], axis=0)
            kp = _head_pads(k_all)
            vp = _head_pads(v_all)
            q2 = [(z_ref[blk, Z_Q + LANES * c:Z_Q + LANES * (c + 1)] * ATTN_SCALE).astype(BF16)
                  for c in range(N_Q_HEADS // 2)]
            probs = _attn_probs_t(q2, kp, _window_mask_t(has_prev), sink_ref)
            for kv in range(2):
                o_t = (_dot_tn(vp[(kv, 0)], probs[2 * kv][0].astype(BF16))
                       + _dot_tn(vp[(kv, 1)], probs[2 * kv + 1][0].astype(BF16)))
                for r in range(2):
                    c = 2 * kv + r
                    y_ref[blk, CONV_WIDTH + LANES * c:CONV_WIDTH + LANES * (c + 1)] = (
                        o_t[:, BLOCK * r:BLOCK * (r + 1)].T.astype(BF16))

    return _pcall(
        body, name=name, grid=(steps,),
        in_specs=[pl.BlockSpec((rows, Z_END), lambda n: (n, 0)),
                  pl.BlockSpec((BLOCK, 2 * LANES), lambda n: (prev_block(n), Z_K // (2 * LANES))),
                  pl.BlockSpec((8, CONV_WIDTH), lambda n: (prev_rows8(n), 1)),
                  pl.BlockSpec((8, CONV_WIDTH), lambda n: (prev_rows8(n), 2)),
                  pl.BlockSpec((3, CONV_WIDTH), lambda n: (0, 0)),
                  pl.BlockSpec(memory_space=pltpu.SMEM)],
        out_specs=pl.BlockSpec((rows, 2 * CONV_WIDTH), lambda n: (n, 0)),
        out_shape=jax.ShapeDtypeStruct((T, 2 * CONV_WIDTH), BF16),
        compiler_params=_params(1),
    )(z, z, z, z, conv_w, sinks)


def _mix_core_bwd(z, dx, wout, win, x, g, tab, conv_w, sinks, name):
    T = z.shape[0]
    D = dx.shape[1]
    nsub = MIX_BLOCKS
    rows = nsub * BLOCK
    steps = T // rows
    last = slice(rows - BLOCK, rows)
    cur = lambda n: jnp.minimum(n, steps - 1)
    prev_block = lambda n: jnp.maximum(nsub * cur(n) - 1, 0)
    prev_rows8 = lambda n: jnp.maximum((rows // 8) * cur(n) - 1, 0)
    next_rows8 = lambda n: jnp.minimum((rows // 8) * (cur(n) + 1), T // 8 - 1)

    def body(z_ref, zkvp_ref, cgp_ref, up_ref, bgn_ref, dx_ref, dxn_ref, wo_ref, tab_ref, tabp_ref, cw_ref, sink_ref,
             wi_ref, xe_ref, g_ref, dxe_ref,
             dz_ref, dcw_ref, dsk_ref, dxb_ref, dxo_ref, dg_ref, held_ref, kv_ref):
        n = pl.program_id(0)

        @pl.when(n == 0)
        def _():
            held_ref[...] = jnp.zeros_like(held_ref)
            kv_ref[...] = jnp.zeros_like(kv_ref)
            dcw_ref[...] = jnp.zeros_like(dcw_ref)
            dsk_ref[...] = jnp.zeros_like(dsk_ref)
            dg_ref[...] = jnp.zeros_like(dg_ref)

        def emit_held():
            dz_ref[:, 0:Z_K] = held_ref[:, 0:Z_K]
            if nsub > 1:
                dz_ref[0:rows - BLOCK, Z_K:Z_END] = held_ref[0:rows - BLOCK, Z_K:Z_END]

        def project_emitted():
            dxn, dg = _norm_bwd(_dot(dz_ref[...], wi_ref[...]), xe_ref[...], g_ref[...])
            dxo_ref[...] = dxe_ref[...] + dxn
            dg_ref[...] += dg

        @pl.when(n < steps)
        def _():
            emit_held()
            dxb = dx_ref[...].astype(BF16)
            dxb_ref[...] = dxb
            dy = _dot_nt(dxb, wo_ref[...])
            dy_next = _dot_nt(dxn_ref[...].astype(BF16), wo_ref[0:CONV_WIDTH, :])
            w0, w1, w2 = cw_ref[0:1, :], cw_ref[1:2, :], cw_ref[2:3, :]
            dk_open, dv_open = kv_ref[:, 0:LANES], kv_ref[:, LANES:2 * LANES]
            for b in range(nsub):
                r0 = b * BLOCK
                blk = slice(r0, r0 + BLOCK)
                before = slice(r0 - BLOCK, r0)
                after8 = slice(r0 + BLOCK, r0 + BLOCK + 8)
                if b == 0:
                    has_prev = n > 0
                    kv_prev, cg_prev, u_prev, tab_p = zkvp_ref[...], cgp_ref[...], up_ref[...], tabp_ref[...]
                else:
                    has_prev = True
                    kv_prev, tab_p = z_ref[before, Z_K:Z_END], tab_ref[before, :]
                    cg_prev = z_ref[r0 - 8:r0, CONV_WIDTH:2 * CONV_WIDTH]
                    u_prev = z_ref[r0 - 8:r0, 2 * CONV_WIDTH:Z_Q]
                if b == nsub - 1:
                    dconv_next = jnp.where(n < steps - 1, dy_next * bgn_ref[...], 0.0)
                else:
                    dconv_next = dy[after8, 0:CONV_WIDTH] * z_ref[after8, 0:CONV_WIDTH]
                bg = z_ref[blk, 0:CONV_WIDTH]
                cg = z_ref[blk, CONV_WIDTH:2 * CONV_WIDTH]
                u = z_ref[blk, 2 * CONV_WIDTH:Z_Q]
                vv, vv1, vv2 = _conv_taps(cg, u, cg_prev, u_prev, has_prev)
                dyc = dy[blk, 0:CONV_WIDTH]
                dbg = dyc * (w0 * vv2 + w1 * vv1 + w2 * vv)
                dconv = dyc * bg
                ext = jnp.concatenate([dconv, dconv_next], axis=0)
                ext_rows = ext.shape[0]
                dvv = (w2 * dconv + w1 * pltpu.roll(ext, ext_rows - 1, 0)[0:BLOCK]
                       + w0 * pltpu.roll(ext, ext_rows - 2, 0)[0:BLOCK])
                dcw_ref[0:1, :] += jnp.sum(dconv * vv2, axis=0, keepdims=True)
                dcw_ref[1:2, :] += jnp.sum(dconv * vv1, axis=0, keepdims=True)
                dcw_ref[2:3, :] += jnp.sum(dconv * vv, axis=0, keepdims=True)

                tab_c = tab_ref[blk, :]
                k_all = jnp.concatenate([kv_prev[:, 0:LANES], z_ref[blk, Z_K:Z_V]], axis=0)
                v_all = jnp.concatenate([kv_prev[:, LANES:2 * LANES], z_ref[blk, Z_V:Z_END]], axis=0)
                kp = _head_pads(k_all)
                vp = _head_pads(v_all)
                chunks = range(N_Q_HEADS // 2)
                q2 = [(z_ref[blk, Z_Q + LANES * c:Z_Q + LANES * (c + 1)] * ATTN_SCALE).astype(BF16) for c in chunks]
                do2 = [dy[blk, CONV_WIDTH + LANES * c:CONV_WIDTH + LANES * (c + 1)].astype(BF16) for c in chunks]
                probs = _attn_probs_t(q2, kp, _window_mask_t(has_prev), sink_ref)
                dq_chunks = []
                dk_nat = jnp.zeros((2 * BLOCK, LANES), F32)
                dv_nat = jnp.zeros((2 * BLOCK, LANES), F32)
                for kv in range(2):
                    q_st = jnp.concatenate([q2[2 * kv], q2[2 * kv + 1]], axis=0)
                    do_st = jnp.concatenate([do2[2 * kv], do2[2 * kv + 1]], axis=0)
                    dq_t = jnp.zeros((LANES, 2 * BLOCK), F32)
                    dk_par, dv_par = [], []
                    for par in range(2):
                        g = 2 * kv + par
                        pr, psink = probs[g]
                        dp = _dot_nt(vp[(kv, par)], do_st)
                        delta = jnp.sum(dp * pr, axis=0, keepdims=True)
                        ds = (pr * (dp - delta)).astype(BF16)
                        dsink = -psink * delta
                        for r in range(2):
                            h = _group_head(g, r)
                            dsk_ref[h:h + 1, :] += jnp.sum(dsink[:, BLOCK * r:BLOCK * (r + 1)])
                        dq_t = dq_t + _dot_tn(kp[(kv, par)], ds)
                        dk_par.append(_dot(ds, q_st))
                        dv_par.append(_dot(pr.astype(BF16), do_st))
                    for r in range(2):
                        dq_chunks.append(_rot_t(dq_t[:, BLOCK * r:BLOCK * (r + 1)].T * ATTN_SCALE, tab_c))
                    dk_nat = dk_nat + _from_pads(dk_par[0], dk_par[1], kv)
                    dv_nat = dv_nat + _from_pads(dv_par[0], dv_par[1], kv)

                done_ref, done = (dz_ref, last) if b == 0 else (held_ref, before)
                done_ref[done, Z_K:Z_V] = _rot_t(dk_open + dk_nat[0:BLOCK], tab_p).astype(BF16)
                done_ref[done, Z_V:Z_END] = (dv_open + dv_nat[0:BLOCK]).astype(BF16)
                dk_open, dv_open = dk_nat[BLOCK:2 * BLOCK], dv_nat[BLOCK:2 * BLOCK]
                held_ref[blk, 0:CONV_WIDTH] = dbg.astype(BF16)
                held_ref[blk, CONV_WIDTH:2 * CONV_WIDTH] = (dvv * u).astype(BF16)
                held_ref[blk, 2 * CONV_WIDTH:Z_Q] = (dvv * cg).astype(BF16)
                for c in range(N_Q_HEADS // 2):
                    held_ref[blk, Z_Q + LANES * c:Z_Q + LANES * (c + 1)] = dq_chunks[c].astype(BF16)
            kv_ref[:, 0:LANES] = dk_open
            kv_ref[:, LANES:2 * LANES] = dv_open
            project_emitted()

        @pl.when(n == steps)
        def _():
            emit_held()
            dz_ref[last, Z_K:Z_V] = _rot_t(kv_ref[:, 0:LANES], tab_ref[last, :]).astype(BF16)
            dz_ref[last, Z_V:Z_END] = kv_ref[:, LANES:2 * LANES].astype(BF16)
            project_emitted()

    emitted = lambda n: (jnp.maximum(n - 1, 0), 0)
    return _pcall(
        body, name=name, grid=(steps + 1,),
        in_specs=[pl.BlockSpec((rows, Z_END), lambda n: (cur(n), 0)),
                  pl.BlockSpec((BLOCK, 2 * LANES), lambda n: (prev_block(n), Z_K // (2 * LANES))),
                  pl.BlockSpec((8, CONV_WIDTH), lambda n: (prev_rows8(n), 1)),
                  pl.BlockSpec((8, CONV_WIDTH), lambda n: (prev_rows8(n), 2)),
                  pl.BlockSpec((8, CONV_WIDTH), lambda n: (next_rows8(n), 0)),
                  pl.BlockSpec((rows, dx.shape[1]), lambda n: (cur(n), 0)),
                  pl.BlockSpec((8, dx.shape[1]), lambda n: (next_rows8(n), 0)),
                  pl.BlockSpec(wout.shape, lambda n: (0, 0)),
                  pl.BlockSpec((rows, 3 * LANES), lambda n: (cur(n), 0)),
                  pl.BlockSpec((BLOCK, 3 * LANES), lambda n: (prev_block(n), 0)),
                  pl.BlockSpec((3, CONV_WIDTH), lambda n: (0, 0)),
                  pl.BlockSpec(memory_space=pltpu.SMEM),
                  pl.BlockSpec(win.shape, lambda n: (0, 0)), pl.BlockSpec((rows, D), emitted),
                  pl.BlockSpec((1, D), lambda n: (0, 0)), pl.BlockSpec((rows, D), emitted)],
        out_specs=[pl.BlockSpec((rows, Z_END), emitted),
                   pl.BlockSpec((8, CONV_WIDTH), lambda n: (0, 0)), pl.BlockSpec((8, LANES), lambda n: (0, 0)),
                   pl.BlockSpec((rows, D), lambda n: (cur(n), 0)),
                   pl.BlockSpec((rows, D), emitted), pl.BlockSpec((1, D), lambda n: (0, 0))],
        out_shape=[jax.ShapeDtypeStruct((T, Z_END), BF16), jax.ShapeDtypeStruct((8, CONV_WIDTH), F32),
                   jax.ShapeDtypeStruct((8, LANES), F32), jax.ShapeDtypeStruct(dx.shape, BF16),
                   jax.ShapeDtypeStruct((T, D), F32), jax.ShapeDtypeStruct((1, D), F32)],
        scratch_shapes=[pltpu.VMEM((rows, Z_END), BF16), pltpu.VMEM((BLOCK, 2 * LANES), F32)],
        compiler_params=_params(1, LARGE_VMEM_LIMIT),
    )(z, z, z, z, z, dx, dx, wout, tab, tab, conv_w, sinks, win, x, g, dx)


def _local_sums(pair, chip, place, name):
    arrays, in_specs, out_specs, out_shape = [], [], [], []
    if pair is not None:
        g, sib = pair
        blk = (1, *sib.shape[1:])
        arrays += [g, sib]
        in_specs += [pl.BlockSpec(blk, lambda q, p: (q, p[1], 0)), pl.BlockSpec(blk, lambda q, p: (q, 0, 0))]
        out_specs.append(pl.BlockSpec(blk, lambda q, p: (q, 0, 0)))
        out_shape.append(jax.ShapeDtypeStruct(sib.shape, BF16))
    if chip is not None:
        g2, sib2, recv2 = chip
        blk = (1, *sib2.shape[1:])
        arrays += [g2, sib2, recv2]
        in_specs += [pl.BlockSpec(blk, lambda q, p: (p[0], p[1], 0)), pl.BlockSpec(blk, lambda q, p: (p[0], 0, 0)),
                     pl.BlockSpec(recv2.shape, lambda q, p: (0, 0, 0))]
        out_specs.append(pl.BlockSpec(sib2.shape[1:], lambda q, p: (p[1], 0)))
        out_shape.append(jax.ShapeDtypeStruct(g2.shape[1:], F32))

    def body(place_ref, *refs):
        refs = list(refs)
        ins, outs = refs[:len(arrays)], refs[len(arrays):]
        if pair is not None:
            g_ref, sib_ref = ins[:2]
            outs[0][...] = (g_ref[...] + sib_ref[...].astype(F32)).astype(BF16)
        if chip is not None:
            g_ref, sib_ref, recv_ref = ins[-3:]

            @pl.when(pl.program_id(0) == 0)
            def _():
                total = g_ref[0] + sib_ref[0].astype(F32)
                for j in range(3):
                    total = total + recv_ref[j].astype(F32)
                outs[-1][...] = total

    return _pcall(
        body, name=name,
        grid_spec=pltpu.PrefetchScalarGridSpec(num_scalar_prefetch=1, grid=(N_CHIPS,),
                                               in_specs=in_specs, out_specs=out_specs),
        out_shape=out_shape, compiler_params=_params(1),
    )(place, *arrays)


def _adamw_math(w, g, m, v):
    m = ADAM_B1 * m + (1.0 - ADAM_B1) * g
    v = ADAM_B2 * v + (1.0 - ADAM_B2) * (g * g)
    m_hat = m / (1.0 - ADAM_B1 ** ADAM_STEP)
    v_hat = v / (1.0 - ADAM_B2 ** ADAM_STEP)
    delta = -ADAM_LR * (m_hat / (jnp.sqrt(v_hat) + ADAM_EPS) + ADAM_WD * w)
    return delta, m, v


def _adamw(ws, gs, ms, vs, row_blocks, name):
    n = len(ws)

    def body(*refs):
        w, g, m, v = refs[:n], refs[n:2 * n], refs[2 * n:3 * n], refs[3 * n:4 * n]
        d, mo, vo, go = refs[4 * n:5 * n], refs[5 * n:6 * n], refs[6 * n:7 * n], refs[7 * n:]
        for t in range(n):
            gv = g[t][...]
            delta, m_new, v_new = _adamw_math(w[t][...], gv, m[t][...], v[t][...])
            d[t][...] = delta
            mo[t][...] = m_new
            vo[t][...] = v_new
            go[t][...] = gv

    specs = [pl.BlockSpec((a.shape[0] // row_blocks, a.shape[1]), lambda i: (i, 0)) for a in ws]
    shapes = [jax.ShapeDtypeStruct(a.shape, F32) for a in ws]
    return _pcall(
        body, name=name, grid=(row_blocks,), in_specs=specs * 4, out_specs=specs * 4, out_shape=shapes * 4,
        compiler_params=_params(1),
    )(*ws, *gs, *ms, *vs)


def kernel(x, ffn1_norm, ffn1_w_gate, ffn1_w_up, ffn1_w_down, mix_norm, w_in, conv_w, attn_sinks, w_out, ffn2_norm, ffn2_w_gate, ffn2_w_up, ffn2_w_down, final_norm, loss_target, m_ffn1_norm, m_ffn1_w_gate, m_ffn1_w_up, m_ffn1_w_down, m_mix_norm, m_w_in, m_conv_w, m_attn_sinks, m_w_out, m_ffn2_norm, m_ffn2_w_gate, m_ffn2_w_up, m_ffn2_w_down, m_final_norm, v_ffn1_norm, v_ffn1_w_gate, v_ffn1_w_up, v_ffn1_w_down, v_mix_norm, v_w_in, v_conv_w, v_attn_sinks, v_w_out, v_ffn2_norm, v_ffn2_w_gate, v_ffn2_w_up, v_ffn2_w_down, v_final_norm):
    T, D = x.shape[1], x.shape[2]
    chip = (2 * lax.axis_index("x") + lax.axis_index("y")).astype(jnp.int32)
    core = lax.axis_index("c").astype(jnp.int32)
    place = jnp.stack([chip, core])
    x0 = x[0]
    target = loss_target[0]
    gf = final_norm.reshape(1, D)

    tr = lambda w: jnp.swapaxes(w[0], 0, 1)
    big = [tr(ffn1_w_gate), tr(ffn1_w_up), ffn1_w_down[0], tr(w_in), w_out[0], tr(ffn2_w_gate), tr(ffn2_w_up), ffn2_w_down[0]]
    transposed = [True, True, False, True, False, True, True, False]
    own_b = [w.astype(BF16) for w in big]

    whole = lambda gathered: gathered.reshape(-1, D)

    tab, got1 = _rope_tables(T, "rope_gather_ffn1", _gather_plan(own_b[0:3]))
    wg1, wu1, wd1 = (whole(g) for g in got1)

    res = _ffn_fwd(x0, ffn1_norm, wg1, wu1, wd1, "ffn1_fwd", _gather_plan(own_b[3:8], [conv_w[0]]))
    x1, h1, gate1, up1, act1 = res[:5]
    win, wout, wg2, wu2, wd2 = (whole(g) for g in res[5:10])
    convw4 = lax.dynamic_update_slice(res[10], conv_w, (chip, 0, 0))
    convw = jnp.transpose(convw4, (1, 0, 2)).reshape(3, -1)
    z, hm = _norm_matmul(x1, mix_norm, win, tab, "mix_in_fwd")
    ymix = _mix_core_fwd(z, convw, attn_sinks, "mix_core_fwd")
    dx3, h2, gate2, up2, act2, dgf, loss_part, x2 = _ffn_fwd(x1, ffn2_norm, wg2, wu2, wd2, "ffn2_fwd",
                                                             head=(gf, target), pre=(ymix, wout))

    dx2, dyb2, dgate2, dup2, dg2 = _ffn_bwd(dx3, x2, ffn2_norm, gate2, up2, wg2, wu2, wd2, "ffn2_bwd")
    dz, dcw, dsk, dx2b, dx1, dgm = _mix_core_bwd(z, dx2, wout, win, x1, mix_norm, tab, convw, attn_sinks, "mix_bwd")
    dx0, dyb1, dgate1, dup1, dg1 = _ffn_bwd(dx1, x0, ffn1_norm, gate1, up1, wg1, wu1, wd1, "ffn1_bwd")

    pad = lambda a: jnp.pad(a, ((0, 0), (0, LANES - a.shape[1])))
    vec = jnp.concatenate([dg1, dgm, dg2, dgf, dcw[0:3].reshape(1, -1), pad(dsk[:, 0].reshape(1, -1)),
                           pad(loss_part[:, 0:1])], axis=1)

    jobs = [("ffn2_dwg", dgate2, h2, 5), ("ffn2_dwu", dup2, h2, 6), ("ffn2_dwd", act2, dyb2, 7),
            ("ffn1_dwg", dgate1, h1, 0), ("ffn1_dwu", dup1, h1, 1), ("ffn1_dwd", act1, dyb1, 2),
            ("mix_dwin", dz, hm, 3), ("mix_dwout", ymix, dx2b, 4)]
    n_jobs = len(jobs)
    grad, grad_b, from_sib, pair_b, from_chips, half, g_big = ({} for _ in range(7))

    def stage_plans(t):
        plans, takers = [], []
        if 0 <= t - 1 < n_jobs:
            plans.append(_sibling_plan([grad_b[t - 1]]))
            takers.append((from_sib, t - 1))
        if 0 <= t - 2 < n_jobs:
            plans.append(_scatter_plan([pair_b[t - 2]]))
            takers.append((from_chips, t - 2))
        if 0 <= t - 3 < n_jobs:
            plans.append(_join_plan([half[t - 3]]))
            takers.append((g_big, jobs[t - 3][3]))
        return plans, takers

    def after_stage(t, landed, takers):
        for (store, key), arr in zip(takers, landed):
            store[key] = arr
        pair = (grad[t - 1], from_sib[t - 1]) if 0 <= t - 1 < n_jobs else None
        chip = (grad[t - 2], from_sib[t - 2], from_chips[t - 2]) if 0 <= t - 2 < n_jobs else None
        if pair or chip:
            sums = list(_local_sums(pair, chip, place, f"local_sums_{t}"))
            if pair:
                pair_b[t - 1] = sums.pop(0)
            if chip:
                half[t - 2] = sums.pop(0)

    for t, (name_, a, b, _) in enumerate(jobs):
        plans, takers = stage_plans(t)
        if t == 0:
            plans.append(_all_gather_plan(jnp.pad(vec, ((0, 7), (0, 0)))))
        res = _matmul_tn(a, b, DW_ROW_SPLIT, name_, _merge_plans(plans))
        grad[t], grad_b[t] = (r.reshape(N_CHIPS, -1, D) for r in res[:2])
        landed = list(res[2:])
        if t == 0:
            vec_blocks = landed.pop()
        after_stage(t, landed, takers)

    ws = big
    ms = [tr(m_ffn1_w_gate), tr(m_ffn1_w_up), m_ffn1_w_down[0], tr(m_w_in), m_w_out[0], tr(m_ffn2_w_gate), tr(m_ffn2_w_up), m_ffn2_w_down[0]]
    vs = [tr(v_ffn1_w_gate), tr(v_ffn1_w_up), v_ffn1_w_down[0], tr(v_w_in), v_w_out[0], tr(v_ffn2_w_gate), tr(v_ffn2_w_up), v_ffn2_w_down[0]]
    for t in range(n_jobs, n_jobs + 3):
        plans, takers = stage_plans(t)
        after_stage(t, _run_comm(_merge_plans(plans), f"grads_tail_{t - n_jobs}"), takers)
    upd = {}
    for name_, idx in (("adamw_a", [0, 1, 2, 4]), ("adamw_b", [3, 5, 6, 7])):
        k = len(idx)
        res = _adamw([ws[i] for i in idx], [g_big[i] for i in idx], [ms[i] for i in idx], [vs[i] for i in idx], ADAMW_ROW_BLOCKS, name_)
        for j, i in enumerate(idx):
            upd[i] = (res[j], res[k + j], res[2 * k + j])
            g_big[i] = res[3 * k + j]

    total = _sum_devices(vec_blocks, "small_sum")[0:1]
    g_n1, g_nm, g_n2, g_nf = (total[:, k * D:(k + 1) * D] for k in range(4))
    cw_full = total[:, 4 * D:4 * D + 3 * CONV_WIDTH].reshape(3, CONV_WIDTH)
    cq = CONV_WIDTH // N_CHIPS
    g_cw = lax.dynamic_slice(cw_full, (0, chip * cq), (3, cq))
    off = 4 * D + 3 * CONV_WIDTH
    g_sk = total[:, off:off + N_Q_HEADS]
    loss = total[0, off + LANES]

    sw = [ffn1_norm, mix_norm, conv_w[0], attn_sinks, ffn2_norm, gf]
    sg = [g_n1, g_nm, g_cw, g_sk, g_n2, g_nf]
    sm = [m_ffn1_norm, m_mix_norm, m_conv_w[0], m_attn_sinks, m_ffn2_norm, m_final_norm.reshape(1, D)]
    sv = [v_ffn1_norm, v_mix_norm, v_conv_w[0], v_attn_sinks, v_ffn2_norm, v_final_norm.reshape(1, D)]
    sres = _adamw(sw, sg, sm, sv, 1, "adamw_small")
    supd = [(sres[j], sres[6 + j], sres[12 + j]) for j in range(6)]

    order = [("s", 0), ("b", 0), ("b", 1), ("b", 2), ("s", 1), ("b", 3), ("s", 2), ("s", 3), ("b", 4),
             ("s", 4), ("b", 5), ("b", 6), ("b", 7), ("s", 5)]

    def leaf(kind, i, which):
        if kind == "b":
            a = g_big[i] if which == 0 else upd[i][which - 1]
            return (jnp.swapaxes(a, 0, 1) if transposed[i] else a)[None]
        a = sg[i] if which == 0 else supd[i][which - 1]
        if i == 2:
            return a[None]
        if i == 5:
            return a.reshape(D)
        return a

    outs = [loss, dx0[None]]
    for which in range(4):
        outs += [leaf(kind, i, which) for kind, i in order]
    return tuple(outs)
```

```python
import jax
import jax.numpy as jnp
import numpy as np
from jax import lax
from jax.experimental import pallas as pl
from jax.experimental.pallas import tpu as pltpu

F32 = jnp.float32
BF16 = jnp.bfloat16
MESH = pl.DeviceIdType.MESH

CONV_WIDTH = 512
N_Q_HEADS = 8
HEAD_DIM = 64
BLOCK = 128
ROPE_THETA = 500000.0
ROT_DIM = 16
RMS_EPS = 1e-5
MASK_VALUE = -1e30
ATTN_SCALE = HEAD_DIM ** -0.5
FFN_RES_SCALE = 0.5
ADAM_LR = 0.001
ADAM_B1 = 0.9
ADAM_B2 = 0.999
ADAM_EPS = 1e-08
ADAM_WD = 0.01
ADAM_STEP = 10

N_CHIPS = 4
N_DEV = 8
LANES = 128
VMEM_LIMIT = 56 * 1024 * 1024

_pcall = pl.pallas_call
HBM_SPEC = pl.BlockSpec(memory_space=pltpu.HBM)
ANY_SPEC = pl.BlockSpec(memory_space=pl.ANY)


def _params(n_axes, vmem=VMEM_LIMIT):
    return pltpu.CompilerParams(dimension_semantics=("arbitrary",) * n_axes, vmem_limit_bytes=vmem)


def _dot(a, b):
    return jnp.dot(a, b, preferred_element_type=F32)


def _dot_nt(a, b):
    return lax.dot_general(a, b, (((1,), (1,)), ((), ())), preferred_element_type=F32)


def _dot_tn(a, b):
    return lax.dot_general(a, b, (((0,), (0,)), ((), ())), preferred_element_type=F32)


def _rms_inv(x):
    return lax.rsqrt(jnp.mean(x * x, axis=-1, keepdims=True) + RMS_EPS)


def _norm_bwd(dh, x, g):
    inv = _rms_inv(x)
    xhat = x * inv
    dg = jnp.sum(dh * xhat, axis=0, keepdims=True)
    dxhat = dh * g
    dx = inv * (dxhat - xhat * jnp.mean(dxhat * xhat, axis=-1, keepdims=True))
    return dx, dg


def _place():
    x, y, c = lax.axis_index("x"), lax.axis_index("y"), lax.axis_index("c")
    chips = [(1 - x, y), (x, 1 - y), (1 - x, 1 - y)]
    return x, y, c, chips


class _Plan:
    def __init__(self, arrays, out_shapes, n_sems, start, finish, middle=None, aliases=None):
        self.arrays, self.out_shapes, self.n_sems = list(arrays), list(out_shapes), n_sems
        self.start, self.finish, self.middle = start, finish, middle
        self.aliases = dict(aliases or {})

    def specs(self):
        k = len(self.arrays)
        sems = [pltpu.SemaphoreType.DMA((self.n_sems,)), pltpu.SemaphoreType.DMA((self.n_sems,))]
        return [HBM_SPEC] * k, [HBM_SPEC] * len(self.out_shapes), self.out_shapes, sems


class _SemSlice:
    def __init__(self, ref, offset):
        self.ref, self.offset = ref, offset

    @property
    def at(self):
        return self

    def __getitem__(self, k):
        return self.ref.at[k + self.offset]


def _merge_plans(plans):
    plans = [p for p in plans if p is not None]
    if len(plans) <= 1:
        return plans[0] if plans else None
    arrays, shapes, aliases, spans, n_sems = [], [], {}, [], 0
    for p in plans:
        a0, o0 = len(arrays), len(shapes)
        spans.append((a0, a0 + len(p.arrays), o0, o0 + len(p.out_shapes), n_sems))
        aliases.update({a0 + i: o0 + j for i, j in p.aliases.items()})
        arrays += p.arrays
        shapes += p.out_shapes
        n_sems += p.n_sems

    def run(which):
        def fn(ins, outs, send_sems, recv_sems):
            for p, (a0, a1, o0, o1, s0) in zip(plans, spans):
                part = getattr(p, which)
                if part is not None:
                    part(ins[a0:a1], outs[o0:o1], _SemSlice(send_sems, s0), _SemSlice(recv_sems, s0))
        return fn

    middle = run("middle") if any(p.middle is not None for p in plans) else None
    return _Plan(arrays, shapes, n_sems, run("start"), run("finish"), middle, aliases)


def _sibling_plan(grads_b):
    n = len(grads_b)

    def copies(ins, outs, send_sems, recv_sems):
        x, y, c, _ = _place()

        def copy(t):
            half = ins[t].shape[1] // 2
            return pltpu.make_async_remote_copy(
                src_ref=ins[t].at[:, pl.ds(pl.multiple_of((1 - c) * half, 16), half), :], dst_ref=outs[t],
                send_sem=send_sems.at[t], recv_sem=recv_sems.at[t], device_id=(x, y, 1 - c), device_id_type=MESH)

        return [copy(t) for t in range(n)]

    def start(*refs):
        for cp in copies(*refs):
            cp.start()

    def finish(*refs):
        for cp in copies(*refs):
            cp.wait()

    shapes = [jax.ShapeDtypeStruct((g.shape[0], g.shape[1] // 2, g.shape[2]), g.dtype) for g in grads_b]
    return _Plan(grads_b, shapes, n, start, finish)


def _scatter_plan(parts_b):
    n = len(parts_b)

    def copies(ins, outs, send_sems, recv_sems):
        x, y, c, chips = _place()

        def copy(t, j):
            px, py = chips[j]
            return pltpu.make_async_remote_copy(
                src_ref=ins[t].at[2 * px + py], dst_ref=outs[t].at[j], send_sem=send_sems.at[3 * t + j],
                recv_sem=recv_sems.at[3 * t + j], device_id=(px, py, c), device_id_type=MESH)

        return [copy(t, j) for t in range(n) for j in range(3)]

    def start(*refs):
        for cp in copies(*refs):
            cp.start()

    def finish(*refs):
        for cp in copies(*refs):
            cp.wait()

    shapes = [jax.ShapeDtypeStruct((3, *p.shape[1:]), p.dtype) for p in parts_b]
    return _Plan(parts_b, shapes, 3 * n, start, finish)


def _gather_plan(shards, small=()):
    n, ns = len(shards), len(small)
    per = 9

    def parts(ins, outs, send_sems, recv_sems):
        x, y, c, chips = _place()
        me = 2 * x + y
        blocks = [2 * px + py for px, py in chips]

        def rows(t, core, piece=None):
            half = ins[t].shape[0] // 2
            if piece is None:
                return pl.ds(pl.multiple_of(core * half, 16), half)
            return pl.ds(pl.multiple_of(core * half + piece * (half // 2), 16), half // 2)

        def remote(src, dst, k, device):
            return pltpu.make_async_remote_copy(src_ref=src, dst_ref=dst, send_sem=send_sems.at[k],
                                                recv_sem=recv_sems.at[k], device_id=device, device_id_type=MESH)

        def first(t, j, block, core):
            return remote(ins[t].at[rows(t, core), :], outs[t].at[block, rows(t, core), :], per * t + j, (*chips[j], c))

        def relay(t, j, block, core):
            ref = outs[t].at[block, rows(t, core, j), :]
            return remote(ref, ref, per * t + 2 + j, (*chips[j], c))

        def passed(t, k, block, core, piece=None):
            ref = outs[t].at[block, rows(t, core, piece), :]
            return remote(ref, ref, per * t + 4 + k, (x, y, 1 - c))

        def own(t):
            return remote(ins[t], outs[t].at[me], per * t + 8, (x, y, 1 - c))

        def whole(s, j, block):
            return remote(ins[n + s], outs[n + s].at[block], per * n + 3 * s + j, (*chips[j], c))

        return c, me, blocks, first, relay, passed, whole, own

    def start(*refs):
        c, me, _, first, _, _, whole, own = parts(*refs)
        for t in range(n):
            for j in range(2):
                first(t, j, me, c).start()
        for s in range(ns):
            for j in range(3):
                whole(s, j, me).start()

    def middle(*refs):
        c, _, blocks, first, relay, passed, _, own = parts(*refs)
        for t in range(n):
            for j in range(2):
                first(t, j, blocks[j], c).wait_recv()
                passed(t, j, blocks[j], c).start()
                relay(t, 1 - j, blocks[j], c).start()
            own(t).start()

    def finish(*refs):
        c, me, blocks, first, relay, passed, whole, own = parts(*refs)
        for t in range(n):
            for j in range(2):
                relay(t, j, blocks[2], c).wait_recv()
                passed(t, 2 + j, blocks[2], c, j).start()
        for t in range(n):
            for j in range(2):
                passed(t, j, blocks[j], 1 - c).wait_recv()
                passed(t, 2 + j, blocks[2], 1 - c, j).wait_recv()
            own(t).wait_recv()
        for s in range(ns):
            for j in range(3):
                whole(s, j, blocks[j]).wait_recv()
        for t in range(n):
            for j in range(2):
                first(t, j, me, c).wait_send()
                relay(t, 1 - j, blocks[j], c).wait_send()
                passed(t, j, blocks[j], c).wait_send()
                passed(t, 2 + j, blocks[2], c, j).wait_send()
            own(t).wait_send()
        for s in range(ns):
            for j in range(3):
                whole(s, j, me).wait_send()

    arrays = [*shards, *small]
    shapes = [jax.ShapeDtypeStruct((N_CHIPS, *a.shape), a.dtype) for a in arrays]
    return _Plan(arrays, shapes, per * n + 3 * ns, start, finish, middle)


def _run_comm(plan, name):
    k = len(plan.arrays)
    in_specs, out_specs, out_shape, sems = plan.specs()

    def body(*refs):
        cr = (refs[:k], refs[k:k + len(out_shape)], refs[-2], refs[-1])
        plan.start(*cr)
        if plan.middle is not None:
            plan.middle(*cr)
        plan.finish(*cr)

    return _pcall(body, name=name, in_specs=in_specs, out_specs=out_specs, out_shape=out_shape,
                  input_output_aliases=plan.aliases, scratch_shapes=sems)(*plan.arrays)


def _carried(plan, in_specs, out_specs, out_shape, scratch):
    aliases = {}
    if plan is not None:
        p_in, p_out, p_shape, p_sems = plan.specs()
        aliases = {len(in_specs) + i: len(out_specs) + j for i, j in plan.aliases.items()}
        in_specs, out_specs = in_specs + p_in, out_specs + p_out
        out_shape, scratch = out_shape + p_shape, scratch + p_sems
    return dict(in_specs=in_specs, out_specs=out_specs, out_shape=out_shape, scratch_shapes=scratch,
                input_output_aliases=aliases)


def _unpack(refs, n_in, n_out, plan):
    k_in = len(plan.arrays) if plan else 0
    k_out = len(plan.out_shapes) if plan else 0
    ins = refs[:n_in]
    outs = refs[n_in + k_in:n_in + k_in + n_out]
    rest = refs[n_in + k_in + n_out + k_out:]
    if plan is None:
        return ins, outs, rest, None
    cr = (refs[n_in:n_in + k_in], refs[n_in + k_in + n_out:n_in + k_in + n_out + k_out], rest[-2], rest[-1])
    return ins, outs, rest[:-2], cr


def _hook(plan, cr, which, cond):
    fn = getattr(plan, which) if plan is not None else None
    if fn is not None:
        pl.when(cond)(lambda: fn(*cr))


def _join_plan(shards):
    n = len(shards)

    def copy(ins, outs, send_sems, recv_sems, t, core):
        x, y, c, _ = _place()
        half = ins[t].shape[0] // 2
        rows = pl.ds(pl.multiple_of(core * half, 8), half)
        return pltpu.make_async_remote_copy(
            src_ref=ins[t].at[rows, :], dst_ref=outs[t].at[rows, :], send_sem=send_sems.at[t],
            recv_sem=recv_sems.at[t], device_id=(x, y, 1 - c), device_id_type=MESH)

    def start(*refs):
        c = lax.axis_index("c")
        for t in range(n):
            copy(*refs, t, c).start()

    def finish(*refs):
        c = lax.axis_index("c")
        for t in range(n):
            copy(*refs, t, 1 - c).wait_recv()
        for t in range(n):
            copy(*refs, t, c).wait_send()

    shapes = [jax.ShapeDtypeStruct(s.shape, s.dtype) for s in shards]
    return _Plan(shards, shapes, n, start, finish, aliases={t: t for t in range(n)})


def _all_gather_plan(vec):
    def parts(ins, outs, send_sems, recv_sems):
        x, y, c, _ = _place()
        me = 4 * x + 2 * y + c
        rel = [((k >> 2) & 1, (k >> 1) & 1, k & 1) for k in range(1, N_DEV)]

        def peer(k):
            fx, fy, fc = rel[k]
            return (x ^ fx, y ^ fy, c ^ fc)

        def copy(k, dev):
            return pltpu.make_async_remote_copy(
                src_ref=ins[0], dst_ref=outs[0].at[dev], send_sem=send_sems.at[k], recv_sem=recv_sems.at[k],
                device_id=peer(k), device_id_type=MESH)

        mine = pltpu.make_async_copy(ins[0], outs[0].at[me], send_sems.at[N_DEV - 1])
        return me, peer, copy, mine

    def start(*refs):
        me, _, copy, mine = parts(*refs)
        mine.start()
        for k in range(N_DEV - 1):
            copy(k, me).start()

    def finish(*refs):
        me, peer, copy, mine = parts(*refs)
        for k in range(N_DEV - 1):
            px, py, pc = peer(k)
            copy(k, 4 * px + 2 * py + pc).wait_recv()
        for k in range(N_DEV - 1):
            copy(k, me).wait_send()
        mine.wait()

    return _Plan([vec], [jax.ShapeDtypeStruct((N_DEV, *vec.shape), vec.dtype)], N_DEV, start, finish)


def _sum_devices(blocks, name):
    def body(b_ref, o_ref):
        total = b_ref[0]
        for dev in range(1, N_DEV):
            total = total + b_ref[dev]
        o_ref[...] = total

    return _pcall(body, name=name, in_specs=[pl.BlockSpec(memory_space=pltpu.VMEM)],
                  out_specs=pl.BlockSpec(memory_space=pltpu.VMEM),
                  out_shape=jax.ShapeDtypeStruct(blocks.shape[1:], F32))(blocks)


TOKEN_TILE = 512
PROJ_TOKEN_TILE = 1024
ADAMW_ROW_BLOCKS = 4
LARGE_VMEM_LIMIT = 62 * 1024 * 1024
DW_TOKEN_TILE = 2048
DW_ROW_SPLIT = 2
MXU_COLS = 256
DH_GROUP = 6


def _chunks(n):
    out, c0 = [], 0
    while c0 < n:
        size = min(MXU_COLS, n - c0)
        out.append((c0, size))
        c0 += size
    return out


def _piece_copies(hbm_ref, vmem_ref, sems, first, pieces):
    return [pltpu.make_async_copy(hbm_ref.at[pl.ds(r0, size), :], vmem_ref.at[pl.ds(r0, size), :], sems.at[first + k])
            for k, (r0, size) in enumerate(pieces)]


def _ffn_fwd(x, g, wgt, wut, wd, name, plan=None, head=None, pre=None):
    T, D = x.shape
    F = wgt.shape[0]
    tm = min(T, TOKEN_TILE)
    ni = T // tm
    n_head = 2 if head is not None else 0
    n_pre = 1 if pre is not None else 0

    def body(*refs):
        ins, outs, scratch, cr = _unpack(refs, 5 + n_head + 2 * n_pre, 5 + n_head + n_pre, plan)
        x_ref, g_ref, wg_hbm, wu_hbm, wd_hbm = ins[:5]
        xo_ref, h_ref, gate_ref, up_ref, act_ref = outs[:5]
        wg_ref, wu_ref, wd_ref, sems = scratch
        i = pl.program_id(0)
        _hook(plan, cr, "start", i == 0)
        chunks = _chunks(F)
        xin_ref = x_ref if pre is None else outs[5 + n_head]
        if head is not None:
            gf_ref, t_ref = ins[5:7]
            dgf_ref, loss_ref = outs[5:7]

            @pl.when(i == 0)
            def _():
                dgf_ref[...] = jnp.zeros_like(dgf_ref)
                loss_ref[...] = jnp.zeros_like(loss_ref)

        def tile(wait_chunk=None, wait_down=None):
            if pre is not None:
                a_ref, w_ref = ins[5 + n_head:]
                xin_ref[...] = x_ref[...] + _dot(a_ref[...], w_ref[...])
            xv = xin_ref[...]
            h = ((xv * _rms_inv(xv)) * g_ref[...]).astype(BF16)
            h_ref[...] = h
            for k, (c0, size) in enumerate(chunks):
                if wait_chunk is not None:
                    wait_chunk(k)
                gate = _dot_nt(h, wg_ref[c0:c0 + size, :])
                up = _dot_nt(h, wu_ref[c0:c0 + size, :])
                gate_ref[:, c0:c0 + size] = gate.astype(BF16)
                up_ref[:, c0:c0 + size] = up.astype(BF16)
                act_ref[:, c0:c0 + size] = (gate * jax.nn.sigmoid(gate) * up).astype(BF16)
            if wait_down is not None:
                wait_down()
            y = xin_ref[...] + FFN_RES_SCALE * _dot(act_ref[...], wd_ref[...])
            if head is None:
                xo_ref[...] = y
            else:
                gf = gf_ref[...]
                diff = (y * _rms_inv(y)) * gf - t_ref[...]
                loss_ref[...] += 0.5 * jnp.sum(jnp.mean(diff * diff, axis=-1, keepdims=True))
                dy, dgf = _norm_bwd(diff * (1.0 / D), y, gf)
                xo_ref[...] = dy
                dgf_ref[...] += dgf

        @pl.when(i == 0)
        def _():
            cg = _piece_copies(wg_hbm, wg_ref, sems, 0, chunks)
            cu = _piece_copies(wu_hbm, wu_ref, sems, len(chunks), chunks)
            cd = _piece_copies(wd_hbm, wd_ref, sems, 2 * len(chunks), [(0, F)])
            for cp in [c for pair in zip(cg, cu) for c in pair] + cd:
                cp.start()

            def wait_chunk(k):
                cg[k].wait()
                cu[k].wait()

            tile(wait_chunk, cd[0].wait)

        @pl.when(i > 0)
        def _():
            tile()

        _hook(plan, cr, "middle", i == ni // 2)
        _hook(plan, cr, "finish", i == ni - 1)

    const = lambda shape: pl.BlockSpec(shape, lambda i: (0, 0))
    rows = lambda width: pl.BlockSpec((tm, width), lambda i: (i, 0))
    in_specs = [rows(D), const((1, D)), ANY_SPEC, ANY_SPEC, ANY_SPEC]
    out_specs = [rows(D), rows(D), rows(F), rows(F), rows(F)]
    out_shape = [jax.ShapeDtypeStruct((T, D), F32), jax.ShapeDtypeStruct((T, D), BF16),
                 jax.ShapeDtypeStruct((T, F), BF16), jax.ShapeDtypeStruct((T, F), BF16), jax.ShapeDtypeStruct((T, F), BF16)]
    if head is not None:
        in_specs += [const((1, D)), rows(D)]
        out_specs += [const((1, D)), const((1, LANES))]
        out_shape += [jax.ShapeDtypeStruct((1, D), F32), jax.ShapeDtypeStruct((1, LANES), F32)]
    if pre is not None:
        in_specs += [rows(pre[0].shape[1]), const(pre[1].shape)]
        out_specs += [rows(D)]
        out_shape += [jax.ShapeDtypeStruct((T, D), F32)]
    io = _carried(plan, in_specs, out_specs, out_shape,
                  [pltpu.VMEM((F, D), BF16), pltpu.VMEM((F, D), BF16), pltpu.VMEM((F, D), BF16),
                   pltpu.SemaphoreType.DMA((2 * len(_chunks(F)) + 1,))])
    return _pcall(
        body, name=name, grid=(ni,), compiler_params=_params(1, VMEM_LIMIT if pre is None else LARGE_VMEM_LIMIT), **io,
    )(x, g, wgt, wut, wd, *(head or ()), *(pre or ()), *(plan.arrays if plan else ()))


def _ffn_bwd(dy, x, g, gate, up, wgt, wut, wd, name):
    T, D = x.shape
    F = wgt.shape[0]
    tm = min(T, TOKEN_TILE)
    ni = T // tm

    def body(dy_ref, x_ref, g_ref, gate_ref, up_ref, wg_hbm, wu_hbm, wd_hbm,
             dx_ref, dyb_ref, dgate_ref, dup_ref, dg_ref, wg_ref, wu_ref, wd_ref, sems):
        i = pl.program_id(0)
        chunks = _chunks(F)
        ends = [k for k in range(len(chunks)) if (k + 1) % DH_GROUP == 0 or k == len(chunks) - 1]
        starts = [0] + [chunks[k][0] + chunks[k][1] for k in ends[:-1]]
        groups = [(r0, chunks[k][0] + chunks[k][1] - r0) for r0, k in zip(starts, ends)]

        @pl.when(i == 0)
        def _():
            dg_ref[...] = jnp.zeros_like(dg_ref)

        def tile(wait_chunk=None, wait_group=None):
            dyb = (FFN_RES_SCALE * dy_ref[...]).astype(BF16)
            dyb_ref[...] = dyb
            dh, group_g, group_u = None, [], []
            for k, (c0, size) in enumerate(chunks):
                if wait_chunk is not None:
                    wait_chunk(k)
                dact = _dot_nt(dyb, wd_ref[c0:c0 + size, :])
                gt = gate_ref[:, c0:c0 + size].astype(F32)
                u = up_ref[:, c0:c0 + size].astype(F32)
                sig = jax.nn.sigmoid(gt)
                dup = (dact * (gt * sig)).astype(BF16)
                dgate = (dact * u * (sig * (1.0 + gt * (1.0 - sig)))).astype(BF16)
                dup_ref[:, c0:c0 + size] = dup
                dgate_ref[:, c0:c0 + size] = dgate
                group_g.append(dgate)
                group_u.append(dup)
                if k in ends:
                    n_group = ends.index(k)
                    if wait_group is not None:
                        wait_group(n_group)
                    r0, n_rows = groups[n_group]
                    part = (_dot(jnp.concatenate(group_g, axis=1), wg_ref[r0:r0 + n_rows, :])
                            + _dot(jnp.concatenate(group_u, axis=1), wu_ref[r0:r0 + n_rows, :]))
                    dh = part if dh is None else dh + part
                    group_g, group_u = [], []
            dxn, dg = _norm_bwd(dh, x_ref[...], g_ref[...])
            dx_ref[...] = dy_ref[...] + dxn
            dg_ref[...] += dg

        @pl.when(i == 0)
        def _():
            cd = _piece_copies(wd_hbm, wd_ref, sems, 0, chunks)
            cg = _piece_copies(wg_hbm, wg_ref, sems, len(chunks), groups)
            cu = _piece_copies(wu_hbm, wu_ref, sems, len(chunks) + len(groups), groups)
            for cp in cd[:ends[0] + 1] + cg[:1] + cu[:1] + cd[ends[0] + 1:] + cg[1:] + cu[1:]:
                cp.start()

            def wait_group(n_group):
                cg[n_group].wait()
                cu[n_group].wait()

            tile(lambda k: cd[k].wait(), wait_group)

        @pl.when(i > 0)
        def _():
            tile()

    return _pcall(
        body, name=name, grid=(ni,),
        in_specs=[pl.BlockSpec((tm, D), lambda i: (i, 0)), pl.BlockSpec((tm, D), lambda i: (i, 0)),
                  pl.BlockSpec((1, D), lambda i: (0, 0)),
                  pl.BlockSpec((tm, F), lambda i: (i, 0)), pl.BlockSpec((tm, F), lambda i: (i, 0)),
                  ANY_SPEC, ANY_SPEC, ANY_SPEC],
        out_specs=[pl.BlockSpec((tm, D), lambda i: (i, 0)), pl.BlockSpec((tm, D), lambda i: (i, 0)),
                   pl.BlockSpec((tm, F), lambda i: (i, 0)), pl.BlockSpec((tm, F), lambda i: (i, 0)),
                   pl.BlockSpec((1, D), lambda i: (0, 0))],
        out_shape=[jax.ShapeDtypeStruct((T, D), F32), jax.ShapeDtypeStruct((T, D), BF16),
                   jax.ShapeDtypeStruct((T, F), BF16), jax.ShapeDtypeStruct((T, F), BF16),
                   jax.ShapeDtypeStruct((1, D), F32)],
        scratch_shapes=[pltpu.VMEM((F, D), BF16), pltpu.VMEM((F, D), BF16), pltpu.VMEM((F, D), BF16),
                        pltpu.SemaphoreType.DMA((len(_chunks(F)) + 2 * pl.cdiv(len(_chunks(F)), DH_GROUP),))],
        compiler_params=_params(1, LARGE_VMEM_LIMIT),
    )(dy, x, g, gate, up, wgt, wut, wd)


def _matmul_tn(a, b, row_split, name, plan=None):
    T, n1 = a.shape
    n2 = b.shape[1]
    tn = n1 // row_split
    tk = min(T, DW_TOKEN_TILE)
    nk = T // tk

    def body(*refs):
        (a_ref, b_ref), (o_ref, ob_ref), _, cr = _unpack(refs, 2, 2, plan)
        j = pl.program_id(0)
        k = pl.program_id(1)
        _hook(plan, cr, "start", jnp.logical_and(j == 0, k == 0))

        @pl.when(k == 0)
        def _():
            o_ref[...] = jnp.zeros_like(o_ref)

        o_ref[...] += _dot_tn(a_ref[...], b_ref[...])

        @pl.when(k == nk - 1)
        def _():
            ob_ref[...] = o_ref[...].astype(BF16)

        _hook(plan, cr, "finish", jnp.logical_and(j == row_split - 1, k == nk - 1))

    io = _carried(
        plan,
        [pl.BlockSpec((tk, tn), lambda j, k: (k, j)), pl.BlockSpec((tk, n2), lambda j, k: (k, 0))],
        [pl.BlockSpec((tn, n2), lambda j, k: (j, 0)), pl.BlockSpec((tn, n2), lambda j, k: (j, 0))],
        [jax.ShapeDtypeStruct((n1, n2), F32), jax.ShapeDtypeStruct((n1, n2), BF16)], [])
    return _pcall(
        body, name=name, grid=(row_split, nk), compiler_params=_params(2), **io,
    )(a, b, *(plan.arrays if plan else ()))


def _norm_matmul(x, g, wt, tab, name):
    T, D = x.shape
    n = wt.shape[0]
    tm = min(T, PROJ_TOKEN_TILE)

    def body(x_ref, g_ref, w_ref, tab_ref, z_ref, h_ref):
        xv = x_ref[...]
        h = ((xv * _rms_inv(xv)) * g_ref[...]).astype(BF16)
        h_ref[...] = h
        z = _dot_nt(h, w_ref[...])
        z_ref[:, 0:Z_Q] = z[:, 0:Z_Q]
        tab_v = tab_ref[...]
        for c0 in range(Z_Q, Z_V, LANES):
            z_ref[:, c0:c0 + LANES] = _rot(z[:, c0:c0 + LANES], tab_v)
        z_ref[:, Z_V:Z_END] = z[:, Z_V:Z_END]

    return _pcall(
        body, name=name, grid=(T // tm,),
        in_specs=[pl.BlockSpec((tm, D), lambda i: (i, 0)), pl.BlockSpec((1, D), lambda i: (0, 0)),
                  pl.BlockSpec((n, D), lambda i: (0, 0)), pl.BlockSpec((tm, 3 * LANES), lambda i: (i, 0))],
        out_specs=[pl.BlockSpec((tm, n), lambda i: (i, 0)), pl.BlockSpec((tm, D), lambda i: (i, 0))],
        out_shape=[jax.ShapeDtypeStruct((T, n), F32), jax.ShapeDtypeStruct((T, D), BF16)],
        compiler_params=_params(1),
    )(x, g, wt, tab)


Z_Q = 3 * CONV_WIDTH
Z_K = Z_Q + N_Q_HEADS * HEAD_DIM
Z_V = Z_K + LANES
Z_END = Z_V + LANES


def _rope_tables(T, name, plan):
    half = ROT_DIM // 2
    inv_freq = ROPE_THETA ** (-jnp.arange(0, ROT_DIM, 2, dtype=F32) / ROT_DIM)
    ang = inv_freq[:, None] * jnp.arange(T, dtype=F32)[None, :]
    cos_sin = jnp.concatenate([jnp.cos(ang), jnp.sin(ang)], axis=0)
    select = np.zeros((2 * half, 3 * LANES), np.float32)
    const = np.zeros((1, 3 * LANES), np.float32)
    for lane in range(LANES):
        d = lane % HEAD_DIM
        if d < half:
            select[d, lane] = 1.0
            select[half + d, LANES + lane] = -1.0
        elif d < ROT_DIM:
            select[d - half, lane] = 1.0
            select[d, 2 * LANES + lane] = 1.0
        else:
            const[0, lane] = 1.0
    tm = PROJ_TOKEN_TILE
    ni = T // tm

    def body(*refs):
        (cs_ref, sel_ref, const_ref), (tab_ref,), _, cr = _unpack(refs, 3, 1, plan)
        i = pl.program_id(0)
        _hook(plan, cr, "start", i == 0)
        tab_ref[...] = lax.dot_general(cs_ref[...], sel_ref[...], (((0,), (0,)), ((), ())),
                                       precision=lax.Precision.HIGHEST, preferred_element_type=F32) + const_ref[...]
        _hook(plan, cr, "middle", i == ni - 1)
        _hook(plan, cr, "finish", i == ni - 1)

    io = _carried(
        plan,
        [pl.BlockSpec((2 * half, tm), lambda i: (0, i)), pl.BlockSpec((2 * half, 3 * LANES), lambda i: (0, 0)),
         pl.BlockSpec((1, 3 * LANES), lambda i: (0, 0))],
        [pl.BlockSpec((tm, 3 * LANES), lambda i: (i, 0))], [jax.ShapeDtypeStruct((T, 3 * LANES), F32)], [])
    res = _pcall(body, name=name, grid=(ni,), compiler_params=_params(1), **io)(
        cos_sin, jnp.asarray(select), jnp.asarray(const), *plan.arrays)
    return res[0], res[1:]


def _tab3(tab):
    return tab[:, 0:LANES], tab[:, LANES:2 * LANES], tab[:, 2 * LANES:3 * LANES]


def _rot(x, tab):
    c, s1, s2 = _tab3(tab)
    return x * c + pltpu.roll(x, LANES - ROT_DIM // 2, 1) * s1 + pltpu.roll(x, ROT_DIM // 2, 1) * s2


def _rot_t(d, tab):
    c, s1, s2 = _tab3(tab)
    return d * c + pltpu.roll(d * s1, ROT_DIM // 2, 1) + pltpu.roll(d * s2, LANES - ROT_DIM // 2, 1)


def _head_pads(a):
    lo = lax.broadcasted_iota(jnp.int32, a.shape, 1) < HEAD_DIM
    nat0 = jnp.where(lo, a, 0.0)
    nat1 = jnp.where(lo, 0.0, a)
    return {
        (0, 0): nat0.astype(BF16), (0, 1): pltpu.roll(nat0, HEAD_DIM, 1).astype(BF16),
        (1, 0): pltpu.roll(nat1, HEAD_DIM, 1).astype(BF16), (1, 1): nat1.astype(BF16),
    }


def _from_pads(even, odd, kv):
    lo = lax.broadcasted_iota(jnp.int32, even.shape, 1) < HEAD_DIM
    if kv == 0:
        return jnp.where(lo, even + pltpu.roll(odd, HEAD_DIM, 1), 0.0)
    return jnp.where(lo, 0.0, pltpu.roll(even, HEAD_DIM, 1) + odd)


N_GROUPS = 4


def _group_head(g, r):
    kv, par = divmod(g, 2)
    return 2 * (2 * kv + r) + par


def _window_mask_t(has_prev):
    jj = lax.broadcasted_iota(jnp.int32, (2 * BLOCK, 2 * BLOCK), 0)
    ii = lax.broadcasted_iota(jnp.int32, (2 * BLOCK, 2 * BLOCK), 1) & (BLOCK - 1)
    rel = jj - BLOCK - ii
    return (rel <= 0) & (rel > -BLOCK) & ((jj >= BLOCK) | has_prev)


def _sink_row(sink_ref, g):
    lane = lax.broadcasted_iota(jnp.int32, (1, 2 * BLOCK), 1)
    return jnp.where(lane < BLOCK, sink_ref[0, _group_head(g, 0)], sink_ref[0, _group_head(g, 1)])


def _attn_probs_t(q2, kp, mask, sink_ref):
    out = []
    for kv in range(2):
        q_st = jnp.concatenate([q2[2 * kv], q2[2 * kv + 1]], axis=0)
        for par in range(2):
            s = jnp.where(mask, _dot_nt(kp[(kv, par)], q_st), MASK_VALUE)
            sink = _sink_row(sink_ref, 2 * kv + par)
            m = jnp.maximum(jnp.max(s, axis=0, keepdims=True), sink)
            p = jnp.exp(s - m)
            esink = jnp.exp(sink - m)
            rden = 1.0 / (jnp.sum(p, axis=0, keepdims=True) + esink)
            out.append((p * rden, esink * rden))
    return out


def _conv_taps(cg, u, cg_prev, u_prev, has_prev):
    vv = cg * u
    halo = jnp.where(has_prev, cg_prev * u_prev, 0.0)
    ext = jnp.concatenate([halo, vv], axis=0)
    rows = ext.shape[0]
    vv1 = pltpu.roll(ext, 1, 0)[8:rows]
    vv2 = pltpu.roll(ext, 2, 0)[8:rows]
    return vv, vv1, vv2


MIX_BLOCKS = 4


def _mix_core_fwd(z, conv_w, sinks, name):
    T = z.shape[0]
    rows = MIX_BLOCKS * BLOCK
    steps = T // rows
    prev_block = lambda n: jnp.maximum(MIX_BLOCKS * n - 1, 0)
    prev_rows8 = lambda n: jnp.maximum((rows // 8) * n - 1, 0)

    def body(z_ref, zkvp_ref, cgp_ref, up_ref, cw_ref, sink_ref, y_ref):
        for b in range(MIX_BLOCKS):
            r0 = b * BLOCK
            blk = slice(r0, r0 + BLOCK)
            if b == 0:
                has_prev = pl.program_id(0) > 0
                kv_prev, cg_prev, u_prev = zkvp_ref[...], cgp_ref[...], up_ref[...]
            else:
                has_prev = True
                kv_prev = z_ref[r0 - BLOCK:r0, Z_K:Z_END]
                cg_prev = z_ref[r0 - 8:r0, CONV_WIDTH:2 * CONV_WIDTH]
                u_prev = z_ref[r0 - 8:r0, 2 * CONV_WIDTH:Z_Q]
            bg = z_ref[blk, 0:CONV_WIDTH]
            vv, vv1, vv2 = _conv_taps(z_ref[blk, CONV_WIDTH:2 * CONV_WIDTH], z_ref[blk, 2 * CONV_WIDTH:Z_Q],
                                      cg_prev, u_prev, has_prev)
            conv = cw_ref[0:1, :] * vv2 + cw_ref[1:2, :] * vv1 + cw_ref[2:3, :] * vv
            y_ref[blk, 0:CONV_WIDTH] = (bg * conv).astype(BF16)

            k_all = jnp.concatenate([kv_prev[:, 0:LANES], z_ref[blk, Z_K:Z_V]], axis=0)
            v_all = jnp.concatenate([kv_prev[:, LANES:2 * LANES], z_ref[blk, Z_V:Z_END]], axis=0)
            kp = _head_pads(k_all)
            vp = _head_pads(v_all)
            q2 = [(z_ref[blk, Z_Q + LANES * c:Z_Q + LANES * (c + 1)] * ATTN_SCALE).astype(BF16)
                  for c in range(N_Q_HEADS // 2)]
            probs = _attn_probs_t(q2, kp, _window_mask_t(has_prev), sink_ref)
            for kv in range(2):
                o_t = (_dot_tn(vp[(kv, 0)], probs[2 * kv][0].astype(BF16))
                       + _dot_tn(vp[(kv, 1)], probs[2 * kv + 1][0].astype(BF16)))
                for r in range(2):
                    c = 2 * kv + r
                    y_ref[blk, CONV_WIDTH + LANES * c:CONV_WIDTH + LANES * (c + 1)] = (
                        o_t[:, BLOCK * r:BLOCK * (r + 1)].T.astype(BF16))

    return _pcall(
        body, name=name, grid=(steps,),
        in_specs=[pl.BlockSpec((rows, Z_END), lambda n: (n, 0)),
                  pl.BlockSpec((BLOCK, 2 * LANES), lambda n: (prev_block(n), Z_K // (2 * LANES))),
                  pl.BlockSpec((8, CONV_WIDTH), lambda n: (prev_rows8(n), 1)),
                  pl.BlockSpec((8, CONV_WIDTH), lambda n: (prev_rows8(n), 2)),
                  pl.BlockSpec((3, CONV_WIDTH), lambda n: (0, 0)),
                  pl.BlockSpec(memory_space=pltpu.SMEM)],
        out_specs=pl.BlockSpec((rows, 2 * CONV_WIDTH), lambda n: (n, 0)),
        out_shape=jax.ShapeDtypeStruct((T, 2 * CONV_WIDTH), BF16),
        compiler_params=_params(1),
    )(z, z, z, z, conv_w, sinks)


def _mix_core_bwd(z, dx, wout, win, x, g, tab, conv_w, sinks, name):
    T = z.shape[0]
    D = dx.shape[1]
    nsub = MIX_BLOCKS
    rows = nsub * BLOCK
    steps = T // rows
    last = slice(rows - BLOCK, rows)
    cur = lambda n: jnp.minimum(n, steps - 1)
    prev_block = lambda n: jnp.maximum(nsub * cur(n) - 1, 0)
    prev_rows8 = lambda n: jnp.maximum((rows // 8) * cur(n) - 1, 0)
    next_rows8 = lambda n: jnp.minimum((rows // 8) * (cur(n) + 1), T // 8 - 1)

    def body(z_ref, zkvp_ref, cgp_ref, up_ref, bgn_ref, dx_ref, dxn_ref, wo_ref, tab_ref, tabp_ref, cw_ref, sink_ref,
             wi_ref, xe_ref, g_ref, dxe_ref,
             dz_ref, dcw_ref, dsk_ref, dxb_ref, dxo_ref, dg_ref, held_ref, kv_ref):
        n = pl.program_id(0)

        @pl.when(n == 0)
        def _():
            held_ref[...] = jnp.zeros_like(held_ref)
            kv_ref[...] = jnp.zeros_like(kv_ref)
            dcw_ref[...] = jnp.zeros_like(dcw_ref)
            dsk_ref[...] = jnp.zeros_like(dsk_ref)
            dg_ref[...] = jnp.zeros_like(dg_ref)

        def emit_held():
            dz_ref[:, 0:Z_K] = held_ref[:, 0:Z_K]
            if nsub > 1:
                dz_ref[0:rows - BLOCK, Z_K:Z_END] = held_ref[0:rows - BLOCK, Z_K:Z_END]

        def project_emitted():
            dxn, dg = _norm_bwd(_dot(dz_ref[...], wi_ref[...]), xe_ref[...], g_ref[...])
            dxo_ref[...] = dxe_ref[...] + dxn
            dg_ref[...] += dg

        @pl.when(n < steps)
        def _():
            emit_held()
            dxb = dx_ref[...].astype(BF16)
            dxb_ref[...] = dxb
            dy = _dot_nt(dxb, wo_ref[...])
            dy_next = _dot_nt(dxn_ref[...].astype(BF16), wo_ref[0:CONV_WIDTH, :])
            w0, w1, w2 = cw_ref[0:1, :], cw_ref[1:2, :], cw_ref[2:3, :]
            dk_open, dv_open = kv_ref[:, 0:LANES], kv_ref[:, LANES:2 * LANES]
            for b in range(nsub):
                r0 = b * BLOCK
                blk = slice(r0, r0 + BLOCK)
                before = slice(r0 - BLOCK, r0)
                after8 = slice(r0 + BLOCK, r0 + BLOCK + 8)
                if b == 0:
                    has_prev = n > 0
                    kv_prev, cg_prev, u_prev, tab_p = zkvp_ref[...], cgp_ref[...], up_ref[...], tabp_ref[...]
                else:
                    has_prev = True
                    kv_prev, tab_p = z_ref[before, Z_K:Z_END], tab_ref[before, :]
                    cg_prev = z_ref[r0 - 8:r0, CONV_WIDTH:2 * CONV_WIDTH]
                    u_prev = z_ref[r0 - 8:r0, 2 * CONV_WIDTH:Z_Q]
                if b == nsub - 1:
                    dconv_next = jnp.where(n < steps - 1, dy_next * bgn_ref[...], 0.0)
                else:
                    dconv_next = dy[after8, 0:CONV_WIDTH] * z_ref[after8, 0:CONV_WIDTH]
                bg = z_ref[blk, 0:CONV_WIDTH]
                cg = z_ref[blk, CONV_WIDTH:2 * CONV_WIDTH]
                u = z_ref[blk, 2 * CONV_WIDTH:Z_Q]
                vv, vv1, vv2 = _conv_taps(cg, u, cg_prev, u_prev, has_prev)
                dyc = dy[blk, 0:CONV_WIDTH]
                dbg = dyc * (w0 * vv2 + w1 * vv1 + w2 * vv)
                dconv = dyc * bg
                ext = jnp.concatenate([dconv, dconv_next], axis=0)
                ext_rows = ext.shape[0]
                dvv = (w2 * dconv + w1 * pltpu.roll(ext, ext_rows - 1, 0)[0:BLOCK]
                       + w0 * pltpu.roll(ext, ext_rows - 2, 0)[0:BLOCK])
                dcw_ref[0:1, :] += jnp.sum(dconv * vv2, axis=0, keepdims=True)
                dcw_ref[1:2, :] += jnp.sum(dconv * vv1, axis=0, keepdims=True)
                dcw_ref[2:3, :] += jnp.sum(dconv * vv, axis=0, keepdims=True)

                tab_c = tab_ref[blk, :]
                k_all = jnp.concatenate([kv_prev[:, 0:LANES], z_ref[blk, Z_K:Z_V]], axis=0)
                v_all = jnp.concatenate([kv_prev[:, LANES:2 * LANES], z_ref[blk, Z_V:Z_END]], axis=0)
                kp = _head_pads(k_all)
                vp = _head_pads(v_all)
                chunks = range(N_Q_HEADS // 2)
                q2 = [(z_ref[blk, Z_Q + LANES * c:Z_Q + LANES * (c + 1)] * ATTN_SCALE).astype(BF16) for c in chunks]
                do2 = [dy[blk, CONV_WIDTH + LANES * c:CONV_WIDTH + LANES * (c + 1)].astype(BF16) for c in chunks]
                probs = _attn_probs_t(q2, kp, _window_mask_t(has_prev), sink_ref)
                dq_chunks = []
                dk_nat = jnp.zeros((2 * BLOCK, LANES), F32)
                dv_nat = jnp.zeros((2 * BLOCK, LANES), F32)
                for kv in range(2):
                    q_st = jnp.concatenate([q2[2 * kv], q2[2 * kv + 1]], axis=0)
                    do_st = jnp.concatenate([do2[2 * kv], do2[2 * kv + 1]], axis=0)
                    dq_t = jnp.zeros((LANES, 2 * BLOCK), F32)
                    dk_par, dv_par = [], []
                    for par in range(2):
                        g = 2 * kv + par
                        pr, psink = probs[g]
                        dp = _dot_nt(vp[(kv, par)], do_st)
                        delta = jnp.sum(dp * pr, axis=0, keepdims=True)
                        ds = (pr * (dp - delta)).astype(BF16)
                        dsink = -psink * delta
                        for r in range(2):
                            h = _group_head(g, r)
                            dsk_ref[h:h + 1, :] += jnp.sum(dsink[:, BLOCK * r:BLOCK * (r + 1)])
                        dq_t = dq_t + _dot_tn(kp[(kv, par)], ds)
                        dk_par.append(_dot(ds, q_st))
                        dv_par.append(_dot(pr.astype(BF16), do_st))
                    for r in range(2):
                        dq_chunks.append(_rot_t(dq_t[:, BLOCK * r:BLOCK * (r + 1)].T * ATTN_SCALE, tab_c))
                    dk_nat = dk_nat + _from_pads(dk_par[0], dk_par[1], kv)
                    dv_nat = dv_nat + _from_pads(dv_par[0], dv_par[1], kv)

                done_ref, done = (dz_ref, last) if b == 0 else (held_ref, before)
                done_ref[done, Z_K:Z_V] = _rot_t(dk_open + dk_nat[0:BLOCK], tab_p).astype(BF16)
                done_ref[done, Z_V:Z_END] = (dv_open + dv_nat[0:BLOCK]).astype(BF16)
                dk_open, dv_open = dk_nat[BLOCK:2 * BLOCK], dv_nat[BLOCK:2 * BLOCK]
                held_ref[blk, 0:CONV_WIDTH] = dbg.astype(BF16)
                held_ref[blk, CONV_WIDTH:2 * CONV_WIDTH] = (dvv * u).astype(BF16)
                held_ref[blk, 2 * CONV_WIDTH:Z_Q] = (dvv * cg).astype(BF16)
                for c in range(N_Q_HEADS // 2):
                    held_ref[blk, Z_Q + LANES * c:Z_Q + LANES * (c + 1)] = dq_chunks[c].astype(BF16)
            kv_ref[:, 0:LANES] = dk_open
            kv_ref[:, LANES:2 * LANES] = dv_open
            project_emitted()

        @pl.when(n == steps)
        def _():
            emit_held()
            dz_ref[last, Z_K:Z_V] = _rot_t(kv_ref[:, 0:LANES], tab_ref[last, :]).astype(BF16)
            dz_ref[last, Z_V:Z_END] = kv_ref[:, LANES:2 * LANES].astype(BF16)
            project_emitted()

    emitted = lambda n: (jnp.maximum(n - 1, 0), 0)
    return _pcall(
        body, name=name, grid=(steps + 1,),
        in_specs=[pl.BlockSpec((rows, Z_END), lambda n: (cur(n), 0)),
                  pl.BlockSpec((BLOCK, 2 * LANES), lambda n: (prev_block(n), Z_K // (2 * LANES))),
                  pl.BlockSpec((8, CONV_WIDTH), lambda n: (prev_rows8(n), 1)),
                  pl.BlockSpec((8, CONV_WIDTH), lambda n: (prev_rows8(n), 2)),
                  pl.BlockSpec((8, CONV_WIDTH), lambda n: (next_rows8(n), 0)),
                  pl.BlockSpec((rows, dx.shape[1]), lambda n: (cur(n), 0)),
                  pl.BlockSpec((8, dx.shape[1]), lambda n: (next_rows8(n), 0)),
                  pl.BlockSpec(wout.shape, lambda n: (0, 0)),
                  pl.BlockSpec((rows, 3 * LANES), lambda n: (cur(n), 0)),
                  pl.BlockSpec((BLOCK, 3 * LANES), lambda n: (prev_block(n), 0)),
                  pl.BlockSpec((3, CONV_WIDTH), lambda n: (0, 0)),
                  pl.BlockSpec(memory_space=pltpu.SMEM),
                  pl.BlockSpec(win.shape, lambda n: (0, 0)), pl.BlockSpec((rows, D), emitted),
                  pl.BlockSpec((1, D), lambda n: (0, 0)), pl.BlockSpec((rows, D), emitted)],
        out_specs=[pl.BlockSpec((rows, Z_END), emitted),
                   pl.BlockSpec((8, CONV_WIDTH), lambda n: (0, 0)), pl.BlockSpec((8, LANES), lambda n: (0, 0)),
                   pl.BlockSpec((rows, D), lambda n: (cur(n), 0)),
                   pl.BlockSpec((rows, D), emitted), pl.BlockSpec((1, D), lambda n: (0, 0))],
        out_shape=[jax.ShapeDtypeStruct((T, Z_END), BF16), jax.ShapeDtypeStruct((8, CONV_WIDTH), F32),
                   jax.ShapeDtypeStruct((8, LANES), F32), jax.ShapeDtypeStruct(dx.shape, BF16),
                   jax.ShapeDtypeStruct((T, D), F32), jax.ShapeDtypeStruct((1, D), F32)],
        scratch_shapes=[pltpu.VMEM((rows, Z_END), BF16), pltpu.VMEM((BLOCK, 2 * LANES), F32)],
        compiler_params=_params(1, LARGE_VMEM_LIMIT),
    )(z, z, z, z, z, dx, dx, wout, tab, tab, conv_w, sinks, win, x, g, dx)


def _local_sums(pair, chip, place, name):
    arrays, in_specs, out_specs, out_shape = [], [], [], []
    if pair is not None:
        g, sib = pair
        blk = (1, *sib.shape[1:])
        arrays += [g, sib]
        in_specs += [pl.BlockSpec(blk, lambda q, p: (q, p[1], 0)), pl.BlockSpec(blk, lambda q, p: (q, 0, 0))]
        out_specs.append(pl.BlockSpec(blk, lambda q, p: (q, 0, 0)))
        out_shape.append(jax.ShapeDtypeStruct(sib.shape, BF16))
    if chip is not None:
        g2, sib2, recv2 = chip
        blk = (1, *sib2.shape[1:])
        arrays += [g2, sib2, recv2]
        in_specs += [pl.BlockSpec(blk, lambda q, p: (p[0], p[1], 0)), pl.BlockSpec(blk, lambda q, p: (p[0], 0, 0)),
                     pl.BlockSpec(recv2.shape, lambda q, p: (0, 0, 0))]
        out_specs.append(pl.BlockSpec(sib2.shape[1:], lambda q, p: (p[1], 0)))
        out_shape.append(jax.ShapeDtypeStruct(g2.shape[1:], F32))

    def body(place_ref, *refs):
        refs = list(refs)
        ins, outs = refs[:len(arrays)], refs[len(arrays):]
        if pair is not None:
            g_ref, sib_ref = ins[:2]
            outs[0][...] = (g_ref[...] + sib_ref[...].astype(F32)).astype(BF16)
        if chip is not None:
            g_ref, sib_ref, recv_ref = ins[-3:]

            @pl.when(pl.program_id(0) == 0)
            def _():
                total = g_ref[0] + sib_ref[0].astype(F32)
                for j in range(3):
                    total = total + recv_ref[j].astype(F32)
                outs[-1][...] = total

    return _pcall(
        body, name=name,
        grid_spec=pltpu.PrefetchScalarGridSpec(num_scalar_prefetch=1, grid=(N_CHIPS,),
                                               in_specs=in_specs, out_specs=out_specs),
        out_shape=out_shape, compiler_params=_params(1),
    )(place, *arrays)


def _adamw_math(w, g, m, v):
    m = ADAM_B1 * m + (1.0 - ADAM_B1) * g
    v = ADAM_B2 * v + (1.0 - ADAM_B2) * (g * g)
    m_hat = m / (1.0 - ADAM_B1 ** ADAM_STEP)
    v_hat = v / (1.0 - ADAM_B2 ** ADAM_STEP)
    delta = -ADAM_LR * (m_hat / (jnp.sqrt(v_hat) + ADAM_EPS) + ADAM_WD * w)
    return delta, m, v


def _adamw(ws, gs, ms, vs, row_blocks, name):
    n = len(ws)

    def body(*refs):
        w, g, m, v = refs[:n], refs[n:2 * n], refs[2 * n:3 * n], refs[3 * n:4 * n]
        d, mo, vo, go = refs[4 * n:5 * n], refs[5 * n:6 * n], refs[6 * n:7 * n], refs[7 * n:]
        for t in range(n):
            gv = g[t][...]
            delta, m_new, v_new = _adamw_math(w[t][...], gv, m[t][...], v[t][...])
            d[t][...] = delta
            mo[t][...] = m_new
            vo[t][...] = v_new
            go[t][...] = gv

    specs = [pl.BlockSpec((a.shape[0] // row_blocks, a.shape[1]), lambda i: (i, 0)) for a in ws]
    shapes = [jax.ShapeDtypeStruct(a.shape, F32) for a in ws]
    return _pcall(
        body, name=name, grid=(row_blocks,), in_specs=specs * 4, out_specs=specs * 4, out_shape=shapes * 4,
        compiler_params=_params(1),
    )(*ws, *gs, *ms, *vs)


def kernel(x, ffn1_norm, ffn1_w_gate, ffn1_w_up, ffn1_w_down, mix_norm, w_in, conv_w, attn_sinks, w_out, ffn2_norm, ffn2_w_gate, ffn2_w_up, ffn2_w_down, final_norm, loss_target, m_ffn1_norm, m_ffn1_w_gate, m_ffn1_w_up, m_ffn1_w_down, m_mix_norm, m_w_in, m_conv_w, m_attn_sinks, m_w_out, m_ffn2_norm, m_ffn2_w_gate, m_ffn2_w_up, m_ffn2_w_down, m_final_norm, v_ffn1_norm, v_ffn1_w_gate, v_ffn1_w_up, v_ffn1_w_down, v_mix_norm, v_w_in, v_conv_w, v_attn_sinks, v_w_out, v_ffn2_norm, v_ffn2_w_gate, v_ffn2_w_up, v_ffn2_w_down, v_final_norm):
    T, D = x.shape[1], x.shape[2]
    chip = (2 * lax.axis_index("x") + lax.axis_index("y")).astype(jnp.int32)
    core = lax.axis_index("c").astype(jnp.int32)
    place = jnp.stack([chip, core])
    x0 = x[0]
    target = loss_target[0]
    gf = final_norm.reshape(1, D)

    tr = lambda w: jnp.swapaxes(w[0], 0, 1)
    big = [tr(ffn1_w_gate), tr(ffn1_w_up), ffn1_w_down[0], tr(w_in), w_out[0], tr(ffn2_w_gate), tr(ffn2_w_up), ffn2_w_down[0]]
    transposed = [True, True, False, True, False, True, True, False]
    own_b = [w.astype(BF16) for w in big]

    whole = lambda gathered: gathered.reshape(-1, D)

    tab, got1 = _rope_tables(T, "rope_gather_ffn1", _gather_plan(own_b[0:3]))
    wg1, wu1, wd1 = (whole(g) for g in got1)

    res = _ffn_fwd(x0, ffn1_norm, wg1, wu1, wd1, "ffn1_fwd", _gather_plan(own_b[3:8], [conv_w[0]]))
    x1, h1, gate1, up1, act1 = res[:5]
    win, wout, wg2, wu2, wd2 = (whole(g) for g in res[5:10])
    convw4 = lax.dynamic_update_slice(res[10], conv_w, (chip, 0, 0))
    convw = jnp.transpose(convw4, (1, 0, 2)).reshape(3, -1)
    z, hm = _norm_matmul(x1, mix_norm, win, tab, "mix_in_fwd")
    ymix = _mix_core_fwd(z, convw, attn_sinks, "mix_core_fwd")
    dx3, h2, gate2, up2, act2, dgf, loss_part, x2 = _ffn_fwd(x1, ffn2_norm, wg2, wu2, wd2, "ffn2_fwd",
                                                             head=(gf, target), pre=(ymix, wout))

    dx2, dyb2, dgate2, dup2, dg2 = _ffn_bwd(dx3, x2, ffn2_norm, gate2, up2, wg2, wu2, wd2, "ffn2_bwd")
    dz, dcw, dsk, dx2b, dx1, dgm = _mix_core_bwd(z, dx2, wout, win, x1, mix_norm, tab, convw, attn_sinks, "mix_bwd")
    dx0, dyb1, dgate1, dup1, dg1 = _ffn_bwd(dx1, x0, ffn1_norm, gate1, up1, wg1, wu1, wd1, "ffn1_bwd")

    pad = lambda a: jnp.pad(a, ((0, 0), (0, LANES - a.shape[1])))
    vec = jnp.concatenate([dg1, dgm, dg2, dgf, dcw[0:3].reshape(1, -1), pad(dsk[:, 0].reshape(1, -1)),
                           pad(loss_part[:, 0:1])], axis=1)

    jobs = [("ffn2_dwg", dgate2, h2, 5), ("ffn2_dwu", dup2, h2, 6), ("ffn2_dwd", act2, dyb2, 7),
            ("ffn1_dwg", dgate1, h1, 0), ("ffn1_dwu", dup1, h1, 1), ("ffn1_dwd", act1, dyb1, 2),
            ("mix_dwin", dz, hm, 3), ("mix_dwout", ymix, dx2b, 4)]
    n_jobs = len(jobs)
    grad, grad_b, from_sib, pair_b, from_chips, half, g_big = ({} for _ in range(7))

    def stage_plans(t):
        plans, takers = [], []
        if 0 <= t - 1 < n_jobs:
            plans.append(_sibling_plan([grad_b[t - 1]]))
            takers.append((from_sib, t - 1))
        if 0 <= t - 2 < n_jobs:
            plans.append(_scatter_plan([pair_b[t - 2]]))
            takers.append((from_chips, t - 2))
        if 0 <= t - 3 < n_jobs:
            plans.append(_join_plan([half[t - 3]]))
            takers.append((g_big, jobs[t - 3][3]))
        return plans, takers

    def after_stage(t, landed, takers):
        for (store, key), arr in zip(takers, landed):
            store[key] = arr
        pair = (grad[t - 1], from_sib[t - 1]) if 0 <= t - 1 < n_jobs else None
        chip = (grad[t - 2], from_sib[t - 2], from_chips[t - 2]) if 0 <= t - 2 < n_jobs else None
        if pair or chip:
            sums = list(_local_sums(pair, chip, place, f"local_sums_{t}"))
            if pair:
                pair_b[t - 1] = sums.pop(0)
            if chip:
                half[t - 2] = sums.pop(0)

    for t, (name_, a, b, _) in enumerate(jobs):
        plans, takers = stage_plans(t)
        if t == 0:
            plans.append(_all_gather_plan(jnp.pad(vec, ((0, 7), (0, 0)))))
        res = _matmul_tn(a, b, DW_ROW_SPLIT, name_, _merge_plans(plans))
        grad[t], grad_b[t] = (r.reshape(N_CHIPS, -1, D) for r in res[:2])
        landed = list(res[2:])
        if t == 0:
            vec_blocks = landed.pop()
        after_stage(t, landed, takers)

    ws = big
    ms = [tr(m_ffn1_w_gate), tr(m_ffn1_w_up), m_ffn1_w_down[0], tr(m_w_in), m_w_out[0], tr(m_ffn2_w_gate), tr(m_ffn2_w_up), m_ffn2_w_down[0]]
    vs = [tr(v_ffn1_w_gate), tr(v_ffn1_w_up), v_ffn1_w_down[0], tr(v_w_in), v_w_out[0], tr(v_ffn2_w_gate), tr(v_ffn2_w_up), v_ffn2_w_down[0]]
    for t in range(n_jobs, n_jobs + 3):
        plans, takers = stage_plans(t)
        after_stage(t, _run_comm(_merge_plans(plans), f"grads_tail_{t - n_jobs}"), takers)
    upd = {}
    for name_, idx in (("adamw_a", [0, 1, 2, 4]), ("adamw_b", [3, 5, 6, 7])):
        k = len(idx)
        res = _adamw([ws[i] for i in idx], [g_big[i] for i in idx], [ms[i] for i in idx], [vs[i] for i in idx], ADAMW_ROW_BLOCKS, name_)
        for j, i in enumerate(idx):
            upd[i] = (res[j], res[k + j], res[2 * k + j])
            g_big[i] = res[3 * k + j]

    total = _sum_devices(vec_blocks, "small_sum")[0:1]
    g_n1, g_nm, g_n2, g_nf = (total[:, k * D:(k + 1) * D] for k in range(4))
    cw_full = total[:, 4 * D:4 * D + 3 * CONV_WIDTH].reshape(3, CONV_WIDTH)
    cq = CONV_WIDTH // N_CHIPS
    g_cw = lax.dynamic_slice(cw_full, (0, chip * cq), (3, cq))
    off = 4 * D + 3 * CONV_WIDTH
    g_sk = total[:, off:off + N_Q_HEADS]
    loss = total[0, off + LANES]

    sw = [ffn1_norm, mix_norm, conv_w[0], attn_sinks, ffn2_norm, gf]
    sg = [g_n1, g_nm, g_cw, g_sk, g_n2, g_nf]
    sm = [m_ffn1_norm, m_mix_norm, m_conv_w[0], m_attn_sinks, m_ffn2_norm, m_final_norm.reshape(1, D)]
    sv = [v_ffn1_norm, v_mix_norm, v_conv_w[0], v_attn_sinks, v_ffn2_norm, v_final_norm.reshape(1, D)]
    sres = _adamw(sw, sg, sm, sv, 1, "adamw_small")
    supd = [(sres[j], sres[6 + j], sres[12 + j]) for j in range(6)]

    order = [("s", 0), ("b", 0), ("b", 1), ("b", 2), ("s", 1), ("b", 3), ("s", 2), ("s", 3), ("b", 4),
             ("s", 4), ("b", 5), ("b", 6), ("b", 7), ("s", 5)]

    def leaf(kind, i, which):
        if kind == "b":
            a = g_big[i] if which == 0 else upd[i][which - 1]
            return (jnp.swapaxes(a, 0, 1) if transposed[i] else a)[None]
        a = sg[i] if which == 0 else supd[i][which - 1]
        if i == 2:
            return a[None]
        if i == 5:
            return a.reshape(D)
        return a

    outs = [loss, dx0[None]]
    for which in range(4):
        outs += [leaf(kind, i, which) for kind, i in order]
    return tuple(outs)
```
